```python
import jax, jax.numpy as jnp
from jax import lax
import numpy as np

D_MODEL = 1024
BATCH = 32
SEQ = 256
DEPTH = 1
DEC_BATCH = 4
DEC_SEQ = 2048
PAST_LEN = 512

GRID_W = 64
N_HEADS = 8
QK_NOPE = 64
QK_ROPE = 32
V_HEAD = 64
Q_LORA = 256
KV_LORA = 128
ATTN_W = N_HEADS * V_HEAD
FNET_GROUPS = 8
FNET_CH = 64
FNET_W = FNET_GROUPS * FNET_CH
MIX_W = ATTN_W + FNET_W
IN_W = Q_LORA + KV_LORA + QK_ROPE + FNET_W
N_EXPERTS = 16
CAP_FACTOR = 2
D_EXPERT = 512
ROPE_BASE = 10000.0
EPS = 1e-6
Q_BLOCK = 128

kernel_name = 'hybrid_mla_fnet_ecmoe_diffusion_step'


def rmsnorm(x, g):
    xf = x.astype(jnp.float32)
    out = xf * lax.rsqrt(jnp.mean(xf * xf, axis=-1, keepdims=True) + EPS)
    return (out * g.astype(jnp.float32)).astype(x.dtype)


def adaln(cvec, w_mod, b_mod):
    m = jax.nn.silu(cvec) @ w_mod + b_mod
    return [t[:, None, :] for t in jnp.split(m, 6, axis=-1)]


def modulate(h, shift, scale):
    return h * (1.0 + scale) + shift


def rope_angles(n_tokens):
    n_rows = n_tokens // GRID_W
    rows = jnp.repeat(jnp.arange(n_rows, dtype=jnp.float32), GRID_W)
    cols = jnp.tile(jnp.arange(GRID_W, dtype=jnp.float32), n_rows)
    n_freq = QK_ROPE // 4
    inv_freq = ROPE_BASE ** (-jnp.arange(n_freq, dtype=jnp.float32) / n_freq)
    return jnp.concatenate([rows[:, None] * inv_freq, cols[:, None] * inv_freq], axis=-1)


def apply_rope(x, ang):
    half = QK_ROPE // 2
    cos = jnp.cos(ang)[:, None, :]
    sin = jnp.sin(ang)[:, None, :]
    xf = x.astype(jnp.float32)
    x1, x2 = xf[..., :half], xf[..., half:]
    out = jnp.concatenate([x1 * cos - x2 * sin, x1 * sin + x2 * cos], axis=-1)
    return out.astype(x.dtype)


def mixing_inputs(h, w_in, q_norm_g, w_uq, kv_norm_g):
    B, T, _ = h.shape
    proj = h @ w_in
    q_lat = proj[..., :Q_LORA]
    kv_lat = proj[..., Q_LORA:Q_LORA + KV_LORA]
    k_pe = proj[..., Q_LORA + KV_LORA:Q_LORA + KV_LORA + QK_ROPE]
    f_in = proj[..., Q_LORA + KV_LORA + QK_ROPE:]
    q = (rmsnorm(q_lat, q_norm_g) @ w_uq).reshape(B, T, N_HEADS, QK_NOPE + QK_ROPE)
    ckv = rmsnorm(kv_lat, kv_norm_g)
    return q[..., :QK_NOPE], q[..., QK_NOPE:], ckv, k_pe, f_in


def decompress(ckv, w_ukv):
    B, T, _ = ckv.shape
    kv = (ckv @ w_ukv).reshape(B, T, N_HEADS, QK_NOPE + V_HEAD)
    return kv[..., :QK_NOPE], kv[..., QK_NOPE:]


def attend(q_nope, q_pe, k_nope, k_pe, v):
    B, T = q_nope.shape[:2]
    nb = T // Q_BLOCK
    qn = q_nope.reshape(B, nb, Q_BLOCK, N_HEADS, QK_NOPE).transpose(1, 0, 2, 3, 4)
    qp = q_pe.reshape(B, nb, Q_BLOCK, N_HEADS, QK_ROPE).transpose(1, 0, 2, 3, 4)
    scale = (QK_NOPE + QK_ROPE) ** -0.5

    def block(args):
        qn_b, qp_b = args
        s = jnp.einsum('bqhd,bkhd->bhqk', qn_b, k_nope) + jnp.einsum('bqhr,bkr->bhqk', qp_b, k_pe)
        p = jax.nn.softmax(s.astype(jnp.float32) * scale, axis=-1).astype(v.dtype)
        return jnp.einsum('bhqk,bkhd->bqhd', p, v)

    out = lax.map(block, (qn, qp))
    return out.transpose(1, 0, 2, 3, 4).reshape(B, T, ATTN_W)


def fourier_mix(f_in, w_fmix):
    B, T, _ = f_in.shape
    f = f_in.reshape(B, T, FNET_GROUPS, FNET_CH).astype(jnp.float32)
    fr = jnp.fft.fft2(f, axes=(1, 3), norm='ortho').real.astype(f_in.dtype)
    return jnp.einsum('btgc,gcd->btgd', fr, w_fmix).reshape(B, T, FNET_W)


def expert_choice(h, w_router, w_e_gate, w_e_up, w_e_down):
    B, n, _ = h.shape
    cap = CAP_FACTOR * n // N_EXPERTS
    aff = jax.nn.softmax((h @ w_router).astype(jnp.float32), axis=-1)
    g, idx = lax.top_k(aff.transpose(0, 2, 1), cap)
    b_idx = jnp.arange(B)[:, None, None]
    xs = h[b_idx, idx]
    hid = jax.nn.silu(jnp.einsum('becd,edf->becf', xs, w_e_gate)) * jnp.einsum('becd,edf->becf', xs, w_e_up)
    ys = jnp.einsum('becf,efd->becd', hid, w_e_down) * g[..., None].astype(h.dtype)
    return jnp.zeros_like(h).at[b_idx, idx].add(ys)


def context_layer(x, mods, lp):
    (norm1_g, w_in, q_norm_g, w_uq, kv_norm_g, w_ukv, w_fmix, w_out,
     norm2_g, w_router, w_e_gate, w_e_up, w_e_down) = lp
    shift1, scale1, gate1, shift2, scale2, gate2 = mods
    h = modulate(rmsnorm(x, norm1_g), shift1, scale1)
    q_nope, q_pe, ckv, k_pe, f_in = mixing_inputs(h, w_in, q_norm_g, w_uq, kv_norm_g)
    k_nope, v = decompress(ckv, w_ukv)
    attn = attend(q_nope, q_pe, k_nope, k_pe, v)
    fmix = fourier_mix(f_in, w_fmix)
    x = x + gate1 * (jnp.concatenate([attn, fmix], axis=-1) @ w_out)
    h2 = modulate(rmsnorm(x, norm2_g), shift2, scale2)
    x = x + gate2 * expert_choice(h2, w_router, w_e_gate, w_e_up, w_e_down)
    return x, ckv, k_pe


def latent_layer(x, mods, ckv_ctx, kpe_ctx, ang, lp):
    (norm1_g, w_in, q_norm_g, w_uq, kv_norm_g, w_ukv, w_fmix, w_out,
     norm2_g, w_router, w_e_gate, w_e_up, w_e_down) = lp
    shift1, scale1, gate1, shift2, scale2, gate2 = mods
    h = modulate(rmsnorm(x, norm1_g), shift1, scale1)
    q_nope, q_pe, ckv, k_pe, f_in = mixing_inputs(h, w_in, q_norm_g, w_uq, kv_norm_g)
    q_pe = apply_rope(q_pe, ang)
    k_pe = apply_rope(k_pe[:, :, None, :], ang)[:, :, 0, :]
    k_nope_l, v_l = decompress(ckv, w_ukv)
    k_nope_c, v_c = decompress(ckv_ctx, w_ukv)
    k_nope = jnp.concatenate([k_nope_c, k_nope_l], axis=1)
    k_pe_all = jnp.concatenate([kpe_ctx, k_pe], axis=1)
    v = jnp.concatenate([v_c, v_l], axis=1)
    attn = attend(q_nope, q_pe, k_nope, k_pe_all, v)
    fmix = fourier_mix(f_in, w_fmix)
    x = x + gate1 * (jnp.concatenate([attn, fmix], axis=-1) @ w_out)
    h2 = modulate(rmsnorm(x, norm2_g), shift2, scale2)
    x = x + gate2 * expert_choice(h2, w_router, w_e_gate, w_e_up, w_e_down)
    return x


def setup_inputs(seed: int = 0) -> dict:
    key = jax.random.key(seed)
    ks = jax.random.split(key, 32)
    f32 = jnp.float32

    def nrm(k, shape, scale):
        return jax.random.normal(k, shape, f32) * scale

    def gain(k, shape):
        return 1.0 + 0.02 * jax.random.normal(k, shape, f32)

    L = DEPTH
    return {
        'x_prompt': nrm(ks[0], (BATCH, SEQ, D_MODEL), 1.0),
        'x_sample': nrm(ks[1], (DEC_BATCH, DEC_SEQ, D_MODEL), 1.0),
        'cache_ckv': nrm(ks[2], (DEC_BATCH, DEPTH, PAST_LEN, KV_LORA), 1.0),
        'cache_kpe': nrm(ks[3], (DEC_BATCH, DEPTH, PAST_LEN, QK_ROPE), 1.0),
        'c': nrm(ks[4], (DEC_BATCH, D_MODEL), 1.0),
        'c_ctx': nrm(ks[5], (D_MODEL,), 1.0),
        'w_mod': nrm(ks[6], (L, D_MODEL, 6 * D_MODEL), 0.5 * D_MODEL ** -0.5),
        'b_mod': nrm(ks[7], (L, 6 * D_MODEL), 0.02),
        'norm1_g': gain(ks[8], (L, D_MODEL)),
        'w_in': nrm(ks[9], (L, D_MODEL, IN_W), D_MODEL ** -0.5),
        'q_norm_g': gain(ks[10], (L, Q_LORA)),
        'w_uq': nrm(ks[11], (L, Q_LORA, N_HEADS * (QK_NOPE + QK_ROPE)), Q_LORA ** -0.5),
        'kv_norm_g': gain(ks[12], (L, KV_LORA)),
        'w_ukv': nrm(ks[13], (L, KV_LORA, N_HEADS * (QK_NOPE + V_HEAD)), KV_LORA ** -0.5),
        'w_fmix': nrm(ks[14], (L, FNET_GROUPS, FNET_CH, FNET_CH), FNET_CH ** -0.5),
        'w_out': nrm(ks[15], (L, MIX_W, D_MODEL), MIX_W ** -0.5),
        'norm2_g': gain(ks[16], (L, D_MODEL)),
        'w_router': nrm(ks[17], (L, D_MODEL, N_EXPERTS), D_MODEL ** -0.5),
        'w_e_gate': nrm(ks[18], (L, N_EXPERTS, D_MODEL, D_EXPERT), D_MODEL ** -0.5),
        'w_e_up': nrm(ks[19], (L, N_EXPERTS, D_MODEL, D_EXPERT), D_MODEL ** -0.5),
        'w_e_down': nrm(ks[20], (L, N_EXPERTS, D_EXPERT, D_MODEL), D_EXPERT ** -0.5),
        'final_g': gain(ks[21], (D_MODEL,)),
    }


def reference(x_prompt, x_sample, cache_ckv, cache_kpe, c, c_ctx, w_mod, b_mod, norm1_g, w_in,
              q_norm_g, w_uq, kv_norm_g, w_ukv, w_fmix, w_out, norm2_g, w_router, w_e_gate,
              w_e_up, w_e_down, final_g):
    ang = rope_angles(x_sample.shape[1])
    xp = x_prompt
    xs = x_sample
    new_ckv = []
    new_kpe = []
    for l in range(DEPTH):
        lp = (norm1_g[l], w_in[l], q_norm_g[l], w_uq[l], kv_norm_g[l], w_ukv[l], w_fmix[l], w_out[l],
              norm2_g[l], w_router[l], w_e_gate[l], w_e_up[l], w_e_down[l])
        mods_ctx = adaln(c_ctx[None, :], w_mod[l], b_mod[l])
        mods_lat = adaln(c, w_mod[l], b_mod[l])
        xp, ckv, kpe = context_layer(xp, mods_ctx, lp)
        new_ckv.append(ckv)
        new_kpe.append(kpe)
        xs = latent_layer(xs, mods_lat, cache_ckv[:, l], cache_kpe[:, l], ang, lp)
    y_prompt = rmsnorm(xp, final_g)
    y_sample = rmsnorm(xs, final_g)
    state_ckv = jnp.stack(new_ckv, axis=1)
    state_kpe = jnp.stack(new_kpe, axis=1)
    return (y_prompt, y_sample, state_ckv, state_kpe)
```

```python
import functools

import jax
import jax.numpy as jnp
import numpy as np
from jax import lax
from jax.experimental import pallas as pl
from jax.experimental.pallas import tpu as pltpu

F32 = jnp.float32
BF16 = jnp.bfloat16

D_MODEL = 1024
N_HEADS = 8
QK_NOPE = 64
QK_ROPE = 32
V_HEAD = 64
Q_LORA = 256
KV_LORA = 128
FNET_GROUPS = 8
FNET_CH = 64
FNET_W = FNET_GROUPS * FNET_CH
N_EXPERTS = 16
CAP_FACTOR = 2
D_EXPERT = 512
GRID_W = 64
ROPE_BASE = 10000.0
EPS = 1e-6

LANES = 128
HEAD_PAD = LANES
QK_W = N_HEADS * HEAD_PAD
V_W = N_HEADS * V_HEAD
TOK_TILE = 256
VMEM_LIMIT = 48 * 1024 * 1024

_NT = (((1,), (1,)), ((), ()))


def _cparams(sem):
    return pltpu.CompilerParams(dimension_semantics=sem, vmem_limit_bytes=VMEM_LIMIT)


def _rms(x, g):
    return x * lax.rsqrt(jnp.mean(x * x, axis=-1, keepdims=True) + EPS) * g


def _dot(a, b):
    return jnp.dot(a, b, preferred_element_type=F32)


def _mods_kernel(c_ref, w_ref, b_ref, o_ref):
    c = c_ref[...]
    s = c * jax.nn.sigmoid(c)
    o_ref[...] = _dot(s.astype(BF16), w_ref[...].astype(BF16)) + b_ref[...]


def _mods(c8, w_mod, b_mod):
    n = w_mod.shape[1]
    tn = 1536
    return pl.pallas_call(
        _mods_kernel,
        grid=(n // tn,),
        in_specs=[pl.BlockSpec((8, D_MODEL), lambda j: (0, 0)),
                  pl.BlockSpec((D_MODEL, tn), lambda j: (0, j)),
                  pl.BlockSpec((1, tn), lambda j: (0, j))],
        out_specs=pl.BlockSpec((8, tn), lambda j: (0, j)),
        out_shape=jax.ShapeDtypeStruct((8, n), F32),
        compiler_params=_cparams(("arbitrary",)),
        name="mods",
    )(c8, w_mod, b_mod)


def _fold_kernel(cc_ref, sc_ref, w_ref, cw_ref, sw_ref):
    for g in range(FNET_GROUPS):
        w = w_ref[g]
        cw_ref[g] = jnp.dot(cc_ref[...], w, preferred_element_type=F32, precision=lax.Precision.HIGHEST)
        sw_ref[g] = jnp.dot(sc_ref[...], w, preferred_element_type=F32, precision=lax.Precision.HIGHEST)


def _fold(cc, sc, w_fmix):
    shp = jax.ShapeDtypeStruct((FNET_GROUPS, FNET_CH, FNET_CH), F32)
    return pl.pallas_call(_fold_kernel, out_shape=(shp, shp), name="fold")(cc, sc, w_fmix)


def _premix_kernel(x_ref, m_ref, g1_ref, win_ref, qg_ref, wuq_ref, kvg_ref, wk_ref, wv_ref, wcs_ref,
                   tq_ref, tk_ref, *outs, rope, emit_state):
    q_ref, k_ref, v_ref, zc_ref, zs_ref = outs[:5]
    shift1 = m_ref[0, 0:1, :]
    scale1 = m_ref[0, 1:2, :]
    h = _rms(x_ref[...], g1_ref[...]) * (1.0 + scale1) + shift1
    proj = _dot(h.astype(BF16), win_ref[...])
    qn = _rms(proj[:, 0:Q_LORA], qg_ref[...]).astype(BF16)
    qq = _dot(qn, wuq_ref[...])
    cosq = tq_ref[:, 0:LANES]
    sinq = tq_ref[:, LANES:2 * LANES]
    for hd in range(N_HEADS):
        lo = hd * HEAD_PAD
        qh = qq[:, lo:lo + HEAD_PAD] * cosq
        if rope:
            qh = qh + qq[:, QK_W + lo:QK_W + lo + HEAD_PAD] * sinq
        q_ref[:, lo:lo + HEAD_PAD] = qh.astype(BF16)
    ckv = _rms(proj[:, Q_LORA:Q_LORA + KV_LORA], kvg_ref[...])
    kpe2 = proj[:, Q_LORA + KV_LORA:Q_LORA + KV_LORA + LANES] * tk_ref[...]
    xk = jnp.concatenate([ckv, kpe2], axis=1).astype(BF16)
    k_ref[...] = _dot(xk, wk_ref[...]).astype(BF16)
    v_ref[...] = _dot(xk[:, 0:KV_LORA], wv_ref[...]).astype(BF16)
    z = _dot(proj[:, 512:1024].astype(BF16), wcs_ref[...])
    zc_ref[...] = z[:, 0:FNET_W].astype(BF16)
    zs_ref[...] = z[:, FNET_W:2 * FNET_W].astype(BF16)
    if emit_state:
        outs[5][...] = ckv
        outs[6][...] = proj[:, Q_LORA + KV_LORA:Q_LORA + KV_LORA + QK_ROPE]


def _premix(x, mods6, g1, win, qg, wuq, kvg, wk, wv, wcs, tq, tk, *, mod_row, tab_row, rope, emit_state):
    n = x.shape[0]
    tm = TOK_TILE
    full = lambda a: pl.BlockSpec(a.shape, lambda i: (0,) * a.ndim)
    out_shape = [jax.ShapeDtypeStruct((n, QK_W), BF16), jax.ShapeDtypeStruct((n, QK_W), BF16),
                 jax.ShapeDtypeStruct((n, V_W), BF16), jax.ShapeDtypeStruct((n, FNET_W), BF16),
                 jax.ShapeDtypeStruct((n, FNET_W), BF16)]
    out_specs = [pl.BlockSpec((tm, QK_W), lambda i: (i, 0)), pl.BlockSpec((tm, QK_W), lambda i: (i, 0)),
                 pl.BlockSpec((tm, V_W), lambda i: (i, 0)), pl.BlockSpec((tm, FNET_W), lambda i: (i, 0)),
                 pl.BlockSpec((tm, FNET_W), lambda i: (i, 0))]
    if emit_state:
        out_shape += [jax.ShapeDtypeStruct((n, KV_LORA), F32), jax.ShapeDtypeStruct((n, QK_ROPE), F32)]
        out_specs += [pl.BlockSpec((tm, KV_LORA), lambda i: (i, 0)), pl.BlockSpec((tm, QK_ROPE), lambda i: (i, 0))]
    return pl.pallas_call(
        functools.partial(_premix_kernel, rope=rope, emit_state=emit_state),
        grid=(n // tm,),
        in_specs=[pl.BlockSpec((tm, D_MODEL), lambda i: (i, 0)),
                  pl.BlockSpec((1, 6, D_MODEL), lambda i: (mod_row(i), 0, 0)),
                  full(g1), full(win), full(qg), full(wuq), full(kvg), full(wk), full(wv), full(wcs),
                  pl.BlockSpec((tm, 2 * LANES), lambda i: (tab_row(i), 0)),
                  pl.BlockSpec((tm, LANES), lambda i: (tab_row(i), 0))],
        out_specs=out_specs,
        out_shape=out_shape,
        compiler_params=_cparams(("parallel",)),
        name="premix_lat" if rope else "premix_ctx",
    )(x, mods6, g1, win, qg, wuq, kvg, wk, wv, wcs, tq, tk)


def _cachekv_kernel(xk_ref, wk_ref, wv_ref, k_ref, v_ref):
    xk = xk_ref[...]
    k_ref[...] = _dot(xk, wk_ref[...]).astype(BF16)
    v_ref[...] = _dot(xk[:, 0:KV_LORA], wv_ref[...]).astype(BF16)


def _cachekv(xk, wk, wv):
    n = xk.shape[0]
    tm = 512
    full = lambda a: pl.BlockSpec(a.shape, lambda i: (0,) * a.ndim)
    return pl.pallas_call(
        _cachekv_kernel,
        grid=(n // tm,),
        in_specs=[pl.BlockSpec((tm, 2 * LANES), lambda i: (i, 0)), full(wk), full(wv)],
        out_specs=[pl.BlockSpec((tm, QK_W), lambda i: (i, 0)), pl.BlockSpec((tm, V_W), lambda i: (i, 0))],
        out_shape=[jax.ShapeDtypeStruct((n, QK_W), BF16), jax.ShapeDtypeStruct((n, V_W), BF16)],
        compiler_params=_cparams(("parallel",)),
        name="cachekv",
    )(xk, wk, wv)


def _attn_kernel(q_ref, *refs, n_kv):
    k_refs = refs[:n_kv]
    v_refs = refs[n_kv:2 * n_kv]
    o_ref = refs[2 * n_kv]
    outs = []
    for hh in range(2):
        lo = hh * HEAD_PAD
        q = q_ref[:, lo:lo + HEAD_PAD]
        ss = [lax.dot_general(q, k[:, lo:lo + HEAD_PAD], _NT, preferred_element_type=F32) for k in k_refs]
        m = functools.reduce(jnp.maximum, [jnp.max(s, axis=1, keepdims=True) for s in ss])
        ps = [jnp.exp(s - m) for s in ss]
        l = functools.reduce(jnp.add, [jnp.sum(p, axis=1, keepdims=True) for p in ps])
        o = functools.reduce(jnp.add, [_dot(p.astype(BF16), v[...]) for p, v in zip(ps, v_refs)])
        outs.append(o * (1.0 / l))
    lane = lax.broadcasted_iota(jnp.int32, outs[0].shape, 1)
    o_ref[...] = jnp.where(lane < V_HEAD, outs[0], outs[1]).astype(BF16)


def _attention(q, ks, vs, *, n_req, t_q, kv_lens, tq):
    n_kv = len(ks)
    nq = t_q // tq
    in_specs = [pl.BlockSpec((tq, 2 * HEAD_PAD), lambda b, p, i: (b * nq + i, p))]
    in_specs += [pl.BlockSpec((kl, 2 * HEAD_PAD), lambda b, p, i: (b, p)) for kl in kv_lens]
    in_specs += [pl.BlockSpec((kl, 2 * V_HEAD), lambda b, p, i: (b, p)) for kl in kv_lens]
    return pl.pallas_call(
        functools.partial(_attn_kernel, n_kv=n_kv),
        grid=(n_req, N_HEADS // 2, nq),
        in_specs=in_specs,
        out_specs=pl.BlockSpec((tq, 2 * V_HEAD), lambda b, p, i: (b * nq + i, p)),
        out_shape=jax.ShapeDtypeStruct((n_req * t_q, V_W), BF16),
        compiler_params=_cparams(("parallel", "parallel", "parallel")),
        name="attn",
    )(q, *ks, *vs)


def _mixout_kernel(x_ref, a_ref, zc_ref, zs_ref, ct_ref, st_ref, wo_ref, m_ref, g2_ref, wr_ref,
                   x1_ref, h2_ref, aff_ref):
    fm = _dot(ct_ref[...], zc_ref[...]) - _dot(st_ref[...], zs_ref[...])
    y = _dot(a_ref[...], wo_ref[0:V_W, :]) + _dot(fm.astype(BF16), wo_ref[V_W:V_W + FNET_W, :])
    gate1 = m_ref[0, 2:3, :]
    shift2 = m_ref[0, 3:4, :]
    scale2 = m_ref[0, 4:5, :]
    x1 = x_ref[...] + gate1 * y
    x1_ref[...] = x1
    h2 = (_rms(x1, g2_ref[...]) * (1.0 + scale2) + shift2).astype(BF16)
    h2_ref[...] = h2
    lg = lax.dot_general(wr_ref[...], h2, _NT, preferred_element_type=F32)
    e = jnp.exp(lg - jnp.max(lg, axis=0, keepdims=True))
    aff_ref[...] = e / jnp.sum(e, axis=0, keepdims=True)


def _mixout(x, attn, zc, zs, ct, st, wo, mods6, g2, wr_t, *, n_req, t, mod_row):
    tr = TOK_TILE
    nr = t // tr
    full = lambda a: pl.BlockSpec(a.shape, lambda b, i: (0,) * a.ndim)
    return pl.pallas_call(
        _mixout_kernel,
        grid=(n_req, nr),
        in_specs=[pl.BlockSpec((tr, D_MODEL), lambda b, i: (b * nr + i, 0)),
                  pl.BlockSpec((tr, V_W), lambda b, i: (b * nr + i, 0)),
                  pl.BlockSpec((t, FNET_W), lambda b, i: (b, 0)),
                  pl.BlockSpec((t, FNET_W), lambda b, i: (b, 0)),
                  pl.BlockSpec((tr, t), lambda b, i: (i, 0)),
                  pl.BlockSpec((tr, t), lambda b, i: (i, 0)),
                  full(wo),
                  pl.BlockSpec((1, 6, D_MODEL), lambda b, i: (mod_row(b), 0, 0)),
                  full(g2), full(wr_t)],
        out_specs=[pl.BlockSpec((tr, D_MODEL), lambda b, i: (b * nr + i, 0)),
                   pl.BlockSpec((tr, D_MODEL), lambda b, i: (b * nr + i, 0)),
                   pl.BlockSpec((None, N_EXPERTS, tr), lambda b, i: (b, 0, i))],
        out_shape=[jax.ShapeDtypeStruct((n_req * t, D_MODEL), F32),
                   jax.ShapeDtypeStruct((n_req * t, D_MODEL), BF16),
                   jax.ShapeDtypeStruct((n_req, N_EXPERTS, t), F32)],
        compiler_params=_cparams(("parallel", "parallel")),
        name="mixout",
    )(x, attn, zc, zs, ct, st, wo, mods6, g2, wr_t)


def _prefix_count(flags, tri):
    n = flags.shape[1]
    carry = None
    outs = []
    for j in range(n // TOK_TILE):
        c = _dot(flags[:, j * TOK_TILE:(j + 1) * TOK_TILE].astype(BF16), tri)
        if carry is not None:
            c = c + carry
        outs.append(c)
        carry = c[:, TOK_TILE - 1:TOK_TILE]
    return outs[0] if len(outs) == 1 else jnp.concatenate(outs, axis=1)


def _route_kernel(aff_ref, pos_ref, *, cap):
    a = aff_ref[...]
    rows = a.shape[0]
    capf = jnp.float32(cap)
    thr = jnp.zeros((rows, 1), jnp.int32)
    for bit in range(30, -1, -1):
        cand = thr | jnp.int32(1 << bit)
        cand_f = lax.bitcast_convert_type(cand, F32)
        cnt = jnp.sum(jnp.where(a >= cand_f, 1.0, 0.0), axis=1, keepdims=True)
        thr = jnp.where(cnt >= capf, cand, thr)
    thr_f = lax.bitcast_convert_type(thr, F32)
    above_f = lax.bitcast_convert_type(thr + 1, F32)
    gt = jnp.where(a >= above_f, 1.0, 0.0)
    tie = jnp.where(a >= thr_f, 1.0, 0.0) - gt
    need = capf - jnp.sum(gt, axis=1, keepdims=True)
    r_i = lax.broadcasted_iota(jnp.int32, (TOK_TILE, TOK_TILE), 0)
    c_i = lax.broadcasted_iota(jnp.int32, (TOK_TILE, TOK_TILE), 1)
    tri = jnp.where(r_i <= c_i, 1.0, 0.0).astype(BF16)
    tie_before = _prefix_count(tie, tri) - tie
    sel = gt + tie * jnp.where(tie_before < need, 1.0, 0.0)
    slot = _prefix_count(sel, tri) - 1.0
    pos_ref[...] = jnp.where(sel > 0.5, slot, -1.0)


def _route(aff_t, cap):
    return pl.pallas_call(
        functools.partial(_route_kernel, cap=cap),
        out_shape=jax.ShapeDtypeStruct(aff_t.shape, F32),
        compiler_params=pltpu.CompilerParams(vmem_limit_bytes=VMEM_LIMIT),
        name="route",
    )(aff_t)


def _gather_kernel(pos_ref, aff_ref, h_ref, xs_ref, g_ref, *, cap):
    pos = pos_ref[0]
    aff = aff_ref[0]
    eb, n = pos.shape
    slot = lax.broadcasted_iota(jnp.int32, (eb, cap, n), 1).astype(F32)
    hit = pos[:, None, :] == slot
    onehot = jnp.where(hit, 1.0, 0.0).reshape(eb * cap, n).astype(BF16)
    xs = _dot(onehot, h_ref[...])
    xs_ref[...] = xs.astype(BF16).reshape(eb, cap, D_MODEL)
    g_ref[...] = jnp.sum(jnp.where(hit, aff[:, None, :], 0.0), axis=2, keepdims=True)


def _gather(pos_t, aff_t, h2, *, n_req, n, cap, eb):
    ne = N_EXPERTS // eb
    pos3 = pos_t.reshape(n_req * ne, eb, n)
    aff3 = aff_t.reshape(n_req * ne, eb, n)
    return pl.pallas_call(
        functools.partial(_gather_kernel, cap=cap),
        grid=(n_req, ne),
        in_specs=[pl.BlockSpec((1, eb, n), lambda b, e: (b * ne + e, 0, 0)),
                  pl.BlockSpec((1, eb, n), lambda b, e: (b * ne + e, 0, 0)),
                  pl.BlockSpec((n, D_MODEL), lambda b, e: (b, 0))],
        out_specs=[pl.BlockSpec((eb, cap, D_MODEL), lambda b, e: (e, b, 0)),
                   pl.BlockSpec((eb, cap, 1), lambda b, e: (e, b, 0))],
        out_shape=[jax.ShapeDtypeStruct((N_EXPERTS, n_req * cap, D_MODEL), BF16),
                   jax.ShapeDtypeStruct((N_EXPERTS, n_req * cap, 1), F32)],
        compiler_params=_cparams(("parallel", "arbitrary")),
        name="gather",
    )(pos3, aff3, h2)


def _ffn_kernel(xc_ref, xl_ref, gc_ref, gl_ref, wg_ref, wu_ref, wd_ref, yc_ref, yl_ref, wgb, wub, wdb):
    wgb[...] = wg_ref[0].astype(BF16)
    wub[...] = wu_ref[0].astype(BF16)
    wdb[...] = wd_ref[0].astype(BF16)
    for x_ref, g_ref, y_ref in ((xc_ref, gc_ref, yc_ref), (xl_ref, gl_ref, yl_ref)):
        for j in range(x_ref.shape[1] // TOK_TILE):
            rows = slice(j * TOK_TILE, (j + 1) * TOK_TILE)
            x = x_ref[0, rows, :]
            gate = _dot(x, wgb[...])
            up = _dot(x, wub[...])
            hid = (gate * jax.nn.sigmoid(gate) * up).astype(BF16)
            ys = _dot(hid, wdb[...]) * g_ref[0, rows, :]
            y_ref[0, rows, :] = ys.astype(BF16)


def _ffn(xc, xl, gc, gl, wg, wu, wd):
    m = xc.shape[1]
    xspec = pl.BlockSpec((1, m, D_MODEL), lambda e: (e, 0, 0))
    gspec = pl.BlockSpec((1, m, 1), lambda e: (e, 0, 0))
    shp = jax.ShapeDtypeStruct((N_EXPERTS, m, D_MODEL), BF16)
    return pl.pallas_call(
        _ffn_kernel,
        grid=(N_EXPERTS,),
        in_specs=[xspec, xspec, gspec, gspec,
                  pl.BlockSpec((1, D_MODEL, D_EXPERT), lambda e: (e, 0, 0)),
                  pl.BlockSpec((1, D_MODEL, D_EXPERT), lambda e: (e, 0, 0)),
                  pl.BlockSpec((1, D_EXPERT, D_MODEL), lambda e: (e, 0, 0))],
        out_specs=[xspec, xspec],
        out_shape=[shp, shp],
        scratch_shapes=[pltpu.VMEM((D_MODEL, D_EXPERT), BF16), pltpu.VMEM((D_MODEL, D_EXPERT), BF16),
                        pltpu.VMEM((D_EXPERT, D_MODEL), BF16)],
        compiler_params=_cparams(("arbitrary",)),
        name="ffn",
    )(xc, xl, gc, gl, wg, wu, wd)


def _combine_kernel(x1_ref, pos_ref, ys_ref, m_ref, fg_ref, o_ref, *, cap):
    pos = pos_ref[...]
    slot = lax.broadcasted_iota(jnp.int32, (1, cap), 1).astype(F32)
    acc = None
    for e in range(N_EXPERTS):
        onehot = jnp.where(pos[:, e:e + 1] == slot, 1.0, 0.0).astype(BF16)
        part = _dot(onehot, ys_ref[e])
        acc = part if acc is None else acc + part
    gate2 = m_ref[0, 5:6, :]
    o_ref[...] = _rms(x1_ref[...] + gate2 * acc, fg_ref[...])


def _combine(x1, pos_tok, ys, mods6, fg, *, n_req, n, cap, mod_row):
    tr = TOK_TILE
    nr = n // tr
    return pl.pallas_call(
        functools.partial(_combine_kernel, cap=cap),
        grid=(n_req, nr),
        in_specs=[pl.BlockSpec((tr, D_MODEL), lambda b, i: (b * nr + i, 0)),
                  pl.BlockSpec((None, tr, N_EXPERTS), lambda b, i: (b, i, 0)),
                  pl.BlockSpec((N_EXPERTS, cap, D_MODEL), lambda b, i: (0, b, 0)),
                  pl.BlockSpec((1, 6, D_MODEL), lambda b, i: (mod_row(b), 0, 0)),
                  pl.BlockSpec((1, D_MODEL), lambda b, i: (0, 0))],
        out_specs=pl.BlockSpec((tr, D_MODEL), lambda b, i: (b * nr + i, 0)),
        out_shape=jax.ShapeDtypeStruct((n_req * n, D_MODEL), F32),
        compiler_params=_cparams(("parallel", "parallel")),
        name="combine",
    )(x1, pos_tok, ys, mods6, fg)


def _rot_half(w):
    half = QK_ROPE // 2
    return jnp.concatenate([-w[..., half:], w[..., :half]], axis=-1)


def _rope_tables(t):
    n_rows = t // GRID_W
    rows = jnp.repeat(jnp.arange(n_rows, dtype=F32), GRID_W)
    cols = jnp.tile(jnp.arange(GRID_W, dtype=F32), n_rows)
    n_freq = QK_ROPE // 4
    inv_freq = ROPE_BASE ** (-jnp.arange(n_freq, dtype=F32) / n_freq)
    ang = jnp.concatenate([rows[:, None] * inv_freq, cols[:, None] * inv_freq], axis=-1)
    cos = jnp.concatenate([jnp.cos(ang), jnp.cos(ang)], axis=-1)
    sin = jnp.concatenate([jnp.sin(ang), jnp.sin(ang)], axis=-1)
    return cos, sin


def _qk_tables(cos, sin):
    t = cos.shape[0]
    scale = (QK_NOPE + QK_ROPE) ** -0.5
    pad = jnp.zeros((t, HEAD_PAD - QK_NOPE - QK_ROPE), F32)
    cosq = jnp.concatenate([jnp.full((t, QK_NOPE), scale, F32), cos * scale, pad], axis=1)
    sinq = jnp.concatenate([jnp.zeros((t, QK_NOPE), F32), sin * scale, pad], axis=1)
    tq = jnp.concatenate([cosq, sinq], axis=1)
    tk = jnp.concatenate([cos, sin, jnp.zeros((t, LANES - 2 * QK_ROPE), F32)], axis=1)
    return tq, tk


def _dft_tables(t):
    k = jnp.arange(t, dtype=jnp.int32)
    ang = ((k[:, None] * k[None, :]) % t).astype(F32) * (2.0 * np.pi / t)
    scale = (t * FNET_CH) ** -0.5
    return (jnp.cos(ang) * scale).astype(BF16), (jnp.sin(ang) * scale).astype(BF16)


def _block_diag(w):
    g, a, b = w.shape
    eye = jnp.eye(g, dtype=w.dtype)
    return (eye[:, None, :, None] * w[:, :, None, :]).reshape(g * a, g * b)


def kernel(x_prompt, x_sample, cache_ckv, cache_kpe, c, c_ctx, w_mod, b_mod, norm1_g, w_in, q_norm_g, w_uq,
           kv_norm_g, w_ukv, w_fmix, w_out, norm2_g, w_router, w_e_gate, w_e_up, w_e_down, final_g):
    assert w_mod.shape[0] == 1, "single-layer problem"
    n_ctx, t_ctx, _ = x_prompt.shape
    n_lat, t_lat, _ = x_sample.shape
    past = cache_ckv.shape[2]
    ctx_row = n_lat

    w_in0 = w_in[0]
    kpe_cols = w_in0[:, Q_LORA + KV_LORA:Q_LORA + KV_LORA + QK_ROPE]
    win = jnp.concatenate([w_in0[:, :Q_LORA + KV_LORA + QK_ROPE], _rot_half(kpe_cols),
                           jnp.zeros((D_MODEL, 512 - Q_LORA - KV_LORA - 2 * QK_ROPE), F32),
                           w_in0[:, Q_LORA + KV_LORA + QK_ROPE:]], axis=1).astype(BF16)
    wq3 = w_uq[0].reshape(Q_LORA, N_HEADS, QK_NOPE + QK_ROPE)
    qpad = jnp.zeros((Q_LORA, N_HEADS, HEAD_PAD - QK_NOPE - QK_ROPE), F32)
    wuq_main = jnp.concatenate([wq3, qpad], axis=2).reshape(Q_LORA, QK_W)
    wuq_rot = jnp.concatenate([jnp.zeros((Q_LORA, N_HEADS, QK_NOPE), F32), _rot_half(wq3[..., QK_NOPE:]), qpad],
                              axis=2).reshape(Q_LORA, QK_W)
    wuq_lat = jnp.concatenate([wuq_main, wuq_rot], axis=1).astype(BF16)
    wuq_ctx = wuq_main.astype(BF16)
    wkv3 = w_ukv[0].reshape(KV_LORA, N_HEADS, QK_NOPE + V_HEAD)
    wk_top = jnp.concatenate([wkv3[..., :QK_NOPE], jnp.zeros((KV_LORA, N_HEADS, HEAD_PAD - QK_NOPE), F32)],
                             axis=2).reshape(KV_LORA, QK_W)
    place = jnp.concatenate([jnp.zeros((QK_ROPE, QK_NOPE), F32), jnp.eye(QK_ROPE, dtype=F32),
                             jnp.zeros((QK_ROPE, HEAD_PAD - QK_NOPE - QK_ROPE), F32)], axis=1)
    place = jnp.tile(place, (1, N_HEADS))
    wk = jnp.concatenate([wk_top, place, place, jnp.zeros((LANES - 2 * QK_ROPE, QK_W), F32)], axis=0).astype(BF16)
    wv = wkv3[..., QK_NOPE:].reshape(KV_LORA, V_W).astype(BF16)
    wo = w_out[0].astype(BF16)
    wr_t = w_router[0].T.astype(BF16)

    cos, sin = _rope_tables(t_lat)
    tq_lat, tk_lat = _qk_tables(cos, sin)
    tq_ctx, tk_ctx = _qk_tables(jnp.ones((TOK_TILE, QK_ROPE), F32), jnp.zeros((TOK_TILE, QK_ROPE), F32))
    ch = jnp.arange(FNET_CH, dtype=jnp.int32)
    ch_ang = ((ch[:, None] * ch[None, :]) % FNET_CH).astype(F32) * (2.0 * np.pi / FNET_CH)
    ct_ctx, st_ctx = _dft_tables(t_ctx)
    ct_lat, st_lat = _dft_tables(t_lat)

    c8 = jnp.concatenate([c, c_ctx[None, :], jnp.zeros((8 - n_lat - 1, D_MODEL), F32)], axis=0)
    mods6 = _mods(c8, w_mod[0], b_mod[0][None, :]).reshape(8, 6, D_MODEL)
    cw, sw = _fold(jnp.cos(ch_ang), jnp.sin(ch_ang), w_fmix[0])
    wcs = jnp.concatenate([_block_diag(cw), _block_diag(sw)], axis=1).astype(BF16)

    g1 = norm1_g[0][None, :]
    qg = q_norm_g[0][None, :]
    kvg = kv_norm_g[0][None, :]
    g2 = norm2_g[0][None, :]
    fg = final_g[None, :]

    xp = x_prompt.reshape(n_ctx * t_ctx, D_MODEL)
    xs = x_sample.reshape(n_lat * t_lat, D_MODEL)
    tiles_lat = t_lat // TOK_TILE

    qc, kc, vc, zcc, zsc, ckv_c, kpe_c = _premix(
        xp, mods6, g1, win, qg, wuq_ctx, kvg, wk, wv, wcs, tq_ctx, tk_ctx,
        mod_row=lambda i: ctx_row, tab_row=lambda i: 0, rope=False, emit_state=True)
    ql, kl, vl, zcl, zsl = _premix(
        xs, mods6, g1, win, qg, wuq_lat, kvg, wk, wv, wcs, tq_lat, tk_lat,
        mod_row=lambda i: i // tiles_lat, tab_row=lambda i: i % tiles_lat, rope=True, emit_state=False)
    xk_cache = jnp.concatenate([cache_ckv[:, 0], cache_kpe[:, 0],
                                jnp.zeros((n_lat, past, 2 * LANES - KV_LORA - QK_ROPE), F32)],
                               axis=-1).reshape(n_lat * past, 2 * LANES).astype(BF16)
    kpast, vpast = _cachekv(xk_cache, wk, wv)

    attn_c = _attention(qc, [kc], [vc], n_req=n_ctx, t_q=t_ctx, kv_lens=[t_ctx], tq=t_ctx)
    attn_l = _attention(ql, [kpast, kl], [vpast, vl], n_req=n_lat, t_q=t_lat, kv_lens=[past, t_lat], tq=TOK_TILE)

    x1c, h2c, affc = _mixout(xp, attn_c, zcc, zsc, ct_ctx, st_ctx, wo, mods6, g2, wr_t,
                             n_req=n_ctx, t=t_ctx, mod_row=lambda b: ctx_row)
    x1l, h2l, affl = _mixout(xs, attn_l, zcl, zsl, ct_lat, st_lat, wo, mods6, g2, wr_t,
                             n_req=n_lat, t=t_lat, mod_row=lambda b: b)

    cap_c = CAP_FACTOR * t_ctx // N_EXPERTS
    cap_l = CAP_FACTOR * t_lat // N_EXPERTS
    affc2 = affc.reshape(n_ctx * N_EXPERTS, t_ctx)
    affl2 = affl.reshape(n_lat * N_EXPERTS, t_lat)
    posc = _route(affc2, cap_c)
    posl = _route(affl2, cap_l)
    xsc, gc = _gather(posc, affc2, h2c, n_req=n_ctx, n=t_ctx, cap=cap_c, eb=N_EXPERTS)
    xsl, gl = _gather(posl, affl2, h2l, n_req=n_lat, n=t_lat, cap=cap_l, eb=1)
    ysc, ysl = _ffn(xsc, xsl, gc, gl, w_e_gate[0], w_e_up[0], w_e_down[0])

    posc_tok = posc.reshape(n_ctx, N_EXPERTS, t_ctx).transpose(0, 2, 1)
    posl_tok = posl.reshape(n_lat, N_EXPERTS, t_lat).transpose(0, 2, 1)
    y_prompt = _combine(x1c, posc_tok, ysc, mods6, fg, n_req=n_ctx, n=t_ctx, cap=cap_c, mod_row=lambda b: ctx_row)
    y_sample = _combine(x1l, posl_tok, ysl, mods6, fg, n_req=n_lat, n=t_lat, cap=cap_l, mod_row=lambda b: b)

    return (y_prompt.reshape(n_ctx, t_ctx, D_MODEL), y_sample.reshape(n_lat, t_lat, D_MODEL),
            ckv_c.reshape(n_ctx, 1, t_ctx, KV_LORA), kpe_c.reshape(n_ctx, 1, t_ctx, QK_ROPE))
```

```python
import functools

import jax
import jax.numpy as jnp
import numpy as np
from jax import lax
from jax.experimental import pallas as pl
from jax.experimental.pallas import tpu as pltpu

F32 = jnp.float32
BF16 = jnp.bfloat16

D_MODEL = 1024
N_HEADS = 8
QK_NOPE = 64
QK_ROPE = 32
V_HEAD = 64
Q_LORA = 256
KV_LORA = 128
FNET_GROUPS = 8
FNET_CH = 64
FNET_W = FNET_GROUPS * FNET_CH
N_EXPERTS = 16
CAP_FACTOR = 2
D_EXPERT = 512
GRID_W = 64
ROPE_BASE = 10000.0
EPS = 1e-6

LANES = 128
HEAD_PAD = LANES
QK_W = N_HEADS * HEAD_PAD
V_W = N_HEADS * V_HEAD
TOK_TILE = 256
KEY_CHUNK = 512
VMEM_LIMIT = 48 * 1024 * 1024

_NT = (((1,), (1,)), ((), ()))


def _cparams(sem):
    return pltpu.CompilerParams(dimension_semantics=sem, vmem_limit_bytes=VMEM_LIMIT)


def _rms(x, g):
    return x * lax.rsqrt(jnp.mean(x * x, axis=-1, keepdims=True) + EPS) * g


def _dot(a, b):
    return jnp.dot(a, b, preferred_element_type=F32)


def _mods_kernel(c_ref, w_ref, b_ref, o_ref):
    c = c_ref[...]
    s = c * jax.nn.sigmoid(c)
    o_ref[...] = _dot(s.astype(BF16), w_ref[...].astype(BF16)) + b_ref[...]


def _mods(c8, w_mod, b_mod):
    n = w_mod.shape[1]
    tn = 1536
    return pl.pallas_call(
        _mods_kernel,
        grid=(n // tn,),
        in_specs=[pl.BlockSpec((8, D_MODEL), lambda j: (0, 0)),
                  pl.BlockSpec((D_MODEL, tn), lambda j: (0, j)),
                  pl.BlockSpec((1, tn), lambda j: (0, j))],
        out_specs=pl.BlockSpec((8, tn), lambda j: (0, j)),
        out_shape=jax.ShapeDtypeStruct((8, n), F32),
        compiler_params=_cparams(("arbitrary",)),
        name="mods",
    )(c8, w_mod, b_mod)


def _fold_kernel(cc_ref, sc_ref, w_ref, cw_ref, sw_ref):
    for g in range(FNET_GROUPS):
        w = w_ref[g]
        cw_ref[g] = jnp.dot(cc_ref[...], w, preferred_element_type=F32, precision=lax.Precision.HIGHEST)
        sw_ref[g] = jnp.dot(sc_ref[...], w, preferred_element_type=F32, precision=lax.Precision.HIGHEST)


def _fold(cc, sc, w_fmix):
    shp = jax.ShapeDtypeStruct((FNET_GROUPS, FNET_CH, FNET_CH), F32)
    return pl.pallas_call(_fold_kernel, out_shape=(shp, shp), name="fold")(cc, sc, w_fmix)


def _premix_kernel(x_ref, m_ref, g1_ref, win_ref, qg_ref, wuq_ref, kvg_ref, wk_ref, wv_ref, wcs_ref,
                   tq_ref, tk_ref, *outs, rope, emit_state):
    q_ref, k_ref, v_ref, zc_ref, zs_ref = outs[:5]
    shift1 = m_ref[0, 0:1, :]
    scale1 = m_ref[0, 1:2, :]
    h = _rms(x_ref[...], g1_ref[...]) * (1.0 + scale1) + shift1
    proj = _dot(h.astype(BF16), win_ref[...])
    qn = _rms(proj[:, 0:Q_LORA], qg_ref[...]).astype(BF16)
    qq = lax.dot_general(wuq_ref[...], qn, _NT, preferred_element_type=F32)
    cosq = tq_ref[0:LANES, :]
    sinq = tq_ref[LANES:2 * LANES, :]
    for hd in range(N_HEADS):
        lo = hd * HEAD_PAD
        qh = qq[lo:lo + HEAD_PAD, :] * cosq
        if rope:
            qh = qh + qq[QK_W + lo:QK_W + lo + HEAD_PAD, :] * sinq
        q_ref[lo:lo + HEAD_PAD, :] = qh.astype(BF16)
    ckv = _rms(proj[:, Q_LORA:Q_LORA + KV_LORA], kvg_ref[...])
    kpe2 = proj[:, Q_LORA + KV_LORA:Q_LORA + KV_LORA + LANES] * tk_ref[...]
    xk = jnp.concatenate([ckv, kpe2], axis=1).astype(BF16)
    k_ref[...] = _dot(xk, wk_ref[...]).astype(BF16)
    v_ref[...] = lax.dot_general(wv_ref[...], xk[:, 0:KV_LORA], _NT, preferred_element_type=F32).astype(BF16)
    z = _dot(proj[:, 512:1024].astype(BF16), wcs_ref[...])
    zc_ref[...] = z[:, 0:FNET_W].astype(BF16)
    zs_ref[...] = z[:, FNET_W:2 * FNET_W].astype(BF16)
    if emit_state:
        outs[5][...] = ckv
        outs[6][...] = proj[:, Q_LORA + KV_LORA:Q_LORA + KV_LORA + QK_ROPE]


def _premix(x, mods6, g1, win, qg, wuq, kvg, wk, wv, wcs, tq, tk, *, mod_row, tab_row, rope, emit_state):
    n = x.shape[0]
    tm = TOK_TILE
    full = lambda a: pl.BlockSpec(a.shape, lambda i: (0,) * a.ndim)
    out_shape = [jax.ShapeDtypeStruct((QK_W, n), BF16), jax.ShapeDtypeStruct((n, QK_W), BF16),
                 jax.ShapeDtypeStruct((V_W, n), BF16), jax.ShapeDtypeStruct((n, FNET_W), BF16),
                 jax.ShapeDtypeStruct((n, FNET_W), BF16)]
    out_specs = [pl.BlockSpec((QK_W, tm), lambda i: (0, i)), pl.BlockSpec((tm, QK_W), lambda i: (i, 0)),
                 pl.BlockSpec((V_W, tm), lambda i: (0, i)), pl.BlockSpec((tm, FNET_W), lambda i: (i, 0)),
                 pl.BlockSpec((tm, FNET_W), lambda i: (i, 0))]
    if emit_state:
        out_shape += [jax.ShapeDtypeStruct((n, KV_LORA), F32), jax.ShapeDtypeStruct((n, QK_ROPE), F32)]
        out_specs += [pl.BlockSpec((tm, KV_LORA), lambda i: (i, 0)), pl.BlockSpec((tm, QK_ROPE), lambda i: (i, 0))]
    return pl.pallas_call(
        functools.partial(_premix_kernel, rope=rope, emit_state=emit_state),
        grid=(n // tm,),
        in_specs=[pl.BlockSpec((tm, D_MODEL), lambda i: (i, 0)),
                  pl.BlockSpec((1, 6, D_MODEL), lambda i: (mod_row(i), 0, 0)),
                  full(g1), full(win), full(qg), full(wuq), full(kvg), full(wk), full(wv), full(wcs),
                  pl.BlockSpec((2 * LANES, tm), lambda i: (0, tab_row(i))),
                  pl.BlockSpec((tm, LANES), lambda i: (tab_row(i), 0))],
        out_specs=out_specs,
        out_shape=out_shape,
        compiler_params=_cparams(("parallel",)),
        name="premix_lat" if rope else "premix_ctx",
    )(x, mods6, g1, win, qg, wuq, kvg, wk, wv, wcs, tq, tk)


def _cachekv_kernel(xk_ref, wk_ref, wv_ref, k_ref, v_ref):
    xk = xk_ref[...]
    k_ref[...] = _dot(xk, wk_ref[...]).astype(BF16)
    v_ref[...] = lax.dot_general(wv_ref[...], xk[:, 0:KV_LORA], _NT, preferred_element_type=F32).astype(BF16)


def _cachekv(xk, wk, wv):
    n = xk.shape[0]
    tm = 512
    full = lambda a: pl.BlockSpec(a.shape, lambda i: (0,) * a.ndim)
    return pl.pallas_call(
        _cachekv_kernel,
        grid=(n // tm,),
        in_specs=[pl.BlockSpec((tm, 2 * LANES), lambda i: (i, 0)), full(wk), full(wv)],
        out_specs=[pl.BlockSpec((tm, QK_W), lambda i: (i, 0)), pl.BlockSpec((V_W, tm), lambda i: (0, i))],
        out_shape=[jax.ShapeDtypeStruct((n, QK_W), BF16), jax.ShapeDtypeStruct((V_W, n), BF16)],
        compiler_params=_cparams(("parallel",)),
        name="cachekv",
    )(xk, wk, wv)


def _attn_kernel(q_ref, *refs, n_kv, n_pairs, kc):
    k_refs = refs[:n_kv]
    v_refs = refs[n_kv:2 * n_kv]
    o_ref = refs[2 * n_kv]
    tq = q_ref.shape[1]
    zero = jnp.zeros((HEAD_PAD, tq), BF16)
    for pr in range(n_pairs):
        lo = pr * 2 * HEAD_PAD
        q0 = q_ref[lo:lo + HEAD_PAD, :]
        q1 = q_ref[lo + HEAD_PAD:lo + 2 * HEAD_PAD, :]
        qbd = jnp.concatenate([jnp.concatenate([q0, zero], axis=1), jnp.concatenate([zero, q1], axis=1)], axis=0)
        vlo = pr * 2 * V_HEAD
        m = l = o = None
        chunks = [(k_ref, v_ref, c0, min(c0 + kc, k_ref.shape[0]))
                  for k_ref, v_ref in zip(k_refs, v_refs) for c0 in range(0, k_ref.shape[0], kc)]
        score = lambda ch: _dot(ch[0][ch[2]:ch[3], lo:lo + 2 * HEAD_PAD], qbd)
        s_next = score(chunks[0])
        for ci, (k_ref, v_ref, c0, c1) in enumerate(chunks):
            s = s_next
            if ci + 1 < len(chunks):
                s_next = score(chunks[ci + 1])
            cm = jnp.max(s, axis=0, keepdims=True)
            if m is None:
                m = cm
                p = jnp.exp2(s - m)
                l = jnp.sum(p, axis=0, keepdims=True)
                o = _dot(v_ref[vlo:vlo + 2 * V_HEAD, c0:c1], p.astype(BF16))
            else:
                m_new = jnp.maximum(m, cm)
                alpha = jnp.exp2(m - m_new)
                p = jnp.exp2(s - m_new)
                l = alpha * l + jnp.sum(p, axis=0, keepdims=True)
                o = alpha * o + _dot(v_ref[vlo:vlo + 2 * V_HEAD, c0:c1], p.astype(BF16))
                m = m_new
        o = o * (1.0 / l)
        ot = jnp.concatenate([o[0:V_HEAD, 0:tq], o[V_HEAD:2 * V_HEAD, tq:2 * tq]], axis=0)
        o_ref[:, vlo:vlo + 2 * V_HEAD] = ot.T.astype(BF16)


def _attention(q_t, ks, vs_t, *, n_req, t_q, kv_lens, tq, pairs_per_step):
    n_kv = len(ks)
    nq = t_q // tq
    pp = pairs_per_step
    in_specs = [pl.BlockSpec((pp * 2 * HEAD_PAD, tq), lambda b, p, i: (p, b * nq + i))]
    in_specs += [pl.BlockSpec((kl, pp * 2 * HEAD_PAD), lambda b, p, i: (b, p)) for kl in kv_lens]
    in_specs += [pl.BlockSpec((pp * 2 * V_HEAD, kl), lambda b, p, i: (p, b)) for kl in kv_lens]
    return pl.pallas_call(
        functools.partial(_attn_kernel, n_kv=n_kv, n_pairs=pp, kc=KEY_CHUNK),
        grid=(n_req, N_HEADS // 2 // pp, nq),
        in_specs=in_specs,
        out_specs=pl.BlockSpec((tq, pp * 2 * V_HEAD), lambda b, p, i: (b * nq + i, p)),
        out_shape=jax.ShapeDtypeStruct((n_req * t_q, V_W), BF16),
        compiler_params=_cparams(("parallel", "parallel", "parallel")),
        name="attn",
    )(q_t, *ks, *vs_t)


def _mixout_kernel(x_ref, a_ref, zc_ref, zs_ref, cb_ref, sb_ref, off_ref, wo_ref, m_ref, g2_ref, wr_ref,
                   x1_ref, h2_ref, aff_ref, *, t):
    off = off_ref[pl.ds(pl.program_id(1), 1), :]
    co = off[:, 0:t]
    so = off[:, t:2 * t]
    cb = cb_ref[...]
    sb = sb_ref[...]
    ct = (cb * co - sb * so).astype(BF16)
    st = (sb * co + cb * so).astype(BF16)
    fm = _dot(ct, zc_ref[...]) - _dot(st, zs_ref[...])
    y = _dot(a_ref[...], wo_ref[0:V_W, :]) + _dot(fm.astype(BF16), wo_ref[V_W:V_W + FNET_W, :])
    gate1 = m_ref[0, 2:3, :]
    shift2 = m_ref[0, 3:4, :]
    scale2 = m_ref[0, 4:5, :]
    x1 = x_ref[...] + gate1 * y
    x1_ref[...] = x1
    h2 = (_rms(x1, g2_ref[...]) * (1.0 + scale2) + shift2).astype(BF16)
    h2_ref[...] = h2
    lg = lax.dot_general(wr_ref[...], h2, _NT, preferred_element_type=F32)
    e = jnp.exp(lg - jnp.max(lg, axis=0, keepdims=True))
    aff_ref[...] = e / jnp.sum(e, axis=0, keepdims=True)


def _mixout(x, attn, zc, zs, cb, sb, off, wo, mods6, g2, wr_t, *, n_req, t, mod_row):
    tr = TOK_TILE
    nr = t // tr
    full = lambda a: pl.BlockSpec(a.shape, lambda b, i: (0,) * a.ndim)
    return pl.pallas_call(
        functools.partial(_mixout_kernel, t=t),
        grid=(n_req, nr),
        in_specs=[pl.BlockSpec((tr, D_MODEL), lambda b, i: (b * nr + i, 0)),
                  pl.BlockSpec((tr, V_W), lambda b, i: (b * nr + i, 0)),
                  pl.BlockSpec((t, FNET_W), lambda b, i: (b, 0)),
                  pl.BlockSpec((t, FNET_W), lambda b, i: (b, 0)),
                  full(cb), full(sb), full(off),
                  full(wo),
                  pl.BlockSpec((1, 6, D_MODEL), lambda b, i: (mod_row(b), 0, 0)),
                  full(g2), full(wr_t)],
        out_specs=[pl.BlockSpec((tr, D_MODEL), lambda b, i: (b * nr + i, 0)),
                   pl.BlockSpec((tr, D_MODEL), lambda b, i: (b * nr + i, 0)),
                   pl.BlockSpec((None, N_EXPERTS, tr), lambda b, i: (b, 0, i))],
        out_shape=[jax.ShapeDtypeStruct((n_req * t, D_MODEL), F32),
                   jax.ShapeDtypeStruct((n_req * t, D_MODEL), BF16),
                   jax.ShapeDtypeStruct((n_req, N_EXPERTS, t), F32)],
        compiler_params=_cparams(("parallel", "parallel")),
        name="mixout",
    )(x, attn, zc, zs, cb, sb, off, wo, mods6, g2, wr_t)


def _prefix_count(flags, tri):
    n = flags.shape[1]
    carry = None
    outs = []
    for j in range(n // TOK_TILE):
        c = _dot(flags[:, j * TOK_TILE:(j + 1) * TOK_TILE].astype(BF16), tri)
        if carry is not None:
            c = c + carry
        outs.append(c)
        carry = c[:, TOK_TILE - 1:TOK_TILE]
    return outs[0] if len(outs) == 1 else jnp.concatenate(outs, axis=1)


def _route_kernel(aff_ref, pos_ref, *, cap):
    a = aff_ref[...]
    rows = a.shape[0]
    capf = jnp.float32(cap)
    thr = jnp.zeros((rows, 1), jnp.int32)
    for bit in range(30, -1, -1):
        cand = thr | jnp.int32(1 << bit)
        cand_f = lax.bitcast_convert_type(cand, F32)
        cnt = jnp.sum(jnp.where(a >= cand_f, 1.0, 0.0), axis=1, keepdims=True)
        thr = jnp.where(cnt >= capf, cand, thr)
    thr_f = lax.bitcast_convert_type(thr, F32)
    above_f = lax.bitcast_convert_type(thr + 1, F32)
    gt = jnp.where(a >= above_f, 1.0, 0.0)
    tie = jnp.where(a >= thr_f, 1.0, 0.0) - gt
    need = capf - jnp.sum(gt, axis=1, keepdims=True)
    r_i = lax.broadcasted_iota(jnp.int32, (TOK_TILE, TOK_TILE), 0)
    c_i = lax.broadcasted_iota(jnp.int32, (TOK_TILE, TOK_TILE), 1)
    tri = jnp.where(r_i <= c_i, 1.0, 0.0).astype(BF16)
    tie_before = _prefix_count(tie, tri) - tie
    sel = gt + tie * jnp.where(tie_before < need, 1.0, 0.0)
    slot = _prefix_count(sel, tri) - 1.0
    pos_ref[...] = jnp.where(sel > 0.5, slot, -1.0)


def _route(aff_t, cap):
    return pl.pallas_call(
        functools.partial(_route_kernel, cap=cap),
        out_shape=jax.ShapeDtypeStruct(aff_t.shape, F32),
        compiler_params=pltpu.CompilerParams(vmem_limit_bytes=VMEM_LIMIT),
        name="route",
    )(aff_t)


def _gather_kernel(pos_ref, aff_ref, h_ref, xs_ref, g_ref, *, cap):
    pos = pos_ref[0]
    aff = aff_ref[0]
    eb, n = pos.shape
    slot = lax.broadcasted_iota(jnp.int32, (eb, cap, n), 1).astype(F32)
    hit = pos[:, None, :] == slot
    onehot = jnp.where(hit, 1.0, 0.0).reshape(eb * cap, n).astype(BF16)
    xs = _dot(onehot, h_ref[...])
    xs_ref[...] = xs.astype(BF16).reshape(eb, cap, D_MODEL)
    g_ref[...] = jnp.sum(jnp.where(hit, aff[:, None, :], 0.0), axis=2, keepdims=True)


def _gather(pos_t, aff_t, h2, *, n_req, n, cap, eb):
    ne = N_EXPERTS // eb
    pos3 = pos_t.reshape(n_req * ne, eb, n)
    aff3 = aff_t.reshape(n_req * ne, eb, n)
    return pl.pallas_call(
        functools.partial(_gather_kernel, cap=cap),
        grid=(n_req, ne),
        in_specs=[pl.BlockSpec((1, eb, n), lambda b, e: (b * ne + e, 0, 0)),
                  pl.BlockSpec((1, eb, n), lambda b, e: (b * ne + e, 0, 0)),
                  pl.BlockSpec((n, D_MODEL), lambda b, e: (b, 0))],
        out_specs=[pl.BlockSpec((eb, cap, D_MODEL), lambda b, e: (e, b, 0)),
                   pl.BlockSpec((eb, cap, 1), lambda b, e: (e, b, 0))],
        out_shape=[jax.ShapeDtypeStruct((N_EXPERTS, n_req * cap, D_MODEL), BF16),
                   jax.ShapeDtypeStruct((N_EXPERTS, n_req * cap, 1), F32)],
        compiler_params=_cparams(("parallel", "arbitrary")),
        name="gather",
    )(pos3, aff3, h2)


def _ffn_kernel(xc_ref, xl_ref, gc_ref, gl_ref, wg_ref, wu_ref, wd_ref, yc_ref, yl_ref, wgb, wub, wdb):
    wgb[...] = wg_ref[0].astype(BF16)
    wub[...] = wu_ref[0].astype(BF16)
    wdb[...] = wd_ref[0].astype(BF16)
    for x_ref, g_ref, y_ref in ((xc_ref, gc_ref, yc_ref), (xl_ref, gl_ref, yl_ref)):
        for j in range(x_ref.shape[1] // TOK_TILE):
            rows = slice(j * TOK_TILE, (j + 1) * TOK_TILE)
            x = x_ref[0, rows, :]
            gate = _dot(x, wgb[...])
            up = _dot(x, wub[...])
            hid = (gate * jax.nn.sigmoid(gate) * up).astype(BF16)
            ys = _dot(hid, wdb[...]) * g_ref[0, rows, :]
            y_ref[0, rows, :] = ys.astype(BF16)


def _ffn(xc, xl, gc, gl, wg, wu, wd):
    m = xc.shape[1]
    xspec = pl.BlockSpec((1, m, D_MODEL), lambda e: (e, 0, 0))
    gspec = pl.BlockSpec((1, m, 1), lambda e: (e, 0, 0))
    shp = jax.ShapeDtypeStruct((N_EXPERTS, m, D_MODEL), BF16)
    return pl.pallas_call(
        _ffn_kernel,
        grid=(N_EXPERTS,),
        in_specs=[xspec, xspec, gspec, gspec,
                  pl.BlockSpec((1, D_MODEL, D_EXPERT), lambda e: (e, 0, 0)),
                  pl.BlockSpec((1, D_MODEL, D_EXPERT), lambda e: (e, 0, 0)),
                  pl.BlockSpec((1, D_EXPERT, D_MODEL), lambda e: (e, 0, 0))],
        out_specs=[xspec, xspec],
        out_shape=[shp, shp],
        scratch_shapes=[pltpu.VMEM((D_MODEL, D_EXPERT), BF16), pltpu.VMEM((D_MODEL, D_EXPERT), BF16),
                        pltpu.VMEM((D_EXPERT, D_MODEL), BF16)],
        compiler_params=_cparams(("arbitrary",)),
        name="ffn",
    )(xc, xl, gc, gl, wg, wu, wd)


def _combine_kernel(x1_ref, pos_ref, ys_ref, m_ref, fg_ref, o_ref, *, cap):
    pos = pos_ref[...].astype(BF16)
    w = N_EXPERTS * cap
    e_i = lax.broadcasted_iota(jnp.int32, (N_EXPERTS, w), 0)
    j_i = lax.broadcasted_iota(jnp.int32, (N_EXPERTS, w), 1)
    spread = jnp.where((j_i >> (cap.bit_length() - 1)) == e_i, 1.0, 0.0).astype(BF16)
    lane_slot = (lax.broadcasted_iota(jnp.int32, (1, w), 1) & (cap - 1)).astype(F32)
    onehot = jnp.where(_dot(pos, spread) == lane_slot, 1.0, 0.0).astype(BF16)
    acc = _dot(onehot, ys_ref[...].reshape(w, D_MODEL))
    gate2 = m_ref[0, 5:6, :]
    o_ref[...] = _rms(x1_ref[...] + gate2 * acc, fg_ref[...])


def _combine(x1, pos_tok, ys, mods6, fg, *, n_req, n, cap, mod_row):
    tr = TOK_TILE
    nr = n // tr
    return pl.pallas_call(
        functools.partial(_combine_kernel, cap=cap),
        grid=(n_req, nr),
        in_specs=[pl.BlockSpec((tr, D_MODEL), lambda b, i: (b * nr + i, 0)),
                  pl.BlockSpec((None, tr, N_EXPERTS), lambda b, i: (b, i, 0)),
                  pl.BlockSpec((N_EXPERTS, cap, D_MODEL), lambda b, i: (0, b, 0)),
                  pl.BlockSpec((1, 6, D_MODEL), lambda b, i: (mod_row(b), 0, 0)),
                  pl.BlockSpec((1, D_MODEL), lambda b, i: (0, 0))],
        out_specs=pl.BlockSpec((tr, D_MODEL), lambda b, i: (b * nr + i, 0)),
        out_shape=jax.ShapeDtypeStruct((n_req * n, D_MODEL), F32),
        compiler_params=_cparams(("parallel", "parallel")),
        name="combine",
    )(x1, pos_tok, ys, mods6, fg)


def _rot_half(w):
    half = QK_ROPE // 2
    return jnp.concatenate([-w[..., half:], w[..., :half]], axis=-1)


def _rope_tables(t):
    n_rows = t // GRID_W
    rows = jnp.repeat(jnp.arange(n_rows, dtype=F32), GRID_W)
    cols = jnp.tile(jnp.arange(GRID_W, dtype=F32), n_rows)
    n_freq = QK_ROPE // 4
    inv_freq = ROPE_BASE ** (-jnp.arange(n_freq, dtype=F32) / n_freq)
    ang = jnp.concatenate([rows[:, None] * inv_freq, cols[:, None] * inv_freq], axis=-1)
    cos = jnp.concatenate([jnp.cos(ang), jnp.cos(ang)], axis=-1)
    sin = jnp.concatenate([jnp.sin(ang), jnp.sin(ang)], axis=-1)
    return cos, sin


def _qk_tables(cos, sin):
    t = cos.shape[0]
    scale = (QK_NOPE + QK_ROPE) ** -0.5 * np.log2(np.e)
    pad = jnp.zeros((t, HEAD_PAD - QK_NOPE - QK_ROPE), F32)
    cosq = jnp.concatenate([jnp.full((t, QK_NOPE), scale, F32), cos * scale, pad], axis=1)
    sinq = jnp.concatenate([jnp.zeros((t, QK_NOPE), F32), sin * scale, pad], axis=1)
    tq_t = jnp.concatenate([cosq, sinq], axis=1).T
    tk = jnp.concatenate([cos, sin, jnp.zeros((t, LANES - 2 * QK_ROPE), F32)], axis=1)
    return tq_t, tk


def _dft_tables(t):
    r = jnp.arange(TOK_TILE, dtype=jnp.int32)
    k = jnp.arange(t, dtype=jnp.int32)
    ang = ((r[:, None] * k[None, :]) % t).astype(F32) * (2.0 * np.pi / t)
    scale = (t * FNET_CH) ** -0.5
    i0 = jnp.arange(t // TOK_TILE, dtype=jnp.int32) * TOK_TILE
    ang_off = ((i0[:, None] * k[None, :]) % t).astype(F32) * (2.0 * np.pi / t)
    off = jnp.concatenate([jnp.cos(ang_off), jnp.sin(ang_off)], axis=1)
    return jnp.cos(ang) * scale, jnp.sin(ang) * scale, off


def _block_diag(w):
    g, a, b = w.shape
    eye = jnp.eye(g, dtype=w.dtype)
    return (eye[:, None, :, None] * w[:, :, None, :]).reshape(g * a, g * b)


def kernel(x_prompt, x_sample, cache_ckv, cache_kpe, c, c_ctx, w_mod, b_mod, norm1_g, w_in, q_norm_g, w_uq,
           kv_norm_g, w_ukv, w_fmix, w_out, norm2_g, w_router, w_e_gate, w_e_up, w_e_down, final_g):
    assert w_mod.shape[0] == 1, "single-layer problem"
    n_ctx, t_ctx, _ = x_prompt.shape
    n_lat, t_lat, _ = x_sample.shape
    past = cache_ckv.shape[2]
    ctx_row = n_lat

    w_in0 = w_in[0]
    kpe_cols = w_in0[:, Q_LORA + KV_LORA:Q_LORA + KV_LORA + QK_ROPE]
    win = jnp.concatenate([w_in0[:, :Q_LORA + KV_LORA + QK_ROPE], _rot_half(kpe_cols),
                           jnp.zeros((D_MODEL, 512 - Q_LORA - KV_LORA - 2 * QK_ROPE), F32),
                           w_in0[:, Q_LORA + KV_LORA + QK_ROPE:]], axis=1).astype(BF16)
    wq3 = w_uq[0].reshape(Q_LORA, N_HEADS, QK_NOPE + QK_ROPE)
    qpad = jnp.zeros((Q_LORA, N_HEADS, HEAD_PAD - QK_NOPE - QK_ROPE), F32)
    wuq_main = jnp.concatenate([wq3, qpad], axis=2).reshape(Q_LORA, QK_W)
    wuq_rot = jnp.concatenate([jnp.zeros((Q_LORA, N_HEADS, QK_NOPE), F32), _rot_half(wq3[..., QK_NOPE:]), qpad],
                              axis=2).reshape(Q_LORA, QK_W)
    wuq_lat = jnp.concatenate([wuq_main, wuq_rot], axis=1).T.astype(BF16)
    wuq_ctx = wuq_main.T.astype(BF16)
    wkv3 = w_ukv[0].reshape(KV_LORA, N_HEADS, QK_NOPE + V_HEAD)
    wk_top = jnp.concatenate([wkv3[..., :QK_NOPE], jnp.zeros((KV_LORA, N_HEADS, HEAD_PAD - QK_NOPE), F32)],
                             axis=2).reshape(KV_LORA, QK_W)
    place = jnp.concatenate([jnp.zeros((QK_ROPE, QK_NOPE), F32), jnp.eye(QK_ROPE, dtype=F32),
                             jnp.zeros((QK_ROPE, HEAD_PAD - QK_NOPE - QK_ROPE), F32)], axis=1)
    place = jnp.tile(place, (1, N_HEADS))
    wk = jnp.concatenate([wk_top, place, place, jnp.zeros((LANES - 2 * QK_ROPE, QK_W), F32)], axis=0).astype(BF16)
    wv = wkv3[..., QK_NOPE:].reshape(KV_LORA, V_W).T.astype(BF16)
    wo = w_out[0].astype(BF16)
    wr_t = w_router[0].T.astype(BF16)

    cos, sin = _rope_tables(t_lat)
    tq_lat, tk_lat = _qk_tables(cos, sin)
    tq_ctx, tk_ctx = _qk_tables(jnp.ones((TOK_TILE, QK_ROPE), F32), jnp.zeros((TOK_TILE, QK_ROPE), F32))
    ch = jnp.arange(FNET_CH, dtype=jnp.int32)
    ch_ang = ((ch[:, None] * ch[None, :]) % FNET_CH).astype(F32) * (2.0 * np.pi / FNET_CH)
    dft_ctx = _dft_tables(t_ctx)
    dft_lat = _dft_tables(t_lat)

    c8 = jnp.concatenate([c, c_ctx[None, :], jnp.zeros((8 - n_lat - 1, D_MODEL), F32)], axis=0)
    mods6 = _mods(c8, w_mod[0], b_mod[0][None, :]).reshape(8, 6, D_MODEL)
    cw, sw = _fold(jnp.cos(ch_ang), jnp.sin(ch_ang), w_fmix[0])
    wcs = jnp.concatenate([_block_diag(cw), _block_diag(sw)], axis=1).astype(BF16)

    g1 = norm1_g[0][None, :]
    qg = q_norm_g[0][None, :]
    kvg = kv_norm_g[0][None, :]
    g2 = norm2_g[0][None, :]
    fg = final_g[None, :]

    xp = x_prompt.reshape(n_ctx * t_ctx, D_MODEL)
    xs = x_sample.reshape(n_lat * t_lat, D_MODEL)
    tiles_lat = t_lat // TOK_TILE

    qc, kc, vc, zcc, zsc, ckv_c, kpe_c = _premix(
        xp, mods6, g1, win, qg, wuq_ctx, kvg, wk, wv, wcs, tq_ctx, tk_ctx,
        mod_row=lambda i: ctx_row, tab_row=lambda i: 0, rope=False, emit_state=True)
    ql, kl, vl, zcl, zsl = _premix(
        xs, mods6, g1, win, qg, wuq_lat, kvg, wk, wv, wcs, tq_lat, tk_lat,
        mod_row=lambda i: i // tiles_lat, tab_row=lambda i: i % tiles_lat, rope=True, emit_state=False)
    xk_cache = jnp.concatenate([cache_ckv[:, 0], cache_kpe[:, 0],
                                jnp.zeros((n_lat, past, 2 * LANES - KV_LORA - QK_ROPE), F32)],
                               axis=-1).reshape(n_lat * past, 2 * LANES).astype(BF16)
    kpast, vpast = _cachekv(xk_cache, wk, wv)

    attn_c = _attention(qc, [kc], [vc], n_req=n_ctx, t_q=t_ctx, kv_lens=[t_ctx], tq=t_ctx,
                        pairs_per_step=N_HEADS // 2)
    attn_l = _attention(ql, [kpast, kl], [vpast, vl], n_req=n_lat, t_q=t_lat, kv_lens=[past, t_lat], tq=TOK_TILE,
                        pairs_per_step=2)

    x1c, h2c, affc = _mixout(xp, attn_c, zcc, zsc, *dft_ctx, wo, mods6, g2, wr_t,
                             n_req=n_ctx, t=t_ctx, mod_row=lambda b: ctx_row)
    x1l, h2l, affl = _mixout(xs, attn_l, zcl, zsl, *dft_lat, wo, mods6, g2, wr_t,
                             n_req=n_lat, t=t_lat, mod_row=lambda b: b)

    cap_c = CAP_FACTOR * t_ctx // N_EXPERTS
    cap_l = CAP_FACTOR * t_lat // N_EXPERTS
    affc2 = affc.reshape(n_ctx * N_EXPERTS, t_ctx)
    affl2 = affl.reshape(n_lat * N_EXPERTS, t_lat)
    posc = _route(affc2, cap_c)
    posl = _route(affl2, cap_l)
    xsc, gc = _gather(posc, affc2, h2c, n_req=n_ctx, n=t_ctx, cap=cap_c, eb=N_EXPERTS)
    xsl, gl = _gather(posl, affl2, h2l, n_req=n_lat, n=t_lat, cap=cap_l, eb=1)
    ysc, ysl = _ffn(xsc, xsl, gc, gl, w_e_gate[0], w_e_up[0], w_e_down[0])

    posc_tok = posc.reshape(n_ctx, N_EXPERTS, t_ctx).transpose(0, 2, 1)
    posl_tok = posl.reshape(n_lat, N_EXPERTS, t_lat).transpose(0, 2, 1)
    y_prompt = _combine(x1c, posc_tok, ysc, mods6, fg, n_req=n_ctx, n=t_ctx, cap=cap_c, mod_row=lambda b: ctx_row)
    y_sample = _combine(x1l, posl_tok, ysl, mods6, fg, n_req=n_lat, n=t_lat, cap=cap_l, mod_row=lambda b: b)

    return (y_prompt.reshape(n_ctx, t_ctx, D_MODEL), y_sample.reshape(n_lat, t_lat, D_MODEL),
            ckv_c.reshape(n_ctx, 1, t_ctx, KV_LORA), kpe_c.reshape(n_ctx, 1, t_ctx, QK_ROPE))
```

```python
import functools

import jax
import jax.numpy as jnp
import numpy as np
from jax import lax
from jax.experimental import pallas as pl
from jax.experimental.pallas import tpu as pltpu

F32 = jnp.float32
BF16 = jnp.bfloat16

D_MODEL = 1024
N_HEADS = 8
QK_NOPE = 64
QK_ROPE = 32
V_HEAD = 64
Q_LORA = 256
KV_LORA = 128
FNET_GROUPS = 8
FNET_CH = 64
FNET_W = FNET_GROUPS * FNET_CH
N_EXPERTS = 16
CAP_FACTOR = 2
D_EXPERT = 512
GRID_W = 64
ROPE_BASE = 10000.0
EPS = 1e-6

LANES = 128
HEAD_PAD = LANES
QK_W = N_HEADS * HEAD_PAD
V_W = N_HEADS * V_HEAD
TOK_TILE = 256
KEY_CHUNK = 512
VMEM_LIMIT = 48 * 1024 * 1024

_NT = (((1,), (1,)), ((), ()))


def _cparams(sem):
    return pltpu.CompilerParams(dimension_semantics=sem, vmem_limit_bytes=VMEM_LIMIT)


def _rms(x, g):
    return x * lax.rsqrt(jnp.mean(x * x, axis=-1, keepdims=True) + EPS) * g


def _dot(a, b):
    return jnp.dot(a, b, preferred_element_type=F32)


def _mods_kernel(c_ref, w_ref, b_ref, o_ref):
    c = c_ref[...]
    s = c * jax.nn.sigmoid(c)
    o_ref[...] = _dot(s.astype(BF16), w_ref[...].astype(BF16)) + b_ref[...]


def _mods(c8, w_mod, b_mod):
    n = w_mod.shape[1]
    tn = 1536
    return pl.pallas_call(
        _mods_kernel,
        grid=(n // tn,),
        in_specs=[pl.BlockSpec((8, D_MODEL), lambda j: (0, 0)),
                  pl.BlockSpec((D_MODEL, tn), lambda j: (0, j)),
                  pl.BlockSpec((1, tn), lambda j: (0, j))],
        out_specs=pl.BlockSpec((8, tn), lambda j: (0, j)),
        out_shape=jax.ShapeDtypeStruct((8, n), F32),
        compiler_params=_cparams(("arbitrary",)),
        name="mods",
    )(c8, w_mod, b_mod)


def _fold_kernel(cc_ref, sc_ref, w_ref, cw_ref, sw_ref):
    for g in range(FNET_GROUPS):
        w = w_ref[g]
        cw_ref[g] = jnp.dot(cc_ref[...], w, preferred_element_type=F32, precision=lax.Precision.HIGHEST)
        sw_ref[g] = jnp.dot(sc_ref[...], w, preferred_element_type=F32, precision=lax.Precision.HIGHEST)


def _fold(cc, sc, w_fmix):
    shp = jax.ShapeDtypeStruct((FNET_GROUPS, FNET_CH, FNET_CH), F32)
    return pl.pallas_call(_fold_kernel, out_shape=(shp, shp), name="fold")(cc, sc, w_fmix)


def _premix_body(x, m_ref, g1_ref, win_ref, qg_ref, wuq_ref, kvg_ref, wk_ref, wv_ref, wcs_ref, tq_ref, tk_ref, rope):
    shift1 = m_ref[0, 0:1, :]
    scale1 = m_ref[0, 1:2, :]
    h = _rms(x, g1_ref[...]) * (1.0 + scale1) + shift1
    proj = _dot(h.astype(BF16), win_ref[...])
    qn = _rms(proj[:, 0:Q_LORA], qg_ref[...]).astype(BF16)
    qq = lax.dot_general(wuq_ref[...], qn, _NT, preferred_element_type=F32)
    cosq = tq_ref[0:LANES, :]
    sinq = tq_ref[LANES:2 * LANES, :]
    q_heads = []
    for hd in range(N_HEADS):
        lo = hd * HEAD_PAD
        qh = qq[lo:lo + HEAD_PAD, :] * cosq
        if rope:
            qh = qh + qq[QK_W + lo:QK_W + lo + HEAD_PAD, :] * sinq
        q_heads.append(qh.astype(BF16))
    ckv = _rms(proj[:, Q_LORA:Q_LORA + KV_LORA], kvg_ref[...])
    kpe2 = proj[:, Q_LORA + KV_LORA:Q_LORA + KV_LORA + LANES] * tk_ref[...]
    xk = jnp.concatenate([ckv, kpe2], axis=1).astype(BF16)
    k = _dot(xk, wk_ref[...]).astype(BF16)
    v_t = lax.dot_general(wv_ref[...], xk[:, 0:KV_LORA], _NT, preferred_element_type=F32).astype(BF16)
    z = _dot(proj[:, 512:1024].astype(BF16), wcs_ref[...])
    zc = z[:, 0:FNET_W].astype(BF16)
    zs = z[:, FNET_W:2 * FNET_W].astype(BF16)
    kpe = proj[:, Q_LORA + KV_LORA:Q_LORA + KV_LORA + QK_ROPE]
    return q_heads, k, v_t, zc, zs, ckv, kpe


def _premix_kernel(x_ref, m_ref, g1_ref, win_ref, qg_ref, wuq_ref, kvg_ref, wk_ref, wv_ref, wcs_ref,
                   tq_ref, tk_ref, q_ref, k_ref, v_ref, zc_ref, zs_ref, *, rope):
    q_heads, k, v_t, zc, zs, _, _ = _premix_body(x_ref[...], m_ref, g1_ref, win_ref, qg_ref, wuq_ref, kvg_ref,
                                                 wk_ref, wv_ref, wcs_ref, tq_ref, tk_ref, rope)
    for hd, qh in enumerate(q_heads):
        q_ref[hd * HEAD_PAD:(hd + 1) * HEAD_PAD, :] = qh
    k_ref[...] = k
    v_ref[...] = v_t
    zc_ref[...] = zc
    zs_ref[...] = zs


def _premix(x, mods6, g1, win, qg, wuq, kvg, wk, wv, wcs, tq, tk, *, mod_row, tab_row, rope):
    n = x.shape[0]
    tm = TOK_TILE
    full = lambda a: pl.BlockSpec(a.shape, lambda i: (0,) * a.ndim)
    out_shape = [jax.ShapeDtypeStruct((QK_W, n), BF16), jax.ShapeDtypeStruct((n, QK_W), BF16),
                 jax.ShapeDtypeStruct((V_W, n), BF16), jax.ShapeDtypeStruct((n, FNET_W), BF16),
                 jax.ShapeDtypeStruct((n, FNET_W), BF16)]
    out_specs = [pl.BlockSpec((QK_W, tm), lambda i: (0, i)), pl.BlockSpec((tm, QK_W), lambda i: (i, 0)),
                 pl.BlockSpec((V_W, tm), lambda i: (0, i)), pl.BlockSpec((tm, FNET_W), lambda i: (i, 0)),
                 pl.BlockSpec((tm, FNET_W), lambda i: (i, 0))]
    return pl.pallas_call(
        functools.partial(_premix_kernel, rope=rope),
        grid=(n // tm,),
        in_specs=[pl.BlockSpec((tm, D_MODEL), lambda i: (i, 0)),
                  pl.BlockSpec((1, 6, D_MODEL), lambda i: (mod_row(i), 0, 0)),
                  full(g1), full(win), full(qg), full(wuq), full(kvg), full(wk), full(wv), full(wcs),
                  pl.BlockSpec((2 * LANES, tm), lambda i: (0, tab_row(i))),
                  pl.BlockSpec((tm, LANES), lambda i: (tab_row(i), 0))],
        out_specs=out_specs,
        out_shape=out_shape,
        compiler_params=_cparams(("parallel",)),
        name="premix",
    )(x, mods6, g1, win, qg, wuq, kvg, wk, wv, wcs, tq, tk)


def _cachekv_kernel(xk_ref, wk_ref, wv_ref, k_ref, v_ref):
    xk = xk_ref[...]
    k_ref[...] = _dot(xk, wk_ref[...]).astype(BF16)
    v_ref[...] = lax.dot_general(wv_ref[...], xk[:, 0:KV_LORA], _NT, preferred_element_type=F32).astype(BF16)


def _cachekv(xk, wk, wv):
    n = xk.shape[0]
    tm = 512
    full = lambda a: pl.BlockSpec(a.shape, lambda i: (0,) * a.ndim)
    return pl.pallas_call(
        _cachekv_kernel,
        grid=(n // tm,),
        in_specs=[pl.BlockSpec((tm, 2 * LANES), lambda i: (i, 0)), full(wk), full(wv)],
        out_specs=[pl.BlockSpec((tm, QK_W), lambda i: (i, 0)), pl.BlockSpec((V_W, tm), lambda i: (0, i))],
        out_shape=[jax.ShapeDtypeStruct((n, QK_W), BF16), jax.ShapeDtypeStruct((V_W, n), BF16)],
        compiler_params=_cparams(("parallel",)),
        name="cachekv",
    )(xk, wk, wv)


def _attn_body(q_heads, k_refs, v_refs, kc):
    tq = q_heads[0].shape[1]
    zero = jnp.zeros((HEAD_PAD, tq), BF16)
    outs = []
    for pr in range(len(q_heads) // 2):
        lo = pr * 2 * HEAD_PAD
        q0 = q_heads[2 * pr]
        q1 = q_heads[2 * pr + 1]
        qbd = jnp.concatenate([jnp.concatenate([q0, zero], axis=1), jnp.concatenate([zero, q1], axis=1)], axis=0)
        vlo = pr * 2 * V_HEAD
        m = l = o = None
        chunks = [(k_ref, v_ref, c0, min(c0 + kc, k_ref.shape[0]))
                  for k_ref, v_ref in zip(k_refs, v_refs) for c0 in range(0, k_ref.shape[0], kc)]
        score = lambda ch: _dot(ch[0][ch[2]:ch[3], lo:lo + 2 * HEAD_PAD], qbd)
        s_next = score(chunks[0])
        for ci, (k_ref, v_ref, c0, c1) in enumerate(chunks):
            s = s_next
            if ci + 1 < len(chunks):
                s_next = score(chunks[ci + 1])
            cm = jnp.max(s, axis=0, keepdims=True)
            if m is None:
                m = cm
                p = jnp.exp2(s - m)
                l = jnp.sum(p, axis=0, keepdims=True)
                o = _dot(v_ref[vlo:vlo + 2 * V_HEAD, c0:c1], p.astype(BF16))
            else:
                m_new = jnp.maximum(m, cm)
                alpha = jnp.exp2(m - m_new)
                p = jnp.exp2(s - m_new)
                l = alpha * l + jnp.sum(p, axis=0, keepdims=True)
                o = alpha * o + _dot(v_ref[vlo:vlo + 2 * V_HEAD, c0:c1], p.astype(BF16))
                m = m_new
        o = o * (1.0 / l)
        ot = jnp.concatenate([o[0:V_HEAD, 0:tq], o[V_HEAD:2 * V_HEAD, tq:2 * tq]], axis=0)
        outs.append(ot.T.astype(BF16))
    return outs


def _attn_kernel(q_ref, *refs, n_kv, n_pairs, kc):
    q_heads = [q_ref[hd * HEAD_PAD:(hd + 1) * HEAD_PAD, :] for hd in range(2 * n_pairs)]
    outs = _attn_body(q_heads, refs[:n_kv], refs[n_kv:2 * n_kv], kc)
    o_ref = refs[2 * n_kv]
    for pr, o in enumerate(outs):
        o_ref[:, pr * 2 * V_HEAD:(pr + 1) * 2 * V_HEAD] = o


def _attention(q_t, ks, vs_t, *, n_req, t_q, kv_lens, tq, pairs_per_step):
    n_kv = len(ks)
    nq = t_q // tq
    pp = pairs_per_step
    in_specs = [pl.BlockSpec((pp * 2 * HEAD_PAD, tq), lambda b, p, i: (p, b * nq + i))]
    in_specs += [pl.BlockSpec((kl, pp * 2 * HEAD_PAD), lambda b, p, i: (b, p)) for kl in kv_lens]
    in_specs += [pl.BlockSpec((pp * 2 * V_HEAD, kl), lambda b, p, i: (p, b)) for kl in kv_lens]
    return pl.pallas_call(
        functools.partial(_attn_kernel, n_kv=n_kv, n_pairs=pp, kc=KEY_CHUNK),
        grid=(n_req, N_HEADS // 2 // pp, nq),
        in_specs=in_specs,
        out_specs=pl.BlockSpec((tq, pp * 2 * V_HEAD), lambda b, p, i: (b * nq + i, p)),
        out_shape=jax.ShapeDtypeStruct((n_req * t_q, V_W), BF16),
        compiler_params=_cparams(("parallel", "parallel", "parallel")),
        name="attn",
    )(q_t, *ks, *vs_t)


def _mixout_body(x, attn_pairs, zc, zs, ct, st, wo_ref, m_ref, g2_ref, wr_ref):
    fm = _dot(ct, zc) - _dot(st, zs)
    y = _dot(fm.astype(BF16), wo_ref[V_W:V_W + FNET_W, :])
    col = 0
    for a in attn_pairs:
        y = y + _dot(a, wo_ref[col:col + a.shape[1], :])
        col += a.shape[1]
    gate1 = m_ref[0, 2:3, :]
    shift2 = m_ref[0, 3:4, :]
    scale2 = m_ref[0, 4:5, :]
    x1 = x + gate1 * y
    h2 = (_rms(x1, g2_ref[...]) * (1.0 + scale2) + shift2).astype(BF16)
    lg = lax.dot_general(wr_ref[...], h2, _NT, preferred_element_type=F32)
    e = jnp.exp(lg - jnp.max(lg, axis=0, keepdims=True))
    return x1, h2, e / jnp.sum(e, axis=0, keepdims=True)


def _mixout_kernel(x_ref, a_ref, zc_ref, zs_ref, cb_ref, sb_ref, off_ref, wo_ref, m_ref, g2_ref, wr_ref,
                   x1_ref, h2_ref, aff_ref, *, t):
    off = off_ref[pl.ds(pl.program_id(1), 1), :]
    co = off[:, 0:t]
    so = off[:, t:2 * t]
    cb = cb_ref[...]
    sb = sb_ref[...]
    ct = (cb * co - sb * so).astype(BF16)
    st = (sb * co + cb * so).astype(BF16)
    x1, h2, aff = _mixout_body(x_ref[...], [a_ref[...]], zc_ref[...], zs_ref[...], ct, st,
                               wo_ref, m_ref, g2_ref, wr_ref)
    x1_ref[...] = x1
    h2_ref[...] = h2
    aff_ref[...] = aff


def _ctx_front_kernel(x_ref, m_ref, g1_ref, win_ref, qg_ref, wuq_ref, kvg_ref, wk_ref, wv_ref, wcs_ref,
                      tq_ref, tk_ref, ct_ref, st_ref, wo_ref, g2_ref, wr_ref,
                      x1_ref, h2_ref, aff_ref, ckv_ref, kpe_ref, *, kc):
    x = x_ref[...]
    q_heads, k, v_t, zc, zs, ckv, kpe = _premix_body(x, m_ref, g1_ref, win_ref, qg_ref, wuq_ref, kvg_ref,
                                                     wk_ref, wv_ref, wcs_ref, tq_ref, tk_ref, False)
    ckv_ref[...] = ckv
    kpe_ref[...] = kpe
    attn = jnp.concatenate(_attn_body(q_heads, [k], [v_t], kc), axis=1)
    x1, h2, aff = _mixout_body(x, [attn], zc, zs, ct_ref[...], st_ref[...], wo_ref, m_ref, g2_ref, wr_ref)
    x1_ref[...] = x1
    h2_ref[...] = h2
    aff_ref[...] = aff


def _ctx_front(x, mods6, g1, win, qg, wuq, kvg, wk, wv, wcs, tq, tk, ct, st, wo, g2, wr_t, *, n_req, t, mod_row):
    full = lambda a: pl.BlockSpec(a.shape, lambda b: (0,) * a.ndim)
    row = lambda w: pl.BlockSpec((t, w), lambda b: (b, 0))
    return pl.pallas_call(
        functools.partial(_ctx_front_kernel, kc=KEY_CHUNK),
        grid=(n_req,),
        in_specs=[row(D_MODEL), pl.BlockSpec((1, 6, D_MODEL), lambda b: (mod_row, 0, 0)),
                  full(g1), full(win), full(qg), full(wuq), full(kvg), full(wk), full(wv), full(wcs),
                  full(tq), full(tk), full(ct), full(st), full(wo), full(g2), full(wr_t)],
        out_specs=[row(D_MODEL), row(D_MODEL), pl.BlockSpec((None, N_EXPERTS, t), lambda b: (b, 0, 0)),
                   row(KV_LORA), row(QK_ROPE)],
        out_shape=[jax.ShapeDtypeStruct((n_req * t, D_MODEL), F32),
                   jax.ShapeDtypeStruct((n_req * t, D_MODEL), BF16),
                   jax.ShapeDtypeStruct((n_req, N_EXPERTS, t), F32),
                   jax.ShapeDtypeStruct((n_req * t, KV_LORA), F32),
                   jax.ShapeDtypeStruct((n_req * t, QK_ROPE), F32)],
        compiler_params=_cparams(("parallel",)),
        name="ctx_front",
    )(x, mods6, g1, win, qg, wuq, kvg, wk, wv, wcs, tq, tk, ct, st, wo, g2, wr_t)


def _mixout(x, attn, zc, zs, cb, sb, off, wo, mods6, g2, wr_t, *, n_req, t, mod_row):
    tr = TOK_TILE
    nr = t // tr
    full = lambda a: pl.BlockSpec(a.shape, lambda b, i: (0,) * a.ndim)
    return pl.pallas_call(
        functools.partial(_mixout_kernel, t=t),
        grid=(n_req, nr),
        in_specs=[pl.BlockSpec((tr, D_MODEL), lambda b, i: (b * nr + i, 0)),
                  pl.BlockSpec((tr, V_W), lambda b, i: (b * nr + i, 0)),
                  pl.BlockSpec((t, FNET_W), lambda b, i: (b, 0)),
                  pl.BlockSpec((t, FNET_W), lambda b, i: (b, 0)),
                  full(cb), full(sb), full(off),
                  full(wo),
                  pl.BlockSpec((1, 6, D_MODEL), lambda b, i: (mod_row(b), 0, 0)),
                  full(g2), full(wr_t)],
        out_specs=[pl.BlockSpec((tr, D_MODEL), lambda b, i: (b * nr + i, 0)),
                   pl.BlockSpec((tr, D_MODEL), lambda b, i: (b * nr + i, 0)),
                   pl.BlockSpec((None, N_EXPERTS, tr), lambda b, i: (b, 0, i))],
        out_shape=[jax.ShapeDtypeStruct((n_req * t, D_MODEL), F32),
                   jax.ShapeDtypeStruct((n_req * t, D_MODEL), BF16),
                   jax.ShapeDtypeStruct((n_req, N_EXPERTS, t), F32)],
        compiler_params=_cparams(("parallel", "parallel")),
        name="mixout",
    )(x, attn, zc, zs, cb, sb, off, wo, mods6, g2, wr_t)


def _prefix_count(flags, tri):
    n = flags.shape[1]
    carry = None
    outs = []
    for j in range(n // TOK_TILE):
        c = _dot(flags[:, j * TOK_TILE:(j + 1) * TOK_TILE].astype(BF16), tri)
        if carry is not None:
            c = c + carry
        outs.append(c)
        carry = c[:, TOK_TILE - 1:TOK_TILE]
    return outs[0] if len(outs) == 1 else jnp.concatenate(outs, axis=1)


def _route_kernel(aff_ref, pos_ref, *, cap):
    a = aff_ref[...]
    rows = a.shape[0]
    capf = jnp.float32(cap)
    thr = jnp.zeros((rows, 1), jnp.int32)
    for bit in range(30, -1, -1):
        cand = thr | jnp.int32(1 << bit)
        cand_f = lax.bitcast_convert_type(cand, F32)
        cnt = jnp.sum(jnp.where(a >= cand_f, 1.0, 0.0), axis=1, keepdims=True)
        thr = jnp.where(cnt >= capf, cand, thr)
    thr_f = lax.bitcast_convert_type(thr, F32)
    above_f = lax.bitcast_convert_type(thr + 1, F32)
    gt = jnp.where(a >= above_f, 1.0, 0.0)
    tie = jnp.where(a >= thr_f, 1.0, 0.0) - gt
    need = capf - jnp.sum(gt, axis=1, keepdims=True)
    r_i = lax.broadcasted_iota(jnp.int32, (TOK_TILE, TOK_TILE), 0)
    c_i = lax.broadcasted_iota(jnp.int32, (TOK_TILE, TOK_TILE), 1)
    tri = jnp.where(r_i <= c_i, 1.0, 0.0).astype(BF16)
    tie_before = _prefix_count(tie, tri) - tie
    sel = gt + tie * jnp.where(tie_before < need, 1.0, 0.0)
    slot = _prefix_count(sel, tri) - 1.0
    pos_ref[...] = jnp.where(sel > 0.5, slot, -1.0)


def _route(aff_t, cap):
    return pl.pallas_call(
        functools.partial(_route_kernel, cap=cap),
        out_shape=jax.ShapeDtypeStruct(aff_t.shape, F32),
        compiler_params=pltpu.CompilerParams(vmem_limit_bytes=VMEM_LIMIT),
        name="route",
    )(aff_t)


def _gather_kernel(pos_ref, aff_ref, h_ref, xs_ref, g_ref, *, cap):
    pos = pos_ref[0]
    aff = aff_ref[0]
    eb, n = pos.shape
    slot = lax.broadcasted_iota(jnp.int32, (eb, cap, n), 1).astype(F32)
    hit = pos[:, None, :] == slot
    onehot = jnp.where(hit, 1.0, 0.0).reshape(eb * cap, n).astype(BF16)
    xs = _dot(onehot, h_ref[...])
    xs_ref[...] = xs.astype(BF16).reshape(eb, cap, D_MODEL)
    g_ref[...] = jnp.sum(jnp.where(hit, aff[:, None, :], 0.0), axis=2, keepdims=True)


def _gather(pos_t, aff_t, h2, *, n_req, n, cap, eb):
    ne = N_EXPERTS // eb
    pos3 = pos_t.reshape(n_req * ne, eb, n)
    aff3 = aff_t.reshape(n_req * ne, eb, n)
    return pl.pallas_call(
        functools.partial(_gather_kernel, cap=cap),
        grid=(n_req, ne),
        in_specs=[pl.BlockSpec((1, eb, n), lambda b, e: (b * ne + e, 0, 0)),
                  pl.BlockSpec((1, eb, n), lambda b, e: (b * ne + e, 0, 0)),
                  pl.BlockSpec((n, D_MODEL), lambda b, e: (b, 0))],
        out_specs=[pl.BlockSpec((eb, cap, D_MODEL), lambda b, e: (e, b, 0)),
                   pl.BlockSpec((eb, cap, 1), lambda b, e: (e, b, 0))],
        out_shape=[jax.ShapeDtypeStruct((N_EXPERTS, n_req * cap, D_MODEL), BF16),
                   jax.ShapeDtypeStruct((N_EXPERTS, n_req * cap, 1), F32)],
        compiler_params=_cparams(("parallel", "arbitrary")),
        name="gather",
    )(pos3, aff3, h2)


def _ffn_kernel(xc_ref, xl_ref, gc_ref, gl_ref, wg_ref, wu_ref, wd_ref, yc_ref, yl_ref, wgb, wub, wdb):
    wgb[...] = wg_ref[0].astype(BF16)
    wub[...] = wu_ref[0].astype(BF16)
    wdb[...] = wd_ref[0].astype(BF16)
    for x_ref, g_ref, y_ref in ((xc_ref, gc_ref, yc_ref), (xl_ref, gl_ref, yl_ref)):
        for j in range(x_ref.shape[1] // TOK_TILE):
            rows = slice(j * TOK_TILE, (j + 1) * TOK_TILE)
            x = x_ref[0, rows, :]
            gate = _dot(x, wgb[...])
            up = _dot(x, wub[...])
            hid = (gate * jax.nn.sigmoid(gate) * up).astype(BF16)
            ys = _dot(hid, wdb[...]) * g_ref[0, rows, :]
            y_ref[0, rows, :] = ys.astype(BF16)


def _ffn(xc, xl, gc, gl, wg, wu, wd):
    m = xc.shape[1]
    xspec = pl.BlockSpec((1, m, D_MODEL), lambda e: (e, 0, 0))
    gspec = pl.BlockSpec((1, m, 1), lambda e: (e, 0, 0))
    shp = jax.ShapeDtypeStruct((N_EXPERTS, m, D_MODEL), BF16)
    return pl.pallas_call(
        _ffn_kernel,
        grid=(N_EXPERTS,),
        in_specs=[xspec, xspec, gspec, gspec,
                  pl.BlockSpec((1, D_MODEL, D_EXPERT), lambda e: (e, 0, 0)),
                  pl.BlockSpec((1, D_MODEL, D_EXPERT), lambda e: (e, 0, 0)),
                  pl.BlockSpec((1, D_EXPERT, D_MODEL), lambda e: (e, 0, 0))],
        out_specs=[xspec, xspec],
        out_shape=[shp, shp],
        scratch_shapes=[pltpu.VMEM((D_MODEL, D_EXPERT), BF16), pltpu.VMEM((D_MODEL, D_EXPERT), BF16),
                        pltpu.VMEM((D_EXPERT, D_MODEL), BF16)],
        compiler_params=_cparams(("arbitrary",)),
        name="ffn",
    )(xc, xl, gc, gl, wg, wu, wd)


def _combine_kernel(x1_ref, pos_ref, ys_ref, m_ref, fg_ref, o_ref, *, cap):
    pos = pos_ref[...].astype(BF16)
    w = N_EXPERTS * cap
    e_i = lax.broadcasted_iota(jnp.int32, (N_EXPERTS, w), 0)
    j_i = lax.broadcasted_iota(jnp.int32, (N_EXPERTS, w), 1)
    spread = jnp.where((j_i >> (cap.bit_length() - 1)) == e_i, 1.0, 0.0).astype(BF16)
    lane_slot = (lax.broadcasted_iota(jnp.int32, (1, w), 1) & (cap - 1)).astype(F32)
    onehot = jnp.where(_dot(pos, spread) == lane_slot, 1.0, 0.0).astype(BF16)
    acc = _dot(onehot, ys_ref[...].reshape(w, D_MODEL))
    gate2 = m_ref[0, 5:6, :]
    o_ref[...] = _rms(x1_ref[...] + gate2 * acc, fg_ref[...])


def _combine(x1, pos_tok, ys, mods6, fg, *, n_req, n, cap, mod_row):
    tr = TOK_TILE
    nr = n // tr
    return pl.pallas_call(
        functools.partial(_combine_kernel, cap=cap),
        grid=(n_req, nr),
        in_specs=[pl.BlockSpec((tr, D_MODEL), lambda b, i: (b * nr + i, 0)),
                  pl.BlockSpec((None, tr, N_EXPERTS), lambda b, i: (b, i, 0)),
                  pl.BlockSpec((N_EXPERTS, cap, D_MODEL), lambda b, i: (0, b, 0)),
                  pl.BlockSpec((1, 6, D_MODEL), lambda b, i: (mod_row(b), 0, 0)),
                  pl.BlockSpec((1, D_MODEL), lambda b, i: (0, 0))],
        out_specs=pl.BlockSpec((tr, D_MODEL), lambda b, i: (b * nr + i, 0)),
        out_shape=jax.ShapeDtypeStruct((n_req * n, D_MODEL), F32),
        compiler_params=_cparams(("parallel", "parallel")),
        name="combine",
    )(x1, pos_tok, ys, mods6, fg)


def _rot_half(w):
    half = QK_ROPE // 2
    return jnp.concatenate([-w[..., half:], w[..., :half]], axis=-1)


def _rope_tables(t):
    n_rows = t // GRID_W
    rows = jnp.repeat(jnp.arange(n_rows, dtype=F32), GRID_W)
    cols = jnp.tile(jnp.arange(GRID_W, dtype=F32), n_rows)
    n_freq = QK_ROPE // 4
    inv_freq = ROPE_BASE ** (-jnp.arange(n_freq, dtype=F32) / n_freq)
    ang = jnp.concatenate([rows[:, None] * inv_freq, cols[:, None] * inv_freq], axis=-1)
    cos = jnp.concatenate([jnp.cos(ang), jnp.cos(ang)], axis=-1)
    sin = jnp.concatenate([jnp.sin(ang), jnp.sin(ang)], axis=-1)
    return cos, sin


def _qk_tables(cos, sin):
    t = cos.shape[0]
    scale = (QK_NOPE + QK_ROPE) ** -0.5 * np.log2(np.e)
    pad = jnp.zeros((t, HEAD_PAD - QK_NOPE - QK_ROPE), F32)
    cosq = jnp.concatenate([jnp.full((t, QK_NOPE), scale, F32), cos * scale, pad], axis=1)
    sinq = jnp.concatenate([jnp.zeros((t, QK_NOPE), F32), sin * scale, pad], axis=1)
    tq_t = jnp.concatenate([cosq, sinq], axis=1).T
    tk = jnp.concatenate([cos, sin, jnp.zeros((t, LANES - 2 * QK_ROPE), F32)], axis=1)
    return tq_t, tk


def _dft_tables(t):
    r = jnp.arange(TOK_TILE, dtype=jnp.int32)
    k = jnp.arange(t, dtype=jnp.int32)
    ang = ((r[:, None] * k[None, :]) % t).astype(F32) * (2.0 * np.pi / t)
    scale = (t * FNET_CH) ** -0.5
    i0 = jnp.arange(t // TOK_TILE, dtype=jnp.int32) * TOK_TILE
    ang_off = ((i0[:, None] * k[None, :]) % t).astype(F32) * (2.0 * np.pi / t)
    off = jnp.concatenate([jnp.cos(ang_off), jnp.sin(ang_off)], axis=1)
    return jnp.cos(ang) * scale, jnp.sin(ang) * scale, off


def _block_diag(w):
    g, a, b = w.shape
    eye = jnp.eye(g, dtype=w.dtype)
    return (eye[:, None, :, None] * w[:, :, None, :]).reshape(g * a, g * b)


def kernel(x_prompt, x_sample, cache_ckv, cache_kpe, c, c_ctx, w_mod, b_mod, norm1_g, w_in, q_norm_g, w_uq,
           kv_norm_g, w_ukv, w_fmix, w_out, norm2_g, w_router, w_e_gate, w_e_up, w_e_down, final_g):
    assert w_mod.shape[0] == 1, "single-layer problem"
    n_ctx, t_ctx, _ = x_prompt.shape
    n_lat, t_lat, _ = x_sample.shape
    past = cache_ckv.shape[2]
    ctx_row = n_lat

    w_in0 = w_in[0]
    kpe_cols = w_in0[:, Q_LORA + KV_LORA:Q_LORA + KV_LORA + QK_ROPE]
    win = jnp.concatenate([w_in0[:, :Q_LORA + KV_LORA + QK_ROPE], _rot_half(kpe_cols),
                           jnp.zeros((D_MODEL, 512 - Q_LORA - KV_LORA - 2 * QK_ROPE), F32),
                           w_in0[:, Q_LORA + KV_LORA + QK_ROPE:]], axis=1).astype(BF16)
    wq3 = w_uq[0].reshape(Q_LORA, N_HEADS, QK_NOPE + QK_ROPE)
    qpad = jnp.zeros((Q_LORA, N_HEADS, HEAD_PAD - QK_NOPE - QK_ROPE), F32)
    wuq_main = jnp.concatenate([wq3, qpad], axis=2).reshape(Q_LORA, QK_W)
    wuq_rot = jnp.concatenate([jnp.zeros((Q_LORA, N_HEADS, QK_NOPE), F32), _rot_half(wq3[..., QK_NOPE:]), qpad],
                              axis=2).reshape(Q_LORA, QK_W)
    wuq_lat = jnp.concatenate([wuq_main, wuq_rot], axis=1).T.astype(BF16)
    wuq_ctx = wuq_main.T.astype(BF16)
    wkv3 = w_ukv[0].reshape(KV_LORA, N_HEADS, QK_NOPE + V_HEAD)
    wk_top = jnp.concatenate([wkv3[..., :QK_NOPE], jnp.zeros((KV_LORA, N_HEADS, HEAD_PAD - QK_NOPE), F32)],
                             axis=2).reshape(KV_LORA, QK_W)
    place = jnp.concatenate([jnp.zeros((QK_ROPE, QK_NOPE), F32), jnp.eye(QK_ROPE, dtype=F32),
                             jnp.zeros((QK_ROPE, HEAD_PAD - QK_NOPE - QK_ROPE), F32)], axis=1)
    place = jnp.tile(place, (1, N_HEADS))
    wk = jnp.concatenate([wk_top, place, place, jnp.zeros((LANES - 2 * QK_ROPE, QK_W), F32)], axis=0).astype(BF16)
    wv = wkv3[..., QK_NOPE:].reshape(KV_LORA, V_W).T.astype(BF16)
    wo = w_out[0].astype(BF16)
    wr_t = w_router[0].T.astype(BF16)

    cos, sin = _rope_tables(t_lat)
    tq_lat, tk_lat = _qk_tables(cos, sin)
    tq_ctx, tk_ctx = _qk_tables(jnp.ones((TOK_TILE, QK_ROPE), F32), jnp.zeros((TOK_TILE, QK_ROPE), F32))
    ch = jnp.arange(FNET_CH, dtype=jnp.int32)
    ch_ang = ((ch[:, None] * ch[None, :]) % FNET_CH).astype(F32) * (2.0 * np.pi / FNET_CH)
    dft_ctx = _dft_tables(t_ctx)
    dft_lat = _dft_tables(t_lat)

    c8 = jnp.concatenate([c, c_ctx[None, :], jnp.zeros((8 - n_lat - 1, D_MODEL), F32)], axis=0)
    mods6 = _mods(c8, w_mod[0], b_mod[0][None, :]).reshape(8, 6, D_MODEL)
    cw, sw = _fold(jnp.cos(ch_ang), jnp.sin(ch_ang), w_fmix[0])
    wcs = jnp.concatenate([_block_diag(cw), _block_diag(sw)], axis=1).astype(BF16)

    g1 = norm1_g[0][None, :]
    qg = q_norm_g[0][None, :]
    kvg = kv_norm_g[0][None, :]
    g2 = norm2_g[0][None, :]
    fg = final_g[None, :]

    xp = x_prompt.reshape(n_ctx * t_ctx, D_MODEL)
    xs = x_sample.reshape(n_lat * t_lat, D_MODEL)
    tiles_lat = t_lat // TOK_TILE

    x1c, h2c, affc, ckv_c, kpe_c = _ctx_front(
        xp, mods6, g1, win, qg, wuq_ctx, kvg, wk, wv, wcs, tq_ctx, tk_ctx,
        dft_ctx[0].astype(BF16), dft_ctx[1].astype(BF16), wo, g2, wr_t, n_req=n_ctx, t=t_ctx, mod_row=ctx_row)
    ql, kl, vl, zcl, zsl = _premix(
        xs, mods6, g1, win, qg, wuq_lat, kvg, wk, wv, wcs, tq_lat, tk_lat,
        mod_row=lambda i: i // tiles_lat, tab_row=lambda i: i % tiles_lat, rope=True)
    xk_cache = jnp.concatenate([cache_ckv[:, 0], cache_kpe[:, 0],
                                jnp.zeros((n_lat, past, 2 * LANES - KV_LORA - QK_ROPE), F32)],
                               axis=-1).reshape(n_lat * past, 2 * LANES).astype(BF16)
    kpast, vpast = _cachekv(xk_cache, wk, wv)

    attn_l = _attention(ql, [kpast, kl], [vpast, vl], n_req=n_lat, t_q=t_lat, kv_lens=[past, t_lat], tq=TOK_TILE,
                        pairs_per_step=2)

    x1l, h2l, affl = _mixout(xs, attn_l, zcl, zsl, *dft_lat, wo, mods6, g2, wr_t,
                             n_req=n_lat, t=t_lat, mod_row=lambda b: b)

    cap_c = CAP_FACTOR * t_ctx // N_EXPERTS
    cap_l = CAP_FACTOR * t_lat // N_EXPERTS
    affc2 = affc.reshape(n_ctx * N_EXPERTS, t_ctx)
    affl2 = affl.reshape(n_lat * N_EXPERTS, t_lat)
    posc = _route(affc2, cap_c)
    posl = _route(affl2, cap_l)
    xsc, gc = _gather(posc, affc2, h2c, n_req=n_ctx, n=t_ctx, cap=cap_c, eb=N_EXPERTS)
    xsl, gl = _gather(posl, affl2, h2l, n_req=n_lat, n=t_lat, cap=cap_l, eb=1)
    ysc, ysl = _ffn(xsc, xsl, gc, gl, w_e_gate[0], w_e_up[0], w_e_down[0])

    posc_tok = posc.reshape(n_ctx, N_EXPERTS, t_ctx).transpose(0, 2, 1)
    posl_tok = posl.reshape(n_lat, N_EXPERTS, t_lat).transpose(0, 2, 1)
    y_prompt = _combine(x1c, posc_tok, ysc, mods6, fg, n_req=n_ctx, n=t_ctx, cap=cap_c, mod_row=lambda b: ctx_row)
    y_sample = _combine(x1l, posl_tok, ysl, mods6, fg, n_req=n_lat, n=t_lat, cap=cap_l, mod_row=lambda b: b)

    return (y_prompt.reshape(n_ctx, t_ctx, D_MODEL), y_sample.reshape(n_lat, t_lat, D_MODEL),
            ckv_c.reshape(n_ctx, 1, t_ctx, KV_LORA), kpe_c.reshape(n_ctx, 1, t_ctx, QK_ROPE))
```

```python
import functools

import jax
import jax.numpy as jnp
import numpy as np
from jax import lax
from jax.experimental import pallas as pl
from jax.experimental.pallas import tpu as pltpu

F32 = jnp.float32
BF16 = jnp.bfloat16

D_MODEL = 1024
N_HEADS = 8
QK_NOPE = 64
QK_ROPE = 32
V_HEAD = 64
Q_LORA = 256
KV_LORA = 128
FNET_GROUPS = 8
FNET_CH = 64
FNET_W = FNET_GROUPS * FNET_CH
N_EXPERTS = 16
CAP_FACTOR = 2
D_EXPERT = 512
GRID_W = 64
ROPE_BASE = 10000.0
EPS = 1e-6

LANES = 128
HEAD_PAD = LANES
QK_W = N_HEADS * HEAD_PAD
V_W = N_HEADS * V_HEAD
TOK_TILE = 256
KEY_CHUNK = 512
SLOT_WIN = 64
OFF_STRIDE = 16
VMEM_LIMIT = 48 * 1024 * 1024

_NT = (((1,), (1,)), ((), ()))


def _cparams(sem):
    return pltpu.CompilerParams(dimension_semantics=sem, vmem_limit_bytes=VMEM_LIMIT)


def _rms(x, g):
    return x * lax.rsqrt(jnp.mean(x * x, axis=-1, keepdims=True) + EPS) * g


def _dot(a, b):
    return jnp.dot(a, b, preferred_element_type=F32)


def _mods_kernel(c_ref, w_ref, b_ref, o_ref):
    c = c_ref[...]
    s = c * jax.nn.sigmoid(c)
    o_ref[...] = _dot(s.astype(BF16), w_ref[...].astype(BF16)) + b_ref[...]


def _mods(c8, w_mod, b_mod):
    n = w_mod.shape[1]
    tn = 1536
    return pl.pallas_call(
        _mods_kernel,
        grid=(n // tn,),
        in_specs=[pl.BlockSpec((8, D_MODEL), lambda j: (0, 0)),
                  pl.BlockSpec((D_MODEL, tn), lambda j: (0, j)),
                  pl.BlockSpec((1, tn), lambda j: (0, j))],
        out_specs=pl.BlockSpec((8, tn), lambda j: (0, j)),
        out_shape=jax.ShapeDtypeStruct((8, n), F32),
        compiler_params=_cparams(("arbitrary",)),
        name="mods",
    )(c8, w_mod, b_mod)


def _fold_kernel(cc_ref, sc_ref, w_ref, cw_ref, sw_ref):
    for g in range(FNET_GROUPS):
        w = w_ref[g]
        cw_ref[g] = jnp.dot(cc_ref[...], w, preferred_element_type=F32, precision=lax.Precision.HIGHEST)
        sw_ref[g] = jnp.dot(sc_ref[...], w, preferred_element_type=F32, precision=lax.Precision.HIGHEST)


def _fold(cc, sc, w_fmix):
    shp = jax.ShapeDtypeStruct((FNET_GROUPS, FNET_CH, FNET_CH), F32)
    return pl.pallas_call(_fold_kernel, out_shape=(shp, shp), name="fold")(cc, sc, w_fmix)


def _premix_body(x, m_ref, g1_ref, win_ref, qg_ref, wuq_ref, kvg_ref, wk_ref, wv_ref, wcs_ref, tq_ref, tk_ref, rope):
    shift1 = m_ref[0, 0:1, :]
    scale1 = m_ref[0, 1:2, :]
    h = _rms(x, g1_ref[...]) * (1.0 + scale1) + shift1
    proj = _dot(h.astype(BF16), win_ref[...])
    qn = _rms(proj[:, 0:Q_LORA], qg_ref[...]).astype(BF16)
    qq = lax.dot_general(wuq_ref[...], qn, _NT, preferred_element_type=F32)
    cosq = tq_ref[0:LANES, :]
    sinq = tq_ref[LANES:2 * LANES, :]
    q_heads = []
    for hd in range(N_HEADS):
        lo = hd * HEAD_PAD
        qh = qq[lo:lo + HEAD_PAD, :] * cosq
        if rope:
            qh = qh + qq[QK_W + lo:QK_W + lo + HEAD_PAD, :] * sinq
        q_heads.append(qh.astype(BF16))
    ckv = _rms(proj[:, Q_LORA:Q_LORA + KV_LORA], kvg_ref[...])
    kpe2 = proj[:, Q_LORA + KV_LORA:Q_LORA + KV_LORA + LANES] * tk_ref[...]
    xk = jnp.concatenate([ckv, kpe2], axis=1).astype(BF16)
    k = _dot(xk, wk_ref[...]).astype(BF16)
    v_t = lax.dot_general(wv_ref[...], xk[:, 0:KV_LORA], _NT, preferred_element_type=F32).astype(BF16)
    z = _dot(proj[:, 512:1024].astype(BF16), wcs_ref[...])
    zc = z[:, 0:FNET_W].astype(BF16)
    zs = z[:, FNET_W:2 * FNET_W].astype(BF16)
    kpe = proj[:, Q_LORA + KV_LORA:Q_LORA + KV_LORA + QK_ROPE]
    return q_heads, k, v_t, zc, zs, ckv, kpe


def _premix_kernel(x_ref, m_ref, g1_ref, win_ref, qg_ref, wuq_ref, kvg_ref, wk_ref, wv_ref, wcs_ref,
                   tq_ref, tk_ref, q_ref, k_ref, v_ref, zc_ref, zs_ref, *, rope):
    q_heads, k, v_t, zc, zs, _, _ = _premix_body(x_ref[...], m_ref, g1_ref, win_ref, qg_ref, wuq_ref, kvg_ref,
                                                 wk_ref, wv_ref, wcs_ref, tq_ref, tk_ref, rope)
    for hd, qh in enumerate(q_heads):
        q_ref[hd * HEAD_PAD:(hd + 1) * HEAD_PAD, :] = qh
    k_ref[...] = k
    v_ref[...] = v_t
    zc_ref[...] = zc
    zs_ref[...] = zs


def _premix(x, mods6, g1, win, qg, wuq, kvg, wk, wv, wcs, tq, tk, *, mod_row, tab_row, rope):
    n = x.shape[0]
    tm = TOK_TILE
    full = lambda a: pl.BlockSpec(a.shape, lambda i: (0,) * a.ndim)
    out_shape = [jax.ShapeDtypeStruct((QK_W, n), BF16), jax.ShapeDtypeStruct((n, QK_W), BF16),
                 jax.ShapeDtypeStruct((V_W, n), BF16), jax.ShapeDtypeStruct((n, FNET_W), BF16),
                 jax.ShapeDtypeStruct((n, FNET_W), BF16)]
    out_specs = [pl.BlockSpec((QK_W, tm), lambda i: (0, i)), pl.BlockSpec((tm, QK_W), lambda i: (i, 0)),
                 pl.BlockSpec((V_W, tm), lambda i: (0, i)), pl.BlockSpec((tm, FNET_W), lambda i: (i, 0)),
                 pl.BlockSpec((tm, FNET_W), lambda i: (i, 0))]
    return pl.pallas_call(
        functools.partial(_premix_kernel, rope=rope),
        grid=(n // tm,),
        in_specs=[pl.BlockSpec((tm, D_MODEL), lambda i: (i, 0)),
                  pl.BlockSpec((1, 6, D_MODEL), lambda i: (mod_row(i), 0, 0)),
                  full(g1), full(win), full(qg), full(wuq), full(kvg), full(wk), full(wv), full(wcs),
                  pl.BlockSpec((2 * LANES, tm), lambda i: (0, tab_row(i))),
                  pl.BlockSpec((tm, LANES), lambda i: (tab_row(i), 0))],
        out_specs=out_specs,
        out_shape=out_shape,
        compiler_params=_cparams(("parallel",)),
        name="premix",
    )(x, mods6, g1, win, qg, wuq, kvg, wk, wv, wcs, tq, tk)


def _cachekv_kernel(xk_ref, wk_ref, wv_ref, k_ref, v_ref):
    xk = xk_ref[...]
    k_ref[...] = _dot(xk, wk_ref[...]).astype(BF16)
    v_ref[...] = lax.dot_general(wv_ref[...], xk[:, 0:KV_LORA], _NT, preferred_element_type=F32).astype(BF16)


def _cachekv(xk, wk, wv):
    n = xk.shape[0]
    tm = 512
    full = lambda a: pl.BlockSpec(a.shape, lambda i: (0,) * a.ndim)
    return pl.pallas_call(
        _cachekv_kernel,
        grid=(n // tm,),
        in_specs=[pl.BlockSpec((tm, 2 * LANES), lambda i: (i, 0)), full(wk), full(wv)],
        out_specs=[pl.BlockSpec((tm, QK_W), lambda i: (i, 0)), pl.BlockSpec((V_W, tm), lambda i: (0, i))],
        out_shape=[jax.ShapeDtypeStruct((n, QK_W), BF16), jax.ShapeDtypeStruct((V_W, n), BF16)],
        compiler_params=_cparams(("parallel",)),
        name="cachekv",
    )(xk, wk, wv)


def _attn_body(q_heads, k_refs, v_refs, kc):
    tq = q_heads[0].shape[1]
    zero = jnp.zeros((HEAD_PAD, tq), BF16)
    outs = []
    for pr in range(len(q_heads) // 2):
        lo = pr * 2 * HEAD_PAD
        q0 = q_heads[2 * pr]
        q1 = q_heads[2 * pr + 1]
        qbd = jnp.concatenate([jnp.concatenate([q0, zero], axis=1), jnp.concatenate([zero, q1], axis=1)], axis=0)
        vlo = pr * 2 * V_HEAD
        m = l = o = None
        chunks = [(k_ref, v_ref, c0, min(c0 + kc, k_ref.shape[0]))
                  for k_ref, v_ref in zip(k_refs, v_refs) for c0 in range(0, k_ref.shape[0], kc)]
        score = lambda ch: _dot(ch[0][ch[2]:ch[3], lo:lo + 2 * HEAD_PAD], qbd)
        s_next = score(chunks[0])
        for ci, (k_ref, v_ref, c0, c1) in enumerate(chunks):
            s = s_next
            if ci + 1 < len(chunks):
                s_next = score(chunks[ci + 1])
            cm = jnp.max(s, axis=0, keepdims=True)
            if m is None:
                m = cm
                p = jnp.exp2(s - m)
                l = jnp.sum(p, axis=0, keepdims=True)
                o = _dot(v_ref[vlo:vlo + 2 * V_HEAD, c0:c1], p.astype(BF16))
            else:
                m_new = jnp.maximum(m, cm)
                alpha = jnp.exp2(m - m_new)
                p = jnp.exp2(s - m_new)
                l = alpha * l + jnp.sum(p, axis=0, keepdims=True)
                o = alpha * o + _dot(v_ref[vlo:vlo + 2 * V_HEAD, c0:c1], p.astype(BF16))
                m = m_new
        o = o * (1.0 / l)
        ot = jnp.concatenate([o[0:V_HEAD, 0:tq], o[V_HEAD:2 * V_HEAD, tq:2 * tq]], axis=0)
        outs.append(ot.T.astype(BF16))
    return outs


def _attn_kernel(q_ref, *refs, n_kv, n_pairs, kc):
    q_heads = [q_ref[hd * HEAD_PAD:(hd + 1) * HEAD_PAD, :] for hd in range(2 * n_pairs)]
    outs = _attn_body(q_heads, refs[:n_kv], refs[n_kv:2 * n_kv], kc)
    o_ref = refs[2 * n_kv]
    for pr, o in enumerate(outs):
        o_ref[:, pr * 2 * V_HEAD:(pr + 1) * 2 * V_HEAD] = o


def _attention(q_t, ks, vs_t, *, n_req, t_q, kv_lens, tq, pairs_per_step):
    n_kv = len(ks)
    nq = t_q // tq
    pp = pairs_per_step
    in_specs = [pl.BlockSpec((pp * 2 * HEAD_PAD, tq), lambda b, p, i: (p, b * nq + i))]
    in_specs += [pl.BlockSpec((kl, pp * 2 * HEAD_PAD), lambda b, p, i: (b, p)) for kl in kv_lens]
    in_specs += [pl.BlockSpec((pp * 2 * V_HEAD, kl), lambda b, p, i: (p, b)) for kl in kv_lens]
    return pl.pallas_call(
        functools.partial(_attn_kernel, n_kv=n_kv, n_pairs=pp, kc=KEY_CHUNK),
        grid=(n_req, N_HEADS // 2 // pp, nq),
        in_specs=in_specs,
        out_specs=pl.BlockSpec((tq, pp * 2 * V_HEAD), lambda b, p, i: (b * nq + i, p)),
        out_shape=jax.ShapeDtypeStruct((n_req * t_q, V_W), BF16),
        compiler_params=_cparams(("parallel", "parallel", "parallel")),
        name="attn",
    )(q_t, *ks, *vs_t)


def _mixout_body(x, attn_pairs, zc, zs, ct, st, wo_ref, m_ref, g2_ref, wr_ref):
    fm = _dot(ct, zc) - _dot(st, zs)
    y = _dot(fm.astype(BF16), wo_ref[V_W:V_W + FNET_W, :])
    col = 0
    for a in attn_pairs:
        y = y + _dot(a, wo_ref[col:col + a.shape[1], :])
        col += a.shape[1]
    gate1 = m_ref[0, 2:3, :]
    shift2 = m_ref[0, 3:4, :]
    scale2 = m_ref[0, 4:5, :]
    x1 = x + gate1 * y
    h2 = (_rms(x1, g2_ref[...]) * (1.0 + scale2) + shift2).astype(BF16)
    lg = lax.dot_general(wr_ref[...], h2, _NT, preferred_element_type=F32)
    e = jnp.exp(lg - jnp.max(lg, axis=0, keepdims=True))
    return x1, h2, e / jnp.sum(e, axis=0, keepdims=True)


def _mixout_kernel(x_ref, a_ref, zc_ref, zs_ref, cb_ref, sb_ref, off_ref, wo_ref, m_ref, g2_ref, wr_ref,
                   x1_ref, h2_ref, aff_ref, *, t):
    off = off_ref[pl.ds(pl.program_id(1), 1), :]
    co = off[:, 0:t]
    so = off[:, t:2 * t]
    cb = cb_ref[...]
    sb = sb_ref[...]
    ct = (cb * co - sb * so).astype(BF16)
    st = (sb * co + cb * so).astype(BF16)
    x1, h2, aff = _mixout_body(x_ref[...], [a_ref[...]], zc_ref[...], zs_ref[...], ct, st,
                               wo_ref, m_ref, g2_ref, wr_ref)
    x1_ref[...] = x1
    h2_ref[...] = h2
    aff_ref[...] = aff


def _ctx_front_kernel(x_ref, m_ref, g1_ref, win_ref, qg_ref, wuq_ref, kvg_ref, wk_ref, wv_ref, wcs_ref,
                      tq_ref, tk_ref, ct_ref, st_ref, wo_ref, g2_ref, wr_ref,
                      x1_ref, h2_ref, aff_ref, ckv_ref, kpe_ref, *, kc):
    x = x_ref[...]
    q_heads, k, v_t, zc, zs, ckv, kpe = _premix_body(x, m_ref, g1_ref, win_ref, qg_ref, wuq_ref, kvg_ref,
                                                     wk_ref, wv_ref, wcs_ref, tq_ref, tk_ref, False)
    ckv_ref[...] = ckv
    kpe_ref[...] = kpe
    attn = jnp.concatenate(_attn_body(q_heads, [k], [v_t], kc), axis=1)
    x1, h2, aff = _mixout_body(x, [attn], zc, zs, ct_ref[...], st_ref[...], wo_ref, m_ref, g2_ref, wr_ref)
    x1_ref[...] = x1
    h2_ref[...] = h2
    aff_ref[...] = aff


def _ctx_front(x, mods6, g1, win, qg, wuq, kvg, wk, wv, wcs, tq, tk, ct, st, wo, g2, wr_t, *, n_req, t, mod_row):
    full = lambda a: pl.BlockSpec(a.shape, lambda b: (0,) * a.ndim)
    row = lambda w: pl.BlockSpec((t, w), lambda b: (b, 0))
    return pl.pallas_call(
        functools.partial(_ctx_front_kernel, kc=KEY_CHUNK),
        grid=(n_req,),
        in_specs=[row(D_MODEL), pl.BlockSpec((1, 6, D_MODEL), lambda b: (mod_row, 0, 0)),
                  full(g1), full(win), full(qg), full(wuq), full(kvg), full(wk), full(wv), full(wcs),
                  full(tq), full(tk), full(ct), full(st), full(wo), full(g2), full(wr_t)],
        out_specs=[row(D_MODEL), row(D_MODEL), pl.BlockSpec((None, N_EXPERTS, t), lambda b: (b, 0, 0)),
                   row(KV_LORA), row(QK_ROPE)],
        out_shape=[jax.ShapeDtypeStruct((n_req * t, D_MODEL), F32),
                   jax.ShapeDtypeStruct((n_req * t, D_MODEL), BF16),
                   jax.ShapeDtypeStruct((n_req, N_EXPERTS, t), F32),
                   jax.ShapeDtypeStruct((n_req * t, KV_LORA), F32),
                   jax.ShapeDtypeStruct((n_req * t, QK_ROPE), F32)],
        compiler_params=_cparams(("parallel",)),
        name="ctx_front",
    )(x, mods6, g1, win, qg, wuq, kvg, wk, wv, wcs, tq, tk, ct, st, wo, g2, wr_t)


def _mixout(x, attn, zc, zs, cb, sb, off, wo, mods6, g2, wr_t, *, n_req, t, mod_row):
    tr = TOK_TILE
    nr = t // tr
    full = lambda a: pl.BlockSpec(a.shape, lambda b, i: (0,) * a.ndim)
    return pl.pallas_call(
        functools.partial(_mixout_kernel, t=t),
        grid=(n_req, nr),
        in_specs=[pl.BlockSpec((tr, D_MODEL), lambda b, i: (b * nr + i, 0)),
                  pl.BlockSpec((tr, V_W), lambda b, i: (b * nr + i, 0)),
                  pl.BlockSpec((t, FNET_W), lambda b, i: (b, 0)),
                  pl.BlockSpec((t, FNET_W), lambda b, i: (b, 0)),
                  full(cb), full(sb), full(off),
                  full(wo),
                  pl.BlockSpec((1, 6, D_MODEL), lambda b, i: (mod_row(b), 0, 0)),
                  full(g2), full(wr_t)],
        out_specs=[pl.BlockSpec((tr, D_MODEL), lambda b, i: (b * nr + i, 0)),
                   pl.BlockSpec((tr, D_MODEL), lambda b, i: (b * nr + i, 0)),
                   pl.BlockSpec((None, N_EXPERTS, tr), lambda b, i: (b, 0, i))],
        out_shape=[jax.ShapeDtypeStruct((n_req * t, D_MODEL), F32),
                   jax.ShapeDtypeStruct((n_req * t, D_MODEL), BF16),
                   jax.ShapeDtypeStruct((n_req, N_EXPERTS, t), F32)],
        compiler_params=_cparams(("parallel", "parallel")),
        name="mixout",
    )(x, attn, zc, zs, cb, sb, off, wo, mods6, g2, wr_t)


def _prefix_count(flags, tri):
    n = flags.shape[1]
    carry = None
    outs = []
    ends = []
    for j in range(n // TOK_TILE):
        c = _dot(flags[:, j * TOK_TILE:(j + 1) * TOK_TILE].astype(BF16), tri)
        if carry is not None:
            c = c + carry
        outs.append(c)
        carry = c[:, TOK_TILE - 1:TOK_TILE]
        ends.append(carry)
    return (outs[0] if len(outs) == 1 else jnp.concatenate(outs, axis=1)), ends


def _route_kernel(aff_ref, pos_ref, off_ref, *, cap):
    a = aff_ref[...]
    rows = a.shape[0]
    capf = jnp.float32(cap)
    thr = jnp.zeros((rows, 1), jnp.int32)
    for bit in range(30, -1, -1):
        cand = thr | jnp.int32(1 << bit)
        cand_f = lax.bitcast_convert_type(cand, F32)
        cnt = jnp.sum(jnp.where(a >= cand_f, 1.0, 0.0), axis=1, keepdims=True)
        thr = jnp.where(cnt >= capf, cand, thr)
    thr_f = lax.bitcast_convert_type(thr, F32)
    above_f = lax.bitcast_convert_type(thr + 1, F32)
    gt = jnp.where(a >= above_f, 1.0, 0.0)
    tie = jnp.where(a >= thr_f, 1.0, 0.0) - gt
    need = capf - jnp.sum(gt, axis=1, keepdims=True)
    r_i = lax.broadcasted_iota(jnp.int32, (TOK_TILE, TOK_TILE), 0)
    c_i = lax.broadcasted_iota(jnp.int32, (TOK_TILE, TOK_TILE), 1)
    tri = jnp.where(r_i <= c_i, 1.0, 0.0).astype(BF16)
    tie_before = _prefix_count(tie, tri)[0] - tie
    sel = gt + tie * jnp.where(tie_before < need, 1.0, 0.0)
    count, ends = _prefix_count(sel, tri)
    pos_ref[...] = jnp.where(sel > 0.5, count - 1.0, -1.0)
    lane = lax.broadcasted_iota(jnp.int32, (rows, LANES), 1)
    offs = jnp.zeros((rows, LANES), F32)
    for j, end in enumerate(ends):
        offs = offs + jnp.where(lane == j + 1, end, 0.0)
    off_ref[...] = offs.astype(jnp.int32)


def _route(aff_t, cap):
    return pl.pallas_call(
        functools.partial(_route_kernel, cap=cap),
        out_shape=[jax.ShapeDtypeStruct(aff_t.shape, F32),
                   jax.ShapeDtypeStruct((aff_t.shape[0], LANES), jnp.int32)],
        compiler_params=pltpu.CompilerParams(vmem_limit_bytes=VMEM_LIMIT),
        name="route",
    )(aff_t)


def _gather_kernel(pos_ref, aff_ref, h_ref, xs_ref, g_ref, *, cap):
    pos = pos_ref[0]
    aff = aff_ref[0]
    eb, n = pos.shape
    slot = lax.broadcasted_iota(jnp.int32, (eb, cap, n), 1).astype(F32)
    hit = pos[:, None, :] == slot
    onehot = jnp.where(hit, 1.0, 0.0).reshape(eb * cap, n).astype(BF16)
    xs = _dot(onehot, h_ref[...])
    xs_ref[...] = xs.astype(BF16).reshape(eb, cap, D_MODEL)
    g_ref[...] = jnp.sum(jnp.where(hit, aff[:, None, :], 0.0), axis=2, keepdims=True)


def _gather(pos_t, aff_t, h2, *, n_req, n, cap, eb):
    ne = N_EXPERTS // eb
    pos3 = pos_t.reshape(n_req * ne, eb, n)
    aff3 = aff_t.reshape(n_req * ne, eb, n)
    return pl.pallas_call(
        functools.partial(_gather_kernel, cap=cap),
        grid=(n_req, ne),
        in_specs=[pl.BlockSpec((1, eb, n), lambda b, e: (b * ne + e, 0, 0)),
                  pl.BlockSpec((1, eb, n), lambda b, e: (b * ne + e, 0, 0)),
                  pl.BlockSpec((n, D_MODEL), lambda b, e: (b, 0))],
        out_specs=[pl.BlockSpec((eb, cap, D_MODEL), lambda b, e: (e, b, 0)),
                   pl.BlockSpec((eb, cap, 1), lambda b, e: (e, b, 0))],
        out_shape=[jax.ShapeDtypeStruct((N_EXPERTS, n_req * cap, D_MODEL), BF16),
                   jax.ShapeDtypeStruct((N_EXPERTS, n_req * cap, 1), F32)],
        compiler_params=_cparams(("parallel", "arbitrary")),
        name="gather",
    )(pos3, aff3, h2)


def _window_plan(off_ref, b, j, cap, w):
    w0 = []
    need = jnp.int32(0)
    for e in range(N_EXPERTS):
        idx = (b * N_EXPERTS + e) * OFF_STRIDE + j
        base = (off_ref[idx] >> 4) << 4
        w0.append(base)
        need = jnp.maximum(need, off_ref[idx + 1] - base)
    return w0, (need + (w - 1)) >> (w.bit_length() - 1)


def _window(w0_e, p, cap, w):
    low = w0_e + p * w
    start = pl.multiple_of(jnp.minimum(low, cap - w), 16)
    return low, start


def _gather_win_kernel(off_ref, pos_ref, aff_ref, h_ref, xs_ref, g_ref, *, cap, w):
    b = pl.program_id(0)
    j = pl.program_id(1)

    @pl.when(j == 0)
    def _init():
        xs_ref[...] = jnp.zeros_like(xs_ref)
        g_ref[...] = jnp.zeros_like(g_ref)

    pos = pos_ref[...]
    aff = aff_ref[...]
    h = h_ref[...]
    w0, n_pass = _window_plan(off_ref, b, j, cap, w)
    r = lax.broadcasted_iota(jnp.int32, (w, 1), 0).astype(F32)

    def one_pass(p, carry):
        starts, hots, gates = [], [], []
        for e in range(N_EXPERTS):
            low, start = _window(w0[e], p, cap, w)
            starts.append(start)
            mine = jnp.where(r >= (low - start).astype(F32), 1.0, 0.0)
            hot = jnp.where(pos[e:e + 1, :] - start.astype(F32) == r, mine, 0.0)
            hots.append(hot)
            gates.append(jnp.sum(hot * aff[e:e + 1, :], axis=1, keepdims=True))
        rows = _dot(jnp.concatenate(hots, axis=0).astype(BF16), h).astype(BF16)
        for e in range(N_EXPERTS):
            win = pl.ds(starts[e], w)
            xs_ref[e, win, :] = xs_ref[e, win, :] + rows[e * w:(e + 1) * w, :]
            g_ref[e, win, :] = g_ref[e, win, :] + gates[e]
        return carry

    lax.fori_loop(0, n_pass, one_pass, 0)


def _gather_win(offs, pos_t, aff_t, h2, *, n_req, n, cap):
    nb = n // TOK_TILE
    grid_spec = pltpu.PrefetchScalarGridSpec(
        num_scalar_prefetch=1,
        grid=(n_req, nb),
        in_specs=[pl.BlockSpec((N_EXPERTS, TOK_TILE), lambda b, j, off: (b, j)),
                  pl.BlockSpec((N_EXPERTS, TOK_TILE), lambda b, j, off: (b, j)),
                  pl.BlockSpec((TOK_TILE, D_MODEL), lambda b, j, off: (b * nb + j, 0))],
        out_specs=[pl.BlockSpec((N_EXPERTS, cap, D_MODEL), lambda b, j, off: (0, b, 0)),
                   pl.BlockSpec((N_EXPERTS, cap, 1), lambda b, j, off: (0, b, 0))])
    return pl.pallas_call(
        functools.partial(_gather_win_kernel, cap=cap, w=SLOT_WIN),
        grid_spec=grid_spec,
        out_shape=[jax.ShapeDtypeStruct((N_EXPERTS, n_req * cap, D_MODEL), BF16),
                   jax.ShapeDtypeStruct((N_EXPERTS, n_req * cap, 1), F32)],
        compiler_params=_cparams(("parallel", "arbitrary")),
        name="gather_win",
    )(offs, pos_t, aff_t, h2)


def _ffn_kernel(xc_ref, xl_ref, gc_ref, gl_ref, wg_ref, wu_ref, wd_ref, yc_ref, yl_ref, wgb, wub, wdb):
    wgb[...] = wg_ref[0].astype(BF16)
    wub[...] = wu_ref[0].astype(BF16)
    wdb[...] = wd_ref[0].astype(BF16)
    for x_ref, g_ref, y_ref in ((xc_ref, gc_ref, yc_ref), (xl_ref, gl_ref, yl_ref)):
        for j in range(x_ref.shape[1] // TOK_TILE):
            rows = slice(j * TOK_TILE, (j + 1) * TOK_TILE)
            x = x_ref[0, rows, :]
            gate = _dot(x, wgb[...])
            up = _dot(x, wub[...])
            hid = (gate * jax.nn.sigmoid(gate) * up).astype(BF16)
            ys = _dot(hid, wdb[...]) * g_ref[0, rows, :]
            y_ref[0, rows, :] = ys.astype(BF16)


def _ffn(xc, xl, gc, gl, wg, wu, wd):
    m = xc.shape[1]
    xspec = pl.BlockSpec((1, m, D_MODEL), lambda e: (e, 0, 0))
    gspec = pl.BlockSpec((1, m, 1), lambda e: (e, 0, 0))
    shp = jax.ShapeDtypeStruct((N_EXPERTS, m, D_MODEL), BF16)
    return pl.pallas_call(
        _ffn_kernel,
        grid=(N_EXPERTS,),
        in_specs=[xspec, xspec, gspec, gspec,
                  pl.BlockSpec((1, D_MODEL, D_EXPERT), lambda e: (e, 0, 0)),
                  pl.BlockSpec((1, D_MODEL, D_EXPERT), lambda e: (e, 0, 0)),
                  pl.BlockSpec((1, D_EXPERT, D_MODEL), lambda e: (e, 0, 0))],
        out_specs=[xspec, xspec],
        out_shape=[shp, shp],
        scratch_shapes=[pltpu.VMEM((D_MODEL, D_EXPERT), BF16), pltpu.VMEM((D_MODEL, D_EXPERT), BF16),
                        pltpu.VMEM((D_EXPERT, D_MODEL), BF16)],
        compiler_params=_cparams(("arbitrary",)),
        name="ffn",
    )(xc, xl, gc, gl, wg, wu, wd)


def _combine_kernel(x1_ref, pos_ref, ys_ref, m_ref, fg_ref, o_ref, *, cap):
    pos = pos_ref[...].astype(BF16)
    w = N_EXPERTS * cap
    e_i = lax.broadcasted_iota(jnp.int32, (N_EXPERTS, w), 0)
    j_i = lax.broadcasted_iota(jnp.int32, (N_EXPERTS, w), 1)
    spread = jnp.where((j_i >> (cap.bit_length() - 1)) == e_i, 1.0, 0.0).astype(BF16)
    lane_slot = (lax.broadcasted_iota(jnp.int32, (1, w), 1) & (cap - 1)).astype(F32)
    onehot = jnp.where(_dot(pos, spread) == lane_slot, 1.0, 0.0).astype(BF16)
    acc = _dot(onehot, ys_ref[...].reshape(w, D_MODEL))
    gate2 = m_ref[0, 5:6, :]
    o_ref[...] = _rms(x1_ref[...] + gate2 * acc, fg_ref[...])


def _combine(x1, pos_tok, ys, mods6, fg, *, n_req, n, cap, mod_row):
    tr = TOK_TILE
    nr = n // tr
    return pl.pallas_call(
        functools.partial(_combine_kernel, cap=cap),
        grid=(n_req, nr),
        in_specs=[pl.BlockSpec((tr, D_MODEL), lambda b, i: (b * nr + i, 0)),
                  pl.BlockSpec((None, tr, N_EXPERTS), lambda b, i: (b, i, 0)),
                  pl.BlockSpec((N_EXPERTS, cap, D_MODEL), lambda b, i: (0, b, 0)),
                  pl.BlockSpec((1, 6, D_MODEL), lambda b, i: (mod_row(b), 0, 0)),
                  pl.BlockSpec((1, D_MODEL), lambda b, i: (0, 0))],
        out_specs=pl.BlockSpec((tr, D_MODEL), lambda b, i: (b * nr + i, 0)),
        out_shape=jax.ShapeDtypeStruct((n_req * n, D_MODEL), F32),
        compiler_params=_cparams(("parallel", "parallel")),
        name="combine",
    )(x1, pos_tok, ys, mods6, fg)


def _combine_win_kernel(off_ref, x1_ref, pos_ref, ys_ref, m_ref, fg_ref, o_ref, acc_ref, *, cap, w):
    b = pl.program_id(0)
    j = pl.program_id(1)
    pos = pos_ref[...]
    w0, n_pass = _window_plan(off_ref, b, j, cap, w)
    width = N_EXPERTS * w
    e_i = lax.broadcasted_iota(jnp.int32, (N_EXPERTS, width), 0)
    j_i = lax.broadcasted_iota(jnp.int32, (N_EXPERTS, width), 1)
    spread = jnp.where((j_i >> (w.bit_length() - 1)) == e_i, 1.0, 0.0).astype(BF16)
    lane_slot = (lax.broadcasted_iota(jnp.int32, (1, width), 1) & (w - 1)).astype(F32)
    lane_e = lax.broadcasted_iota(jnp.int32, (1, N_EXPERTS), 1)
    acc_ref[...] = jnp.zeros_like(acc_ref)

    def one_pass(p, carry):
        start_row = jnp.zeros((1, N_EXPERTS), F32)
        first_row = jnp.zeros((1, N_EXPERTS), F32)
        wins = []
        for e in range(N_EXPERTS):
            low, start = _window(w0[e], p, cap, w)
            start_row = jnp.where(lane_e == e, start.astype(F32), start_row)
            first_row = jnp.where(lane_e == e, (low - start).astype(F32), first_row)
            wins.append(ys_ref[e, pl.ds(start, w), :])
        rel = pos - start_row
        rel = jnp.where(rel >= first_row, rel, -1.0).astype(BF16)
        onehot = jnp.where(_dot(rel, spread) == lane_slot, 1.0, 0.0).astype(BF16)
        acc_ref[...] += _dot(onehot, jnp.concatenate(wins, axis=0))
        return carry

    lax.fori_loop(0, n_pass, one_pass, 0)
    gate2 = m_ref[0, 5:6, :]
    o_ref[...] = _rms(x1_ref[...] + gate2 * acc_ref[...], fg_ref[...])


def _combine_win(offs, x1, pos_tok, ys, mods6, fg, *, n_req, n, cap, mod_row):
    tr = TOK_TILE
    nr = n // tr
    grid_spec = pltpu.PrefetchScalarGridSpec(
        num_scalar_prefetch=1,
        grid=(n_req, nr),
        in_specs=[pl.BlockSpec((tr, D_MODEL), lambda b, i, off: (b * nr + i, 0)),
                  pl.BlockSpec((None, tr, N_EXPERTS), lambda b, i, off: (b, i, 0)),
                  pl.BlockSpec((N_EXPERTS, cap, D_MODEL), lambda b, i, off: (0, b, 0)),
                  pl.BlockSpec((1, 6, D_MODEL), lambda b, i, off: (mod_row(b), 0, 0)),
                  pl.BlockSpec((1, D_MODEL), lambda b, i, off: (0, 0))],
        out_specs=pl.BlockSpec((tr, D_MODEL), lambda b, i, off: (b * nr + i, 0)),
        scratch_shapes=[pltpu.VMEM((tr, D_MODEL), F32)])
    return pl.pallas_call(
        functools.partial(_combine_win_kernel, cap=cap, w=SLOT_WIN),
        grid_spec=grid_spec,
        out_shape=jax.ShapeDtypeStruct((n_req * n, D_MODEL), F32),
        compiler_params=_cparams(("parallel", "parallel")),
        name="combine_win",
    )(offs, x1, pos_tok, ys, mods6, fg)


def _rot_half(w):
    half = QK_ROPE // 2
    return jnp.concatenate([-w[..., half:], w[..., :half]], axis=-1)


def _rope_tables(t):
    n_rows = t // GRID_W
    rows = jnp.repeat(jnp.arange(n_rows, dtype=F32), GRID_W)
    cols = jnp.tile(jnp.arange(GRID_W, dtype=F32), n_rows)
    n_freq = QK_ROPE // 4
    inv_freq = ROPE_BASE ** (-jnp.arange(n_freq, dtype=F32) / n_freq)
    ang = jnp.concatenate([rows[:, None] * inv_freq, cols[:, None] * inv_freq], axis=-1)
    cos = jnp.concatenate([jnp.cos(ang), jnp.cos(ang)], axis=-1)
    sin = jnp.concatenate([jnp.sin(ang), jnp.sin(ang)], axis=-1)
    return cos, sin


def _qk_tables(cos, sin):
    t = cos.shape[0]
    scale = (QK_NOPE + QK_ROPE) ** -0.5 * np.log2(np.e)
    pad = jnp.zeros((t, HEAD_PAD - QK_NOPE - QK_ROPE), F32)
    cosq = jnp.concatenate([jnp.full((t, QK_NOPE), scale, F32), cos * scale, pad], axis=1)
    sinq = jnp.concatenate([jnp.zeros((t, QK_NOPE), F32), sin * scale, pad], axis=1)
    tq_t = jnp.concatenate([cosq, sinq], axis=1).T
    tk = jnp.concatenate([cos, sin, jnp.zeros((t, LANES - 2 * QK_ROPE), F32)], axis=1)
    return tq_t, tk


def _dft_tables(t):
    r = jnp.arange(TOK_TILE, dtype=jnp.int32)
    k = jnp.arange(t, dtype=jnp.int32)
    ang = ((r[:, None] * k[None, :]) % t).astype(F32) * (2.0 * np.pi / t)
    scale = (t * FNET_CH) ** -0.5
    i0 = jnp.arange(t // TOK_TILE, dtype=jnp.int32) * TOK_TILE
    ang_off = ((i0[:, None] * k[None, :]) % t).astype(F32) * (2.0 * np.pi / t)
    off = jnp.concatenate([jnp.cos(ang_off), jnp.sin(ang_off)], axis=1)
    return jnp.cos(ang) * scale, jnp.sin(ang) * scale, off


def _block_diag(w):
    g, a, b = w.shape
    eye = jnp.eye(g, dtype=w.dtype)
    return (eye[:, None, :, None] * w[:, :, None, :]).reshape(g * a, g * b)


def kernel(x_prompt, x_sample, cache_ckv, cache_kpe, c, c_ctx, w_mod, b_mod, norm1_g, w_in, q_norm_g, w_uq,
           kv_norm_g, w_ukv, w_fmix, w_out, norm2_g, w_router, w_e_gate, w_e_up, w_e_down, final_g):
    assert w_mod.shape[0] == 1, "single-layer problem"
    n_ctx, t_ctx, _ = x_prompt.shape
    n_lat, t_lat, _ = x_sample.shape
    past = cache_ckv.shape[2]
    ctx_row = n_lat

    w_in0 = w_in[0]
    kpe_cols = w_in0[:, Q_LORA + KV_LORA:Q_LORA + KV_LORA + QK_ROPE]
    win = jnp.concatenate([w_in0[:, :Q_LORA + KV_LORA + QK_ROPE], _rot_half(kpe_cols),
                           jnp.zeros((D_MODEL, 512 - Q_LORA - KV_LORA - 2 * QK_ROPE), F32),
                           w_in0[:, Q_LORA + KV_LORA + QK_ROPE:]], axis=1).astype(BF16)
    wq3 = w_uq[0].reshape(Q_LORA, N_HEADS, QK_NOPE + QK_ROPE)
    qpad = jnp.zeros((Q_LORA, N_HEADS, HEAD_PAD - QK_NOPE - QK_ROPE), F32)
    wuq_main = jnp.concatenate([wq3, qpad], axis=2).reshape(Q_LORA, QK_W)
    wuq_rot = jnp.concatenate([jnp.zeros((Q_LORA, N_HEADS, QK_NOPE), F32), _rot_half(wq3[..., QK_NOPE:]), qpad],
                              axis=2).reshape(Q_LORA, QK_W)
    wuq_lat = jnp.concatenate([wuq_main, wuq_rot], axis=1).T.astype(BF16)
    wuq_ctx = wuq_main.T.astype(BF16)
    wkv3 = w_ukv[0].reshape(KV_LORA, N_HEADS, QK_NOPE + V_HEAD)
    wk_top = jnp.concatenate([wkv3[..., :QK_NOPE], jnp.zeros((KV_LORA, N_HEADS, HEAD_PAD - QK_NOPE), F32)],
                             axis=2).reshape(KV_LORA, QK_W)
    place = jnp.concatenate([jnp.zeros((QK_ROPE, QK_NOPE), F32), jnp.eye(QK_ROPE, dtype=F32),
                             jnp.zeros((QK_ROPE, HEAD_PAD - QK_NOPE - QK_ROPE), F32)], axis=1)
    place = jnp.tile(place, (1, N_HEADS))
    wk = jnp.concatenate([wk_top, place, place, jnp.zeros((LANES - 2 * QK_ROPE, QK_W), F32)], axis=0).astype(BF16)
    wv = wkv3[..., QK_NOPE:].reshape(KV_LORA, V_W).T.astype(BF16)
    wo = w_out[0].astype(BF16)
    wr_t = w_router[0].T.astype(BF16)

    cos, sin = _rope_tables(t_lat)
    tq_lat, tk_lat = _qk_tables(cos, sin)
    tq_ctx, tk_ctx = _qk_tables(jnp.ones((TOK_TILE, QK_ROPE), F32), jnp.zeros((TOK_TILE, QK_ROPE), F32))
    ch = jnp.arange(FNET_CH, dtype=jnp.int32)
    ch_ang = ((ch[:, None] * ch[None, :]) % FNET_CH).astype(F32) * (2.0 * np.pi / FNET_CH)
    dft_ctx = _dft_tables(t_ctx)
    dft_lat = _dft_tables(t_lat)

    c8 = jnp.concatenate([c, c_ctx[None, :], jnp.zeros((8 - n_lat - 1, D_MODEL), F32)], axis=0)
    mods6 = _mods(c8, w_mod[0], b_mod[0][None, :]).reshape(8, 6, D_MODEL)
    cw, sw = _fold(jnp.cos(ch_ang), jnp.sin(ch_ang), w_fmix[0])
    wcs = jnp.concatenate([_block_diag(cw), _block_diag(sw)], axis=1).astype(BF16)

    g1 = norm1_g[0][None, :]
    qg = q_norm_g[0][None, :]
    kvg = kv_norm_g[0][None, :]
    g2 = norm2_g[0][None, :]
    fg = final_g[None, :]

    xp = x_prompt.reshape(n_ctx * t_ctx, D_MODEL)
    xs = x_sample.reshape(n_lat * t_lat, D_MODEL)
    tiles_lat = t_lat // TOK_TILE

    x1c, h2c, affc, ckv_c, kpe_c = _ctx_front(
        xp, mods6, g1, win, qg, wuq_ctx, kvg, wk, wv, wcs, tq_ctx, tk_ctx,
        dft_ctx[0].astype(BF16), dft_ctx[1].astype(BF16), wo, g2, wr_t, n_req=n_ctx, t=t_ctx, mod_row=ctx_row)
    ql, kl, vl, zcl, zsl = _premix(
        xs, mods6, g1, win, qg, wuq_lat, kvg, wk, wv, wcs, tq_lat, tk_lat,
        mod_row=lambda i: i // tiles_lat, tab_row=lambda i: i % tiles_lat, rope=True)
    xk_cache = jnp.concatenate([cache_ckv[:, 0], cache_kpe[:, 0],
                                jnp.zeros((n_lat, past, 2 * LANES - KV_LORA - QK_ROPE), F32)],
                               axis=-1).reshape(n_lat * past, 2 * LANES).astype(BF16)
    kpast, vpast = _cachekv(xk_cache, wk, wv)

    attn_l = _attention(ql, [kpast, kl], [vpast, vl], n_req=n_lat, t_q=t_lat, kv_lens=[past, t_lat], tq=TOK_TILE,
                        pairs_per_step=2)

    x1l, h2l, affl = _mixout(xs, attn_l, zcl, zsl, *dft_lat, wo, mods6, g2, wr_t,
                             n_req=n_lat, t=t_lat, mod_row=lambda b: b)

    cap_c = CAP_FACTOR * t_ctx // N_EXPERTS
    cap_l = CAP_FACTOR * t_lat // N_EXPERTS
    affc2 = affc.reshape(n_ctx * N_EXPERTS, t_ctx)
    affl2 = affl.reshape(n_lat * N_EXPERTS, t_lat)
    posc, _ = _route(affc2, cap_c)
    posl, offl = _route(affl2, cap_l)
    assert t_lat // TOK_TILE + 1 <= OFF_STRIDE and cap_l % SLOT_WIN == 0
    offl = offl[:, :OFF_STRIDE].reshape(-1)
    xsc, gc = _gather(posc, affc2, h2c, n_req=n_ctx, n=t_ctx, cap=cap_c, eb=N_EXPERTS)
    xsl, gl = _gather_win(offl, posl, affl2, h2l, n_req=n_lat, n=t_lat, cap=cap_l)
    ysc, ysl = _ffn(xsc, xsl, gc, gl, w_e_gate[0], w_e_up[0], w_e_down[0])

    posc_tok = posc.reshape(n_ctx, N_EXPERTS, t_ctx).transpose(0, 2, 1)
    posl_tok = posl.reshape(n_lat, N_EXPERTS, t_lat).transpose(0, 2, 1)
    y_prompt = _combine(x1c, posc_tok, ysc, mods6, fg, n_req=n_ctx, n=t_ctx, cap=cap_c, mod_row=lambda b: ctx_row)
    y_sample = _combine_win(offl, x1l, posl_tok, ysl, mods6, fg, n_req=n_lat, n=t_lat, cap=cap_l,
                            mod_row=lambda b: b)

    return (y_prompt.reshape(n_ctx, t_ctx, D_MODEL), y_sample.reshape(n_lat, t_lat, D_MODEL),
            ckv_c.reshape(n_ctx, 1, t_ctx, KV_LORA), kpe_c.reshape(n_ctx, 1, t_ctx, QK_ROPE))
```

```python
import functools

import jax
import jax.numpy as jnp
import numpy as np
from jax import lax
from jax.experimental import pallas as pl
from jax.experimental.pallas import tpu as pltpu

F32 = jnp.float32
BF16 = jnp.bfloat16

D_MODEL = 1024
N_HEADS = 8
QK_NOPE = 64
QK_ROPE = 32
V_HEAD = 64
Q_LORA = 256
KV_LORA = 128
FNET_GROUPS = 8
FNET_CH = 64
FNET_W = FNET_GROUPS * FNET_CH
N_EXPERTS = 16
CAP_FACTOR = 2
D_EXPERT = 512
GRID_W = 64
ROPE_BASE = 10000.0
EPS = 1e-6

LANES = 128
HEAD_PAD = LANES
QK_W = N_HEADS * HEAD_PAD
V_W = N_HEADS * V_HEAD
TOK_TILE = 256
KEY_CHUNK = 512
SLOT_WIN = 64
OFF_STRIDE = 16
VMEM_LIMIT = 48 * 1024 * 1024

_NT = (((1,), (1,)), ((), ()))


def _cparams(sem):
    return pltpu.CompilerParams(dimension_semantics=sem, vmem_limit_bytes=VMEM_LIMIT)


def _rms(x, g):
    return x * lax.rsqrt(jnp.mean(x * x, axis=-1, keepdims=True) + EPS) * g


def _dot(a, b):
    return jnp.dot(a, b, preferred_element_type=F32)


def _mods_kernel(c_ref, w_ref, b_ref, o_ref):
    c = c_ref[...]
    s = c * jax.nn.sigmoid(c)
    o_ref[...] = _dot(s.astype(BF16), w_ref[...].astype(BF16)) + b_ref[...]


def _mods(c8, w_mod, b_mod):
    n = w_mod.shape[1]
    tn = 1536
    return pl.pallas_call(
        _mods_kernel,
        grid=(n // tn,),
        in_specs=[pl.BlockSpec((8, D_MODEL), lambda j: (0, 0)),
                  pl.BlockSpec((D_MODEL, tn), lambda j: (0, j)),
                  pl.BlockSpec((1, tn), lambda j: (0, j))],
        out_specs=pl.BlockSpec((8, tn), lambda j: (0, j)),
        out_shape=jax.ShapeDtypeStruct((8, n), F32),
        compiler_params=_cparams(("arbitrary",)),
        name="mods",
    )(c8, w_mod, b_mod)


def _fold_kernel(cc_ref, sc_ref, w_ref, cw_ref, sw_ref):
    for g in range(FNET_GROUPS):
        w = w_ref[g]
        cw_ref[g] = jnp.dot(cc_ref[...], w, preferred_element_type=F32, precision=lax.Precision.HIGHEST)
        sw_ref[g] = jnp.dot(sc_ref[...], w, preferred_element_type=F32, precision=lax.Precision.HIGHEST)


def _fold(cc, sc, w_fmix):
    shp = jax.ShapeDtypeStruct((FNET_GROUPS, FNET_CH, FNET_CH), F32)
    return pl.pallas_call(_fold_kernel, out_shape=(shp, shp), name="fold")(cc, sc, w_fmix)


def _premix_body(x, m_ref, g1_ref, win_ref, qg_ref, wuq_ref, kvg_ref, wk_ref, wv_ref, wcs_ref, tq_ref, tk_ref, rope):
    shift1 = m_ref[0, 0:1, :]
    scale1 = m_ref[0, 1:2, :]
    h = _rms(x, g1_ref[...]) * (1.0 + scale1) + shift1
    proj = _dot(h.astype(BF16), win_ref[...])
    qn = _rms(proj[:, 0:Q_LORA], qg_ref[...]).astype(BF16)
    qq = lax.dot_general(wuq_ref[...], qn, _NT, preferred_element_type=F32)
    cosq = tq_ref[0:LANES, :]
    sinq = tq_ref[LANES:2 * LANES, :]
    q_heads = []
    for hd in range(N_HEADS):
        lo = hd * HEAD_PAD
        qh = qq[lo:lo + HEAD_PAD, :] * cosq
        if rope:
            qh = qh + qq[QK_W + lo:QK_W + lo + HEAD_PAD, :] * sinq
        q_heads.append(qh.astype(BF16))
    ckv = _rms(proj[:, Q_LORA:Q_LORA + KV_LORA], kvg_ref[...])
    kpe2 = proj[:, Q_LORA + KV_LORA:Q_LORA + KV_LORA + LANES] * tk_ref[...]
    xk = jnp.concatenate([ckv, kpe2], axis=1).astype(BF16)
    k = _dot(xk, wk_ref[...]).astype(BF16)
    v_t = lax.dot_general(wv_ref[...], xk[:, 0:KV_LORA], _NT, preferred_element_type=F32).astype(BF16)
    z = _dot(proj[:, 512:1024].astype(BF16), wcs_ref[...])
    zc = z[:, 0:FNET_W].astype(BF16)
    zs = z[:, FNET_W:2 * FNET_W].astype(BF16)
    kpe = proj[:, Q_LORA + KV_LORA:Q_LORA + KV_LORA + QK_ROPE]
    return q_heads, k, v_t, zc, zs, ckv, kpe


def _premix_kernel(x_ref, m_ref, g1_ref, win_ref, qg_ref, wuq_ref, kvg_ref, wk_ref, wv_ref, wcs_ref,
                   tq_ref, tk_ref, q_ref, k_ref, v_ref, zc_ref, zs_ref, *, rope):
    q_heads, k, v_t, zc, zs, _, _ = _premix_body(x_ref[...], m_ref, g1_ref, win_ref, qg_ref, wuq_ref, kvg_ref,
                                                 wk_ref, wv_ref, wcs_ref, tq_ref, tk_ref, rope)
    for hd, qh in enumerate(q_heads):
        q_ref[hd * HEAD_PAD:(hd + 1) * HEAD_PAD, :] = qh
    k_ref[...] = k
    v_ref[...] = v_t
    zc_ref[...] = zc
    zs_ref[...] = zs


def _premix(x, mods6, g1, win, qg, wuq, kvg, wk, wv, wcs, tq, tk, *, mod_row, tab_row, rope):
    n = x.shape[0]
    tm = TOK_TILE
    full = lambda a: pl.BlockSpec(a.shape, lambda i: (0,) * a.ndim)
    out_shape = [jax.ShapeDtypeStruct((QK_W, n), BF16), jax.ShapeDtypeStruct((n, QK_W), BF16),
                 jax.ShapeDtypeStruct((V_W, n), BF16), jax.ShapeDtypeStruct((n, FNET_W), BF16),
                 jax.ShapeDtypeStruct((n, FNET_W), BF16)]
    out_specs = [pl.BlockSpec((QK_W, tm), lambda i: (0, i)), pl.BlockSpec((tm, QK_W), lambda i: (i, 0)),
                 pl.BlockSpec((V_W, tm), lambda i: (0, i)), pl.BlockSpec((tm, FNET_W), lambda i: (i, 0)),
                 pl.BlockSpec((tm, FNET_W), lambda i: (i, 0))]
    return pl.pallas_call(
        functools.partial(_premix_kernel, rope=rope),
        grid=(n // tm,),
        in_specs=[pl.BlockSpec((tm, D_MODEL), lambda i: (i, 0)),
                  pl.BlockSpec((1, 6, D_MODEL), lambda i: (mod_row(i), 0, 0)),
                  full(g1), full(win), full(qg), full(wuq), full(kvg), full(wk), full(wv), full(wcs),
                  pl.BlockSpec((2 * LANES, tm), lambda i: (0, tab_row(i))),
                  pl.BlockSpec((tm, LANES), lambda i: (tab_row(i), 0))],
        out_specs=out_specs,
        out_shape=out_shape,
        compiler_params=_cparams(("parallel",)),
        name="premix",
    )(x, mods6, g1, win, qg, wuq, kvg, wk, wv, wcs, tq, tk)


def _cachekv_kernel(xk_ref, wk_ref, wv_ref, k_ref, v_ref):
    xk = xk_ref[...]
    k_ref[...] = _dot(xk, wk_ref[...]).astype(BF16)
    v_ref[...] = lax.dot_general(wv_ref[...], xk[:, 0:KV_LORA], _NT, preferred_element_type=F32).astype(BF16)


def _cachekv(xk, wk, wv):
    n = xk.shape[0]
    tm = 512
    full = lambda a: pl.BlockSpec(a.shape, lambda i: (0,) * a.ndim)
    return pl.pallas_call(
        _cachekv_kernel,
        grid=(n // tm,),
        in_specs=[pl.BlockSpec((tm, 2 * LANES), lambda i: (i, 0)), full(wk), full(wv)],
        out_specs=[pl.BlockSpec((tm, QK_W), lambda i: (i, 0)), pl.BlockSpec((V_W, tm), lambda i: (0, i))],
        out_shape=[jax.ShapeDtypeStruct((n, QK_W), BF16), jax.ShapeDtypeStruct((V_W, n), BF16)],
        compiler_params=_cparams(("parallel",)),
        name="cachekv",
    )(xk, wk, wv)


def _attn_body(q_heads, k_refs, v_refs, kc):
    tq = q_heads[0].shape[1]
    zero = jnp.zeros((HEAD_PAD, tq), BF16)
    n_pairs = len(q_heads) // 2
    qbd = [jnp.concatenate([jnp.concatenate([q_heads[2 * pr], zero], axis=1),
                            jnp.concatenate([zero, q_heads[2 * pr + 1]], axis=1)], axis=0) for pr in range(n_pairs)]
    chunks = [(k_ref, v_ref, c0, min(c0 + kc, k_ref.shape[0]))
              for k_ref, v_ref in zip(k_refs, v_refs) for c0 in range(0, k_ref.shape[0], kc)]
    work = [(pr, ch) for pr in range(n_pairs) for ch in chunks]

    def score(item):
        pr, (k_ref, _, c0, c1) = item
        return _dot(k_ref[c0:c1, pr * 2 * HEAD_PAD:(pr + 1) * 2 * HEAD_PAD], qbd[pr])

    m = [None] * n_pairs
    o = [None] * n_pairs
    s_next = score(work[0])
    for wi, (pr, (k_ref, v_ref, c0, c1)) in enumerate(work):
        s = s_next
        if wi + 1 < len(work):
            s_next = score(work[wi + 1])
        cm = jnp.max(s, axis=0, keepdims=True)
        vlo = pr * 2 * V_HEAD
        va = jnp.concatenate([v_ref[vlo:vlo + 2 * V_HEAD, c0:c1], jnp.ones((16, c1 - c0), BF16)], axis=0)
        if m[pr] is None:
            m[pr] = cm
            o[pr] = _dot(va, jnp.exp2((s - cm).astype(BF16)))
        else:
            m_new = jnp.maximum(m[pr], cm)
            alpha = jnp.exp2(m[pr] - m_new)
            o[pr] = alpha * o[pr] + _dot(va, jnp.exp2((s - m_new).astype(BF16)))
            m[pr] = m_new
    outs = []
    for pr in range(n_pairs):
        on = o[pr][0:2 * V_HEAD, :] * (1.0 / o[pr][2 * V_HEAD:2 * V_HEAD + 1, :])
        ot = jnp.concatenate([on[0:V_HEAD, 0:tq], on[V_HEAD:2 * V_HEAD, tq:2 * tq]], axis=0)
        outs.append(ot.T.astype(BF16))
    return outs


def _attn_kernel(q_ref, *refs, n_kv, n_pairs, kc):
    q_heads = [q_ref[hd * HEAD_PAD:(hd + 1) * HEAD_PAD, :] for hd in range(2 * n_pairs)]
    outs = _attn_body(q_heads, refs[:n_kv], refs[n_kv:2 * n_kv], kc)
    o_ref = refs[2 * n_kv]
    for pr, o in enumerate(outs):
        o_ref[:, pr * 2 * V_HEAD:(pr + 1) * 2 * V_HEAD] = o


def _attention(q_t, ks, vs_t, *, n_req, t_q, kv_lens, tq, pairs_per_step):
    n_kv = len(ks)
    nq = t_q // tq
    pp = pairs_per_step
    in_specs = [pl.BlockSpec((pp * 2 * HEAD_PAD, tq), lambda b, p, i: (p, b * nq + i))]
    in_specs += [pl.BlockSpec((kl, pp * 2 * HEAD_PAD), lambda b, p, i: (b, p)) for kl in kv_lens]
    in_specs += [pl.BlockSpec((pp * 2 * V_HEAD, kl), lambda b, p, i: (p, b)) for kl in kv_lens]
    return pl.pallas_call(
        functools.partial(_attn_kernel, n_kv=n_kv, n_pairs=pp, kc=KEY_CHUNK),
        grid=(n_req, N_HEADS // 2 // pp, nq),
        in_specs=in_specs,
        out_specs=pl.BlockSpec((tq, pp * 2 * V_HEAD), lambda b, p, i: (b * nq + i, p)),
        out_shape=jax.ShapeDtypeStruct((n_req * t_q, V_W), BF16),
        compiler_params=_cparams(("parallel", "parallel", "parallel")),
        name="attn",
    )(q_t, *ks, *vs_t)


def _mixout_body(x, attn_pairs, zc, zs, ct, st, wo_ref, m_ref, g2_ref, wr_ref):
    fm = _dot(ct, zc) - _dot(st, zs)
    y = _dot(fm.astype(BF16), wo_ref[V_W:V_W + FNET_W, :])
    col = 0
    for a in attn_pairs:
        y = y + _dot(a, wo_ref[col:col + a.shape[1], :])
        col += a.shape[1]
    gate1 = m_ref[0, 2:3, :]
    shift2 = m_ref[0, 3:4, :]
    scale2 = m_ref[0, 4:5, :]
    x1 = x + gate1 * y
    h2 = (_rms(x1, g2_ref[...]) * (1.0 + scale2) + shift2).astype(BF16)
    lg = lax.dot_general(wr_ref[...], h2, _NT, preferred_element_type=F32)
    e = jnp.exp(lg - jnp.max(lg, axis=0, keepdims=True))
    return x1, h2, e / jnp.sum(e, axis=0, keepdims=True)


def _mixout_kernel(x_ref, a_ref, zc_ref, zs_ref, cb_ref, sb_ref, off_ref, wo_ref, m_ref, g2_ref, wr_ref,
                   x1_ref, h2_ref, aff_ref, *, t):
    off = off_ref[pl.ds(pl.program_id(1), 1), :]
    co = off[:, 0:t]
    so = off[:, t:2 * t]
    cb = cb_ref[...]
    sb = sb_ref[...]
    ct = (cb * co - sb * so).astype(BF16)
    st = (sb * co + cb * so).astype(BF16)
    x1, h2, aff = _mixout_body(x_ref[...], [a_ref[...]], zc_ref[...], zs_ref[...], ct, st,
                               wo_ref, m_ref, g2_ref, wr_ref)
    x1_ref[...] = x1
    h2_ref[...] = h2
    aff_ref[...] = aff


def _ctx_front_kernel(x_ref, m_ref, g1_ref, win_ref, qg_ref, wuq_ref, kvg_ref, wk_ref, wv_ref, wcs_ref,
                      tq_ref, tk_ref, ct_ref, st_ref, wo_ref, g2_ref, wr_ref,
                      x1_ref, h2_ref, aff_ref, ckv_ref, kpe_ref, *, kc):
    x = x_ref[...]
    q_heads, k, v_t, zc, zs, ckv, kpe = _premix_body(x, m_ref, g1_ref, win_ref, qg_ref, wuq_ref, kvg_ref,
                                                     wk_ref, wv_ref, wcs_ref, tq_ref, tk_ref, False)
    ckv_ref[...] = ckv
    kpe_ref[...] = kpe
    attn = jnp.concatenate(_attn_body(q_heads, [k], [v_t], kc), axis=1)
    x1, h2, aff = _mixout_body(x, [attn], zc, zs, ct_ref[...], st_ref[...], wo_ref, m_ref, g2_ref, wr_ref)
    x1_ref[...] = x1
    h2_ref[...] = h2
    aff_ref[...] = aff


def _ctx_front(x, mods6, g1, win, qg, wuq, kvg, wk, wv, wcs, tq, tk, ct, st, wo, g2, wr_t, *, n_req, t, mod_row):
    full = lambda a: pl.BlockSpec(a.shape, lambda b: (0,) * a.ndim)
    row = lambda w: pl.BlockSpec((t, w), lambda b: (b, 0))
    return pl.pallas_call(
        functools.partial(_ctx_front_kernel, kc=KEY_CHUNK),
        grid=(n_req,),
        in_specs=[row(D_MODEL), pl.BlockSpec((1, 6, D_MODEL), lambda b: (mod_row, 0, 0)),
                  full(g1), full(win), full(qg), full(wuq), full(kvg), full(wk), full(wv), full(wcs),
                  full(tq), full(tk), full(ct), full(st), full(wo), full(g2), full(wr_t)],
        out_specs=[row(D_MODEL), row(D_MODEL), pl.BlockSpec((None, N_EXPERTS, t), lambda b: (b, 0, 0)),
                   row(KV_LORA), row(QK_ROPE)],
        out_shape=[jax.ShapeDtypeStruct((n_req * t, D_MODEL), F32),
                   jax.ShapeDtypeStruct((n_req * t, D_MODEL), BF16),
                   jax.ShapeDtypeStruct((n_req, N_EXPERTS, t), F32),
                   jax.ShapeDtypeStruct((n_req * t, KV_LORA), F32),
                   jax.ShapeDtypeStruct((n_req * t, QK_ROPE), F32)],
        compiler_params=_cparams(("parallel",)),
        name="ctx_front",
    )(x, mods6, g1, win, qg, wuq, kvg, wk, wv, wcs, tq, tk, ct, st, wo, g2, wr_t)


def _mixout(x, attn, zc, zs, cb, sb, off, wo, mods6, g2, wr_t, *, n_req, t, mod_row):
    tr = TOK_TILE
    nr = t // tr
    full = lambda a: pl.BlockSpec(a.shape, lambda b, i: (0,) * a.ndim)
    return pl.pallas_call(
        functools.partial(_mixout_kernel, t=t),
        grid=(n_req, nr),
        in_specs=[pl.BlockSpec((tr, D_MODEL), lambda b, i: (b * nr + i, 0)),
                  pl.BlockSpec((tr, V_W), lambda b, i: (b * nr + i, 0)),
                  pl.BlockSpec((t, FNET_W), lambda b, i: (b, 0)),
                  pl.BlockSpec((t, FNET_W), lambda b, i: (b, 0)),
                  full(cb), full(sb), full(off),
                  full(wo),
                  pl.BlockSpec((1, 6, D_MODEL), lambda b, i: (mod_row(b), 0, 0)),
                  full(g2), full(wr_t)],
        out_specs=[pl.BlockSpec((tr, D_MODEL), lambda b, i: (b * nr + i, 0)),
                   pl.BlockSpec((tr, D_MODEL), lambda b, i: (b * nr + i, 0)),
                   pl.BlockSpec((None, N_EXPERTS, tr), lambda b, i: (b, 0, i))],
        out_shape=[jax.ShapeDtypeStruct((n_req * t, D_MODEL), F32),
                   jax.ShapeDtypeStruct((n_req * t, D_MODEL), BF16),
                   jax.ShapeDtypeStruct((n_req, N_EXPERTS, t), F32)],
        compiler_params=_cparams(("parallel", "parallel")),
        name="mixout",
    )(x, attn, zc, zs, cb, sb, off, wo, mods6, g2, wr_t)


def _prefix_count(flags, tri):
    n = flags.shape[1]
    carry = None
    outs = []
    ends = []
    for j in range(n // TOK_TILE):
        c = _dot(flags[:, j * TOK_TILE:(j + 1) * TOK_TILE].astype(BF16), tri)
        if carry is not None:
            c = c + carry
        outs.append(c)
        carry = c[:, TOK_TILE - 1:TOK_TILE]
        ends.append(carry)
    return (outs[0] if len(outs) == 1 else jnp.concatenate(outs, axis=1)), ends


def _route_kernel(aff_ref, pos_ref, off_ref, *, cap):
    a = aff_ref[...]
    rows = a.shape[0]
    capf = jnp.float32(cap)
    thr = jnp.zeros((rows, 1), jnp.int32)
    for bit in range(30, -1, -1):
        cand = thr | jnp.int32(1 << bit)
        cand_f = lax.bitcast_convert_type(cand, F32)
        cnt = jnp.sum(jnp.where(a >= cand_f, 1.0, 0.0), axis=1, keepdims=True)
        thr = jnp.where(cnt >= capf, cand, thr)
    thr_f = lax.bitcast_convert_type(thr, F32)
    above_f = lax.bitcast_convert_type(thr + 1, F32)
    gt = jnp.where(a >= above_f, 1.0, 0.0)
    tie = jnp.where(a >= thr_f, 1.0, 0.0) - gt
    need = capf - jnp.sum(gt, axis=1, keepdims=True)
    r_i = lax.broadcasted_iota(jnp.int32, (TOK_TILE, TOK_TILE), 0)
    c_i = lax.broadcasted_iota(jnp.int32, (TOK_TILE, TOK_TILE), 1)
    tri = jnp.where(r_i <= c_i, 1.0, 0.0).astype(BF16)
    tie_before = _prefix_count(tie, tri)[0] - tie
    sel = gt + tie * jnp.where(tie_before < need, 1.0, 0.0)
    count, ends = _prefix_count(sel, tri)
    pos_ref[...] = jnp.where(sel > 0.5, count - 1.0, -1.0)
    lane = lax.broadcasted_iota(jnp.int32, (rows, LANES), 1)
    offs = jnp.zeros((rows, LANES), F32)
    for j, end in enumerate(ends):
        offs = offs + jnp.where(lane == j + 1, end, 0.0)
    off_ref[...] = offs.astype(jnp.int32)


def _route(aff_t, cap):
    return pl.pallas_call(
        functools.partial(_route_kernel, cap=cap),
        out_shape=[jax.ShapeDtypeStruct(aff_t.shape, F32),
                   jax.ShapeDtypeStruct((aff_t.shape[0], LANES), jnp.int32)],
        compiler_params=pltpu.CompilerParams(vmem_limit_bytes=VMEM_LIMIT),
        name="route",
    )(aff_t)


def _gather_kernel(pos_ref, aff_ref, h_ref, xs_ref, g_ref, *, cap):
    pos = pos_ref[0]
    aff = aff_ref[0]
    eb, n = pos.shape
    slot = lax.broadcasted_iota(jnp.int32, (eb, cap, n), 1).astype(F32)
    hit = pos[:, None, :] == slot
    onehot = jnp.where(hit, 1.0, 0.0).reshape(eb * cap, n).astype(BF16)
    xs = _dot(onehot, h_ref[...])
    xs_ref[...] = xs.astype(BF16).reshape(eb, cap, D_MODEL)
    g_ref[...] = jnp.sum(jnp.where(hit, aff[:, None, :], 0.0), axis=2, keepdims=True)


def _gather(pos_t, aff_t, h2, *, n_req, n, cap, eb):
    ne = N_EXPERTS // eb
    pos3 = pos_t.reshape(n_req * ne, eb, n)
    aff3 = aff_t.reshape(n_req * ne, eb, n)
    return pl.pallas_call(
        functools.partial(_gather_kernel, cap=cap),
        grid=(n_req, ne),
        in_specs=[pl.BlockSpec((1, eb, n), lambda b, e: (b * ne + e, 0, 0)),
                  pl.BlockSpec((1, eb, n), lambda b, e: (b * ne + e, 0, 0)),
                  pl.BlockSpec((n, D_MODEL), lambda b, e: (b, 0))],
        out_specs=[pl.BlockSpec((eb, cap, D_MODEL), lambda b, e: (e, b, 0)),
                   pl.BlockSpec((eb, cap, 1), lambda b, e: (e, b, 0))],
        out_shape=[jax.ShapeDtypeStruct((N_EXPERTS, n_req * cap, D_MODEL), BF16),
                   jax.ShapeDtypeStruct((N_EXPERTS, n_req * cap, 1), F32)],
        compiler_params=_cparams(("parallel", "arbitrary")),
        name="gather",
    )(pos3, aff3, h2)


def _window_plan(off_ref, b, j, cap, w):
    w0 = []
    need = jnp.int32(0)
    for e in range(N_EXPERTS):
        idx = (b * N_EXPERTS + e) * OFF_STRIDE + j
        base = (off_ref[idx] >> 4) << 4
        w0.append(base)
        need = jnp.maximum(need, off_ref[idx + 1] - base)
    return w0, (need + (w - 1)) >> (w.bit_length() - 1)


def _window(w0_e, p, cap, w):
    low = w0_e + p * w
    start = pl.multiple_of(jnp.minimum(low, cap - w), 16)
    return low, start


def _gather_win_kernel(off_ref, pos_ref, aff_ref, h_ref, xs_ref, g_ref, *, cap, w):
    b = pl.program_id(0)
    j = pl.program_id(1)

    @pl.when(j == 0)
    def _init():
        xs_ref[...] = jnp.zeros_like(xs_ref)
        g_ref[...] = jnp.zeros_like(g_ref)

    pos = pos_ref[...]
    aff = aff_ref[...]
    h = h_ref[...]
    w0, n_pass = _window_plan(off_ref, b, j, cap, w)
    r = lax.broadcasted_iota(jnp.int32, (w, 1), 0).astype(F32)

    def one_pass(p, carry):
        starts, hots, gates = [], [], []
        for e in range(N_EXPERTS):
            low, start = _window(w0[e], p, cap, w)
            starts.append(start)
            mine = jnp.where(r >= (low - start).astype(F32), 1.0, 0.0)
            hot = jnp.where(pos[e:e + 1, :] - start.astype(F32) == r, mine, 0.0)
            hots.append(hot)
            gates.append(jnp.sum(hot * aff[e:e + 1, :], axis=1, keepdims=True))
        rows = _dot(jnp.concatenate(hots, axis=0).astype(BF16), h).astype(BF16)
        for e in range(N_EXPERTS):
            win = pl.ds(starts[e], w)
            xs_ref[e, win, :] = xs_ref[e, win, :] + rows[e * w:(e + 1) * w, :]
            g_ref[e, win, :] = g_ref[e, win, :] + gates[e]
        return carry

    lax.fori_loop(0, n_pass, one_pass, 0)


def _gather_win(offs, pos_t, aff_t, h2, *, n_req, n, cap):
    nb = n // TOK_TILE
    grid_spec = pltpu.PrefetchScalarGridSpec(
        num_scalar_prefetch=1,
        grid=(n_req, nb),
        in_specs=[pl.BlockSpec((N_EXPERTS, TOK_TILE), lambda b, j, off: (b, j)),
                  pl.BlockSpec((N_EXPERTS, TOK_TILE), lambda b, j, off: (b, j)),
                  pl.BlockSpec((TOK_TILE, D_MODEL), lambda b, j, off: (b * nb + j, 0))],
        out_specs=[pl.BlockSpec((N_EXPERTS, cap, D_MODEL), lambda b, j, off: (0, b, 0)),
                   pl.BlockSpec((N_EXPERTS, cap, 1), lambda b, j, off: (0, b, 0))])
    return pl.pallas_call(
        functools.partial(_gather_win_kernel, cap=cap, w=SLOT_WIN),
        grid_spec=grid_spec,
        out_shape=[jax.ShapeDtypeStruct((N_EXPERTS, n_req * cap, D_MODEL), BF16),
                   jax.ShapeDtypeStruct((N_EXPERTS, n_req * cap, 1), F32)],
        compiler_params=_cparams(("parallel", "arbitrary")),
        name="gather_win",
    )(offs, pos_t, aff_t, h2)


def _ffn_kernel(xc_ref, xl_ref, gc_ref, gl_ref, wg_ref, wu_ref, wd_ref, yc_ref, yl_ref, wgb, wub, wdb):
    wgb[...] = wg_ref[0].astype(BF16)
    wub[...] = wu_ref[0].astype(BF16)
    wdb[...] = wd_ref[0].astype(BF16)
    for x_ref, g_ref, y_ref in ((xc_ref, gc_ref, yc_ref), (xl_ref, gl_ref, yl_ref)):
        for j in range(x_ref.shape[1] // TOK_TILE):
            rows = slice(j * TOK_TILE, (j + 1) * TOK_TILE)
            x = x_ref[0, rows, :]
            gate = _dot(x, wgb[...])
            up = _dot(x, wub[...])
            hid = (gate * jax.nn.sigmoid(gate) * up).astype(BF16)
            ys = _dot(hid, wdb[...]) * g_ref[0, rows, :]
            y_ref[0, rows, :] = ys.astype(BF16)


def _ffn(xc, xl, gc, gl, wg, wu, wd):
    m = xc.shape[1]
    xspec = pl.BlockSpec((1, m, D_MODEL), lambda e: (e, 0, 0))
    gspec = pl.BlockSpec((1, m, 1), lambda e: (e, 0, 0))
    shp = jax.ShapeDtypeStruct((N_EXPERTS, m, D_MODEL), BF16)
    return pl.pallas_call(
        _ffn_kernel,
        grid=(N_EXPERTS,),
        in_specs=[xspec, xspec, gspec, gspec,
                  pl.BlockSpec((1, D_MODEL, D_EXPERT), lambda e: (e, 0, 0)),
                  pl.BlockSpec((1, D_MODEL, D_EXPERT), lambda e: (e, 0, 0)),
                  pl.BlockSpec((1, D_EXPERT, D_MODEL), lambda e: (e, 0, 0))],
        out_specs=[xspec, xspec],
        out_shape=[shp, shp],
        scratch_shapes=[pltpu.VMEM((D_MODEL, D_EXPERT), BF16), pltpu.VMEM((D_MODEL, D_EXPERT), BF16),
                        pltpu.VMEM((D_EXPERT, D_MODEL), BF16)],
        compiler_params=_cparams(("arbitrary",)),
        name="ffn",
    )(xc, xl, gc, gl, wg, wu, wd)


def _combine_kernel(x1_ref, pos_ref, ys_ref, m_ref, fg_ref, o_ref, *, cap):
    pos = pos_ref[...].astype(BF16)
    w = N_EXPERTS * cap
    e_i = lax.broadcasted_iota(jnp.int32, (N_EXPERTS, w), 0)
    j_i = lax.broadcasted_iota(jnp.int32, (N_EXPERTS, w), 1)
    spread = jnp.where((j_i >> (cap.bit_length() - 1)) == e_i, 1.0, 0.0).astype(BF16)
    lane_slot = (lax.broadcasted_iota(jnp.int32, (1, w), 1) & (cap - 1)).astype(F32)
    onehot = jnp.where(_dot(pos, spread) == lane_slot, 1.0, 0.0).astype(BF16)
    acc = _dot(onehot, ys_ref[...].reshape(w, D_MODEL))
    gate2 = m_ref[0, 5:6, :]
    o_ref[...] = _rms(x1_ref[...] + gate2 * acc, fg_ref[...])


def _combine(x1, pos_tok, ys, mods6, fg, *, n_req, n, cap, mod_row):
    tr = TOK_TILE
    nr = n // tr
    return pl.pallas_call(
        functools.partial(_combine_kernel, cap=cap),
        grid=(n_req, nr),
        in_specs=[pl.BlockSpec((tr, D_MODEL), lambda b, i: (b * nr + i, 0)),
                  pl.BlockSpec((None, tr, N_EXPERTS), lambda b, i: (b, i, 0)),
                  pl.BlockSpec((N_EXPERTS, cap, D_MODEL), lambda b, i: (0, b, 0)),
                  pl.BlockSpec((1, 6, D_MODEL), lambda b, i: (mod_row(b), 0, 0)),
                  pl.BlockSpec((1, D_MODEL), lambda b, i: (0, 0))],
        out_specs=pl.BlockSpec((tr, D_MODEL), lambda b, i: (b * nr + i, 0)),
        out_shape=jax.ShapeDtypeStruct((n_req * n, D_MODEL), F32),
        compiler_params=_cparams(("parallel", "parallel")),
        name="combine",
    )(x1, pos_tok, ys, mods6, fg)


def _combine_win_kernel(off_ref, x1_ref, pos_ref, ys_ref, m_ref, fg_ref, o_ref, acc_ref, *, cap, w):
    b = pl.program_id(0)
    j = pl.program_id(1)
    pos = pos_ref[...]
    w0, n_pass = _window_plan(off_ref, b, j, cap, w)
    width = N_EXPERTS * w
    e_i = lax.broadcasted_iota(jnp.int32, (N_EXPERTS, width), 0)
    j_i = lax.broadcasted_iota(jnp.int32, (N_EXPERTS, width), 1)
    spread = jnp.where((j_i >> (w.bit_length() - 1)) == e_i, 1.0, 0.0).astype(BF16)
    lane_slot = (lax.broadcasted_iota(jnp.int32, (1, width), 1) & (w - 1)).astype(F32)
    lane_e = lax.broadcasted_iota(jnp.int32, (1, N_EXPERTS), 1)
    acc_ref[...] = jnp.zeros_like(acc_ref)

    def one_pass(p, carry):
        start_row = jnp.zeros((1, N_EXPERTS), F32)
        first_row = jnp.zeros((1, N_EXPERTS), F32)
        wins = []
        for e in range(N_EXPERTS):
            low, start = _window(w0[e], p, cap, w)
            start_row = jnp.where(lane_e == e, start.astype(F32), start_row)
            first_row = jnp.where(lane_e == e, (low - start).astype(F32), first_row)
            wins.append(ys_ref[e, pl.ds(start, w), :])
        rel = pos - start_row
        rel = jnp.where(rel >= first_row, rel, -1.0).astype(BF16)
        onehot = jnp.where(_dot(rel, spread) == lane_slot, 1.0, 0.0).astype(BF16)
        acc_ref[...] += _dot(onehot, jnp.concatenate(wins, axis=0))
        return carry

    lax.fori_loop(0, n_pass, one_pass, 0)
    gate2 = m_ref[0, 5:6, :]
    o_ref[...] = _rms(x1_ref[...] + gate2 * acc_ref[...], fg_ref[...])


def _combine_win(offs, x1, pos_tok, ys, mods6, fg, *, n_req, n, cap, mod_row):
    tr = TOK_TILE
    nr = n // tr
    grid_spec = pltpu.PrefetchScalarGridSpec(
        num_scalar_prefetch=1,
        grid=(n_req, nr),
        in_specs=[pl.BlockSpec((tr, D_MODEL), lambda b, i, off: (b * nr + i, 0)),
                  pl.BlockSpec((None, tr, N_EXPERTS), lambda b, i, off: (b, i, 0)),
                  pl.BlockSpec((N_EXPERTS, cap, D_MODEL), lambda b, i, off: (0, b, 0)),
                  pl.BlockSpec((1, 6, D_MODEL), lambda b, i, off: (mod_row(b), 0, 0)),
                  pl.BlockSpec((1, D_MODEL), lambda b, i, off: (0, 0))],
        out_specs=pl.BlockSpec((tr, D_MODEL), lambda b, i, off: (b * nr + i, 0)),
        scratch_shapes=[pltpu.VMEM((tr, D_MODEL), F32)])
    return pl.pallas_call(
        functools.partial(_combine_win_kernel, cap=cap, w=SLOT_WIN),
        grid_spec=grid_spec,
        out_shape=jax.ShapeDtypeStruct((n_req * n, D_MODEL), F32),
        compiler_params=_cparams(("parallel", "parallel")),
        name="combine_win",
    )(offs, x1, pos_tok, ys, mods6, fg)


def _rot_half(w):
    half = QK_ROPE // 2
    return jnp.concatenate([-w[..., half:], w[..., :half]], axis=-1)


def _rope_tables(t):
    n_rows = t // GRID_W
    rows = np.repeat(np.arange(n_rows, dtype=np.float64), GRID_W)
    cols = np.tile(np.arange(GRID_W, dtype=np.float64), n_rows)
    n_freq = QK_ROPE // 4
    inv_freq = ROPE_BASE ** (-np.arange(n_freq, dtype=np.float64) / n_freq)
    ang = np.concatenate([rows[:, None] * inv_freq, cols[:, None] * inv_freq], axis=-1)
    cos = np.concatenate([np.cos(ang), np.cos(ang)], axis=-1)
    sin = np.concatenate([np.sin(ang), np.sin(ang)], axis=-1)
    return cos, sin


def _qk_tables(cos, sin):
    t = cos.shape[0]
    scale = (QK_NOPE + QK_ROPE) ** -0.5 * np.log2(np.e)
    pad = np.zeros((t, HEAD_PAD - QK_NOPE - QK_ROPE))
    cosq = np.concatenate([np.full((t, QK_NOPE), scale), cos * scale, pad], axis=1)
    sinq = np.concatenate([np.zeros((t, QK_NOPE)), sin * scale, pad], axis=1)
    tq_t = np.concatenate([cosq, sinq], axis=1).T
    tk = np.concatenate([cos, sin, np.zeros((t, LANES - 2 * QK_ROPE))], axis=1)
    return jnp.asarray(tq_t, F32), jnp.asarray(tk, F32)


def _dft_angles(rows, t):
    k = np.arange(t, dtype=np.int64)
    return ((rows[:, None] * k[None, :]) % t).astype(np.float64) * (2.0 * np.pi / t)


def _dft_tables(t):
    ang = _dft_angles(np.arange(TOK_TILE, dtype=np.int64), t)
    scale = (t * FNET_CH) ** -0.5
    ang_off = _dft_angles(np.arange(t // TOK_TILE, dtype=np.int64) * TOK_TILE, t)
    off = np.concatenate([np.cos(ang_off), np.sin(ang_off)], axis=1)
    return jnp.asarray(np.cos(ang) * scale, F32), jnp.asarray(np.sin(ang) * scale, F32), jnp.asarray(off, F32)


def _block_diag(w):
    g, a, b = w.shape
    eye = jnp.eye(g, dtype=w.dtype)
    return (eye[:, None, :, None] * w[:, :, None, :]).reshape(g * a, g * b)


def kernel(x_prompt, x_sample, cache_ckv, cache_kpe, c, c_ctx, w_mod, b_mod, norm1_g, w_in, q_norm_g, w_uq,
           kv_norm_g, w_ukv, w_fmix, w_out, norm2_g, w_router, w_e_gate, w_e_up, w_e_down, final_g):
    assert w_mod.shape[0] == 1, "single-layer problem"
    n_ctx, t_ctx, _ = x_prompt.shape
    n_lat, t_lat, _ = x_sample.shape
    past = cache_ckv.shape[2]
    ctx_row = n_lat

    w_in0 = w_in[0]
    kpe_cols = w_in0[:, Q_LORA + KV_LORA:Q_LORA + KV_LORA + QK_ROPE]
    win = jnp.concatenate([w_in0[:, :Q_LORA + KV_LORA + QK_ROPE], _rot_half(kpe_cols),
                           jnp.zeros((D_MODEL, 512 - Q_LORA - KV_LORA - 2 * QK_ROPE), F32),
                           w_in0[:, Q_LORA + KV_LORA + QK_ROPE:]], axis=1).astype(BF16)
    wq3 = w_uq[0].reshape(Q_LORA, N_HEADS, QK_NOPE + QK_ROPE)
    qpad = jnp.zeros((Q_LORA, N_HEADS, HEAD_PAD - QK_NOPE - QK_ROPE), F32)
    wuq_main = jnp.concatenate([wq3, qpad], axis=2).reshape(Q_LORA, QK_W)
    wuq_rot = jnp.concatenate([jnp.zeros((Q_LORA, N_HEADS, QK_NOPE), F32), _rot_half(wq3[..., QK_NOPE:]), qpad],
                              axis=2).reshape(Q_LORA, QK_W)
    wuq_lat = jnp.concatenate([wuq_main, wuq_rot], axis=1).T.astype(BF16)
    wuq_ctx = wuq_main.T.astype(BF16)
    wkv3 = w_ukv[0].reshape(KV_LORA, N_HEADS, QK_NOPE + V_HEAD)
    wk_top = jnp.concatenate([wkv3[..., :QK_NOPE], jnp.zeros((KV_LORA, N_HEADS, HEAD_PAD - QK_NOPE), F32)],
                             axis=2).reshape(KV_LORA, QK_W)
    place = jnp.concatenate([jnp.zeros((QK_ROPE, QK_NOPE), F32), jnp.eye(QK_ROPE, dtype=F32),
                             jnp.zeros((QK_ROPE, HEAD_PAD - QK_NOPE - QK_ROPE), F32)], axis=1)
    place = jnp.tile(place, (1, N_HEADS))
    wk = jnp.concatenate([wk_top, place, place, jnp.zeros((LANES - 2 * QK_ROPE, QK_W), F32)], axis=0).astype(BF16)
    wv = wkv3[..., QK_NOPE:].reshape(KV_LORA, V_W).T.astype(BF16)
    wo = w_out[0].astype(BF16)
    wr_t = w_router[0].T.astype(BF16)

    cos, sin = _rope_tables(t_lat)
    tq_lat, tk_lat = _qk_tables(cos, sin)
    tq_ctx, tk_ctx = _qk_tables(np.ones((TOK_TILE, QK_ROPE)), np.zeros((TOK_TILE, QK_ROPE)))
    ch_ang = _dft_angles(np.arange(FNET_CH, dtype=np.int64), FNET_CH)
    dft_ctx = _dft_tables(t_ctx)
    dft_lat = _dft_tables(t_lat)

    c8 = jnp.concatenate([c, c_ctx[None, :], jnp.zeros((8 - n_lat - 1, D_MODEL), F32)], axis=0)
    mods6 = _mods(c8, w_mod[0], b_mod[0][None, :]).reshape(8, 6, D_MODEL)
    cw, sw = _fold(jnp.asarray(np.cos(ch_ang), F32), jnp.asarray(np.sin(ch_ang), F32), w_fmix[0])
    wcs = jnp.concatenate([_block_diag(cw), _block_diag(sw)], axis=1).astype(BF16)

    g1 = norm1_g[0][None, :]
    qg = q_norm_g[0][None, :]
    kvg = kv_norm_g[0][None, :]
    g2 = norm2_g[0][None, :]
    fg = final_g[None, :]

    xp = x_prompt.reshape(n_ctx * t_ctx, D_MODEL)
    xs = x_sample.reshape(n_lat * t_lat, D_MODEL)
    tiles_lat = t_lat // TOK_TILE

    x1c, h2c, affc, ckv_c, kpe_c = _ctx_front(
        xp, mods6, g1, win, qg, wuq_ctx, kvg, wk, wv, wcs, tq_ctx, tk_ctx,
        dft_ctx[0].astype(BF16), dft_ctx[1].astype(BF16), wo, g2, wr_t, n_req=n_ctx, t=t_ctx, mod_row=ctx_row)
    ql, kl, vl, zcl, zsl = _premix(
        xs, mods6, g1, win, qg, wuq_lat, kvg, wk, wv, wcs, tq_lat, tk_lat,
        mod_row=lambda i: i // tiles_lat, tab_row=lambda i: i % tiles_lat, rope=True)
    xk_cache = jnp.concatenate([cache_ckv[:, 0], cache_kpe[:, 0],
                                jnp.zeros((n_lat, past, 2 * LANES - KV_LORA - QK_ROPE), F32)],
                               axis=-1).reshape(n_lat * past, 2 * LANES).astype(BF16)
    kpast, vpast = _cachekv(xk_cache, wk, wv)

    attn_l = _attention(ql, [kpast, kl], [vpast, vl], n_req=n_lat, t_q=t_lat, kv_lens=[past, t_lat], tq=TOK_TILE,
                        pairs_per_step=2)

    x1l, h2l, affl = _mixout(xs, attn_l, zcl, zsl, *dft_lat, wo, mods6, g2, wr_t,
                             n_req=n_lat, t=t_lat, mod_row=lambda b: b)

    cap_c = CAP_FACTOR * t_ctx // N_EXPERTS
    cap_l = CAP_FACTOR * t_lat // N_EXPERTS
    affc2 = affc.reshape(n_ctx * N_EXPERTS, t_ctx)
    affl2 = affl.reshape(n_lat * N_EXPERTS, t_lat)
    posc, _ = _route(affc2, cap_c)
    posl, offl = _route(affl2, cap_l)
    assert t_lat // TOK_TILE + 1 <= OFF_STRIDE and cap_l % SLOT_WIN == 0
    offl = offl[:, :OFF_STRIDE].reshape(-1)
    xsc, gc = _gather(posc, affc2, h2c, n_req=n_ctx, n=t_ctx, cap=cap_c, eb=N_EXPERTS)
    xsl, gl = _gather_win(offl, posl, affl2, h2l, n_req=n_lat, n=t_lat, cap=cap_l)
    ysc, ysl = _ffn(xsc, xsl, gc, gl, w_e_gate[0], w_e_up[0], w_e_down[0])

    posc_tok = posc.reshape(n_ctx, N_EXPERTS, t_ctx).transpose(0, 2, 1)
    posl_tok = posl.reshape(n_lat, N_EXPERTS, t_lat).transpose(0, 2, 1)
    y_prompt = _combine(x1c, posc_tok, ysc, mods6, fg, n_req=n_ctx, n=t_ctx, cap=cap_c, mod_row=lambda b: ctx_row)
    y_sample = _combine_win(offl, x1l, posl_tok, ysl, mods6, fg, n_req=n_lat, n=t_lat, cap=cap_l,
                            mod_row=lambda b: b)

    return (y_prompt.reshape(n_ctx, t_ctx, D_MODEL), y_sample.reshape(n_lat, t_lat, D_MODEL),
            ckv_c.reshape(n_ctx, 1, t_ctx, KV_LORA), kpe_c.reshape(n_ctx, 1, t_ctx, QK_ROPE))
```

```python
import functools

import jax
import jax.numpy as jnp
import numpy as np
from jax import lax
from jax.experimental import pallas as pl
from jax.experimental.pallas import tpu as pltpu

F32 = jnp.float32
BF16 = jnp.bfloat16

D_MODEL = 1024
N_HEADS = 8
QK_NOPE = 64
QK_ROPE = 32
V_HEAD = 64
Q_LORA = 256
KV_LORA = 128
FNET_GROUPS = 8
FNET_CH = 64
FNET_W = FNET_GROUPS * FNET_CH
N_EXPERTS = 16
CAP_FACTOR = 2
D_EXPERT = 512
GRID_W = 64
ROPE_BASE = 10000.0
EPS = 1e-6

LANES = 128
HEAD_PAD = LANES
QK_W = N_HEADS * HEAD_PAD
V_W = N_HEADS * V_HEAD
TOK_TILE = 256
MIX_TILE = 512
KEY_CHUNK = 512
SLOT_WIN = 64
OFF_STRIDE = 16
VMEM_LIMIT = 48 * 1024 * 1024

_NT = (((1,), (1,)), ((), ()))


def _cparams(sem):
    return pltpu.CompilerParams(dimension_semantics=sem, vmem_limit_bytes=VMEM_LIMIT)


def _rms(x, g):
    return x * lax.rsqrt(jnp.mean(x * x, axis=-1, keepdims=True) + EPS) * g


def _dot(a, b):
    return jnp.dot(a, b, preferred_element_type=F32)


def _mods_kernel(c_ref, w_ref, b_ref, o_ref):
    c = c_ref[...]
    s = c * jax.nn.sigmoid(c)
    o_ref[...] = _dot(s.astype(BF16), w_ref[...].astype(BF16)) + b_ref[...]


def _mods(c8, w_mod, b_mod):
    n = w_mod.shape[1]
    tn = 1536
    return pl.pallas_call(
        _mods_kernel,
        grid=(n // tn,),
        in_specs=[pl.BlockSpec((8, D_MODEL), lambda j: (0, 0)),
                  pl.BlockSpec((D_MODEL, tn), lambda j: (0, j)),
                  pl.BlockSpec((1, tn), lambda j: (0, j))],
        out_specs=pl.BlockSpec((8, tn), lambda j: (0, j)),
        out_shape=jax.ShapeDtypeStruct((8, n), F32),
        compiler_params=_cparams(("arbitrary",)),
        name="mods",
    )(c8, w_mod, b_mod)


def _fold_kernel(cc_ref, sc_ref, w_ref, cw_ref, sw_ref):
    for g in range(FNET_GROUPS):
        w = w_ref[g]
        cw_ref[g] = jnp.dot(cc_ref[...], w, preferred_element_type=F32, precision=lax.Precision.HIGHEST)
        sw_ref[g] = jnp.dot(sc_ref[...], w, preferred_element_type=F32, precision=lax.Precision.HIGHEST)


def _fold(cc, sc, w_fmix):
    shp = jax.ShapeDtypeStruct((FNET_GROUPS, FNET_CH, FNET_CH), F32)
    return pl.pallas_call(_fold_kernel, out_shape=(shp, shp), name="fold")(cc, sc, w_fmix)


def _premix_body(x, m_ref, g1_ref, win_ref, qg_ref, wuq_ref, kvg_ref, wk_ref, wv_ref, wcs_ref, tq_ref, tk_ref, rope):
    shift1 = m_ref[0, 0:1, :]
    scale1 = m_ref[0, 1:2, :]
    h = _rms(x, g1_ref[...]) * (1.0 + scale1) + shift1
    proj = _dot(h.astype(BF16), win_ref[...])
    qn = _rms(proj[:, 0:Q_LORA], qg_ref[...]).astype(BF16)
    qq = lax.dot_general(wuq_ref[...], qn, _NT, preferred_element_type=F32)
    cosq = tq_ref[0:LANES, :]
    sinq = tq_ref[LANES:2 * LANES, :]
    q_heads = []
    for hd in range(N_HEADS):
        lo = hd * HEAD_PAD
        qh = qq[lo:lo + HEAD_PAD, :] * cosq
        if rope:
            qh = qh + qq[QK_W + lo:QK_W + lo + HEAD_PAD, :] * sinq
        q_heads.append(qh.astype(BF16))
    ckv = _rms(proj[:, Q_LORA:Q_LORA + KV_LORA], kvg_ref[...])
    kpe2 = proj[:, Q_LORA + KV_LORA:Q_LORA + KV_LORA + LANES] * tk_ref[...]
    xk = jnp.concatenate([ckv, kpe2], axis=1).astype(BF16)
    k = _dot(xk, wk_ref[...]).astype(BF16)
    v_t = lax.dot_general(wv_ref[...], xk[:, 0:KV_LORA], _NT, preferred_element_type=F32).astype(BF16)
    z = _dot(proj[:, 512:1024].astype(BF16), wcs_ref[...])
    zc = z[:, 0:FNET_W].astype(BF16)
    zs = z[:, FNET_W:2 * FNET_W].astype(BF16)
    kpe = proj[:, Q_LORA + KV_LORA:Q_LORA + KV_LORA + QK_ROPE]
    return q_heads, k, v_t, zc, zs, ckv, kpe


def _premix_kernel(x_ref, m_ref, g1_ref, win_ref, qg_ref, wuq_ref, kvg_ref, wk_ref, wv_ref, wcs_ref,
                   tq_ref, tk_ref, q_ref, k_ref, v_ref, zc_ref, zs_ref, *, rope):
    q_heads, k, v_t, zc, zs, _, _ = _premix_body(x_ref[...], m_ref, g1_ref, win_ref, qg_ref, wuq_ref, kvg_ref,
                                                 wk_ref, wv_ref, wcs_ref, tq_ref, tk_ref, rope)
    for hd, qh in enumerate(q_heads):
        q_ref[hd * HEAD_PAD:(hd + 1) * HEAD_PAD, :] = qh
    k_ref[...] = k
    v_ref[...] = v_t
    zc_ref[...] = zc
    zs_ref[...] = zs


def _premix(x, mods6, g1, win, qg, wuq, kvg, wk, wv, wcs, tq, tk, *, mod_row, tab_row, rope):
    n = x.shape[0]
    tm = TOK_TILE
    full = lambda a: pl.BlockSpec(a.shape, lambda i: (0,) * a.ndim)
    out_shape = [jax.ShapeDtypeStruct((QK_W, n), BF16), jax.ShapeDtypeStruct((n, QK_W), BF16),
                 jax.ShapeDtypeStruct((V_W, n), BF16), jax.ShapeDtypeStruct((n, FNET_W), BF16),
                 jax.ShapeDtypeStruct((n, FNET_W), BF16)]
    out_specs = [pl.BlockSpec((QK_W, tm), lambda i: (0, i)), pl.BlockSpec((tm, QK_W), lambda i: (i, 0)),
                 pl.BlockSpec((V_W, tm), lambda i: (0, i)), pl.BlockSpec((tm, FNET_W), lambda i: (i, 0)),
                 pl.BlockSpec((tm, FNET_W), lambda i: (i, 0))]
    return pl.pallas_call(
        functools.partial(_premix_kernel, rope=rope),
        grid=(n // tm,),
        in_specs=[pl.BlockSpec((tm, D_MODEL), lambda i: (i, 0)),
                  pl.BlockSpec((1, 6, D_MODEL), lambda i: (mod_row(i), 0, 0)),
                  full(g1), full(win), full(qg), full(wuq), full(kvg), full(wk), full(wv), full(wcs),
                  pl.BlockSpec((2 * LANES, tm), lambda i: (0, tab_row(i))),
                  pl.BlockSpec((tm, LANES), lambda i: (tab_row(i), 0))],
        out_specs=out_specs,
        out_shape=out_shape,
        compiler_params=_cparams(("parallel",)),
        name="premix",
    )(x, mods6, g1, win, qg, wuq, kvg, wk, wv, wcs, tq, tk)


def _cachekv_kernel(xk_ref, wk_ref, wv_ref, k_ref, v_ref):
    xk = xk_ref[...]
    k_ref[...] = _dot(xk, wk_ref[...]).astype(BF16)
    v_ref[...] = lax.dot_general(wv_ref[...], xk[:, 0:KV_LORA], _NT, preferred_element_type=F32).astype(BF16)


def _cachekv(xk, wk, wv):
    n = xk.shape[0]
    tm = 512
    full = lambda a: pl.BlockSpec(a.shape, lambda i: (0,) * a.ndim)
    return pl.pallas_call(
        _cachekv_kernel,
        grid=(n // tm,),
        in_specs=[pl.BlockSpec((tm, 2 * LANES), lambda i: (i, 0)), full(wk), full(wv)],
        out_specs=[pl.BlockSpec((tm, QK_W), lambda i: (i, 0)), pl.BlockSpec((V_W, tm), lambda i: (0, i))],
        out_shape=[jax.ShapeDtypeStruct((n, QK_W), BF16), jax.ShapeDtypeStruct((V_W, n), BF16)],
        compiler_params=_cparams(("parallel",)),
        name="cachekv",
    )(xk, wk, wv)


def _attn_body(q_heads, k_refs, v_refs, kc):
    tq = q_heads[0].shape[1]
    zero = jnp.zeros((HEAD_PAD, tq), BF16)
    n_pairs = len(q_heads) // 2
    qbd = [jnp.concatenate([jnp.concatenate([q_heads[2 * pr], zero], axis=1),
                            jnp.concatenate([zero, q_heads[2 * pr + 1]], axis=1)], axis=0) for pr in range(n_pairs)]
    chunks = [(k_ref, v_ref, c0, min(c0 + kc, k_ref.shape[0]))
              for k_ref, v_ref in zip(k_refs, v_refs) for c0 in range(0, k_ref.shape[0], kc)]
    work = [(pr, ch) for pr in range(n_pairs) for ch in chunks]

    def score(item):
        pr, (k_ref, _, c0, c1) = item
        return _dot(k_ref[c0:c1, pr * 2 * HEAD_PAD:(pr + 1) * 2 * HEAD_PAD], qbd[pr])

    m = [None] * n_pairs
    o = [None] * n_pairs
    s_next = score(work[0])
    for wi, (pr, (k_ref, v_ref, c0, c1)) in enumerate(work):
        s = s_next
        if wi + 1 < len(work):
            s_next = score(work[wi + 1])
        cm = jnp.max(s, axis=0, keepdims=True)
        vlo = pr * 2 * V_HEAD
        va = jnp.concatenate([v_ref[vlo:vlo + 2 * V_HEAD, c0:c1], jnp.ones((16, c1 - c0), BF16)], axis=0)
        if m[pr] is None:
            m[pr] = cm
            o[pr] = _dot(va, jnp.exp2((s - cm).astype(BF16)))
        else:
            m_new = jnp.maximum(m[pr], cm)
            alpha = jnp.exp2(m[pr] - m_new)
            o[pr] = alpha * o[pr] + _dot(va, jnp.exp2((s - m_new).astype(BF16)))
            m[pr] = m_new
    outs = []
    for pr in range(n_pairs):
        on = o[pr][0:2 * V_HEAD, :] * (1.0 / o[pr][2 * V_HEAD:2 * V_HEAD + 1, :])
        ot = jnp.concatenate([on[0:V_HEAD, 0:tq], on[V_HEAD:2 * V_HEAD, tq:2 * tq]], axis=0)
        outs.append(ot.T.astype(BF16))
    return outs


def _attn_kernel(q_ref, *refs, n_kv, n_pairs, kc):
    q_heads = [q_ref[hd * HEAD_PAD:(hd + 1) * HEAD_PAD, :] for hd in range(2 * n_pairs)]
    outs = _attn_body(q_heads, refs[:n_kv], refs[n_kv:2 * n_kv], kc)
    o_ref = refs[2 * n_kv]
    for pr, o in enumerate(outs):
        o_ref[:, pr * 2 * V_HEAD:(pr + 1) * 2 * V_HEAD] = o


def _attention(q_t, ks, vs_t, *, n_req, t_q, kv_lens, tq, pairs_per_step):
    n_kv = len(ks)
    nq = t_q // tq
    pp = pairs_per_step
    in_specs = [pl.BlockSpec((pp * 2 * HEAD_PAD, tq), lambda b, p, i: (p, b * nq + i))]
    in_specs += [pl.BlockSpec((kl, pp * 2 * HEAD_PAD), lambda b, p, i: (b, p)) for kl in kv_lens]
    in_specs += [pl.BlockSpec((pp * 2 * V_HEAD, kl), lambda b, p, i: (p, b)) for kl in kv_lens]
    return pl.pallas_call(
        functools.partial(_attn_kernel, n_kv=n_kv, n_pairs=pp, kc=KEY_CHUNK),
        grid=(n_req, N_HEADS // 2 // pp, nq),
        in_specs=in_specs,
        out_specs=pl.BlockSpec((tq, pp * 2 * V_HEAD), lambda b, p, i: (b * nq + i, p)),
        out_shape=jax.ShapeDtypeStruct((n_req * t_q, V_W), BF16),
        compiler_params=_cparams(("parallel", "parallel", "parallel")),
        name="attn",
    )(q_t, *ks, *vs_t)


def _mixout_body(x, attn_pairs, fm, wo_ref, m_ref, g2_ref, wr_ref):
    y = _dot(fm, wo_ref[V_W:V_W + FNET_W, :])
    col = 0
    for a in attn_pairs:
        y = y + _dot(a, wo_ref[col:col + a.shape[1], :])
        col += a.shape[1]
    gate1 = m_ref[0, 2:3, :]
    shift2 = m_ref[0, 3:4, :]
    scale2 = m_ref[0, 4:5, :]
    x1 = x + gate1 * y
    h2 = (_rms(x1, g2_ref[...]) * (1.0 + scale2) + shift2).astype(BF16)
    lg = lax.dot_general(wr_ref[...], h2, _NT, preferred_element_type=F32)
    e = jnp.exp(lg - jnp.max(lg, axis=0, keepdims=True))
    return x1, h2, e / jnp.sum(e, axis=0, keepdims=True)


def _mixout_kernel(x_ref, a_ref, zc_ref, zs_ref, cb_ref, sb_ref, off_ref, wo_ref, m_ref, g2_ref, wr_ref,
                   x1_ref, h2_ref, aff_ref, zp_ref, zm_ref, *, t):
    half = t // 2
    i = pl.program_id(1)

    @pl.when(i == 0)
    def _fold_halves():
        for src, col in ((zc_ref, 0), (zs_ref, FNET_W)):
            lo = src[0:half, :]
            hi = src[half:t, :]
            zp_ref[:, col:col + FNET_W] = lo + hi
            zm_ref[:, col:col + FNET_W] = lo - hi

    off = off_ref[pl.ds(i, 1), :]
    co = off[:, 0:half]
    so = off[:, half:t]
    cb = cb_ref[...]
    sb = sb_ref[...]
    ct = (cb * co - sb * so).astype(BF16)
    st = (sb * co + cb * so).astype(BF16)
    tr = cb.shape[0]
    h = tr // 2
    even = _dot(ct[0:h, :], zp_ref[:, 0:FNET_W]) - _dot(st[0:h, :], zp_ref[:, FNET_W:2 * FNET_W])
    odd = _dot(ct[h:, :], zm_ref[:, 0:FNET_W]) - _dot(st[h:, :], zm_ref[:, FNET_W:2 * FNET_W])
    k_i = lax.broadcasted_iota(jnp.int32, (tr, tr), 0)
    j_i = lax.broadcasted_iota(jnp.int32, (tr, tr), 1)
    perm = jnp.where(j_i == (k_i >> 1) + h * (k_i & 1), 1.0, 0.0).astype(BF16)
    fm = _dot(perm, jnp.concatenate([even, odd], axis=0).astype(BF16)).astype(BF16)
    x1, h2, aff = _mixout_body(x_ref[...], [a_ref[...]], fm, wo_ref, m_ref, g2_ref, wr_ref)
    x1_ref[...] = x1
    h2_ref[...] = h2
    aff_ref[...] = aff


def _ctx_front_kernel(x_ref, m_ref, g1_ref, win_ref, qg_ref, wuq_ref, kvg_ref, wk_ref, wv_ref, wcs_ref,
                      tq_ref, tk_ref, ct_ref, st_ref, wo_ref, g2_ref, wr_ref,
                      x1_ref, h2_ref, aff_ref, ckv_ref, kpe_ref, *, kc):
    x = x_ref[...]
    q_heads, k, v_t, zc, zs, ckv, kpe = _premix_body(x, m_ref, g1_ref, win_ref, qg_ref, wuq_ref, kvg_ref,
                                                     wk_ref, wv_ref, wcs_ref, tq_ref, tk_ref, False)
    ckv_ref[...] = ckv
    kpe_ref[...] = kpe
    attn = jnp.concatenate(_attn_body(q_heads, [k], [v_t], kc), axis=1)
    fm = (_dot(ct_ref[...], zc) - _dot(st_ref[...], zs)).astype(BF16)
    x1, h2, aff = _mixout_body(x, [attn], fm, wo_ref, m_ref, g2_ref, wr_ref)
    x1_ref[...] = x1
    h2_ref[...] = h2
    aff_ref[...] = aff


def _ctx_front(x, mods6, g1, win, qg, wuq, kvg, wk, wv, wcs, tq, tk, ct, st, wo, g2, wr_t, *, n_req, t, mod_row):
    full = lambda a: pl.BlockSpec(a.shape, lambda b: (0,) * a.ndim)
    row = lambda w: pl.BlockSpec((t, w), lambda b: (b, 0))
    return pl.pallas_call(
        functools.partial(_ctx_front_kernel, kc=KEY_CHUNK),
        grid=(n_req,),
        in_specs=[row(D_MODEL), pl.BlockSpec((1, 6, D_MODEL), lambda b: (mod_row, 0, 0)),
                  full(g1), full(win), full(qg), full(wuq), full(kvg), full(wk), full(wv), full(wcs),
                  full(tq), full(tk), full(ct), full(st), full(wo), full(g2), full(wr_t)],
        out_specs=[row(D_MODEL), row(D_MODEL), pl.BlockSpec((None, N_EXPERTS, t), lambda b: (b, 0, 0)),
                   row(KV_LORA), row(QK_ROPE)],
        out_shape=[jax.ShapeDtypeStruct((n_req * t, D_MODEL), F32),
                   jax.ShapeDtypeStruct((n_req * t, D_MODEL), BF16),
                   jax.ShapeDtypeStruct((n_req, N_EXPERTS, t), F32),
                   jax.ShapeDtypeStruct((n_req * t, KV_LORA), F32),
                   jax.ShapeDtypeStruct((n_req * t, QK_ROPE), F32)],
        compiler_params=_cparams(("parallel",)),
        name="ctx_front",
    )(x, mods6, g1, win, qg, wuq, kvg, wk, wv, wcs, tq, tk, ct, st, wo, g2, wr_t)


def _mixout(x, attn, zc, zs, cb, sb, off, wo, mods6, g2, wr_t, *, n_req, t, mod_row):
    tr = MIX_TILE
    nr = t // tr
    full = lambda a: pl.BlockSpec(a.shape, lambda b, i: (0,) * a.ndim)
    return pl.pallas_call(
        functools.partial(_mixout_kernel, t=t),
        grid=(n_req, nr),
        in_specs=[pl.BlockSpec((tr, D_MODEL), lambda b, i: (b * nr + i, 0)),
                  pl.BlockSpec((tr, V_W), lambda b, i: (b * nr + i, 0)),
                  pl.BlockSpec((t, FNET_W), lambda b, i: (b, 0)),
                  pl.BlockSpec((t, FNET_W), lambda b, i: (b, 0)),
                  full(cb), full(sb), full(off),
                  full(wo),
                  pl.BlockSpec((1, 6, D_MODEL), lambda b, i: (mod_row(b), 0, 0)),
                  full(g2), full(wr_t)],
        out_specs=[pl.BlockSpec((tr, D_MODEL), lambda b, i: (b * nr + i, 0)),
                   pl.BlockSpec((tr, D_MODEL), lambda b, i: (b * nr + i, 0)),
                   pl.BlockSpec((None, N_EXPERTS, tr), lambda b, i: (b, 0, i))],
        out_shape=[jax.ShapeDtypeStruct((n_req * t, D_MODEL), F32),
                   jax.ShapeDtypeStruct((n_req * t, D_MODEL), BF16),
                   jax.ShapeDtypeStruct((n_req, N_EXPERTS, t), F32)],
        scratch_shapes=[pltpu.VMEM((t // 2, 2 * FNET_W), BF16), pltpu.VMEM((t // 2, 2 * FNET_W), BF16)],
        compiler_params=_cparams(("parallel", "arbitrary")),
        name="mixout",
    )(x, attn, zc, zs, cb, sb, off, wo, mods6, g2, wr_t)


def _prefix_count(flags, tri):
    n = flags.shape[1]
    carry = None
    outs = []
    ends = []
    for j in range(n // TOK_TILE):
        c = _dot(flags[:, j * TOK_TILE:(j + 1) * TOK_TILE].astype(BF16), tri)
        if carry is not None:
            c = c + carry
        outs.append(c)
        carry = c[:, TOK_TILE - 1:TOK_TILE]
        ends.append(carry)
    return (outs[0] if len(outs) == 1 else jnp.concatenate(outs, axis=1)), ends


def _route_kernel(aff_ref, pos_ref, off_ref, *, cap):
    a = aff_ref[...]
    rows = a.shape[0]
    capf = jnp.float32(cap)
    thr = jnp.zeros((rows, 1), jnp.int32)
    for bit in range(30, -1, -1):
        cand = thr | jnp.int32(1 << bit)
        cand_f = lax.bitcast_convert_type(cand, F32)
        cnt = jnp.sum(jnp.where(a >= cand_f, 1.0, 0.0), axis=1, keepdims=True)
        thr = jnp.where(cnt >= capf, cand, thr)
    thr_f = lax.bitcast_convert_type(thr, F32)
    above_f = lax.bitcast_convert_type(thr + 1, F32)
    gt = jnp.where(a >= above_f, 1.0, 0.0)
    tie = jnp.where(a >= thr_f, 1.0, 0.0) - gt
    need = capf - jnp.sum(gt, axis=1, keepdims=True)
    r_i = lax.broadcasted_iota(jnp.int32, (TOK_TILE, TOK_TILE), 0)
    c_i = lax.broadcasted_iota(jnp.int32, (TOK_TILE, TOK_TILE), 1)
    tri = jnp.where(r_i <= c_i, 1.0, 0.0).astype(BF16)
    tie_before = _prefix_count(tie, tri)[0] - tie
    sel = gt + tie * jnp.where(tie_before < need, 1.0, 0.0)
    count, ends = _prefix_count(sel, tri)
    pos_ref[...] = jnp.where(sel > 0.5, count - 1.0, -1.0)
    lane = lax.broadcasted_iota(jnp.int32, (rows, LANES), 1)
    offs = jnp.zeros((rows, LANES), F32)
    for j, end in enumerate(ends):
        offs = offs + jnp.where(lane == j + 1, end, 0.0)
    off_ref[...] = offs.astype(jnp.int32)


def _route(aff_t, cap):
    return pl.pallas_call(
        functools.partial(_route_kernel, cap=cap),
        out_shape=[jax.ShapeDtypeStruct(aff_t.shape, F32),
                   jax.ShapeDtypeStruct((aff_t.shape[0], LANES), jnp.int32)],
        compiler_params=pltpu.CompilerParams(vmem_limit_bytes=VMEM_LIMIT),
        name="route",
    )(aff_t)


def _gather_kernel(pos_ref, aff_ref, h_ref, xs_ref, g_ref, *, cap):
    pos = pos_ref[0]
    aff = aff_ref[0]
    eb, n = pos.shape
    slot = lax.broadcasted_iota(jnp.int32, (eb, cap, n), 1).astype(F32)
    hit = pos[:, None, :] == slot
    onehot = jnp.where(hit, 1.0, 0.0).reshape(eb * cap, n).astype(BF16)
    xs = _dot(onehot, h_ref[...])
    xs_ref[...] = xs.astype(BF16).reshape(eb, cap, D_MODEL)
    g_ref[...] = jnp.sum(jnp.where(hit, aff[:, None, :], 0.0), axis=2, keepdims=True)


def _gather(pos_t, aff_t, h2, *, n_req, n, cap, eb):
    ne = N_EXPERTS // eb
    pos3 = pos_t.reshape(n_req * ne, eb, n)
    aff3 = aff_t.reshape(n_req * ne, eb, n)
    return pl.pallas_call(
        functools.partial(_gather_kernel, cap=cap),
        grid=(n_req, ne),
        in_specs=[pl.BlockSpec((1, eb, n), lambda b, e: (b * ne + e, 0, 0)),
                  pl.BlockSpec((1, eb, n), lambda b, e: (b * ne + e, 0, 0)),
                  pl.BlockSpec((n, D_MODEL), lambda b, e: (b, 0))],
        out_specs=[pl.BlockSpec((eb, cap, D_MODEL), lambda b, e: (e, b, 0)),
                   pl.BlockSpec((eb, cap, 1), lambda b, e: (e, b, 0))],
        out_shape=[jax.ShapeDtypeStruct((N_EXPERTS, n_req * cap, D_MODEL), BF16),
                   jax.ShapeDtypeStruct((N_EXPERTS, n_req * cap, 1), F32)],
        compiler_params=_cparams(("parallel", "arbitrary")),
        name="gather",
    )(pos3, aff3, h2)


def _window_plan(off_ref, b, j, cap, w):
    w0 = []
    need = jnp.int32(0)
    for e in range(N_EXPERTS):
        idx = (b * N_EXPERTS + e) * OFF_STRIDE + j
        base = (off_ref[idx] >> 4) << 4
        w0.append(base)
        need = jnp.maximum(need, off_ref[idx + 1] - base)
    return w0, (need + (w - 1)) >> (w.bit_length() - 1)


def _window(w0_e, p, cap, w):
    low = w0_e + p * w
    start = pl.multiple_of(jnp.minimum(low, cap - w), 16)
    return low, start


def _gather_win_kernel(off_ref, pos_ref, aff_ref, h_ref, xs_ref, g_ref, *, cap, w):
    b = pl.program_id(0)
    j = pl.program_id(1)

    @pl.when(j == 0)
    def _init():
        xs_ref[...] = jnp.zeros_like(xs_ref)
        g_ref[...] = jnp.zeros_like(g_ref)

    pos = pos_ref[...]
    aff = aff_ref[...]
    h = h_ref[...]
    w0, n_pass = _window_plan(off_ref, b, j, cap, w)
    r = lax.broadcasted_iota(jnp.int32, (w, 1), 0).astype(F32)

    def one_pass(p, carry):
        starts, hots, gates = [], [], []
        for e in range(N_EXPERTS):
            low, start = _window(w0[e], p, cap, w)
            starts.append(start)
            mine = jnp.where(r >= (low - start).astype(F32), 1.0, 0.0)
            hot = jnp.where(pos[e:e + 1, :] - start.astype(F32) == r, mine, 0.0)
            hots.append(hot)
            gates.append(jnp.sum(hot * aff[e:e + 1, :], axis=1, keepdims=True))
        rows = _dot(jnp.concatenate(hots, axis=0).astype(BF16), h).astype(BF16)
        for e in range(N_EXPERTS):
            win = pl.ds(starts[e], w)
            xs_ref[e, win, :] = xs_ref[e, win, :] + rows[e * w:(e + 1) * w, :]
            g_ref[e, win, :] = g_ref[e, win, :] + gates[e]
        return carry

    lax.fori_loop(0, n_pass, one_pass, 0)


def _gather_win(offs, pos_t, aff_t, h2, *, n_req, n, cap):
    nb = n // TOK_TILE
    grid_spec = pltpu.PrefetchScalarGridSpec(
        num_scalar_prefetch=1,
        grid=(n_req, nb),
        in_specs=[pl.BlockSpec((N_EXPERTS, TOK_TILE), lambda b, j, off: (b, j)),
                  pl.BlockSpec((N_EXPERTS, TOK_TILE), lambda b, j, off: (b, j)),
                  pl.BlockSpec((TOK_TILE, D_MODEL), lambda b, j, off: (b * nb + j, 0))],
        out_specs=[pl.BlockSpec((N_EXPERTS, cap, D_MODEL), lambda b, j, off: (0, b, 0)),
                   pl.BlockSpec((N_EXPERTS, cap, 1), lambda b, j, off: (0, b, 0))])
    return pl.pallas_call(
        functools.partial(_gather_win_kernel, cap=cap, w=SLOT_WIN),
        grid_spec=grid_spec,
        out_shape=[jax.ShapeDtypeStruct((N_EXPERTS, n_req * cap, D_MODEL), BF16),
                   jax.ShapeDtypeStruct((N_EXPERTS, n_req * cap, 1), F32)],
        compiler_params=_cparams(("parallel", "arbitrary")),
        name="gather_win",
    )(offs, pos_t, aff_t, h2)


def _ffn_kernel(xc_ref, xl_ref, gc_ref, gl_ref, wg_ref, wu_ref, wd_ref, yc_ref, yl_ref, wgb, wub, wdb):
    wgb[...] = wg_ref[0].astype(BF16)
    wub[...] = wu_ref[0].astype(BF16)
    wdb[...] = wd_ref[0].astype(BF16)
    for x_ref, g_ref, y_ref in ((xc_ref, gc_ref, yc_ref), (xl_ref, gl_ref, yl_ref)):
        for j in range(x_ref.shape[1] // TOK_TILE):
            rows = slice(j * TOK_TILE, (j + 1) * TOK_TILE)
            x = x_ref[0, rows, :]
            gate = _dot(x, wgb[...])
            up = _dot(x, wub[...])
            hid = (gate * jax.nn.sigmoid(gate) * up).astype(BF16)
            ys = _dot(hid, wdb[...]) * g_ref[0, rows, :]
            y_ref[0, rows, :] = ys.astype(BF16)


def _ffn(xc, xl, gc, gl, wg, wu, wd):
    m = xc.shape[1]
    xspec = pl.BlockSpec((1, m, D_MODEL), lambda e: (e, 0, 0))
    gspec = pl.BlockSpec((1, m, 1), lambda e: (e, 0, 0))
    shp = jax.ShapeDtypeStruct((N_EXPERTS, m, D_MODEL), BF16)
    return pl.pallas_call(
        _ffn_kernel,
        grid=(N_EXPERTS,),
        in_specs=[xspec, xspec, gspec, gspec,
                  pl.BlockSpec((1, D_MODEL, D_EXPERT), lambda e: (e, 0, 0)),
                  pl.BlockSpec((1, D_MODEL, D_EXPERT), lambda e: (e, 0, 0)),
                  pl.BlockSpec((1, D_EXPERT, D_MODEL), lambda e: (e, 0, 0))],
        out_specs=[xspec, xspec],
        out_shape=[shp, shp],
        scratch_shapes=[pltpu.VMEM((D_MODEL, D_EXPERT), BF16), pltpu.VMEM((D_MODEL, D_EXPERT), BF16),
                        pltpu.VMEM((D_EXPERT, D_MODEL), BF16)],
        compiler_params=_cparams(("arbitrary",)),
        name="ffn",
    )(xc, xl, gc, gl, wg, wu, wd)


def _combine_kernel(x1_ref, pos_ref, ys_ref, m_ref, fg_ref, o_ref, *, cap):
    pos = pos_ref[...].astype(BF16)
    w = N_EXPERTS * cap
    e_i = lax.broadcasted_iota(jnp.int32, (N_EXPERTS, w), 0)
    j_i = lax.broadcasted_iota(jnp.int32, (N_EXPERTS, w), 1)
    spread = jnp.where((j_i >> (cap.bit_length() - 1)) == e_i, 1.0, 0.0).astype(BF16)
    lane_slot = (lax.broadcasted_iota(jnp.int32, (1, w), 1) & (cap - 1)).astype(F32)
    onehot = jnp.where(_dot(pos, spread) == lane_slot, 1.0, 0.0).astype(BF16)
    acc = _dot(onehot, ys_ref[...].reshape(w, D_MODEL))
    gate2 = m_ref[0, 5:6, :]
    o_ref[...] = _rms(x1_ref[...] + gate2 * acc, fg_ref[...])


def _combine(x1, pos_tok, ys, mods6, fg, *, n_req, n, cap, mod_row):
    tr = TOK_TILE
    nr = n // tr
    return pl.pallas_call(
        functools.partial(_combine_kernel, cap=cap),
        grid=(n_req, nr),
        in_specs=[pl.BlockSpec((tr, D_MODEL), lambda b, i: (b * nr + i, 0)),
                  pl.BlockSpec((None, tr, N_EXPERTS), lambda b, i: (b, i, 0)),
                  pl.BlockSpec((N_EXPERTS, cap, D_MODEL), lambda b, i: (0, b, 0)),
                  pl.BlockSpec((1, 6, D_MODEL), lambda b, i: (mod_row(b), 0, 0)),
                  pl.BlockSpec((1, D_MODEL), lambda b, i: (0, 0))],
        out_specs=pl.BlockSpec((tr, D_MODEL), lambda b, i: (b * nr + i, 0)),
        out_shape=jax.ShapeDtypeStruct((n_req * n, D_MODEL), F32),
        compiler_params=_cparams(("parallel", "parallel")),
        name="combine",
    )(x1, pos_tok, ys, mods6, fg)


def _combine_win_kernel(off_ref, x1_ref, pos_ref, ys_ref, m_ref, fg_ref, o_ref, acc_ref, *, cap, w):
    b = pl.program_id(0)
    j = pl.program_id(1)
    pos = pos_ref[...]
    w0, n_pass = _window_plan(off_ref, b, j, cap, w)
    width = N_EXPERTS * w
    e_i = lax.broadcasted_iota(jnp.int32, (N_EXPERTS, width), 0)
    j_i = lax.broadcasted_iota(jnp.int32, (N_EXPERTS, width), 1)
    spread = jnp.where((j_i >> (w.bit_length() - 1)) == e_i, 1.0, 0.0).astype(BF16)
    lane_slot = (lax.broadcasted_iota(jnp.int32, (1, width), 1) & (w - 1)).astype(F32)
    lane_e = lax.broadcasted_iota(jnp.int32, (1, N_EXPERTS), 1)
    acc_ref[...] = jnp.zeros_like(acc_ref)

    def one_pass(p, carry):
        start_row = jnp.zeros((1, N_EXPERTS), F32)
        first_row = jnp.zeros((1, N_EXPERTS), F32)
        wins = []
        for e in range(N_EXPERTS):
            low, start = _window(w0[e], p, cap, w)
            start_row = jnp.where(lane_e == e, start.astype(F32), start_row)
            first_row = jnp.where(lane_e == e, (low - start).astype(F32), first_row)
            wins.append(ys_ref[e, pl.ds(start, w), :])
        rel = pos - start_row
        rel = jnp.where(rel >= first_row, rel, -1.0).astype(BF16)
        onehot = jnp.where(_dot(rel, spread) == lane_slot, 1.0, 0.0).astype(BF16)
        acc_ref[...] += _dot(onehot, jnp.concatenate(wins, axis=0))
        return carry

    lax.fori_loop(0, n_pass, one_pass, 0)
    gate2 = m_ref[0, 5:6, :]
    o_ref[...] = _rms(x1_ref[...] + gate2 * acc_ref[...], fg_ref[...])


def _combine_win(offs, x1, pos_tok, ys, mods6, fg, *, n_req, n, cap, mod_row):
    tr = TOK_TILE
    nr = n // tr
    grid_spec = pltpu.PrefetchScalarGridSpec(
        num_scalar_prefetch=1,
        grid=(n_req, nr),
        in_specs=[pl.BlockSpec((tr, D_MODEL), lambda b, i, off: (b * nr + i, 0)),
                  pl.BlockSpec((None, tr, N_EXPERTS), lambda b, i, off: (b, i, 0)),
                  pl.BlockSpec((N_EXPERTS, cap, D_MODEL), lambda b, i, off: (0, b, 0)),
                  pl.BlockSpec((1, 6, D_MODEL), lambda b, i, off: (mod_row(b), 0, 0)),
                  pl.BlockSpec((1, D_MODEL), lambda b, i, off: (0, 0))],
        out_specs=pl.BlockSpec((tr, D_MODEL), lambda b, i, off: (b * nr + i, 0)),
        scratch_shapes=[pltpu.VMEM((tr, D_MODEL), F32)])
    return pl.pallas_call(
        functools.partial(_combine_win_kernel, cap=cap, w=SLOT_WIN),
        grid_spec=grid_spec,
        out_shape=jax.ShapeDtypeStruct((n_req * n, D_MODEL), F32),
        compiler_params=_cparams(("parallel", "parallel")),
        name="combine_win",
    )(offs, x1, pos_tok, ys, mods6, fg)


def _rot_half(w):
    half = QK_ROPE // 2
    return jnp.concatenate([-w[..., half:], w[..., :half]], axis=-1)


def _rope_tables(t):
    n_rows = t // GRID_W
    rows = np.repeat(np.arange(n_rows, dtype=np.float64), GRID_W)
    cols = np.tile(np.arange(GRID_W, dtype=np.float64), n_rows)
    n_freq = QK_ROPE // 4
    inv_freq = ROPE_BASE ** (-np.arange(n_freq, dtype=np.float64) / n_freq)
    ang = np.concatenate([rows[:, None] * inv_freq, cols[:, None] * inv_freq], axis=-1)
    cos = np.concatenate([np.cos(ang), np.cos(ang)], axis=-1)
    sin = np.concatenate([np.sin(ang), np.sin(ang)], axis=-1)
    return cos, sin


def _qk_tables(cos, sin):
    t = cos.shape[0]
    scale = (QK_NOPE + QK_ROPE) ** -0.5 * np.log2(np.e)
    pad = np.zeros((t, HEAD_PAD - QK_NOPE - QK_ROPE))
    cosq = np.concatenate([np.full((t, QK_NOPE), scale), cos * scale, pad], axis=1)
    sinq = np.concatenate([np.zeros((t, QK_NOPE)), sin * scale, pad], axis=1)
    tq_t = np.concatenate([cosq, sinq], axis=1).T
    tk = np.concatenate([cos, sin, np.zeros((t, LANES - 2 * QK_ROPE))], axis=1)
    return jnp.asarray(tq_t, F32), jnp.asarray(tk, F32)


def _dft_angles(rows, t):
    k = np.arange(t, dtype=np.int64)
    return ((rows[:, None] * k[None, :]) % t).astype(np.float64) * (2.0 * np.pi / t)


def _dft_tables(t):
    ang = _dft_angles(np.arange(t, dtype=np.int64), t)
    scale = (t * FNET_CH) ** -0.5
    return jnp.asarray(np.cos(ang) * scale, F32).astype(BF16), jnp.asarray(np.sin(ang) * scale, F32).astype(BF16)


def _dft_half_tables(t):
    r = np.arange(MIX_TILE, dtype=np.int64)
    ang = _dft_angles(np.concatenate([r[0::2], r[1::2]]), t)[:, :t // 2]
    scale = (t * FNET_CH) ** -0.5
    ang_off = _dft_angles(np.arange(t // MIX_TILE, dtype=np.int64) * MIX_TILE, t)[:, :t // 2]
    off = np.concatenate([np.cos(ang_off), np.sin(ang_off)], axis=1)
    return jnp.asarray(np.cos(ang) * scale, F32), jnp.asarray(np.sin(ang) * scale, F32), jnp.asarray(off, F32)


def _block_diag(w):
    g, a, b = w.shape
    eye = jnp.eye(g, dtype=w.dtype)
    return (eye[:, None, :, None] * w[:, :, None, :]).reshape(g * a, g * b)


def kernel(x_prompt, x_sample, cache_ckv, cache_kpe, c, c_ctx, w_mod, b_mod, norm1_g, w_in, q_norm_g, w_uq,
           kv_norm_g, w_ukv, w_fmix, w_out, norm2_g, w_router, w_e_gate, w_e_up, w_e_down, final_g):
    assert w_mod.shape[0] == 1, "single-layer problem"
    n_ctx, t_ctx, _ = x_prompt.shape
    n_lat, t_lat, _ = x_sample.shape
    past = cache_ckv.shape[2]
    ctx_row = n_lat

    w_in0 = w_in[0]
    kpe_cols = w_in0[:, Q_LORA + KV_LORA:Q_LORA + KV_LORA + QK_ROPE]
    win = jnp.concatenate([w_in0[:, :Q_LORA + KV_LORA + QK_ROPE], _rot_half(kpe_cols),
                           jnp.zeros((D_MODEL, 512 - Q_LORA - KV_LORA - 2 * QK_ROPE), F32),
                           w_in0[:, Q_LORA + KV_LORA + QK_ROPE:]], axis=1).astype(BF16)
    wq3 = w_uq[0].reshape(Q_LORA, N_HEADS, QK_NOPE + QK_ROPE)
    qpad = jnp.zeros((Q_LORA, N_HEADS, HEAD_PAD - QK_NOPE - QK_ROPE), F32)
    wuq_main = jnp.concatenate([wq3, qpad], axis=2).reshape(Q_LORA, QK_W)
    wuq_rot = jnp.concatenate([jnp.zeros((Q_LORA, N_HEADS, QK_NOPE), F32), _rot_half(wq3[..., QK_NOPE:]), qpad],
                              axis=2).reshape(Q_LORA, QK_W)
    wuq_lat = jnp.concatenate([wuq_main, wuq_rot], axis=1).T.astype(BF16)
    wuq_ctx = wuq_main.T.astype(BF16)
    wkv3 = w_ukv[0].reshape(KV_LORA, N_HEADS, QK_NOPE + V_HEAD)
    wk_top = jnp.concatenate([wkv3[..., :QK_NOPE], jnp.zeros((KV_LORA, N_HEADS, HEAD_PAD - QK_NOPE), F32)],
                             axis=2).reshape(KV_LORA, QK_W)
    place = jnp.concatenate([jnp.zeros((QK_ROPE, QK_NOPE), F32), jnp.eye(QK_ROPE, dtype=F32),
                             jnp.zeros((QK_ROPE, HEAD_PAD - QK_NOPE - QK_ROPE), F32)], axis=1)
    place = jnp.tile(place, (1, N_HEADS))
    wk = jnp.concatenate([wk_top, place, place, jnp.zeros((LANES - 2 * QK_ROPE, QK_W), F32)], axis=0).astype(BF16)
    wv = wkv3[..., QK_NOPE:].reshape(KV_LORA, V_W).T.astype(BF16)
    wo = w_out[0].astype(BF16)
    wr_t = w_router[0].T.astype(BF16)

    cos, sin = _rope_tables(t_lat)
    tq_lat, tk_lat = _qk_tables(cos, sin)
    tq_ctx, tk_ctx = _qk_tables(np.ones((TOK_TILE, QK_ROPE)), np.zeros((TOK_TILE, QK_ROPE)))
    ch_ang = _dft_angles(np.arange(FNET_CH, dtype=np.int64), FNET_CH)
    dft_ctx = _dft_tables(t_ctx)
    dft_lat = _dft_half_tables(t_lat)

    c8 = jnp.concatenate([c, c_ctx[None, :], jnp.zeros((8 - n_lat - 1, D_MODEL), F32)], axis=0)
    mods6 = _mods(c8, w_mod[0], b_mod[0][None, :]).reshape(8, 6, D_MODEL)
    cw, sw = _fold(jnp.asarray(np.cos(ch_ang), F32), jnp.asarray(np.sin(ch_ang), F32), w_fmix[0])
    wcs = jnp.concatenate([_block_diag(cw), _block_diag(sw)], axis=1).astype(BF16)

    g1 = norm1_g[0][None, :]
    qg = q_norm_g[0][None, :]
    kvg = kv_norm_g[0][None, :]
    g2 = norm2_g[0][None, :]
    fg = final_g[None, :]

    xp = x_prompt.reshape(n_ctx * t_ctx, D_MODEL)
    xs = x_sample.reshape(n_lat * t_lat, D_MODEL)
    tiles_lat = t_lat // TOK_TILE

    x1c, h2c, affc, ckv_c, kpe_c = _ctx_front(
        xp, mods6, g1, win, qg, wuq_ctx, kvg, wk, wv, wcs, tq_ctx, tk_ctx,
        *dft_ctx, wo, g2, wr_t, n_req=n_ctx, t=t_ctx, mod_row=ctx_row)
    ql, kl, vl, zcl, zsl = _premix(
        xs, mods6, g1, win, qg, wuq_lat, kvg, wk, wv, wcs, tq_lat, tk_lat,
        mod_row=lambda i: i // tiles_lat, tab_row=lambda i: i % tiles_lat, rope=True)
    xk_cache = jnp.concatenate([cache_ckv[:, 0], cache_kpe[:, 0],
                                jnp.zeros((n_lat, past, 2 * LANES - KV_LORA - QK_ROPE), F32)],
                               axis=-1).reshape(n_lat * past, 2 * LANES).astype(BF16)
    kpast, vpast = _cachekv(xk_cache, wk, wv)

    attn_l = _attention(ql, [kpast, kl], [vpast, vl], n_req=n_lat, t_q=t_lat, kv_lens=[past, t_lat], tq=TOK_TILE,
                        pairs_per_step=2)

    x1l, h2l, affl = _mixout(xs, attn_l, zcl, zsl, *dft_lat, wo, mods6, g2, wr_t,
                             n_req=n_lat, t=t_lat, mod_row=lambda b: b)

    cap_c = CAP_FACTOR * t_ctx // N_EXPERTS
    cap_l = CAP_FACTOR * t_lat // N_EXPERTS
    affc2 = affc.reshape(n_ctx * N_EXPERTS, t_ctx)
    affl2 = affl.reshape(n_lat * N_EXPERTS, t_lat)
    posc, _ = _route(affc2, cap_c)
    posl, offl = _route(affl2, cap_l)
    assert t_lat // TOK_TILE + 1 <= OFF_STRIDE and cap_l % SLOT_WIN == 0
    offl = offl[:, :OFF_STRIDE].reshape(-1)
    xsc, gc = _gather(posc, affc2, h2c, n_req=n_ctx, n=t_ctx, cap=cap_c, eb=N_EXPERTS)
    xsl, gl = _gather_win(offl, posl, affl2, h2l, n_req=n_lat, n=t_lat, cap=cap_l)
    ysc, ysl = _ffn(xsc, xsl, gc, gl, w_e_gate[0], w_e_up[0], w_e_down[0])

    posc_tok = posc.reshape(n_ctx, N_EXPERTS, t_ctx).transpose(0, 2, 1)
    posl_tok = posl.reshape(n_lat, N_EXPERTS, t_lat).transpose(0, 2, 1)
    y_prompt = _combine(x1c, posc_tok, ysc, mods6, fg, n_req=n_ctx, n=t_ctx, cap=cap_c, mod_row=lambda b: ctx_row)
    y_sample = _combine_win(offl, x1l, posl_tok, ysl, mods6, fg, n_req=n_lat, n=t_lat, cap=cap_l,
                            mod_row=lambda b: b)

    return (y_prompt.reshape(n_ctx, t_ctx, D_MODEL), y_sample.reshape(n_lat, t_lat, D_MODEL),
            ckv_c.reshape(n_ctx, 1, t_ctx, KV_LORA), kpe_c.reshape(n_ctx, 1, t_ctx, QK_ROPE))
```

```python
import functools

import jax
import jax.numpy as jnp
import numpy as np
from jax import lax
from jax.experimental import pallas as pl
from jax.experimental.pallas import tpu as pltpu

F32 = jnp.float32
BF16 = jnp.bfloat16

D_MODEL = 1024
N_HEADS = 8
QK_NOPE = 64
QK_ROPE = 32
V_HEAD = 64
Q_LORA = 256
KV_LORA = 128
FNET_GROUPS = 8
FNET_CH = 64
FNET_W = FNET_GROUPS * FNET_CH
N_EXPERTS = 16
CAP_FACTOR = 2
D_EXPERT = 512
GRID_W = 64
ROPE_BASE = 10000.0
EPS = 1e-6

LANES = 128
HEAD_PAD = LANES
QK_W = N_HEADS * HEAD_PAD
V_W = N_HEADS * V_HEAD
TOK_TILE = 256
MIX_TILE = 512
CTX_REQS = 2
MOE_REQS = 4
KEY_CHUNK = 512
SLOT_WIN = 64
OFF_STRIDE = 16
VMEM_LIMIT = 48 * 1024 * 1024

_NT = (((1,), (1,)), ((), ()))


def _cparams(sem):
    return pltpu.CompilerParams(dimension_semantics=sem, vmem_limit_bytes=VMEM_LIMIT)


def _rms(x, g):
    return x * lax.rsqrt(jnp.mean(x * x, axis=-1, keepdims=True) + EPS) * g


def _dot(a, b):
    return jnp.dot(a, b, preferred_element_type=F32)


def _mods_kernel(c_ref, w_ref, b_ref, o_ref):
    c = c_ref[...]
    s = c * jax.nn.sigmoid(c)
    o_ref[...] = _dot(s.astype(BF16), w_ref[...].astype(BF16)) + b_ref[...]


def _mods(c8, w_mod, b_mod):
    n = w_mod.shape[1]
    tn = 1536
    return pl.pallas_call(
        _mods_kernel,
        grid=(n // tn,),
        in_specs=[pl.BlockSpec((8, D_MODEL), lambda j: (0, 0)),
                  pl.BlockSpec((D_MODEL, tn), lambda j: (0, j)),
                  pl.BlockSpec((1, tn), lambda j: (0, j))],
        out_specs=pl.BlockSpec((8, tn), lambda j: (0, j)),
        out_shape=jax.ShapeDtypeStruct((8, n), F32),
        compiler_params=_cparams(("arbitrary",)),
        name="mods",
    )(c8, w_mod, b_mod)


def _fold_kernel(cc_ref, sc_ref, w_ref, cw_ref, sw_ref):
    for g in range(FNET_GROUPS):
        w = w_ref[g]
        cw_ref[g] = jnp.dot(cc_ref[...], w, preferred_element_type=F32, precision=lax.Precision.HIGHEST)
        sw_ref[g] = jnp.dot(sc_ref[...], w, preferred_element_type=F32, precision=lax.Precision.HIGHEST)


def _fold(cc, sc, w_fmix):
    shp = jax.ShapeDtypeStruct((FNET_GROUPS, FNET_CH, FNET_CH), F32)
    return pl.pallas_call(_fold_kernel, out_shape=(shp, shp), name="fold")(cc, sc, w_fmix)


def _premix_body(x, m_ref, g1_ref, win_ref, qg_ref, wuq_ref, kvg_ref, wk_ref, wv_ref, wcs_ref, tq_ref, tk_ref, rope):
    shift1 = m_ref[0, 0:1, :]
    scale1 = m_ref[0, 1:2, :]
    h = _rms(x, g1_ref[...]) * (1.0 + scale1) + shift1
    proj = _dot(h.astype(BF16), win_ref[...])
    qn = _rms(proj[:, 0:Q_LORA], qg_ref[...]).astype(BF16)
    qq = lax.dot_general(wuq_ref[...], qn, _NT, preferred_element_type=F32)
    cosq = tq_ref[0:LANES, :]
    sinq = tq_ref[LANES:2 * LANES, :]
    q_heads = []
    for hd in range(N_HEADS):
        lo = hd * HEAD_PAD
        qh = qq[lo:lo + HEAD_PAD, :] * cosq
        if rope:
            qh = qh + qq[QK_W + lo:QK_W + lo + HEAD_PAD, :] * sinq
        q_heads.append(qh.astype(BF16))
    ckv = _rms(proj[:, Q_LORA:Q_LORA + KV_LORA], kvg_ref[...])
    kpe2 = proj[:, Q_LORA + KV_LORA:Q_LORA + KV_LORA + LANES] * tk_ref[...]
    xk = jnp.concatenate([ckv, kpe2], axis=1).astype(BF16)
    k = _dot(xk, wk_ref[...]).astype(BF16)
    v_t = lax.dot_general(wv_ref[...], xk[:, 0:KV_LORA], _NT, preferred_element_type=F32).astype(BF16)
    z = _dot(proj[:, 512:1024].astype(BF16), wcs_ref[...])
    zc = z[:, 0:FNET_W].astype(BF16)
    zs = z[:, FNET_W:2 * FNET_W].astype(BF16)
    kpe = proj[:, Q_LORA + KV_LORA:Q_LORA + KV_LORA + QK_ROPE]
    return q_heads, k, v_t, zc, zs, ckv, kpe


def _premix_kernel(x_ref, m_ref, g1_ref, win_ref, qg_ref, wuq_ref, kvg_ref, wk_ref, wv_ref, wcs_ref,
                   tq_ref, tk_ref, q_ref, k_ref, v_ref, zc_ref, zs_ref, *, rope):
    q_heads, k, v_t, zc, zs, _, _ = _premix_body(x_ref[...], m_ref, g1_ref, win_ref, qg_ref, wuq_ref, kvg_ref,
                                                 wk_ref, wv_ref, wcs_ref, tq_ref, tk_ref, rope)
    for hd, qh in enumerate(q_heads):
        q_ref[hd * HEAD_PAD:(hd + 1) * HEAD_PAD, :] = qh
    k_ref[...] = k
    v_ref[...] = v_t
    zc_ref[...] = zc
    zs_ref[...] = zs


def _premix(x, mods6, g1, win, qg, wuq, kvg, wk, wv, wcs, tq, tk, *, mod_row, tab_row, rope):
    n = x.shape[0]
    tm = MIX_TILE
    full = lambda a: pl.BlockSpec(a.shape, lambda i: (0,) * a.ndim)
    out_shape = [jax.ShapeDtypeStruct((QK_W, n), BF16), jax.ShapeDtypeStruct((n, QK_W), BF16),
                 jax.ShapeDtypeStruct((V_W, n), BF16), jax.ShapeDtypeStruct((n, FNET_W), BF16),
                 jax.ShapeDtypeStruct((n, FNET_W), BF16)]
    out_specs = [pl.BlockSpec((QK_W, tm), lambda i: (0, i)), pl.BlockSpec((tm, QK_W), lambda i: (i, 0)),
                 pl.BlockSpec((V_W, tm), lambda i: (0, i)), pl.BlockSpec((tm, FNET_W), lambda i: (i, 0)),
                 pl.BlockSpec((tm, FNET_W), lambda i: (i, 0))]
    return pl.pallas_call(
        functools.partial(_premix_kernel, rope=rope),
        grid=(n // tm,),
        in_specs=[pl.BlockSpec((tm, D_MODEL), lambda i: (i, 0)),
                  pl.BlockSpec((1, 6, D_MODEL), lambda i: (mod_row(i), 0, 0)),
                  full(g1), full(win), full(qg), full(wuq), full(kvg), full(wk), full(wv), full(wcs),
                  pl.BlockSpec((2 * LANES, tm), lambda i: (0, tab_row(i))),
                  pl.BlockSpec((tm, LANES), lambda i: (tab_row(i), 0))],
        out_specs=out_specs,
        out_shape=out_shape,
        compiler_params=_cparams(("parallel",)),
        name="premix",
    )(x, mods6, g1, win, qg, wuq, kvg, wk, wv, wcs, tq, tk)


def _cachekv_kernel(xk_ref, wk_ref, wv_ref, k_ref, v_ref):
    xk = xk_ref[...]
    k_ref[...] = _dot(xk, wk_ref[...]).astype(BF16)
    v_ref[...] = lax.dot_general(wv_ref[...], xk[:, 0:KV_LORA], _NT, preferred_element_type=F32).astype(BF16)


def _cachekv(xk, wk, wv):
    n = xk.shape[0]
    tm = 512
    full = lambda a: pl.BlockSpec(a.shape, lambda i: (0,) * a.ndim)
    return pl.pallas_call(
        _cachekv_kernel,
        grid=(n // tm,),
        in_specs=[pl.BlockSpec((tm, 2 * LANES), lambda i: (i, 0)), full(wk), full(wv)],
        out_specs=[pl.BlockSpec((tm, QK_W), lambda i: (i, 0)), pl.BlockSpec((V_W, tm), lambda i: (0, i))],
        out_shape=[jax.ShapeDtypeStruct((n, QK_W), BF16), jax.ShapeDtypeStruct((V_W, n), BF16)],
        compiler_params=_cparams(("parallel",)),
        name="cachekv",
    )(xk, wk, wv)


def _attn_body(q_heads, k_refs, v_refs, kc):
    tq = q_heads[0].shape[1]
    zero = jnp.zeros((HEAD_PAD, tq), BF16)
    n_pairs = len(q_heads) // 2
    qbd = [jnp.concatenate([jnp.concatenate([q_heads[2 * pr], zero], axis=1),
                            jnp.concatenate([zero, q_heads[2 * pr + 1]], axis=1)], axis=0) for pr in range(n_pairs)]
    chunks = [(k_ref, v_ref, c0, min(c0 + kc, k_ref.shape[0]))
              for k_ref, v_ref in zip(k_refs, v_refs) for c0 in range(0, k_ref.shape[0], kc)]
    work = [(pr, ch) for pr in range(n_pairs) for ch in chunks]

    def score(item):
        pr, (k_ref, _, c0, c1) = item
        return _dot(k_ref[c0:c1, pr * 2 * HEAD_PAD:(pr + 1) * 2 * HEAD_PAD], qbd[pr])

    m = [None] * n_pairs
    o = [None] * n_pairs
    s_next = score(work[0])
    for wi, (pr, (k_ref, v_ref, c0, c1)) in enumerate(work):
        s = s_next
        if wi + 1 < len(work):
            s_next = score(work[wi + 1])
        cm = jnp.max(s, axis=0, keepdims=True)
        vlo = pr * 2 * V_HEAD
        va = jnp.concatenate([v_ref[vlo:vlo + 2 * V_HEAD, c0:c1], jnp.ones((16, c1 - c0), BF16)], axis=0)
        if m[pr] is None:
            m[pr] = cm
            o[pr] = _dot(va, jnp.exp2((s - cm).astype(BF16)))
        else:
            m_new = jnp.maximum(m[pr], cm)
            alpha = jnp.exp2(m[pr] - m_new)
            o[pr] = alpha * o[pr] + _dot(va, jnp.exp2((s - m_new).astype(BF16)))
            m[pr] = m_new
    outs = []
    for pr in range(n_pairs):
        on = o[pr][0:2 * V_HEAD, :] * (1.0 / o[pr][2 * V_HEAD:2 * V_HEAD + 1, :])
        ot = jnp.concatenate([on[0:V_HEAD, 0:tq], on[V_HEAD:2 * V_HEAD, tq:2 * tq]], axis=0)
        outs.append(ot.T.astype(BF16))
    return outs


def _attn_kernel(q_ref, *refs, n_kv, n_pairs, kc):
    q_heads = [q_ref[hd * HEAD_PAD:(hd + 1) * HEAD_PAD, :] for hd in range(2 * n_pairs)]
    outs = _attn_body(q_heads, refs[:n_kv], refs[n_kv:2 * n_kv], kc)
    o_ref = refs[2 * n_kv]
    for pr, o in enumerate(outs):
        o_ref[:, pr * 2 * V_HEAD:(pr + 1) * 2 * V_HEAD] = o


def _attention(q_t, ks, vs_t, *, n_req, t_q, kv_lens, tq, pairs_per_step):
    n_kv = len(ks)
    nq = t_q // tq
    pp = pairs_per_step
    in_specs = [pl.BlockSpec((pp * 2 * HEAD_PAD, tq), lambda b, p, i: (p, b * nq + i))]
    in_specs += [pl.BlockSpec((kl, pp * 2 * HEAD_PAD), lambda b, p, i: (b, p)) for kl in kv_lens]
    in_specs += [pl.BlockSpec((pp * 2 * V_HEAD, kl), lambda b, p, i: (p, b)) for kl in kv_lens]
    return pl.pallas_call(
        functools.partial(_attn_kernel, n_kv=n_kv, n_pairs=pp, kc=KEY_CHUNK),
        grid=(n_req, N_HEADS // 2 // pp, nq),
        in_specs=in_specs,
        out_specs=pl.BlockSpec((tq, pp * 2 * V_HEAD), lambda b, p, i: (b * nq + i, p)),
        out_shape=jax.ShapeDtypeStruct((n_req * t_q, V_W), BF16),
        compiler_params=_cparams(("parallel", "parallel", "parallel")),
        name="attn",
    )(q_t, *ks, *vs_t)


def _mixout_body(x, attn_pairs, fm, wo_ref, m_ref, g2_ref, wr_ref):
    y = _dot(fm, wo_ref[V_W:V_W + FNET_W, :])
    col = 0
    for a in attn_pairs:
        y = y + _dot(a, wo_ref[col:col + a.shape[1], :])
        col += a.shape[1]
    gate1 = m_ref[0, 2:3, :]
    shift2 = m_ref[0, 3:4, :]
    scale2 = m_ref[0, 4:5, :]
    x1 = x + gate1 * y
    h2 = (_rms(x1, g2_ref[...]) * (1.0 + scale2) + shift2).astype(BF16)
    lg = lax.dot_general(wr_ref[...], h2, _NT, preferred_element_type=F32)
    e = jnp.exp(lg - jnp.max(lg, axis=0, keepdims=True))
    return x1, h2, e / jnp.sum(e, axis=0, keepdims=True)


def _mixout_kernel(x_ref, a_ref, zc_ref, zs_ref, cb_ref, sb_ref, off_ref, wo_ref, m_ref, g2_ref, wr_ref,
                   x1_ref, h2_ref, aff_ref, zp_ref, zm_ref, *, t):
    half = t // 2
    i = pl.program_id(1)

    @pl.when(i == 0)
    def _fold_halves():
        for src, col in ((zc_ref, 0), (zs_ref, FNET_W)):
            lo = src[0:half, :]
            hi = src[half:t, :]
            zp_ref[:, col:col + FNET_W] = lo + hi
            zm_ref[:, col:col + FNET_W] = lo - hi

    off = off_ref[pl.ds(i, 1), :]
    co = off[:, 0:half]
    so = off[:, half:t]
    cb = cb_ref[...]
    sb = sb_ref[...]
    ct = (cb * co - sb * so).astype(BF16)
    st = (sb * co + cb * so).astype(BF16)
    tr = cb.shape[0]
    h = tr // 2
    even = _dot(ct[0:h, :], zp_ref[:, 0:FNET_W]) - _dot(st[0:h, :], zp_ref[:, FNET_W:2 * FNET_W])
    odd = _dot(ct[h:, :], zm_ref[:, 0:FNET_W]) - _dot(st[h:, :], zm_ref[:, FNET_W:2 * FNET_W])
    k_i = lax.broadcasted_iota(jnp.int32, (tr, tr), 0)
    j_i = lax.broadcasted_iota(jnp.int32, (tr, tr), 1)
    perm = jnp.where(j_i == (k_i >> 1) + h * (k_i & 1), 1.0, 0.0).astype(BF16)
    fm = _dot(perm, jnp.concatenate([even, odd], axis=0).astype(BF16)).astype(BF16)
    x1, h2, aff = _mixout_body(x_ref[...], [a_ref[...]], fm, wo_ref, m_ref, g2_ref, wr_ref)
    x1_ref[...] = x1
    h2_ref[...] = h2
    aff_ref[...] = aff


def _ctx_front_kernel(x_ref, m_ref, g1_ref, win_ref, qg_ref, wuq_ref, kvg_ref, wk_ref, wv_ref, wcs_ref,
                      tq_ref, tk_ref, ct_ref, st_ref, wo_ref, g2_ref, wr_ref,
                      x1_ref, h2_ref, aff_ref, ckv_ref, kpe_ref, *, kc, t):
    x = x_ref[...]
    q_heads, k, v_t, zc, zs, ckv, kpe = _premix_body(x, m_ref, g1_ref, win_ref, qg_ref, wuq_ref, kvg_ref,
                                                     wk_ref, wv_ref, wcs_ref, tq_ref, tk_ref, False)
    ckv_ref[...] = ckv
    kpe_ref[...] = kpe
    attn, fm = [], []
    for r in range(x.shape[0] // t):
        rows = slice(r * t, (r + 1) * t)
        attn.append(jnp.concatenate(_attn_body([q[:, rows] for q in q_heads], [k[rows, :]], [v_t[:, rows]], kc),
                                    axis=1))
        fm.append((_dot(ct_ref[...], zc[rows, :]) - _dot(st_ref[...], zs[rows, :])).astype(BF16))
    x1, h2, aff = _mixout_body(x, [jnp.concatenate(attn, axis=0)], jnp.concatenate(fm, axis=0),
                               wo_ref, m_ref, g2_ref, wr_ref)
    x1_ref[...] = x1
    h2_ref[...] = h2
    for r in range(x.shape[0] // t):
        aff_ref[r] = aff[:, r * t:(r + 1) * t]


def _ctx_front(x, mods6, g1, win, qg, wuq, kvg, wk, wv, wcs, tq, tk, ct, st, wo, g2, wr_t, *, n_req, t, mod_row):
    full = lambda a: pl.BlockSpec(a.shape, lambda b: (0,) * a.ndim)
    rps = CTX_REQS
    row = lambda w: pl.BlockSpec((rps * t, w), lambda b: (b, 0))
    return pl.pallas_call(
        functools.partial(_ctx_front_kernel, kc=KEY_CHUNK, t=t),
        grid=(n_req // rps,),
        in_specs=[row(D_MODEL), pl.BlockSpec((1, 6, D_MODEL), lambda b: (mod_row, 0, 0)),
                  full(g1), full(win), full(qg), full(wuq), full(kvg), full(wk), full(wv), full(wcs),
                  full(tq), full(tk), full(ct), full(st), full(wo), full(g2), full(wr_t)],
        out_specs=[row(D_MODEL), row(D_MODEL), pl.BlockSpec((rps, N_EXPERTS, t), lambda b: (b, 0, 0)),
                   row(KV_LORA), row(QK_ROPE)],
        out_shape=[jax.ShapeDtypeStruct((n_req * t, D_MODEL), F32),
                   jax.ShapeDtypeStruct((n_req * t, D_MODEL), BF16),
                   jax.ShapeDtypeStruct((n_req, N_EXPERTS, t), F32),
                   jax.ShapeDtypeStruct((n_req * t, KV_LORA), F32),
                   jax.ShapeDtypeStruct((n_req * t, QK_ROPE), F32)],
        compiler_params=_cparams(("parallel",)),
        name="ctx_front",
    )(x, mods6, g1, win, qg, wuq, kvg, wk, wv, wcs, tq, tk, ct, st, wo, g2, wr_t)


def _mixout(x, attn, zc, zs, cb, sb, off, wo, mods6, g2, wr_t, *, n_req, t, mod_row):
    tr = MIX_TILE
    nr = t // tr
    full = lambda a: pl.BlockSpec(a.shape, lambda b, i: (0,) * a.ndim)
    return pl.pallas_call(
        functools.partial(_mixout_kernel, t=t),
        grid=(n_req, nr),
        in_specs=[pl.BlockSpec((tr, D_MODEL), lambda b, i: (b * nr + i, 0)),
                  pl.BlockSpec((tr, V_W), lambda b, i: (b * nr + i, 0)),
                  pl.BlockSpec((t, FNET_W), lambda b, i: (b, 0)),
                  pl.BlockSpec((t, FNET_W), lambda b, i: (b, 0)),
                  full(cb), full(sb), full(off),
                  full(wo),
                  pl.BlockSpec((1, 6, D_MODEL), lambda b, i: (mod_row(b), 0, 0)),
                  full(g2), full(wr_t)],
        out_specs=[pl.BlockSpec((tr, D_MODEL), lambda b, i: (b * nr + i, 0)),
                   pl.BlockSpec((tr, D_MODEL), lambda b, i: (b * nr + i, 0)),
                   pl.BlockSpec((None, N_EXPERTS, tr), lambda b, i: (b, 0, i))],
        out_shape=[jax.ShapeDtypeStruct((n_req * t, D_MODEL), F32),
                   jax.ShapeDtypeStruct((n_req * t, D_MODEL), BF16),
                   jax.ShapeDtypeStruct((n_req, N_EXPERTS, t), F32)],
        scratch_shapes=[pltpu.VMEM((t // 2, 2 * FNET_W), BF16), pltpu.VMEM((t // 2, 2 * FNET_W), BF16)],
        compiler_params=_cparams(("parallel", "arbitrary")),
        name="mixout",
    )(x, attn, zc, zs, cb, sb, off, wo, mods6, g2, wr_t)


def _prefix_count(flags, tri):
    n = flags.shape[1]
    carry = None
    outs = []
    ends = []
    for j in range(n // TOK_TILE):
        c = _dot(flags[:, j * TOK_TILE:(j + 1) * TOK_TILE].astype(BF16), tri)
        if carry is not None:
            c = c + carry
        outs.append(c)
        carry = c[:, TOK_TILE - 1:TOK_TILE]
        ends.append(carry)
    return (outs[0] if len(outs) == 1 else jnp.concatenate(outs, axis=1)), ends


def _route_kernel(aff_ref, pos_ref, off_ref, *, cap):
    a = aff_ref[...]
    rows = a.shape[0]
    capf = jnp.float32(cap)
    thr = jnp.zeros((rows, 1), jnp.int32)
    for bit in range(30, -1, -1):
        cand = thr | jnp.int32(1 << bit)
        cand_f = lax.bitcast_convert_type(cand, F32)
        cnt = jnp.sum(jnp.where(a >= cand_f, 1.0, 0.0), axis=1, keepdims=True)
        thr = jnp.where(cnt >= capf, cand, thr)
    thr_f = lax.bitcast_convert_type(thr, F32)
    above_f = lax.bitcast_convert_type(thr + 1, F32)
    gt = jnp.where(a >= above_f, 1.0, 0.0)
    tie = jnp.where(a >= thr_f, 1.0, 0.0) - gt
    need = capf - jnp.sum(gt, axis=1, keepdims=True)
    r_i = lax.broadcasted_iota(jnp.int32, (TOK_TILE, TOK_TILE), 0)
    c_i = lax.broadcasted_iota(jnp.int32, (TOK_TILE, TOK_TILE), 1)
    tri = jnp.where(r_i <= c_i, 1.0, 0.0).astype(BF16)
    tie_before = _prefix_count(tie, tri)[0] - tie
    sel = gt + tie * jnp.where(tie_before < need, 1.0, 0.0)
    count, ends = _prefix_count(sel, tri)
    pos_ref[...] = jnp.where(sel > 0.5, count - 1.0, -1.0)
    lane = lax.broadcasted_iota(jnp.int32, (rows, LANES), 1)
    offs = jnp.zeros((rows, LANES), F32)
    for j, end in enumerate(ends):
        offs = offs + jnp.where(lane == j + 1, end, 0.0)
    off_ref[...] = offs.astype(jnp.int32)


def _route(aff_t, cap):
    return pl.pallas_call(
        functools.partial(_route_kernel, cap=cap),
        out_shape=[jax.ShapeDtypeStruct(aff_t.shape, F32),
                   jax.ShapeDtypeStruct((aff_t.shape[0], LANES), jnp.int32)],
        compiler_params=pltpu.CompilerParams(vmem_limit_bytes=VMEM_LIMIT),
        name="route",
    )(aff_t)


def _gather_kernel(pos_ref, aff_ref, h_ref, xs_ref, g_ref, *, cap):
    rps, ne, n = pos_ref.shape
    slot = lax.broadcasted_iota(jnp.int32, (ne, cap, n), 1).astype(F32)
    for r in range(rps):
        pos = pos_ref[r]
        aff = aff_ref[r]
        hit = pos[:, None, :] == slot
        onehot = jnp.where(hit, 1.0, 0.0).reshape(ne * cap, n).astype(BF16)
        xs = _dot(onehot, h_ref[r * n:(r + 1) * n, :])
        xs_ref[:, r * cap:(r + 1) * cap, :] = xs.astype(BF16).reshape(ne, cap, D_MODEL)
        g_ref[:, r * cap:(r + 1) * cap, :] = jnp.sum(jnp.where(hit, aff[:, None, :], 0.0), axis=2, keepdims=True)


def _gather(pos_t, aff_t, h2, *, n_req, n, cap, rps):
    pos3 = pos_t.reshape(n_req, N_EXPERTS, n)
    aff3 = aff_t.reshape(n_req, N_EXPERTS, n)
    return pl.pallas_call(
        functools.partial(_gather_kernel, cap=cap),
        grid=(n_req // rps,),
        in_specs=[pl.BlockSpec((rps, N_EXPERTS, n), lambda b: (b, 0, 0)),
                  pl.BlockSpec((rps, N_EXPERTS, n), lambda b: (b, 0, 0)),
                  pl.BlockSpec((rps * n, D_MODEL), lambda b: (b, 0))],
        out_specs=[pl.BlockSpec((N_EXPERTS, rps * cap, D_MODEL), lambda b: (0, b, 0)),
                   pl.BlockSpec((N_EXPERTS, rps * cap, 1), lambda b: (0, b, 0))],
        out_shape=[jax.ShapeDtypeStruct((N_EXPERTS, n_req * cap, D_MODEL), BF16),
                   jax.ShapeDtypeStruct((N_EXPERTS, n_req * cap, 1), F32)],
        compiler_params=_cparams(("parallel",)),
        name="gather",
    )(pos3, aff3, h2)


def _window_plan(off_ref, b, j, cap, w):
    w0 = []
    need = jnp.int32(0)
    for e in range(N_EXPERTS):
        idx = (b * N_EXPERTS + e) * OFF_STRIDE + j
        base = (off_ref[idx] >> 4) << 4
        w0.append(base)
        need = jnp.maximum(need, off_ref[idx + 1] - base)
    return w0, (need + (w - 1)) >> (w.bit_length() - 1)


def _window(w0_e, p, cap, w):
    low = w0_e + p * w
    start = pl.multiple_of(jnp.minimum(low, cap - w), 16)
    return low, start


def _gather_win_kernel(off_ref, pos_ref, aff_ref, h_ref, xs_ref, g_ref, *, cap, w):
    b = pl.program_id(0)
    j = pl.program_id(1)

    @pl.when(j == 0)
    def _init():
        xs_ref[...] = jnp.zeros_like(xs_ref)
        g_ref[...] = jnp.zeros_like(g_ref)

    pos = pos_ref[...]
    aff = aff_ref[...]
    h = h_ref[...]
    w0, n_pass = _window_plan(off_ref, b, j, cap, w)
    r = lax.broadcasted_iota(jnp.int32, (w, 1), 0).astype(F32)

    def one_pass(p, carry):
        starts, hots, gates = [], [], []
        for e in range(N_EXPERTS):
            low, start = _window(w0[e], p, cap, w)
            starts.append(start)
            mine = jnp.where(r >= (low - start).astype(F32), 1.0, 0.0)
            hot = jnp.where(pos[e:e + 1, :] - start.astype(F32) == r, mine, 0.0)
            hots.append(hot)
            gates.append(jnp.sum(hot * aff[e:e + 1, :], axis=1, keepdims=True))
        rows = _dot(jnp.concatenate(hots, axis=0).astype(BF16), h).astype(BF16)
        for e in range(N_EXPERTS):
            win = pl.ds(starts[e], w)
            xs_ref[e, win, :] = xs_ref[e, win, :] + rows[e * w:(e + 1) * w, :]
            g_ref[e, win, :] = g_ref[e, win, :] + gates[e]
        return carry

    lax.fori_loop(0, n_pass, one_pass, 0)


def _gather_win(offs, pos_t, aff_t, h2, *, n_req, n, cap):
    nb = n // TOK_TILE
    grid_spec = pltpu.PrefetchScalarGridSpec(
        num_scalar_prefetch=1,
        grid=(n_req, nb),
        in_specs=[pl.BlockSpec((N_EXPERTS, TOK_TILE), lambda b, j, off: (b, j)),
                  pl.BlockSpec((N_EXPERTS, TOK_TILE), lambda b, j, off: (b, j)),
                  pl.BlockSpec((TOK_TILE, D_MODEL), lambda b, j, off: (b * nb + j, 0))],
        out_specs=[pl.BlockSpec((N_EXPERTS, cap, D_MODEL), lambda b, j, off: (0, b, 0)),
                   pl.BlockSpec((N_EXPERTS, cap, 1), lambda b, j, off: (0, b, 0))])
    return pl.pallas_call(
        functools.partial(_gather_win_kernel, cap=cap, w=SLOT_WIN),
        grid_spec=grid_spec,
        out_shape=[jax.ShapeDtypeStruct((N_EXPERTS, n_req * cap, D_MODEL), BF16),
                   jax.ShapeDtypeStruct((N_EXPERTS, n_req * cap, 1), F32)],
        compiler_params=_cparams(("parallel", "arbitrary")),
        name="gather_win",
    )(offs, pos_t, aff_t, h2)


def _ffn_kernel(xc_ref, xl_ref, gc_ref, gl_ref, wg_ref, wu_ref, wd_ref, yc_ref, yl_ref, wgb, wub, wdb):
    wgb[...] = wg_ref[0].astype(BF16)
    wub[...] = wu_ref[0].astype(BF16)
    wdb[...] = wd_ref[0].astype(BF16)
    for x_ref, g_ref, y_ref in ((xc_ref, gc_ref, yc_ref), (xl_ref, gl_ref, yl_ref)):
        for j in range(x_ref.shape[1] // TOK_TILE):
            rows = slice(j * TOK_TILE, (j + 1) * TOK_TILE)
            x = x_ref[0, rows, :]
            gate = _dot(x, wgb[...])
            up = _dot(x, wub[...])
            hid = (gate * jax.nn.sigmoid(gate) * up).astype(BF16)
            ys = _dot(hid, wdb[...]) * g_ref[0, rows, :]
            y_ref[0, rows, :] = ys.astype(BF16)


def _ffn(xc, xl, gc, gl, wg, wu, wd):
    m = xc.shape[1]
    xspec = pl.BlockSpec((1, m, D_MODEL), lambda e: (e, 0, 0))
    gspec = pl.BlockSpec((1, m, 1), lambda e: (e, 0, 0))
    shp = jax.ShapeDtypeStruct((N_EXPERTS, m, D_MODEL), BF16)
    return pl.pallas_call(
        _ffn_kernel,
        grid=(N_EXPERTS,),
        in_specs=[xspec, xspec, gspec, gspec,
                  pl.BlockSpec((1, D_MODEL, D_EXPERT), lambda e: (e, 0, 0)),
                  pl.BlockSpec((1, D_MODEL, D_EXPERT), lambda e: (e, 0, 0)),
                  pl.BlockSpec((1, D_EXPERT, D_MODEL), lambda e: (e, 0, 0))],
        out_specs=[xspec, xspec],
        out_shape=[shp, shp],
        scratch_shapes=[pltpu.VMEM((D_MODEL, D_EXPERT), BF16), pltpu.VMEM((D_MODEL, D_EXPERT), BF16),
                        pltpu.VMEM((D_EXPERT, D_MODEL), BF16)],
        compiler_params=_cparams(("arbitrary",)),
        name="ffn",
    )(xc, xl, gc, gl, wg, wu, wd)


def _combine_kernel(x1_ref, pos_ref, ys_ref, m_ref, fg_ref, o_ref, *, cap):
    rps, n, _ = pos_ref.shape
    w = N_EXPERTS * cap
    e_i = lax.broadcasted_iota(jnp.int32, (N_EXPERTS, w), 0)
    j_i = lax.broadcasted_iota(jnp.int32, (N_EXPERTS, w), 1)
    spread = jnp.where((j_i >> (cap.bit_length() - 1)) == e_i, 1.0, 0.0).astype(BF16)
    lane_slot = (lax.broadcasted_iota(jnp.int32, (1, w), 1) & (cap - 1)).astype(F32)
    gate2 = m_ref[0, 5:6, :]
    for r in range(rps):
        pos = pos_ref[r].astype(BF16)
        onehot = jnp.where(_dot(pos, spread) == lane_slot, 1.0, 0.0).astype(BF16)
        acc = _dot(onehot, ys_ref[:, r * cap:(r + 1) * cap, :].reshape(w, D_MODEL))
        rows = slice(r * n, (r + 1) * n)
        o_ref[rows, :] = _rms(x1_ref[rows, :] + gate2 * acc, fg_ref[...])


def _combine(x1, pos_tok, ys, mods6, fg, *, n_req, n, cap, mod_row, rps):
    return pl.pallas_call(
        functools.partial(_combine_kernel, cap=cap),
        grid=(n_req // rps,),
        in_specs=[pl.BlockSpec((rps * n, D_MODEL), lambda b: (b, 0)),
                  pl.BlockSpec((rps, n, N_EXPERTS), lambda b: (b, 0, 0)),
                  pl.BlockSpec((N_EXPERTS, rps * cap, D_MODEL), lambda b: (0, b, 0)),
                  pl.BlockSpec((1, 6, D_MODEL), lambda b: (mod_row, 0, 0)),
                  pl.BlockSpec((1, D_MODEL), lambda b: (0, 0))],
        out_specs=pl.BlockSpec((rps * n, D_MODEL), lambda b: (b, 0)),
        out_shape=jax.ShapeDtypeStruct((n_req * n, D_MODEL), F32),
        compiler_params=_cparams(("parallel",)),
        name="combine",
    )(x1, pos_tok, ys, mods6, fg)


def _combine_win_kernel(off_ref, x1_ref, pos_ref, ys_ref, m_ref, fg_ref, o_ref, acc_ref, *, cap, w):
    b = pl.program_id(0)
    j = pl.program_id(1)
    pos = pos_ref[...]
    w0, n_pass = _window_plan(off_ref, b, j, cap, w)
    width = N_EXPERTS * w
    e_i = lax.broadcasted_iota(jnp.int32, (N_EXPERTS, width), 0)
    j_i = lax.broadcasted_iota(jnp.int32, (N_EXPERTS, width), 1)
    spread = jnp.where((j_i >> (w.bit_length() - 1)) == e_i, 1.0, 0.0).astype(BF16)
    lane_slot = (lax.broadcasted_iota(jnp.int32, (1, width), 1) & (w - 1)).astype(F32)
    lane_e = lax.broadcasted_iota(jnp.int32, (1, N_EXPERTS), 1)
    acc_ref[...] = jnp.zeros_like(acc_ref)

    def one_pass(p, carry):
        start_row = jnp.zeros((1, N_EXPERTS), F32)
        first_row = jnp.zeros((1, N_EXPERTS), F32)
        wins = []
        for e in range(N_EXPERTS):
            low, start = _window(w0[e], p, cap, w)
            start_row = jnp.where(lane_e == e, start.astype(F32), start_row)
            first_row = jnp.where(lane_e == e, (low - start).astype(F32), first_row)
            wins.append(ys_ref[e, pl.ds(start, w), :])
        rel = pos - start_row
        rel = jnp.where(rel >= first_row, rel, -1.0).astype(BF16)
        onehot = jnp.where(_dot(rel, spread) == lane_slot, 1.0, 0.0).astype(BF16)
        acc_ref[...] += _dot(onehot, jnp.concatenate(wins, axis=0))
        return carry

    lax.fori_loop(0, n_pass, one_pass, 0)
    gate2 = m_ref[0, 5:6, :]
    o_ref[...] = _rms(x1_ref[...] + gate2 * acc_ref[...], fg_ref[...])


def _combine_win(offs, x1, pos_tok, ys, mods6, fg, *, n_req, n, cap, mod_row):
    tr = TOK_TILE
    nr = n // tr
    grid_spec = pltpu.PrefetchScalarGridSpec(
        num_scalar_prefetch=1,
        grid=(n_req, nr),
        in_specs=[pl.BlockSpec((tr, D_MODEL), lambda b, i, off: (b * nr + i, 0)),
                  pl.BlockSpec((None, tr, N_EXPERTS), lambda b, i, off: (b, i, 0)),
                  pl.BlockSpec((N_EXPERTS, cap, D_MODEL), lambda b, i, off: (0, b, 0)),
                  pl.BlockSpec((1, 6, D_MODEL), lambda b, i, off: (mod_row(b), 0, 0)),
                  pl.BlockSpec((1, D_MODEL), lambda b, i, off: (0, 0))],
        out_specs=pl.BlockSpec((tr, D_MODEL), lambda b, i, off: (b * nr + i, 0)),
        scratch_shapes=[pltpu.VMEM((tr, D_MODEL), F32)])
    return pl.pallas_call(
        functools.partial(_combine_win_kernel, cap=cap, w=SLOT_WIN),
        grid_spec=grid_spec,
        out_shape=jax.ShapeDtypeStruct((n_req * n, D_MODEL), F32),
        compiler_params=_cparams(("parallel", "parallel")),
        name="combine_win",
    )(offs, x1, pos_tok, ys, mods6, fg)


def _rot_half(w):
    half = QK_ROPE // 2
    return jnp.concatenate([-w[..., half:], w[..., :half]], axis=-1)


def _rope_tables(t):
    n_rows = t // GRID_W
    rows = np.repeat(np.arange(n_rows, dtype=np.float64), GRID_W)
    cols = np.tile(np.arange(GRID_W, dtype=np.float64), n_rows)
    n_freq = QK_ROPE // 4
    inv_freq = ROPE_BASE ** (-np.arange(n_freq, dtype=np.float64) / n_freq)
    ang = np.concatenate([rows[:, None] * inv_freq, cols[:, None] * inv_freq], axis=-1)
    cos = np.concatenate([np.cos(ang), np.cos(ang)], axis=-1)
    sin = np.concatenate([np.sin(ang), np.sin(ang)], axis=-1)
    return cos, sin


def _qk_tables(cos, sin):
    t = cos.shape[0]
    scale = (QK_NOPE + QK_ROPE) ** -0.5 * np.log2(np.e)
    pad = np.zeros((t, HEAD_PAD - QK_NOPE - QK_ROPE))
    cosq = np.concatenate([np.full((t, QK_NOPE), scale), cos * scale, pad], axis=1)
    sinq = np.concatenate([np.zeros((t, QK_NOPE)), sin * scale, pad], axis=1)
    tq_t = np.concatenate([cosq, sinq], axis=1).T
    tk = np.concatenate([cos, sin, np.zeros((t, LANES - 2 * QK_ROPE))], axis=1)
    return jnp.asarray(tq_t, F32), jnp.asarray(tk, F32)


def _dft_angles(rows, t):
    k = np.arange(t, dtype=np.int64)
    return ((rows[:, None] * k[None, :]) % t).astype(np.float64) * (2.0 * np.pi / t)


def _dft_tables(t):
    ang = _dft_angles(np.arange(t, dtype=np.int64), t)
    scale = (t * FNET_CH) ** -0.5
    return jnp.asarray(np.cos(ang) * scale, F32).astype(BF16), jnp.asarray(np.sin(ang) * scale, F32).astype(BF16)


def _dft_half_tables(t):
    r = np.arange(MIX_TILE, dtype=np.int64)
    ang = _dft_angles(np.concatenate([r[0::2], r[1::2]]), t)[:, :t // 2]
    scale = (t * FNET_CH) ** -0.5
    ang_off = _dft_angles(np.arange(t // MIX_TILE, dtype=np.int64) * MIX_TILE, t)[:, :t // 2]
    off = np.concatenate([np.cos(ang_off), np.sin(ang_off)], axis=1)
    return jnp.asarray(np.cos(ang) * scale, F32), jnp.asarray(np.sin(ang) * scale, F32), jnp.asarray(off, F32)


def _block_diag(w):
    g, a, b = w.shape
    eye = jnp.eye(g, dtype=w.dtype)
    return (eye[:, None, :, None] * w[:, :, None, :]).reshape(g * a, g * b)


def kernel(x_prompt, x_sample, cache_ckv, cache_kpe, c, c_ctx, w_mod, b_mod, norm1_g, w_in, q_norm_g, w_uq,
           kv_norm_g, w_ukv, w_fmix, w_out, norm2_g, w_router, w_e_gate, w_e_up, w_e_down, final_g):
    assert w_mod.shape[0] == 1, "single-layer problem"
    n_ctx, t_ctx, _ = x_prompt.shape
    n_lat, t_lat, _ = x_sample.shape
    past = cache_ckv.shape[2]
    ctx_row = n_lat

    w_in0 = w_in[0]
    kpe_cols = w_in0[:, Q_LORA + KV_LORA:Q_LORA + KV_LORA + QK_ROPE]
    win = jnp.concatenate([w_in0[:, :Q_LORA + KV_LORA + QK_ROPE], _rot_half(kpe_cols),
                           jnp.zeros((D_MODEL, 512 - Q_LORA - KV_LORA - 2 * QK_ROPE), F32),
                           w_in0[:, Q_LORA + KV_LORA + QK_ROPE:]], axis=1).astype(BF16)
    wq3 = w_uq[0].reshape(Q_LORA, N_HEADS, QK_NOPE + QK_ROPE)
    qpad = jnp.zeros((Q_LORA, N_HEADS, HEAD_PAD - QK_NOPE - QK_ROPE), F32)
    wuq_main = jnp.concatenate([wq3, qpad], axis=2).reshape(Q_LORA, QK_W)
    wuq_rot = jnp.concatenate([jnp.zeros((Q_LORA, N_HEADS, QK_NOPE), F32), _rot_half(wq3[..., QK_NOPE:]), qpad],
                              axis=2).reshape(Q_LORA, QK_W)
    wuq_lat = jnp.concatenate([wuq_main, wuq_rot], axis=1).T.astype(BF16)
    wuq_ctx = wuq_main.T.astype(BF16)
    wkv3 = w_ukv[0].reshape(KV_LORA, N_HEADS, QK_NOPE + V_HEAD)
    wk_top = jnp.concatenate([wkv3[..., :QK_NOPE], jnp.zeros((KV_LORA, N_HEADS, HEAD_PAD - QK_NOPE), F32)],
                             axis=2).reshape(KV_LORA, QK_W)
    place = jnp.concatenate([jnp.zeros((QK_ROPE, QK_NOPE), F32), jnp.eye(QK_ROPE, dtype=F32),
                             jnp.zeros((QK_ROPE, HEAD_PAD - QK_NOPE - QK_ROPE), F32)], axis=1)
    place = jnp.tile(place, (1, N_HEADS))
    wk = jnp.concatenate([wk_top, place, place, jnp.zeros((LANES - 2 * QK_ROPE, QK_W), F32)], axis=0).astype(BF16)
    wv = wkv3[..., QK_NOPE:].reshape(KV_LORA, V_W).T.astype(BF16)
    wo = w_out[0].astype(BF16)
    wr_t = w_router[0].T.astype(BF16)

    cos, sin = _rope_tables(t_lat)
    tq_lat, tk_lat = _qk_tables(cos, sin)
    assert n_ctx % CTX_REQS == 0
    tq_ctx, tk_ctx = _qk_tables(np.ones((CTX_REQS * t_ctx, QK_ROPE)), np.zeros((CTX_REQS * t_ctx, QK_ROPE)))
    ch_ang = _dft_angles(np.arange(FNET_CH, dtype=np.int64), FNET_CH)
    dft_ctx = _dft_tables(t_ctx)
    dft_lat = _dft_half_tables(t_lat)

    c8 = jnp.concatenate([c, c_ctx[None, :], jnp.zeros((8 - n_lat - 1, D_MODEL), F32)], axis=0)
    mods6 = _mods(c8, w_mod[0], b_mod[0][None, :]).reshape(8, 6, D_MODEL)
    cw, sw = _fold(jnp.asarray(np.cos(ch_ang), F32), jnp.asarray(np.sin(ch_ang), F32), w_fmix[0])
    wcs = jnp.concatenate([_block_diag(cw), _block_diag(sw)], axis=1).astype(BF16)

    g1 = norm1_g[0][None, :]
    qg = q_norm_g[0][None, :]
    kvg = kv_norm_g[0][None, :]
    g2 = norm2_g[0][None, :]
    fg = final_g[None, :]

    xp = x_prompt.reshape(n_ctx * t_ctx, D_MODEL)
    xs = x_sample.reshape(n_lat * t_lat, D_MODEL)
    tiles_lat = t_lat // MIX_TILE

    x1c, h2c, affc, ckv_c, kpe_c = _ctx_front(
        xp, mods6, g1, win, qg, wuq_ctx, kvg, wk, wv, wcs, tq_ctx, tk_ctx,
        *dft_ctx, wo, g2, wr_t, n_req=n_ctx, t=t_ctx, mod_row=ctx_row)
    ql, kl, vl, zcl, zsl = _premix(
        xs, mods6, g1, win, qg, wuq_lat, kvg, wk, wv, wcs, tq_lat, tk_lat,
        mod_row=lambda i: i // tiles_lat, tab_row=lambda i: i % tiles_lat, rope=True)
    xk_cache = jnp.concatenate([cache_ckv[:, 0], cache_kpe[:, 0],
                                jnp.zeros((n_lat, past, 2 * LANES - KV_LORA - QK_ROPE), F32)],
                               axis=-1).reshape(n_lat * past, 2 * LANES).astype(BF16)
    kpast, vpast = _cachekv(xk_cache, wk, wv)

    attn_l = _attention(ql, [kpast, kl], [vpast, vl], n_req=n_lat, t_q=t_lat, kv_lens=[past, t_lat], tq=TOK_TILE,
                        pairs_per_step=2)

    x1l, h2l, affl = _mixout(xs, attn_l, zcl, zsl, *dft_lat, wo, mods6, g2, wr_t,
                             n_req=n_lat, t=t_lat, mod_row=lambda b: b)

    cap_c = CAP_FACTOR * t_ctx // N_EXPERTS
    cap_l = CAP_FACTOR * t_lat // N_EXPERTS
    affc2 = affc.reshape(n_ctx * N_EXPERTS, t_ctx)
    affl2 = affl.reshape(n_lat * N_EXPERTS, t_lat)
    posc, _ = _route(affc2, cap_c)
    posl, offl = _route(affl2, cap_l)
    assert t_lat // TOK_TILE + 1 <= OFF_STRIDE and cap_l % SLOT_WIN == 0
    offl = offl[:, :OFF_STRIDE].reshape(-1)
    xsc, gc = _gather(posc, affc2, h2c, n_req=n_ctx, n=t_ctx, cap=cap_c, rps=MOE_REQS)
    xsl, gl = _gather_win(offl, posl, affl2, h2l, n_req=n_lat, n=t_lat, cap=cap_l)
    ysc, ysl = _ffn(xsc, xsl, gc, gl, w_e_gate[0], w_e_up[0], w_e_down[0])

    posc_tok = posc.reshape(n_ctx, N_EXPERTS, t_ctx).transpose(0, 2, 1)
    posl_tok = posl.reshape(n_lat, N_EXPERTS, t_lat).transpose(0, 2, 1)
    y_prompt = _combine(x1c, posc_tok, ysc, mods6, fg, n_req=n_ctx, n=t_ctx, cap=cap_c, mod_row=ctx_row,
                        rps=MOE_REQS)
    y_sample = _combine_win(offl, x1l, posl_tok, ysl, mods6, fg, n_req=n_lat, n=t_lat, cap=cap_l,
                            mod_row=lambda b: b)

    return (y_prompt.reshape(n_ctx, t_ctx, D_MODEL), y_sample.reshape(n_lat, t_lat, D_MODEL),
            ckv_c.reshape(n_ctx, 1, t_ctx, KV_LORA), kpe_c.reshape(n_ctx, 1, t_ctx, QK_ROPE))
```

```python
import functools

import jax
import jax.numpy as jnp
import numpy as np
from jax import lax
from jax.experimental import pallas as pl
from jax.experimental.pallas import tpu as pltpu

F32 = jnp.float32
BF16 = jnp.bfloat16

D_MODEL = 1024
N_HEADS = 8
QK_NOPE = 64
QK_ROPE = 32
V_HEAD = 64
Q_LORA = 256
KV_LORA = 128
FNET_GROUPS = 8
FNET_CH = 64
FNET_W = FNET_GROUPS * FNET_CH
N_EXPERTS = 16
CAP_FACTOR = 2
D_EXPERT = 512
GRID_W = 64
ROPE_BASE = 10000.0
EPS = 1e-6

LANES = 128
HEAD_PAD = LANES
QK_W = N_HEADS * HEAD_PAD
V_W = N_HEADS * V_HEAD
TOK_TILE = 256
MIX_TILE = 512
CTX_REQS = 2
MOE_REQS = 4
KEY_CHUNK = 512
SLOT_WIN = 64
OFF_STRIDE = 16
VMEM_LIMIT = 48 * 1024 * 1024

_NT = (((1,), (1,)), ((), ()))


def _cparams(sem):
    return pltpu.CompilerParams(dimension_semantics=sem, vmem_limit_bytes=VMEM_LIMIT)


def _rms(x, g):
    return x * lax.rsqrt(jnp.mean(x * x, axis=-1, keepdims=True) + EPS) * g


def _dot(a, b):
    return jnp.dot(a, b, preferred_element_type=F32)


def _mods_kernel(c_ref, w_ref, b_ref, o_ref):
    c = c_ref[...]
    s = c * jax.nn.sigmoid(c)
    o_ref[...] = _dot(s.astype(BF16), w_ref[...].astype(BF16)) + b_ref[...]


def _mods(c8, w_mod, b_mod):
    n = w_mod.shape[1]
    tn = 1536
    return pl.pallas_call(
        _mods_kernel,
        grid=(n // tn,),
        in_specs=[pl.BlockSpec((8, D_MODEL), lambda j: (0, 0)),
                  pl.BlockSpec((D_MODEL, tn), lambda j: (0, j)),
                  pl.BlockSpec((1, tn), lambda j: (0, j))],
        out_specs=pl.BlockSpec((8, tn), lambda j: (0, j)),
        out_shape=jax.ShapeDtypeStruct((8, n), F32),
        compiler_params=_cparams(("arbitrary",)),
        name="mods",
    )(c8, w_mod, b_mod)


def _fold_kernel(cc_ref, sc_ref, w_ref, cw_ref, sw_ref):
    for g in range(FNET_GROUPS):
        w = w_ref[g]
        cw_ref[g] = jnp.dot(cc_ref[...], w, preferred_element_type=F32, precision=lax.Precision.HIGHEST)
        sw_ref[g] = jnp.dot(sc_ref[...], w, preferred_element_type=F32, precision=lax.Precision.HIGHEST)


def _fold(cc, sc, w_fmix):
    shp = jax.ShapeDtypeStruct((FNET_GROUPS, FNET_CH, FNET_CH), F32)
    return pl.pallas_call(_fold_kernel, out_shape=(shp, shp), name="fold")(cc, sc, w_fmix)


def _premix_body(x, m_ref, g1_ref, win_ref, qg_ref, wuq_ref, kvg_ref, wk_ref, wv_ref, wcs_ref, tq_ref, tk_ref, rope):
    shift1 = m_ref[0, 0:1, :]
    scale1 = m_ref[0, 1:2, :]
    h = _rms(x, g1_ref[...]) * (1.0 + scale1) + shift1
    proj = _dot(h.astype(BF16), win_ref[...])
    qn = _rms(proj[:, 0:Q_LORA], qg_ref[...]).astype(BF16)
    qq = lax.dot_general(wuq_ref[...], qn, _NT, preferred_element_type=F32)
    cosq = tq_ref[0:LANES, :]
    sinq = tq_ref[LANES:2 * LANES, :]
    q_heads = []
    for hd in range(N_HEADS):
        lo = hd * HEAD_PAD
        qh = qq[lo:lo + HEAD_PAD, :] * cosq
        if rope:
            qh = qh + qq[QK_W + lo:QK_W + lo + HEAD_PAD, :] * sinq
        q_heads.append(qh.astype(BF16))
    ckv = _rms(proj[:, Q_LORA:Q_LORA + KV_LORA], kvg_ref[...])
    kpe2 = proj[:, Q_LORA + KV_LORA:Q_LORA + KV_LORA + LANES] * tk_ref[...]
    xk = jnp.concatenate([ckv, kpe2], axis=1).astype(BF16)
    k = _dot(xk, wk_ref[...]).astype(BF16)
    v_t = lax.dot_general(wv_ref[...], xk[:, 0:KV_LORA], _NT, preferred_element_type=F32).astype(BF16)
    hw = FNET_W // 2
    f_in = proj[:, 512:1024].astype(BF16)
    z = [_dot(f_in[:, a * hw:(a + 1) * hw], wcs_ref[a * hw:(a + 1) * hw, :]) for a in range(2)]
    zc = jnp.concatenate([za[:, 0:hw] for za in z], axis=1).astype(BF16)
    zs = jnp.concatenate([za[:, hw:2 * hw] for za in z], axis=1).astype(BF16)
    kpe = proj[:, Q_LORA + KV_LORA:Q_LORA + KV_LORA + QK_ROPE]
    return q_heads, k, v_t, zc, zs, ckv, kpe


def _premix_kernel(x_ref, m_ref, g1_ref, win_ref, qg_ref, wuq_ref, kvg_ref, wk_ref, wv_ref, wcs_ref,
                   tq_ref, tk_ref, q_ref, k_ref, v_ref, zc_ref, zs_ref, *, rope):
    q_heads, k, v_t, zc, zs, _, _ = _premix_body(x_ref[...], m_ref, g1_ref, win_ref, qg_ref, wuq_ref, kvg_ref,
                                                 wk_ref, wv_ref, wcs_ref, tq_ref, tk_ref, rope)
    for hd, qh in enumerate(q_heads):
        q_ref[hd * HEAD_PAD:(hd + 1) * HEAD_PAD, :] = qh
    k_ref[...] = k
    v_ref[...] = v_t
    zc_ref[...] = zc
    zs_ref[...] = zs


def _premix(x, mods6, g1, win, qg, wuq, kvg, wk, wv, wcs, tq, tk, *, mod_row, tab_row, rope):
    n = x.shape[0]
    tm = MIX_TILE
    full = lambda a: pl.BlockSpec(a.shape, lambda i: (0,) * a.ndim)
    out_shape = [jax.ShapeDtypeStruct((QK_W, n), BF16), jax.ShapeDtypeStruct((n, QK_W), BF16),
                 jax.ShapeDtypeStruct((V_W, n), BF16), jax.ShapeDtypeStruct((n, FNET_W), BF16),
                 jax.ShapeDtypeStruct((n, FNET_W), BF16)]
    out_specs = [pl.BlockSpec((QK_W, tm), lambda i: (0, i)), pl.BlockSpec((tm, QK_W), lambda i: (i, 0)),
                 pl.BlockSpec((V_W, tm), lambda i: (0, i)), pl.BlockSpec((tm, FNET_W), lambda i: (i, 0)),
                 pl.BlockSpec((tm, FNET_W), lambda i: (i, 0))]
    return pl.pallas_call(
        functools.partial(_premix_kernel, rope=rope),
        grid=(n // tm,),
        in_specs=[pl.BlockSpec((tm, D_MODEL), lambda i: (i, 0)),
                  pl.BlockSpec((1, 6, D_MODEL), lambda i: (mod_row(i), 0, 0)),
                  full(g1), full(win), full(qg), full(wuq), full(kvg), full(wk), full(wv), full(wcs),
                  pl.BlockSpec((2 * LANES, tm), lambda i: (0, tab_row(i))),
                  pl.BlockSpec((tm, LANES), lambda i: (tab_row(i), 0))],
        out_specs=out_specs,
        out_shape=out_shape,
        compiler_params=_cparams(("parallel",)),
        name="premix",
    )(x, mods6, g1, win, qg, wuq, kvg, wk, wv, wcs, tq, tk)


def _cachekv_kernel(xk_ref, wk_ref, wv_ref, k_ref, v_ref):
    xk = xk_ref[...]
    k_ref[...] = _dot(xk, wk_ref[...]).astype(BF16)
    v_ref[...] = lax.dot_general(wv_ref[...], xk[:, 0:KV_LORA], _NT, preferred_element_type=F32).astype(BF16)


def _cachekv(xk, wk, wv):
    n = xk.shape[0]
    tm = 512
    full = lambda a: pl.BlockSpec(a.shape, lambda i: (0,) * a.ndim)
    return pl.pallas_call(
        _cachekv_kernel,
        grid=(n // tm,),
        in_specs=[pl.BlockSpec((tm, 2 * LANES), lambda i: (i, 0)), full(wk), full(wv)],
        out_specs=[pl.BlockSpec((tm, QK_W), lambda i: (i, 0)), pl.BlockSpec((V_W, tm), lambda i: (0, i))],
        out_shape=[jax.ShapeDtypeStruct((n, QK_W), BF16), jax.ShapeDtypeStruct((V_W, n), BF16)],
        compiler_params=_cparams(("parallel",)),
        name="cachekv",
    )(xk, wk, wv)


def _attn_body(q_heads, k_refs, v_refs, kc):
    tq = q_heads[0].shape[1]
    zero = jnp.zeros((HEAD_PAD, tq), BF16)
    n_pairs = len(q_heads) // 2
    qbd = [jnp.concatenate([jnp.concatenate([q_heads[2 * pr], zero], axis=1),
                            jnp.concatenate([zero, q_heads[2 * pr + 1]], axis=1)], axis=0) for pr in range(n_pairs)]
    chunks = [(k_ref, v_ref, c0, min(c0 + kc, k_ref.shape[0]))
              for k_ref, v_ref in zip(k_refs, v_refs) for c0 in range(0, k_ref.shape[0], kc)]
    work = [(pr, ch) for pr in range(n_pairs) for ch in chunks]

    def score(item):
        pr, (k_ref, _, c0, c1) = item
        return _dot(k_ref[c0:c1, pr * 2 * HEAD_PAD:(pr + 1) * 2 * HEAD_PAD], qbd[pr])

    m = [None] * n_pairs
    o = [None] * n_pairs
    s_next = score(work[0])
    for wi, (pr, (k_ref, v_ref, c0, c1)) in enumerate(work):
        s = s_next
        if wi + 1 < len(work):
            s_next = score(work[wi + 1])
        cm = jnp.max(s, axis=0, keepdims=True)
        vlo = pr * 2 * V_HEAD
        va = jnp.concatenate([v_ref[vlo:vlo + 2 * V_HEAD, c0:c1], jnp.ones((16, c1 - c0), BF16)], axis=0)
        if m[pr] is None:
            m[pr] = cm
            o[pr] = _dot(va, jnp.exp2((s - cm).astype(BF16)))
        else:
            m_new = jnp.maximum(m[pr], cm)
            alpha = jnp.exp2(m[pr] - m_new)
            o[pr] = alpha * o[pr] + _dot(va, jnp.exp2((s - m_new).astype(BF16)))
            m[pr] = m_new
    outs = []
    for pr in range(n_pairs):
        on = o[pr][0:2 * V_HEAD, :] * (1.0 / o[pr][2 * V_HEAD:2 * V_HEAD + 1, :])
        ot = jnp.concatenate([on[0:V_HEAD, 0:tq], on[V_HEAD:2 * V_HEAD, tq:2 * tq]], axis=0)
        outs.append(ot.T.astype(BF16))
    return outs


def _attn_kernel(q_ref, *refs, n_kv, n_pairs, kc):
    q_heads = [q_ref[hd * HEAD_PAD:(hd + 1) * HEAD_PAD, :] for hd in range(2 * n_pairs)]
    outs = _attn_body(q_heads, refs[:n_kv], refs[n_kv:2 * n_kv], kc)
    o_ref = refs[2 * n_kv]
    for pr, o in enumerate(outs):
        o_ref[:, pr * 2 * V_HEAD:(pr + 1) * 2 * V_HEAD] = o


def _attention(q_t, ks, vs_t, *, n_req, t_q, kv_lens, tq, pairs_per_step):
    n_kv = len(ks)
    nq = t_q // tq
    pp = pairs_per_step
    in_specs = [pl.BlockSpec((pp * 2 * HEAD_PAD, tq), lambda b, p, i: (p, b * nq + i))]
    in_specs += [pl.BlockSpec((kl, pp * 2 * HEAD_PAD), lambda b, p, i: (b, p)) for kl in kv_lens]
    in_specs += [pl.BlockSpec((pp * 2 * V_HEAD, kl), lambda b, p, i: (p, b)) for kl in kv_lens]
    return pl.pallas_call(
        functools.partial(_attn_kernel, n_kv=n_kv, n_pairs=pp, kc=KEY_CHUNK),
        grid=(n_req, N_HEADS // 2 // pp, nq),
        in_specs=in_specs,
        out_specs=pl.BlockSpec((tq, pp * 2 * V_HEAD), lambda b, p, i: (b * nq + i, p)),
        out_shape=jax.ShapeDtypeStruct((n_req * t_q, V_W), BF16),
        compiler_params=_cparams(("parallel", "parallel", "parallel")),
        name="attn",
    )(q_t, *ks, *vs_t)


def _mixout_body(x, attn_pairs, fm, wo_ref, m_ref, g2_ref, wr_ref):
    y = _dot(fm, wo_ref[V_W:V_W + FNET_W, :])
    col = 0
    for a in attn_pairs:
        y = y + _dot(a, wo_ref[col:col + a.shape[1], :])
        col += a.shape[1]
    gate1 = m_ref[0, 2:3, :]
    shift2 = m_ref[0, 3:4, :]
    scale2 = m_ref[0, 4:5, :]
    x1 = x + gate1 * y
    h2 = (_rms(x1, g2_ref[...]) * (1.0 + scale2) + shift2).astype(BF16)
    lg = lax.dot_general(wr_ref[...], h2, _NT, preferred_element_type=F32)
    e = jnp.exp(lg - jnp.max(lg, axis=0, keepdims=True))
    return x1, h2, e / jnp.sum(e, axis=0, keepdims=True)


def _mixout_kernel(x_ref, a_ref, zc_ref, zs_ref, cb_ref, sb_ref, off_ref, wo_ref, m_ref, g2_ref, wr_ref,
                   x1_ref, h2_ref, aff_ref, zp_ref, zm_ref, *, t):
    half = t // 2
    i = pl.program_id(1)

    @pl.when(i == 0)
    def _fold_halves():
        for src, col in ((zc_ref, 0), (zs_ref, FNET_W)):
            lo = src[0:half, :]
            hi = src[half:t, :]
            zp_ref[:, col:col + FNET_W] = lo + hi
            zm_ref[:, col:col + FNET_W] = lo - hi

    off = off_ref[pl.ds(i, 1), :]
    co = off[:, 0:half]
    so = off[:, half:t]
    cb = cb_ref[...]
    sb = sb_ref[...]
    ct = (cb * co - sb * so).astype(BF16)
    st = (sb * co + cb * so).astype(BF16)
    tr = cb.shape[0]
    h = tr // 2
    even = _dot(ct[0:h, :], zp_ref[:, 0:FNET_W]) - _dot(st[0:h, :], zp_ref[:, FNET_W:2 * FNET_W])
    odd = _dot(ct[h:, :], zm_ref[:, 0:FNET_W]) - _dot(st[h:, :], zm_ref[:, FNET_W:2 * FNET_W])
    k_i = lax.broadcasted_iota(jnp.int32, (tr, tr), 0)
    j_i = lax.broadcasted_iota(jnp.int32, (tr, tr), 1)
    perm = jnp.where(j_i == (k_i >> 1) + h * (k_i & 1), 1.0, 0.0).astype(BF16)
    fm = _dot(perm, jnp.concatenate([even, odd], axis=0).astype(BF16)).astype(BF16)
    x1, h2, aff = _mixout_body(x_ref[...], [a_ref[...]], fm, wo_ref, m_ref, g2_ref, wr_ref)
    x1_ref[...] = x1
    h2_ref[...] = h2
    aff_ref[...] = aff


def _ctx_front_kernel(x_ref, m_ref, g1_ref, win_ref, qg_ref, wuq_ref, kvg_ref, wk_ref, wv_ref, wcs_ref,
                      tq_ref, tk_ref, ct_ref, st_ref, wo_ref, g2_ref, wr_ref,
                      x1_ref, h2_ref, aff_ref, ckv_ref, kpe_ref, *, kc, t):
    x = x_ref[...]
    q_heads, k, v_t, zc, zs, ckv, kpe = _premix_body(x, m_ref, g1_ref, win_ref, qg_ref, wuq_ref, kvg_ref,
                                                     wk_ref, wv_ref, wcs_ref, tq_ref, tk_ref, False)
    ckv_ref[...] = ckv
    kpe_ref[...] = kpe
    attn, fm = [], []
    for r in range(x.shape[0] // t):
        rows = slice(r * t, (r + 1) * t)
        attn.append(jnp.concatenate(_attn_body([q[:, rows] for q in q_heads], [k[rows, :]], [v_t[:, rows]], kc),
                                    axis=1))
        fm.append((_dot(ct_ref[...], zc[rows, :]) - _dot(st_ref[...], zs[rows, :])).astype(BF16))
    x1, h2, aff = _mixout_body(x, [jnp.concatenate(attn, axis=0)], jnp.concatenate(fm, axis=0),
                               wo_ref, m_ref, g2_ref, wr_ref)
    x1_ref[...] = x1
    h2_ref[...] = h2
    for r in range(x.shape[0] // t):
        aff_ref[r] = aff[:, r * t:(r + 1) * t]


def _ctx_front(x, mods6, g1, win, qg, wuq, kvg, wk, wv, wcs, tq, tk, ct, st, wo, g2, wr_t, *, n_req, t, mod_row):
    full = lambda a: pl.BlockSpec(a.shape, lambda b: (0,) * a.ndim)
    rps = CTX_REQS
    row = lambda w: pl.BlockSpec((rps * t, w), lambda b: (b, 0))
    return pl.pallas_call(
        functools.partial(_ctx_front_kernel, kc=KEY_CHUNK, t=t),
        grid=(n_req // rps,),
        in_specs=[row(D_MODEL), pl.BlockSpec((1, 6, D_MODEL), lambda b: (mod_row, 0, 0)),
                  full(g1), full(win), full(qg), full(wuq), full(kvg), full(wk), full(wv), full(wcs),
                  full(tq), full(tk), full(ct), full(st), full(wo), full(g2), full(wr_t)],
        out_specs=[row(D_MODEL), row(D_MODEL), pl.BlockSpec((rps, N_EXPERTS, t), lambda b: (b, 0, 0)),
                   row(KV_LORA), row(QK_ROPE)],
        out_shape=[jax.ShapeDtypeStruct((n_req * t, D_MODEL), F32),
                   jax.ShapeDtypeStruct((n_req * t, D_MODEL), BF16),
                   jax.ShapeDtypeStruct((n_req, N_EXPERTS, t), F32),
                   jax.ShapeDtypeStruct((n_req * t, KV_LORA), F32),
                   jax.ShapeDtypeStruct((n_req * t, QK_ROPE), F32)],
        compiler_params=_cparams(("parallel",)),
        name="ctx_front",
    )(x, mods6, g1, win, qg, wuq, kvg, wk, wv, wcs, tq, tk, ct, st, wo, g2, wr_t)


def _mixout(x, attn, zc, zs, cb, sb, off, wo, mods6, g2, wr_t, *, n_req, t, mod_row):
    tr = MIX_TILE
    nr = t // tr
    full = lambda a: pl.BlockSpec(a.shape, lambda b, i: (0,) * a.ndim)
    return pl.pallas_call(
        functools.partial(_mixout_kernel, t=t),
        grid=(n_req, nr),
        in_specs=[pl.BlockSpec((tr, D_MODEL), lambda b, i: (b * nr + i, 0)),
                  pl.BlockSpec((tr, V_W), lambda b, i: (b * nr + i, 0)),
                  pl.BlockSpec((t, FNET_W), lambda b, i: (b, 0)),
                  pl.BlockSpec((t, FNET_W), lambda b, i: (b, 0)),
                  full(cb), full(sb), full(off),
                  full(wo),
                  pl.BlockSpec((1, 6, D_MODEL), lambda b, i: (mod_row(b), 0, 0)),
                  full(g2), full(wr_t)],
        out_specs=[pl.BlockSpec((tr, D_MODEL), lambda b, i: (b * nr + i, 0)),
                   pl.BlockSpec((tr, D_MODEL), lambda b, i: (b * nr + i, 0)),
                   pl.BlockSpec((None, N_EXPERTS, tr), lambda b, i: (b, 0, i))],
        out_shape=[jax.ShapeDtypeStruct((n_req * t, D_MODEL), F32),
                   jax.ShapeDtypeStruct((n_req * t, D_MODEL), BF16),
                   jax.ShapeDtypeStruct((n_req, N_EXPERTS, t), F32)],
        scratch_shapes=[pltpu.VMEM((t // 2, 2 * FNET_W), BF16), pltpu.VMEM((t // 2, 2 * FNET_W), BF16)],
        compiler_params=_cparams(("parallel", "arbitrary")),
        name="mixout",
    )(x, attn, zc, zs, cb, sb, off, wo, mods6, g2, wr_t)


def _prefix_count(flags, tri):
    n = flags.shape[1]
    carry = None
    outs = []
    ends = []
    for j in range(n // TOK_TILE):
        c = _dot(flags[:, j * TOK_TILE:(j + 1) * TOK_TILE].astype(BF16), tri)
        if carry is not None:
            c = c + carry
        outs.append(c)
        carry = c[:, TOK_TILE - 1:TOK_TILE]
        ends.append(carry)
    return (outs[0] if len(outs) == 1 else jnp.concatenate(outs, axis=1)), ends


def _route_kernel(aff_ref, pos_ref, off_ref, *, cap):
    a = aff_ref[...]
    rows = a.shape[0]
    capf = jnp.float32(cap)
    thr = jnp.zeros((rows, 1), jnp.int32)
    for bit in range(30, -1, -1):
        cand = thr | jnp.int32(1 << bit)
        cand_f = lax.bitcast_convert_type(cand, F32)
        cnt = jnp.sum(jnp.where(a >= cand_f, 1.0, 0.0), axis=1, keepdims=True)
        thr = jnp.where(cnt >= capf, cand, thr)
    thr_f = lax.bitcast_convert_type(thr, F32)
    above_f = lax.bitcast_convert_type(thr + 1, F32)
    gt = jnp.where(a >= above_f, 1.0, 0.0)
    tie = jnp.where(a >= thr_f, 1.0, 0.0) - gt
    need = capf - jnp.sum(gt, axis=1, keepdims=True)
    r_i = lax.broadcasted_iota(jnp.int32, (TOK_TILE, TOK_TILE), 0)
    c_i = lax.broadcasted_iota(jnp.int32, (TOK_TILE, TOK_TILE), 1)
    tri = jnp.where(r_i <= c_i, 1.0, 0.0).astype(BF16)
    tie_before = _prefix_count(tie, tri)[0] - tie
    sel = gt + tie * jnp.where(tie_before < need, 1.0, 0.0)
    count, ends = _prefix_count(sel, tri)
    pos_ref[...] = jnp.where(sel > 0.5, count - 1.0, -1.0)
    lane = lax.broadcasted_iota(jnp.int32, (rows, LANES), 1)
    offs = jnp.zeros((rows, LANES), F32)
    for j, end in enumerate(ends):
        offs = offs + jnp.where(lane == j + 1, end, 0.0)
    off_ref[...] = offs.astype(jnp.int32)


def _route(aff_t, cap):
    return pl.pallas_call(
        functools.partial(_route_kernel, cap=cap),
        out_shape=[jax.ShapeDtypeStruct(aff_t.shape, F32),
                   jax.ShapeDtypeStruct((aff_t.shape[0], LANES), jnp.int32)],
        compiler_params=pltpu.CompilerParams(vmem_limit_bytes=VMEM_LIMIT),
        name="route",
    )(aff_t)


def _gather_kernel(pos_ref, aff_ref, h_ref, xs_ref, g_ref, *, cap):
    rps, ne, n = pos_ref.shape
    slot = lax.broadcasted_iota(jnp.int32, (ne, cap, n), 1).astype(F32)
    for r in range(rps):
        pos = pos_ref[r]
        aff = aff_ref[r]
        hit = pos[:, None, :] == slot
        onehot = jnp.where(hit, 1.0, 0.0).reshape(ne * cap, n).astype(BF16)
        xs = _dot(onehot, h_ref[r * n:(r + 1) * n, :])
        xs_ref[:, r * cap:(r + 1) * cap, :] = xs.astype(BF16).reshape(ne, cap, D_MODEL)
        g_ref[:, r * cap:(r + 1) * cap, :] = jnp.sum(jnp.where(hit, aff[:, None, :], 0.0), axis=2, keepdims=True)


def _gather(pos_t, aff_t, h2, *, n_req, n, cap, rps):
    pos3 = pos_t.reshape(n_req, N_EXPERTS, n)
    aff3 = aff_t.reshape(n_req, N_EXPERTS, n)
    return pl.pallas_call(
        functools.partial(_gather_kernel, cap=cap),
        grid=(n_req // rps,),
        in_specs=[pl.BlockSpec((rps, N_EXPERTS, n), lambda b: (b, 0, 0)),
                  pl.BlockSpec((rps, N_EXPERTS, n), lambda b: (b, 0, 0)),
                  pl.BlockSpec((rps * n, D_MODEL), lambda b: (b, 0))],
        out_specs=[pl.BlockSpec((N_EXPERTS, rps * cap, D_MODEL), lambda b: (0, b, 0)),
                   pl.BlockSpec((N_EXPERTS, rps * cap, 1), lambda b: (0, b, 0))],
        out_shape=[jax.ShapeDtypeStruct((N_EXPERTS, n_req * cap, D_MODEL), BF16),
                   jax.ShapeDtypeStruct((N_EXPERTS, n_req * cap, 1), F32)],
        compiler_params=_cparams(("parallel",)),
        name="gather",
    )(pos3, aff3, h2)


def _window_plan(off_ref, b, j, cap, w):
    w0 = []
    need = jnp.int32(0)
    for e in range(N_EXPERTS):
        idx = (b * N_EXPERTS + e) * OFF_STRIDE + j
        base = (off_ref[idx] >> 4) << 4
        w0.append(base)
        need = jnp.maximum(need, off_ref[idx + 1] - base)
    return w0, (need + (w - 1)) >> (w.bit_length() - 1)


def _window(w0_e, p, cap, w):
    low = w0_e + p * w
    start = pl.multiple_of(jnp.minimum(low, cap - w), 16)
    return low, start


def _gather_win_kernel(off_ref, pos_ref, aff_ref, h_ref, xs_ref, g_ref, *, cap, w):
    b = pl.program_id(0)
    j = pl.program_id(1)

    @pl.when(j == 0)
    def _init():
        xs_ref[...] = jnp.zeros_like(xs_ref)
        g_ref[...] = jnp.zeros_like(g_ref)

    pos = pos_ref[...]
    aff = aff_ref[...]
    h = h_ref[...]
    w0, n_pass = _window_plan(off_ref, b, j, cap, w)
    r = lax.broadcasted_iota(jnp.int32, (w, 1), 0).astype(F32)

    def one_pass(p, carry):
        starts, hots, gates = [], [], []
        for e in range(N_EXPERTS):
            low, start = _window(w0[e], p, cap, w)
            starts.append(start)
            mine = jnp.where(r >= (low - start).astype(F32), 1.0, 0.0)
            hot = jnp.where(pos[e:e + 1, :] - start.astype(F32) == r, mine, 0.0)
            hots.append(hot)
            gates.append(jnp.sum(hot * aff[e:e + 1, :], axis=1, keepdims=True))
        rows = _dot(jnp.concatenate(hots, axis=0).astype(BF16), h).astype(BF16)
        for e in range(N_EXPERTS):
            win = pl.ds(starts[e], w)
            xs_ref[e, win, :] = xs_ref[e, win, :] + rows[e * w:(e + 1) * w, :]
            g_ref[e, win, :] = g_ref[e, win, :] + gates[e]
        return carry

    one_pass(0, 0)
    lax.fori_loop(1, n_pass, one_pass, 0)


def _gather_win(offs, pos_t, aff_t, h2, *, n_req, n, cap):
    nb = n // TOK_TILE
    grid_spec = pltpu.PrefetchScalarGridSpec(
        num_scalar_prefetch=1,
        grid=(n_req, nb),
        in_specs=[pl.BlockSpec((N_EXPERTS, TOK_TILE), lambda b, j, off: (b, j)),
                  pl.BlockSpec((N_EXPERTS, TOK_TILE), lambda b, j, off: (b, j)),
                  pl.BlockSpec((TOK_TILE, D_MODEL), lambda b, j, off: (b * nb + j, 0))],
        out_specs=[pl.BlockSpec((N_EXPERTS, cap, D_MODEL), lambda b, j, off: (0, b, 0)),
                   pl.BlockSpec((N_EXPERTS, cap, 1), lambda b, j, off: (0, b, 0))])
    return pl.pallas_call(
        functools.partial(_gather_win_kernel, cap=cap, w=SLOT_WIN),
        grid_spec=grid_spec,
        out_shape=[jax.ShapeDtypeStruct((N_EXPERTS, n_req * cap, D_MODEL), BF16),
                   jax.ShapeDtypeStruct((N_EXPERTS, n_req * cap, 1), F32)],
        compiler_params=_cparams(("parallel", "arbitrary")),
        name="gather_win",
    )(offs, pos_t, aff_t, h2)


def _ffn_kernel(xc_ref, xl_ref, gc_ref, gl_ref, wg_ref, wu_ref, wd_ref, yc_ref, yl_ref, wgb, wub, wdb):
    wgb[...] = wg_ref[0].astype(BF16)
    wub[...] = wu_ref[0].astype(BF16)
    wdb[...] = wd_ref[0].astype(BF16)
    for x_ref, g_ref, y_ref in ((xc_ref, gc_ref, yc_ref), (xl_ref, gl_ref, yl_ref)):
        for j in range(x_ref.shape[1] // TOK_TILE):
            rows = slice(j * TOK_TILE, (j + 1) * TOK_TILE)
            x = x_ref[0, rows, :]
            gate = _dot(x, wgb[...])
            up = _dot(x, wub[...])
            hid = (gate * jax.nn.sigmoid(gate) * up).astype(BF16)
            ys = _dot(hid, wdb[...]) * g_ref[0, rows, :]
            y_ref[0, rows, :] = ys.astype(BF16)


def _ffn(xc, xl, gc, gl, wg, wu, wd):
    m = xc.shape[1]
    xspec = pl.BlockSpec((1, m, D_MODEL), lambda e: (e, 0, 0))
    gspec = pl.BlockSpec((1, m, 1), lambda e: (e, 0, 0))
    shp = jax.ShapeDtypeStruct((N_EXPERTS, m, D_MODEL), BF16)
    return pl.pallas_call(
        _ffn_kernel,
        grid=(N_EXPERTS,),
        in_specs=[xspec, xspec, gspec, gspec,
                  pl.BlockSpec((1, D_MODEL, D_EXPERT), lambda e: (e, 0, 0)),
                  pl.BlockSpec((1, D_MODEL, D_EXPERT), lambda e: (e, 0, 0)),
                  pl.BlockSpec((1, D_EXPERT, D_MODEL), lambda e: (e, 0, 0))],
        out_specs=[xspec, xspec],
        out_shape=[shp, shp],
        scratch_shapes=[pltpu.VMEM((D_MODEL, D_EXPERT), BF16), pltpu.VMEM((D_MODEL, D_EXPERT), BF16),
                        pltpu.VMEM((D_EXPERT, D_MODEL), BF16)],
        compiler_params=_cparams(("arbitrary",)),
        name="ffn",
    )(xc, xl, gc, gl, wg, wu, wd)


def _combine_kernel(x1_ref, pos_ref, ys_ref, m_ref, fg_ref, o_ref, *, cap):
    rps, n, _ = pos_ref.shape
    w = N_EXPERTS * cap
    e_i = lax.broadcasted_iota(jnp.int32, (N_EXPERTS, w), 0)
    j_i = lax.broadcasted_iota(jnp.int32, (N_EXPERTS, w), 1)
    spread = jnp.where((j_i >> (cap.bit_length() - 1)) == e_i, 1.0, 0.0).astype(BF16)
    lane_slot = (lax.broadcasted_iota(jnp.int32, (1, w), 1) & (cap - 1)).astype(F32)
    gate2 = m_ref[0, 5:6, :]
    for r in range(rps):
        pos = pos_ref[r].astype(BF16)
        onehot = jnp.where(_dot(pos, spread) == lane_slot, 1.0, 0.0).astype(BF16)
        acc = _dot(onehot, ys_ref[:, r * cap:(r + 1) * cap, :].reshape(w, D_MODEL))
        rows = slice(r * n, (r + 1) * n)
        o_ref[rows, :] = _rms(x1_ref[rows, :] + gate2 * acc, fg_ref[...])


def _combine(x1, pos_tok, ys, mods6, fg, *, n_req, n, cap, mod_row, rps):
    return pl.pallas_call(
        functools.partial(_combine_kernel, cap=cap),
        grid=(n_req // rps,),
        in_specs=[pl.BlockSpec((rps * n, D_MODEL), lambda b: (b, 0)),
                  pl.BlockSpec((rps, n, N_EXPERTS), lambda b: (b, 0, 0)),
                  pl.BlockSpec((N_EXPERTS, rps * cap, D_MODEL), lambda b: (0, b, 0)),
                  pl.BlockSpec((1, 6, D_MODEL), lambda b: (mod_row, 0, 0)),
                  pl.BlockSpec((1, D_MODEL), lambda b: (0, 0))],
        out_specs=pl.BlockSpec((rps * n, D_MODEL), lambda b: (b, 0)),
        out_shape=jax.ShapeDtypeStruct((n_req * n, D_MODEL), F32),
        compiler_params=_cparams(("parallel",)),
        name="combine",
    )(x1, pos_tok, ys, mods6, fg)


def _combine_win_kernel(off_ref, x1_ref, pos_ref, ys_ref, m_ref, fg_ref, o_ref, acc_ref, *, cap, w):
    b = pl.program_id(0)
    j = pl.program_id(1)
    pos = pos_ref[...]
    w0, n_pass = _window_plan(off_ref, b, j, cap, w)
    width = N_EXPERTS * w
    e_i = lax.broadcasted_iota(jnp.int32, (N_EXPERTS, width), 0)
    j_i = lax.broadcasted_iota(jnp.int32, (N_EXPERTS, width), 1)
    spread = jnp.where((j_i >> (w.bit_length() - 1)) == e_i, 1.0, 0.0).astype(BF16)
    lane_slot = (lax.broadcasted_iota(jnp.int32, (1, width), 1) & (w - 1)).astype(F32)
    lane_e = lax.broadcasted_iota(jnp.int32, (1, N_EXPERTS), 1)

    def window_sum(p):
        start_row = jnp.zeros((1, N_EXPERTS), F32)
        first_row = jnp.zeros((1, N_EXPERTS), F32)
        wins = []
        for e in range(N_EXPERTS):
            low, start = _window(w0[e], p, cap, w)
            start_row = jnp.where(lane_e == e, start.astype(F32), start_row)
            first_row = jnp.where(lane_e == e, (low - start).astype(F32), first_row)
            wins.append(ys_ref[e, pl.ds(start, w), :])
        rel = pos - start_row
        rel = jnp.where(rel >= first_row, rel, -1.0).astype(BF16)
        onehot = jnp.where(_dot(rel, spread) == lane_slot, 1.0, 0.0).astype(BF16)
        return _dot(onehot, jnp.concatenate(wins, axis=0))

    def finish(acc):
        o_ref[...] = _rms(x1_ref[...] + m_ref[0, 5:6, :] * acc, fg_ref[...])

    acc0 = window_sum(0)
    acc_ref[...] = acc0
    finish(acc0)

    @pl.when(n_pass > 1)
    def _more_passes():
        def one_pass(p, carry):
            acc_ref[...] += window_sum(p)
            return carry

        lax.fori_loop(1, n_pass, one_pass, 0)
        finish(acc_ref[...])


def _combine_win(offs, x1, pos_tok, ys, mods6, fg, *, n_req, n, cap, mod_row):
    tr = TOK_TILE
    nr = n // tr
    grid_spec = pltpu.PrefetchScalarGridSpec(
        num_scalar_prefetch=1,
        grid=(n_req, nr),
        in_specs=[pl.BlockSpec((tr, D_MODEL), lambda b, i, off: (b * nr + i, 0)),
                  pl.BlockSpec((None, tr, N_EXPERTS), lambda b, i, off: (b, i, 0)),
                  pl.BlockSpec((N_EXPERTS, cap, D_MODEL), lambda b, i, off: (0, b, 0)),
                  pl.BlockSpec((1, 6, D_MODEL), lambda b, i, off: (mod_row(b), 0, 0)),
                  pl.BlockSpec((1, D_MODEL), lambda b, i, off: (0, 0))],
        out_specs=pl.BlockSpec((tr, D_MODEL), lambda b, i, off: (b * nr + i, 0)),
        scratch_shapes=[pltpu.VMEM((tr, D_MODEL), F32)])
    return pl.pallas_call(
        functools.partial(_combine_win_kernel, cap=cap, w=SLOT_WIN),
        grid_spec=grid_spec,
        out_shape=jax.ShapeDtypeStruct((n_req * n, D_MODEL), F32),
        compiler_params=_cparams(("parallel", "parallel")),
        name="combine_win",
    )(offs, x1, pos_tok, ys, mods6, fg)


def _rot_half(w):
    half = QK_ROPE // 2
    return jnp.concatenate([-w[..., half:], w[..., :half]], axis=-1)


def _rope_tables(t):
    n_rows = t // GRID_W
    rows = np.repeat(np.arange(n_rows, dtype=np.float64), GRID_W)
    cols = np.tile(np.arange(GRID_W, dtype=np.float64), n_rows)
    n_freq = QK_ROPE // 4
    inv_freq = ROPE_BASE ** (-np.arange(n_freq, dtype=np.float64) / n_freq)
    ang = np.concatenate([rows[:, None] * inv_freq, cols[:, None] * inv_freq], axis=-1)
    cos = np.concatenate([np.cos(ang), np.cos(ang)], axis=-1)
    sin = np.concatenate([np.sin(ang), np.sin(ang)], axis=-1)
    return cos, sin


def _qk_tables(cos, sin):
    t = cos.shape[0]
    scale = (QK_NOPE + QK_ROPE) ** -0.5 * np.log2(np.e)
    pad = np.zeros((t, HEAD_PAD - QK_NOPE - QK_ROPE))
    cosq = np.concatenate([np.full((t, QK_NOPE), scale), cos * scale, pad], axis=1)
    sinq = np.concatenate([np.zeros((t, QK_NOPE)), sin * scale, pad], axis=1)
    tq_t = np.concatenate([cosq, sinq], axis=1).T
    tk = np.concatenate([cos, sin, np.zeros((t, LANES - 2 * QK_ROPE))], axis=1)
    return jnp.asarray(tq_t, F32), jnp.asarray(tk, F32)


def _dft_angles(rows, t):
    k = np.arange(t, dtype=np.int64)
    return ((rows[:, None] * k[None, :]) % t).astype(np.float64) * (2.0 * np.pi / t)


def _dft_tables(t):
    ang = _dft_angles(np.arange(t, dtype=np.int64), t)
    scale = (t * FNET_CH) ** -0.5
    return jnp.asarray(np.cos(ang) * scale, F32).astype(BF16), jnp.asarray(np.sin(ang) * scale, F32).astype(BF16)


def _dft_half_tables(t):
    r = np.arange(MIX_TILE, dtype=np.int64)
    ang = _dft_angles(np.concatenate([r[0::2], r[1::2]]), t)[:, :t // 2]
    scale = (t * FNET_CH) ** -0.5
    ang_off = _dft_angles(np.arange(t // MIX_TILE, dtype=np.int64) * MIX_TILE, t)[:, :t // 2]
    off = np.concatenate([np.cos(ang_off), np.sin(ang_off)], axis=1)
    return jnp.asarray(np.cos(ang) * scale, F32), jnp.asarray(np.sin(ang) * scale, F32), jnp.asarray(off, F32)


def _block_diag(w):
    g, a, b = w.shape
    eye = jnp.eye(g, dtype=w.dtype)
    return (eye[:, None, :, None] * w[:, :, None, :]).reshape(g * a, g * b)


def kernel(x_prompt, x_sample, cache_ckv, cache_kpe, c, c_ctx, w_mod, b_mod, norm1_g, w_in, q_norm_g, w_uq,
           kv_norm_g, w_ukv, w_fmix, w_out, norm2_g, w_router, w_e_gate, w_e_up, w_e_down, final_g):
    assert w_mod.shape[0] == 1, "single-layer problem"
    n_ctx, t_ctx, _ = x_prompt.shape
    n_lat, t_lat, _ = x_sample.shape
    past = cache_ckv.shape[2]
    ctx_row = n_lat

    w_in0 = w_in[0]
    kpe_cols = w_in0[:, Q_LORA + KV_LORA:Q_LORA + KV_LORA + QK_ROPE]
    win = jnp.concatenate([w_in0[:, :Q_LORA + KV_LORA + QK_ROPE], _rot_half(kpe_cols),
                           jnp.zeros((D_MODEL, 512 - Q_LORA - KV_LORA - 2 * QK_ROPE), F32),
                           w_in0[:, Q_LORA + KV_LORA + QK_ROPE:]], axis=1).astype(BF16)
    wq3 = w_uq[0].reshape(Q_LORA, N_HEADS, QK_NOPE + QK_ROPE)
    qpad = jnp.zeros((Q_LORA, N_HEADS, HEAD_PAD - QK_NOPE - QK_ROPE), F32)
    wuq_main = jnp.concatenate([wq3, qpad], axis=2).reshape(Q_LORA, QK_W)
    wuq_rot = jnp.concatenate([jnp.zeros((Q_LORA, N_HEADS, QK_NOPE), F32), _rot_half(wq3[..., QK_NOPE:]), qpad],
                              axis=2).reshape(Q_LORA, QK_W)
    wuq_lat = jnp.concatenate([wuq_main, wuq_rot], axis=1).T.astype(BF16)
    wuq_ctx = wuq_main.T.astype(BF16)
    wkv3 = w_ukv[0].reshape(KV_LORA, N_HEADS, QK_NOPE + V_HEAD)
    wk_top = jnp.concatenate([wkv3[..., :QK_NOPE], jnp.zeros((KV_LORA, N_HEADS, HEAD_PAD - QK_NOPE), F32)],
                             axis=2).reshape(KV_LORA, QK_W)
    place = jnp.concatenate([jnp.zeros((QK_ROPE, QK_NOPE), F32), jnp.eye(QK_ROPE, dtype=F32),
                             jnp.zeros((QK_ROPE, HEAD_PAD - QK_NOPE - QK_ROPE), F32)], axis=1)
    place = jnp.tile(place, (1, N_HEADS))
    wk = jnp.concatenate([wk_top, place, place, jnp.zeros((LANES - 2 * QK_ROPE, QK_W), F32)], axis=0).astype(BF16)
    wv = wkv3[..., QK_NOPE:].reshape(KV_LORA, V_W).T.astype(BF16)
    wo = w_out[0].astype(BF16)
    wr_t = w_router[0].T.astype(BF16)

    cos, sin = _rope_tables(t_lat)
    tq_lat, tk_lat = _qk_tables(cos, sin)
    assert n_ctx % CTX_REQS == 0
    tq_ctx, tk_ctx = _qk_tables(np.ones((CTX_REQS * t_ctx, QK_ROPE)), np.zeros((CTX_REQS * t_ctx, QK_ROPE)))
    ch_ang = _dft_angles(np.arange(FNET_CH, dtype=np.int64), FNET_CH)
    dft_ctx = _dft_tables(t_ctx)
    dft_lat = _dft_half_tables(t_lat)

    c8 = jnp.concatenate([c, c_ctx[None, :], jnp.zeros((8 - n_lat - 1, D_MODEL), F32)], axis=0)
    mods6 = _mods(c8, w_mod[0], b_mod[0][None, :]).reshape(8, 6, D_MODEL)
    cw, sw = _fold(jnp.asarray(np.cos(ch_ang), F32), jnp.asarray(np.sin(ch_ang), F32), w_fmix[0])
    gh = FNET_GROUPS // 2
    wcs = jnp.concatenate([jnp.concatenate([_block_diag(cw[a * gh:(a + 1) * gh]), _block_diag(sw[a * gh:(a + 1) * gh])],
                                           axis=1) for a in range(2)], axis=0).astype(BF16)

    g1 = norm1_g[0][None, :]
    qg = q_norm_g[0][None, :]
    kvg = kv_norm_g[0][None, :]
    g2 = norm2_g[0][None, :]
    fg = final_g[None, :]

    xp = x_prompt.reshape(n_ctx * t_ctx, D_MODEL)
    xs = x_sample.reshape(n_lat * t_lat, D_MODEL)
    tiles_lat = t_lat // MIX_TILE

    x1c, h2c, affc, ckv_c, kpe_c = _ctx_front(
        xp, mods6, g1, win, qg, wuq_ctx, kvg, wk, wv, wcs, tq_ctx, tk_ctx,
        *dft_ctx, wo, g2, wr_t, n_req=n_ctx, t=t_ctx, mod_row=ctx_row)
    ql, kl, vl, zcl, zsl = _premix(
        xs, mods6, g1, win, qg, wuq_lat, kvg, wk, wv, wcs, tq_lat, tk_lat,
        mod_row=lambda i: i // tiles_lat, tab_row=lambda i: i % tiles_lat, rope=True)
    xk_cache = jnp.concatenate([cache_ckv[:, 0], cache_kpe[:, 0],
                                jnp.zeros((n_lat, past, 2 * LANES - KV_LORA - QK_ROPE), F32)],
                               axis=-1).reshape(n_lat * past, 2 * LANES).astype(BF16)
    kpast, vpast = _cachekv(xk_cache, wk, wv)

    attn_l = _attention(ql, [kpast, kl], [vpast, vl], n_req=n_lat, t_q=t_lat, kv_lens=[past, t_lat], tq=TOK_TILE,
                        pairs_per_step=2)

    x1l, h2l, affl = _mixout(xs, attn_l, zcl, zsl, *dft_lat, wo, mods6, g2, wr_t,
                             n_req=n_lat, t=t_lat, mod_row=lambda b: b)

    cap_c = CAP_FACTOR * t_ctx // N_EXPERTS
    cap_l = CAP_FACTOR * t_lat // N_EXPERTS
    affc2 = affc.reshape(n_ctx * N_EXPERTS, t_ctx)
    affl2 = affl.reshape(n_lat * N_EXPERTS, t_lat)
    posc, _ = _route(affc2, cap_c)
    posl, offl = _route(affl2, cap_l)
    assert t_lat // TOK_TILE + 1 <= OFF_STRIDE and cap_l % SLOT_WIN == 0
    offl = offl[:, :OFF_STRIDE].reshape(-1)
    xsc, gc = _gather(posc, affc2, h2c, n_req=n_ctx, n=t_ctx, cap=cap_c, rps=MOE_REQS)
    xsl, gl = _gather_win(offl, posl, affl2, h2l, n_req=n_lat, n=t_lat, cap=cap_l)
    ysc, ysl = _ffn(xsc, xsl, gc, gl, w_e_gate[0], w_e_up[0], w_e_down[0])

    posc_tok = posc.reshape(n_ctx, N_EXPERTS, t_ctx).transpose(0, 2, 1)
    posl_tok = posl.reshape(n_lat, N_EXPERTS, t_lat).transpose(0, 2, 1)
    y_prompt = _combine(x1c, posc_tok, ysc, mods6, fg, n_req=n_ctx, n=t_ctx, cap=cap_c, mod_row=ctx_row,
                        rps=MOE_REQS)
    y_sample = _combine_win(offl, x1l, posl_tok, ysl, mods6, fg, n_req=n_lat, n=t_lat, cap=cap_l,
                            mod_row=lambda b: b)

    return (y_prompt.reshape(n_ctx, t_ctx, D_MODEL), y_sample.reshape(n_lat, t_lat, D_MODEL),
            ckv_c.reshape(n_ctx, 1, t_ctx, KV_LORA), kpe_c.reshape(n_ctx, 1, t_ctx, QK_ROPE))
```

```python
import functools

import jax
import jax.numpy as jnp
import numpy as np
from jax import lax
from jax.experimental import pallas as pl
from jax.experimental.pallas import tpu as pltpu

F32 = jnp.float32
BF16 = jnp.bfloat16

D_MODEL = 1024
N_HEADS = 8
QK_NOPE = 64
QK_ROPE = 32
V_HEAD = 64
Q_LORA = 256
KV_LORA = 128
FNET_GROUPS = 8
FNET_CH = 64
FNET_W = FNET_GROUPS * FNET_CH
N_EXPERTS = 16
CAP_FACTOR = 2
D_EXPERT = 512
GRID_W = 64
ROPE_BASE = 10000.0
EPS = 1e-6

LANES = 128
HEAD_PAD = LANES
QK_W = N_HEADS * HEAD_PAD
V_W = N_HEADS * V_HEAD
TOK_TILE = 256
MIX_TILE = 512
CTX_REQS = 4
MOE_REQS = 4
KEY_CHUNK = 512
SLOT_WIN = 64
OFF_STRIDE = 16
VMEM_LIMIT = 48 * 1024 * 1024

_NT = (((1,), (1,)), ((), ()))


def _cparams(sem):
    return pltpu.CompilerParams(dimension_semantics=sem, vmem_limit_bytes=VMEM_LIMIT)


def _rms(x, g):
    return x * lax.rsqrt(jnp.mean(x * x, axis=-1, keepdims=True) + EPS) * g


def _dot(a, b):
    return jnp.dot(a, b, preferred_element_type=F32)


def _mods_kernel(c_ref, w_ref, b_ref, o_ref):
    c = c_ref[...]
    s = c * jax.nn.sigmoid(c)
    o_ref[...] = _dot(s.astype(BF16), w_ref[...].astype(BF16)) + b_ref[...]


def _mods(c8, w_mod, b_mod):
    n = w_mod.shape[1]
    tn = 1536
    return pl.pallas_call(
        _mods_kernel,
        grid=(n // tn,),
        in_specs=[pl.BlockSpec((8, D_MODEL), lambda j: (0, 0)),
                  pl.BlockSpec((D_MODEL, tn), lambda j: (0, j)),
                  pl.BlockSpec((1, tn), lambda j: (0, j))],
        out_specs=pl.BlockSpec((8, tn), lambda j: (0, j)),
        out_shape=jax.ShapeDtypeStruct((8, n), F32),
        compiler_params=_cparams(("arbitrary",)),
        name="mods",
    )(c8, w_mod, b_mod)


def _fold_kernel(cc_ref, sc_ref, w_ref, cw_ref, sw_ref):
    for g in range(FNET_GROUPS):
        w = w_ref[g]
        cw_ref[g] = jnp.dot(cc_ref[...], w, preferred_element_type=F32, precision=lax.Precision.HIGHEST)
        sw_ref[g] = jnp.dot(sc_ref[...], w, preferred_element_type=F32, precision=lax.Precision.HIGHEST)


def _fold(cc, sc, w_fmix):
    shp = jax.ShapeDtypeStruct((FNET_GROUPS, FNET_CH, FNET_CH), F32)
    return pl.pallas_call(_fold_kernel, out_shape=(shp, shp), name="fold")(cc, sc, w_fmix)


def _premix_body(x, m_ref, g1_ref, win_ref, qg_ref, wuq_ref, kvg_ref, wk_ref, wv_ref, wcs_ref, tq_ref, tk_ref, rope):
    shift1 = m_ref[0, 0:1, :]
    scale1 = m_ref[0, 1:2, :]
    h = _rms(x, g1_ref[...]) * (1.0 + scale1) + shift1
    proj = _dot(h.astype(BF16), win_ref[...])
    qn = _rms(proj[:, 0:Q_LORA], qg_ref[...]).astype(BF16)
    qq = lax.dot_general(wuq_ref[...], qn, _NT, preferred_element_type=F32)
    cosq = tq_ref[0:LANES, :]
    sinq = tq_ref[LANES:2 * LANES, :]
    q_heads = []
    for hd in range(N_HEADS):
        lo = hd * HEAD_PAD
        qh = qq[lo:lo + HEAD_PAD, :] * cosq
        if rope:
            rot = qq[QK_W + hd * QK_ROPE:QK_W + (hd + 1) * QK_ROPE, :] * sinq[QK_NOPE:QK_NOPE + QK_ROPE, :]
            qh = qh + jnp.concatenate([jnp.zeros((QK_NOPE, rot.shape[1]), F32), rot,
                                       jnp.zeros((HEAD_PAD - QK_NOPE - QK_ROPE, rot.shape[1]), F32)], axis=0)
        q_heads.append(qh.astype(BF16))
    ckv = _rms(proj[:, Q_LORA:Q_LORA + KV_LORA], kvg_ref[...])
    kpe2 = proj[:, Q_LORA + KV_LORA:Q_LORA + KV_LORA + LANES] * tk_ref[...]
    xk = jnp.concatenate([ckv, kpe2], axis=1).astype(BF16)
    k = _dot(xk, wk_ref[...]).astype(BF16)
    v_t = lax.dot_general(wv_ref[...], xk[:, 0:KV_LORA], _NT, preferred_element_type=F32).astype(BF16)
    hw = FNET_W // 2
    f_in = proj[:, 512:1024].astype(BF16)
    z = [_dot(f_in[:, a * hw:(a + 1) * hw], wcs_ref[a * hw:(a + 1) * hw, :]) for a in range(2)]
    zc = jnp.concatenate([za[:, 0:hw] for za in z], axis=1).astype(BF16)
    zs = jnp.concatenate([za[:, hw:2 * hw] for za in z], axis=1).astype(BF16)
    kpe = proj[:, Q_LORA + KV_LORA:Q_LORA + KV_LORA + LANES]
    return q_heads, k, v_t, zc, zs, ckv, kpe


def _premix_kernel(x_ref, m_ref, g1_ref, win_ref, qg_ref, wuq_ref, kvg_ref, wk_ref, wv_ref, wcs_ref,
                   tq_ref, tk_ref, q_ref, k_ref, v_ref, zc_ref, zs_ref, *, rope):
    q_heads, k, v_t, zc, zs, _, _ = _premix_body(x_ref[...], m_ref, g1_ref, win_ref, qg_ref, wuq_ref, kvg_ref,
                                                 wk_ref, wv_ref, wcs_ref, tq_ref, tk_ref, rope)
    for hd, qh in enumerate(q_heads):
        q_ref[hd * HEAD_PAD:(hd + 1) * HEAD_PAD, :] = qh
    k_ref[...] = k
    v_ref[...] = v_t
    zc_ref[...] = zc
    zs_ref[...] = zs


def _premix(x, mods6, g1, win, qg, wuq, kvg, wk, wv, wcs, tq, tk, *, mod_row, tab_row, rope):
    n = x.shape[0]
    tm = MIX_TILE
    full = lambda a: pl.BlockSpec(a.shape, lambda i: (0,) * a.ndim)
    out_shape = [jax.ShapeDtypeStruct((QK_W, n), BF16), jax.ShapeDtypeStruct((n, QK_W), BF16),
                 jax.ShapeDtypeStruct((V_W, n), BF16), jax.ShapeDtypeStruct((n, FNET_W), BF16),
                 jax.ShapeDtypeStruct((n, FNET_W), BF16)]
    out_specs = [pl.BlockSpec((QK_W, tm), lambda i: (0, i)), pl.BlockSpec((tm, QK_W), lambda i: (i, 0)),
                 pl.BlockSpec((V_W, tm), lambda i: (0, i)), pl.BlockSpec((tm, FNET_W), lambda i: (i, 0)),
                 pl.BlockSpec((tm, FNET_W), lambda i: (i, 0))]
    return pl.pallas_call(
        functools.partial(_premix_kernel, rope=rope),
        grid=(n // tm,),
        in_specs=[pl.BlockSpec((tm, D_MODEL), lambda i: (i, 0)),
                  pl.BlockSpec((1, 6, D_MODEL), lambda i: (mod_row(i), 0, 0)),
                  full(g1), full(win), full(qg), full(wuq), full(kvg), full(wk), full(wv), full(wcs),
                  pl.BlockSpec((2 * LANES, tm), lambda i: (0, tab_row(i))),
                  pl.BlockSpec((tm, LANES), lambda i: (tab_row(i), 0))],
        out_specs=out_specs,
        out_shape=out_shape,
        compiler_params=_cparams(("parallel",)),
        name="premix",
    )(x, mods6, g1, win, qg, wuq, kvg, wk, wv, wcs, tq, tk)


def _cachekv_kernel(xk_ref, wk_ref, wv_ref, k_ref, v_ref):
    xk = xk_ref[...]
    k_ref[...] = _dot(xk, wk_ref[...]).astype(BF16)
    v_ref[...] = lax.dot_general(wv_ref[...], xk[:, 0:KV_LORA], _NT, preferred_element_type=F32).astype(BF16)


def _cachekv(xk, wk, wv):
    n = xk.shape[0]
    tm = 512
    full = lambda a: pl.BlockSpec(a.shape, lambda i: (0,) * a.ndim)
    return pl.pallas_call(
        _cachekv_kernel,
        grid=(n // tm,),
        in_specs=[pl.BlockSpec((tm, 2 * LANES), lambda i: (i, 0)), full(wk), full(wv)],
        out_specs=[pl.BlockSpec((tm, QK_W), lambda i: (i, 0)), pl.BlockSpec((V_W, tm), lambda i: (0, i))],
        out_shape=[jax.ShapeDtypeStruct((n, QK_W), BF16), jax.ShapeDtypeStruct((V_W, n), BF16)],
        compiler_params=_cparams(("parallel",)),
        name="cachekv",
    )(xk, wk, wv)


def _attn_body(q_heads, k_refs, v_refs, kc):
    tq = q_heads[0].shape[1]
    zero = jnp.zeros((HEAD_PAD, tq), BF16)
    n_pairs = len(q_heads) // 2
    qbd = [jnp.concatenate([jnp.concatenate([q_heads[2 * pr], zero], axis=1),
                            jnp.concatenate([zero, q_heads[2 * pr + 1]], axis=1)], axis=0) for pr in range(n_pairs)]
    chunks = [(k_ref, v_ref, c0, min(c0 + kc, k_ref.shape[0]))
              for k_ref, v_ref in zip(k_refs, v_refs) for c0 in range(0, k_ref.shape[0], kc)]
    work = [(pr, ch) for pr in range(n_pairs) for ch in chunks]

    def score(item):
        pr, (k_ref, _, c0, c1) = item
        return _dot(k_ref[c0:c1, pr * 2 * HEAD_PAD:(pr + 1) * 2 * HEAD_PAD], qbd[pr])

    m = [None] * n_pairs
    o = [None] * n_pairs
    s_next = score(work[0])
    for wi, (pr, (k_ref, v_ref, c0, c1)) in enumerate(work):
        s = s_next
        if wi + 1 < len(work):
            s_next = score(work[wi + 1])
        cm = jnp.max(s, axis=0, keepdims=True)
        vlo = pr * 2 * V_HEAD
        va = jnp.concatenate([v_ref[vlo:vlo + 2 * V_HEAD, c0:c1], jnp.ones((16, c1 - c0), BF16)], axis=0)
        if m[pr] is None:
            m[pr] = cm
            o[pr] = _dot(va, jnp.exp2((s - cm).astype(BF16)))
        else:
            m_new = jnp.maximum(m[pr], cm)
            alpha = jnp.exp2(m[pr] - m_new)
            o[pr] = alpha * o[pr] + _dot(va, jnp.exp2((s - m_new).astype(BF16)))
            m[pr] = m_new
    outs = []
    for pr in range(n_pairs):
        on = o[pr][0:2 * V_HEAD, :] * (1.0 / o[pr][2 * V_HEAD:2 * V_HEAD + 1, :])
        ot = jnp.concatenate([on[0:V_HEAD, 0:tq], on[V_HEAD:2 * V_HEAD, tq:2 * tq]], axis=0)
        outs.append(ot.T.astype(BF16))
    return outs


def _attn_kernel(q_ref, *refs, n_kv, n_pairs, kc):
    q_heads = [q_ref[hd * HEAD_PAD:(hd + 1) * HEAD_PAD, :] for hd in range(2 * n_pairs)]
    outs = _attn_body(q_heads, refs[:n_kv], refs[n_kv:2 * n_kv], kc)
    o_ref = refs[2 * n_kv]
    for pr, o in enumerate(outs):
        o_ref[:, pr * 2 * V_HEAD:(pr + 1) * 2 * V_HEAD] = o


def _attention(q_t, ks, vs_t, *, n_req, t_q, kv_lens, tq, pairs_per_step):
    n_kv = len(ks)
    nq = t_q // tq
    pp = pairs_per_step
    in_specs = [pl.BlockSpec((pp * 2 * HEAD_PAD, tq), lambda b, p, i: (p, b * nq + i))]
    in_specs += [pl.BlockSpec((kl, pp * 2 * HEAD_PAD), lambda b, p, i: (b, p)) for kl in kv_lens]
    in_specs += [pl.BlockSpec((pp * 2 * V_HEAD, kl), lambda b, p, i: (p, b)) for kl in kv_lens]
    return pl.pallas_call(
        functools.partial(_attn_kernel, n_kv=n_kv, n_pairs=pp, kc=KEY_CHUNK),
        grid=(n_req, N_HEADS // 2 // pp, nq),
        in_specs=in_specs,
        out_specs=pl.BlockSpec((tq, pp * 2 * V_HEAD), lambda b, p, i: (b * nq + i, p)),
        out_shape=jax.ShapeDtypeStruct((n_req * t_q, V_W), BF16),
        compiler_params=_cparams(("parallel", "parallel", "parallel")),
        name="attn",
    )(q_t, *ks, *vs_t)


def _mixout_body(x, attn_pairs, fm, wo_ref, m_ref, g2_ref, wr_ref):
    y = _dot(fm, wo_ref[V_W:V_W + FNET_W, :])
    col = 0
    for a in attn_pairs:
        y = y + _dot(a, wo_ref[col:col + a.shape[1], :])
        col += a.shape[1]
    gate1 = m_ref[0, 2:3, :]
    shift2 = m_ref[0, 3:4, :]
    scale2 = m_ref[0, 4:5, :]
    x1 = x + gate1 * y
    h2 = (_rms(x1, g2_ref[...]) * (1.0 + scale2) + shift2).astype(BF16)
    lg = lax.dot_general(wr_ref[...], h2, _NT, preferred_element_type=F32)
    e = jnp.exp(lg - jnp.max(lg, axis=0, keepdims=True))
    return x1, h2, e / jnp.sum(e, axis=0, keepdims=True)


def _mixout_kernel(x_ref, a_ref, zc_ref, zs_ref, cb_ref, sb_ref, off_ref, wo_ref, m_ref, g2_ref, wr_ref,
                   x1_ref, h2_ref, aff_ref, zp_ref, zm_ref, *, t):
    half = t // 2
    i = pl.program_id(1)

    @pl.when(i == 0)
    def _fold_halves():
        for src, col in ((zc_ref, 0), (zs_ref, FNET_W)):
            lo = src[0:half, :]
            hi = src[half:t, :]
            zp_ref[:, col:col + FNET_W] = lo + hi
            zm_ref[:, col:col + FNET_W] = lo - hi

    off = off_ref[pl.ds(i, 1), :]
    co = off[:, 0:half]
    so = off[:, half:t]
    cb = cb_ref[...]
    sb = sb_ref[...]
    ct = (cb * co - sb * so).astype(BF16)
    st = (sb * co + cb * so).astype(BF16)
    tr = cb.shape[0]
    h = tr // 2
    even = _dot(ct[0:h, :], zp_ref[:, 0:FNET_W]) - _dot(st[0:h, :], zp_ref[:, FNET_W:2 * FNET_W])
    odd = _dot(ct[h:, :], zm_ref[:, 0:FNET_W]) - _dot(st[h:, :], zm_ref[:, FNET_W:2 * FNET_W])
    k_i = lax.broadcasted_iota(jnp.int32, (tr, tr), 0)
    j_i = lax.broadcasted_iota(jnp.int32, (tr, tr), 1)
    perm = jnp.where(j_i == (k_i >> 1) + h * (k_i & 1), 1.0, 0.0).astype(BF16)
    fm = _dot(perm, jnp.concatenate([even, odd], axis=0).astype(BF16)).astype(BF16)
    x1, h2, aff = _mixout_body(x_ref[...], [a_ref[...]], fm, wo_ref, m_ref, g2_ref, wr_ref)
    x1_ref[...] = x1
    h2_ref[...] = h2
    aff_ref[...] = aff


def _ctx_front_kernel(x_ref, m_ref, g1_ref, win_ref, qg_ref, wuq_ref, kvg_ref, wk_ref, wv_ref, wcs_ref,
                      tq_ref, tk_ref, ct_ref, st_ref, wo_ref, g2_ref, wr_ref,
                      x1_ref, h2_ref, aff_ref, ckv_ref, kpe_ref, *, kc, t):
    x = x_ref[...]
    q_heads, k, v_t, zc, zs, ckv, kpe = _premix_body(x, m_ref, g1_ref, win_ref, qg_ref, wuq_ref, kvg_ref,
                                                     wk_ref, wv_ref, wcs_ref, tq_ref, tk_ref, False)
    ckv_ref[...] = ckv
    for r in range(x.shape[0] // t):
        kpe_ref[r] = kpe[r * t:(r + 1) * t, :].T[0:QK_ROPE, :]
    attn, fm = [], []
    for r in range(x.shape[0] // t):
        rows = slice(r * t, (r + 1) * t)
        attn.append(jnp.concatenate(_attn_body([q[:, rows] for q in q_heads], [k[rows, :]], [v_t[:, rows]], kc),
                                    axis=1))
        fm.append((_dot(ct_ref[...], zc[rows, :]) - _dot(st_ref[...], zs[rows, :])).astype(BF16))
    x1, h2, aff = _mixout_body(x, [jnp.concatenate(attn, axis=0)], jnp.concatenate(fm, axis=0),
                               wo_ref, m_ref, g2_ref, wr_ref)
    x1_ref[...] = x1
    h2_ref[...] = h2
    for r in range(x.shape[0] // t):
        aff_ref[r] = aff[:, r * t:(r + 1) * t]


def _ctx_front(x, mods6, g1, win, qg, wuq, kvg, wk, wv, wcs, tq, tk, ct, st, wo, g2, wr_t, *, n_req, t, mod_row):
    full = lambda a: pl.BlockSpec(a.shape, lambda b: (0,) * a.ndim)
    rps = CTX_REQS
    row = lambda w: pl.BlockSpec((rps * t, w), lambda b: (b, 0))
    return pl.pallas_call(
        functools.partial(_ctx_front_kernel, kc=KEY_CHUNK, t=t),
        grid=(n_req // rps,),
        in_specs=[row(D_MODEL), pl.BlockSpec((1, 6, D_MODEL), lambda b: (mod_row, 0, 0)),
                  full(g1), full(win), full(qg), full(wuq), full(kvg), full(wk), full(wv), full(wcs),
                  full(tq), full(tk), full(ct), full(st), full(wo), full(g2), full(wr_t)],
        out_specs=[row(D_MODEL), row(D_MODEL), pl.BlockSpec((rps, N_EXPERTS, t), lambda b: (b, 0, 0)),
                   row(KV_LORA), pl.BlockSpec((rps, QK_ROPE, t), lambda b: (b, 0, 0))],
        out_shape=[jax.ShapeDtypeStruct((n_req * t, D_MODEL), F32),
                   jax.ShapeDtypeStruct((n_req * t, D_MODEL), BF16),
                   jax.ShapeDtypeStruct((n_req, N_EXPERTS, t), F32),
                   jax.ShapeDtypeStruct((n_req * t, KV_LORA), F32),
                   jax.ShapeDtypeStruct((n_req, QK_ROPE, t), F32)],
        compiler_params=_cparams(("parallel",)),
        name="ctx_front",
    )(x, mods6, g1, win, qg, wuq, kvg, wk, wv, wcs, tq, tk, ct, st, wo, g2, wr_t)


def _mixout(x, attn, zc, zs, cb, sb, off, wo, mods6, g2, wr_t, *, n_req, t, mod_row):
    tr = MIX_TILE
    nr = t // tr
    full = lambda a: pl.BlockSpec(a.shape, lambda b, i: (0,) * a.ndim)
    return pl.pallas_call(
        functools.partial(_mixout_kernel, t=t),
        grid=(n_req, nr),
        in_specs=[pl.BlockSpec((tr, D_MODEL), lambda b, i: (b * nr + i, 0)),
                  pl.BlockSpec((tr, V_W), lambda b, i: (b * nr + i, 0)),
                  pl.BlockSpec((t, FNET_W), lambda b, i: (b, 0)),
                  pl.BlockSpec((t, FNET_W), lambda b, i: (b, 0)),
                  full(cb), full(sb), full(off),
                  full(wo),
                  pl.BlockSpec((1, 6, D_MODEL), lambda b, i: (mod_row(b), 0, 0)),
                  full(g2), full(wr_t)],
        out_specs=[pl.BlockSpec((tr, D_MODEL), lambda b, i: (b * nr + i, 0)),
                   pl.BlockSpec((tr, D_MODEL), lambda b, i: (b * nr + i, 0)),
                   pl.BlockSpec((None, N_EXPERTS, tr), lambda b, i: (b, 0, i))],
        out_shape=[jax.ShapeDtypeStruct((n_req * t, D_MODEL), F32),
                   jax.ShapeDtypeStruct((n_req * t, D_MODEL), BF16),
                   jax.ShapeDtypeStruct((n_req, N_EXPERTS, t), F32)],
        scratch_shapes=[pltpu.VMEM((t // 2, 2 * FNET_W), BF16), pltpu.VMEM((t // 2, 2 * FNET_W), BF16)],
        compiler_params=_cparams(("parallel", "arbitrary")),
        name="mixout",
    )(x, attn, zc, zs, cb, sb, off, wo, mods6, g2, wr_t)


def _prefix_count(flags, tri):
    n = flags.shape[1]
    carry = None
    outs = []
    ends = []
    for j in range(n // TOK_TILE):
        c = _dot(flags[:, j * TOK_TILE:(j + 1) * TOK_TILE].astype(BF16), tri)
        if carry is not None:
            c = c + carry
        outs.append(c)
        carry = c[:, TOK_TILE - 1:TOK_TILE]
        ends.append(carry)
    return (outs[0] if len(outs) == 1 else jnp.concatenate(outs, axis=1)), ends


def _route_kernel(aff_ref, pos_ref, off_ref, *, cap):
    a = aff_ref[...]
    rows = a.shape[0]
    capf = jnp.float32(cap)
    thr = jnp.zeros((rows, 1), jnp.int32)
    for bit in range(30, -1, -1):
        cand = thr | jnp.int32(1 << bit)
        cand_f = lax.bitcast_convert_type(cand, F32)
        cnt = jnp.sum(jnp.where(a >= cand_f, 1.0, 0.0), axis=1, keepdims=True)
        thr = jnp.where(cnt >= capf, cand, thr)
    thr_f = lax.bitcast_convert_type(thr, F32)
    above_f = lax.bitcast_convert_type(thr + 1, F32)
    gt = jnp.where(a >= above_f, 1.0, 0.0)
    tie = jnp.where(a >= thr_f, 1.0, 0.0) - gt
    need = capf - jnp.sum(gt, axis=1, keepdims=True)
    r_i = lax.broadcasted_iota(jnp.int32, (TOK_TILE, TOK_TILE), 0)
    c_i = lax.broadcasted_iota(jnp.int32, (TOK_TILE, TOK_TILE), 1)
    tri = jnp.where(r_i <= c_i, 1.0, 0.0).astype(BF16)
    tie_before = _prefix_count(tie, tri)[0] - tie
    sel = gt + tie * jnp.where(tie_before < need, 1.0, 0.0)
    count, ends = _prefix_count(sel, tri)
    pos_ref[...] = jnp.where(sel > 0.5, count - 1.0, -1.0)
    lane = lax.broadcasted_iota(jnp.int32, (rows, LANES), 1)
    offs = jnp.zeros((rows, LANES), F32)
    for j, end in enumerate(ends):
        offs = offs + jnp.where(lane == j + 1, end, 0.0)
    off_ref[...] = offs.astype(jnp.int32)


def _route(aff_t, cap):
    return pl.pallas_call(
        functools.partial(_route_kernel, cap=cap),
        out_shape=[jax.ShapeDtypeStruct(aff_t.shape, F32),
                   jax.ShapeDtypeStruct((aff_t.shape[0], LANES), jnp.int32)],
        compiler_params=pltpu.CompilerParams(vmem_limit_bytes=VMEM_LIMIT),
        name="route",
    )(aff_t)


def _gather_kernel(pos_ref, aff_ref, h_ref, xs_ref, g_ref, *, cap):
    rps, ne, n = pos_ref.shape
    slot = lax.broadcasted_iota(jnp.int32, (ne, cap, n), 1).astype(F32)
    for r in range(rps):
        pos = pos_ref[r]
        aff = aff_ref[r]
        hit = pos[:, None, :] == slot
        onehot = jnp.where(hit, 1.0, 0.0).reshape(ne * cap, n).astype(BF16)
        xs = _dot(onehot, h_ref[r * n:(r + 1) * n, :])
        xs_ref[:, r * cap:(r + 1) * cap, :] = xs.astype(BF16).reshape(ne, cap, D_MODEL)
        g_ref[:, r * cap:(r + 1) * cap, :] = jnp.sum(jnp.where(hit, aff[:, None, :], 0.0), axis=2, keepdims=True)


def _gather(pos_t, aff_t, h2, *, n_req, n, cap, rps):
    pos3 = pos_t.reshape(n_req, N_EXPERTS, n)
    aff3 = aff_t.reshape(n_req, N_EXPERTS, n)
    return pl.pallas_call(
        functools.partial(_gather_kernel, cap=cap),
        grid=(n_req // rps,),
        in_specs=[pl.BlockSpec((rps, N_EXPERTS, n), lambda b: (b, 0, 0)),
                  pl.BlockSpec((rps, N_EXPERTS, n), lambda b: (b, 0, 0)),
                  pl.BlockSpec((rps * n, D_MODEL), lambda b: (b, 0))],
        out_specs=[pl.BlockSpec((N_EXPERTS, rps * cap, D_MODEL), lambda b: (0, b, 0)),
                   pl.BlockSpec((N_EXPERTS, rps * cap, 1), lambda b: (0, b, 0))],
        out_shape=[jax.ShapeDtypeStruct((N_EXPERTS, n_req * cap, D_MODEL), BF16),
                   jax.ShapeDtypeStruct((N_EXPERTS, n_req * cap, 1), F32)],
        compiler_params=_cparams(("parallel",)),
        name="gather",
    )(pos3, aff3, h2)


def _window_plan(off_ref, b, j, cap, w):
    w0 = []
    need = jnp.int32(0)
    for e in range(N_EXPERTS):
        idx = (b * N_EXPERTS + e) * OFF_STRIDE + j
        base = (off_ref[idx] >> 4) << 4
        w0.append(base)
        need = jnp.maximum(need, off_ref[idx + 1] - base)
    return w0, (need + (w - 1)) >> (w.bit_length() - 1)


def _window(w0_e, p, cap, w):
    low = w0_e + p * w
    start = pl.multiple_of(jnp.minimum(low, cap - w), 16)
    return low, start


def _gather_win_kernel(off_ref, pos_ref, aff_ref, h_ref, xs_ref, g_ref, *, cap, w):
    b = pl.program_id(0)
    j = pl.program_id(1)

    @pl.when(j == 0)
    def _init():
        xs_ref[...] = jnp.zeros_like(xs_ref)
        g_ref[...] = jnp.zeros_like(g_ref)

    pos = pos_ref[...]
    aff = aff_ref[...]
    h = h_ref[...]
    w0, n_pass = _window_plan(off_ref, b, j, cap, w)
    r = lax.broadcasted_iota(jnp.int32, (w, 1), 0).astype(F32)

    def one_pass(p, carry):
        starts, hots, gates = [], [], []
        for e in range(N_EXPERTS):
            low, start = _window(w0[e], p, cap, w)
            starts.append(start)
            mine = jnp.where(r >= (low - start).astype(F32), 1.0, 0.0)
            hot = jnp.where(pos[e:e + 1, :] - start.astype(F32) == r, mine, 0.0)
            hots.append(hot)
            gates.append(jnp.sum(hot * aff[e:e + 1, :], axis=1, keepdims=True))
        rows = _dot(jnp.concatenate(hots, axis=0).astype(BF16), h).astype(BF16)
        for e in range(N_EXPERTS):
            win = pl.ds(starts[e], w)
            xs_ref[e, win, :] = xs_ref[e, win, :] + rows[e * w:(e + 1) * w, :]
            g_ref[e, win, :] = g_ref[e, win, :] + gates[e]
        return carry

    one_pass(0, 0)
    lax.fori_loop(1, n_pass, one_pass, 0)


def _gather_win(offs, pos_t, aff_t, h2, *, n_req, n, cap):
    nb = n // TOK_TILE
    grid_spec = pltpu.PrefetchScalarGridSpec(
        num_scalar_prefetch=1,
        grid=(n_req, nb),
        in_specs=[pl.BlockSpec((N_EXPERTS, TOK_TILE), lambda b, j, off: (b, j)),
                  pl.BlockSpec((N_EXPERTS, TOK_TILE), lambda b, j, off: (b, j)),
                  pl.BlockSpec((TOK_TILE, D_MODEL), lambda b, j, off: (b * nb + j, 0))],
        out_specs=[pl.BlockSpec((N_EXPERTS, cap, D_MODEL), lambda b, j, off: (0, b, 0)),
                   pl.BlockSpec((N_EXPERTS, cap, 1), lambda b, j, off: (0, b, 0))])
    return pl.pallas_call(
        functools.partial(_gather_win_kernel, cap=cap, w=SLOT_WIN),
        grid_spec=grid_spec,
        out_shape=[jax.ShapeDtypeStruct((N_EXPERTS, n_req * cap, D_MODEL), BF16),
                   jax.ShapeDtypeStruct((N_EXPERTS, n_req * cap, 1), F32)],
        compiler_params=_cparams(("parallel", "arbitrary")),
        name="gather_win",
    )(offs, pos_t, aff_t, h2)


def _ffn_kernel(xc_ref, xl_ref, gc_ref, gl_ref, wg_ref, wu_ref, wd_ref, yc_ref, yl_ref, wgb, wub, wdb):
    wgb[...] = wg_ref[0].astype(BF16)
    wub[...] = wu_ref[0].astype(BF16)
    wdb[...] = wd_ref[0].astype(BF16)
    for x_ref, g_ref, y_ref in ((xc_ref, gc_ref, yc_ref), (xl_ref, gl_ref, yl_ref)):
        for j in range(x_ref.shape[1] // TOK_TILE):
            rows = slice(j * TOK_TILE, (j + 1) * TOK_TILE)
            x = x_ref[0, rows, :]
            gate = _dot(x, wgb[...])
            up = _dot(x, wub[...])
            hid = (gate * jax.nn.sigmoid(gate) * up).astype(BF16)
            ys = _dot(hid, wdb[...]) * g_ref[0, rows, :]
            y_ref[0, rows, :] = ys.astype(BF16)


def _ffn(xc, xl, gc, gl, wg, wu, wd):
    m = xc.shape[1]
    xspec = pl.BlockSpec((1, m, D_MODEL), lambda e: (e, 0, 0))
    gspec = pl.BlockSpec((1, m, 1), lambda e: (e, 0, 0))
    shp = jax.ShapeDtypeStruct((N_EXPERTS, m, D_MODEL), BF16)
    return pl.pallas_call(
        _ffn_kernel,
        grid=(N_EXPERTS,),
        in_specs=[xspec, xspec, gspec, gspec,
                  pl.BlockSpec((1, D_MODEL, D_EXPERT), lambda e: (e, 0, 0)),
                  pl.BlockSpec((1, D_MODEL, D_EXPERT), lambda e: (e, 0, 0)),
                  pl.BlockSpec((1, D_EXPERT, D_MODEL), lambda e: (e, 0, 0))],
        out_specs=[xspec, xspec],
        out_shape=[shp, shp],
        scratch_shapes=[pltpu.VMEM((D_MODEL, D_EXPERT), BF16), pltpu.VMEM((D_MODEL, D_EXPERT), BF16),
                        pltpu.VMEM((D_EXPERT, D_MODEL), BF16)],
        compiler_params=_cparams(("arbitrary",)),
        name="ffn",
    )(xc, xl, gc, gl, wg, wu, wd)


def _combine_kernel(x1_ref, pos_ref, ys_ref, m_ref, fg_ref, o_ref, *, cap):
    rps, n, _ = pos_ref.shape
    w = N_EXPERTS * cap
    e_i = lax.broadcasted_iota(jnp.int32, (N_EXPERTS, w), 0)
    j_i = lax.broadcasted_iota(jnp.int32, (N_EXPERTS, w), 1)
    spread = jnp.where((j_i >> (cap.bit_length() - 1)) == e_i, 1.0, 0.0).astype(BF16)
    lane_slot = (lax.broadcasted_iota(jnp.int32, (1, w), 1) & (cap - 1)).astype(F32)
    gate2 = m_ref[0, 5:6, :]
    for r in range(rps):
        pos = pos_ref[r].astype(BF16)
        onehot = jnp.where(_dot(pos, spread) == lane_slot, 1.0, 0.0).astype(BF16)
        acc = _dot(onehot, ys_ref[:, r * cap:(r + 1) * cap, :].reshape(w, D_MODEL))
        rows = slice(r * n, (r + 1) * n)
        o_ref[rows, :] = _rms(x1_ref[rows, :] + gate2 * acc, fg_ref[...])


def _combine(x1, pos_tok, ys, mods6, fg, *, n_req, n, cap, mod_row, rps):
    return pl.pallas_call(
        functools.partial(_combine_kernel, cap=cap),
        grid=(n_req // rps,),
        in_specs=[pl.BlockSpec((rps * n, D_MODEL), lambda b: (b, 0)),
                  pl.BlockSpec((rps, n, N_EXPERTS), lambda b: (b, 0, 0)),
                  pl.BlockSpec((N_EXPERTS, rps * cap, D_MODEL), lambda b: (0, b, 0)),
                  pl.BlockSpec((1, 6, D_MODEL), lambda b: (mod_row, 0, 0)),
                  pl.BlockSpec((1, D_MODEL), lambda b: (0, 0))],
        out_specs=pl.BlockSpec((rps * n, D_MODEL), lambda b: (b, 0)),
        out_shape=jax.ShapeDtypeStruct((n_req * n, D_MODEL), F32),
        compiler_params=_cparams(("parallel",)),
        name="combine",
    )(x1, pos_tok, ys, mods6, fg)


def _combine_win_kernel(off_ref, x1_ref, pos_ref, ys_ref, m_ref, fg_ref, o_ref, acc_ref, *, cap, w):
    b = pl.program_id(0)
    j = pl.program_id(1)
    pos = pos_ref[...]
    w0, n_pass = _window_plan(off_ref, b, j, cap, w)
    width = N_EXPERTS * w
    e_i = lax.broadcasted_iota(jnp.int32, (N_EXPERTS, width), 0)
    j_i = lax.broadcasted_iota(jnp.int32, (N_EXPERTS, width), 1)
    spread = jnp.where((j_i >> (w.bit_length() - 1)) == e_i, 1.0, 0.0).astype(BF16)
    lane_slot = (lax.broadcasted_iota(jnp.int32, (1, width), 1) & (w - 1)).astype(F32)
    lane_e = lax.broadcasted_iota(jnp.int32, (1, N_EXPERTS), 1)

    def window_sum(p):
        start_row = jnp.zeros((1, N_EXPERTS), F32)
        first_row = jnp.zeros((1, N_EXPERTS), F32)
        wins = []
        for e in range(N_EXPERTS):
            low, start = _window(w0[e], p, cap, w)
            start_row = jnp.where(lane_e == e, start.astype(F32), start_row)
            first_row = jnp.where(lane_e == e, (low - start).astype(F32), first_row)
            wins.append(ys_ref[e, pl.ds(start, w), :])
        rel = pos - start_row
        rel = jnp.where(rel >= first_row, rel, -1.0).astype(BF16)
        onehot = jnp.where(_dot(rel, spread) == lane_slot, 1.0, 0.0).astype(BF16)
        return _dot(onehot, jnp.concatenate(wins, axis=0))

    def finish(acc):
        o_ref[...] = _rms(x1_ref[...] + m_ref[0, 5:6, :] * acc, fg_ref[...])

    acc0 = window_sum(0)
    acc_ref[...] = acc0
    finish(acc0)

    @pl.when(n_pass > 1)
    def _more_passes():
        def one_pass(p, carry):
            acc_ref[...] += window_sum(p)
            return carry

        lax.fori_loop(1, n_pass, one_pass, 0)
        finish(acc_ref[...])


def _combine_win(offs, x1, pos_tok, ys, mods6, fg, *, n_req, n, cap, mod_row):
    tr = TOK_TILE
    nr = n // tr
    grid_spec = pltpu.PrefetchScalarGridSpec(
        num_scalar_prefetch=1,
        grid=(n_req, nr),
        in_specs=[pl.BlockSpec((tr, D_MODEL), lambda b, i, off: (b * nr + i, 0)),
                  pl.BlockSpec((None, tr, N_EXPERTS), lambda b, i, off: (b, i, 0)),
                  pl.BlockSpec((N_EXPERTS, cap, D_MODEL), lambda b, i, off: (0, b, 0)),
                  pl.BlockSpec((1, 6, D_MODEL), lambda b, i, off: (mod_row(b), 0, 0)),
                  pl.BlockSpec((1, D_MODEL), lambda b, i, off: (0, 0))],
        out_specs=pl.BlockSpec((tr, D_MODEL), lambda b, i, off: (b * nr + i, 0)),
        scratch_shapes=[pltpu.VMEM((tr, D_MODEL), F32)])
    return pl.pallas_call(
        functools.partial(_combine_win_kernel, cap=cap, w=SLOT_WIN),
        grid_spec=grid_spec,
        out_shape=jax.ShapeDtypeStruct((n_req * n, D_MODEL), F32),
        compiler_params=_cparams(("parallel", "parallel")),
        name="combine_win",
    )(offs, x1, pos_tok, ys, mods6, fg)


def _rot_half(w):
    half = QK_ROPE // 2
    return jnp.concatenate([-w[..., half:], w[..., :half]], axis=-1)


def _rope_tables(t):
    n_rows = t // GRID_W
    rows = np.repeat(np.arange(n_rows, dtype=np.float64), GRID_W)
    cols = np.tile(np.arange(GRID_W, dtype=np.float64), n_rows)
    n_freq = QK_ROPE // 4
    inv_freq = ROPE_BASE ** (-np.arange(n_freq, dtype=np.float64) / n_freq)
    ang = np.concatenate([rows[:, None] * inv_freq, cols[:, None] * inv_freq], axis=-1)
    cos = np.concatenate([np.cos(ang), np.cos(ang)], axis=-1)
    sin = np.concatenate([np.sin(ang), np.sin(ang)], axis=-1)
    return cos, sin


def _qk_tables(cos, sin):
    t = cos.shape[0]
    scale = (QK_NOPE + QK_ROPE) ** -0.5 * np.log2(np.e)
    pad = np.zeros((t, HEAD_PAD - QK_NOPE - QK_ROPE))
    cosq = np.concatenate([np.full((t, QK_NOPE), scale), cos * scale, pad], axis=1)
    sinq = np.concatenate([np.zeros((t, QK_NOPE)), sin * scale, pad], axis=1)
    tq_t = np.concatenate([cosq, sinq], axis=1).T
    tk = np.concatenate([cos, sin, np.zeros((t, LANES - 2 * QK_ROPE))], axis=1)
    return jnp.asarray(tq_t, F32), jnp.asarray(tk, F32)


def _dft_angles(rows, t):
    k = np.arange(t, dtype=np.int64)
    return ((rows[:, None] * k[None, :]) % t).astype(np.float64) * (2.0 * np.pi / t)


def _dft_tables(t):
    ang = _dft_angles(np.arange(t, dtype=np.int64), t)
    scale = (t * FNET_CH) ** -0.5
    return jnp.asarray(np.cos(ang) * scale, F32).astype(BF16), jnp.asarray(np.sin(ang) * scale, F32).astype(BF16)


def _dft_half_tables(t):
    r = np.arange(MIX_TILE, dtype=np.int64)
    ang = _dft_angles(np.concatenate([r[0::2], r[1::2]]), t)[:, :t // 2]
    scale = (t * FNET_CH) ** -0.5
    ang_off = _dft_angles(np.arange(t // MIX_TILE, dtype=np.int64) * MIX_TILE, t)[:, :t // 2]
    off = np.concatenate([np.cos(ang_off), np.sin(ang_off)], axis=1)
    return jnp.asarray(np.cos(ang) * scale, F32), jnp.asarray(np.sin(ang) * scale, F32), jnp.asarray(off, F32)


def _block_diag(w):
    g, a, b = w.shape
    eye = jnp.eye(g, dtype=w.dtype)
    return (eye[:, None, :, None] * w[:, :, None, :]).reshape(g * a, g * b)


def kernel(x_prompt, x_sample, cache_ckv, cache_kpe, c, c_ctx, w_mod, b_mod, norm1_g, w_in, q_norm_g, w_uq,
           kv_norm_g, w_ukv, w_fmix, w_out, norm2_g, w_router, w_e_gate, w_e_up, w_e_down, final_g):
    assert w_mod.shape[0] == 1, "single-layer problem"
    n_ctx, t_ctx, _ = x_prompt.shape
    n_lat, t_lat, _ = x_sample.shape
    past = cache_ckv.shape[2]
    ctx_row = n_lat

    w_in0 = w_in[0]
    kpe_cols = w_in0[:, Q_LORA + KV_LORA:Q_LORA + KV_LORA + QK_ROPE]
    win = jnp.concatenate([w_in0[:, :Q_LORA + KV_LORA + QK_ROPE], _rot_half(kpe_cols),
                           jnp.zeros((D_MODEL, 512 - Q_LORA - KV_LORA - 2 * QK_ROPE), F32),
                           w_in0[:, Q_LORA + KV_LORA + QK_ROPE:]], axis=1).astype(BF16)
    wq3 = w_uq[0].reshape(Q_LORA, N_HEADS, QK_NOPE + QK_ROPE)
    qpad = jnp.zeros((Q_LORA, N_HEADS, HEAD_PAD - QK_NOPE - QK_ROPE), F32)
    wuq_main = jnp.concatenate([wq3, qpad], axis=2).reshape(Q_LORA, QK_W)
    wuq_rot = _rot_half(wq3[..., QK_NOPE:]).reshape(Q_LORA, N_HEADS * QK_ROPE)
    wuq_lat = jnp.concatenate([wuq_main, wuq_rot], axis=1).T.astype(BF16)
    wuq_ctx = wuq_main.T.astype(BF16)
    wkv3 = w_ukv[0].reshape(KV_LORA, N_HEADS, QK_NOPE + V_HEAD)
    wk_top = jnp.concatenate([wkv3[..., :QK_NOPE], jnp.zeros((KV_LORA, N_HEADS, HEAD_PAD - QK_NOPE), F32)],
                             axis=2).reshape(KV_LORA, QK_W)
    place = jnp.concatenate([jnp.zeros((QK_ROPE, QK_NOPE), F32), jnp.eye(QK_ROPE, dtype=F32),
                             jnp.zeros((QK_ROPE, HEAD_PAD - QK_NOPE - QK_ROPE), F32)], axis=1)
    place = jnp.tile(place, (1, N_HEADS))
    wk = jnp.concatenate([wk_top, place, place, jnp.zeros((LANES - 2 * QK_ROPE, QK_W), F32)], axis=0).astype(BF16)
    wv = wkv3[..., QK_NOPE:].reshape(KV_LORA, V_W).T.astype(BF16)
    wo = w_out[0].astype(BF16)
    wr_t = w_router[0].T.astype(BF16)

    cos, sin = _rope_tables(t_lat)
    tq_lat, tk_lat = _qk_tables(cos, sin)
    assert n_ctx % CTX_REQS == 0
    tq_ctx, tk_ctx = _qk_tables(np.ones((CTX_REQS * t_ctx, QK_ROPE)), np.zeros((CTX_REQS * t_ctx, QK_ROPE)))
    ch_ang = _dft_angles(np.arange(FNET_CH, dtype=np.int64), FNET_CH)
    dft_ctx = _dft_tables(t_ctx)
    dft_lat = _dft_half_tables(t_lat)

    c8 = jnp.concatenate([c, c_ctx[None, :], jnp.zeros((8 - n_lat - 1, D_MODEL), F32)], axis=0)
    mods6 = _mods(c8, w_mod[0], b_mod[0][None, :]).reshape(8, 6, D_MODEL)
    cw, sw = _fold(jnp.asarray(np.cos(ch_ang), F32), jnp.asarray(np.sin(ch_ang), F32), w_fmix[0])
    gh = FNET_GROUPS // 2
    wcs = jnp.concatenate([jnp.concatenate([_block_diag(cw[a * gh:(a + 1) * gh]), _block_diag(sw[a * gh:(a + 1) * gh])],
                                           axis=1) for a in range(2)], axis=0).astype(BF16)

    g1 = norm1_g[0][None, :]
    qg = q_norm_g[0][None, :]
    kvg = kv_norm_g[0][None, :]
    g2 = norm2_g[0][None, :]
    fg = final_g[None, :]

    xp = x_prompt.reshape(n_ctx * t_ctx, D_MODEL)
    xs = x_sample.reshape(n_lat * t_lat, D_MODEL)
    tiles_lat = t_lat // MIX_TILE

    x1c, h2c, affc, ckv_c, kpe_c = _ctx_front(
        xp, mods6, g1, win, qg, wuq_ctx, kvg, wk, wv, wcs, tq_ctx, tk_ctx,
        *dft_ctx, wo, g2, wr_t, n_req=n_ctx, t=t_ctx, mod_row=ctx_row)
    ql, kl, vl, zcl, zsl = _premix(
        xs, mods6, g1, win, qg, wuq_lat, kvg, wk, wv, wcs, tq_lat, tk_lat,
        mod_row=lambda i: i // tiles_lat, tab_row=lambda i: i % tiles_lat, rope=True)
    xk_cache = jnp.concatenate([cache_ckv[:, 0], cache_kpe[:, 0],
                                jnp.zeros((n_lat, past, 2 * LANES - KV_LORA - QK_ROPE), F32)],
                               axis=-1).reshape(n_lat * past, 2 * LANES).astype(BF16)
    kpast, vpast = _cachekv(xk_cache, wk, wv)

    attn_l = _attention(ql, [kpast, kl], [vpast, vl], n_req=n_lat, t_q=t_lat, kv_lens=[past, t_lat], tq=TOK_TILE,
                        pairs_per_step=2)

    x1l, h2l, affl = _mixout(xs, attn_l, zcl, zsl, *dft_lat, wo, mods6, g2, wr_t,
                             n_req=n_lat, t=t_lat, mod_row=lambda b: b)

    cap_c = CAP_FACTOR * t_ctx // N_EXPERTS
    cap_l = CAP_FACTOR * t_lat // N_EXPERTS
    affc2 = affc.reshape(n_ctx * N_EXPERTS, t_ctx)
    affl2 = affl.reshape(n_lat * N_EXPERTS, t_lat)
    posc, _ = _route(affc2, cap_c)
    posl, offl = _route(affl2, cap_l)
    assert t_lat // TOK_TILE + 1 <= OFF_STRIDE and cap_l % SLOT_WIN == 0
    offl = offl[:, :OFF_STRIDE].reshape(-1)
    xsc, gc = _gather(posc, affc2, h2c, n_req=n_ctx, n=t_ctx, cap=cap_c, rps=MOE_REQS)
    xsl, gl = _gather_win(offl, posl, affl2, h2l, n_req=n_lat, n=t_lat, cap=cap_l)
    ysc, ysl = _ffn(xsc, xsl, gc, gl, w_e_gate[0], w_e_up[0], w_e_down[0])

    posc_tok = posc.reshape(n_ctx, N_EXPERTS, t_ctx).transpose(0, 2, 1)
    posl_tok = posl.reshape(n_lat, N_EXPERTS, t_lat).transpose(0, 2, 1)
    y_prompt = _combine(x1c, posc_tok, ysc, mods6, fg, n_req=n_ctx, n=t_ctx, cap=cap_c, mod_row=ctx_row,
                        rps=MOE_REQS)
    y_sample = _combine_win(offl, x1l, posl_tok, ysl, mods6, fg, n_req=n_lat, n=t_lat, cap=cap_l,
                            mod_row=lambda b: b)

    return (y_prompt.reshape(n_ctx, t_ctx, D_MODEL), y_sample.reshape(n_lat, t_lat, D_MODEL),
            ckv_c.reshape(n_ctx, 1, t_ctx, KV_LORA), kpe_c.transpose(0, 2, 1).reshape(n_ctx, 1, t_ctx, QK_ROPE))
```

```python
import functools

import jax
import jax.numpy as jnp
import numpy as np
from jax import lax
from jax.experimental import pallas as pl
from jax.experimental.pallas import tpu as pltpu

F32 = jnp.float32
BF16 = jnp.bfloat16

D_MODEL = 1024
N_HEADS = 8
QK_NOPE = 64
QK_ROPE = 32
V_HEAD = 64
Q_LORA = 256
KV_LORA = 128
FNET_GROUPS = 8
FNET_CH = 64
FNET_W = FNET_GROUPS * FNET_CH
N_EXPERTS = 16
CAP_FACTOR = 2
D_EXPERT = 512
GRID_W = 64
ROPE_BASE = 10000.0
EPS = 1e-6

LANES = 128
HEAD_PAD = LANES
QK_W = N_HEADS * HEAD_PAD
V_W = N_HEADS * V_HEAD
TOK_TILE = 256
MIX_TILE = 512
CTX_REQS = 4
MOE_REQS = 4
KEY_CHUNK = 512
SLOT_WIN = 64
WIN_BLOCKS = 2
OFF_STRIDE = 16
VMEM_LIMIT = 48 * 1024 * 1024

_NT = (((1,), (1,)), ((), ()))


def _cparams(sem):
    return pltpu.CompilerParams(dimension_semantics=sem, vmem_limit_bytes=VMEM_LIMIT)


def _rms(x, g):
    return x * lax.rsqrt(jnp.mean(x * x, axis=-1, keepdims=True) + EPS) * g


def _dot(a, b):
    return jnp.dot(a, b, preferred_element_type=F32)


def _mods_kernel(c_ref, w_ref, b_ref, o_ref):
    c = c_ref[...]
    s = c * jax.nn.sigmoid(c)
    o_ref[...] = _dot(s.astype(BF16), w_ref[...].astype(BF16)) + b_ref[...]


def _mods(c8, w_mod, b_mod):
    n = w_mod.shape[1]
    tn = 1536
    return pl.pallas_call(
        _mods_kernel,
        grid=(n // tn,),
        in_specs=[pl.BlockSpec((8, D_MODEL), lambda j: (0, 0)),
                  pl.BlockSpec((D_MODEL, tn), lambda j: (0, j)),
                  pl.BlockSpec((1, tn), lambda j: (0, j))],
        out_specs=pl.BlockSpec((8, tn), lambda j: (0, j)),
        out_shape=jax.ShapeDtypeStruct((8, n), F32),
        compiler_params=_cparams(("arbitrary",)),
        name="mods",
    )(c8, w_mod, b_mod)


def _fold_kernel(cc_ref, sc_ref, w_ref, cw_ref, sw_ref):
    for g in range(FNET_GROUPS):
        w = w_ref[g]
        cw_ref[g] = jnp.dot(cc_ref[...], w, preferred_element_type=F32, precision=lax.Precision.HIGHEST)
        sw_ref[g] = jnp.dot(sc_ref[...], w, preferred_element_type=F32, precision=lax.Precision.HIGHEST)


def _fold(cc, sc, w_fmix):
    shp = jax.ShapeDtypeStruct((FNET_GROUPS, FNET_CH, FNET_CH), F32)
    return pl.pallas_call(_fold_kernel, out_shape=(shp, shp), name="fold")(cc, sc, w_fmix)


def _premix_body(x, m_ref, g1_ref, win_ref, qg_ref, wuq_ref, kvg_ref, wk_ref, wv_ref, wcs_ref, tq_ref, tk_ref, rope):
    shift1 = m_ref[0, 0:1, :]
    scale1 = m_ref[0, 1:2, :]
    h = _rms(x, g1_ref[...]) * (1.0 + scale1) + shift1
    proj = _dot(h.astype(BF16), win_ref[...])
    qn = _rms(proj[:, 0:Q_LORA], qg_ref[...]).astype(BF16)
    qq = lax.dot_general(wuq_ref[...], qn, _NT, preferred_element_type=F32)
    cosq = tq_ref[0:LANES, :]
    sinq = tq_ref[LANES:2 * LANES, :]
    q_heads = []
    for hd in range(N_HEADS):
        lo = hd * HEAD_PAD
        qh = qq[lo:lo + HEAD_PAD, :] * cosq
        if rope:
            rot = qq[QK_W + hd * QK_ROPE:QK_W + (hd + 1) * QK_ROPE, :] * sinq[QK_NOPE:QK_NOPE + QK_ROPE, :]
            qh = qh + jnp.concatenate([jnp.zeros((QK_NOPE, rot.shape[1]), F32), rot,
                                       jnp.zeros((HEAD_PAD - QK_NOPE - QK_ROPE, rot.shape[1]), F32)], axis=0)
        q_heads.append(qh.astype(BF16))
    ckv = _rms(proj[:, Q_LORA:Q_LORA + KV_LORA], kvg_ref[...])
    kpe2 = proj[:, Q_LORA + KV_LORA:Q_LORA + KV_LORA + LANES] * tk_ref[...]
    xk = jnp.concatenate([ckv, kpe2], axis=1).astype(BF16)
    k = _dot(xk, wk_ref[...]).astype(BF16)
    v_t = lax.dot_general(wv_ref[...], xk[:, 0:KV_LORA], _NT, preferred_element_type=F32).astype(BF16)
    hw = FNET_W // 2
    f_in = proj[:, 512:1024].astype(BF16)
    z = [_dot(f_in[:, a * hw:(a + 1) * hw], wcs_ref[a * hw:(a + 1) * hw, :]) for a in range(2)]
    zc = jnp.concatenate([za[:, 0:hw] for za in z], axis=1).astype(BF16)
    zs = jnp.concatenate([za[:, hw:2 * hw] for za in z], axis=1).astype(BF16)
    kpe = proj[:, Q_LORA + KV_LORA:Q_LORA + KV_LORA + LANES]
    return q_heads, k, v_t, zc, zs, ckv, kpe


def _premix_kernel(x_ref, m_ref, g1_ref, win_ref, qg_ref, wuq_ref, kvg_ref, wk_ref, wv_ref, wcs_ref,
                   tq_ref, tk_ref, q_ref, k_ref, v_ref, zc_ref, zs_ref, *, rope):
    q_heads, k, v_t, zc, zs, _, _ = _premix_body(x_ref[...], m_ref, g1_ref, win_ref, qg_ref, wuq_ref, kvg_ref,
                                                 wk_ref, wv_ref, wcs_ref, tq_ref, tk_ref, rope)
    for hd, qh in enumerate(q_heads):
        q_ref[hd * HEAD_PAD:(hd + 1) * HEAD_PAD, :] = qh
    k_ref[...] = k
    v_ref[...] = v_t
    zc_ref[...] = zc
    zs_ref[...] = zs


def _premix(x, mods6, g1, win, qg, wuq, kvg, wk, wv, wcs, tq, tk, *, mod_row, tab_row, rope):
    n = x.shape[0]
    tm = MIX_TILE
    full = lambda a: pl.BlockSpec(a.shape, lambda i: (0,) * a.ndim)
    out_shape = [jax.ShapeDtypeStruct((QK_W, n), BF16), jax.ShapeDtypeStruct((n, QK_W), BF16),
                 jax.ShapeDtypeStruct((V_W, n), BF16), jax.ShapeDtypeStruct((n, FNET_W), BF16),
                 jax.ShapeDtypeStruct((n, FNET_W), BF16)]
    out_specs = [pl.BlockSpec((QK_W, tm), lambda i: (0, i)), pl.BlockSpec((tm, QK_W), lambda i: (i, 0)),
                 pl.BlockSpec((V_W, tm), lambda i: (0, i)), pl.BlockSpec((tm, FNET_W), lambda i: (i, 0)),
                 pl.BlockSpec((tm, FNET_W), lambda i: (i, 0))]
    return pl.pallas_call(
        functools.partial(_premix_kernel, rope=rope),
        grid=(n // tm,),
        in_specs=[pl.BlockSpec((tm, D_MODEL), lambda i: (i, 0)),
                  pl.BlockSpec((1, 6, D_MODEL), lambda i: (mod_row(i), 0, 0)),
                  full(g1), full(win), full(qg), full(wuq), full(kvg), full(wk), full(wv), full(wcs),
                  pl.BlockSpec((2 * LANES, tm), lambda i: (0, tab_row(i))),
                  pl.BlockSpec((tm, LANES), lambda i: (tab_row(i), 0))],
        out_specs=out_specs,
        out_shape=out_shape,
        compiler_params=_cparams(("parallel",)),
        name="premix",
    )(x, mods6, g1, win, qg, wuq, kvg, wk, wv, wcs, tq, tk)


def _cachekv_kernel(xk_ref, wk_ref, wv_ref, k_ref, v_ref):
    xk = xk_ref[...]
    k_ref[...] = _dot(xk, wk_ref[...]).astype(BF16)
    v_ref[...] = lax.dot_general(wv_ref[...], xk[:, 0:KV_LORA], _NT, preferred_element_type=F32).astype(BF16)


def _cachekv(xk, wk, wv):
    n = xk.shape[0]
    tm = 512
    full = lambda a: pl.BlockSpec(a.shape, lambda i: (0,) * a.ndim)
    return pl.pallas_call(
        _cachekv_kernel,
        grid=(n // tm,),
        in_specs=[pl.BlockSpec((tm, 2 * LANES), lambda i: (i, 0)), full(wk), full(wv)],
        out_specs=[pl.BlockSpec((tm, QK_W), lambda i: (i, 0)), pl.BlockSpec((V_W, tm), lambda i: (0, i))],
        out_shape=[jax.ShapeDtypeStruct((n, QK_W), BF16), jax.ShapeDtypeStruct((V_W, n), BF16)],
        compiler_params=_cparams(("parallel",)),
        name="cachekv",
    )(xk, wk, wv)


def _attn_body(q_heads, k_refs, v_refs, kc):
    tq = q_heads[0].shape[1]
    zero = jnp.zeros((HEAD_PAD, tq), BF16)
    n_pairs = len(q_heads) // 2
    qbd = [jnp.concatenate([jnp.concatenate([q_heads[2 * pr], zero], axis=1),
                            jnp.concatenate([zero, q_heads[2 * pr + 1]], axis=1)], axis=0) for pr in range(n_pairs)]
    chunks = [(k_ref, v_ref, c0, min(c0 + kc, k_ref.shape[0]))
              for k_ref, v_ref in zip(k_refs, v_refs) for c0 in range(0, k_ref.shape[0], kc)]
    work = [(pr, ch) for pr in range(n_pairs) for ch in chunks]

    def score(item):
        pr, (k_ref, _, c0, c1) = item
        return _dot(k_ref[c0:c1, pr * 2 * HEAD_PAD:(pr + 1) * 2 * HEAD_PAD], qbd[pr])

    m = [None] * n_pairs
    o = [None] * n_pairs
    s_next = score(work[0])
    for wi, (pr, (k_ref, v_ref, c0, c1)) in enumerate(work):
        s = s_next
        if wi + 1 < len(work):
            s_next = score(work[wi + 1])
        cm = jnp.max(s, axis=0, keepdims=True)
        vlo = pr * 2 * V_HEAD
        va = jnp.concatenate([v_ref[vlo:vlo + 2 * V_HEAD, c0:c1], jnp.ones((16, c1 - c0), BF16)], axis=0)
        if m[pr] is None:
            m[pr] = cm
            o[pr] = _dot(va, jnp.exp2((s - cm).astype(BF16)))
        else:
            m_new = jnp.maximum(m[pr], cm)
            alpha = jnp.exp2(m[pr] - m_new)
            o[pr] = alpha * o[pr] + _dot(va, jnp.exp2((s - m_new).astype(BF16)))
            m[pr] = m_new
    outs = []
    for pr in range(n_pairs):
        on = o[pr][0:2 * V_HEAD, :] * (1.0 / o[pr][2 * V_HEAD:2 * V_HEAD + 1, :])
        ot = jnp.concatenate([on[0:V_HEAD, 0:tq], on[V_HEAD:2 * V_HEAD, tq:2 * tq]], axis=0)
        outs.append(ot.T.astype(BF16))
    return outs


def _attn_kernel(q_ref, *refs, n_kv, n_pairs, kc):
    q_heads = [q_ref[hd * HEAD_PAD:(hd + 1) * HEAD_PAD, :] for hd in range(2 * n_pairs)]
    outs = _attn_body(q_heads, refs[:n_kv], refs[n_kv:2 * n_kv], kc)
    o_ref = refs[2 * n_kv]
    for pr, o in enumerate(outs):
        o_ref[:, pr * 2 * V_HEAD:(pr + 1) * 2 * V_HEAD] = o


def _attention(q_t, ks, vs_t, *, n_req, t_q, kv_lens, tq, pairs_per_step):
    n_kv = len(ks)
    nq = t_q // tq
    pp = pairs_per_step
    in_specs = [pl.BlockSpec((pp * 2 * HEAD_PAD, tq), lambda b, p, i: (p, b * nq + i))]
    in_specs += [pl.BlockSpec((kl, pp * 2 * HEAD_PAD), lambda b, p, i: (b, p)) for kl in kv_lens]
    in_specs += [pl.BlockSpec((pp * 2 * V_HEAD, kl), lambda b, p, i: (p, b)) for kl in kv_lens]
    return pl.pallas_call(
        functools.partial(_attn_kernel, n_kv=n_kv, n_pairs=pp, kc=KEY_CHUNK),
        grid=(n_req, N_HEADS // 2 // pp, nq),
        in_specs=in_specs,
        out_specs=pl.BlockSpec((tq, pp * 2 * V_HEAD), lambda b, p, i: (b * nq + i, p)),
        out_shape=jax.ShapeDtypeStruct((n_req * t_q, V_W), BF16),
        compiler_params=_cparams(("parallel", "parallel", "parallel")),
        name="attn",
    )(q_t, *ks, *vs_t)


def _mixout_body(x, attn_pairs, fm, wo_ref, m_ref, g2_ref, wr_ref):
    y = _dot(fm, wo_ref[V_W:V_W + FNET_W, :])
    col = 0
    for a in attn_pairs:
        y = y + _dot(a, wo_ref[col:col + a.shape[1], :])
        col += a.shape[1]
    gate1 = m_ref[0, 2:3, :]
    shift2 = m_ref[0, 3:4, :]
    scale2 = m_ref[0, 4:5, :]
    x1 = x + gate1 * y
    h2 = (_rms(x1, g2_ref[...]) * (1.0 + scale2) + shift2).astype(BF16)
    lg = lax.dot_general(wr_ref[...], h2, _NT, preferred_element_type=F32)
    e = jnp.exp(lg - jnp.max(lg, axis=0, keepdims=True))
    return x1, h2, e / jnp.sum(e, axis=0, keepdims=True)


def _mixout_kernel(x_ref, a_ref, zc_ref, zs_ref, cb_ref, sb_ref, off_ref, wo_ref, m_ref, g2_ref, wr_ref,
                   x1_ref, h2_ref, aff_ref, zp_ref, zm_ref, *, t):
    half = t // 2
    i = pl.program_id(1)

    @pl.when(i == 0)
    def _fold_halves():
        for src, col in ((zc_ref, 0), (zs_ref, FNET_W)):
            lo = src[0:half, :]
            hi = src[half:t, :]
            zp_ref[:, col:col + FNET_W] = lo + hi
            zm_ref[:, col:col + FNET_W] = lo - hi

    off = off_ref[pl.ds(i, 1), :]
    co = off[:, 0:half]
    so = off[:, half:t]
    cb = cb_ref[...]
    sb = sb_ref[...]
    ct = (cb * co - sb * so).astype(BF16)
    st = (sb * co + cb * so).astype(BF16)
    tr = cb.shape[0]
    h = tr // 2
    even = _dot(ct[0:h, :], zp_ref[:, 0:FNET_W]) - _dot(st[0:h, :], zp_ref[:, FNET_W:2 * FNET_W])
    odd = _dot(ct[h:, :], zm_ref[:, 0:FNET_W]) - _dot(st[h:, :], zm_ref[:, FNET_W:2 * FNET_W])
    k_i = lax.broadcasted_iota(jnp.int32, (tr, tr), 0)
    j_i = lax.broadcasted_iota(jnp.int32, (tr, tr), 1)
    perm = jnp.where(j_i == (k_i >> 1) + h * (k_i & 1), 1.0, 0.0).astype(BF16)
    fm = _dot(perm, jnp.concatenate([even, odd], axis=0).astype(BF16)).astype(BF16)
    x1, h2, aff = _mixout_body(x_ref[...], [a_ref[...]], fm, wo_ref, m_ref, g2_ref, wr_ref)
    x1_ref[...] = x1
    h2_ref[...] = h2
    aff_ref[...] = aff


def _ctx_front_kernel(x_ref, m_ref, g1_ref, win_ref, qg_ref, wuq_ref, kvg_ref, wk_ref, wv_ref, wcs_ref,
                      tq_ref, tk_ref, ct_ref, st_ref, wo_ref, g2_ref, wr_ref,
                      x1_ref, h2_ref, aff_ref, ckv_ref, kpe_ref, *, kc, t):
    x = x_ref[...]
    q_heads, k, v_t, zc, zs, ckv, kpe = _premix_body(x, m_ref, g1_ref, win_ref, qg_ref, wuq_ref, kvg_ref,
                                                     wk_ref, wv_ref, wcs_ref, tq_ref, tk_ref, False)
    ckv_ref[...] = ckv
    for r in range(x.shape[0] // t):
        kpe_ref[r] = kpe[r * t:(r + 1) * t, :].T[0:QK_ROPE, :]
    attn, fm = [], []
    for r in range(x.shape[0] // t):
        rows = slice(r * t, (r + 1) * t)
        attn.append(jnp.concatenate(_attn_body([q[:, rows] for q in q_heads], [k[rows, :]], [v_t[:, rows]], kc),
                                    axis=1))
        fm.append((_dot(ct_ref[...], zc[rows, :]) - _dot(st_ref[...], zs[rows, :])).astype(BF16))
    x1, h2, aff = _mixout_body(x, [jnp.concatenate(attn, axis=0)], jnp.concatenate(fm, axis=0),
                               wo_ref, m_ref, g2_ref, wr_ref)
    x1_ref[...] = x1
    h2_ref[...] = h2
    for r in range(x.shape[0] // t):
        aff_ref[r] = aff[:, r * t:(r + 1) * t]


def _ctx_front(x, mods6, g1, win, qg, wuq, kvg, wk, wv, wcs, tq, tk, ct, st, wo, g2, wr_t, *, n_req, t, mod_row):
    full = lambda a: pl.BlockSpec(a.shape, lambda b: (0,) * a.ndim)
    rps = CTX_REQS
    row = lambda w: pl.BlockSpec((rps * t, w), lambda b: (b, 0))
    return pl.pallas_call(
        functools.partial(_ctx_front_kernel, kc=KEY_CHUNK, t=t),
        grid=(n_req // rps,),
        in_specs=[row(D_MODEL), pl.BlockSpec((1, 6, D_MODEL), lambda b: (mod_row, 0, 0)),
                  full(g1), full(win), full(qg), full(wuq), full(kvg), full(wk), full(wv), full(wcs),
                  full(tq), full(tk), full(ct), full(st), full(wo), full(g2), full(wr_t)],
        out_specs=[row(D_MODEL), row(D_MODEL), pl.BlockSpec((rps, N_EXPERTS, t), lambda b: (b, 0, 0)),
                   row(KV_LORA), pl.BlockSpec((rps, QK_ROPE, t), lambda b: (b, 0, 0))],
        out_shape=[jax.ShapeDtypeStruct((n_req * t, D_MODEL), F32),
                   jax.ShapeDtypeStruct((n_req * t, D_MODEL), BF16),
                   jax.ShapeDtypeStruct((n_req, N_EXPERTS, t), F32),
                   jax.ShapeDtypeStruct((n_req * t, KV_LORA), F32),
                   jax.ShapeDtypeStruct((n_req, QK_ROPE, t), F32)],
        compiler_params=_cparams(("parallel",)),
        name="ctx_front",
    )(x, mods6, g1, win, qg, wuq, kvg, wk, wv, wcs, tq, tk, ct, st, wo, g2, wr_t)


def _mixout(x, attn, zc, zs, cb, sb, off, wo, mods6, g2, wr_t, *, n_req, t, mod_row):
    tr = MIX_TILE
    nr = t // tr
    full = lambda a: pl.BlockSpec(a.shape, lambda b, i: (0,) * a.ndim)
    return pl.pallas_call(
        functools.partial(_mixout_kernel, t=t),
        grid=(n_req, nr),
        in_specs=[pl.BlockSpec((tr, D_MODEL), lambda b, i: (b * nr + i, 0)),
                  pl.BlockSpec((tr, V_W), lambda b, i: (b * nr + i, 0)),
                  pl.BlockSpec((t, FNET_W), lambda b, i: (b, 0)),
                  pl.BlockSpec((t, FNET_W), lambda b, i: (b, 0)),
                  full(cb), full(sb), full(off),
                  full(wo),
                  pl.BlockSpec((1, 6, D_MODEL), lambda b, i: (mod_row(b), 0, 0)),
                  full(g2), full(wr_t)],
        out_specs=[pl.BlockSpec((tr, D_MODEL), lambda b, i: (b * nr + i, 0)),
                   pl.BlockSpec((tr, D_MODEL), lambda b, i: (b * nr + i, 0)),
                   pl.BlockSpec((None, N_EXPERTS, tr), lambda b, i: (b, 0, i))],
        out_shape=[jax.ShapeDtypeStruct((n_req * t, D_MODEL), F32),
                   jax.ShapeDtypeStruct((n_req * t, D_MODEL), BF16),
                   jax.ShapeDtypeStruct((n_req, N_EXPERTS, t), F32)],
        scratch_shapes=[pltpu.VMEM((t // 2, 2 * FNET_W), BF16), pltpu.VMEM((t // 2, 2 * FNET_W), BF16)],
        compiler_params=_cparams(("parallel", "arbitrary")),
        name="mixout",
    )(x, attn, zc, zs, cb, sb, off, wo, mods6, g2, wr_t)


def _prefix_count(flags, tri):
    n = flags.shape[1]
    carry = None
    outs = []
    ends = []
    for j in range(n // TOK_TILE):
        c = _dot(flags[:, j * TOK_TILE:(j + 1) * TOK_TILE].astype(BF16), tri)
        if carry is not None:
            c = c + carry
        outs.append(c)
        carry = c[:, TOK_TILE - 1:TOK_TILE]
        ends.append(carry)
    return (outs[0] if len(outs) == 1 else jnp.concatenate(outs, axis=1)), ends


def _route_kernel(aff_ref, pos_ref, off_ref, *, cap):
    a = aff_ref[...]
    rows = a.shape[0]
    capf = jnp.float32(cap)
    thr = jnp.zeros((rows, 1), jnp.int32)
    for bit in range(30, -1, -1):
        cand = thr | jnp.int32(1 << bit)
        cand_f = lax.bitcast_convert_type(cand, F32)
        cnt = jnp.sum(jnp.where(a >= cand_f, 1.0, 0.0), axis=1, keepdims=True)
        thr = jnp.where(cnt >= capf, cand, thr)
    thr_f = lax.bitcast_convert_type(thr, F32)
    above_f = lax.bitcast_convert_type(thr + 1, F32)
    gt = jnp.where(a >= above_f, 1.0, 0.0)
    tie = jnp.where(a >= thr_f, 1.0, 0.0) - gt
    need = capf - jnp.sum(gt, axis=1, keepdims=True)
    r_i = lax.broadcasted_iota(jnp.int32, (TOK_TILE, TOK_TILE), 0)
    c_i = lax.broadcasted_iota(jnp.int32, (TOK_TILE, TOK_TILE), 1)
    tri = jnp.where(r_i <= c_i, 1.0, 0.0).astype(BF16)
    tie_before = _prefix_count(tie, tri)[0] - tie
    sel = gt + tie * jnp.where(tie_before < need, 1.0, 0.0)
    count, ends = _prefix_count(sel, tri)
    pos_ref[...] = jnp.where(sel > 0.5, count - 1.0, -1.0)
    lane = lax.broadcasted_iota(jnp.int32, (rows, LANES), 1)
    offs = jnp.zeros((rows, LANES), F32)
    for j, end in enumerate(ends):
        offs = offs + jnp.where(lane == j + 1, end, 0.0)
    off_ref[...] = offs.astype(jnp.int32)


def _route(aff_t, cap):
    return pl.pallas_call(
        functools.partial(_route_kernel, cap=cap),
        out_shape=[jax.ShapeDtypeStruct(aff_t.shape, F32),
                   jax.ShapeDtypeStruct((aff_t.shape[0], LANES), jnp.int32)],
        compiler_params=pltpu.CompilerParams(vmem_limit_bytes=VMEM_LIMIT),
        name="route",
    )(aff_t)


def _gather_kernel(pos_ref, aff_ref, h_ref, xs_ref, g_ref, *, cap):
    rps, ne, n = pos_ref.shape
    slot = lax.broadcasted_iota(jnp.int32, (ne, cap, n), 1).astype(F32)
    for r in range(rps):
        pos = pos_ref[r]
        aff = aff_ref[r]
        hit = pos[:, None, :] == slot
        onehot = jnp.where(hit, 1.0, 0.0).reshape(ne * cap, n).astype(BF16)
        xs = _dot(onehot, h_ref[r * n:(r + 1) * n, :])
        xs_ref[:, r * cap:(r + 1) * cap, :] = xs.astype(BF16).reshape(ne, cap, D_MODEL)
        g_ref[:, r * cap:(r + 1) * cap, :] = jnp.sum(jnp.where(hit, aff[:, None, :], 0.0), axis=2, keepdims=True)


def _gather(pos_t, aff_t, h2, *, n_req, n, cap, rps):
    pos3 = pos_t.reshape(n_req, N_EXPERTS, n)
    aff3 = aff_t.reshape(n_req, N_EXPERTS, n)
    return pl.pallas_call(
        functools.partial(_gather_kernel, cap=cap),
        grid=(n_req // rps,),
        in_specs=[pl.BlockSpec((rps, N_EXPERTS, n), lambda b: (b, 0, 0)),
                  pl.BlockSpec((rps, N_EXPERTS, n), lambda b: (b, 0, 0)),
                  pl.BlockSpec((rps * n, D_MODEL), lambda b: (b, 0))],
        out_specs=[pl.BlockSpec((N_EXPERTS, rps * cap, D_MODEL), lambda b: (0, b, 0)),
                   pl.BlockSpec((N_EXPERTS, rps * cap, 1), lambda b: (0, b, 0))],
        out_shape=[jax.ShapeDtypeStruct((N_EXPERTS, n_req * cap, D_MODEL), BF16),
                   jax.ShapeDtypeStruct((N_EXPERTS, n_req * cap, 1), F32)],
        compiler_params=_cparams(("parallel",)),
        name="gather",
    )(pos3, aff3, h2)


def _window_plan(off_ref, b, j, cap, w):
    w0 = []
    need = jnp.int32(0)
    for e in range(N_EXPERTS):
        idx = (b * N_EXPERTS + e) * OFF_STRIDE + j
        base = (off_ref[idx] >> 4) << 4
        w0.append(base)
        need = jnp.maximum(need, off_ref[idx + 1] - base)
    return w0, (need + (w - 1)) >> (w.bit_length() - 1)


def _window(w0_e, p, cap, w):
    low = w0_e + p * w
    start = pl.multiple_of(jnp.minimum(low, cap - w), 16)
    return low, start


def _gather_win_kernel(off_ref, pos_ref, aff_ref, h_ref, xs_ref, g_ref, *, cap, w):
    b = pl.program_id(0)
    step = pl.program_id(1)

    @pl.when(step == 0)
    def _init():
        xs_ref[...] = jnp.zeros_like(xs_ref)
        g_ref[...] = jnp.zeros_like(g_ref)

    r = lax.broadcasted_iota(jnp.int32, (w, 1), 0).astype(F32)
    n_sub = pos_ref.shape[1] // TOK_TILE
    for sub in range(n_sub):
        cols = slice(sub * TOK_TILE, (sub + 1) * TOK_TILE)
        pos = pos_ref[:, cols]
        aff = aff_ref[:, cols]
        h = h_ref[cols, :]
        w0, n_pass = _window_plan(off_ref, b, step * n_sub + sub, cap, w)

        def one_pass(p, carry, pos=pos, aff=aff, h=h, w0=w0):
            starts, hots, gates = [], [], []
            for e in range(N_EXPERTS):
                low, start = _window(w0[e], p, cap, w)
                starts.append(start)
                mine = jnp.where(r >= (low - start).astype(F32), 1.0, 0.0)
                hot = jnp.where(pos[e:e + 1, :] - start.astype(F32) == r, mine, 0.0)
                hots.append(hot)
                gates.append(jnp.sum(hot * aff[e:e + 1, :], axis=1, keepdims=True))
            rows = _dot(jnp.concatenate(hots, axis=0).astype(BF16), h).astype(BF16)
            for e in range(N_EXPERTS):
                win = pl.ds(starts[e], w)
                xs_ref[e, win, :] = xs_ref[e, win, :] + rows[e * w:(e + 1) * w, :]
                g_ref[e, win, :] = g_ref[e, win, :] + gates[e]
            return carry

        one_pass(0, 0)
        lax.fori_loop(1, n_pass, one_pass, 0)


def _gather_win(offs, pos_t, aff_t, h2, *, n_req, n, cap):
    tb = WIN_BLOCKS * TOK_TILE
    nb = n // tb
    grid_spec = pltpu.PrefetchScalarGridSpec(
        num_scalar_prefetch=1,
        grid=(n_req, nb),
        in_specs=[pl.BlockSpec((N_EXPERTS, tb), lambda b, j, off: (b, j)),
                  pl.BlockSpec((N_EXPERTS, tb), lambda b, j, off: (b, j)),
                  pl.BlockSpec((tb, D_MODEL), lambda b, j, off: (b * nb + j, 0))],
        out_specs=[pl.BlockSpec((N_EXPERTS, cap, D_MODEL), lambda b, j, off: (0, b, 0)),
                   pl.BlockSpec((N_EXPERTS, cap, 1), lambda b, j, off: (0, b, 0))])
    return pl.pallas_call(
        functools.partial(_gather_win_kernel, cap=cap, w=SLOT_WIN),
        grid_spec=grid_spec,
        out_shape=[jax.ShapeDtypeStruct((N_EXPERTS, n_req * cap, D_MODEL), BF16),
                   jax.ShapeDtypeStruct((N_EXPERTS, n_req * cap, 1), F32)],
        compiler_params=_cparams(("parallel", "arbitrary")),
        name="gather_win",
    )(offs, pos_t, aff_t, h2)


def _ffn_kernel(xc_ref, xl_ref, gc_ref, gl_ref, wg_ref, wu_ref, wd_ref, yc_ref, yl_ref, wgb, wub, wdb):
    wgb[...] = wg_ref[0].astype(BF16)
    wub[...] = wu_ref[0].astype(BF16)
    wdb[...] = wd_ref[0].astype(BF16)
    for x_ref, g_ref, y_ref in ((xc_ref, gc_ref, yc_ref), (xl_ref, gl_ref, yl_ref)):
        for j in range(x_ref.shape[1] // TOK_TILE):
            rows = slice(j * TOK_TILE, (j + 1) * TOK_TILE)
            x = x_ref[0, rows, :]
            gate = _dot(x, wgb[...])
            up = _dot(x, wub[...])
            hid = (gate * jax.nn.sigmoid(gate) * up).astype(BF16)
            ys = _dot(hid, wdb[...]) * g_ref[0, rows, :]
            y_ref[0, rows, :] = ys.astype(BF16)


def _ffn(xc, xl, gc, gl, wg, wu, wd):
    m = xc.shape[1]
    xspec = pl.BlockSpec((1, m, D_MODEL), lambda e: (e, 0, 0))
    gspec = pl.BlockSpec((1, m, 1), lambda e: (e, 0, 0))
    shp = jax.ShapeDtypeStruct((N_EXPERTS, m, D_MODEL), BF16)
    return pl.pallas_call(
        _ffn_kernel,
        grid=(N_EXPERTS,),
        in_specs=[xspec, xspec, gspec, gspec,
                  pl.BlockSpec((1, D_MODEL, D_EXPERT), lambda e: (e, 0, 0)),
                  pl.BlockSpec((1, D_MODEL, D_EXPERT), lambda e: (e, 0, 0)),
                  pl.BlockSpec((1, D_EXPERT, D_MODEL), lambda e: (e, 0, 0))],
        out_specs=[xspec, xspec],
        out_shape=[shp, shp],
        scratch_shapes=[pltpu.VMEM((D_MODEL, D_EXPERT), BF16), pltpu.VMEM((D_MODEL, D_EXPERT), BF16),
                        pltpu.VMEM((D_EXPERT, D_MODEL), BF16)],
        compiler_params=_cparams(("arbitrary",)),
        name="ffn",
    )(xc, xl, gc, gl, wg, wu, wd)


def _combine_kernel(x1_ref, pos_ref, ys_ref, m_ref, fg_ref, o_ref, *, cap):
    rps, n, _ = pos_ref.shape
    w = N_EXPERTS * cap
    e_i = lax.broadcasted_iota(jnp.int32, (N_EXPERTS, w), 0)
    j_i = lax.broadcasted_iota(jnp.int32, (N_EXPERTS, w), 1)
    spread = jnp.where((j_i >> (cap.bit_length() - 1)) == e_i, 1.0, 0.0).astype(BF16)
    lane_slot = (lax.broadcasted_iota(jnp.int32, (1, w), 1) & (cap - 1)).astype(F32)
    gate2 = m_ref[0, 5:6, :]
    for r in range(rps):
        pos = pos_ref[r].astype(BF16)
        onehot = jnp.where(_dot(pos, spread) == lane_slot, 1.0, 0.0).astype(BF16)
        acc = _dot(onehot, ys_ref[:, r * cap:(r + 1) * cap, :].reshape(w, D_MODEL))
        rows = slice(r * n, (r + 1) * n)
        o_ref[rows, :] = _rms(x1_ref[rows, :] + gate2 * acc, fg_ref[...])


def _combine(x1, pos_tok, ys, mods6, fg, *, n_req, n, cap, mod_row, rps):
    return pl.pallas_call(
        functools.partial(_combine_kernel, cap=cap),
        grid=(n_req // rps,),
        in_specs=[pl.BlockSpec((rps * n, D_MODEL), lambda b: (b, 0)),
                  pl.BlockSpec((rps, n, N_EXPERTS), lambda b: (b, 0, 0)),
                  pl.BlockSpec((N_EXPERTS, rps * cap, D_MODEL), lambda b: (0, b, 0)),
                  pl.BlockSpec((1, 6, D_MODEL), lambda b: (mod_row, 0, 0)),
                  pl.BlockSpec((1, D_MODEL), lambda b: (0, 0))],
        out_specs=pl.BlockSpec((rps * n, D_MODEL), lambda b: (b, 0)),
        out_shape=jax.ShapeDtypeStruct((n_req * n, D_MODEL), F32),
        compiler_params=_cparams(("parallel",)),
        name="combine",
    )(x1, pos_tok, ys, mods6, fg)


def _combine_win_kernel(off_ref, x1_ref, pos_ref, ys_ref, m_ref, fg_ref, o_ref, acc_ref, *, cap, w):
    b = pl.program_id(0)
    step = pl.program_id(1)
    width = N_EXPERTS * w
    e_i = lax.broadcasted_iota(jnp.int32, (N_EXPERTS, width), 0)
    j_i = lax.broadcasted_iota(jnp.int32, (N_EXPERTS, width), 1)
    spread = jnp.where((j_i >> (w.bit_length() - 1)) == e_i, 1.0, 0.0).astype(BF16)
    lane_slot = (lax.broadcasted_iota(jnp.int32, (1, width), 1) & (w - 1)).astype(F32)
    lane_e = lax.broadcasted_iota(jnp.int32, (1, N_EXPERTS), 1)
    n_sub = pos_ref.shape[0] // TOK_TILE
    for sub in range(n_sub):
        rows = slice(sub * TOK_TILE, (sub + 1) * TOK_TILE)
        pos = pos_ref[rows, :]
        w0, n_pass = _window_plan(off_ref, b, step * n_sub + sub, cap, w)

        def window_sum(p, pos=pos, w0=w0):
            start_row = jnp.zeros((1, N_EXPERTS), F32)
            first_row = jnp.zeros((1, N_EXPERTS), F32)
            wins = []
            for e in range(N_EXPERTS):
                low, start = _window(w0[e], p, cap, w)
                start_row = jnp.where(lane_e == e, start.astype(F32), start_row)
                first_row = jnp.where(lane_e == e, (low - start).astype(F32), first_row)
                wins.append(ys_ref[e, pl.ds(start, w), :])
            rel = pos - start_row
            rel = jnp.where(rel >= first_row, rel, -1.0).astype(BF16)
            onehot = jnp.where(_dot(rel, spread) == lane_slot, 1.0, 0.0).astype(BF16)
            return _dot(onehot, jnp.concatenate(wins, axis=0))

        def finish(acc, rows=rows):
            o_ref[rows, :] = _rms(x1_ref[rows, :] + m_ref[0, 5:6, :] * acc, fg_ref[...])

        acc0 = window_sum(0)
        acc_ref[sub] = acc0
        finish(acc0)

        @pl.when(n_pass > 1)
        def _more_passes(sub=sub, n_pass=n_pass, window_sum=window_sum, finish=finish):
            def one_pass(p, carry):
                acc_ref[sub] += window_sum(p)
                return carry

            lax.fori_loop(1, n_pass, one_pass, 0)
            finish(acc_ref[sub])


def _combine_win(offs, x1, pos_tok, ys, mods6, fg, *, n_req, n, cap, mod_row):
    tr = WIN_BLOCKS * TOK_TILE
    nr = n // tr
    grid_spec = pltpu.PrefetchScalarGridSpec(
        num_scalar_prefetch=1,
        grid=(n_req, nr),
        in_specs=[pl.BlockSpec((tr, D_MODEL), lambda b, i, off: (b * nr + i, 0)),
                  pl.BlockSpec((None, tr, N_EXPERTS), lambda b, i, off: (b, i, 0)),
                  pl.BlockSpec((N_EXPERTS, cap, D_MODEL), lambda b, i, off: (0, b, 0)),
                  pl.BlockSpec((1, 6, D_MODEL), lambda b, i, off: (mod_row(b), 0, 0)),
                  pl.BlockSpec((1, D_MODEL), lambda b, i, off: (0, 0))],
        out_specs=pl.BlockSpec((tr, D_MODEL), lambda b, i, off: (b * nr + i, 0)),
        scratch_shapes=[pltpu.VMEM((WIN_BLOCKS, TOK_TILE, D_MODEL), F32)])
    return pl.pallas_call(
        functools.partial(_combine_win_kernel, cap=cap, w=SLOT_WIN),
        grid_spec=grid_spec,
        out_shape=jax.ShapeDtypeStruct((n_req * n, D_MODEL), F32),
        compiler_params=_cparams(("parallel", "parallel")),
        name="combine_win",
    )(offs, x1, pos_tok, ys, mods6, fg)


def _rot_half(w):
    half = QK_ROPE // 2
    return jnp.concatenate([-w[..., half:], w[..., :half]], axis=-1)


def _rope_tables(t):
    n_rows = t // GRID_W
    rows = np.repeat(np.arange(n_rows, dtype=np.float64), GRID_W)
    cols = np.tile(np.arange(GRID_W, dtype=np.float64), n_rows)
    n_freq = QK_ROPE // 4
    inv_freq = ROPE_BASE ** (-np.arange(n_freq, dtype=np.float64) / n_freq)
    ang = np.concatenate([rows[:, None] * inv_freq, cols[:, None] * inv_freq], axis=-1)
    cos = np.concatenate([np.cos(ang), np.cos(ang)], axis=-1)
    sin = np.concatenate([np.sin(ang), np.sin(ang)], axis=-1)
    return cos, sin


def _qk_tables(cos, sin):
    t = cos.shape[0]
    scale = (QK_NOPE + QK_ROPE) ** -0.5 * np.log2(np.e)
    pad = np.zeros((t, HEAD_PAD - QK_NOPE - QK_ROPE))
    cosq = np.concatenate([np.full((t, QK_NOPE), scale), cos * scale, pad], axis=1)
    sinq = np.concatenate([np.zeros((t, QK_NOPE)), sin * scale, pad], axis=1)
    tq_t = np.concatenate([cosq, sinq], axis=1).T
    tk = np.concatenate([cos, sin, np.zeros((t, LANES - 2 * QK_ROPE))], axis=1)
    return jnp.asarray(tq_t, F32), jnp.asarray(tk, F32)


def _dft_angles(rows, t):
    k = np.arange(t, dtype=np.int64)
    return ((rows[:, None] * k[None, :]) % t).astype(np.float64) * (2.0 * np.pi / t)


def _dft_tables(t):
    ang = _dft_angles(np.arange(t, dtype=np.int64), t)
    scale = (t * FNET_CH) ** -0.5
    return jnp.asarray(np.cos(ang) * scale, F32).astype(BF16), jnp.asarray(np.sin(ang) * scale, F32).astype(BF16)


def _dft_half_tables(t):
    r = np.arange(MIX_TILE, dtype=np.int64)
    ang = _dft_angles(np.concatenate([r[0::2], r[1::2]]), t)[:, :t // 2]
    scale = (t * FNET_CH) ** -0.5
    ang_off = _dft_angles(np.arange(t // MIX_TILE, dtype=np.int64) * MIX_TILE, t)[:, :t // 2]
    off = np.concatenate([np.cos(ang_off), np.sin(ang_off)], axis=1)
    return jnp.asarray(np.cos(ang) * scale, F32), jnp.asarray(np.sin(ang) * scale, F32), jnp.asarray(off, F32)


def _block_diag(w):
    g, a, b = w.shape
    eye = jnp.eye(g, dtype=w.dtype)
    return (eye[:, None, :, None] * w[:, :, None, :]).reshape(g * a, g * b)


def kernel(x_prompt, x_sample, cache_ckv, cache_kpe, c, c_ctx, w_mod, b_mod, norm1_g, w_in, q_norm_g, w_uq,
           kv_norm_g, w_ukv, w_fmix, w_out, norm2_g, w_router, w_e_gate, w_e_up, w_e_down, final_g):
    assert w_mod.shape[0] == 1, "single-layer problem"
    n_ctx, t_ctx, _ = x_prompt.shape
    n_lat, t_lat, _ = x_sample.shape
    past = cache_ckv.shape[2]
    ctx_row = n_lat

    w_in0 = w_in[0]
    kpe_cols = w_in0[:, Q_LORA + KV_LORA:Q_LORA + KV_LORA + QK_ROPE]
    win = jnp.concatenate([w_in0[:, :Q_LORA + KV_LORA + QK_ROPE], _rot_half(kpe_cols),
                           jnp.zeros((D_MODEL, 512 - Q_LORA - KV_LORA - 2 * QK_ROPE), F32),
                           w_in0[:, Q_LORA + KV_LORA + QK_ROPE:]], axis=1).astype(BF16)
    wq3 = w_uq[0].reshape(Q_LORA, N_HEADS, QK_NOPE + QK_ROPE)
    qpad = jnp.zeros((Q_LORA, N_HEADS, HEAD_PAD - QK_NOPE - QK_ROPE), F32)
    wuq_main = jnp.concatenate([wq3, qpad], axis=2).reshape(Q_LORA, QK_W)
    wuq_rot = _rot_half(wq3[..., QK_NOPE:]).reshape(Q_LORA, N_HEADS * QK_ROPE)
    wuq_lat = jnp.concatenate([wuq_main, wuq_rot], axis=1).T.astype(BF16)
    wuq_ctx = wuq_main.T.astype(BF16)
    wkv3 = w_ukv[0].reshape(KV_LORA, N_HEADS, QK_NOPE + V_HEAD)
    wk_top = jnp.concatenate([wkv3[..., :QK_NOPE], jnp.zeros((KV_LORA, N_HEADS, HEAD_PAD - QK_NOPE), F32)],
                             axis=2).reshape(KV_LORA, QK_W)
    place = jnp.concatenate([jnp.zeros((QK_ROPE, QK_NOPE), F32), jnp.eye(QK_ROPE, dtype=F32),
                             jnp.zeros((QK_ROPE, HEAD_PAD - QK_NOPE - QK_ROPE), F32)], axis=1)
    place = jnp.tile(place, (1, N_HEADS))
    wk = jnp.concatenate([wk_top, place, place, jnp.zeros((LANES - 2 * QK_ROPE, QK_W), F32)], axis=0).astype(BF16)
    wv = wkv3[..., QK_NOPE:].reshape(KV_LORA, V_W).T.astype(BF16)
    wo = w_out[0].astype(BF16)
    wr_t = w_router[0].T.astype(BF16)

    cos, sin = _rope_tables(t_lat)
    tq_lat, tk_lat = _qk_tables(cos, sin)
    assert n_ctx % CTX_REQS == 0
    tq_ctx, tk_ctx = _qk_tables(np.ones((CTX_REQS * t_ctx, QK_ROPE)), np.zeros((CTX_REQS * t_ctx, QK_ROPE)))
    ch_ang = _dft_angles(np.arange(FNET_CH, dtype=np.int64), FNET_CH)
    dft_ctx = _dft_tables(t_ctx)
    dft_lat = _dft_half_tables(t_lat)

    c8 = jnp.concatenate([c, c_ctx[None, :], jnp.zeros((8 - n_lat - 1, D_MODEL), F32)], axis=0)
    mods6 = _mods(c8, w_mod[0], b_mod[0][None, :]).reshape(8, 6, D_MODEL)
    cw, sw = _fold(jnp.asarray(np.cos(ch_ang), F32), jnp.asarray(np.sin(ch_ang), F32), w_fmix[0])
    gh = FNET_GROUPS // 2
    wcs = jnp.concatenate([jnp.concatenate([_block_diag(cw[a * gh:(a + 1) * gh]), _block_diag(sw[a * gh:(a + 1) * gh])],
                                           axis=1) for a in range(2)], axis=0).astype(BF16)

    g1 = norm1_g[0][None, :]
    qg = q_norm_g[0][None, :]
    kvg = kv_norm_g[0][None, :]
    g2 = norm2_g[0][None, :]
    fg = final_g[None, :]

    xp = x_prompt.reshape(n_ctx * t_ctx, D_MODEL)
    xs = x_sample.reshape(n_lat * t_lat, D_MODEL)
    tiles_lat = t_lat // MIX_TILE

    x1c, h2c, affc, ckv_c, kpe_c = _ctx_front(
        xp, mods6, g1, win, qg, wuq_ctx, kvg, wk, wv, wcs, tq_ctx, tk_ctx,
        *dft_ctx, wo, g2, wr_t, n_req=n_ctx, t=t_ctx, mod_row=ctx_row)
    ql, kl, vl, zcl, zsl = _premix(
        xs, mods6, g1, win, qg, wuq_lat, kvg, wk, wv, wcs, tq_lat, tk_lat,
        mod_row=lambda i: i // tiles_lat, tab_row=lambda i: i % tiles_lat, rope=True)
    xk_cache = jnp.concatenate([cache_ckv[:, 0], cache_kpe[:, 0],
                                jnp.zeros((n_lat, past, 2 * LANES - KV_LORA - QK_ROPE), F32)],
                               axis=-1).reshape(n_lat * past, 2 * LANES).astype(BF16)
    kpast, vpast = _cachekv(xk_cache, wk, wv)

    attn_l = _attention(ql, [kpast, kl], [vpast, vl], n_req=n_lat, t_q=t_lat, kv_lens=[past, t_lat], tq=TOK_TILE,
                        pairs_per_step=N_HEADS // 2)

    x1l, h2l, affl = _mixout(xs, attn_l, zcl, zsl, *dft_lat, wo, mods6, g2, wr_t,
                             n_req=n_lat, t=t_lat, mod_row=lambda b: b)

    cap_c = CAP_FACTOR * t_ctx // N_EXPERTS
    cap_l = CAP_FACTOR * t_lat // N_EXPERTS
    affc2 = affc.reshape(n_ctx * N_EXPERTS, t_ctx)
    affl2 = affl.reshape(n_lat * N_EXPERTS, t_lat)
    posc, _ = _route(affc2, cap_c)
    posl, offl = _route(affl2, cap_l)
    assert t_lat // TOK_TILE + 1 <= OFF_STRIDE and cap_l % SLOT_WIN == 0
    offl = offl[:, :OFF_STRIDE].reshape(-1)
    xsc, gc = _gather(posc, affc2, h2c, n_req=n_ctx, n=t_ctx, cap=cap_c, rps=MOE_REQS)
    xsl, gl = _gather_win(offl, posl, affl2, h2l, n_req=n_lat, n=t_lat, cap=cap_l)
    ysc, ysl = _ffn(xsc, xsl, gc, gl, w_e_gate[0], w_e_up[0], w_e_down[0])

    posc_tok = posc.reshape(n_ctx, N_EXPERTS, t_ctx).transpose(0, 2, 1)
    posl_tok = posl.reshape(n_lat, N_EXPERTS, t_lat).transpose(0, 2, 1)
    y_prompt = _combine(x1c, posc_tok, ysc, mods6, fg, n_req=n_ctx, n=t_ctx, cap=cap_c, mod_row=ctx_row,
                        rps=MOE_REQS)
    y_sample = _combine_win(offl, x1l, posl_tok, ysl, mods6, fg, n_req=n_lat, n=t_lat, cap=cap_l,
                            mod_row=lambda b: b)

    return (y_prompt.reshape(n_ctx, t_ctx, D_MODEL), y_sample.reshape(n_lat, t_lat, D_MODEL),
            ckv_c.reshape(n_ctx, 1, t_ctx, KV_LORA), kpe_c.transpose(0, 2, 1).reshape(n_ctx, 1, t_ctx, QK_ROPE))
```

```python
import functools

import jax
import jax.numpy as jnp
import numpy as np
from jax import lax
from jax.experimental import pallas as pl
from jax.experimental.pallas import tpu as pltpu

F32 = jnp.float32
BF16 = jnp.bfloat16

D_MODEL = 1024
N_HEADS = 8
QK_NOPE = 64
QK_ROPE = 32
V_HEAD = 64
Q_LORA = 256
KV_LORA = 128
FNET_GROUPS = 8
FNET_CH = 64
FNET_W = FNET_GROUPS * FNET_CH
N_EXPERTS = 16
CAP_FACTOR = 2
D_EXPERT = 512
GRID_W = 64
ROPE_BASE = 10000.0
EPS = 1e-6

LANES = 128
HEAD_PAD = LANES
QK_W = N_HEADS * HEAD_PAD
V_W = N_HEADS * V_HEAD
TOK_TILE = 256
MIX_TILE = 512
CTX_REQS = 4
MOE_REQS = 4
KEY_CHUNK = 512
SLOT_WIN = 64
WIN_BLOCKS = 4
OFF_STRIDE = 16
VMEM_LIMIT = 48 * 1024 * 1024

_NT = (((1,), (1,)), ((), ()))


def _cparams(sem):
    return pltpu.CompilerParams(dimension_semantics=sem, vmem_limit_bytes=VMEM_LIMIT)


def _rms(x, g):
    return x * lax.rsqrt(jnp.mean(x * x, axis=-1, keepdims=True) + EPS) * g


def _dot(a, b):
    return jnp.dot(a, b, preferred_element_type=F32)


def _mods_kernel(c_ref, w_ref, b_ref, o_ref):
    c = c_ref[...]
    s = c * jax.nn.sigmoid(c)
    o_ref[...] = _dot(s.astype(BF16), w_ref[...].astype(BF16)) + b_ref[...]


def _mods(c8, w_mod, b_mod):
    n = w_mod.shape[1]
    tn = 1536
    return pl.pallas_call(
        _mods_kernel,
        grid=(n // tn,),
        in_specs=[pl.BlockSpec((8, D_MODEL), lambda j: (0, 0)),
                  pl.BlockSpec((D_MODEL, tn), lambda j: (0, j)),
                  pl.BlockSpec((1, tn), lambda j: (0, j))],
        out_specs=pl.BlockSpec((8, tn), lambda j: (0, j)),
        out_shape=jax.ShapeDtypeStruct((8, n), F32),
        compiler_params=_cparams(("arbitrary",)),
        name="mods",
    )(c8, w_mod, b_mod)


def _fold_kernel(cc_ref, sc_ref, w_ref, cw_ref, sw_ref):
    for g in range(FNET_GROUPS):
        w = w_ref[g]
        cw_ref[g] = jnp.dot(cc_ref[...], w, preferred_element_type=F32, precision=lax.Precision.HIGHEST)
        sw_ref[g] = jnp.dot(sc_ref[...], w, preferred_element_type=F32, precision=lax.Precision.HIGHEST)


def _fold(cc, sc, w_fmix):
    shp = jax.ShapeDtypeStruct((FNET_GROUPS, FNET_CH, FNET_CH), F32)
    return pl.pallas_call(_fold_kernel, out_shape=(shp, shp), name="fold")(cc, sc, w_fmix)


def _premix_body(x, m_ref, g1_ref, win_ref, qg_ref, wuq_ref, kvg_ref, wk_ref, wv_ref, wcs_ref, tq_ref, tk_ref, rope):
    shift1 = m_ref[0, 0:1, :]
    scale1 = m_ref[0, 1:2, :]
    h = _rms(x, g1_ref[...]) * (1.0 + scale1) + shift1
    proj = _dot(h.astype(BF16), win_ref[...])
    qn = _rms(proj[:, 0:Q_LORA], qg_ref[...]).astype(BF16)
    qq = lax.dot_general(wuq_ref[...], qn, _NT, preferred_element_type=F32)
    cosq = tq_ref[0:LANES, :]
    sinq = tq_ref[LANES:2 * LANES, :]
    q_heads = []
    for hd in range(N_HEADS):
        lo = hd * HEAD_PAD
        qh = qq[lo:lo + HEAD_PAD, :] * cosq
        if rope:
            rot = qq[QK_W + hd * QK_ROPE:QK_W + (hd + 1) * QK_ROPE, :] * sinq[QK_NOPE:QK_NOPE + QK_ROPE, :]
            qh = qh + jnp.concatenate([jnp.zeros((QK_NOPE, rot.shape[1]), F32), rot,
                                       jnp.zeros((HEAD_PAD - QK_NOPE - QK_ROPE, rot.shape[1]), F32)], axis=0)
        q_heads.append(qh.astype(BF16))
    ckv = _rms(proj[:, Q_LORA:Q_LORA + KV_LORA], kvg_ref[...])
    kpe2 = proj[:, Q_LORA + KV_LORA:Q_LORA + KV_LORA + LANES] * tk_ref[...]
    xk = jnp.concatenate([ckv, kpe2], axis=1).astype(BF16)
    k = _dot(xk, wk_ref[...]).astype(BF16)
    v_t = lax.dot_general(wv_ref[...], xk[:, 0:KV_LORA], _NT, preferred_element_type=F32).astype(BF16)
    hw = FNET_W // 2
    f_in = proj[:, 512:1024].astype(BF16)
    z = [_dot(f_in[:, a * hw:(a + 1) * hw], wcs_ref[a * hw:(a + 1) * hw, :]) for a in range(2)]
    zc = jnp.concatenate([za[:, 0:hw] for za in z], axis=1).astype(BF16)
    zs = jnp.concatenate([za[:, hw:2 * hw] for za in z], axis=1).astype(BF16)
    kpe = proj[:, Q_LORA + KV_LORA:Q_LORA + KV_LORA + LANES]
    return q_heads, k, v_t, zc, zs, ckv, kpe


def _premix_kernel(x_ref, m_ref, g1_ref, win_ref, qg_ref, wuq_ref, kvg_ref, wk_ref, wv_ref, wcs_ref,
                   tq_ref, tk_ref, q_ref, k_ref, v_ref, zc_ref, zs_ref, *, rope):
    q_heads, k, v_t, zc, zs, _, _ = _premix_body(x_ref[...], m_ref, g1_ref, win_ref, qg_ref, wuq_ref, kvg_ref,
                                                 wk_ref, wv_ref, wcs_ref, tq_ref, tk_ref, rope)
    for hd, qh in enumerate(q_heads):
        q_ref[hd * HEAD_PAD:(hd + 1) * HEAD_PAD, :] = qh
    k_ref[...] = k
    v_ref[...] = v_t
    zc_ref[...] = zc
    zs_ref[...] = zs


def _premix(x, mods6, g1, win, qg, wuq, kvg, wk, wv, wcs, tq, tk, *, mod_row, tab_row, rope):
    n = x.shape[0]
    tm = MIX_TILE
    full = lambda a: pl.BlockSpec(a.shape, lambda i: (0,) * a.ndim)
    out_shape = [jax.ShapeDtypeStruct((QK_W, n), BF16), jax.ShapeDtypeStruct((n, QK_W), BF16),
                 jax.ShapeDtypeStruct((V_W, n), BF16), jax.ShapeDtypeStruct((n, FNET_W), BF16),
                 jax.ShapeDtypeStruct((n, FNET_W), BF16)]
    out_specs = [pl.BlockSpec((QK_W, tm), lambda i: (0, i)), pl.BlockSpec((tm, QK_W), lambda i: (i, 0)),
                 pl.BlockSpec((V_W, tm), lambda i: (0, i)), pl.BlockSpec((tm, FNET_W), lambda i: (i, 0)),
                 pl.BlockSpec((tm, FNET_W), lambda i: (i, 0))]
    return pl.pallas_call(
        functools.partial(_premix_kernel, rope=rope),
        grid=(n // tm,),
        in_specs=[pl.BlockSpec((tm, D_MODEL), lambda i: (i, 0)),
                  pl.BlockSpec((1, 6, D_MODEL), lambda i: (mod_row(i), 0, 0)),
                  full(g1), full(win), full(qg), full(wuq), full(kvg), full(wk), full(wv), full(wcs),
                  pl.BlockSpec((2 * LANES, tm), lambda i: (0, tab_row(i))),
                  pl.BlockSpec((tm, LANES), lambda i: (tab_row(i), 0))],
        out_specs=out_specs,
        out_shape=out_shape,
        compiler_params=_cparams(("parallel",)),
        name="premix",
    )(x, mods6, g1, win, qg, wuq, kvg, wk, wv, wcs, tq, tk)


def _cachekv_kernel(xk_ref, wk_ref, wv_ref, k_ref, v_ref):
    xk = xk_ref[...]
    k_ref[...] = _dot(xk, wk_ref[...]).astype(BF16)
    v_ref[...] = lax.dot_general(wv_ref[...], xk[:, 0:KV_LORA], _NT, preferred_element_type=F32).astype(BF16)


def _cachekv(xk, wk, wv):
    n = xk.shape[0]
    tm = 512
    full = lambda a: pl.BlockSpec(a.shape, lambda i: (0,) * a.ndim)
    return pl.pallas_call(
        _cachekv_kernel,
        grid=(n // tm,),
        in_specs=[pl.BlockSpec((tm, 2 * LANES), lambda i: (i, 0)), full(wk), full(wv)],
        out_specs=[pl.BlockSpec((tm, QK_W), lambda i: (i, 0)), pl.BlockSpec((V_W, tm), lambda i: (0, i))],
        out_shape=[jax.ShapeDtypeStruct((n, QK_W), BF16), jax.ShapeDtypeStruct((V_W, n), BF16)],
        compiler_params=_cparams(("parallel",)),
        name="cachekv",
    )(xk, wk, wv)


def _attn_body(q_heads, k_refs, v_refs, kc):
    tq = q_heads[0].shape[1]
    zero = jnp.zeros((HEAD_PAD, tq), BF16)
    n_pairs = len(q_heads) // 2
    qbd = [jnp.concatenate([jnp.concatenate([q_heads[2 * pr], zero], axis=1),
                            jnp.concatenate([zero, q_heads[2 * pr + 1]], axis=1)], axis=0) for pr in range(n_pairs)]
    chunks = [(k_ref, v_ref, c0, min(c0 + kc, k_ref.shape[0]))
              for k_ref, v_ref in zip(k_refs, v_refs) for c0 in range(0, k_ref.shape[0], kc)]
    work = [(pr, ch) for pr in range(n_pairs) for ch in chunks]

    def score(item):
        pr, (k_ref, _, c0, c1) = item
        return _dot(k_ref[c0:c1, pr * 2 * HEAD_PAD:(pr + 1) * 2 * HEAD_PAD], qbd[pr])

    m = [None] * n_pairs
    o = [None] * n_pairs
    s_next = score(work[0])
    for wi, (pr, (k_ref, v_ref, c0, c1)) in enumerate(work):
        s = s_next
        if wi + 1 < len(work):
            s_next = score(work[wi + 1])
        cm = jnp.max(s, axis=0, keepdims=True)
        vlo = pr * 2 * V_HEAD
        va = jnp.concatenate([v_ref[vlo:vlo + 2 * V_HEAD, c0:c1], jnp.ones((16, c1 - c0), BF16)], axis=0)
        if m[pr] is None:
            m[pr] = cm
            o[pr] = _dot(va, jnp.exp2((s - cm).astype(BF16)))
        else:
            m_new = jnp.maximum(m[pr], cm)
            alpha = jnp.exp2(m[pr] - m_new)
            o[pr] = alpha * o[pr] + _dot(va, jnp.exp2((s - m_new).astype(BF16)))
            m[pr] = m_new
    outs = []
    for pr in range(n_pairs):
        on = o[pr][0:2 * V_HEAD, :] * (1.0 / o[pr][2 * V_HEAD:2 * V_HEAD + 1, :])
        ot = jnp.concatenate([on[0:V_HEAD, 0:tq], on[V_HEAD:2 * V_HEAD, tq:2 * tq]], axis=0)
        outs.append(ot.T.astype(BF16))
    return outs


def _attn_kernel(q_ref, *refs, n_kv, n_pairs, kc):
    q_heads = [q_ref[hd * HEAD_PAD:(hd + 1) * HEAD_PAD, :] for hd in range(2 * n_pairs)]
    outs = _attn_body(q_heads, refs[:n_kv], refs[n_kv:2 * n_kv], kc)
    o_ref = refs[2 * n_kv]
    for pr, o in enumerate(outs):
        o_ref[:, pr * 2 * V_HEAD:(pr + 1) * 2 * V_HEAD] = o


def _attention(q_t, ks, vs_t, *, n_req, t_q, kv_lens, tq, pairs_per_step):
    n_kv = len(ks)
    nq = t_q // tq
    pp = pairs_per_step
    in_specs = [pl.BlockSpec((pp * 2 * HEAD_PAD, tq), lambda b, p, i: (p, b * nq + i))]
    in_specs += [pl.BlockSpec((kl, pp * 2 * HEAD_PAD), lambda b, p, i: (b, p)) for kl in kv_lens]
    in_specs += [pl.BlockSpec((pp * 2 * V_HEAD, kl), lambda b, p, i: (p, b)) for kl in kv_lens]
    return pl.pallas_call(
        functools.partial(_attn_kernel, n_kv=n_kv, n_pairs=pp, kc=KEY_CHUNK),
        grid=(n_req, N_HEADS // 2 // pp, nq),
        in_specs=in_specs,
        out_specs=pl.BlockSpec((tq, pp * 2 * V_HEAD), lambda b, p, i: (b * nq + i, p)),
        out_shape=jax.ShapeDtypeStruct((n_req * t_q, V_W), BF16),
        compiler_params=_cparams(("parallel", "parallel", "parallel")),
        name="attn",
    )(q_t, *ks, *vs_t)


def _mixout_body(x, attn_pairs, fm, wo_ref, m_ref, g2_ref, wr_ref):
    y = _dot(fm, wo_ref[V_W:V_W + FNET_W, :])
    col = 0
    for a in attn_pairs:
        y = y + _dot(a, wo_ref[col:col + a.shape[1], :])
        col += a.shape[1]
    gate1 = m_ref[0, 2:3, :]
    shift2 = m_ref[0, 3:4, :]
    scale2 = m_ref[0, 4:5, :]
    x1 = x + gate1 * y
    h2 = (_rms(x1, g2_ref[...]) * (1.0 + scale2) + shift2).astype(BF16)
    lg = lax.dot_general(wr_ref[...], h2, _NT, preferred_element_type=F32)
    e = jnp.exp(lg - jnp.max(lg, axis=0, keepdims=True))
    return x1, h2, e / jnp.sum(e, axis=0, keepdims=True)


def _mixout_kernel(x_ref, a_ref, zc_ref, zs_ref, cb_ref, sb_ref, off_ref, wo_ref, m_ref, g2_ref, wr_ref,
                   x1_ref, h2_ref, aff_ref, zp_ref, zm_ref, *, t):
    half = t // 2
    i = pl.program_id(1)

    @pl.when(i == 0)
    def _fold_halves():
        for src, col in ((zc_ref, 0), (zs_ref, FNET_W)):
            lo = src[0:half, :]
            hi = src[half:t, :]
            zp_ref[:, col:col + FNET_W] = lo + hi
            zm_ref[:, col:col + FNET_W] = lo - hi

    off = off_ref[pl.ds(i, 1), :]
    co = off[:, 0:half]
    so = off[:, half:t]
    cb = cb_ref[...]
    sb = sb_ref[...]
    ct = (cb * co - sb * so).astype(BF16)
    st = (sb * co + cb * so).astype(BF16)
    tr = cb.shape[0]
    h = tr // 2
    even = _dot(ct[0:h, :], zp_ref[:, 0:FNET_W]) - _dot(st[0:h, :], zp_ref[:, FNET_W:2 * FNET_W])
    odd = _dot(ct[h:, :], zm_ref[:, 0:FNET_W]) - _dot(st[h:, :], zm_ref[:, FNET_W:2 * FNET_W])
    k_i = lax.broadcasted_iota(jnp.int32, (tr, tr), 0)
    j_i = lax.broadcasted_iota(jnp.int32, (tr, tr), 1)
    perm = jnp.where(j_i == (k_i >> 1) + h * (k_i & 1), 1.0, 0.0).astype(BF16)
    fm = _dot(perm, jnp.concatenate([even, odd], axis=0).astype(BF16)).astype(BF16)
    x1, h2, aff = _mixout_body(x_ref[...], [a_ref[...]], fm, wo_ref, m_ref, g2_ref, wr_ref)
    x1_ref[...] = x1
    h2_ref[...] = h2
    aff_ref[...] = aff


def _ctx_front_kernel(x_ref, m_ref, g1_ref, win_ref, qg_ref, wuq_ref, kvg_ref, wk_ref, wv_ref, wcs_ref,
                      tq_ref, tk_ref, ct_ref, st_ref, wo_ref, g2_ref, wr_ref,
                      x1_ref, h2_ref, aff_ref, ckv_ref, kpe_ref, *, kc, t):
    x = x_ref[...]
    q_heads, k, v_t, zc, zs, ckv, kpe = _premix_body(x, m_ref, g1_ref, win_ref, qg_ref, wuq_ref, kvg_ref,
                                                     wk_ref, wv_ref, wcs_ref, tq_ref, tk_ref, False)
    ckv_ref[...] = ckv
    for r in range(x.shape[0] // t):
        kpe_ref[r] = kpe[r * t:(r + 1) * t, :].T[0:QK_ROPE, :]
    attn, fm = [], []
    for r in range(x.shape[0] // t):
        rows = slice(r * t, (r + 1) * t)
        attn.append(jnp.concatenate(_attn_body([q[:, rows] for q in q_heads], [k[rows, :]], [v_t[:, rows]], kc),
                                    axis=1))
        fm.append((_dot(ct_ref[...], zc[rows, :]) - _dot(st_ref[...], zs[rows, :])).astype(BF16))
    x1, h2, aff = _mixout_body(x, [jnp.concatenate(attn, axis=0)], jnp.concatenate(fm, axis=0),
                               wo_ref, m_ref, g2_ref, wr_ref)
    x1_ref[...] = x1
    h2_ref[...] = h2
    for r in range(x.shape[0] // t):
        aff_ref[r] = aff[:, r * t:(r + 1) * t]


def _ctx_front(x, mods6, g1, win, qg, wuq, kvg, wk, wv, wcs, tq, tk, ct, st, wo, g2, wr_t, *, n_req, t, mod_row):
    full = lambda a: pl.BlockSpec(a.shape, lambda b: (0,) * a.ndim)
    rps = CTX_REQS
    row = lambda w: pl.BlockSpec((rps * t, w), lambda b: (b, 0))
    return pl.pallas_call(
        functools.partial(_ctx_front_kernel, kc=KEY_CHUNK, t=t),
        grid=(n_req // rps,),
        in_specs=[row(D_MODEL), pl.BlockSpec((1, 6, D_MODEL), lambda b: (mod_row, 0, 0)),
                  full(g1), full(win), full(qg), full(wuq), full(kvg), full(wk), full(wv), full(wcs),
                  full(tq), full(tk), full(ct), full(st), full(wo), full(g2), full(wr_t)],
        out_specs=[row(D_MODEL), row(D_MODEL), pl.BlockSpec((rps, N_EXPERTS, t), lambda b: (b, 0, 0)),
                   row(KV_LORA), pl.BlockSpec((rps, QK_ROPE, t), lambda b: (b, 0, 0))],
        out_shape=[jax.ShapeDtypeStruct((n_req * t, D_MODEL), F32),
                   jax.ShapeDtypeStruct((n_req * t, D_MODEL), BF16),
                   jax.ShapeDtypeStruct((n_req, N_EXPERTS, t), F32),
                   jax.ShapeDtypeStruct((n_req * t, KV_LORA), F32),
                   jax.ShapeDtypeStruct((n_req, QK_ROPE, t), F32)],
        compiler_params=_cparams(("parallel",)),
        name="ctx_front",
    )(x, mods6, g1, win, qg, wuq, kvg, wk, wv, wcs, tq, tk, ct, st, wo, g2, wr_t)


def _mixout(x, attn, zc, zs, cb, sb, off, wo, mods6, g2, wr_t, *, n_req, t, mod_row):
    tr = MIX_TILE
    nr = t // tr
    full = lambda a: pl.BlockSpec(a.shape, lambda b, i: (0,) * a.ndim)
    return pl.pallas_call(
        functools.partial(_mixout_kernel, t=t),
        grid=(n_req, nr),
        in_specs=[pl.BlockSpec((tr, D_MODEL), lambda b, i: (b * nr + i, 0)),
                  pl.BlockSpec((tr, V_W), lambda b, i: (b * nr + i, 0)),
                  pl.BlockSpec((t, FNET_W), lambda b, i: (b, 0)),
                  pl.BlockSpec((t, FNET_W), lambda b, i: (b, 0)),
                  full(cb), full(sb), full(off),
                  full(wo),
                  pl.BlockSpec((1, 6, D_MODEL), lambda b, i: (mod_row(b), 0, 0)),
                  full(g2), full(wr_t)],
        out_specs=[pl.BlockSpec((tr, D_MODEL), lambda b, i: (b * nr + i, 0)),
                   pl.BlockSpec((tr, D_MODEL), lambda b, i: (b * nr + i, 0)),
                   pl.BlockSpec((None, N_EXPERTS, tr), lambda b, i: (b, 0, i))],
        out_shape=[jax.ShapeDtypeStruct((n_req * t, D_MODEL), F32),
                   jax.ShapeDtypeStruct((n_req * t, D_MODEL), BF16),
                   jax.ShapeDtypeStruct((n_req, N_EXPERTS, t), F32)],
        scratch_shapes=[pltpu.VMEM((t // 2, 2 * FNET_W), BF16), pltpu.VMEM((t // 2, 2 * FNET_W), BF16)],
        compiler_params=_cparams(("parallel", "arbitrary")),
        name="mixout",
    )(x, attn, zc, zs, cb, sb, off, wo, mods6, g2, wr_t)


def _prefix_count(flags, tri):
    n = flags.shape[1]
    carry = None
    outs = []
    ends = []
    for j in range(n // TOK_TILE):
        c = _dot(flags[:, j * TOK_TILE:(j + 1) * TOK_TILE].astype(BF16), tri)
        if carry is not None:
            c = c + carry
        outs.append(c)
        carry = c[:, TOK_TILE - 1:TOK_TILE]
        ends.append(carry)
    return (outs[0] if len(outs) == 1 else jnp.concatenate(outs, axis=1)), ends


def _route_kernel(aff_ref, pos_ref, off_ref, *, cap):
    a = aff_ref[...]
    rows = a.shape[0]
    capf = jnp.float32(cap)
    thr = jnp.zeros((rows, 1), jnp.int32)
    for bit in range(30, -1, -1):
        cand = thr | jnp.int32(1 << bit)
        cand_f = lax.bitcast_convert_type(cand, F32)
        cnt = jnp.sum(jnp.where(a >= cand_f, 1.0, 0.0), axis=1, keepdims=True)
        thr = jnp.where(cnt >= capf, cand, thr)
    thr_f = lax.bitcast_convert_type(thr, F32)
    above_f = lax.bitcast_convert_type(thr + 1, F32)
    gt = jnp.where(a >= above_f, 1.0, 0.0)
    tie = jnp.where(a >= thr_f, 1.0, 0.0) - gt
    need = capf - jnp.sum(gt, axis=1, keepdims=True)
    r_i = lax.broadcasted_iota(jnp.int32, (TOK_TILE, TOK_TILE), 0)
    c_i = lax.broadcasted_iota(jnp.int32, (TOK_TILE, TOK_TILE), 1)
    tri = jnp.where(r_i <= c_i, 1.0, 0.0).astype(BF16)
    tie_before = _prefix_count(tie, tri)[0] - tie
    sel = gt + tie * jnp.where(tie_before < need, 1.0, 0.0)
    count, ends = _prefix_count(sel, tri)
    pos_ref[...] = jnp.where(sel > 0.5, count - 1.0, -1.0)
    lane = lax.broadcasted_iota(jnp.int32, (rows, LANES), 1)
    offs = jnp.zeros((rows, LANES), F32)
    for j, end in enumerate(ends):
        offs = offs + jnp.where(lane == j + 1, end, 0.0)
    off_ref[...] = offs.astype(jnp.int32)


def _route(aff_t, cap):
    return pl.pallas_call(
        functools.partial(_route_kernel, cap=cap),
        out_shape=[jax.ShapeDtypeStruct(aff_t.shape, F32),
                   jax.ShapeDtypeStruct((aff_t.shape[0], LANES), jnp.int32)],
        compiler_params=pltpu.CompilerParams(vmem_limit_bytes=VMEM_LIMIT),
        name="route",
    )(aff_t)


def _gather_kernel(pos_ref, aff_ref, h_ref, xs_ref, g_ref, *, cap):
    rps, ne, n = pos_ref.shape
    slot = lax.broadcasted_iota(jnp.int32, (ne, cap, n), 1).astype(F32)
    for r in range(rps):
        pos = pos_ref[r]
        aff = aff_ref[r]
        hit = pos[:, None, :] == slot
        onehot = jnp.where(hit, 1.0, 0.0).reshape(ne * cap, n).astype(BF16)
        xs = _dot(onehot, h_ref[r * n:(r + 1) * n, :])
        xs_ref[:, r * cap:(r + 1) * cap, :] = xs.astype(BF16).reshape(ne, cap, D_MODEL)
        g_ref[:, r * cap:(r + 1) * cap, :] = jnp.sum(jnp.where(hit, aff[:, None, :], 0.0), axis=2, keepdims=True)


def _gather(pos_t, aff_t, h2, *, n_req, n, cap, rps):
    pos3 = pos_t.reshape(n_req, N_EXPERTS, n)
    aff3 = aff_t.reshape(n_req, N_EXPERTS, n)
    return pl.pallas_call(
        functools.partial(_gather_kernel, cap=cap),
        grid=(n_req // rps,),
        in_specs=[pl.BlockSpec((rps, N_EXPERTS, n), lambda b: (b, 0, 0)),
                  pl.BlockSpec((rps, N_EXPERTS, n), lambda b: (b, 0, 0)),
                  pl.BlockSpec((rps * n, D_MODEL), lambda b: (b, 0))],
        out_specs=[pl.BlockSpec((N_EXPERTS, rps * cap, D_MODEL), lambda b: (0, b, 0)),
                   pl.BlockSpec((N_EXPERTS, rps * cap, 1), lambda b: (0, b, 0))],
        out_shape=[jax.ShapeDtypeStruct((N_EXPERTS, n_req * cap, D_MODEL), BF16),
                   jax.ShapeDtypeStruct((N_EXPERTS, n_req * cap, 1), F32)],
        compiler_params=_cparams(("parallel",)),
        name="gather",
    )(pos3, aff3, h2)


def _window_plan(off_ref, b, j, cap, w):
    w0 = []
    need = jnp.int32(0)
    for e in range(N_EXPERTS):
        idx = (b * N_EXPERTS + e) * OFF_STRIDE + j
        base = (off_ref[idx] >> 4) << 4
        w0.append(base)
        need = jnp.maximum(need, off_ref[idx + 1] - base)
    return w0, (need + (w - 1)) >> (w.bit_length() - 1)


def _window(w0_e, p, cap, w):
    low = w0_e + p * w
    start = pl.multiple_of(jnp.minimum(low, cap - w), 16)
    return low, start


def _gather_win_kernel(off_ref, pos_ref, aff_ref, h_ref, xs_ref, g_ref, *, cap, w):
    b = pl.program_id(0)
    step = pl.program_id(1)

    @pl.when(step == 0)
    def _init():
        xs_ref[...] = jnp.zeros_like(xs_ref)
        g_ref[...] = jnp.zeros_like(g_ref)

    r = lax.broadcasted_iota(jnp.int32, (w, 1), 0).astype(F32)
    n_sub = pos_ref.shape[1] // TOK_TILE
    for sub in range(n_sub):
        cols = slice(sub * TOK_TILE, (sub + 1) * TOK_TILE)
        pos = pos_ref[:, cols]
        aff = aff_ref[:, cols]
        h = h_ref[cols, :]
        w0, n_pass = _window_plan(off_ref, b, step * n_sub + sub, cap, w)

        def one_pass(p, carry, pos=pos, aff=aff, h=h, w0=w0):
            starts, hots, gates = [], [], []
            for e in range(N_EXPERTS):
                low, start = _window(w0[e], p, cap, w)
                starts.append(start)
                mine = jnp.where(r >= (low - start).astype(F32), 1.0, 0.0)
                hot = jnp.where(pos[e:e + 1, :] - start.astype(F32) == r, mine, 0.0)
                hots.append(hot)
                gates.append(jnp.sum(hot * aff[e:e + 1, :], axis=1, keepdims=True))
            rows = _dot(jnp.concatenate(hots, axis=0).astype(BF16), h).astype(BF16)
            for e in range(N_EXPERTS):
                win = pl.ds(starts[e], w)
                xs_ref[e, win, :] = xs_ref[e, win, :] + rows[e * w:(e + 1) * w, :]
                g_ref[e, win, :] = g_ref[e, win, :] + gates[e]
            return carry

        one_pass(0, 0)
        lax.fori_loop(1, n_pass, one_pass, 0)


def _gather_win(offs, pos_t, aff_t, h2, *, n_req, n, cap):
    tb = WIN_BLOCKS * TOK_TILE
    nb = n // tb
    grid_spec = pltpu.PrefetchScalarGridSpec(
        num_scalar_prefetch=1,
        grid=(n_req, nb),
        in_specs=[pl.BlockSpec((N_EXPERTS, tb), lambda b, j, off: (b, j)),
                  pl.BlockSpec((N_EXPERTS, tb), lambda b, j, off: (b, j)),
                  pl.BlockSpec((tb, D_MODEL), lambda b, j, off: (b * nb + j, 0))],
        out_specs=[pl.BlockSpec((N_EXPERTS, cap, D_MODEL), lambda b, j, off: (0, b, 0)),
                   pl.BlockSpec((N_EXPERTS, cap, 1), lambda b, j, off: (0, b, 0))])
    return pl.pallas_call(
        functools.partial(_gather_win_kernel, cap=cap, w=SLOT_WIN),
        grid_spec=grid_spec,
        out_shape=[jax.ShapeDtypeStruct((N_EXPERTS, n_req * cap, D_MODEL), BF16),
                   jax.ShapeDtypeStruct((N_EXPERTS, n_req * cap, 1), F32)],
        compiler_params=_cparams(("parallel", "arbitrary")),
        name="gather_win",
    )(offs, pos_t, aff_t, h2)


def _ffn_kernel(xc_ref, xl_ref, gc_ref, gl_ref, wg_ref, wu_ref, wd_ref, yc_ref, yl_ref, wgb, wub, wdb):
    wgb[...] = wg_ref[0].astype(BF16)
    wub[...] = wu_ref[0].astype(BF16)
    wdb[...] = wd_ref[0].astype(BF16)
    for x_ref, g_ref, y_ref in ((xc_ref, gc_ref, yc_ref), (xl_ref, gl_ref, yl_ref)):
        for j in range(x_ref.shape[1] // TOK_TILE):
            rows = slice(j * TOK_TILE, (j + 1) * TOK_TILE)
            x = x_ref[0, rows, :]
            gate = _dot(x, wgb[...])
            up = _dot(x, wub[...])
            hid = (gate * jax.nn.sigmoid(gate) * up).astype(BF16)
            ys = _dot(hid, wdb[...]) * g_ref[0, rows, :]
            y_ref[0, rows, :] = ys.astype(BF16)


def _ffn(xc, xl, gc, gl, wg, wu, wd):
    m = xc.shape[1]
    xspec = pl.BlockSpec((1, m, D_MODEL), lambda e: (e, 0, 0))
    gspec = pl.BlockSpec((1, m, 1), lambda e: (e, 0, 0))
    shp = jax.ShapeDtypeStruct((N_EXPERTS, m, D_MODEL), BF16)
    return pl.pallas_call(
        _ffn_kernel,
        grid=(N_EXPERTS,),
        in_specs=[xspec, xspec, gspec, gspec,
                  pl.BlockSpec((1, D_MODEL, D_EXPERT), lambda e: (e, 0, 0)),
                  pl.BlockSpec((1, D_MODEL, D_EXPERT), lambda e: (e, 0, 0)),
                  pl.BlockSpec((1, D_EXPERT, D_MODEL), lambda e: (e, 0, 0))],
        out_specs=[xspec, xspec],
        out_shape=[shp, shp],
        scratch_shapes=[pltpu.VMEM((D_MODEL, D_EXPERT), BF16), pltpu.VMEM((D_MODEL, D_EXPERT), BF16),
                        pltpu.VMEM((D_EXPERT, D_MODEL), BF16)],
        compiler_params=_cparams(("arbitrary",)),
        name="ffn",
    )(xc, xl, gc, gl, wg, wu, wd)


def _combine_kernel(x1_ref, pos_ref, ys_ref, m_ref, fg_ref, o_ref, *, cap):
    rps, n, _ = pos_ref.shape
    w = N_EXPERTS * cap
    e_i = lax.broadcasted_iota(jnp.int32, (N_EXPERTS, w), 0)
    j_i = lax.broadcasted_iota(jnp.int32, (N_EXPERTS, w), 1)
    spread = jnp.where((j_i >> (cap.bit_length() - 1)) == e_i, 1.0, 0.0).astype(BF16)
    lane_slot = (lax.broadcasted_iota(jnp.int32, (1, w), 1) & (cap - 1)).astype(F32)
    gate2 = m_ref[0, 5:6, :]
    for r in range(rps):
        pos = pos_ref[r].astype(BF16)
        onehot = jnp.where(_dot(pos, spread) == lane_slot, 1.0, 0.0).astype(BF16)
        acc = _dot(onehot, ys_ref[:, r * cap:(r + 1) * cap, :].reshape(w, D_MODEL))
        rows = slice(r * n, (r + 1) * n)
        o_ref[rows, :] = _rms(x1_ref[rows, :] + gate2 * acc, fg_ref[...])


def _combine(x1, pos_tok, ys, mods6, fg, *, n_req, n, cap, mod_row, rps):
    return pl.pallas_call(
        functools.partial(_combine_kernel, cap=cap),
        grid=(n_req // rps,),
        in_specs=[pl.BlockSpec((rps * n, D_MODEL), lambda b: (b, 0)),
                  pl.BlockSpec((rps, n, N_EXPERTS), lambda b: (b, 0, 0)),
                  pl.BlockSpec((N_EXPERTS, rps * cap, D_MODEL), lambda b: (0, b, 0)),
                  pl.BlockSpec((1, 6, D_MODEL), lambda b: (mod_row, 0, 0)),
                  pl.BlockSpec((1, D_MODEL), lambda b: (0, 0))],
        out_specs=pl.BlockSpec((rps * n, D_MODEL), lambda b: (b, 0)),
        out_shape=jax.ShapeDtypeStruct((n_req * n, D_MODEL), F32),
        compiler_params=_cparams(("parallel",)),
        name="combine",
    )(x1, pos_tok, ys, mods6, fg)


def _combine_win_kernel(off_ref, x1_ref, pos_ref, ys_ref, m_ref, fg_ref, o_ref, acc_ref, *, cap, w):
    b = pl.program_id(0)
    step = pl.program_id(1)
    width = N_EXPERTS * w
    e_i = lax.broadcasted_iota(jnp.int32, (N_EXPERTS, width), 0)
    j_i = lax.broadcasted_iota(jnp.int32, (N_EXPERTS, width), 1)
    spread = jnp.where((j_i >> (w.bit_length() - 1)) == e_i, 1.0, 0.0).astype(BF16)
    lane_slot = (lax.broadcasted_iota(jnp.int32, (1, width), 1) & (w - 1)).astype(F32)
    lane_e = lax.broadcasted_iota(jnp.int32, (1, N_EXPERTS), 1)
    n_sub = pos_ref.shape[0] // TOK_TILE
    for sub in range(n_sub):
        rows = slice(sub * TOK_TILE, (sub + 1) * TOK_TILE)
        pos = pos_ref[rows, :]
        w0, n_pass = _window_plan(off_ref, b, step * n_sub + sub, cap, w)

        def window_sum(p, pos=pos, w0=w0):
            start_row = jnp.zeros((1, N_EXPERTS), F32)
            first_row = jnp.zeros((1, N_EXPERTS), F32)
            wins = []
            for e in range(N_EXPERTS):
                low, start = _window(w0[e], p, cap, w)
                start_row = jnp.where(lane_e == e, start.astype(F32), start_row)
                first_row = jnp.where(lane_e == e, (low - start).astype(F32), first_row)
                wins.append(ys_ref[e, pl.ds(start, w), :])
            rel = pos - start_row
            rel = jnp.where(rel >= first_row, rel, -1.0).astype(BF16)
            onehot = jnp.where(_dot(rel, spread) == lane_slot, 1.0, 0.0).astype(BF16)
            return _dot(onehot, jnp.concatenate(wins, axis=0))

        def finish(acc, rows=rows):
            o_ref[rows, :] = _rms(x1_ref[rows, :] + m_ref[0, 5:6, :] * acc, fg_ref[...])

        acc0 = window_sum(0)
        acc_ref[sub] = acc0
        finish(acc0)

        @pl.when(n_pass > 1)
        def _more_passes(sub=sub, n_pass=n_pass, window_sum=window_sum, finish=finish):
            def one_pass(p, carry):
                acc_ref[sub] += window_sum(p)
                return carry

            lax.fori_loop(1, n_pass, one_pass, 0)
            finish(acc_ref[sub])


def _combine_win(offs, x1, pos_tok, ys, mods6, fg, *, n_req, n, cap, mod_row):
    tr = WIN_BLOCKS * TOK_TILE
    nr = n // tr
    grid_spec = pltpu.PrefetchScalarGridSpec(
        num_scalar_prefetch=1,
        grid=(n_req, nr),
        in_specs=[pl.BlockSpec((tr, D_MODEL), lambda b, i, off: (b * nr + i, 0)),
                  pl.BlockSpec((None, tr, N_EXPERTS), lambda b, i, off: (b, i, 0)),
                  pl.BlockSpec((N_EXPERTS, cap, D_MODEL), lambda b, i, off: (0, b, 0)),
                  pl.BlockSpec((1, 6, D_MODEL), lambda b, i, off: (mod_row(b), 0, 0)),
                  pl.BlockSpec((1, D_MODEL), lambda b, i, off: (0, 0))],
        out_specs=pl.BlockSpec((tr, D_MODEL), lambda b, i, off: (b * nr + i, 0)),
        scratch_shapes=[pltpu.VMEM((WIN_BLOCKS, TOK_TILE, D_MODEL), F32)])
    return pl.pallas_call(
        functools.partial(_combine_win_kernel, cap=cap, w=SLOT_WIN),
        grid_spec=grid_spec,
        out_shape=jax.ShapeDtypeStruct((n_req * n, D_MODEL), F32),
        compiler_params=_cparams(("parallel", "parallel")),
        name="combine_win",
    )(offs, x1, pos_tok, ys, mods6, fg)


def _rot_half(w):
    half = QK_ROPE // 2
    return jnp.concatenate([-w[..., half:], w[..., :half]], axis=-1)


def _rope_tables(t):
    n_rows = t // GRID_W
    rows = np.repeat(np.arange(n_rows, dtype=np.float64), GRID_W)
    cols = np.tile(np.arange(GRID_W, dtype=np.float64), n_rows)
    n_freq = QK_ROPE // 4
    inv_freq = ROPE_BASE ** (-np.arange(n_freq, dtype=np.float64) / n_freq)
    ang = np.concatenate([rows[:, None] * inv_freq, cols[:, None] * inv_freq], axis=-1)
    cos = np.concatenate([np.cos(ang), np.cos(ang)], axis=-1)
    sin = np.concatenate([np.sin(ang), np.sin(ang)], axis=-1)
    return cos, sin


def _qk_tables(cos, sin):
    t = cos.shape[0]
    scale = (QK_NOPE + QK_ROPE) ** -0.5 * np.log2(np.e)
    pad = np.zeros((t, HEAD_PAD - QK_NOPE - QK_ROPE))
    cosq = np.concatenate([np.full((t, QK_NOPE), scale), cos * scale, pad], axis=1)
    sinq = np.concatenate([np.zeros((t, QK_NOPE)), sin * scale, pad], axis=1)
    tq_t = np.concatenate([cosq, sinq], axis=1).T
    tk = np.concatenate([cos, sin, np.zeros((t, LANES - 2 * QK_ROPE))], axis=1)
    return jnp.asarray(tq_t, F32), jnp.asarray(tk, F32)


def _dft_angles(rows, t):
    k = np.arange(t, dtype=np.int64)
    return ((rows[:, None] * k[None, :]) % t).astype(np.float64) * (2.0 * np.pi / t)


def _dft_tables(t):
    ang = _dft_angles(np.arange(t, dtype=np.int64), t)
    scale = (t * FNET_CH) ** -0.5
    return jnp.asarray(np.cos(ang) * scale, F32).astype(BF16), jnp.asarray(np.sin(ang) * scale, F32).astype(BF16)


def _dft_half_tables(t):
    r = np.arange(MIX_TILE, dtype=np.int64)
    ang = _dft_angles(np.concatenate([r[0::2], r[1::2]]), t)[:, :t // 2]
    scale = (t * FNET_CH) ** -0.5
    ang_off = _dft_angles(np.arange(t // MIX_TILE, dtype=np.int64) * MIX_TILE, t)[:, :t // 2]
    off = np.concatenate([np.cos(ang_off), np.sin(ang_off)], axis=1)
    return jnp.asarray(np.cos(ang) * scale, F32), jnp.asarray(np.sin(ang) * scale, F32), jnp.asarray(off, F32)


def _block_diag(w):
    g, a, b = w.shape
    eye = jnp.eye(g, dtype=w.dtype)
    return (eye[:, None, :, None] * w[:, :, None, :]).reshape(g * a, g * b)


def kernel(x_prompt, x_sample, cache_ckv, cache_kpe, c, c_ctx, w_mod, b_mod, norm1_g, w_in, q_norm_g, w_uq,
           kv_norm_g, w_ukv, w_fmix, w_out, norm2_g, w_router, w_e_gate, w_e_up, w_e_down, final_g):
    assert w_mod.shape[0] == 1, "single-layer problem"
    n_ctx, t_ctx, _ = x_prompt.shape
    n_lat, t_lat, _ = x_sample.shape
    past = cache_ckv.shape[2]
    ctx_row = n_lat

    w_in0 = w_in[0]
    kpe_cols = w_in0[:, Q_LORA + KV_LORA:Q_LORA + KV_LORA + QK_ROPE]
    win = jnp.concatenate([w_in0[:, :Q_LORA + KV_LORA + QK_ROPE], _rot_half(kpe_cols),
                           jnp.zeros((D_MODEL, 512 - Q_LORA - KV_LORA - 2 * QK_ROPE), F32),
                           w_in0[:, Q_LORA + KV_LORA + QK_ROPE:]], axis=1).astype(BF16)
    wq3 = w_uq[0].reshape(Q_LORA, N_HEADS, QK_NOPE + QK_ROPE)
    qpad = jnp.zeros((Q_LORA, N_HEADS, HEAD_PAD - QK_NOPE - QK_ROPE), F32)
    wuq_main = jnp.concatenate([wq3, qpad], axis=2).reshape(Q_LORA, QK_W)
    wuq_rot = _rot_half(wq3[..., QK_NOPE:]).reshape(Q_LORA, N_HEADS * QK_ROPE)
    wuq_lat = jnp.concatenate([wuq_main, wuq_rot], axis=1).T.astype(BF16)
    wuq_ctx = wuq_main.T.astype(BF16)
    wkv3 = w_ukv[0].reshape(KV_LORA, N_HEADS, QK_NOPE + V_HEAD)
    wk_top = jnp.concatenate([wkv3[..., :QK_NOPE], jnp.zeros((KV_LORA, N_HEADS, HEAD_PAD - QK_NOPE), F32)],
                             axis=2).reshape(KV_LORA, QK_W)
    place = jnp.concatenate([jnp.zeros((QK_ROPE, QK_NOPE), F32), jnp.eye(QK_ROPE, dtype=F32),
                             jnp.zeros((QK_ROPE, HEAD_PAD - QK_NOPE - QK_ROPE), F32)], axis=1)
    place = jnp.tile(place, (1, N_HEADS))
    wk = jnp.concatenate([wk_top, place, place, jnp.zeros((LANES - 2 * QK_ROPE, QK_W), F32)], axis=0).astype(BF16)
    wv = wkv3[..., QK_NOPE:].reshape(KV_LORA, V_W).T.astype(BF16)
    wo = w_out[0].astype(BF16)
    wr_t = w_router[0].T.astype(BF16)

    cos, sin = _rope_tables(t_lat)
    tq_lat, tk_lat = _qk_tables(cos, sin)
    assert n_ctx % CTX_REQS == 0
    tq_ctx, tk_ctx = _qk_tables(np.ones((CTX_REQS * t_ctx, QK_ROPE)), np.zeros((CTX_REQS * t_ctx, QK_ROPE)))
    ch_ang = _dft_angles(np.arange(FNET_CH, dtype=np.int64), FNET_CH)
    dft_ctx = _dft_tables(t_ctx)
    dft_lat = _dft_half_tables(t_lat)

    c8 = jnp.concatenate([c, c_ctx[None, :], jnp.zeros((8 - n_lat - 1, D_MODEL), F32)], axis=0)
    mods6 = _mods(c8, w_mod[0], b_mod[0][None, :]).reshape(8, 6, D_MODEL)
    cw, sw = _fold(jnp.asarray(np.cos(ch_ang), F32), jnp.asarray(np.sin(ch_ang), F32), w_fmix[0])
    gh = FNET_GROUPS // 2
    wcs = jnp.concatenate([jnp.concatenate([_block_diag(cw[a * gh:(a + 1) * gh]), _block_diag(sw[a * gh:(a + 1) * gh])],
                                           axis=1) for a in range(2)], axis=0).astype(BF16)

    g1 = norm1_g[0][None, :]
    qg = q_norm_g[0][None, :]
    kvg = kv_norm_g[0][None, :]
    g2 = norm2_g[0][None, :]
    fg = final_g[None, :]

    xp = x_prompt.reshape(n_ctx * t_ctx, D_MODEL)
    xs = x_sample.reshape(n_lat * t_lat, D_MODEL)
    tiles_lat = t_lat // MIX_TILE

    x1c, h2c, affc, ckv_c, kpe_c = _ctx_front(
        xp, mods6, g1, win, qg, wuq_ctx, kvg, wk, wv, wcs, tq_ctx, tk_ctx,
        *dft_ctx, wo, g2, wr_t, n_req=n_ctx, t=t_ctx, mod_row=ctx_row)
    ql, kl, vl, zcl, zsl = _premix(
        xs, mods6, g1, win, qg, wuq_lat, kvg, wk, wv, wcs, tq_lat, tk_lat,
        mod_row=lambda i: i // tiles_lat, tab_row=lambda i: i % tiles_lat, rope=True)
    xk_cache = jnp.concatenate([cache_ckv[:, 0], cache_kpe[:, 0],
                                jnp.zeros((n_lat, past, 2 * LANES - KV_LORA - QK_ROPE), F32)],
                               axis=-1).reshape(n_lat * past, 2 * LANES).astype(BF16)
    kpast, vpast = _cachekv(xk_cache, wk, wv)

    attn_l = _attention(ql, [kpast, kl], [vpast, vl], n_req=n_lat, t_q=t_lat, kv_lens=[past, t_lat], tq=TOK_TILE,
                        pairs_per_step=2)

    x1l, h2l, affl = _mixout(xs, attn_l, zcl, zsl, *dft_lat, wo, mods6, g2, wr_t,
                             n_req=n_lat, t=t_lat, mod_row=lambda b: b)

    cap_c = CAP_FACTOR * t_ctx // N_EXPERTS
    cap_l = CAP_FACTOR * t_lat // N_EXPERTS
    affc2 = affc.reshape(n_ctx * N_EXPERTS, t_ctx)
    affl2 = affl.reshape(n_lat * N_EXPERTS, t_lat)
    posc, _ = _route(affc2, cap_c)
    posl, offl = _route(affl2, cap_l)
    assert t_lat // TOK_TILE + 1 <= OFF_STRIDE and cap_l % SLOT_WIN == 0
    offl = offl[:, :OFF_STRIDE].reshape(-1)
    xsc, gc = _gather(posc, affc2, h2c, n_req=n_ctx, n=t_ctx, cap=cap_c, rps=MOE_REQS)
    xsl, gl = _gather_win(offl, posl, affl2, h2l, n_req=n_lat, n=t_lat, cap=cap_l)
    ysc, ysl = _ffn(xsc, xsl, gc, gl, w_e_gate[0], w_e_up[0], w_e_down[0])

    posc_tok = posc.reshape(n_ctx, N_EXPERTS, t_ctx).transpose(0, 2, 1)
    posl_tok = posl.reshape(n_lat, N_EXPERTS, t_lat).transpose(0, 2, 1)
    y_prompt = _combine(x1c, posc_tok, ysc, mods6, fg, n_req=n_ctx, n=t_ctx, cap=cap_c, mod_row=ctx_row,
                        rps=MOE_REQS)
    y_sample = _combine_win(offl, x1l, posl_tok, ysl, mods6, fg, n_req=n_lat, n=t_lat, cap=cap_l,
                            mod_row=lambda b: b)

    return (y_prompt.reshape(n_ctx, t_ctx, D_MODEL), y_sample.reshape(n_lat, t_lat, D_MODEL),
            ckv_c.reshape(n_ctx, 1, t_ctx, KV_LORA), kpe_c.transpose(0, 2, 1).reshape(n_ctx, 1, t_ctx, QK_ROPE))
```

```python
import functools

import jax
import jax.numpy as jnp
import numpy as np
from jax import lax
from jax.experimental import pallas as pl
from jax.experimental.pallas import tpu as pltpu

F32 = jnp.float32
BF16 = jnp.bfloat16

D_MODEL = 1024
N_HEADS = 8
QK_NOPE = 64
QK_ROPE = 32
V_HEAD = 64
Q_LORA = 256
KV_LORA = 128
FNET_GROUPS = 8
FNET_CH = 64
FNET_W = FNET_GROUPS * FNET_CH
N_EXPERTS = 16
CAP_FACTOR = 2
D_EXPERT = 512
GRID_W = 64
ROPE_BASE = 10000.0
EPS = 1e-6

LANES = 128
SUBLANES = 8
HEAD_PAD = LANES
QK_W = N_HEADS * HEAD_PAD
V_W = N_HEADS * V_HEAD
TOK_TILE = 256
MIX_TILE = 512
MODS_TILE = 1536
CTX_REQS = 4
MOE_REQS = 4
KEY_CHUNK = 512
SLOT_WIN = 64
WIN_BLOCKS = 4
OFF_STRIDE = 16
VMEM_LIMIT = 48 * 1024 * 1024

_NT = (((1,), (1,)), ((), ()))


def _cparams(sem):
    return pltpu.CompilerParams(dimension_semantics=sem, vmem_limit_bytes=VMEM_LIMIT)


def _rms(x, g):
    return x * lax.rsqrt(jnp.mean(x * x, axis=-1, keepdims=True) + EPS) * g


def _dot(a, b):
    return jnp.dot(a, b, preferred_element_type=F32)


def _mods_kernel(c_ref, w_ref, b_ref, o_ref):
    c = c_ref[...]
    s = c * jax.nn.sigmoid(c)
    o_ref[...] = _dot(s.astype(BF16), w_ref[...].astype(BF16)) + b_ref[...]


def _mods(c8, w_mod, b_mod):
    n = w_mod.shape[1]
    tn = MODS_TILE
    return pl.pallas_call(
        _mods_kernel,
        grid=(n // tn,),
        in_specs=[pl.BlockSpec((SUBLANES, D_MODEL), lambda j: (0, 0)),
                  pl.BlockSpec((D_MODEL, tn), lambda j: (0, j)),
                  pl.BlockSpec((1, tn), lambda j: (0, j))],
        out_specs=pl.BlockSpec((SUBLANES, tn), lambda j: (0, j)),
        out_shape=jax.ShapeDtypeStruct((SUBLANES, n), F32),
        compiler_params=_cparams(("arbitrary",)),
        name="mods",
    )(c8, w_mod, b_mod)


def _fold_kernel(cc_ref, sc_ref, w_ref, cw_ref, sw_ref):
    for g in range(FNET_GROUPS):
        w = w_ref[g]
        cw_ref[g] = jnp.dot(cc_ref[...], w, preferred_element_type=F32, precision=lax.Precision.HIGHEST)
        sw_ref[g] = jnp.dot(sc_ref[...], w, preferred_element_type=F32, precision=lax.Precision.HIGHEST)


def _fold(cc, sc, w_fmix):
    shp = jax.ShapeDtypeStruct((FNET_GROUPS, FNET_CH, FNET_CH), F32)
    return pl.pallas_call(_fold_kernel, out_shape=(shp, shp), name="fold")(cc, sc, w_fmix)


def _premix_body(x, m_ref, g1_ref, win_ref, qg_ref, wuq_ref, kvg_ref, wk_ref, wv_ref, wcs_ref, tq_ref, tk_ref, rope):
    shift1 = m_ref[0, 0:1, :]
    scale1 = m_ref[0, 1:2, :]
    h = _rms(x, g1_ref[...]) * (1.0 + scale1) + shift1
    proj = _dot(h.astype(BF16), win_ref[...])
    qn = _rms(proj[:, 0:Q_LORA], qg_ref[...]).astype(BF16)
    qq = lax.dot_general(wuq_ref[...], qn, _NT, preferred_element_type=F32)
    cosq = tq_ref[0:LANES, :]
    sinq = tq_ref[LANES:2 * LANES, :]
    q_heads = []
    for hd in range(N_HEADS):
        lo = hd * HEAD_PAD
        qh = qq[lo:lo + HEAD_PAD, :] * cosq
        if rope:
            rot = qq[QK_W + hd * QK_ROPE:QK_W + (hd + 1) * QK_ROPE, :] * sinq[QK_NOPE:QK_NOPE + QK_ROPE, :]
            qh = qh + jnp.concatenate([jnp.zeros((QK_NOPE, rot.shape[1]), F32), rot,
                                       jnp.zeros((HEAD_PAD - QK_NOPE - QK_ROPE, rot.shape[1]), F32)], axis=0)
        q_heads.append(qh.astype(BF16))
    ckv = _rms(proj[:, Q_LORA:Q_LORA + KV_LORA], kvg_ref[...])
    kpe2 = proj[:, Q_LORA + KV_LORA:Q_LORA + KV_LORA + LANES] * tk_ref[...]
    xk = jnp.concatenate([ckv, kpe2], axis=1).astype(BF16)
    k = _dot(xk, wk_ref[...]).astype(BF16)
    v_t = lax.dot_general(wv_ref[...], xk[:, 0:KV_LORA], _NT, preferred_element_type=F32).astype(BF16)
    hw = FNET_W // 2
    f_in = proj[:, 512:1024].astype(BF16)
    z = [_dot(f_in[:, a * hw:(a + 1) * hw], wcs_ref[a * hw:(a + 1) * hw, :]) for a in range(2)]
    zc = jnp.concatenate([za[:, 0:hw] for za in z], axis=1).astype(BF16)
    zs = jnp.concatenate([za[:, hw:2 * hw] for za in z], axis=1).astype(BF16)
    kpe = proj[:, Q_LORA + KV_LORA:Q_LORA + KV_LORA + LANES]
    return q_heads, k, v_t, zc, zs, ckv, kpe


def _premix_kernel(x_ref, m_ref, g1_ref, win_ref, qg_ref, wuq_ref, kvg_ref, wk_ref, wv_ref, wcs_ref,
                   tq_ref, tk_ref, q_ref, k_ref, v_ref, zc_ref, zs_ref, *, rope):
    q_heads, k, v_t, zc, zs, _, _ = _premix_body(x_ref[...], m_ref, g1_ref, win_ref, qg_ref, wuq_ref, kvg_ref,
                                                 wk_ref, wv_ref, wcs_ref, tq_ref, tk_ref, rope)
    for hd, qh in enumerate(q_heads):
        q_ref[hd * HEAD_PAD:(hd + 1) * HEAD_PAD, :] = qh
    k_ref[...] = k
    v_ref[...] = v_t
    zc_ref[...] = zc
    zs_ref[...] = zs


def _premix(x, mods6, g1, win, qg, wuq, kvg, wk, wv, wcs, tq, tk, *, mod_row, tab_row, rope):
    n = x.shape[0]
    tm = MIX_TILE
    full = lambda a: pl.BlockSpec(a.shape, lambda i: (0,) * a.ndim)
    out_shape = [jax.ShapeDtypeStruct((QK_W, n), BF16), jax.ShapeDtypeStruct((n, QK_W), BF16),
                 jax.ShapeDtypeStruct((V_W, n), BF16), jax.ShapeDtypeStruct((n, FNET_W), BF16),
                 jax.ShapeDtypeStruct((n, FNET_W), BF16)]
    out_specs = [pl.BlockSpec((QK_W, tm), lambda i: (0, i)), pl.BlockSpec((tm, QK_W), lambda i: (i, 0)),
                 pl.BlockSpec((V_W, tm), lambda i: (0, i)), pl.BlockSpec((tm, FNET_W), lambda i: (i, 0)),
                 pl.BlockSpec((tm, FNET_W), lambda i: (i, 0))]
    return pl.pallas_call(
        functools.partial(_premix_kernel, rope=rope),
        grid=(n // tm,),
        in_specs=[pl.BlockSpec((tm, D_MODEL), lambda i: (i, 0)),
                  pl.BlockSpec((1, 6, D_MODEL), lambda i: (mod_row(i), 0, 0)),
                  full(g1), full(win), full(qg), full(wuq), full(kvg), full(wk), full(wv), full(wcs),
                  pl.BlockSpec((2 * LANES, tm), lambda i: (0, tab_row(i))),
                  pl.BlockSpec((tm, LANES), lambda i: (tab_row(i), 0))],
        out_specs=out_specs,
        out_shape=out_shape,
        compiler_params=_cparams(("parallel",)),
        name="premix",
    )(x, mods6, g1, win, qg, wuq, kvg, wk, wv, wcs, tq, tk)


def _cachekv_kernel(xk_ref, wk_ref, wv_ref, k_ref, v_ref):
    xk = xk_ref[...]
    k_ref[...] = _dot(xk, wk_ref[...]).astype(BF16)
    v_ref[...] = lax.dot_general(wv_ref[...], xk[:, 0:KV_LORA], _NT, preferred_element_type=F32).astype(BF16)


def _cachekv(xk, wk, wv):
    n = xk.shape[0]
    tm = 512
    full = lambda a: pl.BlockSpec(a.shape, lambda i: (0,) * a.ndim)
    return pl.pallas_call(
        _cachekv_kernel,
        grid=(n // tm,),
        in_specs=[pl.BlockSpec((tm, 2 * LANES), lambda i: (i, 0)), full(wk), full(wv)],
        out_specs=[pl.BlockSpec((tm, QK_W), lambda i: (i, 0)), pl.BlockSpec((V_W, tm), lambda i: (0, i))],
        out_shape=[jax.ShapeDtypeStruct((n, QK_W), BF16), jax.ShapeDtypeStruct((V_W, n), BF16)],
        compiler_params=_cparams(("parallel",)),
        name="cachekv",
    )(xk, wk, wv)


def _attn_body(q_heads, k_refs, v_refs, kc):
    tq = q_heads[0].shape[1]
    zero = jnp.zeros((HEAD_PAD, tq), BF16)
    n_pairs = len(q_heads) // 2
    qbd = [jnp.concatenate([jnp.concatenate([q_heads[2 * pr], zero], axis=1),
                            jnp.concatenate([zero, q_heads[2 * pr + 1]], axis=1)], axis=0) for pr in range(n_pairs)]
    chunks = [(k_ref, v_ref, c0, min(c0 + kc, k_ref.shape[0]))
              for k_ref, v_ref in zip(k_refs, v_refs) for c0 in range(0, k_ref.shape[0], kc)]
    work = [(pr, ch) for pr in range(n_pairs) for ch in chunks]

    def score(item):
        pr, (k_ref, _, c0, c1) = item
        return _dot(k_ref[c0:c1, pr * 2 * HEAD_PAD:(pr + 1) * 2 * HEAD_PAD], qbd[pr]).astype(BF16)

    m = [None] * n_pairs
    o = [None] * n_pairs
    s_next = score(work[0])
    for wi, (pr, (k_ref, v_ref, c0, c1)) in enumerate(work):
        s = s_next
        if wi + 1 < len(work):
            s_next = score(work[wi + 1])
        cm = jnp.max(s, axis=0, keepdims=True)
        vlo = pr * 2 * V_HEAD
        va = jnp.concatenate([v_ref[vlo:vlo + 2 * V_HEAD, c0:c1], jnp.ones((16, c1 - c0), BF16)], axis=0)
        if m[pr] is None:
            m[pr] = cm
            o[pr] = _dot(va, jnp.exp2(s - cm))
        else:
            m_new = jnp.maximum(m[pr], cm)
            alpha = jnp.exp2(m[pr].astype(F32) - m_new.astype(F32))
            o[pr] = alpha * o[pr] + _dot(va, jnp.exp2(s - m_new))
            m[pr] = m_new
    outs = []
    for pr in range(n_pairs):
        on = o[pr][0:2 * V_HEAD, :] * (1.0 / o[pr][2 * V_HEAD:2 * V_HEAD + 1, :])
        ot = jnp.concatenate([on[0:V_HEAD, 0:tq], on[V_HEAD:2 * V_HEAD, tq:2 * tq]], axis=0)
        outs.append(ot.T.astype(BF16))
    return outs


def _attn_kernel(q_ref, *refs, n_kv, n_pairs, kc):
    q_heads = [q_ref[hd * HEAD_PAD:(hd + 1) * HEAD_PAD, :] for hd in range(2 * n_pairs)]
    outs = _attn_body(q_heads, refs[:n_kv], refs[n_kv:2 * n_kv], kc)
    o_ref = refs[2 * n_kv]
    for pr, o in enumerate(outs):
        o_ref[:, pr * 2 * V_HEAD:(pr + 1) * 2 * V_HEAD] = o


def _attention(q_t, ks, vs_t, *, n_req, t_q, kv_lens, tq, pairs_per_step):
    n_kv = len(ks)
    nq = t_q // tq
    pp = pairs_per_step
    in_specs = [pl.BlockSpec((pp * 2 * HEAD_PAD, tq), lambda b, p, i: (p, b * nq + i))]
    in_specs += [pl.BlockSpec((kl, pp * 2 * HEAD_PAD), lambda b, p, i: (b, p)) for kl in kv_lens]
    in_specs += [pl.BlockSpec((pp * 2 * V_HEAD, kl), lambda b, p, i: (p, b)) for kl in kv_lens]
    return pl.pallas_call(
        functools.partial(_attn_kernel, n_kv=n_kv, n_pairs=pp, kc=KEY_CHUNK),
        grid=(n_req, N_HEADS // 2 // pp, nq),
        in_specs=in_specs,
        out_specs=pl.BlockSpec((tq, pp * 2 * V_HEAD), lambda b, p, i: (b * nq + i, p)),
        out_shape=jax.ShapeDtypeStruct((n_req * t_q, V_W), BF16),
        compiler_params=_cparams(("parallel", "parallel", "parallel")),
        name="attn",
    )(q_t, *ks, *vs_t)


def _mixout_body(x, attn_pairs, fm, wo_ref, m_ref, g2_ref, wr_ref):
    y = _dot(fm, wo_ref[V_W:V_W + FNET_W, :])
    col = 0
    for a in attn_pairs:
        y = y + _dot(a, wo_ref[col:col + a.shape[1], :])
        col += a.shape[1]
    gate1 = m_ref[0, 2:3, :]
    shift2 = m_ref[0, 3:4, :]
    scale2 = m_ref[0, 4:5, :]
    x1 = x + gate1 * y
    h2 = (_rms(x1, g2_ref[...]) * (1.0 + scale2) + shift2).astype(BF16)
    lg = lax.dot_general(wr_ref[...], h2, _NT, preferred_element_type=F32)
    e = jnp.exp(lg - jnp.max(lg, axis=0, keepdims=True))
    return x1, h2, e / jnp.sum(e, axis=0, keepdims=True)


def _mixout_kernel(x_ref, a_ref, zc_ref, zs_ref, cb_ref, sb_ref, off_ref, wo_ref, m_ref, g2_ref, wr_ref,
                   x1_ref, h2_ref, aff_ref, zp_ref, zm_ref, *, t):
    half = t // 2
    i = pl.program_id(1)

    @pl.when(i == 0)
    def _fold_halves():
        for src, col in ((zc_ref, 0), (zs_ref, FNET_W)):
            lo = src[0:half, :]
            hi = src[half:t, :]
            zp_ref[:, col:col + FNET_W] = lo + hi
            zm_ref[:, col:col + FNET_W] = lo - hi

    off = off_ref[pl.ds(i, 1), :]
    co = off[:, 0:half]
    so = off[:, half:t]
    cb = cb_ref[...]
    sb = sb_ref[...]
    ct = (cb * co - sb * so).astype(BF16)
    st = (sb * co + cb * so).astype(BF16)
    tr = cb.shape[0]
    h = tr // 2
    even = _dot(ct[0:h, :], zp_ref[:, 0:FNET_W]) - _dot(st[0:h, :], zp_ref[:, FNET_W:2 * FNET_W])
    odd = _dot(ct[h:, :], zm_ref[:, 0:FNET_W]) - _dot(st[h:, :], zm_ref[:, FNET_W:2 * FNET_W])
    k_i = lax.broadcasted_iota(jnp.int32, (tr, tr), 0)
    j_i = lax.broadcasted_iota(jnp.int32, (tr, tr), 1)
    perm = jnp.where(j_i == (k_i >> 1) + h * (k_i & 1), 1.0, 0.0).astype(BF16)
    fm = _dot(perm, jnp.concatenate([even, odd], axis=0).astype(BF16)).astype(BF16)
    x1, h2, aff = _mixout_body(x_ref[...], [a_ref[...]], fm, wo_ref, m_ref, g2_ref, wr_ref)
    x1_ref[...] = x1
    h2_ref[...] = h2
    aff_ref[...] = aff


def _ctx_front_kernel(x_ref, m_ref, g1_ref, win_ref, qg_ref, wuq_ref, kvg_ref, wk_ref, wv_ref, wcs_ref,
                      tq_ref, tk_ref, ct_ref, st_ref, wo_ref, g2_ref, wr_ref,
                      x1_ref, h2_ref, aff_ref, ckv_ref, kpe_ref, *, kc, t):
    x = x_ref[...]
    q_heads, k, v_t, zc, zs, ckv, kpe = _premix_body(x, m_ref, g1_ref, win_ref, qg_ref, wuq_ref, kvg_ref,
                                                     wk_ref, wv_ref, wcs_ref, tq_ref, tk_ref, False)
    ckv_ref[...] = ckv
    for r in range(x.shape[0] // t):
        kpe_ref[r] = kpe[r * t:(r + 1) * t, :].T[0:QK_ROPE, :]
    attn, fm = [], []
    for r in range(x.shape[0] // t):
        rows = slice(r * t, (r + 1) * t)
        attn.append(jnp.concatenate(_attn_body([q[:, rows] for q in q_heads], [k[rows, :]], [v_t[:, rows]], kc),
                                    axis=1))
        fm.append((_dot(ct_ref[...], zc[rows, :]) - _dot(st_ref[...], zs[rows, :])).astype(BF16))
    x1, h2, aff = _mixout_body(x_ref[...], [jnp.concatenate(attn, axis=0)], jnp.concatenate(fm, axis=0),
                               wo_ref, m_ref, g2_ref, wr_ref)
    x1_ref[...] = x1
    h2_ref[...] = h2
    for r in range(x.shape[0] // t):
        aff_ref[r] = aff[:, r * t:(r + 1) * t]


def _ctx_front(x, mods6, g1, win, qg, wuq, kvg, wk, wv, wcs, tq, tk, ct, st, wo, g2, wr_t, *, n_req, t, mod_row):
    full = lambda a: pl.BlockSpec(a.shape, lambda b: (0,) * a.ndim)
    rps = CTX_REQS
    row = lambda w: pl.BlockSpec((rps * t, w), lambda b: (b, 0))
    return pl.pallas_call(
        functools.partial(_ctx_front_kernel, kc=KEY_CHUNK, t=t),
        grid=(n_req // rps,),
        in_specs=[row(D_MODEL), pl.BlockSpec((1, 6, D_MODEL), lambda b: (mod_row, 0, 0)),
                  full(g1), full(win), full(qg), full(wuq), full(kvg), full(wk), full(wv), full(wcs),
                  full(tq), full(tk), full(ct), full(st), full(wo), full(g2), full(wr_t)],
        out_specs=[row(D_MODEL), row(D_MODEL), pl.BlockSpec((rps, N_EXPERTS, t), lambda b: (b, 0, 0)),
                   row(KV_LORA), pl.BlockSpec((rps, QK_ROPE, t), lambda b: (b, 0, 0))],
        out_shape=[jax.ShapeDtypeStruct((n_req * t, D_MODEL), F32),
                   jax.ShapeDtypeStruct((n_req * t, D_MODEL), BF16),
                   jax.ShapeDtypeStruct((n_req, N_EXPERTS, t), F32),
                   jax.ShapeDtypeStruct((n_req * t, KV_LORA), F32),
                   jax.ShapeDtypeStruct((n_req, QK_ROPE, t), F32)],
        compiler_params=_cparams(("parallel",)),
        name="ctx_front",
    )(x, mods6, g1, win, qg, wuq, kvg, wk, wv, wcs, tq, tk, ct, st, wo, g2, wr_t)


def _mixout(x, attn, zc, zs, cb, sb, off, wo, mods6, g2, wr_t, *, n_req, t, mod_row):
    tr = MIX_TILE
    nr = t // tr
    full = lambda a: pl.BlockSpec(a.shape, lambda b, i: (0,) * a.ndim)
    return pl.pallas_call(
        functools.partial(_mixout_kernel, t=t),
        grid=(n_req, nr),
        in_specs=[pl.BlockSpec((tr, D_MODEL), lambda b, i: (b * nr + i, 0)),
                  pl.BlockSpec((tr, V_W), lambda b, i: (b * nr + i, 0)),
                  pl.BlockSpec((t, FNET_W), lambda b, i: (b, 0)),
                  pl.BlockSpec((t, FNET_W), lambda b, i: (b, 0)),
                  full(cb), full(sb), full(off),
                  full(wo),
                  pl.BlockSpec((1, 6, D_MODEL), lambda b, i: (mod_row(b), 0, 0)),
                  full(g2), full(wr_t)],
        out_specs=[pl.BlockSpec((tr, D_MODEL), lambda b, i: (b * nr + i, 0)),
                   pl.BlockSpec((tr, D_MODEL), lambda b, i: (b * nr + i, 0)),
                   pl.BlockSpec((None, N_EXPERTS, tr), lambda b, i: (b, 0, i))],
        out_shape=[jax.ShapeDtypeStruct((n_req * t, D_MODEL), F32),
                   jax.ShapeDtypeStruct((n_req * t, D_MODEL), BF16),
                   jax.ShapeDtypeStruct((n_req, N_EXPERTS, t), F32)],
        scratch_shapes=[pltpu.VMEM((t // 2, 2 * FNET_W), BF16), pltpu.VMEM((t // 2, 2 * FNET_W), BF16)],
        compiler_params=_cparams(("parallel", "arbitrary")),
        name="mixout",
    )(x, attn, zc, zs, cb, sb, off, wo, mods6, g2, wr_t)


def _prefix_count(flags, tri):
    n = flags.shape[1]
    carry = None
    outs = []
    ends = []
    for j in range(n // TOK_TILE):
        c = _dot(flags[:, j * TOK_TILE:(j + 1) * TOK_TILE].astype(BF16), tri)
        if carry is not None:
            c = c + carry
        outs.append(c)
        carry = c[:, TOK_TILE - 1:TOK_TILE]
        ends.append(carry)
    return (outs[0] if len(outs) == 1 else jnp.concatenate(outs, axis=1)), ends


def _route_kernel(aff_ref, pos_ref, off_ref, *, cap):
    a = aff_ref[...]
    rows = a.shape[0]
    capf = jnp.float32(cap)
    thr = jnp.zeros((rows, 1), jnp.int32)
    for bit in range(30, -1, -1):
        cand = thr | jnp.int32(1 << bit)
        cand_f = lax.bitcast_convert_type(cand, F32)
        cnt = jnp.sum(jnp.where(a >= cand_f, 1.0, 0.0), axis=1, keepdims=True)
        thr = jnp.where(cnt >= capf, cand, thr)
    thr_f = lax.bitcast_convert_type(thr, F32)
    above_f = lax.bitcast_convert_type(thr + 1, F32)
    gt = jnp.where(a >= above_f, 1.0, 0.0)
    tie = jnp.where(a >= thr_f, 1.0, 0.0) - gt
    need = capf - jnp.sum(gt, axis=1, keepdims=True)
    r_i = lax.broadcasted_iota(jnp.int32, (TOK_TILE, TOK_TILE), 0)
    c_i = lax.broadcasted_iota(jnp.int32, (TOK_TILE, TOK_TILE), 1)
    tri = jnp.where(r_i <= c_i, 1.0, 0.0).astype(BF16)
    tie_before = _prefix_count(tie, tri)[0] - tie
    sel = gt + tie * jnp.where(tie_before < need, 1.0, 0.0)
    count, ends = _prefix_count(sel, tri)
    pos_ref[...] = jnp.where(sel > 0.5, count - 1.0, -1.0)
    lane = lax.broadcasted_iota(jnp.int32, (rows, LANES), 1)
    offs = jnp.zeros((rows, LANES), F32)
    for j, end in enumerate(ends):
        offs = offs + jnp.where(lane == j + 1, end, 0.0)
    off_ref[...] = offs.astype(jnp.int32)


def _route(aff_t, cap):
    return pl.pallas_call(
        functools.partial(_route_kernel, cap=cap),
        out_shape=[jax.ShapeDtypeStruct(aff_t.shape, F32),
                   jax.ShapeDtypeStruct((aff_t.shape[0], LANES), jnp.int32)],
        compiler_params=pltpu.CompilerParams(vmem_limit_bytes=VMEM_LIMIT),
        name="route",
    )(aff_t)


def _gather_kernel(pos_ref, aff_ref, h_ref, xs_ref, g_ref, *, cap):
    rps, ne, n = pos_ref.shape
    slot = lax.broadcasted_iota(jnp.int32, (ne, cap, n), 1).astype(F32)
    for r in range(rps):
        pos = pos_ref[r]
        aff = aff_ref[r]
        hit = pos[:, None, :] == slot
        onehot = jnp.where(hit, 1.0, 0.0).reshape(ne * cap, n).astype(BF16)
        xs = _dot(onehot, h_ref[r * n:(r + 1) * n, :])
        xs_ref[:, r * cap:(r + 1) * cap, :] = xs.astype(BF16).reshape(ne, cap, D_MODEL)
        g_ref[:, r * cap:(r + 1) * cap, :] = jnp.sum(jnp.where(hit, aff[:, None, :], 0.0), axis=2, keepdims=True)


def _gather(pos_t, aff_t, h2, *, n_req, n, cap, rps):
    pos3 = pos_t.reshape(n_req, N_EXPERTS, n)
    aff3 = aff_t.reshape(n_req, N_EXPERTS, n)
    return pl.pallas_call(
        functools.partial(_gather_kernel, cap=cap),
        grid=(n_req // rps,),
        in_specs=[pl.BlockSpec((rps, N_EXPERTS, n), lambda b: (b, 0, 0)),
                  pl.BlockSpec((rps, N_EXPERTS, n), lambda b: (b, 0, 0)),
                  pl.BlockSpec((rps * n, D_MODEL), lambda b: (b, 0))],
        out_specs=[pl.BlockSpec((N_EXPERTS, rps * cap, D_MODEL), lambda b: (0, b, 0)),
                   pl.BlockSpec((N_EXPERTS, rps * cap, 1), lambda b: (0, b, 0))],
        out_shape=[jax.ShapeDtypeStruct((N_EXPERTS, n_req * cap, D_MODEL), BF16),
                   jax.ShapeDtypeStruct((N_EXPERTS, n_req * cap, 1), F32)],
        compiler_params=_cparams(("parallel",)),
        name="gather",
    )(pos3, aff3, h2)


def _window_plan(off_ref, b, j, cap, w):
    w0 = []
    need = jnp.int32(0)
    for e in range(N_EXPERTS):
        idx = (b * N_EXPERTS + e) * OFF_STRIDE + j
        base = (off_ref[idx] >> 4) << 4
        w0.append(base)
        need = jnp.maximum(need, off_ref[idx + 1] - base)
    return w0, (need + (w - 1)) >> (w.bit_length() - 1)


def _window(w0_e, p, cap, w):
    low = w0_e + p * w
    start = pl.multiple_of(jnp.minimum(low, cap - w), 16)
    return low, start


def _gather_win_kernel(off_ref, pos_ref, aff_ref, h_ref, xs_ref, g_ref, *, cap, w):
    b = pl.program_id(0)
    step = pl.program_id(1)

    @pl.when(step == 0)
    def _init():
        xs_ref[...] = jnp.zeros_like(xs_ref)
        g_ref[...] = jnp.zeros_like(g_ref)

    r = lax.broadcasted_iota(jnp.int32, (w, 1), 0).astype(F32)
    n_sub = pos_ref.shape[1] // TOK_TILE
    for sub in range(n_sub):
        cols = slice(sub * TOK_TILE, (sub + 1) * TOK_TILE)
        pos = pos_ref[:, cols]
        aff = aff_ref[:, cols]
        h = h_ref[cols, :]
        w0, n_pass = _window_plan(off_ref, b, step * n_sub + sub, cap, w)

        def one_pass(p, carry, pos=pos, aff=aff, h=h, w0=w0):
            starts, hots, gates = [], [], []
            for e in range(N_EXPERTS):
                low, start = _window(w0[e], p, cap, w)
                starts.append(start)
                mine = jnp.where(r >= (low - start).astype(F32), 1.0, 0.0)
                hot = jnp.where(pos[e:e + 1, :] - start.astype(F32) == r, mine, 0.0)
                hots.append(hot)
                gates.append(jnp.sum(hot * aff[e:e + 1, :], axis=1, keepdims=True))
            rows = _dot(jnp.concatenate(hots, axis=0).astype(BF16), h).astype(BF16)
            for e in range(N_EXPERTS):
                win = pl.ds(starts[e], w)
                xs_ref[e, win, :] = xs_ref[e, win, :] + rows[e * w:(e + 1) * w, :]
                g_ref[e, win, :] = g_ref[e, win, :] + gates[e]
            return carry

        one_pass(0, 0)
        lax.fori_loop(1, n_pass, one_pass, 0)


def _gather_win(offs, pos_t, aff_t, h2, *, n_req, n, cap):
    tb = WIN_BLOCKS * TOK_TILE
    nb = n // tb
    grid_spec = pltpu.PrefetchScalarGridSpec(
        num_scalar_prefetch=1,
        grid=(n_req, nb),
        in_specs=[pl.BlockSpec((N_EXPERTS, tb), lambda b, j, off: (b, j)),
                  pl.BlockSpec((N_EXPERTS, tb), lambda b, j, off: (b, j)),
                  pl.BlockSpec((tb, D_MODEL), lambda b, j, off: (b * nb + j, 0))],
        out_specs=[pl.BlockSpec((N_EXPERTS, cap, D_MODEL), lambda b, j, off: (0, b, 0)),
                   pl.BlockSpec((N_EXPERTS, cap, 1), lambda b, j, off: (0, b, 0))])
    return pl.pallas_call(
        functools.partial(_gather_win_kernel, cap=cap, w=SLOT_WIN),
        grid_spec=grid_spec,
        out_shape=[jax.ShapeDtypeStruct((N_EXPERTS, n_req * cap, D_MODEL), BF16),
                   jax.ShapeDtypeStruct((N_EXPERTS, n_req * cap, 1), F32)],
        compiler_params=_cparams(("parallel", "arbitrary")),
        name="gather_win",
    )(offs, pos_t, aff_t, h2)


def _ffn_kernel(xc_ref, xl_ref, gc_ref, gl_ref, wg_ref, wu_ref, wd_ref, yc_ref, yl_ref, wgb, wub, wdb):
    wgb[...] = wg_ref[0].astype(BF16)
    wub[...] = wu_ref[0].astype(BF16)
    wdb[...] = wd_ref[0].astype(BF16)
    for x_ref, g_ref, y_ref in ((xc_ref, gc_ref, yc_ref), (xl_ref, gl_ref, yl_ref)):
        for j in range(x_ref.shape[1] // TOK_TILE):
            rows = slice(j * TOK_TILE, (j + 1) * TOK_TILE)
            x = x_ref[0, rows, :]
            gate = _dot(x, wgb[...]).astype(BF16)
            up = _dot(x, wub[...]).astype(BF16)
            hid = gate * jax.nn.sigmoid(gate) * up
            ys = _dot(hid, wdb[...]) * g_ref[0, rows, :]
            y_ref[0, rows, :] = ys.astype(BF16)


def _ffn(xc, xl, gc, gl, wg, wu, wd):
    m = xc.shape[1]
    xspec = pl.BlockSpec((1, m, D_MODEL), lambda e: (e, 0, 0))
    gspec = pl.BlockSpec((1, m, 1), lambda e: (e, 0, 0))
    shp = jax.ShapeDtypeStruct((N_EXPERTS, m, D_MODEL), BF16)
    return pl.pallas_call(
        _ffn_kernel,
        grid=(N_EXPERTS,),
        in_specs=[xspec, xspec, gspec, gspec,
                  pl.BlockSpec((1, D_MODEL, D_EXPERT), lambda e: (e, 0, 0)),
                  pl.BlockSpec((1, D_MODEL, D_EXPERT), lambda e: (e, 0, 0)),
                  pl.BlockSpec((1, D_EXPERT, D_MODEL), lambda e: (e, 0, 0))],
        out_specs=[xspec, xspec],
        out_shape=[shp, shp],
        scratch_shapes=[pltpu.VMEM((D_MODEL, D_EXPERT), BF16), pltpu.VMEM((D_MODEL, D_EXPERT), BF16),
                        pltpu.VMEM((D_EXPERT, D_MODEL), BF16)],
        compiler_params=_cparams(("arbitrary",)),
        name="ffn",
    )(xc, xl, gc, gl, wg, wu, wd)


def _combine_kernel(x1_ref, pos_ref, ys_ref, m_ref, fg_ref, o_ref, *, cap):
    rps, n, _ = pos_ref.shape
    w = N_EXPERTS * cap
    e_i = lax.broadcasted_iota(jnp.int32, (N_EXPERTS, w), 0)
    j_i = lax.broadcasted_iota(jnp.int32, (N_EXPERTS, w), 1)
    spread = jnp.where((j_i >> (cap.bit_length() - 1)) == e_i, 1.0, 0.0).astype(BF16)
    lane_slot = (lax.broadcasted_iota(jnp.int32, (1, w), 1) & (cap - 1)).astype(F32)
    gate2 = m_ref[0, 5:6, :]
    for r in range(rps):
        pos = pos_ref[r].astype(BF16)
        onehot = jnp.where(_dot(pos, spread) == lane_slot, 1.0, 0.0).astype(BF16)
        acc = _dot(onehot, ys_ref[:, r * cap:(r + 1) * cap, :].reshape(w, D_MODEL))
        rows = slice(r * n, (r + 1) * n)
        o_ref[rows, :] = _rms(x1_ref[rows, :] + gate2 * acc, fg_ref[...])


def _combine(x1, pos_tok, ys, mods6, fg, *, n_req, n, cap, mod_row, rps):
    return pl.pallas_call(
        functools.partial(_combine_kernel, cap=cap),
        grid=(n_req // rps,),
        in_specs=[pl.BlockSpec((rps * n, D_MODEL), lambda b: (b, 0)),
                  pl.BlockSpec((rps, n, N_EXPERTS), lambda b: (b, 0, 0)),
                  pl.BlockSpec((N_EXPERTS, rps * cap, D_MODEL), lambda b: (0, b, 0)),
                  pl.BlockSpec((1, 6, D_MODEL), lambda b: (mod_row, 0, 0)),
                  pl.BlockSpec((1, D_MODEL), lambda b: (0, 0))],
        out_specs=pl.BlockSpec((rps * n, D_MODEL), lambda b: (b, 0)),
        out_shape=jax.ShapeDtypeStruct((n_req * n, D_MODEL), F32),
        compiler_params=_cparams(("parallel",)),
        name="combine",
    )(x1, pos_tok, ys, mods6, fg)


def _combine_win_kernel(off_ref, x1_ref, pos_ref, ys_ref, m_ref, fg_ref, o_ref, acc_ref, *, cap, w):
    b = pl.program_id(0)
    step = pl.program_id(1)
    width = N_EXPERTS * w
    e_i = lax.broadcasted_iota(jnp.int32, (N_EXPERTS, width), 0)
    j_i = lax.broadcasted_iota(jnp.int32, (N_EXPERTS, width), 1)
    spread = jnp.where((j_i >> (w.bit_length() - 1)) == e_i, 1.0, 0.0).astype(BF16)
    lane_slot = (lax.broadcasted_iota(jnp.int32, (1, width), 1) & (w - 1)).astype(F32)
    lane_e = lax.broadcasted_iota(jnp.int32, (1, N_EXPERTS), 1)
    n_sub = pos_ref.shape[0] // TOK_TILE
    for sub in range(n_sub):
        rows = slice(sub * TOK_TILE, (sub + 1) * TOK_TILE)
        pos = pos_ref[rows, :]
        w0, n_pass = _window_plan(off_ref, b, step * n_sub + sub, cap, w)

        def window_sum(p, pos=pos, w0=w0):
            start_row = jnp.zeros((1, N_EXPERTS), F32)
            first_row = jnp.zeros((1, N_EXPERTS), F32)
            wins = []
            for e in range(N_EXPERTS):
                low, start = _window(w0[e], p, cap, w)
                start_row = jnp.where(lane_e == e, start.astype(F32), start_row)
                first_row = jnp.where(lane_e == e, (low - start).astype(F32), first_row)
                wins.append(ys_ref[e, pl.ds(start, w), :])
            rel = pos - start_row
            rel = jnp.where(rel >= first_row, rel, -1.0).astype(BF16)
            onehot = jnp.where(_dot(rel, spread) == lane_slot, 1.0, 0.0).astype(BF16)
            return _dot(onehot, jnp.concatenate(wins, axis=0))

        def finish(acc, rows=rows):
            o_ref[rows, :] = _rms(x1_ref[rows, :] + m_ref[0, 5:6, :] * acc, fg_ref[...])

        acc0 = window_sum(0)
        acc_ref[sub] = acc0
        finish(acc0)

        @pl.when(n_pass > 1)
        def _more_passes(sub=sub, n_pass=n_pass, window_sum=window_sum, finish=finish):
            def one_pass(p, carry):
                acc_ref[sub] += window_sum(p)
                return carry

            lax.fori_loop(1, n_pass, one_pass, 0)
            finish(acc_ref[sub])


def _combine_win(offs, x1, pos_tok, ys, mods6, fg, *, n_req, n, cap, mod_row):
    tr = WIN_BLOCKS * TOK_TILE
    nr = n // tr
    grid_spec = pltpu.PrefetchScalarGridSpec(
        num_scalar_prefetch=1,
        grid=(n_req, nr),
        in_specs=[pl.BlockSpec((tr, D_MODEL), lambda b, i, off: (b * nr + i, 0)),
                  pl.BlockSpec((None, tr, N_EXPERTS), lambda b, i, off: (b, i, 0)),
                  pl.BlockSpec((N_EXPERTS, cap, D_MODEL), lambda b, i, off: (0, b, 0)),
                  pl.BlockSpec((1, 6, D_MODEL), lambda b, i, off: (mod_row(b), 0, 0)),
                  pl.BlockSpec((1, D_MODEL), lambda b, i, off: (0, 0))],
        out_specs=pl.BlockSpec((tr, D_MODEL), lambda b, i, off: (b * nr + i, 0)),
        scratch_shapes=[pltpu.VMEM((WIN_BLOCKS, TOK_TILE, D_MODEL), F32)])
    return pl.pallas_call(
        functools.partial(_combine_win_kernel, cap=cap, w=SLOT_WIN),
        grid_spec=grid_spec,
        out_shape=jax.ShapeDtypeStruct((n_req * n, D_MODEL), F32),
        compiler_params=_cparams(("parallel", "parallel")),
        name="combine_win",
    )(offs, x1, pos_tok, ys, mods6, fg)


def _rot_half(w):
    half = QK_ROPE // 2
    return jnp.concatenate([-w[..., half:], w[..., :half]], axis=-1)


def _rope_tables(t):
    n_rows = t // GRID_W
    rows = np.repeat(np.arange(n_rows, dtype=np.float64), GRID_W)
    cols = np.tile(np.arange(GRID_W, dtype=np.float64), n_rows)
    n_freq = QK_ROPE // 4
    inv_freq = ROPE_BASE ** (-np.arange(n_freq, dtype=np.float64) / n_freq)
    ang = np.concatenate([rows[:, None] * inv_freq, cols[:, None] * inv_freq], axis=-1)
    cos = np.concatenate([np.cos(ang), np.cos(ang)], axis=-1)
    sin = np.concatenate([np.sin(ang), np.sin(ang)], axis=-1)
    return cos, sin


def _qk_tables(cos, sin):
    t = cos.shape[0]
    scale = (QK_NOPE + QK_ROPE) ** -0.5 * np.log2(np.e)
    pad = np.zeros((t, HEAD_PAD - QK_NOPE - QK_ROPE))
    cosq = np.concatenate([np.full((t, QK_NOPE), scale), cos * scale, pad], axis=1)
    sinq = np.concatenate([np.zeros((t, QK_NOPE)), sin * scale, pad], axis=1)
    tq_t = np.concatenate([cosq, sinq], axis=1).T
    tk = np.concatenate([cos, sin, np.zeros((t, LANES - 2 * QK_ROPE))], axis=1)
    return jnp.asarray(tq_t, F32), jnp.asarray(tk, F32)


def _dft_angles(rows, t):
    k = np.arange(t, dtype=np.int64)
    return ((rows[:, None] * k[None, :]) % t).astype(np.float64) * (2.0 * np.pi / t)


def _dft_tables(t):
    ang = _dft_angles(np.arange(t, dtype=np.int64), t)
    scale = (t * FNET_CH) ** -0.5
    return jnp.asarray(np.cos(ang) * scale, F32).astype(BF16), jnp.asarray(np.sin(ang) * scale, F32).astype(BF16)


def _dft_half_tables(t):
    r = np.arange(MIX_TILE, dtype=np.int64)
    ang = _dft_angles(np.concatenate([r[0::2], r[1::2]]), t)[:, :t // 2]
    scale = (t * FNET_CH) ** -0.5
    ang_off = _dft_angles(np.arange(t // MIX_TILE, dtype=np.int64) * MIX_TILE, t)[:, :t // 2]
    off = np.concatenate([np.cos(ang_off), np.sin(ang_off)], axis=1)
    return jnp.asarray(np.cos(ang) * scale, F32), jnp.asarray(np.sin(ang) * scale, F32), jnp.asarray(off, F32)


def _block_diag(w):
    g, a, b = w.shape
    eye = jnp.eye(g, dtype=w.dtype)
    return (eye[:, None, :, None] * w[:, :, None, :]).reshape(g * a, g * b)


def kernel(x_prompt, x_sample, cache_ckv, cache_kpe, c, c_ctx, w_mod, b_mod, norm1_g, w_in, q_norm_g, w_uq,
           kv_norm_g, w_ukv, w_fmix, w_out, norm2_g, w_router, w_e_gate, w_e_up, w_e_down, final_g):
    assert w_mod.shape[0] == 1, "single-layer problem"
    n_ctx, t_ctx, _ = x_prompt.shape
    n_lat, t_lat, _ = x_sample.shape
    past = cache_ckv.shape[2]
    ctx_row = n_lat

    w_in0 = w_in[0]
    kpe_cols = w_in0[:, Q_LORA + KV_LORA:Q_LORA + KV_LORA + QK_ROPE]
    win = jnp.concatenate([w_in0[:, :Q_LORA + KV_LORA + QK_ROPE], _rot_half(kpe_cols),
                           jnp.zeros((D_MODEL, 512 - Q_LORA - KV_LORA - 2 * QK_ROPE), F32),
                           w_in0[:, Q_LORA + KV_LORA + QK_ROPE:]], axis=1).astype(BF16)
    wq3 = w_uq[0].reshape(Q_LORA, N_HEADS, QK_NOPE + QK_ROPE)
    qpad = jnp.zeros((Q_LORA, N_HEADS, HEAD_PAD - QK_NOPE - QK_ROPE), F32)
    wuq_main = jnp.concatenate([wq3, qpad], axis=2).reshape(Q_LORA, QK_W)
    wuq_rot = _rot_half(wq3[..., QK_NOPE:]).reshape(Q_LORA, N_HEADS * QK_ROPE)
    wuq_lat = jnp.concatenate([wuq_main, wuq_rot], axis=1).T.astype(BF16)
    wuq_ctx = wuq_main.T.astype(BF16)
    wkv3 = w_ukv[0].reshape(KV_LORA, N_HEADS, QK_NOPE + V_HEAD)
    wk_top = jnp.concatenate([wkv3[..., :QK_NOPE], jnp.zeros((KV_LORA, N_HEADS, HEAD_PAD - QK_NOPE), F32)],
                             axis=2).reshape(KV_LORA, QK_W)
    place = jnp.concatenate([jnp.zeros((QK_ROPE, QK_NOPE), F32), jnp.eye(QK_ROPE, dtype=F32),
                             jnp.zeros((QK_ROPE, HEAD_PAD - QK_NOPE - QK_ROPE), F32)], axis=1)
    place = jnp.tile(place, (1, N_HEADS))
    wk = jnp.concatenate([wk_top, place, place, jnp.zeros((LANES - 2 * QK_ROPE, QK_W), F32)], axis=0).astype(BF16)
    wv = wkv3[..., QK_NOPE:].reshape(KV_LORA, V_W).T.astype(BF16)
    wo = w_out[0].astype(BF16)
    wr_t = w_router[0].T.astype(BF16)

    cos, sin = _rope_tables(t_lat)
    tq_lat, tk_lat = _qk_tables(cos, sin)
    assert n_ctx % CTX_REQS == 0
    tq_ctx, tk_ctx = _qk_tables(np.ones((CTX_REQS * t_ctx, QK_ROPE)), np.zeros((CTX_REQS * t_ctx, QK_ROPE)))
    ch_ang = _dft_angles(np.arange(FNET_CH, dtype=np.int64), FNET_CH)
    dft_ctx = _dft_tables(t_ctx)
    dft_lat = _dft_half_tables(t_lat)

    assert n_lat + 1 <= SUBLANES
    c8 = jnp.concatenate([c, c_ctx[None, :], jnp.zeros((SUBLANES - n_lat - 1, D_MODEL), F32)], axis=0)
    mods6 = _mods(c8, w_mod[0], b_mod[0][None, :]).reshape(SUBLANES, 6, D_MODEL)
    cw, sw = _fold(jnp.asarray(np.cos(ch_ang), F32), jnp.asarray(np.sin(ch_ang), F32), w_fmix[0])
    gh = FNET_GROUPS // 2
    wcs = jnp.concatenate([jnp.concatenate([_block_diag(cw[a * gh:(a + 1) * gh]), _block_diag(sw[a * gh:(a + 1) * gh])],
                                           axis=1) for a in range(2)], axis=0).astype(BF16)

    g1 = norm1_g[0][None, :]
    qg = q_norm_g[0][None, :]
    kvg = kv_norm_g[0][None, :]
    g2 = norm2_g[0][None, :]
    fg = final_g[None, :]

    xp = x_prompt.reshape(n_ctx * t_ctx, D_MODEL)
    xs = x_sample.reshape(n_lat * t_lat, D_MODEL)
    tiles_lat = t_lat // MIX_TILE

    x1c, h2c, affc, ckv_c, kpe_c = _ctx_front(
        xp, mods6, g1, win, qg, wuq_ctx, kvg, wk, wv, wcs, tq_ctx, tk_ctx,
        *dft_ctx, wo, g2, wr_t, n_req=n_ctx, t=t_ctx, mod_row=ctx_row)
    ql, kl, vl, zcl, zsl = _premix(
        xs, mods6, g1, win, qg, wuq_lat, kvg, wk, wv, wcs, tq_lat, tk_lat,
        mod_row=lambda i: i // tiles_lat, tab_row=lambda i: i % tiles_lat, rope=True)
    xk_cache = jnp.concatenate([cache_ckv[:, 0], cache_kpe[:, 0],
                                jnp.zeros((n_lat, past, 2 * LANES - KV_LORA - QK_ROPE), F32)],
                               axis=-1).reshape(n_lat * past, 2 * LANES).astype(BF16)
    kpast, vpast = _cachekv(xk_cache, wk, wv)

    attn_l = _attention(ql, [kpast, kl], [vpast, vl], n_req=n_lat, t_q=t_lat, kv_lens=[past, t_lat], tq=TOK_TILE,
                        pairs_per_step=2)

    x1l, h2l, affl = _mixout(xs, attn_l, zcl, zsl, *dft_lat, wo, mods6, g2, wr_t,
                             n_req=n_lat, t=t_lat, mod_row=lambda b: b)

    cap_c = CAP_FACTOR * t_ctx // N_EXPERTS
    cap_l = CAP_FACTOR * t_lat // N_EXPERTS
    affc2 = affc.reshape(n_ctx * N_EXPERTS, t_ctx)
    affl2 = affl.reshape(n_lat * N_EXPERTS, t_lat)
    posc, _ = _route(affc2, cap_c)
    posl, offl = _route(affl2, cap_l)
    assert t_lat // TOK_TILE + 1 <= OFF_STRIDE and cap_l % SLOT_WIN == 0
    offl = offl[:, :OFF_STRIDE].reshape(-1)
    xsc, gc = _gather(posc, affc2, h2c, n_req=n_ctx, n=t_ctx, cap=cap_c, rps=MOE_REQS)
    xsl, gl = _gather_win(offl, posl, affl2, h2l, n_req=n_lat, n=t_lat, cap=cap_l)
    ysc, ysl = _ffn(xsc, xsl, gc, gl, w_e_gate[0], w_e_up[0], w_e_down[0])

    posc_tok = posc.reshape(n_ctx, N_EXPERTS, t_ctx).transpose(0, 2, 1)
    posl_tok = posl.reshape(n_lat, N_EXPERTS, t_lat).transpose(0, 2, 1)
    y_prompt = _combine(x1c, posc_tok, ysc, mods6, fg, n_req=n_ctx, n=t_ctx, cap=cap_c, mod_row=ctx_row,
                        rps=MOE_REQS)
    y_sample = _combine_win(offl, x1l, posl_tok, ysl, mods6, fg, n_req=n_lat, n=t_lat, cap=cap_l,
                            mod_row=lambda b: b)

    return (y_prompt.reshape(n_ctx, t_ctx, D_MODEL), y_sample.reshape(n_lat, t_lat, D_MODEL),
            ckv_c.reshape(n_ctx, 1, t_ctx, KV_LORA), kpe_c.transpose(0, 2, 1).reshape(n_ctx, 1, t_ctx, QK_ROPE))
```

```python
import functools

import jax
import jax.numpy as jnp
import numpy as np
from jax import lax
from jax.experimental import pallas as pl
from jax.experimental.pallas import tpu as pltpu

F32 = jnp.float32
BF16 = jnp.bfloat16

D_MODEL = 1024
N_HEADS = 8
QK_NOPE = 64
QK_ROPE = 32
V_HEAD = 64
Q_LORA = 256
KV_LORA = 128
FNET_GROUPS = 8
FNET_CH = 64
FNET_W = FNET_GROUPS * FNET_CH
N_EXPERTS = 16
CAP_FACTOR = 2
D_EXPERT = 512
GRID_W = 64
ROPE_BASE = 10000.0
EPS = 1e-6

LANES = 128
SUBLANES = 8
HEAD_PAD = LANES
QK_W = N_HEADS * HEAD_PAD
V_W = N_HEADS * V_HEAD
TOK_TILE = 256
MIX_TILE = 512
MODS_TILE = 1536
CTX_REQS = 4
MOE_REQS = 4
KEY_CHUNK = 512
SLOT_WIN = 64
WIN_BLOCKS = 4
OFF_STRIDE = 16
VMEM_LIMIT = 48 * 1024 * 1024

_NT = (((1,), (1,)), ((), ()))


def _cparams(sem):
    return pltpu.CompilerParams(dimension_semantics=sem, vmem_limit_bytes=VMEM_LIMIT)


def _rms(x, g):
    return x * lax.rsqrt(jnp.mean(x * x, axis=-1, keepdims=True) + EPS) * g


def _dot(a, b):
    return jnp.dot(a, b, preferred_element_type=F32)


def _mods_kernel(c_ref, w_ref, b_ref, o_ref):
    c = c_ref[...]
    s = c * jax.nn.sigmoid(c)
    o_ref[...] = _dot(s.astype(BF16), w_ref[...].astype(BF16)) + b_ref[...]


def _mods(c8, w_mod, b_mod):
    n = w_mod.shape[1]
    tn = MODS_TILE
    return pl.pallas_call(
        _mods_kernel,
        grid=(n // tn,),
        in_specs=[pl.BlockSpec((SUBLANES, D_MODEL), lambda j: (0, 0)),
                  pl.BlockSpec((D_MODEL, tn), lambda j: (0, j)),
                  pl.BlockSpec((1, tn), lambda j: (0, j))],
        out_specs=pl.BlockSpec((SUBLANES, tn), lambda j: (0, j)),
        out_shape=jax.ShapeDtypeStruct((SUBLANES, n), F32),
        compiler_params=_cparams(("arbitrary",)),
        name="mods",
    )(c8, w_mod, b_mod)


def _fold_kernel(cc_ref, sc_ref, w_ref, cw_ref, sw_ref):
    for g in range(FNET_GROUPS):
        w = w_ref[g]
        cw_ref[g] = jnp.dot(cc_ref[...], w, preferred_element_type=F32, precision=lax.Precision.HIGHEST)
        sw_ref[g] = jnp.dot(sc_ref[...], w, preferred_element_type=F32, precision=lax.Precision.HIGHEST)


def _fold(cc, sc, w_fmix):
    shp = jax.ShapeDtypeStruct((FNET_GROUPS, FNET_CH, FNET_CH), F32)
    return pl.pallas_call(_fold_kernel, out_shape=(shp, shp), name="fold")(cc, sc, w_fmix)


def _premix_body(x, m_ref, g1_ref, win_ref, qg_ref, wuq_ref, kvg_ref, wk_ref, wv_ref, wcs_ref, tq_ref, tk_ref, rope):
    shift1 = m_ref[0, 0:1, :]
    scale1 = m_ref[0, 1:2, :]
    h = _rms(x, g1_ref[...]) * (1.0 + scale1) + shift1
    proj = _dot(h.astype(BF16), win_ref[...])
    qn = _rms(proj[:, 0:Q_LORA], qg_ref[...]).astype(BF16)
    qq = lax.dot_general(wuq_ref[...], qn, _NT, preferred_element_type=F32)
    cosq = tq_ref[0:LANES, :]
    sinq = tq_ref[LANES:2 * LANES, :]
    q_heads = []
    for hd in range(N_HEADS):
        lo = hd * HEAD_PAD
        qh = qq[lo:lo + HEAD_PAD, :] * cosq
        if rope:
            rot = qq[QK_W + hd * QK_ROPE:QK_W + (hd + 1) * QK_ROPE, :] * sinq[QK_NOPE:QK_NOPE + QK_ROPE, :]
            qh = qh + jnp.concatenate([jnp.zeros((QK_NOPE, rot.shape[1]), F32), rot,
                                       jnp.zeros((HEAD_PAD - QK_NOPE - QK_ROPE, rot.shape[1]), F32)], axis=0)
        q_heads.append(qh.astype(BF16))
    ckv = _rms(proj[:, Q_LORA:Q_LORA + KV_LORA], kvg_ref[...])
    kpe2 = proj[:, Q_LORA + KV_LORA:Q_LORA + KV_LORA + LANES] * tk_ref[...]
    xk = jnp.concatenate([ckv, kpe2], axis=1).astype(BF16)
    k = _dot(xk, wk_ref[...]).astype(BF16)
    v_t = lax.dot_general(wv_ref[...], xk[:, 0:KV_LORA], _NT, preferred_element_type=F32).astype(BF16)
    hw = FNET_W // 2
    f_in = proj[:, 512:1024].astype(BF16)
    z = [_dot(f_in[:, a * hw:(a + 1) * hw], wcs_ref[a * hw:(a + 1) * hw, :]) for a in range(2)]
    zc = jnp.concatenate([za[:, 0:hw] for za in z], axis=1).astype(BF16)
    zs = jnp.concatenate([za[:, hw:2 * hw] for za in z], axis=1).astype(BF16)
    kpe = proj[:, Q_LORA + KV_LORA:Q_LORA + KV_LORA + LANES]
    return q_heads, k, v_t, zc, zs, ckv, kpe


def _premix_kernel(x_ref, m_ref, g1_ref, win_ref, qg_ref, wuq_ref, kvg_ref, wk_ref, wv_ref, wcs_ref,
                   tq_ref, tk_ref, q_ref, k_ref, v_ref, zc_ref, zs_ref, *, rope):
    q_heads, k, v_t, zc, zs, _, _ = _premix_body(x_ref[...], m_ref, g1_ref, win_ref, qg_ref, wuq_ref, kvg_ref,
                                                 wk_ref, wv_ref, wcs_ref, tq_ref, tk_ref, rope)
    for hd, qh in enumerate(q_heads):
        q_ref[hd * HEAD_PAD:(hd + 1) * HEAD_PAD, :] = qh
    k_ref[...] = k
    v_ref[...] = v_t
    zc_ref[...] = zc
    zs_ref[...] = zs


def _premix(x, mods6, g1, win, qg, wuq, kvg, wk, wv, wcs, tq, tk, *, mod_row, tab_row, rope):
    n = x.shape[0]
    tm = MIX_TILE
    full = lambda a: pl.BlockSpec(a.shape, lambda i: (0,) * a.ndim)
    out_shape = [jax.ShapeDtypeStruct((QK_W, n), BF16), jax.ShapeDtypeStruct((n, QK_W), BF16),
                 jax.ShapeDtypeStruct((V_W, n), BF16), jax.ShapeDtypeStruct((n, FNET_W), BF16),
                 jax.ShapeDtypeStruct((n, FNET_W), BF16)]
    out_specs = [pl.BlockSpec((QK_W, tm), lambda i: (0, i)), pl.BlockSpec((tm, QK_W), lambda i: (i, 0)),
                 pl.BlockSpec((V_W, tm), lambda i: (0, i)), pl.BlockSpec((tm, FNET_W), lambda i: (i, 0)),
                 pl.BlockSpec((tm, FNET_W), lambda i: (i, 0))]
    return pl.pallas_call(
        functools.partial(_premix_kernel, rope=rope),
        grid=(n // tm,),
        in_specs=[pl.BlockSpec((tm, D_MODEL), lambda i: (i, 0)),
                  pl.BlockSpec((1, 6, D_MODEL), lambda i: (mod_row(i), 0, 0)),
                  full(g1), full(win), full(qg), full(wuq), full(kvg), full(wk), full(wv), full(wcs),
                  pl.BlockSpec((2 * LANES, tm), lambda i: (0, tab_row(i))),
                  pl.BlockSpec((tm, LANES), lambda i: (tab_row(i), 0))],
        out_specs=out_specs,
        out_shape=out_shape,
        compiler_params=_cparams(("parallel",)),
        name="premix",
    )(x, mods6, g1, win, qg, wuq, kvg, wk, wv, wcs, tq, tk)


def _cachekv_kernel(xk_ref, wk_ref, wv_ref, k_ref, v_ref):
    xk = xk_ref[...]
    k_ref[...] = _dot(xk, wk_ref[...]).astype(BF16)
    v_ref[...] = lax.dot_general(wv_ref[...], xk[:, 0:KV_LORA], _NT, preferred_element_type=F32).astype(BF16)


def _cachekv(xk, wk, wv):
    n = xk.shape[0]
    tm = 512
    full = lambda a: pl.BlockSpec(a.shape, lambda i: (0,) * a.ndim)
    return pl.pallas_call(
        _cachekv_kernel,
        grid=(n // tm,),
        in_specs=[pl.BlockSpec((tm, 2 * LANES), lambda i: (i, 0)), full(wk), full(wv)],
        out_specs=[pl.BlockSpec((tm, QK_W), lambda i: (i, 0)), pl.BlockSpec((V_W, tm), lambda i: (0, i))],
        out_shape=[jax.ShapeDtypeStruct((n, QK_W), BF16), jax.ShapeDtypeStruct((V_W, n), BF16)],
        compiler_params=_cparams(("parallel",)),
        name="cachekv",
    )(xk, wk, wv)


def _attn_body(q_heads, k_refs, v_refs, kc):
    tq = q_heads[0].shape[1]
    zero = jnp.zeros((HEAD_PAD, tq), BF16)
    n_pairs = len(q_heads) // 2
    qbd = [jnp.concatenate([jnp.concatenate([q_heads[2 * pr], zero], axis=1),
                            jnp.concatenate([zero, q_heads[2 * pr + 1]], axis=1)], axis=0) for pr in range(n_pairs)]
    chunks = [(k_ref, v_ref, c0, min(c0 + kc, k_ref.shape[0]))
              for k_ref, v_ref in zip(k_refs, v_refs) for c0 in range(0, k_ref.shape[0], kc)]
    work = [(pr, ch) for pr in range(n_pairs) for ch in chunks]

    def score(item):
        pr, (k_ref, _, c0, c1) = item
        return _dot(k_ref[c0:c1, pr * 2 * HEAD_PAD:(pr + 1) * 2 * HEAD_PAD], qbd[pr]).astype(BF16)

    m = [None] * n_pairs
    o = [None] * n_pairs
    s_next = score(work[0])
    for wi, (pr, (k_ref, v_ref, c0, c1)) in enumerate(work):
        s = s_next
        if wi + 1 < len(work):
            s_next = score(work[wi + 1])
        cm = jnp.max(s, axis=0, keepdims=True)
        vlo = pr * 2 * V_HEAD
        va = jnp.concatenate([v_ref[vlo:vlo + 2 * V_HEAD, c0:c1], jnp.ones((16, c1 - c0), BF16)], axis=0)
        if m[pr] is None:
            m[pr] = cm
            o[pr] = _dot(va, jnp.exp2(s - cm))
        else:
            m_new = jnp.maximum(m[pr], cm)
            alpha = jnp.exp2(m[pr].astype(F32) - m_new.astype(F32))
            o[pr] = alpha * o[pr] + _dot(va, jnp.exp2(s - m_new))
            m[pr] = m_new
    outs = []
    for pr in range(n_pairs):
        on = o[pr][0:2 * V_HEAD, :] * (1.0 / o[pr][2 * V_HEAD:2 * V_HEAD + 1, :])
        ot = jnp.concatenate([on[0:V_HEAD, 0:tq], on[V_HEAD:2 * V_HEAD, tq:2 * tq]], axis=0)
        outs.append(ot.T.astype(BF16))
    return outs


def _attn_kernel(q_ref, *refs, n_kv, n_pairs, kc):
    q_heads = [q_ref[hd * HEAD_PAD:(hd + 1) * HEAD_PAD, :] for hd in range(2 * n_pairs)]
    outs = _attn_body(q_heads, refs[:n_kv], refs[n_kv:2 * n_kv], kc)
    o_ref = refs[2 * n_kv]
    for pr, o in enumerate(outs):
        o_ref[:, pr * 2 * V_HEAD:(pr + 1) * 2 * V_HEAD] = o


def _attention(q_t, ks, vs_t, *, n_req, t_q, kv_lens, tq, pairs_per_step):
    n_kv = len(ks)
    nq = t_q // tq
    pp = pairs_per_step
    in_specs = [pl.BlockSpec((pp * 2 * HEAD_PAD, tq), lambda b, p, i: (p, b * nq + i))]
    in_specs += [pl.BlockSpec((kl, pp * 2 * HEAD_PAD), lambda b, p, i: (b, p)) for kl in kv_lens]
    in_specs += [pl.BlockSpec((pp * 2 * V_HEAD, kl), lambda b, p, i: (p, b)) for kl in kv_lens]
    return pl.pallas_call(
        functools.partial(_attn_kernel, n_kv=n_kv, n_pairs=pp, kc=KEY_CHUNK),
        grid=(n_req, N_HEADS // 2 // pp, nq),
        in_specs=in_specs,
        out_specs=pl.BlockSpec((tq, pp * 2 * V_HEAD), lambda b, p, i: (b * nq + i, p)),
        out_shape=jax.ShapeDtypeStruct((n_req * t_q, V_W), BF16),
        compiler_params=_cparams(("parallel", "parallel", "parallel")),
        name="attn",
    )(q_t, *ks, *vs_t)


def _mixout_body(x, attn_pairs, fm, wo_ref, m_ref, g2_ref, wr_ref):
    y = _dot(fm, wo_ref[V_W:V_W + FNET_W, :])
    col = 0
    for a in attn_pairs:
        y = y + _dot(a, wo_ref[col:col + a.shape[1], :])
        col += a.shape[1]
    gate1 = m_ref[0, 2:3, :]
    shift2 = m_ref[0, 3:4, :]
    scale2 = m_ref[0, 4:5, :]
    x1 = x + gate1 * y
    h2 = (_rms(x1, g2_ref[...]) * (1.0 + scale2) + shift2).astype(BF16)
    lg = lax.dot_general(wr_ref[...], h2, _NT, preferred_element_type=F32)
    e = jnp.exp(lg - jnp.max(lg, axis=0, keepdims=True))
    return x1, h2, e / jnp.sum(e, axis=0, keepdims=True)


def _mixout_kernel(x_ref, a_ref, zc_ref, zs_ref, cb_ref, sb_ref, off_ref, wo_ref, m_ref, g2_ref, wr_ref,
                   x1_ref, h2_ref, aff_ref, zp_ref, zm_ref, il_ref, *, t):
    half = t // 2
    i = pl.program_id(1)

    @pl.when(i == 0)
    def _fold_halves():
        for src, col in ((zc_ref, 0), (zs_ref, FNET_W)):
            lo = src[0:half, :]
            hi = src[half:t, :]
            zp_ref[:, col:col + FNET_W] = lo + hi
            zm_ref[:, col:col + FNET_W] = lo - hi

    off = off_ref[pl.ds(i, 1), :]
    co = off[:, 0:half]
    so = off[:, half:t]
    cb = cb_ref[...]
    sb = sb_ref[...]
    ct = (cb * co - sb * so).astype(BF16)
    st = (sb * co + cb * so).astype(BF16)
    tr = cb.shape[0]
    h = tr // 2
    even = _dot(ct[0:h, :], zp_ref[:, 0:FNET_W]) - _dot(st[0:h, :], zp_ref[:, FNET_W:2 * FNET_W])
    odd = _dot(ct[h:, :], zm_ref[:, 0:FNET_W]) - _dot(st[h:, :], zm_ref[:, FNET_W:2 * FNET_W])
    for c in range(FNET_W // LANES):
        il_ref[c, pl.ds(0, h, stride=2), :] = even[:, c * LANES:(c + 1) * LANES]
        il_ref[c, pl.ds(1, h, stride=2), :] = odd[:, c * LANES:(c + 1) * LANES]
    fm = jnp.concatenate([il_ref[c] for c in range(FNET_W // LANES)], axis=1).astype(BF16)
    x1, h2, aff = _mixout_body(x_ref[...], [a_ref[...]], fm, wo_ref, m_ref, g2_ref, wr_ref)
    x1_ref[...] = x1
    h2_ref[...] = h2
    aff_ref[...] = aff


def _ctx_front_kernel(x_ref, m_ref, g1_ref, win_ref, qg_ref, wuq_ref, kvg_ref, wk_ref, wv_ref, wcs_ref,
                      tq_ref, tk_ref, ct_ref, st_ref, wo_ref, g2_ref, wr_ref,
                      x1_ref, h2_ref, aff_ref, ckv_ref, kpe_ref, *, kc, t):
    x = x_ref[...]
    q_heads, k, v_t, zc, zs, ckv, kpe = _premix_body(x, m_ref, g1_ref, win_ref, qg_ref, wuq_ref, kvg_ref,
                                                     wk_ref, wv_ref, wcs_ref, tq_ref, tk_ref, False)
    ckv_ref[...] = ckv
    for r in range(x.shape[0] // t):
        kpe_ref[r] = kpe[r * t:(r + 1) * t, :].T[0:QK_ROPE, :]
    attn, fm = [], []
    for r in range(x.shape[0] // t):
        rows = slice(r * t, (r + 1) * t)
        attn.append(jnp.concatenate(_attn_body([q[:, rows] for q in q_heads], [k[rows, :]], [v_t[:, rows]], kc),
                                    axis=1))
        fm.append((_dot(ct_ref[...], zc[rows, :]) - _dot(st_ref[...], zs[rows, :])).astype(BF16))
    x1, h2, aff = _mixout_body(x_ref[...], [jnp.concatenate(attn, axis=0)], jnp.concatenate(fm, axis=0),
                               wo_ref, m_ref, g2_ref, wr_ref)
    x1_ref[...] = x1
    h2_ref[...] = h2
    for r in range(x.shape[0] // t):
        aff_ref[r] = aff[:, r * t:(r + 1) * t]


def _ctx_front(x, mods6, g1, win, qg, wuq, kvg, wk, wv, wcs, tq, tk, ct, st, wo, g2, wr_t, *, n_req, t, mod_row):
    full = lambda a: pl.BlockSpec(a.shape, lambda b: (0,) * a.ndim)
    rps = CTX_REQS
    row = lambda w: pl.BlockSpec((rps * t, w), lambda b: (b, 0))
    return pl.pallas_call(
        functools.partial(_ctx_front_kernel, kc=KEY_CHUNK, t=t),
        grid=(n_req // rps,),
        in_specs=[row(D_MODEL), pl.BlockSpec((1, 6, D_MODEL), lambda b: (mod_row, 0, 0)),
                  full(g1), full(win), full(qg), full(wuq), full(kvg), full(wk), full(wv), full(wcs),
                  full(tq), full(tk), full(ct), full(st), full(wo), full(g2), full(wr_t)],
        out_specs=[row(D_MODEL), row(D_MODEL), pl.BlockSpec((rps, N_EXPERTS, t), lambda b: (b, 0, 0)),
                   row(KV_LORA), pl.BlockSpec((rps, QK_ROPE, t), lambda b: (b, 0, 0))],
        out_shape=[jax.ShapeDtypeStruct((n_req * t, D_MODEL), F32),
                   jax.ShapeDtypeStruct((n_req * t, D_MODEL), BF16),
                   jax.ShapeDtypeStruct((n_req, N_EXPERTS, t), F32),
                   jax.ShapeDtypeStruct((n_req * t, KV_LORA), F32),
                   jax.ShapeDtypeStruct((n_req, QK_ROPE, t), F32)],
        compiler_params=_cparams(("parallel",)),
        name="ctx_front",
    )(x, mods6, g1, win, qg, wuq, kvg, wk, wv, wcs, tq, tk, ct, st, wo, g2, wr_t)


def _mixout(x, attn, zc, zs, cb, sb, off, wo, mods6, g2, wr_t, *, n_req, t, mod_row):
    tr = MIX_TILE
    nr = t // tr
    full = lambda a: pl.BlockSpec(a.shape, lambda b, i: (0,) * a.ndim)
    return pl.pallas_call(
        functools.partial(_mixout_kernel, t=t),
        grid=(n_req, nr),
        in_specs=[pl.BlockSpec((tr, D_MODEL), lambda b, i: (b * nr + i, 0)),
                  pl.BlockSpec((tr, V_W), lambda b, i: (b * nr + i, 0)),
                  pl.BlockSpec((t, FNET_W), lambda b, i: (b, 0)),
                  pl.BlockSpec((t, FNET_W), lambda b, i: (b, 0)),
                  full(cb), full(sb), full(off),
                  full(wo),
                  pl.BlockSpec((1, 6, D_MODEL), lambda b, i: (mod_row(b), 0, 0)),
                  full(g2), full(wr_t)],
        out_specs=[pl.BlockSpec((tr, D_MODEL), lambda b, i: (b * nr + i, 0)),
                   pl.BlockSpec((tr, D_MODEL), lambda b, i: (b * nr + i, 0)),
                   pl.BlockSpec((None, N_EXPERTS, tr), lambda b, i: (b, 0, i))],
        out_shape=[jax.ShapeDtypeStruct((n_req * t, D_MODEL), F32),
                   jax.ShapeDtypeStruct((n_req * t, D_MODEL), BF16),
                   jax.ShapeDtypeStruct((n_req, N_EXPERTS, t), F32)],
        scratch_shapes=[pltpu.VMEM((t // 2, 2 * FNET_W), BF16), pltpu.VMEM((t // 2, 2 * FNET_W), BF16),
                        pltpu.VMEM((FNET_W // LANES, tr, LANES), F32)],
        compiler_params=_cparams(("parallel", "arbitrary")),
        name="mixout",
    )(x, attn, zc, zs, cb, sb, off, wo, mods6, g2, wr_t)


def _prefix_count(flags, tri):
    n = flags.shape[1]
    carry = None
    outs = []
    ends = []
    for j in range(n // TOK_TILE):
        c = _dot(flags[:, j * TOK_TILE:(j + 1) * TOK_TILE].astype(BF16), tri)
        if carry is not None:
            c = c + carry
        outs.append(c)
        carry = c[:, TOK_TILE - 1:TOK_TILE]
        ends.append(carry)
    return (outs[0] if len(outs) == 1 else jnp.concatenate(outs, axis=1)), ends


def _route_kernel(aff_ref, pos_ref, off_ref, *, cap):
    a = aff_ref[...]
    rows = a.shape[0]
    capf = jnp.float32(cap)
    thr = jnp.zeros((rows, 1), jnp.int32)
    for bit in range(30, -1, -1):
        cand = thr | jnp.int32(1 << bit)
        cand_f = lax.bitcast_convert_type(cand, F32)
        cnt = jnp.sum(jnp.where(a >= cand_f, 1.0, 0.0), axis=1, keepdims=True)
        thr = jnp.where(cnt >= capf, cand, thr)
    thr_f = lax.bitcast_convert_type(thr, F32)
    above_f = lax.bitcast_convert_type(thr + 1, F32)
    gt = jnp.where(a >= above_f, 1.0, 0.0)
    tie = jnp.where(a >= thr_f, 1.0, 0.0) - gt
    need = capf - jnp.sum(gt, axis=1, keepdims=True)
    r_i = lax.broadcasted_iota(jnp.int32, (TOK_TILE, TOK_TILE), 0)
    c_i = lax.broadcasted_iota(jnp.int32, (TOK_TILE, TOK_TILE), 1)
    tri = jnp.where(r_i <= c_i, 1.0, 0.0).astype(BF16)
    tie_before = _prefix_count(tie, tri)[0] - tie
    sel = gt + tie * jnp.where(tie_before < need, 1.0, 0.0)
    count, ends = _prefix_count(sel, tri)
    pos_ref[...] = jnp.where(sel > 0.5, count - 1.0, -1.0)
    lane = lax.broadcasted_iota(jnp.int32, (rows, LANES), 1)
    offs = jnp.zeros((rows, LANES), F32)
    for j, end in enumerate(ends):
        offs = offs + jnp.where(lane == j + 1, end, 0.0)
    off_ref[...] = offs.astype(jnp.int32)


def _route(aff_t, cap):
    return pl.pallas_call(
        functools.partial(_route_kernel, cap=cap),
        out_shape=[jax.ShapeDtypeStruct(aff_t.shape, F32),
                   jax.ShapeDtypeStruct((aff_t.shape[0], LANES), jnp.int32)],
        compiler_params=pltpu.CompilerParams(vmem_limit_bytes=VMEM_LIMIT),
        name="route",
    )(aff_t)


def _gather_kernel(pos_ref, aff_ref, h_ref, xs_ref, g_ref, *, cap):
    rps, ne, n = pos_ref.shape
    slot = lax.broadcasted_iota(jnp.int32, (ne, cap, n), 1).astype(F32)
    for r in range(rps):
        pos = pos_ref[r]
        aff = aff_ref[r]
        hit = pos[:, None, :] == slot
        onehot = jnp.where(hit, 1.0, 0.0).reshape(ne * cap, n).astype(BF16)
        xs = _dot(onehot, h_ref[r * n:(r + 1) * n, :])
        xs_ref[:, r * cap:(r + 1) * cap, :] = xs.astype(BF16).reshape(ne, cap, D_MODEL)
        g_ref[:, r * cap:(r + 1) * cap, :] = jnp.sum(jnp.where(hit, aff[:, None, :], 0.0), axis=2, keepdims=True)


def _gather(pos_t, aff_t, h2, *, n_req, n, cap, rps):
    pos3 = pos_t.reshape(n_req, N_EXPERTS, n)
    aff3 = aff_t.reshape(n_req, N_EXPERTS, n)
    return pl.pallas_call(
        functools.partial(_gather_kernel, cap=cap),
        grid=(n_req // rps,),
        in_specs=[pl.BlockSpec((rps, N_EXPERTS, n), lambda b: (b, 0, 0)),
                  pl.BlockSpec((rps, N_EXPERTS, n), lambda b: (b, 0, 0)),
                  pl.BlockSpec((rps * n, D_MODEL), lambda b: (b, 0))],
        out_specs=[pl.BlockSpec((N_EXPERTS, rps * cap, D_MODEL), lambda b: (0, b, 0)),
                   pl.BlockSpec((N_EXPERTS, rps * cap, 1), lambda b: (0, b, 0))],
        out_shape=[jax.ShapeDtypeStruct((N_EXPERTS, n_req * cap, D_MODEL), BF16),
                   jax.ShapeDtypeStruct((N_EXPERTS, n_req * cap, 1), F32)],
        compiler_params=_cparams(("parallel",)),
        name="gather",
    )(pos3, aff3, h2)


def _window_plan(off_ref, b, j, cap, w):
    w0 = []
    need = jnp.int32(0)
    for e in range(N_EXPERTS):
        idx = (b * N_EXPERTS + e) * OFF_STRIDE + j
        base = (off_ref[idx] >> 4) << 4
        w0.append(base)
        need = jnp.maximum(need, off_ref[idx + 1] - base)
    return w0, (need + (w - 1)) >> (w.bit_length() - 1)


def _window(w0_e, p, cap, w):
    low = w0_e + p * w
    start = pl.multiple_of(jnp.minimum(low, cap - w), 16)
    return low, start


def _gather_win_kernel(off_ref, pos_ref, aff_ref, h_ref, xs_ref, g_ref, *, cap, w):
    b = pl.program_id(0)
    step = pl.program_id(1)

    @pl.when(step == 0)
    def _init():
        xs_ref[...] = jnp.zeros_like(xs_ref)
        g_ref[...] = jnp.zeros_like(g_ref)

    r = lax.broadcasted_iota(jnp.int32, (w, 1), 0).astype(F32)
    n_sub = pos_ref.shape[1] // TOK_TILE
    for sub in range(n_sub):
        cols = slice(sub * TOK_TILE, (sub + 1) * TOK_TILE)
        pos = pos_ref[:, cols]
        aff = aff_ref[:, cols]
        h = h_ref[cols, :]
        w0, n_pass = _window_plan(off_ref, b, step * n_sub + sub, cap, w)

        def one_pass(p, carry, pos=pos, aff=aff, h=h, w0=w0):
            starts, hots, gates = [], [], []
            for e in range(N_EXPERTS):
                low, start = _window(w0[e], p, cap, w)
                starts.append(start)
                mine = jnp.where(r >= (low - start).astype(F32), 1.0, 0.0)
                hot = jnp.where(pos[e:e + 1, :] - start.astype(F32) == r, mine, 0.0)
                hots.append(hot)
                gates.append(jnp.sum(hot * aff[e:e + 1, :], axis=1, keepdims=True))
            rows = _dot(jnp.concatenate(hots, axis=0).astype(BF16), h).astype(BF16)
            for e in range(N_EXPERTS):
                win = pl.ds(starts[e], w)
                xs_ref[e, win, :] = xs_ref[e, win, :] + rows[e * w:(e + 1) * w, :]
                g_ref[e, win, :] = g_ref[e, win, :] + gates[e]
            return carry

        one_pass(0, 0)
        lax.fori_loop(1, n_pass, one_pass, 0)


def _gather_win(offs, pos_t, aff_t, h2, *, n_req, n, cap):
    tb = WIN_BLOCKS * TOK_TILE
    nb = n // tb
    grid_spec = pltpu.PrefetchScalarGridSpec(
        num_scalar_prefetch=1,
        grid=(n_req, nb),
        in_specs=[pl.BlockSpec((N_EXPERTS, tb), lambda b, j, off: (b, j)),
                  pl.BlockSpec((N_EXPERTS, tb), lambda b, j, off: (b, j)),
                  pl.BlockSpec((tb, D_MODEL), lambda b, j, off: (b * nb + j, 0))],
        out_specs=[pl.BlockSpec((N_EXPERTS, cap, D_MODEL), lambda b, j, off: (0, b, 0)),
                   pl.BlockSpec((N_EXPERTS, cap, 1), lambda b, j, off: (0, b, 0))])
    return pl.pallas_call(
        functools.partial(_gather_win_kernel, cap=cap, w=SLOT_WIN),
        grid_spec=grid_spec,
        out_shape=[jax.ShapeDtypeStruct((N_EXPERTS, n_req * cap, D_MODEL), BF16),
                   jax.ShapeDtypeStruct((N_EXPERTS, n_req * cap, 1), F32)],
        compiler_params=_cparams(("parallel", "arbitrary")),
        name="gather_win",
    )(offs, pos_t, aff_t, h2)


def _ffn_kernel(xc_ref, xl_ref, gc_ref, gl_ref, wg_ref, wu_ref, wd_ref, yc_ref, yl_ref, wgb, wub, wdb):
    wgb[...] = wg_ref[0].astype(BF16)
    wub[...] = wu_ref[0].astype(BF16)
    wdb[...] = wd_ref[0].astype(BF16)
    for x_ref, g_ref, y_ref in ((xc_ref, gc_ref, yc_ref), (xl_ref, gl_ref, yl_ref)):
        for j in range(x_ref.shape[1] // TOK_TILE):
            rows = slice(j * TOK_TILE, (j + 1) * TOK_TILE)
            x = x_ref[0, rows, :]
            gate = _dot(x, wgb[...])
            up = _dot(x, wub[...])
            hid = (gate * jax.nn.sigmoid(gate) * up).astype(BF16)
            ys = _dot(hid, wdb[...]) * g_ref[0, rows, :]
            y_ref[0, rows, :] = ys.astype(BF16)


def _ffn(xc, xl, gc, gl, wg, wu, wd):
    m = xc.shape[1]
    xspec = pl.BlockSpec((1, m, D_MODEL), lambda e: (e, 0, 0))
    gspec = pl.BlockSpec((1, m, 1), lambda e: (e, 0, 0))
    shp = jax.ShapeDtypeStruct((N_EXPERTS, m, D_MODEL), BF16)
    return pl.pallas_call(
        _ffn_kernel,
        grid=(N_EXPERTS,),
        in_specs=[xspec, xspec, gspec, gspec,
                  pl.BlockSpec((1, D_MODEL, D_EXPERT), lambda e: (e, 0, 0)),
                  pl.BlockSpec((1, D_MODEL, D_EXPERT), lambda e: (e, 0, 0)),
                  pl.BlockSpec((1, D_EXPERT, D_MODEL), lambda e: (e, 0, 0))],
        out_specs=[xspec, xspec],
        out_shape=[shp, shp],
        scratch_shapes=[pltpu.VMEM((D_MODEL, D_EXPERT), BF16), pltpu.VMEM((D_MODEL, D_EXPERT), BF16),
                        pltpu.VMEM((D_EXPERT, D_MODEL), BF16)],
        compiler_params=_cparams(("arbitrary",)),
        name="ffn",
    )(xc, xl, gc, gl, wg, wu, wd)


def _combine_kernel(x1_ref, pos_ref, ys_ref, m_ref, fg_ref, o_ref, *, cap):
    rps, n, _ = pos_ref.shape
    w = N_EXPERTS * cap
    e_i = lax.broadcasted_iota(jnp.int32, (N_EXPERTS, w), 0)
    j_i = lax.broadcasted_iota(jnp.int32, (N_EXPERTS, w), 1)
    spread = jnp.where((j_i >> (cap.bit_length() - 1)) == e_i, 1.0, 0.0).astype(BF16)
    lane_slot = (lax.broadcasted_iota(jnp.int32, (1, w), 1) & (cap - 1)).astype(F32)
    gate2 = m_ref[0, 5:6, :]
    for r in range(rps):
        pos = pos_ref[r].astype(BF16)
        onehot = jnp.where(_dot(pos, spread) == lane_slot, 1.0, 0.0).astype(BF16)
        acc = _dot(onehot, ys_ref[:, r * cap:(r + 1) * cap, :].reshape(w, D_MODEL))
        rows = slice(r * n, (r + 1) * n)
        o_ref[rows, :] = _rms(x1_ref[rows, :] + gate2 * acc, fg_ref[...])


def _combine(x1, pos_tok, ys, mods6, fg, *, n_req, n, cap, mod_row, rps):
    return pl.pallas_call(
        functools.partial(_combine_kernel, cap=cap),
        grid=(n_req // rps,),
        in_specs=[pl.BlockSpec((rps * n, D_MODEL), lambda b: (b, 0)),
                  pl.BlockSpec((rps, n, N_EXPERTS), lambda b: (b, 0, 0)),
                  pl.BlockSpec((N_EXPERTS, rps * cap, D_MODEL), lambda b: (0, b, 0)),
                  pl.BlockSpec((1, 6, D_MODEL), lambda b: (mod_row, 0, 0)),
                  pl.BlockSpec((1, D_MODEL), lambda b: (0, 0))],
        out_specs=pl.BlockSpec((rps * n, D_MODEL), lambda b: (b, 0)),
        out_shape=jax.ShapeDtypeStruct((n_req * n, D_MODEL), F32),
        compiler_params=_cparams(("parallel",)),
        name="combine",
    )(x1, pos_tok, ys, mods6, fg)


def _combine_win_kernel(off_ref, x1_ref, pos_ref, ys_ref, m_ref, fg_ref, o_ref, acc_ref, *, cap, w):
    b = pl.program_id(0)
    step = pl.program_id(1)
    width = N_EXPERTS * w
    e_i = lax.broadcasted_iota(jnp.int32, (N_EXPERTS, width), 0)
    j_i = lax.broadcasted_iota(jnp.int32, (N_EXPERTS, width), 1)
    spread = jnp.where((j_i >> (w.bit_length() - 1)) == e_i, 1.0, 0.0).astype(BF16)
    lane_slot = (lax.broadcasted_iota(jnp.int32, (1, width), 1) & (w - 1)).astype(F32)
    lane_e = lax.broadcasted_iota(jnp.int32, (1, N_EXPERTS), 1)
    n_sub = pos_ref.shape[0] // TOK_TILE
    for sub in range(n_sub):
        rows = slice(sub * TOK_TILE, (sub + 1) * TOK_TILE)
        pos = pos_ref[rows, :]
        w0, n_pass = _window_plan(off_ref, b, step * n_sub + sub, cap, w)

        def window_sum(p, pos=pos, w0=w0):
            start_row = jnp.zeros((1, N_EXPERTS), F32)
            first_row = jnp.zeros((1, N_EXPERTS), F32)
            wins = []
            for e in range(N_EXPERTS):
                low, start = _window(w0[e], p, cap, w)
                start_row = jnp.where(lane_e == e, start.astype(F32), start_row)
                first_row = jnp.where(lane_e == e, (low - start).astype(F32), first_row)
                wins.append(ys_ref[e, pl.ds(start, w), :])
            rel = pos - start_row
            rel = jnp.where(rel >= first_row, rel, -1.0).astype(BF16)
            onehot = jnp.where(_dot(rel, spread) == lane_slot, 1.0, 0.0).astype(BF16)
            return _dot(onehot, jnp.concatenate(wins, axis=0))

        def finish(acc, rows=rows):
            o_ref[rows, :] = _rms(x1_ref[rows, :] + m_ref[0, 5:6, :] * acc, fg_ref[...])

        acc0 = window_sum(0)
        acc_ref[sub] = acc0
        finish(acc0)

        @pl.when(n_pass > 1)
        def _more_passes(sub=sub, n_pass=n_pass, window_sum=window_sum, finish=finish):
            def one_pass(p, carry):
                acc_ref[sub] += window_sum(p)
                return carry

            lax.fori_loop(1, n_pass, one_pass, 0)
            finish(acc_ref[sub])


def _combine_win(offs, x1, pos_tok, ys, mods6, fg, *, n_req, n, cap, mod_row):
    tr = WIN_BLOCKS * TOK_TILE
    nr = n // tr
    grid_spec = pltpu.PrefetchScalarGridSpec(
        num_scalar_prefetch=1,
        grid=(n_req, nr),
        in_specs=[pl.BlockSpec((tr, D_MODEL), lambda b, i, off: (b * nr + i, 0)),
                  pl.BlockSpec((None, tr, N_EXPERTS), lambda b, i, off: (b, i, 0)),
                  pl.BlockSpec((N_EXPERTS, cap, D_MODEL), lambda b, i, off: (0, b, 0)),
                  pl.BlockSpec((1, 6, D_MODEL), lambda b, i, off: (mod_row(b), 0, 0)),
                  pl.BlockSpec((1, D_MODEL), lambda b, i, off: (0, 0))],
        out_specs=pl.BlockSpec((tr, D_MODEL), lambda b, i, off: (b * nr + i, 0)),
        scratch_shapes=[pltpu.VMEM((WIN_BLOCKS, TOK_TILE, D_MODEL), F32)])
    return pl.pallas_call(
        functools.partial(_combine_win_kernel, cap=cap, w=SLOT_WIN),
        grid_spec=grid_spec,
        out_shape=jax.ShapeDtypeStruct((n_req * n, D_MODEL), F32),
        compiler_params=_cparams(("parallel", "parallel")),
        name="combine_win",
    )(offs, x1, pos_tok, ys, mods6, fg)


def _rot_half(w):
    half = QK_ROPE // 2
    return jnp.concatenate([-w[..., half:], w[..., :half]], axis=-1)


def _rope_tables(t):
    n_rows = t // GRID_W
    rows = np.repeat(np.arange(n_rows, dtype=np.float64), GRID_W)
    cols = np.tile(np.arange(GRID_W, dtype=np.float64), n_rows)
    n_freq = QK_ROPE // 4
    inv_freq = ROPE_BASE ** (-np.arange(n_freq, dtype=np.float64) / n_freq)
    ang = np.concatenate([rows[:, None] * inv_freq, cols[:, None] * inv_freq], axis=-1)
    cos = np.concatenate([np.cos(ang), np.cos(ang)], axis=-1)
    sin = np.concatenate([np.sin(ang), np.sin(ang)], axis=-1)
    return cos, sin


def _qk_tables(cos, sin):
    t = cos.shape[0]
    scale = (QK_NOPE + QK_ROPE) ** -0.5 * np.log2(np.e)
    pad = np.zeros((t, HEAD_PAD - QK_NOPE - QK_ROPE))
    cosq = np.concatenate([np.full((t, QK_NOPE), scale), cos * scale, pad], axis=1)
    sinq = np.concatenate([np.zeros((t, QK_NOPE)), sin * scale, pad], axis=1)
    tq_t = np.concatenate([cosq, sinq], axis=1).T
    tk = np.concatenate([cos, sin, np.zeros((t, LANES - 2 * QK_ROPE))], axis=1)
    return jnp.asarray(tq_t, F32), jnp.asarray(tk, F32)


def _dft_angles(rows, t):
    k = np.arange(t, dtype=np.int64)
    return ((rows[:, None] * k[None, :]) % t).astype(np.float64) * (2.0 * np.pi / t)


def _dft_tables(t):
    ang = _dft_angles(np.arange(t, dtype=np.int64), t)
    scale = (t * FNET_CH) ** -0.5
    return jnp.asarray(np.cos(ang) * scale, F32).astype(BF16), jnp.asarray(np.sin(ang) * scale, F32).astype(BF16)


def _dft_half_tables(t):
    r = np.arange(MIX_TILE, dtype=np.int64)
    ang = _dft_angles(np.concatenate([r[0::2], r[1::2]]), t)[:, :t // 2]
    scale = (t * FNET_CH) ** -0.5
    ang_off = _dft_angles(np.arange(t // MIX_TILE, dtype=np.int64) * MIX_TILE, t)[:, :t // 2]
    off = np.concatenate([np.cos(ang_off), np.sin(ang_off)], axis=1)
    return jnp.asarray(np.cos(ang) * scale, F32), jnp.asarray(np.sin(ang) * scale, F32), jnp.asarray(off, F32)


def _block_diag(w):
    g, a, b = w.shape
    eye = jnp.eye(g, dtype=w.dtype)
    return (eye[:, None, :, None] * w[:, :, None, :]).reshape(g * a, g * b)


def kernel(x_prompt, x_sample, cache_ckv, cache_kpe, c, c_ctx, w_mod, b_mod, norm1_g, w_in, q_norm_g, w_uq,
           kv_norm_g, w_ukv, w_fmix, w_out, norm2_g, w_router, w_e_gate, w_e_up, w_e_down, final_g):
    assert w_mod.shape[0] == 1, "single-layer problem"
    n_ctx, t_ctx, _ = x_prompt.shape
    n_lat, t_lat, _ = x_sample.shape
    past = cache_ckv.shape[2]
    ctx_row = n_lat

    w_in0 = w_in[0]
    kpe_cols = w_in0[:, Q_LORA + KV_LORA:Q_LORA + KV_LORA + QK_ROPE]
    win = jnp.concatenate([w_in0[:, :Q_LORA + KV_LORA + QK_ROPE], _rot_half(kpe_cols),
                           jnp.zeros((D_MODEL, 512 - Q_LORA - KV_LORA - 2 * QK_ROPE), F32),
                           w_in0[:, Q_LORA + KV_LORA + QK_ROPE:]], axis=1).astype(BF16)
    wq3 = w_uq[0].reshape(Q_LORA, N_HEADS, QK_NOPE + QK_ROPE)
    qpad = jnp.zeros((Q_LORA, N_HEADS, HEAD_PAD - QK_NOPE - QK_ROPE), F32)
    wuq_main = jnp.concatenate([wq3, qpad], axis=2).reshape(Q_LORA, QK_W)
    wuq_rot = _rot_half(wq3[..., QK_NOPE:]).reshape(Q_LORA, N_HEADS * QK_ROPE)
    wuq_lat = jnp.concatenate([wuq_main, wuq_rot], axis=1).T.astype(BF16)
    wuq_ctx = wuq_main.T.astype(BF16)
    wkv3 = w_ukv[0].reshape(KV_LORA, N_HEADS, QK_NOPE + V_HEAD)
    wk_top = jnp.concatenate([wkv3[..., :QK_NOPE], jnp.zeros((KV_LORA, N_HEADS, HEAD_PAD - QK_NOPE), F32)],
                             axis=2).reshape(KV_LORA, QK_W)
    place = jnp.concatenate([jnp.zeros((QK_ROPE, QK_NOPE), F32), jnp.eye(QK_ROPE, dtype=F32),
                             jnp.zeros((QK_ROPE, HEAD_PAD - QK_NOPE - QK_ROPE), F32)], axis=1)
    place = jnp.tile(place, (1, N_HEADS))
    wk = jnp.concatenate([wk_top, place, place, jnp.zeros((LANES - 2 * QK_ROPE, QK_W), F32)], axis=0).astype(BF16)
    wv = wkv3[..., QK_NOPE:].reshape(KV_LORA, V_W).T.astype(BF16)
    wo = w_out[0].astype(BF16)
    wr_t = w_router[0].T.astype(BF16)

    cos, sin = _rope_tables(t_lat)
    tq_lat, tk_lat = _qk_tables(cos, sin)
    assert n_ctx % CTX_REQS == 0
    tq_ctx, tk_ctx = _qk_tables(np.ones((CTX_REQS * t_ctx, QK_ROPE)), np.zeros((CTX_REQS * t_ctx, QK_ROPE)))
    ch_ang = _dft_angles(np.arange(FNET_CH, dtype=np.int64), FNET_CH)
    dft_ctx = _dft_tables(t_ctx)
    dft_lat = _dft_half_tables(t_lat)

    assert n_lat + 1 <= SUBLANES
    c8 = jnp.concatenate([c, c_ctx[None, :], jnp.zeros((SUBLANES - n_lat - 1, D_MODEL), F32)], axis=0)
    mods6 = _mods(c8, w_mod[0], b_mod[0][None, :]).reshape(SUBLANES, 6, D_MODEL)
    cw, sw = _fold(jnp.asarray(np.cos(ch_ang), F32), jnp.asarray(np.sin(ch_ang), F32), w_fmix[0])
    gh = FNET_GROUPS // 2
    wcs = jnp.concatenate([jnp.concatenate([_block_diag(cw[a * gh:(a + 1) * gh]), _block_diag(sw[a * gh:(a + 1) * gh])],
                                           axis=1) for a in range(2)], axis=0).astype(BF16)

    g1 = norm1_g[0][None, :]
    qg = q_norm_g[0][None, :]
    kvg = kv_norm_g[0][None, :]
    g2 = norm2_g[0][None, :]
    fg = final_g[None, :]

    xp = x_prompt.reshape(n_ctx * t_ctx, D_MODEL)
    xs = x_sample.reshape(n_lat * t_lat, D_MODEL)
    tiles_lat = t_lat // MIX_TILE

    x1c, h2c, affc, ckv_c, kpe_c = _ctx_front(
        xp, mods6, g1, win, qg, wuq_ctx, kvg, wk, wv, wcs, tq_ctx, tk_ctx,
        *dft_ctx, wo, g2, wr_t, n_req=n_ctx, t=t_ctx, mod_row=ctx_row)
    ql, kl, vl, zcl, zsl = _premix(
        xs, mods6, g1, win, qg, wuq_lat, kvg, wk, wv, wcs, tq_lat, tk_lat,
        mod_row=lambda i: i // tiles_lat, tab_row=lambda i: i % tiles_lat, rope=True)
    xk_cache = jnp.concatenate([cache_ckv[:, 0], cache_kpe[:, 0],
                                jnp.zeros((n_lat, past, 2 * LANES - KV_LORA - QK_ROPE), F32)],
                               axis=-1).reshape(n_lat * past, 2 * LANES).astype(BF16)
    kpast, vpast = _cachekv(xk_cache, wk, wv)

    attn_l = _attention(ql, [kpast, kl], [vpast, vl], n_req=n_lat, t_q=t_lat, kv_lens=[past, t_lat], tq=TOK_TILE,
                        pairs_per_step=2)

    x1l, h2l, affl = _mixout(xs, attn_l, zcl, zsl, *dft_lat, wo, mods6, g2, wr_t,
                             n_req=n_lat, t=t_lat, mod_row=lambda b: b)

    cap_c = CAP_FACTOR * t_ctx // N_EXPERTS
    cap_l = CAP_FACTOR * t_lat // N_EXPERTS
    affc2 = affc.reshape(n_ctx * N_EXPERTS, t_ctx)
    affl2 = affl.reshape(n_lat * N_EXPERTS, t_lat)
    posc, _ = _route(affc2, cap_c)
    posl, offl = _route(affl2, cap_l)
    assert t_lat // TOK_TILE + 1 <= OFF_STRIDE and cap_l % SLOT_WIN == 0
    offl = offl[:, :OFF_STRIDE].reshape(-1)
    xsc, gc = _gather(posc, affc2, h2c, n_req=n_ctx, n=t_ctx, cap=cap_c, rps=MOE_REQS)
    xsl, gl = _gather_win(offl, posl, affl2, h2l, n_req=n_lat, n=t_lat, cap=cap_l)
    ysc, ysl = _ffn(xsc, xsl, gc, gl, w_e_gate[0], w_e_up[0], w_e_down[0])

    posc_tok = posc.reshape(n_ctx, N_EXPERTS, t_ctx).transpose(0, 2, 1)
    posl_tok = posl.reshape(n_lat, N_EXPERTS, t_lat).transpose(0, 2, 1)
    y_prompt = _combine(x1c, posc_tok, ysc, mods6, fg, n_req=n_ctx, n=t_ctx, cap=cap_c, mod_row=ctx_row,
                        rps=MOE_REQS)
    y_sample = _combine_win(offl, x1l, posl_tok, ysl, mods6, fg, n_req=n_lat, n=t_lat, cap=cap_l,
                            mod_row=lambda b: b)

    return (y_prompt.reshape(n_ctx, t_ctx, D_MODEL), y_sample.reshape(n_lat, t_lat, D_MODEL),
            ckv_c.reshape(n_ctx, 1, t_ctx, KV_LORA), kpe_c.transpose(0, 2, 1).reshape(n_ctx, 1, t_ctx, QK_ROPE))
```

```python
import functools

import jax
import jax.numpy as jnp
import numpy as np
from jax import lax
from jax.experimental import pallas as pl
from jax.experimental.pallas import tpu as pltpu

F32 = jnp.float32
BF16 = jnp.bfloat16

D_MODEL = 1024
N_HEADS = 8
QK_NOPE = 64
QK_ROPE = 32
V_HEAD = 64
Q_LORA = 256
KV_LORA = 128
FNET_GROUPS = 8
FNET_CH = 64
FNET_W = FNET_GROUPS * FNET_CH
N_EXPERTS = 16
CAP_FACTOR = 2
D_EXPERT = 512
GRID_W = 64
ROPE_BASE = 10000.0
EPS = 1e-6

LANES = 128
SUBLANES = 8
HEAD_PAD = LANES
QK_W = N_HEADS * HEAD_PAD
V_W = N_HEADS * V_HEAD
TOK_TILE = 256
MIX_TILE = 512
MODS_TILE = 1536
CTX_REQS = 4
MOE_REQS = 4
KEY_CHUNK = 512
SLOT_WIN = 64
WIN_BLOCKS = 4
OFF_STRIDE = 16
VMEM_LIMIT = 48 * 1024 * 1024

_NT = (((1,), (1,)), ((), ()))


def _cparams(sem):
    return pltpu.CompilerParams(dimension_semantics=sem, vmem_limit_bytes=VMEM_LIMIT)


def _rms(x, g):
    return x * lax.rsqrt(jnp.mean(x * x, axis=-1, keepdims=True) + EPS) * g


def _dot(a, b):
    return jnp.dot(a, b, preferred_element_type=F32)


def _mods_kernel(c_ref, w_ref, b_ref, o_ref):
    c = c_ref[...]
    s = c * jax.nn.sigmoid(c)
    o_ref[...] = _dot(s.astype(BF16), w_ref[...].astype(BF16)) + b_ref[...]


def _mods(c8, w_mod, b_mod):
    n = w_mod.shape[1]
    tn = MODS_TILE
    return pl.pallas_call(
        _mods_kernel,
        grid=(n // tn,),
        in_specs=[pl.BlockSpec((SUBLANES, D_MODEL), lambda j: (0, 0)),
                  pl.BlockSpec((D_MODEL, tn), lambda j: (0, j)),
                  pl.BlockSpec((1, tn), lambda j: (0, j))],
        out_specs=pl.BlockSpec((SUBLANES, tn), lambda j: (0, j)),
        out_shape=jax.ShapeDtypeStruct((SUBLANES, n), F32),
        compiler_params=_cparams(("arbitrary",)),
        name="mods",
    )(c8, w_mod, b_mod)


def _fold_kernel(cc_ref, sc_ref, w_ref, cw_ref, sw_ref):
    for g in range(FNET_GROUPS):
        w = w_ref[g]
        cw_ref[g] = jnp.dot(cc_ref[...], w, preferred_element_type=F32, precision=lax.Precision.HIGHEST)
        sw_ref[g] = jnp.dot(sc_ref[...], w, preferred_element_type=F32, precision=lax.Precision.HIGHEST)


def _fold(cc, sc, w_fmix):
    shp = jax.ShapeDtypeStruct((FNET_GROUPS, FNET_CH, FNET_CH), F32)
    return pl.pallas_call(_fold_kernel, out_shape=(shp, shp), name="fold")(cc, sc, w_fmix)


KPE_LO = Q_LORA + KV_LORA
PROJ_W = 1024


def _winprep_kernel(wt_ref, o_ref):
    half = QK_ROPE // 2
    kpe_hi = KPE_LO + QK_ROPE
    o_ref[0:kpe_hi, :] = wt_ref[0:kpe_hi, :].astype(BF16)
    o_ref[kpe_hi:kpe_hi + half, :] = (-wt_ref[KPE_LO + half:kpe_hi, :]).astype(BF16)
    o_ref[kpe_hi + half:kpe_hi + QK_ROPE, :] = wt_ref[KPE_LO:KPE_LO + half, :].astype(BF16)
    o_ref[kpe_hi + QK_ROPE:PROJ_W - FNET_W, :] = jnp.zeros((PROJ_W - FNET_W - kpe_hi - QK_ROPE, D_MODEL), BF16)
    o_ref[PROJ_W - FNET_W:PROJ_W, :] = wt_ref[kpe_hi:kpe_hi + FNET_W, :].astype(BF16)


def _winprep(w_in_t):
    assert w_in_t.shape == (KPE_LO + QK_ROPE + FNET_W, D_MODEL)
    return pl.pallas_call(
        _winprep_kernel,
        out_shape=jax.ShapeDtypeStruct((PROJ_W, D_MODEL), BF16),
        compiler_params=pltpu.CompilerParams(vmem_limit_bytes=VMEM_LIMIT),
        name="winprep",
    )(w_in_t)


def _premix_body(x, m_ref, g1_ref, win_ref, qg_ref, wuq_ref, kvg_ref, wk_ref, wv_ref, wcs_ref, tq_ref, tk_ref, rope):
    shift1 = m_ref[0, 0:1, :]
    scale1 = m_ref[0, 1:2, :]
    h = _rms(x, g1_ref[...]) * (1.0 + scale1) + shift1
    proj = lax.dot_general(h.astype(BF16), win_ref[...], _NT, preferred_element_type=F32)
    qn = _rms(proj[:, 0:Q_LORA], qg_ref[...]).astype(BF16)
    qq = lax.dot_general(wuq_ref[...], qn, _NT, preferred_element_type=F32)
    cosq = tq_ref[0:LANES, :]
    sinq = tq_ref[LANES:2 * LANES, :]
    q_heads = []
    for hd in range(N_HEADS):
        lo = hd * HEAD_PAD
        qh = qq[lo:lo + HEAD_PAD, :] * cosq
        if rope:
            rot = qq[QK_W + hd * QK_ROPE:QK_W + (hd + 1) * QK_ROPE, :] * sinq[QK_NOPE:QK_NOPE + QK_ROPE, :]
            qh = qh + jnp.concatenate([jnp.zeros((QK_NOPE, rot.shape[1]), F32), rot,
                                       jnp.zeros((HEAD_PAD - QK_NOPE - QK_ROPE, rot.shape[1]), F32)], axis=0)
        q_heads.append(qh.astype(BF16))
    ckv = _rms(proj[:, Q_LORA:Q_LORA + KV_LORA], kvg_ref[...])
    kpe2 = proj[:, Q_LORA + KV_LORA:Q_LORA + KV_LORA + LANES] * tk_ref[...]
    xk = jnp.concatenate([ckv, kpe2], axis=1).astype(BF16)
    k = _dot(xk, wk_ref[...]).astype(BF16)
    v_t = lax.dot_general(wv_ref[...], xk[:, 0:KV_LORA], _NT, preferred_element_type=F32).astype(BF16)
    hw = FNET_W // 2
    f_in = proj[:, 512:1024].astype(BF16)
    z = [_dot(f_in[:, a * hw:(a + 1) * hw], wcs_ref[a * hw:(a + 1) * hw, :]) for a in range(2)]
    zc = jnp.concatenate([za[:, 0:hw] for za in z], axis=1).astype(BF16)
    zs = jnp.concatenate([za[:, hw:2 * hw] for za in z], axis=1).astype(BF16)
    kpe = proj[:, Q_LORA + KV_LORA:Q_LORA + KV_LORA + LANES]
    return q_heads, k, v_t, zc, zs, ckv, kpe


def _premix_kernel(x_ref, m_ref, g1_ref, win_ref, qg_ref, wuq_ref, kvg_ref, wk_ref, wv_ref, wcs_ref,
                   tq_ref, tk_ref, q_ref, k_ref, v_ref, zc_ref, zs_ref, *, rope):
    q_heads, k, v_t, zc, zs, _, _ = _premix_body(x_ref[...], m_ref, g1_ref, win_ref, qg_ref, wuq_ref, kvg_ref,
                                                 wk_ref, wv_ref, wcs_ref, tq_ref, tk_ref, rope)
    for hd, qh in enumerate(q_heads):
        q_ref[hd * HEAD_PAD:(hd + 1) * HEAD_PAD, :] = qh
    k_ref[...] = k
    v_ref[...] = v_t
    zc_ref[...] = zc
    zs_ref[...] = zs


def _premix(x, mods6, g1, win, qg, wuq, kvg, wk, wv, wcs, tq, tk, *, mod_row, tab_row, rope):
    n = x.shape[0]
    tm = MIX_TILE
    full = lambda a: pl.BlockSpec(a.shape, lambda i: (0,) * a.ndim)
    out_shape = [jax.ShapeDtypeStruct((QK_W, n), BF16), jax.ShapeDtypeStruct((n, QK_W), BF16),
                 jax.ShapeDtypeStruct((V_W, n), BF16), jax.ShapeDtypeStruct((n, FNET_W), BF16),
                 jax.ShapeDtypeStruct((n, FNET_W), BF16)]
    out_specs = [pl.BlockSpec((QK_W, tm), lambda i: (0, i)), pl.BlockSpec((tm, QK_W), lambda i: (i, 0)),
                 pl.BlockSpec((V_W, tm), lambda i: (0, i)), pl.BlockSpec((tm, FNET_W), lambda i: (i, 0)),
                 pl.BlockSpec((tm, FNET_W), lambda i: (i, 0))]
    return pl.pallas_call(
        functools.partial(_premix_kernel, rope=rope),
        grid=(n // tm,),
        in_specs=[pl.BlockSpec((tm, D_MODEL), lambda i: (i, 0)),
                  pl.BlockSpec((1, 6, D_MODEL), lambda i: (mod_row(i), 0, 0)),
                  full(g1), full(win), full(qg), full(wuq), full(kvg), full(wk), full(wv), full(wcs),
                  pl.BlockSpec((2 * LANES, tm), lambda i: (0, tab_row(i))),
                  pl.BlockSpec((tm, LANES), lambda i: (tab_row(i), 0))],
        out_specs=out_specs,
        out_shape=out_shape,
        compiler_params=_cparams(("parallel",)),
        name="premix",
    )(x, mods6, g1, win, qg, wuq, kvg, wk, wv, wcs, tq, tk)


def _cachekv_kernel(xk_ref, wk_ref, wv_ref, k_ref, v_ref):
    xk = xk_ref[...]
    k_ref[...] = _dot(xk, wk_ref[...]).astype(BF16)
    v_ref[...] = lax.dot_general(wv_ref[...], xk[:, 0:KV_LORA], _NT, preferred_element_type=F32).astype(BF16)


def _cachekv(xk, wk, wv):
    n = xk.shape[0]
    tm = 512
    full = lambda a: pl.BlockSpec(a.shape, lambda i: (0,) * a.ndim)
    return pl.pallas_call(
        _cachekv_kernel,
        grid=(n // tm,),
        in_specs=[pl.BlockSpec((tm, 2 * LANES), lambda i: (i, 0)), full(wk), full(wv)],
        out_specs=[pl.BlockSpec((tm, QK_W), lambda i: (i, 0)), pl.BlockSpec((V_W, tm), lambda i: (0, i))],
        out_shape=[jax.ShapeDtypeStruct((n, QK_W), BF16), jax.ShapeDtypeStruct((V_W, n), BF16)],
        compiler_params=_cparams(("parallel",)),
        name="cachekv",
    )(xk, wk, wv)


def _attn_body(q_heads, k_refs, v_refs, kc):
    tq = q_heads[0].shape[1]
    zero = jnp.zeros((HEAD_PAD, tq), BF16)
    n_pairs = len(q_heads) // 2
    qbd = [jnp.concatenate([jnp.concatenate([q_heads[2 * pr], zero], axis=1),
                            jnp.concatenate([zero, q_heads[2 * pr + 1]], axis=1)], axis=0) for pr in range(n_pairs)]
    chunks = [(k_ref, v_ref, c0, min(c0 + kc, k_ref.shape[0]))
              for k_ref, v_ref in zip(k_refs, v_refs) for c0 in range(0, k_ref.shape[0], kc)]
    work = [(pr, ch) for pr in range(n_pairs) for ch in chunks]

    def score(item):
        pr, (k_ref, _, c0, c1) = item
        return _dot(k_ref[c0:c1, pr * 2 * HEAD_PAD:(pr + 1) * 2 * HEAD_PAD], qbd[pr]).astype(BF16)

    m = [None] * n_pairs
    o = [None] * n_pairs
    s_next = score(work[0])
    for wi, (pr, (k_ref, v_ref, c0, c1)) in enumerate(work):
        s = s_next
        if wi + 1 < len(work):
            s_next = score(work[wi + 1])
        cm = jnp.max(s, axis=0, keepdims=True)
        vlo = pr * 2 * V_HEAD
        va = jnp.concatenate([v_ref[vlo:vlo + 2 * V_HEAD, c0:c1], jnp.ones((16, c1 - c0), BF16)], axis=0)
        if m[pr] is None:
            m[pr] = cm
            o[pr] = _dot(va, jnp.exp2(s - cm))
        else:
            m_new = jnp.maximum(m[pr], cm)
            alpha = jnp.exp2(m[pr].astype(F32) - m_new.astype(F32))
            o[pr] = alpha * o[pr] + _dot(va, jnp.exp2(s - m_new))
            m[pr] = m_new
    outs = []
    for pr in range(n_pairs):
        on = o[pr][0:2 * V_HEAD, :] * (1.0 / o[pr][2 * V_HEAD:2 * V_HEAD + 1, :])
        ot = jnp.concatenate([on[0:V_HEAD, 0:tq], on[V_HEAD:2 * V_HEAD, tq:2 * tq]], axis=0)
        outs.append(ot.T.astype(BF16))
    return outs


def _attn_kernel(q_ref, *refs, n_kv, n_pairs, kc):
    q_heads = [q_ref[hd * HEAD_PAD:(hd + 1) * HEAD_PAD, :] for hd in range(2 * n_pairs)]
    outs = _attn_body(q_heads, refs[:n_kv], refs[n_kv:2 * n_kv], kc)
    o_ref = refs[2 * n_kv]
    for pr, o in enumerate(outs):
        o_ref[:, pr * 2 * V_HEAD:(pr + 1) * 2 * V_HEAD] = o


def _attention(q_t, ks, vs_t, *, n_req, t_q, kv_lens, tq, pairs_per_step):
    n_kv = len(ks)
    nq = t_q // tq
    pp = pairs_per_step
    in_specs = [pl.BlockSpec((pp * 2 * HEAD_PAD, tq), lambda b, p, i: (p, b * nq + i))]
    in_specs += [pl.BlockSpec((kl, pp * 2 * HEAD_PAD), lambda b, p, i: (b, p)) for kl in kv_lens]
    in_specs += [pl.BlockSpec((pp * 2 * V_HEAD, kl), lambda b, p, i: (p, b)) for kl in kv_lens]
    return pl.pallas_call(
        functools.partial(_attn_kernel, n_kv=n_kv, n_pairs=pp, kc=KEY_CHUNK),
        grid=(n_req, N_HEADS // 2 // pp, nq),
        in_specs=in_specs,
        out_specs=pl.BlockSpec((tq, pp * 2 * V_HEAD), lambda b, p, i: (b * nq + i, p)),
        out_shape=jax.ShapeDtypeStruct((n_req * t_q, V_W), BF16),
        compiler_params=_cparams(("parallel", "parallel", "parallel")),
        name="attn",
    )(q_t, *ks, *vs_t)


def _mixout_body(x, attn_pairs, fm, wo_ref, m_ref, g2_ref, wr_ref):
    y = _dot(fm, wo_ref[V_W:V_W + FNET_W, :])
    col = 0
    for a in attn_pairs:
        y = y + _dot(a, wo_ref[col:col + a.shape[1], :])
        col += a.shape[1]
    gate1 = m_ref[0, 2:3, :]
    shift2 = m_ref[0, 3:4, :]
    scale2 = m_ref[0, 4:5, :]
    x1 = x + gate1 * y
    h2 = (_rms(x1, g2_ref[...]) * (1.0 + scale2) + shift2).astype(BF16)
    lg = lax.dot_general(wr_ref[...], h2, _NT, preferred_element_type=F32)
    e = jnp.exp(lg - jnp.max(lg, axis=0, keepdims=True))
    return x1, h2, e / jnp.sum(e, axis=0, keepdims=True)


def _mixout_kernel(x_ref, a_ref, zc_ref, zs_ref, cb_ref, sb_ref, off_ref, wo_ref, m_ref, g2_ref, wr_ref,
                   x1_ref, h2_ref, aff_ref, zp_ref, zm_ref, il_ref, *, t):
    half = t // 2
    i = pl.program_id(1)

    @pl.when(i == 0)
    def _fold_halves():
        for src, col in ((zc_ref, 0), (zs_ref, FNET_W)):
            lo = src[0:half, :]
            hi = src[half:t, :]
            zp_ref[:, col:col + FNET_W] = lo + hi
            zm_ref[:, col:col + FNET_W] = lo - hi

    off = off_ref[pl.ds(i, 1), :]
    co = off[:, 0:half]
    so = off[:, half:t]
    cb = cb_ref[...]
    sb = sb_ref[...]
    ct = (cb * co - sb * so).astype(BF16)
    st = (sb * co + cb * so).astype(BF16)
    tr = cb.shape[0]
    h = tr // 2
    even = _dot(ct[0:h, :], zp_ref[:, 0:FNET_W]) - _dot(st[0:h, :], zp_ref[:, FNET_W:2 * FNET_W])
    odd = _dot(ct[h:, :], zm_ref[:, 0:FNET_W]) - _dot(st[h:, :], zm_ref[:, FNET_W:2 * FNET_W])
    for c in range(FNET_W // LANES):
        il_ref[c, pl.ds(0, h, stride=2), :] = even[:, c * LANES:(c + 1) * LANES]
        il_ref[c, pl.ds(1, h, stride=2), :] = odd[:, c * LANES:(c + 1) * LANES]
    fm = jnp.concatenate([il_ref[c] for c in range(FNET_W // LANES)], axis=1).astype(BF16)
    x1, h2, aff = _mixout_body(x_ref[...], [a_ref[...]], fm, wo_ref, m_ref, g2_ref, wr_ref)
    x1_ref[...] = x1
    h2_ref[...] = h2
    aff_ref[...] = aff


def _ctx_front_kernel(x_ref, m_ref, g1_ref, win_ref, qg_ref, wuq_ref, kvg_ref, wk_ref, wv_ref, wcs_ref,
                      tq_ref, tk_ref, ct_ref, st_ref, wo_ref, g2_ref, wr_ref,
                      x1_ref, h2_ref, aff_ref, ckv_ref, kpe_ref, *, kc, t):
    x = x_ref[...]
    q_heads, k, v_t, zc, zs, ckv, kpe = _premix_body(x, m_ref, g1_ref, win_ref, qg_ref, wuq_ref, kvg_ref,
                                                     wk_ref, wv_ref, wcs_ref, tq_ref, tk_ref, False)
    ckv_ref[...] = ckv
    for r in range(x.shape[0] // t):
        kpe_ref[r] = kpe[r * t:(r + 1) * t, :].T[0:QK_ROPE, :]
    attn, fm = [], []
    for r in range(x.shape[0] // t):
        rows = slice(r * t, (r + 1) * t)
        attn.append(jnp.concatenate(_attn_body([q[:, rows] for q in q_heads], [k[rows, :]], [v_t[:, rows]], kc),
                                    axis=1))
        fm.append((_dot(ct_ref[...], zc[rows, :]) - _dot(st_ref[...], zs[rows, :])).astype(BF16))
    x1, h2, aff = _mixout_body(x_ref[...], [jnp.concatenate(attn, axis=0)], jnp.concatenate(fm, axis=0),
                               wo_ref, m_ref, g2_ref, wr_ref)
    x1_ref[...] = x1
    h2_ref[...] = h2
    for r in range(x.shape[0] // t):
        aff_ref[r] = aff[:, r * t:(r + 1) * t]


def _ctx_front(x, mods6, g1, win, qg, wuq, kvg, wk, wv, wcs, tq, tk, ct, st, wo, g2, wr_t, *, n_req, t, mod_row):
    full = lambda a: pl.BlockSpec(a.shape, lambda b: (0,) * a.ndim)
    rps = CTX_REQS
    row = lambda w: pl.BlockSpec((rps * t, w), lambda b: (b, 0))
    return pl.pallas_call(
        functools.partial(_ctx_front_kernel, kc=KEY_CHUNK, t=t),
        grid=(n_req // rps,),
        in_specs=[row(D_MODEL), pl.BlockSpec((1, 6, D_MODEL), lambda b: (mod_row, 0, 0)),
                  full(g1), full(win), full(qg), full(wuq), full(kvg), full(wk), full(wv), full(wcs),
                  full(tq), full(tk), full(ct), full(st), full(wo), full(g2), full(wr_t)],
        out_specs=[row(D_MODEL), row(D_MODEL), pl.BlockSpec((rps, N_EXPERTS, t), lambda b: (b, 0, 0)),
                   row(KV_LORA), pl.BlockSpec((rps, QK_ROPE, t), lambda b: (b, 0, 0))],
        out_shape=[jax.ShapeDtypeStruct((n_req * t, D_MODEL), F32),
                   jax.ShapeDtypeStruct((n_req * t, D_MODEL), BF16),
                   jax.ShapeDtypeStruct((n_req, N_EXPERTS, t), F32),
                   jax.ShapeDtypeStruct((n_req * t, KV_LORA), F32),
                   jax.ShapeDtypeStruct((n_req, QK_ROPE, t), F32)],
        compiler_params=_cparams(("parallel",)),
        name="ctx_front",
    )(x, mods6, g1, win, qg, wuq, kvg, wk, wv, wcs, tq, tk, ct, st, wo, g2, wr_t)


def _mixout(x, attn, zc, zs, cb, sb, off, wo, mods6, g2, wr_t, *, n_req, t, mod_row):
    tr = MIX_TILE
    nr = t // tr
    full = lambda a: pl.BlockSpec(a.shape, lambda b, i: (0,) * a.ndim)
    return pl.pallas_call(
        functools.partial(_mixout_kernel, t=t),
        grid=(n_req, nr),
        in_specs=[pl.BlockSpec((tr, D_MODEL), lambda b, i: (b * nr + i, 0)),
                  pl.BlockSpec((tr, V_W), lambda b, i: (b * nr + i, 0)),
                  pl.BlockSpec((t, FNET_W), lambda b, i: (b, 0)),
                  pl.BlockSpec((t, FNET_W), lambda b, i: (b, 0)),
                  full(cb), full(sb), full(off),
                  full(wo),
                  pl.BlockSpec((1, 6, D_MODEL), lambda b, i: (mod_row(b), 0, 0)),
                  full(g2), full(wr_t)],
        out_specs=[pl.BlockSpec((tr, D_MODEL), lambda b, i: (b * nr + i, 0)),
                   pl.BlockSpec((tr, D_MODEL), lambda b, i: (b * nr + i, 0)),
                   pl.BlockSpec((None, N_EXPERTS, tr), lambda b, i: (b, 0, i))],
        out_shape=[jax.ShapeDtypeStruct((n_req * t, D_MODEL), F32),
                   jax.ShapeDtypeStruct((n_req * t, D_MODEL), BF16),
                   jax.ShapeDtypeStruct((n_req, N_EXPERTS, t), F32)],
        scratch_shapes=[pltpu.VMEM((t // 2, 2 * FNET_W), BF16), pltpu.VMEM((t // 2, 2 * FNET_W), BF16),
                        pltpu.VMEM((FNET_W // LANES, tr, LANES), F32)],
        compiler_params=_cparams(("parallel", "arbitrary")),
        name="mixout",
    )(x, attn, zc, zs, cb, sb, off, wo, mods6, g2, wr_t)


def _prefix_count(flags, tri):
    n = flags.shape[1]
    carry = None
    outs = []
    ends = []
    for j in range(n // TOK_TILE):
        c = _dot(flags[:, j * TOK_TILE:(j + 1) * TOK_TILE].astype(BF16), tri)
        if carry is not None:
            c = c + carry
        outs.append(c)
        carry = c[:, TOK_TILE - 1:TOK_TILE]
        ends.append(carry)
    return (outs[0] if len(outs) == 1 else jnp.concatenate(outs, axis=1)), ends


def _route_kernel(aff_ref, pos_ref, off_ref, *, cap):
    a = aff_ref[...]
    rows = a.shape[0]
    capf = jnp.float32(cap)
    thr = jnp.zeros((rows, 1), jnp.int32)
    for bit in range(30, -1, -1):
        cand = thr | jnp.int32(1 << bit)
        cand_f = lax.bitcast_convert_type(cand, F32)
        cnt = jnp.sum(jnp.where(a >= cand_f, 1.0, 0.0), axis=1, keepdims=True)
        thr = jnp.where(cnt >= capf, cand, thr)
    thr_f = lax.bitcast_convert_type(thr, F32)
    above_f = lax.bitcast_convert_type(thr + 1, F32)
    gt = jnp.where(a >= above_f, 1.0, 0.0)
    tie = jnp.where(a >= thr_f, 1.0, 0.0) - gt
    need = capf - jnp.sum(gt, axis=1, keepdims=True)
    r_i = lax.broadcasted_iota(jnp.int32, (TOK_TILE, TOK_TILE), 0)
    c_i = lax.broadcasted_iota(jnp.int32, (TOK_TILE, TOK_TILE), 1)
    tri = jnp.where(r_i <= c_i, 1.0, 0.0).astype(BF16)
    tie_before = _prefix_count(tie, tri)[0] - tie
    sel = gt + tie * jnp.where(tie_before < need, 1.0, 0.0)
    count, ends = _prefix_count(sel, tri)
    pos_ref[...] = jnp.where(sel > 0.5, count - 1.0, -1.0)
    lane = lax.broadcasted_iota(jnp.int32, (rows, LANES), 1)
    offs = jnp.zeros((rows, LANES), F32)
    for j, end in enumerate(ends):
        offs = offs + jnp.where(lane == j + 1, end, 0.0)
    off_ref[...] = offs.astype(jnp.int32)


def _route(aff_t, cap):
    return pl.pallas_call(
        functools.partial(_route_kernel, cap=cap),
        out_shape=[jax.ShapeDtypeStruct(aff_t.shape, F32),
                   jax.ShapeDtypeStruct((aff_t.shape[0], LANES), jnp.int32)],
        compiler_params=pltpu.CompilerParams(vmem_limit_bytes=VMEM_LIMIT),
        name="route",
    )(aff_t)


def _gather_kernel(pos_ref, aff_ref, h_ref, xs_ref, g_ref, *, cap):
    rps, ne, n = pos_ref.shape
    slot = lax.broadcasted_iota(jnp.int32, (ne, cap, n), 1).astype(F32)
    for r in range(rps):
        pos = pos_ref[r]
        aff = aff_ref[r]
        hit = pos[:, None, :] == slot
        onehot = jnp.where(hit, 1.0, 0.0).reshape(ne * cap, n).astype(BF16)
        xs = _dot(onehot, h_ref[r * n:(r + 1) * n, :])
        xs_ref[:, r * cap:(r + 1) * cap, :] = xs.astype(BF16).reshape(ne, cap, D_MODEL)
        g_ref[:, r * cap:(r + 1) * cap, :] = jnp.sum(jnp.where(hit, aff[:, None, :], 0.0), axis=2, keepdims=True)


def _gather(pos_t, aff_t, h2, *, n_req, n, cap, rps):
    pos3 = pos_t.reshape(n_req, N_EXPERTS, n)
    aff3 = aff_t.reshape(n_req, N_EXPERTS, n)
    return pl.pallas_call(
        functools.partial(_gather_kernel, cap=cap),
        grid=(n_req // rps,),
        in_specs=[pl.BlockSpec((rps, N_EXPERTS, n), lambda b: (b, 0, 0)),
                  pl.BlockSpec((rps, N_EXPERTS, n), lambda b: (b, 0, 0)),
                  pl.BlockSpec((rps * n, D_MODEL), lambda b: (b, 0))],
        out_specs=[pl.BlockSpec((N_EXPERTS, rps * cap, D_MODEL), lambda b: (0, b, 0)),
                   pl.BlockSpec((N_EXPERTS, rps * cap, 1), lambda b: (0, b, 0))],
        out_shape=[jax.ShapeDtypeStruct((N_EXPERTS, n_req * cap, D_MODEL), BF16),
                   jax.ShapeDtypeStruct((N_EXPERTS, n_req * cap, 1), F32)],
        compiler_params=_cparams(("parallel",)),
        name="gather",
    )(pos3, aff3, h2)


def _window_plan(off_ref, b, j, cap, w):
    w0 = []
    need = jnp.int32(0)
    for e in range(N_EXPERTS):
        idx = (b * N_EXPERTS + e) * OFF_STRIDE + j
        base = (off_ref[idx] >> 4) << 4
        w0.append(base)
        need = jnp.maximum(need, off_ref[idx + 1] - base)
    return w0, (need + (w - 1)) >> (w.bit_length() - 1)


def _window(w0_e, p, cap, w):
    low = w0_e + p * w
    start = pl.multiple_of(jnp.minimum(low, cap - w), 16)
    return low, start


def _gather_win_kernel(off_ref, pos_ref, aff_ref, h_ref, xs_ref, g_ref, *, cap, w):
    b = pl.program_id(0)
    step = pl.program_id(1)

    @pl.when(step == 0)
    def _init():
        xs_ref[...] = jnp.zeros_like(xs_ref)
        g_ref[...] = jnp.zeros_like(g_ref)

    r = lax.broadcasted_iota(jnp.int32, (w, 1), 0).astype(F32)
    n_sub = pos_ref.shape[1] // TOK_TILE
    for sub in range(n_sub):
        cols = slice(sub * TOK_TILE, (sub + 1) * TOK_TILE)
        pos = pos_ref[:, cols]
        aff = aff_ref[:, cols]
        h = h_ref[cols, :]
        w0, n_pass = _window_plan(off_ref, b, step * n_sub + sub, cap, w)

        def one_pass(p, carry, pos=pos, aff=aff, h=h, w0=w0):
            starts, hots, gates = [], [], []
            for e in range(N_EXPERTS):
                low, start = _window(w0[e], p, cap, w)
                starts.append(start)
                mine = jnp.where(r >= (low - start).astype(F32), 1.0, 0.0)
                hot = jnp.where(pos[e:e + 1, :] - start.astype(F32) == r, mine, 0.0)
                hots.append(hot)
                gates.append(jnp.sum(hot * aff[e:e + 1, :], axis=1, keepdims=True))
            rows = _dot(jnp.concatenate(hots, axis=0).astype(BF16), h).astype(BF16)
            for e in range(N_EXPERTS):
                win = pl.ds(starts[e], w)
                xs_ref[e, win, :] = xs_ref[e, win, :] + rows[e * w:(e + 1) * w, :]
                g_ref[e, win, :] = g_ref[e, win, :] + gates[e]
            return carry

        one_pass(0, 0)
        lax.fori_loop(1, n_pass, one_pass, 0)


def _gather_win(offs, pos_t, aff_t, h2, *, n_req, n, cap):
    tb = WIN_BLOCKS * TOK_TILE
    nb = n // tb
    grid_spec = pltpu.PrefetchScalarGridSpec(
        num_scalar_prefetch=1,
        grid=(n_req, nb),
        in_specs=[pl.BlockSpec((N_EXPERTS, tb), lambda b, j, off: (b, j)),
                  pl.BlockSpec((N_EXPERTS, tb), lambda b, j, off: (b, j)),
                  pl.BlockSpec((tb, D_MODEL), lambda b, j, off: (b * nb + j, 0))],
        out_specs=[pl.BlockSpec((N_EXPERTS, cap, D_MODEL), lambda b, j, off: (0, b, 0)),
                   pl.BlockSpec((N_EXPERTS, cap, 1), lambda b, j, off: (0, b, 0))])
    return pl.pallas_call(
        functools.partial(_gather_win_kernel, cap=cap, w=SLOT_WIN),
        grid_spec=grid_spec,
        out_shape=[jax.ShapeDtypeStruct((N_EXPERTS, n_req * cap, D_MODEL), BF16),
                   jax.ShapeDtypeStruct((N_EXPERTS, n_req * cap, 1), F32)],
        compiler_params=_cparams(("parallel", "arbitrary")),
        name="gather_win",
    )(offs, pos_t, aff_t, h2)


def _ffn_kernel(xc_ref, xl_ref, gc_ref, gl_ref, wg_ref, wu_ref, wd_ref, yc_ref, yl_ref, wgb, wub, wdb):
    wgb[...] = wg_ref[0].astype(BF16)
    wub[...] = wu_ref[0].astype(BF16)
    wdb[...] = wd_ref[0].astype(BF16)
    for x_ref, g_ref, y_ref in ((xc_ref, gc_ref, yc_ref), (xl_ref, gl_ref, yl_ref)):
        for j in range(x_ref.shape[1] // TOK_TILE):
            rows = slice(j * TOK_TILE, (j + 1) * TOK_TILE)
            x = x_ref[0, rows, :]
            gate = _dot(x, wgb[...])
            up = _dot(x, wub[...])
            hid = (gate * jax.nn.sigmoid(gate) * up).astype(BF16)
            ys = _dot(hid, wdb[...]) * g_ref[0, rows, :]
            y_ref[0, rows, :] = ys.astype(BF16)


def _ffn(xc, xl, gc, gl, wg, wu, wd):
    m = xc.shape[1]
    xspec = pl.BlockSpec((1, m, D_MODEL), lambda e: (e, 0, 0))
    gspec = pl.BlockSpec((1, m, 1), lambda e: (e, 0, 0))
    shp = jax.ShapeDtypeStruct((N_EXPERTS, m, D_MODEL), BF16)
    return pl.pallas_call(
        _ffn_kernel,
        grid=(N_EXPERTS,),
        in_specs=[xspec, xspec, gspec, gspec,
                  pl.BlockSpec((1, D_MODEL, D_EXPERT), lambda e: (e, 0, 0)),
                  pl.BlockSpec((1, D_MODEL, D_EXPERT), lambda e: (e, 0, 0)),
                  pl.BlockSpec((1, D_EXPERT, D_MODEL), lambda e: (e, 0, 0))],
        out_specs=[xspec, xspec],
        out_shape=[shp, shp],
        scratch_shapes=[pltpu.VMEM((D_MODEL, D_EXPERT), BF16), pltpu.VMEM((D_MODEL, D_EXPERT), BF16),
                        pltpu.VMEM((D_EXPERT, D_MODEL), BF16)],
        compiler_params=_cparams(("arbitrary",)),
        name="ffn",
    )(xc, xl, gc, gl, wg, wu, wd)


def _combine_kernel(x1_ref, pos_ref, ys_ref, m_ref, fg_ref, o_ref, *, cap):
    rps, n, _ = pos_ref.shape
    w = N_EXPERTS * cap
    e_i = lax.broadcasted_iota(jnp.int32, (N_EXPERTS, w), 0)
    j_i = lax.broadcasted_iota(jnp.int32, (N_EXPERTS, w), 1)
    spread = jnp.where((j_i >> (cap.bit_length() - 1)) == e_i, 1.0, 0.0).astype(BF16)
    lane_slot = (lax.broadcasted_iota(jnp.int32, (1, w), 1) & (cap - 1)).astype(F32)
    gate2 = m_ref[0, 5:6, :]
    for r in range(rps):
        pos = pos_ref[r].astype(BF16)
        onehot = jnp.where(_dot(pos, spread) == lane_slot, 1.0, 0.0).astype(BF16)
        acc = _dot(onehot, ys_ref[:, r * cap:(r + 1) * cap, :].reshape(w, D_MODEL))
        rows = slice(r * n, (r + 1) * n)
        o_ref[rows, :] = _rms(x1_ref[rows, :] + gate2 * acc, fg_ref[...])


def _combine(x1, pos_tok, ys, mods6, fg, *, n_req, n, cap, mod_row, rps):
    return pl.pallas_call(
        functools.partial(_combine_kernel, cap=cap),
        grid=(n_req // rps,),
        in_specs=[pl.BlockSpec((rps * n, D_MODEL), lambda b: (b, 0)),
                  pl.BlockSpec((rps, n, N_EXPERTS), lambda b: (b, 0, 0)),
                  pl.BlockSpec((N_EXPERTS, rps * cap, D_MODEL), lambda b: (0, b, 0)),
                  pl.BlockSpec((1, 6, D_MODEL), lambda b: (mod_row, 0, 0)),
                  pl.BlockSpec((1, D_MODEL), lambda b: (0, 0))],
        out_specs=pl.BlockSpec((rps * n, D_MODEL), lambda b: (b, 0)),
        out_shape=jax.ShapeDtypeStruct((n_req * n, D_MODEL), F32),
        compiler_params=_cparams(("parallel",)),
        name="combine",
    )(x1, pos_tok, ys, mods6, fg)


def _combine_win_kernel(off_ref, x1_ref, pos_ref, ys_ref, m_ref, fg_ref, o_ref, acc_ref, *, cap, w):
    b = pl.program_id(0)
    step = pl.program_id(1)
    width = N_EXPERTS * w
    e_i = lax.broadcasted_iota(jnp.int32, (N_EXPERTS, width), 0)
    j_i = lax.broadcasted_iota(jnp.int32, (N_EXPERTS, width), 1)
    spread = jnp.where((j_i >> (w.bit_length() - 1)) == e_i, 1.0, 0.0).astype(BF16)
    lane_slot = (lax.broadcasted_iota(jnp.int32, (1, width), 1) & (w - 1)).astype(F32)
    lane_e = lax.broadcasted_iota(jnp.int32, (1, N_EXPERTS), 1)
    n_sub = pos_ref.shape[0] // TOK_TILE
    for sub in range(n_sub):
        rows = slice(sub * TOK_TILE, (sub + 1) * TOK_TILE)
        pos = pos_ref[rows, :]
        w0, n_pass = _window_plan(off_ref, b, step * n_sub + sub, cap, w)

        def window_sum(p, pos=pos, w0=w0):
            start_row = jnp.zeros((1, N_EXPERTS), F32)
            first_row = jnp.zeros((1, N_EXPERTS), F32)
            wins = []
            for e in range(N_EXPERTS):
                low, start = _window(w0[e], p, cap, w)
                start_row = jnp.where(lane_e == e, start.astype(F32), start_row)
                first_row = jnp.where(lane_e == e, (low - start).astype(F32), first_row)
                wins.append(ys_ref[e, pl.ds(start, w), :])
            rel = pos - start_row
            rel = jnp.where(rel >= first_row, rel, -1.0).astype(BF16)
            onehot = jnp.where(_dot(rel, spread) == lane_slot, 1.0, 0.0).astype(BF16)
            return _dot(onehot, jnp.concatenate(wins, axis=0))

        def finish(acc, rows=rows):
            o_ref[rows, :] = _rms(x1_ref[rows, :] + m_ref[0, 5:6, :] * acc, fg_ref[...])

        acc0 = window_sum(0)
        acc_ref[sub] = acc0
        finish(acc0)

        @pl.when(n_pass > 1)
        def _more_passes(sub=sub, n_pass=n_pass, window_sum=window_sum, finish=finish):
            def one_pass(p, carry):
                acc_ref[sub] += window_sum(p)
                return carry

            lax.fori_loop(1, n_pass, one_pass, 0)
            finish(acc_ref[sub])


def _combine_win(offs, x1, pos_tok, ys, mods6, fg, *, n_req, n, cap, mod_row):
    tr = WIN_BLOCKS * TOK_TILE
    nr = n // tr
    grid_spec = pltpu.PrefetchScalarGridSpec(
        num_scalar_prefetch=1,
        grid=(n_req, nr),
        in_specs=[pl.BlockSpec((tr, D_MODEL), lambda b, i, off: (b * nr + i, 0)),
                  pl.BlockSpec((None, tr, N_EXPERTS), lambda b, i, off: (b, i, 0)),
                  pl.BlockSpec((N_EXPERTS, cap, D_MODEL), lambda b, i, off: (0, b, 0)),
                  pl.BlockSpec((1, 6, D_MODEL), lambda b, i, off: (mod_row(b), 0, 0)),
                  pl.BlockSpec((1, D_MODEL), lambda b, i, off: (0, 0))],
        out_specs=pl.BlockSpec((tr, D_MODEL), lambda b, i, off: (b * nr + i, 0)),
        scratch_shapes=[pltpu.VMEM((WIN_BLOCKS, TOK_TILE, D_MODEL), F32)])
    return pl.pallas_call(
        functools.partial(_combine_win_kernel, cap=cap, w=SLOT_WIN),
        grid_spec=grid_spec,
        out_shape=jax.ShapeDtypeStruct((n_req * n, D_MODEL), F32),
        compiler_params=_cparams(("parallel", "parallel")),
        name="combine_win",
    )(offs, x1, pos_tok, ys, mods6, fg)


def _rot_half(w):
    half = QK_ROPE // 2
    return jnp.concatenate([-w[..., half:], w[..., :half]], axis=-1)


def _rope_tables(t):
    n_rows = t // GRID_W
    rows = np.repeat(np.arange(n_rows, dtype=np.float64), GRID_W)
    cols = np.tile(np.arange(GRID_W, dtype=np.float64), n_rows)
    n_freq = QK_ROPE // 4
    inv_freq = ROPE_BASE ** (-np.arange(n_freq, dtype=np.float64) / n_freq)
    ang = np.concatenate([rows[:, None] * inv_freq, cols[:, None] * inv_freq], axis=-1)
    cos = np.concatenate([np.cos(ang), np.cos(ang)], axis=-1)
    sin = np.concatenate([np.sin(ang), np.sin(ang)], axis=-1)
    return cos, sin


def _qk_tables(cos, sin):
    t = cos.shape[0]
    scale = (QK_NOPE + QK_ROPE) ** -0.5 * np.log2(np.e)
    pad = np.zeros((t, HEAD_PAD - QK_NOPE - QK_ROPE))
    cosq = np.concatenate([np.full((t, QK_NOPE), scale), cos * scale, pad], axis=1)
    sinq = np.concatenate([np.zeros((t, QK_NOPE)), sin * scale, pad], axis=1)
    tq_t = np.concatenate([cosq, sinq], axis=1).T
    tk = np.concatenate([cos, sin, np.zeros((t, LANES - 2 * QK_ROPE))], axis=1)
    return jnp.asarray(tq_t, F32), jnp.asarray(tk, F32)


def _dft_angles(rows, t):
    k = np.arange(t, dtype=np.int64)
    return ((rows[:, None] * k[None, :]) % t).astype(np.float64) * (2.0 * np.pi / t)


def _dft_tables(t):
    ang = _dft_angles(np.arange(t, dtype=np.int64), t)
    scale = (t * FNET_CH) ** -0.5
    return jnp.asarray(np.cos(ang) * scale, F32).astype(BF16), jnp.asarray(np.sin(ang) * scale, F32).astype(BF16)


def _dft_half_tables(t):
    r = np.arange(MIX_TILE, dtype=np.int64)
    ang = _dft_angles(np.concatenate([r[0::2], r[1::2]]), t)[:, :t // 2]
    scale = (t * FNET_CH) ** -0.5
    ang_off = _dft_angles(np.arange(t // MIX_TILE, dtype=np.int64) * MIX_TILE, t)[:, :t // 2]
    off = np.concatenate([np.cos(ang_off), np.sin(ang_off)], axis=1)
    return jnp.asarray(np.cos(ang) * scale, F32), jnp.asarray(np.sin(ang) * scale, F32), jnp.asarray(off, F32)


def _block_diag(w):
    g, a, b = w.shape
    eye = jnp.eye(g, dtype=w.dtype)
    return (eye[:, None, :, None] * w[:, :, None, :]).reshape(g * a, g * b)


def kernel(x_prompt, x_sample, cache_ckv, cache_kpe, c, c_ctx, w_mod, b_mod, norm1_g, w_in, q_norm_g, w_uq,
           kv_norm_g, w_ukv, w_fmix, w_out, norm2_g, w_router, w_e_gate, w_e_up, w_e_down, final_g):
    assert w_mod.shape[0] == 1, "single-layer problem"
    n_ctx, t_ctx, _ = x_prompt.shape
    n_lat, t_lat, _ = x_sample.shape
    past = cache_ckv.shape[2]
    ctx_row = n_lat

    win = _winprep(jnp.swapaxes(w_in, 1, 2)[0])
    wq3 = w_uq[0].reshape(Q_LORA, N_HEADS, QK_NOPE + QK_ROPE)
    qpad = jnp.zeros((Q_LORA, N_HEADS, HEAD_PAD - QK_NOPE - QK_ROPE), F32)
    wuq_main = jnp.concatenate([wq3, qpad], axis=2).reshape(Q_LORA, QK_W)
    wuq_rot = _rot_half(wq3[..., QK_NOPE:]).reshape(Q_LORA, N_HEADS * QK_ROPE)
    wuq_lat = jnp.concatenate([wuq_main, wuq_rot], axis=1).T.astype(BF16)
    wuq_ctx = wuq_main.T.astype(BF16)
    wkv3 = w_ukv[0].reshape(KV_LORA, N_HEADS, QK_NOPE + V_HEAD)
    wk_top = jnp.concatenate([wkv3[..., :QK_NOPE], jnp.zeros((KV_LORA, N_HEADS, HEAD_PAD - QK_NOPE), F32)],
                             axis=2).reshape(KV_LORA, QK_W)
    place = jnp.concatenate([jnp.zeros((QK_ROPE, QK_NOPE), F32), jnp.eye(QK_ROPE, dtype=F32),
                             jnp.zeros((QK_ROPE, HEAD_PAD - QK_NOPE - QK_ROPE), F32)], axis=1)
    place = jnp.tile(place, (1, N_HEADS))
    wk = jnp.concatenate([wk_top, place, place, jnp.zeros((LANES - 2 * QK_ROPE, QK_W), F32)], axis=0).astype(BF16)
    wv = wkv3[..., QK_NOPE:].reshape(KV_LORA, V_W).T.astype(BF16)
    wo = w_out[0].astype(BF16)
    wr_t = w_router[0].T.astype(BF16)

    cos, sin = _rope_tables(t_lat)
    tq_lat, tk_lat = _qk_tables(cos, sin)
    assert n_ctx % CTX_REQS == 0
    tq_ctx, tk_ctx = _qk_tables(np.ones((CTX_REQS * t_ctx, QK_ROPE)), np.zeros((CTX_REQS * t_ctx, QK_ROPE)))
    ch_ang = _dft_angles(np.arange(FNET_CH, dtype=np.int64), FNET_CH)
    dft_ctx = _dft_tables(t_ctx)
    dft_lat = _dft_half_tables(t_lat)

    assert n_lat + 1 <= SUBLANES
    c8 = jnp.concatenate([c, c_ctx[None, :], jnp.zeros((SUBLANES - n_lat - 1, D_MODEL), F32)], axis=0)
    mods6 = _mods(c8, w_mod[0], b_mod[0][None, :]).reshape(SUBLANES, 6, D_MODEL)
    cw, sw = _fold(jnp.asarray(np.cos(ch_ang), F32), jnp.asarray(np.sin(ch_ang), F32), w_fmix[0])
    gh = FNET_GROUPS // 2
    wcs = jnp.concatenate([jnp.concatenate([_block_diag(cw[a * gh:(a + 1) * gh]), _block_diag(sw[a * gh:(a + 1) * gh])],
                                           axis=1) for a in range(2)], axis=0).astype(BF16)

    g1 = norm1_g[0][None, :]
    qg = q_norm_g[0][None, :]
    kvg = kv_norm_g[0][None, :]
    g2 = norm2_g[0][None, :]
    fg = final_g[None, :]

    xp = x_prompt.reshape(n_ctx * t_ctx, D_MODEL)
    xs = x_sample.reshape(n_lat * t_lat, D_MODEL)
    tiles_lat = t_lat // MIX_TILE

    x1c, h2c, affc, ckv_c, kpe_c = _ctx_front(
        xp, mods6, g1, win, qg, wuq_ctx, kvg, wk, wv, wcs, tq_ctx, tk_ctx,
        *dft_ctx, wo, g2, wr_t, n_req=n_ctx, t=t_ctx, mod_row=ctx_row)
    ql, kl, vl, zcl, zsl = _premix(
        xs, mods6, g1, win, qg, wuq_lat, kvg, wk, wv, wcs, tq_lat, tk_lat,
        mod_row=lambda i: i // tiles_lat, tab_row=lambda i: i % tiles_lat, rope=True)
    xk_cache = jnp.concatenate([cache_ckv[:, 0], cache_kpe[:, 0],
                                jnp.zeros((n_lat, past, 2 * LANES - KV_LORA - QK_ROPE), F32)],
                               axis=-1).reshape(n_lat * past, 2 * LANES).astype(BF16)
    kpast, vpast = _cachekv(xk_cache, wk, wv)

    attn_l = _attention(ql, [kpast, kl], [vpast, vl], n_req=n_lat, t_q=t_lat, kv_lens=[past, t_lat], tq=TOK_TILE,
                        pairs_per_step=2)

    x1l, h2l, affl = _mixout(xs, attn_l, zcl, zsl, *dft_lat, wo, mods6, g2, wr_t,
                             n_req=n_lat, t=t_lat, mod_row=lambda b: b)

    cap_c = CAP_FACTOR * t_ctx // N_EXPERTS
    cap_l = CAP_FACTOR * t_lat // N_EXPERTS
    affc2 = affc.reshape(n_ctx * N_EXPERTS, t_ctx)
    affl2 = affl.reshape(n_lat * N_EXPERTS, t_lat)
    posc, _ = _route(affc2, cap_c)
    posl, offl = _route(affl2, cap_l)
    assert t_lat // TOK_TILE + 1 <= OFF_STRIDE and cap_l % SLOT_WIN == 0
    offl = offl[:, :OFF_STRIDE].reshape(-1)
    xsc, gc = _gather(posc, affc2, h2c, n_req=n_ctx, n=t_ctx, cap=cap_c, rps=MOE_REQS)
    xsl, gl = _gather_win(offl, posl, affl2, h2l, n_req=n_lat, n=t_lat, cap=cap_l)
    ysc, ysl = _ffn(xsc, xsl, gc, gl, w_e_gate[0], w_e_up[0], w_e_down[0])

    posc_tok = posc.reshape(n_ctx, N_EXPERTS, t_ctx).transpose(0, 2, 1)
    posl_tok = posl.reshape(n_lat, N_EXPERTS, t_lat).transpose(0, 2, 1)
    y_prompt = _combine(x1c, posc_tok, ysc, mods6, fg, n_req=n_ctx, n=t_ctx, cap=cap_c, mod_row=ctx_row,
                        rps=MOE_REQS)
    y_sample = _combine_win(offl, x1l, posl_tok, ysl, mods6, fg, n_req=n_lat, n=t_lat, cap=cap_l,
                            mod_row=lambda b: b)

    return (y_prompt.reshape(n_ctx, t_ctx, D_MODEL), y_sample.reshape(n_lat, t_lat, D_MODEL),
            ckv_c.reshape(n_ctx, 1, t_ctx, KV_LORA), kpe_c.transpose(0, 2, 1).reshape(n_ctx, 1, t_ctx, QK_ROPE))
```

```python
import functools

import jax
import jax.numpy as jnp
import numpy as np
from jax import lax
from jax.experimental import pallas as pl
from jax.experimental.pallas import tpu as pltpu

F32 = jnp.float32
BF16 = jnp.bfloat16

D_MODEL = 1024
N_HEADS = 8
QK_NOPE = 64
QK_ROPE = 32
V_HEAD = 64
Q_LORA = 256
KV_LORA = 128
FNET_GROUPS = 8
FNET_CH = 64
FNET_W = FNET_GROUPS * FNET_CH
N_EXPERTS = 16
CAP_FACTOR = 2
D_EXPERT = 512
GRID_W = 64
ROPE_BASE = 10000.0
EPS = 1e-6

LANES = 128
SUBLANES = 8
HEAD_PAD = LANES
QK_W = N_HEADS * HEAD_PAD
V_W = N_HEADS * V_HEAD
TOK_TILE = 256
MIX_TILE = 512
FFN_ROWS = 512
MODS_TILE = 1536
CTX_REQS = 4
MOE_REQS = 4
KEY_CHUNK = 512
SLOT_WIN = 64
WIN_BLOCKS = 4
OFF_STRIDE = 16
VMEM_LIMIT = 48 * 1024 * 1024

_NT = (((1,), (1,)), ((), ()))


def _cparams(sem):
    return pltpu.CompilerParams(dimension_semantics=sem, vmem_limit_bytes=VMEM_LIMIT)


def _rms(x, g):
    return x * lax.rsqrt(jnp.mean(x * x, axis=-1, keepdims=True) + EPS) * g


def _dot(a, b):
    return jnp.dot(a, b, preferred_element_type=F32)


def _mods_kernel(c_ref, w_ref, b_ref, o_ref):
    c = c_ref[...]
    s = c * jax.nn.sigmoid(c)
    o_ref[...] = _dot(s.astype(BF16), w_ref[...].astype(BF16)) + b_ref[...]


def _mods(c8, w_mod, b_mod):
    n = w_mod.shape[1]
    tn = MODS_TILE
    return pl.pallas_call(
        _mods_kernel,
        grid=(n // tn,),
        in_specs=[pl.BlockSpec((SUBLANES, D_MODEL), lambda j: (0, 0)),
                  pl.BlockSpec((D_MODEL, tn), lambda j: (0, j)),
                  pl.BlockSpec((1, tn), lambda j: (0, j))],
        out_specs=pl.BlockSpec((SUBLANES, tn), lambda j: (0, j)),
        out_shape=jax.ShapeDtypeStruct((SUBLANES, n), F32),
        compiler_params=_cparams(("arbitrary",)),
        name="mods",
    )(c8, w_mod, b_mod)


def _fold_kernel(cc_ref, sc_ref, w_ref, cw_ref, sw_ref):
    for g in range(FNET_GROUPS):
        w = w_ref[g]
        cw_ref[g] = jnp.dot(cc_ref[...], w, preferred_element_type=F32, precision=lax.Precision.HIGHEST)
        sw_ref[g] = jnp.dot(sc_ref[...], w, preferred_element_type=F32, precision=lax.Precision.HIGHEST)


def _fold(cc, sc, w_fmix):
    shp = jax.ShapeDtypeStruct((FNET_GROUPS, FNET_CH, FNET_CH), F32)
    return pl.pallas_call(_fold_kernel, out_shape=(shp, shp), name="fold")(cc, sc, w_fmix)


KPE_LO = Q_LORA + KV_LORA
PROJ_W = 1024


def _winprep_kernel(wt_ref, o_ref):
    half = QK_ROPE // 2
    kpe_hi = KPE_LO + QK_ROPE
    o_ref[0:kpe_hi, :] = wt_ref[0:kpe_hi, :].astype(BF16)
    o_ref[kpe_hi:kpe_hi + half, :] = (-wt_ref[KPE_LO + half:kpe_hi, :]).astype(BF16)
    o_ref[kpe_hi + half:kpe_hi + QK_ROPE, :] = wt_ref[KPE_LO:KPE_LO + half, :].astype(BF16)
    o_ref[kpe_hi + QK_ROPE:PROJ_W - FNET_W, :] = jnp.zeros((PROJ_W - FNET_W - kpe_hi - QK_ROPE, D_MODEL), BF16)
    o_ref[PROJ_W - FNET_W:PROJ_W, :] = wt_ref[kpe_hi:kpe_hi + FNET_W, :].astype(BF16)


def _winprep(w_in_t):
    assert w_in_t.shape == (KPE_LO + QK_ROPE + FNET_W, D_MODEL)
    return pl.pallas_call(
        _winprep_kernel,
        out_shape=jax.ShapeDtypeStruct((PROJ_W, D_MODEL), BF16),
        compiler_params=pltpu.CompilerParams(vmem_limit_bytes=VMEM_LIMIT),
        name="winprep",
    )(w_in_t)


def _premix_body(x, m_ref, g1_ref, win_ref, qg_ref, wuq_ref, kvg_ref, wk_ref, wv_ref, wcs_ref, tq_ref, tk_ref, rope):
    shift1 = m_ref[0, 0:1, :]
    scale1 = m_ref[0, 1:2, :]
    h = _rms(x, g1_ref[...]) * (1.0 + scale1) + shift1
    proj = lax.dot_general(h.astype(BF16), win_ref[...], _NT, preferred_element_type=F32)
    qn = _rms(proj[:, 0:Q_LORA], qg_ref[...]).astype(BF16)
    qq = lax.dot_general(wuq_ref[...], qn, _NT, preferred_element_type=F32)
    cosq = tq_ref[0:LANES, :]
    sinq = tq_ref[LANES:2 * LANES, :]
    q_heads = []
    for hd in range(N_HEADS):
        lo = hd * HEAD_PAD
        qh = qq[lo:lo + HEAD_PAD, :] * cosq
        if rope:
            rot = qq[QK_W + hd * QK_ROPE:QK_W + (hd + 1) * QK_ROPE, :] * sinq[QK_NOPE:QK_NOPE + QK_ROPE, :]
            qh = qh + jnp.concatenate([jnp.zeros((QK_NOPE, rot.shape[1]), F32), rot,
                                       jnp.zeros((HEAD_PAD - QK_NOPE - QK_ROPE, rot.shape[1]), F32)], axis=0)
        q_heads.append(qh.astype(BF16))
    ckv = _rms(proj[:, Q_LORA:Q_LORA + KV_LORA], kvg_ref[...])
    kpe2 = proj[:, Q_LORA + KV_LORA:Q_LORA + KV_LORA + LANES] * tk_ref[...]
    xk = jnp.concatenate([ckv, kpe2], axis=1).astype(BF16)
    k = _dot(xk, wk_ref[...]).astype(BF16)
    v_t = lax.dot_general(wv_ref[...], xk[:, 0:KV_LORA], _NT, preferred_element_type=F32).astype(BF16)
    hw = FNET_W // 2
    f_in = proj[:, 512:1024].astype(BF16)
    z = [_dot(f_in[:, a * hw:(a + 1) * hw], wcs_ref[a * hw:(a + 1) * hw, :]) for a in range(2)]
    zc = jnp.concatenate([za[:, 0:hw] for za in z], axis=1).astype(BF16)
    zs = jnp.concatenate([za[:, hw:2 * hw] for za in z], axis=1).astype(BF16)
    kpe = proj[:, Q_LORA + KV_LORA:Q_LORA + KV_LORA + LANES]
    return q_heads, k, v_t, zc, zs, ckv, kpe


def _premix_kernel(x_ref, m_ref, g1_ref, win_ref, qg_ref, wuq_ref, kvg_ref, wk_ref, wv_ref, wcs_ref,
                   tq_ref, tk_ref, q_ref, k_ref, v_ref, zc_ref, zs_ref, *, rope):
    q_heads, k, v_t, zc, zs, _, _ = _premix_body(x_ref[...], m_ref, g1_ref, win_ref, qg_ref, wuq_ref, kvg_ref,
                                                 wk_ref, wv_ref, wcs_ref, tq_ref, tk_ref, rope)
    for hd, qh in enumerate(q_heads):
        q_ref[hd * HEAD_PAD:(hd + 1) * HEAD_PAD, :] = qh
    k_ref[...] = k
    v_ref[...] = v_t
    zc_ref[...] = zc
    zs_ref[...] = zs


def _premix(x, mods6, g1, win, qg, wuq, kvg, wk, wv, wcs, tq, tk, *, mod_row, tab_row, rope):
    n = x.shape[0]
    tm = MIX_TILE
    full = lambda a: pl.BlockSpec(a.shape, lambda i: (0,) * a.ndim)
    out_shape = [jax.ShapeDtypeStruct((QK_W, n), BF16), jax.ShapeDtypeStruct((n, QK_W), BF16),
                 jax.ShapeDtypeStruct((V_W, n), BF16), jax.ShapeDtypeStruct((n, FNET_W), BF16),
                 jax.ShapeDtypeStruct((n, FNET_W), BF16)]
    out_specs = [pl.BlockSpec((QK_W, tm), lambda i: (0, i)), pl.BlockSpec((tm, QK_W), lambda i: (i, 0)),
                 pl.BlockSpec((V_W, tm), lambda i: (0, i)), pl.BlockSpec((tm, FNET_W), lambda i: (i, 0)),
                 pl.BlockSpec((tm, FNET_W), lambda i: (i, 0))]
    return pl.pallas_call(
        functools.partial(_premix_kernel, rope=rope),
        grid=(n // tm,),
        in_specs=[pl.BlockSpec((tm, D_MODEL), lambda i: (i, 0)),
                  pl.BlockSpec((1, 6, D_MODEL), lambda i: (mod_row(i), 0, 0)),
                  full(g1), full(win), full(qg), full(wuq), full(kvg), full(wk), full(wv), full(wcs),
                  pl.BlockSpec((2 * LANES, tm), lambda i: (0, tab_row(i))),
                  pl.BlockSpec((tm, LANES), lambda i: (tab_row(i), 0))],
        out_specs=out_specs,
        out_shape=out_shape,
        compiler_params=_cparams(("parallel",)),
        name="premix",
    )(x, mods6, g1, win, qg, wuq, kvg, wk, wv, wcs, tq, tk)


def _cachekv_kernel(xk_ref, wk_ref, wv_ref, k_ref, v_ref):
    xk = xk_ref[...]
    k_ref[...] = _dot(xk, wk_ref[...]).astype(BF16)
    v_ref[...] = lax.dot_general(wv_ref[...], xk[:, 0:KV_LORA], _NT, preferred_element_type=F32).astype(BF16)


def _cachekv(xk, wk, wv):
    n = xk.shape[0]
    tm = 512
    full = lambda a: pl.BlockSpec(a.shape, lambda i: (0,) * a.ndim)
    return pl.pallas_call(
        _cachekv_kernel,
        grid=(n // tm,),
        in_specs=[pl.BlockSpec((tm, 2 * LANES), lambda i: (i, 0)), full(wk), full(wv)],
        out_specs=[pl.BlockSpec((tm, QK_W), lambda i: (i, 0)), pl.BlockSpec((V_W, tm), lambda i: (0, i))],
        out_shape=[jax.ShapeDtypeStruct((n, QK_W), BF16), jax.ShapeDtypeStruct((V_W, n), BF16)],
        compiler_params=_cparams(("parallel",)),
        name="cachekv",
    )(xk, wk, wv)


def _attn_body(q_heads, k_refs, v_refs, kc):
    tq = q_heads[0].shape[1]
    zero = jnp.zeros((HEAD_PAD, tq), BF16)
    n_pairs = len(q_heads) // 2
    qbd = [jnp.concatenate([jnp.concatenate([q_heads[2 * pr], zero], axis=1),
                            jnp.concatenate([zero, q_heads[2 * pr + 1]], axis=1)], axis=0) for pr in range(n_pairs)]
    chunks = [(k_ref, v_ref, c0, min(c0 + kc, k_ref.shape[0]))
              for k_ref, v_ref in zip(k_refs, v_refs) for c0 in range(0, k_ref.shape[0], kc)]
    work = [(pr, ch) for pr in range(n_pairs) for ch in chunks]

    def score(item):
        pr, (k_ref, _, c0, c1) = item
        return _dot(k_ref[c0:c1, pr * 2 * HEAD_PAD:(pr + 1) * 2 * HEAD_PAD], qbd[pr]).astype(BF16)

    m = [None] * n_pairs
    o = [None] * n_pairs
    s_next = score(work[0])
    for wi, (pr, (k_ref, v_ref, c0, c1)) in enumerate(work):
        s = s_next
        if wi + 1 < len(work):
            s_next = score(work[wi + 1])
        cm = jnp.max(s, axis=0, keepdims=True)
        vlo = pr * 2 * V_HEAD
        va = jnp.concatenate([v_ref[vlo:vlo + 2 * V_HEAD, c0:c1], jnp.ones((16, c1 - c0), BF16)], axis=0)
        if m[pr] is None:
            m[pr] = cm
            o[pr] = _dot(va, jnp.exp2(s - cm))
        else:
            m_new = jnp.maximum(m[pr], cm)
            alpha = jnp.exp2(m[pr].astype(F32) - m_new.astype(F32))
            o[pr] = alpha * o[pr] + _dot(va, jnp.exp2(s - m_new))
            m[pr] = m_new
    outs = []
    for pr in range(n_pairs):
        on = o[pr][0:2 * V_HEAD, :] * (1.0 / o[pr][2 * V_HEAD:2 * V_HEAD + 1, :])
        ot = jnp.concatenate([on[0:V_HEAD, 0:tq], on[V_HEAD:2 * V_HEAD, tq:2 * tq]], axis=0)
        outs.append(ot.T.astype(BF16))
    return outs


def _attn_kernel(q_ref, *refs, n_kv, n_pairs, kc):
    q_heads = [q_ref[hd * HEAD_PAD:(hd + 1) * HEAD_PAD, :] for hd in range(2 * n_pairs)]
    outs = _attn_body(q_heads, refs[:n_kv], refs[n_kv:2 * n_kv], kc)
    o_ref = refs[2 * n_kv]
    for pr, o in enumerate(outs):
        o_ref[:, pr * 2 * V_HEAD:(pr + 1) * 2 * V_HEAD] = o


def _attention(q_t, ks, vs_t, *, n_req, t_q, kv_lens, tq, pairs_per_step):
    n_kv = len(ks)
    nq = t_q // tq
    pp = pairs_per_step
    in_specs = [pl.BlockSpec((pp * 2 * HEAD_PAD, tq), lambda b, p, i: (p, b * nq + i))]
    in_specs += [pl.BlockSpec((kl, pp * 2 * HEAD_PAD), lambda b, p, i: (b, p)) for kl in kv_lens]
    in_specs += [pl.BlockSpec((pp * 2 * V_HEAD, kl), lambda b, p, i: (p, b)) for kl in kv_lens]
    return pl.pallas_call(
        functools.partial(_attn_kernel, n_kv=n_kv, n_pairs=pp, kc=KEY_CHUNK),
        grid=(n_req, N_HEADS // 2 // pp, nq),
        in_specs=in_specs,
        out_specs=pl.BlockSpec((tq, pp * 2 * V_HEAD), lambda b, p, i: (b * nq + i, p)),
        out_shape=jax.ShapeDtypeStruct((n_req * t_q, V_W), BF16),
        compiler_params=_cparams(("parallel", "parallel", "parallel")),
        name="attn",
    )(q_t, *ks, *vs_t)


def _mixout_body(x, attn_pairs, fm, wo_ref, m_ref, g2_ref, wr_ref):
    y = _dot(fm, wo_ref[V_W:V_W + FNET_W, :])
    col = 0
    for a in attn_pairs:
        y = y + _dot(a, wo_ref[col:col + a.shape[1], :])
        col += a.shape[1]
    gate1 = m_ref[0, 2:3, :]
    shift2 = m_ref[0, 3:4, :]
    scale2 = m_ref[0, 4:5, :]
    x1 = x + gate1 * y
    h2 = (_rms(x1, g2_ref[...]) * (1.0 + scale2) + shift2).astype(BF16)
    lg = lax.dot_general(wr_ref[...], h2, _NT, preferred_element_type=F32)
    e = jnp.exp(lg - jnp.max(lg, axis=0, keepdims=True))
    return x1, h2, e / jnp.sum(e, axis=0, keepdims=True)


def _mixout_kernel(x_ref, a_ref, zc_ref, zs_ref, cb_ref, sb_ref, off_ref, wo_ref, m_ref, g2_ref, wr_ref,
                   x1_ref, h2_ref, aff_ref, zp_ref, zm_ref, il_ref, *, t):
    half = t // 2
    i = pl.program_id(1)

    @pl.when(i == 0)
    def _fold_halves():
        for src, col in ((zc_ref, 0), (zs_ref, FNET_W)):
            lo = src[0:half, :]
            hi = src[half:t, :]
            zp_ref[:, col:col + FNET_W] = lo + hi
            zm_ref[:, col:col + FNET_W] = lo - hi

    off = off_ref[pl.ds(i, 1), :]
    co = off[:, 0:half]
    so = off[:, half:t]
    cb = cb_ref[...]
    sb = sb_ref[...]
    ct = (cb * co - sb * so).astype(BF16)
    st = (sb * co + cb * so).astype(BF16)
    tr = cb.shape[0]
    h = tr // 2
    even = _dot(ct[0:h, :], zp_ref[:, 0:FNET_W]) - _dot(st[0:h, :], zp_ref[:, FNET_W:2 * FNET_W])
    odd = _dot(ct[h:, :], zm_ref[:, 0:FNET_W]) - _dot(st[h:, :], zm_ref[:, FNET_W:2 * FNET_W])
    for c in range(FNET_W // LANES):
        il_ref[c, pl.ds(0, h, stride=2), :] = even[:, c * LANES:(c + 1) * LANES]
        il_ref[c, pl.ds(1, h, stride=2), :] = odd[:, c * LANES:(c + 1) * LANES]
    fm = jnp.concatenate([il_ref[c] for c in range(FNET_W // LANES)], axis=1).astype(BF16)
    x1, h2, aff = _mixout_body(x_ref[...], [a_ref[...]], fm, wo_ref, m_ref, g2_ref, wr_ref)
    x1_ref[...] = x1
    h2_ref[...] = h2
    aff_ref[...] = aff


def _ctx_front_kernel(x_ref, m_ref, g1_ref, win_ref, qg_ref, wuq_ref, kvg_ref, wk_ref, wv_ref, wcs_ref,
                      tq_ref, tk_ref, ct_ref, st_ref, wo_ref, g2_ref, wr_ref,
                      x1_ref, h2_ref, aff_ref, ckv_ref, kpe_ref, *, kc, t):
    x = x_ref[...]
    q_heads, k, v_t, zc, zs, ckv, kpe = _premix_body(x, m_ref, g1_ref, win_ref, qg_ref, wuq_ref, kvg_ref,
                                                     wk_ref, wv_ref, wcs_ref, tq_ref, tk_ref, False)
    ckv_ref[...] = ckv
    for r in range(x.shape[0] // t):
        kpe_ref[r] = kpe[r * t:(r + 1) * t, :].T[0:QK_ROPE, :]
    attn, fm = [], []
    for r in range(x.shape[0] // t):
        rows = slice(r * t, (r + 1) * t)
        attn.append(jnp.concatenate(_attn_body([q[:, rows] for q in q_heads], [k[rows, :]], [v_t[:, rows]], kc),
                                    axis=1))
        fm.append((_dot(ct_ref[...], zc[rows, :]) - _dot(st_ref[...], zs[rows, :])).astype(BF16))
    x1, h2, aff = _mixout_body(x_ref[...], [jnp.concatenate(attn, axis=0)], jnp.concatenate(fm, axis=0),
                               wo_ref, m_ref, g2_ref, wr_ref)
    x1_ref[...] = x1
    h2_ref[...] = h2
    for r in range(x.shape[0] // t):
        aff_ref[r] = aff[:, r * t:(r + 1) * t]


def _ctx_front(x, mods6, g1, win, qg, wuq, kvg, wk, wv, wcs, tq, tk, ct, st, wo, g2, wr_t, *, n_req, t, mod_row):
    full = lambda a: pl.BlockSpec(a.shape, lambda b: (0,) * a.ndim)
    rps = CTX_REQS
    row = lambda w: pl.BlockSpec((rps * t, w), lambda b: (b, 0))
    return pl.pallas_call(
        functools.partial(_ctx_front_kernel, kc=KEY_CHUNK, t=t),
        grid=(n_req // rps,),
        in_specs=[row(D_MODEL), pl.BlockSpec((1, 6, D_MODEL), lambda b: (mod_row, 0, 0)),
                  full(g1), full(win), full(qg), full(wuq), full(kvg), full(wk), full(wv), full(wcs),
                  full(tq), full(tk), full(ct), full(st), full(wo), full(g2), full(wr_t)],
        out_specs=[row(D_MODEL), row(D_MODEL), pl.BlockSpec((rps, N_EXPERTS, t), lambda b: (b, 0, 0)),
                   row(KV_LORA), pl.BlockSpec((rps, QK_ROPE, t), lambda b: (b, 0, 0))],
        out_shape=[jax.ShapeDtypeStruct((n_req * t, D_MODEL), F32),
                   jax.ShapeDtypeStruct((n_req * t, D_MODEL), BF16),
                   jax.ShapeDtypeStruct((n_req, N_EXPERTS, t), F32),
                   jax.ShapeDtypeStruct((n_req * t, KV_LORA), F32),
                   jax.ShapeDtypeStruct((n_req, QK_ROPE, t), F32)],
        compiler_params=_cparams(("parallel",)),
        name="ctx_front",
    )(x, mods6, g1, win, qg, wuq, kvg, wk, wv, wcs, tq, tk, ct, st, wo, g2, wr_t)


def _mixout(x, attn, zc, zs, cb, sb, off, wo, mods6, g2, wr_t, *, n_req, t, mod_row):
    tr = MIX_TILE
    nr = t // tr
    full = lambda a: pl.BlockSpec(a.shape, lambda b, i: (0,) * a.ndim)
    return pl.pallas_call(
        functools.partial(_mixout_kernel, t=t),
        grid=(n_req, nr),
        in_specs=[pl.BlockSpec((tr, D_MODEL), lambda b, i: (b * nr + i, 0)),
                  pl.BlockSpec((tr, V_W), lambda b, i: (b * nr + i, 0)),
                  pl.BlockSpec((t, FNET_W), lambda b, i: (b, 0)),
                  pl.BlockSpec((t, FNET_W), lambda b, i: (b, 0)),
                  full(cb), full(sb), full(off),
                  full(wo),
                  pl.BlockSpec((1, 6, D_MODEL), lambda b, i: (mod_row(b), 0, 0)),
                  full(g2), full(wr_t)],
        out_specs=[pl.BlockSpec((tr, D_MODEL), lambda b, i: (b * nr + i, 0)),
                   pl.BlockSpec((tr, D_MODEL), lambda b, i: (b * nr + i, 0)),
                   pl.BlockSpec((None, N_EXPERTS, tr), lambda b, i: (b, 0, i))],
        out_shape=[jax.ShapeDtypeStruct((n_req * t, D_MODEL), F32),
                   jax.ShapeDtypeStruct((n_req * t, D_MODEL), BF16),
                   jax.ShapeDtypeStruct((n_req, N_EXPERTS, t), F32)],
        scratch_shapes=[pltpu.VMEM((t // 2, 2 * FNET_W), BF16), pltpu.VMEM((t // 2, 2 * FNET_W), BF16),
                        pltpu.VMEM((FNET_W // LANES, tr, LANES), F32)],
        compiler_params=_cparams(("parallel", "arbitrary")),
        name="mixout",
    )(x, attn, zc, zs, cb, sb, off, wo, mods6, g2, wr_t)


def _prefix_count(flags, tri):
    n = flags.shape[1]
    carry = None
    outs = []
    ends = []
    for j in range(n // TOK_TILE):
        c = _dot(flags[:, j * TOK_TILE:(j + 1) * TOK_TILE].astype(BF16), tri)
        if carry is not None:
            c = c + carry
        outs.append(c)
        carry = c[:, TOK_TILE - 1:TOK_TILE]
        ends.append(carry)
    return (outs[0] if len(outs) == 1 else jnp.concatenate(outs, axis=1)), ends


def _route_kernel(aff_ref, pos_ref, off_ref, *, cap):
    a = aff_ref[...]
    rows = a.shape[0]
    capf = jnp.float32(cap)
    thr = jnp.zeros((rows, 1), jnp.int32)
    for bit in range(30, -1, -1):
        cand = thr | jnp.int32(1 << bit)
        cand_f = lax.bitcast_convert_type(cand, F32)
        cnt = jnp.sum(jnp.where(a >= cand_f, 1.0, 0.0), axis=1, keepdims=True)
        thr = jnp.where(cnt >= capf, cand, thr)
    thr_f = lax.bitcast_convert_type(thr, F32)
    above_f = lax.bitcast_convert_type(thr + 1, F32)
    gt = jnp.where(a >= above_f, 1.0, 0.0)
    tie = jnp.where(a >= thr_f, 1.0, 0.0) - gt
    need = capf - jnp.sum(gt, axis=1, keepdims=True)
    r_i = lax.broadcasted_iota(jnp.int32, (TOK_TILE, TOK_TILE), 0)
    c_i = lax.broadcasted_iota(jnp.int32, (TOK_TILE, TOK_TILE), 1)
    tri = jnp.where(r_i <= c_i, 1.0, 0.0).astype(BF16)
    tie_before = _prefix_count(tie, tri)[0] - tie
    sel = gt + tie * jnp.where(tie_before < need, 1.0, 0.0)
    count, ends = _prefix_count(sel, tri)
    pos_ref[...] = jnp.where(sel > 0.5, count - 1.0, -1.0)
    lane = lax.broadcasted_iota(jnp.int32, (rows, LANES), 1)
    offs = jnp.zeros((rows, LANES), F32)
    for j, end in enumerate(ends):
        offs = offs + jnp.where(lane == j + 1, end, 0.0)
    off_ref[...] = offs.astype(jnp.int32)


def _route(aff_t, cap):
    return pl.pallas_call(
        functools.partial(_route_kernel, cap=cap),
        out_shape=[jax.ShapeDtypeStruct(aff_t.shape, F32),
                   jax.ShapeDtypeStruct((aff_t.shape[0], LANES), jnp.int32)],
        compiler_params=pltpu.CompilerParams(vmem_limit_bytes=VMEM_LIMIT),
        name="route",
    )(aff_t)


def _gather_kernel(pos_ref, aff_ref, h_ref, xs_ref, g_ref, *, cap):
    rps, ne, n = pos_ref.shape
    slot = lax.broadcasted_iota(jnp.int32, (ne, cap, n), 1).astype(F32)
    for r in range(rps):
        pos = pos_ref[r]
        aff = aff_ref[r]
        hit = pos[:, None, :] == slot
        onehot = jnp.where(hit, 1.0, 0.0).reshape(ne * cap, n).astype(BF16)
        xs = _dot(onehot, h_ref[r * n:(r + 1) * n, :])
        xs_ref[:, r * cap:(r + 1) * cap, :] = xs.astype(BF16).reshape(ne, cap, D_MODEL)
        g_ref[:, r * cap:(r + 1) * cap, :] = jnp.sum(jnp.where(hit, aff[:, None, :], 0.0), axis=2, keepdims=True)


def _gather(pos_t, aff_t, h2, *, n_req, n, cap, rps):
    pos3 = pos_t.reshape(n_req, N_EXPERTS, n)
    aff3 = aff_t.reshape(n_req, N_EXPERTS, n)
    return pl.pallas_call(
        functools.partial(_gather_kernel, cap=cap),
        grid=(n_req // rps,),
        in_specs=[pl.BlockSpec((rps, N_EXPERTS, n), lambda b: (b, 0, 0)),
                  pl.BlockSpec((rps, N_EXPERTS, n), lambda b: (b, 0, 0)),
                  pl.BlockSpec((rps * n, D_MODEL), lambda b: (b, 0))],
        out_specs=[pl.BlockSpec((N_EXPERTS, rps * cap, D_MODEL), lambda b: (0, b, 0)),
                   pl.BlockSpec((N_EXPERTS, rps * cap, 1), lambda b: (0, b, 0))],
        out_shape=[jax.ShapeDtypeStruct((N_EXPERTS, n_req * cap, D_MODEL), BF16),
                   jax.ShapeDtypeStruct((N_EXPERTS, n_req * cap, 1), F32)],
        compiler_params=_cparams(("parallel",)),
        name="gather",
    )(pos3, aff3, h2)


def _window_plan(off_ref, b, j, cap, w):
    w0 = []
    need = jnp.int32(0)
    for e in range(N_EXPERTS):
        idx = (b * N_EXPERTS + e) * OFF_STRIDE + j
        base = (off_ref[idx] >> 4) << 4
        w0.append(base)
        need = jnp.maximum(need, off_ref[idx + 1] - base)
    return w0, (need + (w - 1)) >> (w.bit_length() - 1)


def _window(w0_e, p, cap, w):
    low = w0_e + p * w
    start = pl.multiple_of(jnp.minimum(low, cap - w), 16)
    return low, start


def _gather_win_kernel(off_ref, pos_ref, aff_ref, h_ref, xs_ref, g_ref, *, cap, w):
    b = pl.program_id(0)
    step = pl.program_id(1)

    @pl.when(step == 0)
    def _init():
        xs_ref[...] = jnp.zeros_like(xs_ref)
        g_ref[...] = jnp.zeros_like(g_ref)

    r = lax.broadcasted_iota(jnp.int32, (w, 1), 0).astype(F32)
    n_sub = pos_ref.shape[1] // TOK_TILE
    for sub in range(n_sub):
        cols = slice(sub * TOK_TILE, (sub + 1) * TOK_TILE)
        pos = pos_ref[:, cols]
        aff = aff_ref[:, cols]
        h = h_ref[cols, :]
        w0, n_pass = _window_plan(off_ref, b, step * n_sub + sub, cap, w)

        def one_pass(p, carry, pos=pos, aff=aff, h=h, w0=w0):
            starts, hots, gates = [], [], []
            for e in range(N_EXPERTS):
                low, start = _window(w0[e], p, cap, w)
                starts.append(start)
                mine = jnp.where(r >= (low - start).astype(F32), 1.0, 0.0)
                hot = jnp.where(pos[e:e + 1, :] - start.astype(F32) == r, mine, 0.0)
                hots.append(hot)
                gates.append(jnp.sum(hot * aff[e:e + 1, :], axis=1, keepdims=True))
            rows = _dot(jnp.concatenate(hots, axis=0).astype(BF16), h).astype(BF16)
            for e in range(N_EXPERTS):
                win = pl.ds(starts[e], w)
                xs_ref[e, win, :] = xs_ref[e, win, :] + rows[e * w:(e + 1) * w, :]
                g_ref[e, win, :] = g_ref[e, win, :] + gates[e]
            return carry

        one_pass(0, 0)
        lax.fori_loop(1, n_pass, one_pass, 0)


def _gather_win(offs, pos_t, aff_t, h2, *, n_req, n, cap):
    tb = WIN_BLOCKS * TOK_TILE
    nb = n // tb
    grid_spec = pltpu.PrefetchScalarGridSpec(
        num_scalar_prefetch=1,
        grid=(n_req, nb),
        in_specs=[pl.BlockSpec((N_EXPERTS, tb), lambda b, j, off: (b, j)),
                  pl.BlockSpec((N_EXPERTS, tb), lambda b, j, off: (b, j)),
                  pl.BlockSpec((tb, D_MODEL), lambda b, j, off: (b * nb + j, 0))],
        out_specs=[pl.BlockSpec((N_EXPERTS, cap, D_MODEL), lambda b, j, off: (0, b, 0)),
                   pl.BlockSpec((N_EXPERTS, cap, 1), lambda b, j, off: (0, b, 0))])
    return pl.pallas_call(
        functools.partial(_gather_win_kernel, cap=cap, w=SLOT_WIN),
        grid_spec=grid_spec,
        out_shape=[jax.ShapeDtypeStruct((N_EXPERTS, n_req * cap, D_MODEL), BF16),
                   jax.ShapeDtypeStruct((N_EXPERTS, n_req * cap, 1), F32)],
        compiler_params=_cparams(("parallel", "arbitrary")),
        name="gather_win",
    )(offs, pos_t, aff_t, h2)


def _ffn_kernel(xc_ref, xl_ref, gc_ref, gl_ref, wg_ref, wu_ref, wd_ref, yc_ref, yl_ref, wgb, wub, wdb):
    wgb[...] = wg_ref[0].astype(BF16)
    wub[...] = wu_ref[0].astype(BF16)
    wdb[...] = wd_ref[0].astype(BF16)
    for x_ref, g_ref, y_ref in ((xc_ref, gc_ref, yc_ref), (xl_ref, gl_ref, yl_ref)):
        for j in range(x_ref.shape[1] // FFN_ROWS):
            rows = slice(j * FFN_ROWS, (j + 1) * FFN_ROWS)
            x = x_ref[0, rows, :]
            gate = _dot(x, wgb[...])
            up = _dot(x, wub[...])
            hid = (gate * jax.nn.sigmoid(gate) * up).astype(BF16)
            ys = _dot(hid, wdb[...]) * g_ref[0, rows, :]
            y_ref[0, rows, :] = ys.astype(BF16)


def _ffn(xc, xl, gc, gl, wg, wu, wd):
    m = xc.shape[1]
    xspec = pl.BlockSpec((1, m, D_MODEL), lambda e: (e, 0, 0))
    gspec = pl.BlockSpec((1, m, 1), lambda e: (e, 0, 0))
    shp = jax.ShapeDtypeStruct((N_EXPERTS, m, D_MODEL), BF16)
    return pl.pallas_call(
        _ffn_kernel,
        grid=(N_EXPERTS,),
        in_specs=[xspec, xspec, gspec, gspec,
                  pl.BlockSpec((1, D_MODEL, D_EXPERT), lambda e: (e, 0, 0)),
                  pl.BlockSpec((1, D_MODEL, D_EXPERT), lambda e: (e, 0, 0)),
                  pl.BlockSpec((1, D_EXPERT, D_MODEL), lambda e: (e, 0, 0))],
        out_specs=[xspec, xspec],
        out_shape=[shp, shp],
        scratch_shapes=[pltpu.VMEM((D_MODEL, D_EXPERT), BF16), pltpu.VMEM((D_MODEL, D_EXPERT), BF16),
                        pltpu.VMEM((D_EXPERT, D_MODEL), BF16)],
        compiler_params=_cparams(("arbitrary",)),
        name="ffn",
    )(xc, xl, gc, gl, wg, wu, wd)


def _combine_kernel(x1_ref, pos_ref, ys_ref, m_ref, fg_ref, o_ref, *, cap):
    rps, n, _ = pos_ref.shape
    w = N_EXPERTS * cap
    e_i = lax.broadcasted_iota(jnp.int32, (N_EXPERTS, w), 0)
    j_i = lax.broadcasted_iota(jnp.int32, (N_EXPERTS, w), 1)
    spread = jnp.where((j_i >> (cap.bit_length() - 1)) == e_i, 1.0, 0.0).astype(BF16)
    lane_slot = (lax.broadcasted_iota(jnp.int32, (1, w), 1) & (cap - 1)).astype(F32)
    gate2 = m_ref[0, 5:6, :]
    for r in range(rps):
        pos = pos_ref[r].astype(BF16)
        onehot = jnp.where(_dot(pos, spread) == lane_slot, 1.0, 0.0).astype(BF16)
        acc = _dot(onehot, ys_ref[:, r * cap:(r + 1) * cap, :].reshape(w, D_MODEL))
        rows = slice(r * n, (r + 1) * n)
        o_ref[rows, :] = _rms(x1_ref[rows, :] + gate2 * acc, fg_ref[...])


def _combine(x1, pos_tok, ys, mods6, fg, *, n_req, n, cap, mod_row, rps):
    return pl.pallas_call(
        functools.partial(_combine_kernel, cap=cap),
        grid=(n_req // rps,),
        in_specs=[pl.BlockSpec((rps * n, D_MODEL), lambda b: (b, 0)),
                  pl.BlockSpec((rps, n, N_EXPERTS), lambda b: (b, 0, 0)),
                  pl.BlockSpec((N_EXPERTS, rps * cap, D_MODEL), lambda b: (0, b, 0)),
                  pl.BlockSpec((1, 6, D_MODEL), lambda b: (mod_row, 0, 0)),
                  pl.BlockSpec((1, D_MODEL), lambda b: (0, 0))],
        out_specs=pl.BlockSpec((rps * n, D_MODEL), lambda b: (b, 0)),
        out_shape=jax.ShapeDtypeStruct((n_req * n, D_MODEL), F32),
        compiler_params=_cparams(("parallel",)),
        name="combine",
    )(x1, pos_tok, ys, mods6, fg)


def _combine_win_kernel(off_ref, x1_ref, pos_ref, ys_ref, m_ref, fg_ref, o_ref, acc_ref, *, cap, w):
    b = pl.program_id(0)
    step = pl.program_id(1)
    width = N_EXPERTS * w
    e_i = lax.broadcasted_iota(jnp.int32, (N_EXPERTS, width), 0)
    j_i = lax.broadcasted_iota(jnp.int32, (N_EXPERTS, width), 1)
    spread = jnp.where((j_i >> (w.bit_length() - 1)) == e_i, 1.0, 0.0).astype(BF16)
    lane_slot = (lax.broadcasted_iota(jnp.int32, (1, width), 1) & (w - 1)).astype(F32)
    lane_e = lax.broadcasted_iota(jnp.int32, (1, N_EXPERTS), 1)
    n_sub = pos_ref.shape[0] // TOK_TILE
    for sub in range(n_sub):
        rows = slice(sub * TOK_TILE, (sub + 1) * TOK_TILE)
        pos = pos_ref[rows, :]
        w0, n_pass = _window_plan(off_ref, b, step * n_sub + sub, cap, w)

        def window_sum(p, pos=pos, w0=w0):
            start_row = jnp.zeros((1, N_EXPERTS), F32)
            first_row = jnp.zeros((1, N_EXPERTS), F32)
            wins = []
            for e in range(N_EXPERTS):
                low, start = _window(w0[e], p, cap, w)
                start_row = jnp.where(lane_e == e, start.astype(F32), start_row)
                first_row = jnp.where(lane_e == e, (low - start).astype(F32), first_row)
                wins.append(ys_ref[e, pl.ds(start, w), :])
            rel = pos - start_row
            rel = jnp.where(rel >= first_row, rel, -1.0).astype(BF16)
            onehot = jnp.where(_dot(rel, spread) == lane_slot, 1.0, 0.0).astype(BF16)
            return _dot(onehot, jnp.concatenate(wins, axis=0))

        def finish(acc, rows=rows):
            o_ref[rows, :] = _rms(x1_ref[rows, :] + m_ref[0, 5:6, :] * acc, fg_ref[...])

        acc0 = window_sum(0)
        acc_ref[sub] = acc0
        finish(acc0)

        @pl.when(n_pass > 1)
        def _more_passes(sub=sub, n_pass=n_pass, window_sum=window_sum, finish=finish):
            def one_pass(p, carry):
                acc_ref[sub] += window_sum(p)
                return carry

            lax.fori_loop(1, n_pass, one_pass, 0)
            finish(acc_ref[sub])


def _combine_win(offs, x1, pos_tok, ys, mods6, fg, *, n_req, n, cap, mod_row):
    tr = WIN_BLOCKS * TOK_TILE
    nr = n // tr
    grid_spec = pltpu.PrefetchScalarGridSpec(
        num_scalar_prefetch=1,
        grid=(n_req, nr),
        in_specs=[pl.BlockSpec((tr, D_MODEL), lambda b, i, off: (b * nr + i, 0)),
                  pl.BlockSpec((None, tr, N_EXPERTS), lambda b, i, off: (b, i, 0)),
                  pl.BlockSpec((N_EXPERTS, cap, D_MODEL), lambda b, i, off: (0, b, 0)),
                  pl.BlockSpec((1, 6, D_MODEL), lambda b, i, off: (mod_row(b), 0, 0)),
                  pl.BlockSpec((1, D_MODEL), lambda b, i, off: (0, 0))],
        out_specs=pl.BlockSpec((tr, D_MODEL), lambda b, i, off: (b * nr + i, 0)),
        scratch_shapes=[pltpu.VMEM((WIN_BLOCKS, TOK_TILE, D_MODEL), F32)])
    return pl.pallas_call(
        functools.partial(_combine_win_kernel, cap=cap, w=SLOT_WIN),
        grid_spec=grid_spec,
        out_shape=jax.ShapeDtypeStruct((n_req * n, D_MODEL), F32),
        compiler_params=_cparams(("parallel", "parallel")),
        name="combine_win",
    )(offs, x1, pos_tok, ys, mods6, fg)


def _rot_half(w):
    half = QK_ROPE // 2
    return jnp.concatenate([-w[..., half:], w[..., :half]], axis=-1)


def _rope_tables(t):
    n_rows = t // GRID_W
    rows = np.repeat(np.arange(n_rows, dtype=np.float64), GRID_W)
    cols = np.tile(np.arange(GRID_W, dtype=np.float64), n_rows)
    n_freq = QK_ROPE // 4
    inv_freq = ROPE_BASE ** (-np.arange(n_freq, dtype=np.float64) / n_freq)
    ang = np.concatenate([rows[:, None] * inv_freq, cols[:, None] * inv_freq], axis=-1)
    cos = np.concatenate([np.cos(ang), np.cos(ang)], axis=-1)
    sin = np.concatenate([np.sin(ang), np.sin(ang)], axis=-1)
    return cos, sin


def _qk_tables(cos, sin):
    t = cos.shape[0]
    scale = (QK_NOPE + QK_ROPE) ** -0.5 * np.log2(np.e)
    pad = np.zeros((t, HEAD_PAD - QK_NOPE - QK_ROPE))
    cosq = np.concatenate([np.full((t, QK_NOPE), scale), cos * scale, pad], axis=1)
    sinq = np.concatenate([np.zeros((t, QK_NOPE)), sin * scale, pad], axis=1)
    tq_t = np.concatenate([cosq, sinq], axis=1).T
    tk = np.concatenate([cos, sin, np.zeros((t, LANES - 2 * QK_ROPE))], axis=1)
    return jnp.asarray(tq_t, F32), jnp.asarray(tk, F32)


def _dft_angles(rows, t):
    k = np.arange(t, dtype=np.int64)
    return ((rows[:, None] * k[None, :]) % t).astype(np.float64) * (2.0 * np.pi / t)


def _dft_tables(t):
    ang = _dft_angles(np.arange(t, dtype=np.int64), t)
    scale = (t * FNET_CH) ** -0.5
    return jnp.asarray(np.cos(ang) * scale, F32).astype(BF16), jnp.asarray(np.sin(ang) * scale, F32).astype(BF16)


def _dft_half_tables(t):
    r = np.arange(MIX_TILE, dtype=np.int64)
    ang = _dft_angles(np.concatenate([r[0::2], r[1::2]]), t)[:, :t // 2]
    scale = (t * FNET_CH) ** -0.5
    ang_off = _dft_angles(np.arange(t // MIX_TILE, dtype=np.int64) * MIX_TILE, t)[:, :t // 2]
    off = np.concatenate([np.cos(ang_off), np.sin(ang_off)], axis=1)
    return jnp.asarray(np.cos(ang) * scale, F32), jnp.asarray(np.sin(ang) * scale, F32), jnp.asarray(off, F32)


def _block_diag(w):
    g, a, b = w.shape
    eye = jnp.eye(g, dtype=w.dtype)
    return (eye[:, None, :, None] * w[:, :, None, :]).reshape(g * a, g * b)


def kernel(x_prompt, x_sample, cache_ckv, cache_kpe, c, c_ctx, w_mod, b_mod, norm1_g, w_in, q_norm_g, w_uq,
           kv_norm_g, w_ukv, w_fmix, w_out, norm2_g, w_router, w_e_gate, w_e_up, w_e_down, final_g):
    assert w_mod.shape[0] == 1, "single-layer problem"
    n_ctx, t_ctx, _ = x_prompt.shape
    n_lat, t_lat, _ = x_sample.shape
    past = cache_ckv.shape[2]
    ctx_row = n_lat

    win = _winprep(jnp.swapaxes(w_in, 1, 2)[0])
    wq3 = w_uq[0].reshape(Q_LORA, N_HEADS, QK_NOPE + QK_ROPE)
    qpad = jnp.zeros((Q_LORA, N_HEADS, HEAD_PAD - QK_NOPE - QK_ROPE), F32)
    wuq_main = jnp.concatenate([wq3, qpad], axis=2).reshape(Q_LORA, QK_W)
    wuq_rot = _rot_half(wq3[..., QK_NOPE:]).reshape(Q_LORA, N_HEADS * QK_ROPE)
    wuq_lat = jnp.concatenate([wuq_main, wuq_rot], axis=1).T.astype(BF16)
    wuq_ctx = wuq_main.T.astype(BF16)
    wkv3 = w_ukv[0].reshape(KV_LORA, N_HEADS, QK_NOPE + V_HEAD)
    wk_top = jnp.concatenate([wkv3[..., :QK_NOPE], jnp.zeros((KV_LORA, N_HEADS, HEAD_PAD - QK_NOPE), F32)],
                             axis=2).reshape(KV_LORA, QK_W)
    place = jnp.concatenate([jnp.zeros((QK_ROPE, QK_NOPE), F32), jnp.eye(QK_ROPE, dtype=F32),
                             jnp.zeros((QK_ROPE, HEAD_PAD - QK_NOPE - QK_ROPE), F32)], axis=1)
    place = jnp.tile(place, (1, N_HEADS))
    wk = jnp.concatenate([wk_top, place, place, jnp.zeros((LANES - 2 * QK_ROPE, QK_W), F32)], axis=0).astype(BF16)
    wv = wkv3[..., QK_NOPE:].reshape(KV_LORA, V_W).T.astype(BF16)
    wo = w_out[0].astype(BF16)
    wr_t = w_router[0].T.astype(BF16)

    cos, sin = _rope_tables(t_lat)
    tq_lat, tk_lat = _qk_tables(cos, sin)
    assert n_ctx % CTX_REQS == 0
    tq_ctx, tk_ctx = _qk_tables(np.ones((CTX_REQS * t_ctx, QK_ROPE)), np.zeros((CTX_REQS * t_ctx, QK_ROPE)))
    ch_ang = _dft_angles(np.arange(FNET_CH, dtype=np.int64), FNET_CH)
    dft_ctx = _dft_tables(t_ctx)
    dft_lat = _dft_half_tables(t_lat)

    assert n_lat + 1 <= SUBLANES
    c8 = jnp.concatenate([c, c_ctx[None, :], jnp.zeros((SUBLANES - n_lat - 1, D_MODEL), F32)], axis=0)
    mods6 = _mods(c8, w_mod[0], b_mod[0][None, :]).reshape(SUBLANES, 6, D_MODEL)
    cw, sw = _fold(jnp.asarray(np.cos(ch_ang), F32), jnp.asarray(np.sin(ch_ang), F32), w_fmix[0])
    gh = FNET_GROUPS // 2
    wcs = jnp.concatenate([jnp.concatenate([_block_diag(cw[a * gh:(a + 1) * gh]), _block_diag(sw[a * gh:(a + 1) * gh])],
                                           axis=1) for a in range(2)], axis=0).astype(BF16)

    g1 = norm1_g[0][None, :]
    qg = q_norm_g[0][None, :]
    kvg = kv_norm_g[0][None, :]
    g2 = norm2_g[0][None, :]
    fg = final_g[None, :]

    xp = x_prompt.reshape(n_ctx * t_ctx, D_MODEL)
    xs = x_sample.reshape(n_lat * t_lat, D_MODEL)
    tiles_lat = t_lat // MIX_TILE

    x1c, h2c, affc, ckv_c, kpe_c = _ctx_front(
        xp, mods6, g1, win, qg, wuq_ctx, kvg, wk, wv, wcs, tq_ctx, tk_ctx,
        *dft_ctx, wo, g2, wr_t, n_req=n_ctx, t=t_ctx, mod_row=ctx_row)
    ql, kl, vl, zcl, zsl = _premix(
        xs, mods6, g1, win, qg, wuq_lat, kvg, wk, wv, wcs, tq_lat, tk_lat,
        mod_row=lambda i: i // tiles_lat, tab_row=lambda i: i % tiles_lat, rope=True)
    xk_cache = jnp.concatenate([cache_ckv[:, 0], cache_kpe[:, 0],
                                jnp.zeros((n_lat, past, 2 * LANES - KV_LORA - QK_ROPE), F32)],
                               axis=-1).reshape(n_lat * past, 2 * LANES).astype(BF16)
    kpast, vpast = _cachekv(xk_cache, wk, wv)

    attn_l = _attention(ql, [kpast, kl], [vpast, vl], n_req=n_lat, t_q=t_lat, kv_lens=[past, t_lat], tq=TOK_TILE,
                        pairs_per_step=2)

    x1l, h2l, affl = _mixout(xs, attn_l, zcl, zsl, *dft_lat, wo, mods6, g2, wr_t,
                             n_req=n_lat, t=t_lat, mod_row=lambda b: b)

    cap_c = CAP_FACTOR * t_ctx // N_EXPERTS
    cap_l = CAP_FACTOR * t_lat // N_EXPERTS
    affc2 = affc.reshape(n_ctx * N_EXPERTS, t_ctx)
    affl2 = affl.reshape(n_lat * N_EXPERTS, t_lat)
    posc, _ = _route(affc2, cap_c)
    posl, offl = _route(affl2, cap_l)
    assert t_lat // TOK_TILE + 1 <= OFF_STRIDE and cap_l % SLOT_WIN == 0
    offl = offl[:, :OFF_STRIDE].reshape(-1)
    xsc, gc = _gather(posc, affc2, h2c, n_req=n_ctx, n=t_ctx, cap=cap_c, rps=MOE_REQS)
    xsl, gl = _gather_win(offl, posl, affl2, h2l, n_req=n_lat, n=t_lat, cap=cap_l)
    ysc, ysl = _ffn(xsc, xsl, gc, gl, w_e_gate[0], w_e_up[0], w_e_down[0])

    posc_tok = posc.reshape(n_ctx, N_EXPERTS, t_ctx).transpose(0, 2, 1)
    posl_tok = posl.reshape(n_lat, N_EXPERTS, t_lat).transpose(0, 2, 1)
    y_prompt = _combine(x1c, posc_tok, ysc, mods6, fg, n_req=n_ctx, n=t_ctx, cap=cap_c, mod_row=ctx_row,
                        rps=MOE_REQS)
    y_sample = _combine_win(offl, x1l, posl_tok, ysl, mods6, fg, n_req=n_lat, n=t_lat, cap=cap_l,
                            mod_row=lambda b: b)

    return (y_prompt.reshape(n_ctx, t_ctx, D_MODEL), y_sample.reshape(n_lat, t_lat, D_MODEL),
            ckv_c.reshape(n_ctx, 1, t_ctx, KV_LORA), kpe_c.transpose(0, 2, 1).reshape(n_ctx, 1, t_ctx, QK_ROPE))
```

```python
import functools

import jax
import jax.numpy as jnp
import numpy as np
from jax import lax
from jax.experimental import pallas as pl
from jax.experimental.pallas import tpu as pltpu

F32 = jnp.float32
BF16 = jnp.bfloat16

D_MODEL = 1024
N_HEADS = 8
QK_NOPE = 64
QK_ROPE = 32
V_HEAD = 64
Q_LORA = 256
KV_LORA = 128
FNET_GROUPS = 8
FNET_CH = 64
FNET_W = FNET_GROUPS * FNET_CH
N_EXPERTS = 16
CAP_FACTOR = 2
D_EXPERT = 512
GRID_W = 64
ROPE_BASE = 10000.0
EPS = 1e-6

LANES = 128
SUBLANES = 8
HEAD_PAD = LANES
QK_W = N_HEADS * HEAD_PAD
V_W = N_HEADS * V_HEAD
TOK_TILE = 256
PRE_TILE = 1024
MIX_TILE = 512
FFN_ROWS = 1024
MODS_TILE = 1536
CTX_REQS = 4
MOE_REQS = 4
KEY_CHUNK = 512
SLOT_WIN = 64
WIN_BLOCKS = 4
OFF_STRIDE = 16
VMEM_LIMIT = 48 * 1024 * 1024

_NT = (((1,), (1,)), ((), ()))


def _cparams(sem):
    return pltpu.CompilerParams(dimension_semantics=sem, vmem_limit_bytes=VMEM_LIMIT)


def _rms(x, g):
    return x * lax.rsqrt(jnp.mean(x * x, axis=-1, keepdims=True) + EPS) * g


def _dot(a, b):
    return jnp.dot(a, b, preferred_element_type=F32)


def _mods_kernel(c_ref, w_ref, b_ref, o_ref):
    c = c_ref[...]
    s = c * jax.nn.sigmoid(c)
    o_ref[...] = _dot(s.astype(BF16), w_ref[...].astype(BF16)) + b_ref[...]


def _mods(c8, w_mod, b_mod):
    n = w_mod.shape[1]
    tn = MODS_TILE
    return pl.pallas_call(
        _mods_kernel,
        grid=(n // tn,),
        in_specs=[pl.BlockSpec((SUBLANES, D_MODEL), lambda j: (0, 0)),
                  pl.BlockSpec((D_MODEL, tn), lambda j: (0, j)),
                  pl.BlockSpec((1, tn), lambda j: (0, j))],
        out_specs=pl.BlockSpec((SUBLANES, tn), lambda j: (0, j)),
        out_shape=jax.ShapeDtypeStruct((SUBLANES, n), F32),
        compiler_params=_cparams(("arbitrary",)),
        name="mods",
    )(c8, w_mod, b_mod)


def _fold_kernel(cc_ref, sc_ref, w_ref, cw_ref, sw_ref):
    for g in range(FNET_GROUPS):
        w = w_ref[g]
        cw_ref[g] = jnp.dot(cc_ref[...], w, preferred_element_type=F32, precision=lax.Precision.HIGHEST)
        sw_ref[g] = jnp.dot(sc_ref[...], w, preferred_element_type=F32, precision=lax.Precision.HIGHEST)


def _fold(cc, sc, w_fmix):
    shp = jax.ShapeDtypeStruct((FNET_GROUPS, FNET_CH, FNET_CH), F32)
    return pl.pallas_call(_fold_kernel, out_shape=(shp, shp), name="fold")(cc, sc, w_fmix)


KPE_LO = Q_LORA + KV_LORA
PROJ_W = 1024


def _winprep_kernel(wt_ref, o_ref):
    half = QK_ROPE // 2
    kpe_hi = KPE_LO + QK_ROPE
    o_ref[0:kpe_hi, :] = wt_ref[0:kpe_hi, :].astype(BF16)
    o_ref[kpe_hi:kpe_hi + half, :] = (-wt_ref[KPE_LO + half:kpe_hi, :]).astype(BF16)
    o_ref[kpe_hi + half:kpe_hi + QK_ROPE, :] = wt_ref[KPE_LO:KPE_LO + half, :].astype(BF16)
    o_ref[kpe_hi + QK_ROPE:PROJ_W - FNET_W, :] = jnp.zeros((PROJ_W - FNET_W - kpe_hi - QK_ROPE, D_MODEL), BF16)
    o_ref[PROJ_W - FNET_W:PROJ_W, :] = wt_ref[kpe_hi:kpe_hi + FNET_W, :].astype(BF16)


def _winprep(w_in_t):
    assert w_in_t.shape == (KPE_LO + QK_ROPE + FNET_W, D_MODEL)
    return pl.pallas_call(
        _winprep_kernel,
        out_shape=jax.ShapeDtypeStruct((PROJ_W, D_MODEL), BF16),
        compiler_params=pltpu.CompilerParams(vmem_limit_bytes=VMEM_LIMIT),
        name="winprep",
    )(w_in_t)


def _premix_body(x, m_ref, g1_ref, win_ref, qg_ref, wuq_ref, kvg_ref, wk_ref, wv_ref, wcs_ref, tq_ref, tk_ref, rope):
    shift1 = m_ref[0, 0:1, :]
    scale1 = m_ref[0, 1:2, :]
    h = _rms(x, g1_ref[...]) * (1.0 + scale1) + shift1
    proj = lax.dot_general(h.astype(BF16), win_ref[...], _NT, preferred_element_type=F32)
    qn = _rms(proj[:, 0:Q_LORA], qg_ref[...]).astype(BF16)
    qq = lax.dot_general(wuq_ref[...], qn, _NT, preferred_element_type=F32)
    cosq = tq_ref[0:LANES, :]
    sinq = tq_ref[LANES:2 * LANES, :]
    q_heads = []
    for hd in range(N_HEADS):
        lo = hd * HEAD_PAD
        qh = qq[lo:lo + HEAD_PAD, :] * cosq
        if rope:
            rot = qq[QK_W + hd * QK_ROPE:QK_W + (hd + 1) * QK_ROPE, :] * sinq[QK_NOPE:QK_NOPE + QK_ROPE, :]
            qh = qh + jnp.concatenate([jnp.zeros((QK_NOPE, rot.shape[1]), F32), rot,
                                       jnp.zeros((HEAD_PAD - QK_NOPE - QK_ROPE, rot.shape[1]), F32)], axis=0)
        q_heads.append(qh.astype(BF16))
    ckv = _rms(proj[:, Q_LORA:Q_LORA + KV_LORA], kvg_ref[...])
    kpe2 = proj[:, Q_LORA + KV_LORA:Q_LORA + KV_LORA + LANES] * tk_ref[...]
    xk = jnp.concatenate([ckv, kpe2], axis=1).astype(BF16)
    k = _dot(xk, wk_ref[...]).astype(BF16)
    v_t = lax.dot_general(wv_ref[...], xk[:, 0:KV_LORA], _NT, preferred_element_type=F32).astype(BF16)
    hw = FNET_W // 2
    f_in = proj[:, 512:1024].astype(BF16)
    z = [_dot(f_in[:, a * hw:(a + 1) * hw], wcs_ref[a * hw:(a + 1) * hw, :]) for a in range(2)]
    zc = jnp.concatenate([za[:, 0:hw] for za in z], axis=1).astype(BF16)
    zs = jnp.concatenate([za[:, hw:2 * hw] for za in z], axis=1).astype(BF16)
    kpe = proj[:, Q_LORA + KV_LORA:Q_LORA + KV_LORA + LANES]
    return q_heads, k, v_t, zc, zs, ckv, kpe


def _premix_kernel(x_ref, m_ref, g1_ref, win_ref, qg_ref, wuq_ref, kvg_ref, wk_ref, wv_ref, wcs_ref,
                   tq_ref, tk_ref, q_ref, k_ref, v_ref, zc_ref, zs_ref, *, rope):
    q_heads, k, v_t, zc, zs, _, _ = _premix_body(x_ref[...], m_ref, g1_ref, win_ref, qg_ref, wuq_ref, kvg_ref,
                                                 wk_ref, wv_ref, wcs_ref, tq_ref, tk_ref, rope)
    for hd, qh in enumerate(q_heads):
        q_ref[hd * HEAD_PAD:(hd + 1) * HEAD_PAD, :] = qh
    k_ref[...] = k
    v_ref[...] = v_t
    zc_ref[...] = zc
    zs_ref[...] = zs


def _premix(x, mods6, g1, win, qg, wuq, kvg, wk, wv, wcs, tq, tk, *, mod_row, tab_row, rope):
    n = x.shape[0]
    tm = PRE_TILE
    full = lambda a: pl.BlockSpec(a.shape, lambda i: (0,) * a.ndim)
    out_shape = [jax.ShapeDtypeStruct((QK_W, n), BF16), jax.ShapeDtypeStruct((n, QK_W), BF16),
                 jax.ShapeDtypeStruct((V_W, n), BF16), jax.ShapeDtypeStruct((n, FNET_W), BF16),
                 jax.ShapeDtypeStruct((n, FNET_W), BF16)]
    out_specs = [pl.BlockSpec((QK_W, tm), lambda i: (0, i)), pl.BlockSpec((tm, QK_W), lambda i: (i, 0)),
                 pl.BlockSpec((V_W, tm), lambda i: (0, i)), pl.BlockSpec((tm, FNET_W), lambda i: (i, 0)),
                 pl.BlockSpec((tm, FNET_W), lambda i: (i, 0))]
    return pl.pallas_call(
        functools.partial(_premix_kernel, rope=rope),
        grid=(n // tm,),
        in_specs=[pl.BlockSpec((tm, D_MODEL), lambda i: (i, 0)),
                  pl.BlockSpec((1, 6, D_MODEL), lambda i: (mod_row(i), 0, 0)),
                  full(g1), full(win), full(qg), full(wuq), full(kvg), full(wk), full(wv), full(wcs),
                  pl.BlockSpec((2 * LANES, tm), lambda i: (0, tab_row(i))),
                  pl.BlockSpec((tm, LANES), lambda i: (tab_row(i), 0))],
        out_specs=out_specs,
        out_shape=out_shape,
        compiler_params=_cparams(("parallel",)),
        name="premix",
    )(x, mods6, g1, win, qg, wuq, kvg, wk, wv, wcs, tq, tk)


def _cachekv_kernel(xk_ref, wk_ref, wv_ref, k_ref, v_ref):
    xk = xk_ref[...]
    k_ref[...] = _dot(xk, wk_ref[...]).astype(BF16)
    v_ref[...] = lax.dot_general(wv_ref[...], xk[:, 0:KV_LORA], _NT, preferred_element_type=F32).astype(BF16)


def _cachekv(xk, wk, wv):
    n = xk.shape[0]
    tm = 512
    full = lambda a: pl.BlockSpec(a.shape, lambda i: (0,) * a.ndim)
    return pl.pallas_call(
        _cachekv_kernel,
        grid=(n // tm,),
        in_specs=[pl.BlockSpec((tm, 2 * LANES), lambda i: (i, 0)), full(wk), full(wv)],
        out_specs=[pl.BlockSpec((tm, QK_W), lambda i: (i, 0)), pl.BlockSpec((V_W, tm), lambda i: (0, i))],
        out_shape=[jax.ShapeDtypeStruct((n, QK_W), BF16), jax.ShapeDtypeStruct((V_W, n), BF16)],
        compiler_params=_cparams(("parallel",)),
        name="cachekv",
    )(xk, wk, wv)


def _attn_body(q_heads, k_refs, v_refs, kc):
    tq = q_heads[0].shape[1]
    zero = jnp.zeros((HEAD_PAD, tq), BF16)
    n_pairs = len(q_heads) // 2
    qbd = [jnp.concatenate([jnp.concatenate([q_heads[2 * pr], zero], axis=1),
                            jnp.concatenate([zero, q_heads[2 * pr + 1]], axis=1)], axis=0) for pr in range(n_pairs)]
    chunks = [(k_ref, v_ref, c0, min(c0 + kc, k_ref.shape[0]))
              for k_ref, v_ref in zip(k_refs, v_refs) for c0 in range(0, k_ref.shape[0], kc)]
    work = [(pr, ch) for pr in range(n_pairs) for ch in chunks]

    def score(item):
        pr, (k_ref, _, c0, c1) = item
        return _dot(k_ref[c0:c1, pr * 2 * HEAD_PAD:(pr + 1) * 2 * HEAD_PAD], qbd[pr]).astype(BF16)

    m = [None] * n_pairs
    o = [None] * n_pairs
    s_next = score(work[0])
    for wi, (pr, (k_ref, v_ref, c0, c1)) in enumerate(work):
        s = s_next
        if wi + 1 < len(work):
            s_next = score(work[wi + 1])
        cm = jnp.max(s, axis=0, keepdims=True)
        vlo = pr * 2 * V_HEAD
        va = jnp.concatenate([v_ref[vlo:vlo + 2 * V_HEAD, c0:c1], jnp.ones((16, c1 - c0), BF16)], axis=0)
        if m[pr] is None:
            m[pr] = cm
            o[pr] = _dot(va, jnp.exp2(s - cm))
        else:
            m_new = jnp.maximum(m[pr], cm)
            alpha = jnp.exp2(m[pr].astype(F32) - m_new.astype(F32))
            o[pr] = alpha * o[pr] + _dot(va, jnp.exp2(s - m_new))
            m[pr] = m_new
    outs = []
    for pr in range(n_pairs):
        on = o[pr][0:2 * V_HEAD, :] * (1.0 / o[pr][2 * V_HEAD:2 * V_HEAD + 1, :])
        ot = jnp.concatenate([on[0:V_HEAD, 0:tq], on[V_HEAD:2 * V_HEAD, tq:2 * tq]], axis=0)
        outs.append(ot.T.astype(BF16))
    return outs


def _attn_kernel(q_ref, *refs, n_kv, n_pairs, kc):
    q_heads = [q_ref[hd * HEAD_PAD:(hd + 1) * HEAD_PAD, :] for hd in range(2 * n_pairs)]
    outs = _attn_body(q_heads, refs[:n_kv], refs[n_kv:2 * n_kv], kc)
    o_ref = refs[2 * n_kv]
    for pr, o in enumerate(outs):
        o_ref[:, pr * 2 * V_HEAD:(pr + 1) * 2 * V_HEAD] = o


def _attention(q_t, ks, vs_t, *, n_req, t_q, kv_lens, tq, pairs_per_step):
    n_kv = len(ks)
    nq = t_q // tq
    pp = pairs_per_step
    in_specs = [pl.BlockSpec((pp * 2 * HEAD_PAD, tq), lambda b, p, i: (p, b * nq + i))]
    in_specs += [pl.BlockSpec((kl, pp * 2 * HEAD_PAD), lambda b, p, i: (b, p)) for kl in kv_lens]
    in_specs += [pl.BlockSpec((pp * 2 * V_HEAD, kl), lambda b, p, i: (p, b)) for kl in kv_lens]
    return pl.pallas_call(
        functools.partial(_attn_kernel, n_kv=n_kv, n_pairs=pp, kc=KEY_CHUNK),
        grid=(n_req, N_HEADS // 2 // pp, nq),
        in_specs=in_specs,
        out_specs=pl.BlockSpec((tq, pp * 2 * V_HEAD), lambda b, p, i: (b * nq + i, p)),
        out_shape=jax.ShapeDtypeStruct((n_req * t_q, V_W), BF16),
        compiler_params=_cparams(("parallel", "parallel", "parallel")),
        name="attn",
    )(q_t, *ks, *vs_t)


def _mixout_body(x, attn_pairs, fm, wo_ref, m_ref, g2_ref, wr_ref):
    y = _dot(fm, wo_ref[V_W:V_W + FNET_W, :])
    col = 0
    for a in attn_pairs:
        y = y + _dot(a, wo_ref[col:col + a.shape[1], :])
        col += a.shape[1]
    gate1 = m_ref[0, 2:3, :]
    shift2 = m_ref[0, 3:4, :]
    scale2 = m_ref[0, 4:5, :]
    x1 = x + gate1 * y
    h2 = (_rms(x1, g2_ref[...]) * (1.0 + scale2) + shift2).astype(BF16)
    lg = lax.dot_general(wr_ref[...], h2, _NT, preferred_element_type=F32)
    e = jnp.exp(lg - jnp.max(lg, axis=0, keepdims=True))
    return x1, h2, e / jnp.sum(e, axis=0, keepdims=True)


def _mixout_kernel(x_ref, a_ref, zc_ref, zs_ref, cb_ref, sb_ref, off_ref, wo_ref, m_ref, g2_ref, wr_ref,
                   x1_ref, h2_ref, aff_ref, zp_ref, zm_ref, il_ref, *, t):
    half = t // 2
    i = pl.program_id(1)

    @pl.when(i == 0)
    def _fold_halves():
        for src, col in ((zc_ref, 0), (zs_ref, FNET_W)):
            lo = src[0:half, :]
            hi = src[half:t, :]
            zp_ref[:, col:col + FNET_W] = lo + hi
            zm_ref[:, col:col + FNET_W] = lo - hi

    off = off_ref[pl.ds(i, 1), :]
    co = off[:, 0:half]
    so = off[:, half:t]
    cb = cb_ref[...]
    sb = sb_ref[...]
    ct = (cb * co - sb * so).astype(BF16)
    st = (sb * co + cb * so).astype(BF16)
    tr = cb.shape[0]
    h = tr // 2
    even = _dot(ct[0:h, :], zp_ref[:, 0:FNET_W]) - _dot(st[0:h, :], zp_ref[:, FNET_W:2 * FNET_W])
    odd = _dot(ct[h:, :], zm_ref[:, 0:FNET_W]) - _dot(st[h:, :], zm_ref[:, FNET_W:2 * FNET_W])
    for c in range(FNET_W // LANES):
        il_ref[c, pl.ds(0, h, stride=2), :] = even[:, c * LANES:(c + 1) * LANES]
        il_ref[c, pl.ds(1, h, stride=2), :] = odd[:, c * LANES:(c + 1) * LANES]
    fm = jnp.concatenate([il_ref[c] for c in range(FNET_W // LANES)], axis=1).astype(BF16)
    x1, h2, aff = _mixout_body(x_ref[...], [a_ref[...]], fm, wo_ref, m_ref, g2_ref, wr_ref)
    x1_ref[...] = x1
    h2_ref[...] = h2
    aff_ref[...] = aff


def _ctx_front_kernel(x_ref, m_ref, g1_ref, win_ref, qg_ref, wuq_ref, kvg_ref, wk_ref, wv_ref, wcs_ref,
                      tq_ref, tk_ref, ct_ref, st_ref, wo_ref, g2_ref, wr_ref,
                      x1_ref, h2_ref, aff_ref, ckv_ref, kpe_ref, *, kc, t):
    x = x_ref[...]
    q_heads, k, v_t, zc, zs, ckv, kpe = _premix_body(x, m_ref, g1_ref, win_ref, qg_ref, wuq_ref, kvg_ref,
                                                     wk_ref, wv_ref, wcs_ref, tq_ref, tk_ref, False)
    ckv_ref[...] = ckv
    for r in range(x.shape[0] // t):
        kpe_ref[r] = kpe[r * t:(r + 1) * t, :].T[0:QK_ROPE, :]
    attn, fm = [], []
    for r in range(x.shape[0] // t):
        rows = slice(r * t, (r + 1) * t)
        attn.append(jnp.concatenate(_attn_body([q[:, rows] for q in q_heads], [k[rows, :]], [v_t[:, rows]], kc),
                                    axis=1))
        fm.append((_dot(ct_ref[...], zc[rows, :]) - _dot(st_ref[...], zs[rows, :])).astype(BF16))
    x1, h2, aff = _mixout_body(x_ref[...], [jnp.concatenate(attn, axis=0)], jnp.concatenate(fm, axis=0),
                               wo_ref, m_ref, g2_ref, wr_ref)
    x1_ref[...] = x1
    h2_ref[...] = h2
    for r in range(x.shape[0] // t):
        aff_ref[r] = aff[:, r * t:(r + 1) * t]


def _ctx_front(x, mods6, g1, win, qg, wuq, kvg, wk, wv, wcs, tq, tk, ct, st, wo, g2, wr_t, *, n_req, t, mod_row):
    full = lambda a: pl.BlockSpec(a.shape, lambda b: (0,) * a.ndim)
    rps = CTX_REQS
    row = lambda w: pl.BlockSpec((rps * t, w), lambda b: (b, 0))
    return pl.pallas_call(
        functools.partial(_ctx_front_kernel, kc=KEY_CHUNK, t=t),
        grid=(n_req // rps,),
        in_specs=[row(D_MODEL), pl.BlockSpec((1, 6, D_MODEL), lambda b: (mod_row, 0, 0)),
                  full(g1), full(win), full(qg), full(wuq), full(kvg), full(wk), full(wv), full(wcs),
                  full(tq), full(tk), full(ct), full(st), full(wo), full(g2), full(wr_t)],
        out_specs=[row(D_MODEL), row(D_MODEL), pl.BlockSpec((rps, N_EXPERTS, t), lambda b: (b, 0, 0)),
                   row(KV_LORA), pl.BlockSpec((rps, QK_ROPE, t), lambda b: (b, 0, 0))],
        out_shape=[jax.ShapeDtypeStruct((n_req * t, D_MODEL), F32),
                   jax.ShapeDtypeStruct((n_req * t, D_MODEL), BF16),
                   jax.ShapeDtypeStruct((n_req, N_EXPERTS, t), F32),
                   jax.ShapeDtypeStruct((n_req * t, KV_LORA), F32),
                   jax.ShapeDtypeStruct((n_req, QK_ROPE, t), F32)],
        compiler_params=_cparams(("parallel",)),
        name="ctx_front",
    )(x, mods6, g1, win, qg, wuq, kvg, wk, wv, wcs, tq, tk, ct, st, wo, g2, wr_t)


def _mixout(x, attn, zc, zs, cb, sb, off, wo, mods6, g2, wr_t, *, n_req, t, mod_row):
    tr = MIX_TILE
    nr = t // tr
    full = lambda a: pl.BlockSpec(a.shape, lambda b, i: (0,) * a.ndim)
    return pl.pallas_call(
        functools.partial(_mixout_kernel, t=t),
        grid=(n_req, nr),
        in_specs=[pl.BlockSpec((tr, D_MODEL), lambda b, i: (b * nr + i, 0)),
                  pl.BlockSpec((tr, V_W), lambda b, i: (b * nr + i, 0)),
                  pl.BlockSpec((t, FNET_W), lambda b, i: (b, 0)),
                  pl.BlockSpec((t, FNET_W), lambda b, i: (b, 0)),
                  full(cb), full(sb), full(off),
                  full(wo),
                  pl.BlockSpec((1, 6, D_MODEL), lambda b, i: (mod_row(b), 0, 0)),
                  full(g2), full(wr_t)],
        out_specs=[pl.BlockSpec((tr, D_MODEL), lambda b, i: (b * nr + i, 0)),
                   pl.BlockSpec((tr, D_MODEL), lambda b, i: (b * nr + i, 0)),
                   pl.BlockSpec((None, N_EXPERTS, tr), lambda b, i: (b, 0, i))],
        out_shape=[jax.ShapeDtypeStruct((n_req * t, D_MODEL), F32),
                   jax.ShapeDtypeStruct((n_req * t, D_MODEL), BF16),
                   jax.ShapeDtypeStruct((n_req, N_EXPERTS, t), F32)],
        scratch_shapes=[pltpu.VMEM((t // 2, 2 * FNET_W), BF16), pltpu.VMEM((t // 2, 2 * FNET_W), BF16),
                        pltpu.VMEM((FNET_W // LANES, tr, LANES), F32)],
        compiler_params=_cparams(("parallel", "arbitrary")),
        name="mixout",
    )(x, attn, zc, zs, cb, sb, off, wo, mods6, g2, wr_t)


def _prefix_count(flags, tri):
    n = flags.shape[1]
    carry = None
    outs = []
    ends = []
    for j in range(n // TOK_TILE):
        c = _dot(flags[:, j * TOK_TILE:(j + 1) * TOK_TILE].astype(BF16), tri)
        if carry is not None:
            c = c + carry
        outs.append(c)
        carry = c[:, TOK_TILE - 1:TOK_TILE]
        ends.append(carry)
    return (outs[0] if len(outs) == 1 else jnp.concatenate(outs, axis=1)), ends


def _route_kernel(aff_ref, pos_ref, off_ref, *, cap):
    a = aff_ref[...]
    rows = a.shape[0]
    capf = jnp.float32(cap)
    thr = jnp.zeros((rows, 1), jnp.int32)
    for bit in range(30, -1, -1):
        cand = thr | jnp.int32(1 << bit)
        cand_f = lax.bitcast_convert_type(cand, F32)
        cnt = jnp.sum(jnp.where(a >= cand_f, 1.0, 0.0), axis=1, keepdims=True)
        thr = jnp.where(cnt >= capf, cand, thr)
    thr_f = lax.bitcast_convert_type(thr, F32)
    above_f = lax.bitcast_convert_type(thr + 1, F32)
    gt = jnp.where(a >= above_f, 1.0, 0.0)
    tie = jnp.where(a >= thr_f, 1.0, 0.0) - gt
    need = capf - jnp.sum(gt, axis=1, keepdims=True)
    r_i = lax.broadcasted_iota(jnp.int32, (TOK_TILE, TOK_TILE), 0)
    c_i = lax.broadcasted_iota(jnp.int32, (TOK_TILE, TOK_TILE), 1)
    tri = jnp.where(r_i <= c_i, 1.0, 0.0).astype(BF16)
    tie_before = _prefix_count(tie, tri)[0] - tie
    sel = gt + tie * jnp.where(tie_before < need, 1.0, 0.0)
    count, ends = _prefix_count(sel, tri)
    pos_ref[...] = jnp.where(sel > 0.5, count - 1.0, -1.0)
    lane = lax.broadcasted_iota(jnp.int32, (rows, LANES), 1)
    offs = jnp.zeros((rows, LANES), F32)
    for j, end in enumerate(ends):
        offs = offs + jnp.where(lane == j + 1, end, 0.0)
    off_ref[...] = offs.astype(jnp.int32)


def _route(aff_t, cap):
    return pl.pallas_call(
        functools.partial(_route_kernel, cap=cap),
        out_shape=[jax.ShapeDtypeStruct(aff_t.shape, F32),
                   jax.ShapeDtypeStruct((aff_t.shape[0], LANES), jnp.int32)],
        compiler_params=pltpu.CompilerParams(vmem_limit_bytes=VMEM_LIMIT),
        name="route",
    )(aff_t)


def _gather_kernel(pos_ref, aff_ref, h_ref, xs_ref, g_ref, *, cap):
    rps, ne, n = pos_ref.shape
    slot = lax.broadcasted_iota(jnp.int32, (ne, cap, n), 1).astype(F32)
    for r in range(rps):
        pos = pos_ref[r]
        aff = aff_ref[r]
        hit = pos[:, None, :] == slot
        onehot = jnp.where(hit, 1.0, 0.0).reshape(ne * cap, n).astype(BF16)
        xs = _dot(onehot, h_ref[r * n:(r + 1) * n, :])
        xs_ref[:, r * cap:(r + 1) * cap, :] = xs.astype(BF16).reshape(ne, cap, D_MODEL)
        g_ref[:, r * cap:(r + 1) * cap, :] = jnp.sum(jnp.where(hit, aff[:, None, :], 0.0), axis=2, keepdims=True)


def _gather(pos_t, aff_t, h2, *, n_req, n, cap, rps):
    pos3 = pos_t.reshape(n_req, N_EXPERTS, n)
    aff3 = aff_t.reshape(n_req, N_EXPERTS, n)
    return pl.pallas_call(
        functools.partial(_gather_kernel, cap=cap),
        grid=(n_req // rps,),
        in_specs=[pl.BlockSpec((rps, N_EXPERTS, n), lambda b: (b, 0, 0)),
                  pl.BlockSpec((rps, N_EXPERTS, n), lambda b: (b, 0, 0)),
                  pl.BlockSpec((rps * n, D_MODEL), lambda b: (b, 0))],
        out_specs=[pl.BlockSpec((N_EXPERTS, rps * cap, D_MODEL), lambda b: (0, b, 0)),
                   pl.BlockSpec((N_EXPERTS, rps * cap, 1), lambda b: (0, b, 0))],
        out_shape=[jax.ShapeDtypeStruct((N_EXPERTS, n_req * cap, D_MODEL), BF16),
                   jax.ShapeDtypeStruct((N_EXPERTS, n_req * cap, 1), F32)],
        compiler_params=_cparams(("parallel",)),
        name="gather",
    )(pos3, aff3, h2)


def _window_plan(off_ref, b, j, cap, w):
    w0 = []
    need = jnp.int32(0)
    for e in range(N_EXPERTS):
        idx = (b * N_EXPERTS + e) * OFF_STRIDE + j
        base = (off_ref[idx] >> 4) << 4
        w0.append(base)
        need = jnp.maximum(need, off_ref[idx + 1] - base)
    return w0, (need + (w - 1)) >> (w.bit_length() - 1)


def _window(w0_e, p, cap, w):
    low = w0_e + p * w
    start = pl.multiple_of(jnp.minimum(low, cap - w), 16)
    return low, start


def _gather_win_kernel(off_ref, pos_ref, aff_ref, h_ref, xs_ref, g_ref, *, cap, w):
    b = pl.program_id(0)
    step = pl.program_id(1)

    @pl.when(step == 0)
    def _init():
        xs_ref[...] = jnp.zeros_like(xs_ref)
        g_ref[...] = jnp.zeros_like(g_ref)

    r = lax.broadcasted_iota(jnp.int32, (w, 1), 0).astype(F32)
    n_sub = pos_ref.shape[1] // TOK_TILE
    for sub in range(n_sub):
        cols = slice(sub * TOK_TILE, (sub + 1) * TOK_TILE)
        pos = pos_ref[:, cols]
        aff = aff_ref[:, cols]
        h = h_ref[cols, :]
        w0, n_pass = _window_plan(off_ref, b, step * n_sub + sub, cap, w)

        def one_pass(p, carry, pos=pos, aff=aff, h=h, w0=w0):
            starts, hots, gates = [], [], []
            for e in range(N_EXPERTS):
                low, start = _window(w0[e], p, cap, w)
                starts.append(start)
                mine = jnp.where(r >= (low - start).astype(F32), 1.0, 0.0)
                hot = jnp.where(pos[e:e + 1, :] - start.astype(F32) == r, mine, 0.0)
                hots.append(hot)
                gates.append(jnp.sum(hot * aff[e:e + 1, :], axis=1, keepdims=True))
            rows = _dot(jnp.concatenate(hots, axis=0).astype(BF16), h).astype(BF16)
            for e in range(N_EXPERTS):
                win = pl.ds(starts[e], w)
                xs_ref[e, win, :] = xs_ref[e, win, :] + rows[e * w:(e + 1) * w, :]
                g_ref[e, win, :] = g_ref[e, win, :] + gates[e]
            return carry

        one_pass(0, 0)
        lax.fori_loop(1, n_pass, one_pass, 0)


def _gather_win(offs, pos_t, aff_t, h2, *, n_req, n, cap):
    tb = WIN_BLOCKS * TOK_TILE
    nb = n // tb
    grid_spec = pltpu.PrefetchScalarGridSpec(
        num_scalar_prefetch=1,
        grid=(n_req, nb),
        in_specs=[pl.BlockSpec((N_EXPERTS, tb), lambda b, j, off: (b, j)),
                  pl.BlockSpec((N_EXPERTS, tb), lambda b, j, off: (b, j)),
                  pl.BlockSpec((tb, D_MODEL), lambda b, j, off: (b * nb + j, 0))],
        out_specs=[pl.BlockSpec((N_EXPERTS, cap, D_MODEL), lambda b, j, off: (0, b, 0)),
                   pl.BlockSpec((N_EXPERTS, cap, 1), lambda b, j, off: (0, b, 0))])
    return pl.pallas_call(
        functools.partial(_gather_win_kernel, cap=cap, w=SLOT_WIN),
        grid_spec=grid_spec,
        out_shape=[jax.ShapeDtypeStruct((N_EXPERTS, n_req * cap, D_MODEL), BF16),
                   jax.ShapeDtypeStruct((N_EXPERTS, n_req * cap, 1), F32)],
        compiler_params=_cparams(("parallel", "arbitrary")),
        name="gather_win",
    )(offs, pos_t, aff_t, h2)


def _ffn_kernel(xc_ref, xl_ref, gc_ref, gl_ref, wg_ref, wu_ref, wd_ref, yc_ref, yl_ref, wgb, wub, wdb):
    wgb[...] = wg_ref[0].astype(BF16)
    wub[...] = wu_ref[0].astype(BF16)
    wdb[...] = wd_ref[0].astype(BF16)
    for x_ref, g_ref, y_ref in ((xc_ref, gc_ref, yc_ref), (xl_ref, gl_ref, yl_ref)):
        for j in range(x_ref.shape[1] // FFN_ROWS):
            rows = slice(j * FFN_ROWS, (j + 1) * FFN_ROWS)
            x = x_ref[0, rows, :]
            gate = _dot(x, wgb[...])
            up = _dot(x, wub[...])
            hid = (gate * jax.nn.sigmoid(gate) * up).astype(BF16)
            ys = _dot(hid, wdb[...]) * g_ref[0, rows, :]
            y_ref[0, rows, :] = ys.astype(BF16)


def _ffn(xc, xl, gc, gl, wg, wu, wd):
    m = xc.shape[1]
    xspec = pl.BlockSpec((1, m, D_MODEL), lambda e: (e, 0, 0))
    gspec = pl.BlockSpec((1, m, 1), lambda e: (e, 0, 0))
    shp = jax.ShapeDtypeStruct((N_EXPERTS, m, D_MODEL), BF16)
    return pl.pallas_call(
        _ffn_kernel,
        grid=(N_EXPERTS,),
        in_specs=[xspec, xspec, gspec, gspec,
                  pl.BlockSpec((1, D_MODEL, D_EXPERT), lambda e: (e, 0, 0)),
                  pl.BlockSpec((1, D_MODEL, D_EXPERT), lambda e: (e, 0, 0)),
                  pl.BlockSpec((1, D_EXPERT, D_MODEL), lambda e: (e, 0, 0))],
        out_specs=[xspec, xspec],
        out_shape=[shp, shp],
        scratch_shapes=[pltpu.VMEM((D_MODEL, D_EXPERT), BF16), pltpu.VMEM((D_MODEL, D_EXPERT), BF16),
                        pltpu.VMEM((D_EXPERT, D_MODEL), BF16)],
        compiler_params=_cparams(("arbitrary",)),
        name="ffn",
    )(xc, xl, gc, gl, wg, wu, wd)


def _combine_kernel(x1_ref, pos_ref, ys_ref, m_ref, fg_ref, o_ref, *, cap):
    rps, n, _ = pos_ref.shape
    w = N_EXPERTS * cap
    e_i = lax.broadcasted_iota(jnp.int32, (N_EXPERTS, w), 0)
    j_i = lax.broadcasted_iota(jnp.int32, (N_EXPERTS, w), 1)
    spread = jnp.where((j_i >> (cap.bit_length() - 1)) == e_i, 1.0, 0.0).astype(BF16)
    lane_slot = (lax.broadcasted_iota(jnp.int32, (1, w), 1) & (cap - 1)).astype(F32)
    gate2 = m_ref[0, 5:6, :]
    for r in range(rps):
        pos = pos_ref[r].astype(BF16)
        onehot = jnp.where(_dot(pos, spread) == lane_slot, 1.0, 0.0).astype(BF16)
        acc = _dot(onehot, ys_ref[:, r * cap:(r + 1) * cap, :].reshape(w, D_MODEL))
        rows = slice(r * n, (r + 1) * n)
        o_ref[rows, :] = _rms(x1_ref[rows, :] + gate2 * acc, fg_ref[...])


def _combine(x1, pos_tok, ys, mods6, fg, *, n_req, n, cap, mod_row, rps):
    return pl.pallas_call(
        functools.partial(_combine_kernel, cap=cap),
        grid=(n_req // rps,),
        in_specs=[pl.BlockSpec((rps * n, D_MODEL), lambda b: (b, 0)),
                  pl.BlockSpec((rps, n, N_EXPERTS), lambda b: (b, 0, 0)),
                  pl.BlockSpec((N_EXPERTS, rps * cap, D_MODEL), lambda b: (0, b, 0)),
                  pl.BlockSpec((1, 6, D_MODEL), lambda b: (mod_row, 0, 0)),
                  pl.BlockSpec((1, D_MODEL), lambda b: (0, 0))],
        out_specs=pl.BlockSpec((rps * n, D_MODEL), lambda b: (b, 0)),
        out_shape=jax.ShapeDtypeStruct((n_req * n, D_MODEL), F32),
        compiler_params=_cparams(("parallel",)),
        name="combine",
    )(x1, pos_tok, ys, mods6, fg)


def _combine_win_kernel(off_ref, x1_ref, pos_ref, ys_ref, m_ref, fg_ref, o_ref, acc_ref, *, cap, w):
    b = pl.program_id(0)
    step = pl.program_id(1)
    width = N_EXPERTS * w
    e_i = lax.broadcasted_iota(jnp.int32, (N_EXPERTS, width), 0)
    j_i = lax.broadcasted_iota(jnp.int32, (N_EXPERTS, width), 1)
    spread = jnp.where((j_i >> (w.bit_length() - 1)) == e_i, 1.0, 0.0).astype(BF16)
    lane_slot = (lax.broadcasted_iota(jnp.int32, (1, width), 1) & (w - 1)).astype(F32)
    lane_e = lax.broadcasted_iota(jnp.int32, (1, N_EXPERTS), 1)
    n_sub = pos_ref.shape[0] // TOK_TILE
    for sub in range(n_sub):
        rows = slice(sub * TOK_TILE, (sub + 1) * TOK_TILE)
        pos = pos_ref[rows, :]
        w0, n_pass = _window_plan(off_ref, b, step * n_sub + sub, cap, w)

        def window_sum(p, pos=pos, w0=w0):
            start_row = jnp.zeros((1, N_EXPERTS), F32)
            first_row = jnp.zeros((1, N_EXPERTS), F32)
            wins = []
            for e in range(N_EXPERTS):
                low, start = _window(w0[e], p, cap, w)
                start_row = jnp.where(lane_e == e, start.astype(F32), start_row)
                first_row = jnp.where(lane_e == e, (low - start).astype(F32), first_row)
                wins.append(ys_ref[e, pl.ds(start, w), :])
            rel = pos - start_row
            rel = jnp.where(rel >= first_row, rel, -1.0).astype(BF16)
            onehot = jnp.where(_dot(rel, spread) == lane_slot, 1.0, 0.0).astype(BF16)
            return _dot(onehot, jnp.concatenate(wins, axis=0))

        def finish(acc, rows=rows):
            o_ref[rows, :] = _rms(x1_ref[rows, :] + m_ref[0, 5:6, :] * acc, fg_ref[...])

        acc0 = window_sum(0)
        acc_ref[sub] = acc0
        finish(acc0)

        @pl.when(n_pass > 1)
        def _more_passes(sub=sub, n_pass=n_pass, window_sum=window_sum, finish=finish):
            def one_pass(p, carry):
                acc_ref[sub] += window_sum(p)
                return carry

            lax.fori_loop(1, n_pass, one_pass, 0)
            finish(acc_ref[sub])


def _combine_win(offs, x1, pos_tok, ys, mods6, fg, *, n_req, n, cap, mod_row):
    tr = WIN_BLOCKS * TOK_TILE
    nr = n // tr
    grid_spec = pltpu.PrefetchScalarGridSpec(
        num_scalar_prefetch=1,
        grid=(n_req, nr),
        in_specs=[pl.BlockSpec((tr, D_MODEL), lambda b, i, off: (b * nr + i, 0)),
                  pl.BlockSpec((None, tr, N_EXPERTS), lambda b, i, off: (b, i, 0)),
                  pl.BlockSpec((N_EXPERTS, cap, D_MODEL), lambda b, i, off: (0, b, 0)),
                  pl.BlockSpec((1, 6, D_MODEL), lambda b, i, off: (mod_row(b), 0, 0)),
                  pl.BlockSpec((1, D_MODEL), lambda b, i, off: (0, 0))],
        out_specs=pl.BlockSpec((tr, D_MODEL), lambda b, i, off: (b * nr + i, 0)),
        scratch_shapes=[pltpu.VMEM((WIN_BLOCKS, TOK_TILE, D_MODEL), F32)])
    return pl.pallas_call(
        functools.partial(_combine_win_kernel, cap=cap, w=SLOT_WIN),
        grid_spec=grid_spec,
        out_shape=jax.ShapeDtypeStruct((n_req * n, D_MODEL), F32),
        compiler_params=_cparams(("parallel", "parallel")),
        name="combine_win",
    )(offs, x1, pos_tok, ys, mods6, fg)


def _rot_half(w):
    half = QK_ROPE // 2
    return jnp.concatenate([-w[..., half:], w[..., :half]], axis=-1)


def _rope_tables(t):
    n_rows = t // GRID_W
    rows = np.repeat(np.arange(n_rows, dtype=np.float64), GRID_W)
    cols = np.tile(np.arange(GRID_W, dtype=np.float64), n_rows)
    n_freq = QK_ROPE // 4
    inv_freq = ROPE_BASE ** (-np.arange(n_freq, dtype=np.float64) / n_freq)
    ang = np.concatenate([rows[:, None] * inv_freq, cols[:, None] * inv_freq], axis=-1)
    cos = np.concatenate([np.cos(ang), np.cos(ang)], axis=-1)
    sin = np.concatenate([np.sin(ang), np.sin(ang)], axis=-1)
    return cos, sin


def _qk_tables(cos, sin):
    t = cos.shape[0]
    scale = (QK_NOPE + QK_ROPE) ** -0.5 * np.log2(np.e)
    pad = np.zeros((t, HEAD_PAD - QK_NOPE - QK_ROPE))
    cosq = np.concatenate([np.full((t, QK_NOPE), scale), cos * scale, pad], axis=1)
    sinq = np.concatenate([np.zeros((t, QK_NOPE)), sin * scale, pad], axis=1)
    tq_t = np.concatenate([cosq, sinq], axis=1).T
    tk = np.concatenate([cos, sin, np.zeros((t, LANES - 2 * QK_ROPE))], axis=1)
    return jnp.asarray(tq_t, F32), jnp.asarray(tk, F32)


def _dft_angles(rows, t):
    k = np.arange(t, dtype=np.int64)
    return ((rows[:, None] * k[None, :]) % t).astype(np.float64) * (2.0 * np.pi / t)


def _dft_tables(t):
    ang = _dft_angles(np.arange(t, dtype=np.int64), t)
    scale = (t * FNET_CH) ** -0.5
    return jnp.asarray(np.cos(ang) * scale, F32).astype(BF16), jnp.asarray(np.sin(ang) * scale, F32).astype(BF16)


def _dft_half_tables(t):
    r = np.arange(MIX_TILE, dtype=np.int64)
    ang = _dft_angles(np.concatenate([r[0::2], r[1::2]]), t)[:, :t // 2]
    scale = (t * FNET_CH) ** -0.5
    ang_off = _dft_angles(np.arange(t // MIX_TILE, dtype=np.int64) * MIX_TILE, t)[:, :t // 2]
    off = np.concatenate([np.cos(ang_off), np.sin(ang_off)], axis=1)
    return jnp.asarray(np.cos(ang) * scale, F32), jnp.asarray(np.sin(ang) * scale, F32), jnp.asarray(off, F32)


def _block_diag(w):
    g, a, b = w.shape
    eye = jnp.eye(g, dtype=w.dtype)
    return (eye[:, None, :, None] * w[:, :, None, :]).reshape(g * a, g * b)


def kernel(x_prompt, x_sample, cache_ckv, cache_kpe, c, c_ctx, w_mod, b_mod, norm1_g, w_in, q_norm_g, w_uq,
           kv_norm_g, w_ukv, w_fmix, w_out, norm2_g, w_router, w_e_gate, w_e_up, w_e_down, final_g):
    assert w_mod.shape[0] == 1, "single-layer problem"
    n_ctx, t_ctx, _ = x_prompt.shape
    n_lat, t_lat, _ = x_sample.shape
    past = cache_ckv.shape[2]
    ctx_row = n_lat

    win = _winprep(jnp.swapaxes(w_in, 1, 2)[0])
    wq3 = w_uq[0].reshape(Q_LORA, N_HEADS, QK_NOPE + QK_ROPE)
    qpad = jnp.zeros((Q_LORA, N_HEADS, HEAD_PAD - QK_NOPE - QK_ROPE), F32)
    wuq_main = jnp.concatenate([wq3, qpad], axis=2).reshape(Q_LORA, QK_W)
    wuq_rot = _rot_half(wq3[..., QK_NOPE:]).reshape(Q_LORA, N_HEADS * QK_ROPE)
    wuq_lat = jnp.concatenate([wuq_main, wuq_rot], axis=1).T.astype(BF16)
    wuq_ctx = wuq_main.T.astype(BF16)
    wkv3 = w_ukv[0].reshape(KV_LORA, N_HEADS, QK_NOPE + V_HEAD)
    wk_top = jnp.concatenate([wkv3[..., :QK_NOPE], jnp.zeros((KV_LORA, N_HEADS, HEAD_PAD - QK_NOPE), F32)],
                             axis=2).reshape(KV_LORA, QK_W)
    place = jnp.concatenate([jnp.zeros((QK_ROPE, QK_NOPE), F32), jnp.eye(QK_ROPE, dtype=F32),
                             jnp.zeros((QK_ROPE, HEAD_PAD - QK_NOPE - QK_ROPE), F32)], axis=1)
    place = jnp.tile(place, (1, N_HEADS))
    wk = jnp.concatenate([wk_top, place, place, jnp.zeros((LANES - 2 * QK_ROPE, QK_W), F32)], axis=0).astype(BF16)
    wv = wkv3[..., QK_NOPE:].reshape(KV_LORA, V_W).T.astype(BF16)
    wo = w_out[0].astype(BF16)
    wr_t = w_router[0].T.astype(BF16)

    cos, sin = _rope_tables(t_lat)
    tq_lat, tk_lat = _qk_tables(cos, sin)
    assert n_ctx % CTX_REQS == 0
    tq_ctx, tk_ctx = _qk_tables(np.ones((CTX_REQS * t_ctx, QK_ROPE)), np.zeros((CTX_REQS * t_ctx, QK_ROPE)))
    ch_ang = _dft_angles(np.arange(FNET_CH, dtype=np.int64), FNET_CH)
    dft_ctx = _dft_tables(t_ctx)
    dft_lat = _dft_half_tables(t_lat)

    assert n_lat + 1 <= SUBLANES
    c8 = jnp.concatenate([c, c_ctx[None, :], jnp.zeros((SUBLANES - n_lat - 1, D_MODEL), F32)], axis=0)
    mods6 = _mods(c8, w_mod[0], b_mod[0][None, :]).reshape(SUBLANES, 6, D_MODEL)
    cw, sw = _fold(jnp.asarray(np.cos(ch_ang), F32), jnp.asarray(np.sin(ch_ang), F32), w_fmix[0])
    gh = FNET_GROUPS // 2
    wcs = jnp.concatenate([jnp.concatenate([_block_diag(cw[a * gh:(a + 1) * gh]), _block_diag(sw[a * gh:(a + 1) * gh])],
                                           axis=1) for a in range(2)], axis=0).astype(BF16)

    g1 = norm1_g[0][None, :]
    qg = q_norm_g[0][None, :]
    kvg = kv_norm_g[0][None, :]
    g2 = norm2_g[0][None, :]
    fg = final_g[None, :]

    xp = x_prompt.reshape(n_ctx * t_ctx, D_MODEL)
    xs = x_sample.reshape(n_lat * t_lat, D_MODEL)
    tiles_lat = t_lat // PRE_TILE

    x1c, h2c, affc, ckv_c, kpe_c = _ctx_front(
        xp, mods6, g1, win, qg, wuq_ctx, kvg, wk, wv, wcs, tq_ctx, tk_ctx,
        *dft_ctx, wo, g2, wr_t, n_req=n_ctx, t=t_ctx, mod_row=ctx_row)
    ql, kl, vl, zcl, zsl = _premix(
        xs, mods6, g1, win, qg, wuq_lat, kvg, wk, wv, wcs, tq_lat, tk_lat,
        mod_row=lambda i: i // tiles_lat, tab_row=lambda i: i % tiles_lat, rope=True)
    xk_cache = jnp.concatenate([cache_ckv[:, 0], cache_kpe[:, 0],
                                jnp.zeros((n_lat, past, 2 * LANES - KV_LORA - QK_ROPE), F32)],
                               axis=-1).reshape(n_lat * past, 2 * LANES).astype(BF16)
    kpast, vpast = _cachekv(xk_cache, wk, wv)

    attn_l = _attention(ql, [kpast, kl], [vpast, vl], n_req=n_lat, t_q=t_lat, kv_lens=[past, t_lat], tq=TOK_TILE,
                        pairs_per_step=2)

    x1l, h2l, affl = _mixout(xs, attn_l, zcl, zsl, *dft_lat, wo, mods6, g2, wr_t,
                             n_req=n_lat, t=t_lat, mod_row=lambda b: b)

    cap_c = CAP_FACTOR * t_ctx // N_EXPERTS
    cap_l = CAP_FACTOR * t_lat // N_EXPERTS
    affc2 = affc.reshape(n_ctx * N_EXPERTS, t_ctx)
    affl2 = affl.reshape(n_lat * N_EXPERTS, t_lat)
    posc, _ = _route(affc2, cap_c)
    posl, offl = _route(affl2, cap_l)
    assert t_lat // TOK_TILE + 1 <= OFF_STRIDE and cap_l % SLOT_WIN == 0
    offl = offl[:, :OFF_STRIDE].reshape(-1)
    xsc, gc = _gather(posc, affc2, h2c, n_req=n_ctx, n=t_ctx, cap=cap_c, rps=MOE_REQS)
    xsl, gl = _gather_win(offl, posl, affl2, h2l, n_req=n_lat, n=t_lat, cap=cap_l)
    ysc, ysl = _ffn(xsc, xsl, gc, gl, w_e_gate[0], w_e_up[0], w_e_down[0])

    posc_tok = posc.reshape(n_ctx, N_EXPERTS, t_ctx).transpose(0, 2, 1)
    posl_tok = posl.reshape(n_lat, N_EXPERTS, t_lat).transpose(0, 2, 1)
    y_prompt = _combine(x1c, posc_tok, ysc, mods6, fg, n_req=n_ctx, n=t_ctx, cap=cap_c, mod_row=ctx_row,
                        rps=MOE_REQS)
    y_sample = _combine_win(offl, x1l, posl_tok, ysl, mods6, fg, n_req=n_lat, n=t_lat, cap=cap_l,
                            mod_row=lambda b: b)

    return (y_prompt.reshape(n_ctx, t_ctx, D_MODEL), y_sample.reshape(n_lat, t_lat, D_MODEL),
            ckv_c.reshape(n_ctx, 1, t_ctx, KV_LORA), kpe_c.transpose(0, 2, 1).reshape(n_ctx, 1, t_ctx, QK_ROPE))
```

```python
import functools

import jax
import jax.numpy as jnp
import numpy as np
from jax import lax
from jax.experimental import pallas as pl
from jax.experimental.pallas import tpu as pltpu

F32 = jnp.float32
BF16 = jnp.bfloat16

D_MODEL = 1024
N_HEADS = 8
QK_NOPE = 64
QK_ROPE = 32
V_HEAD = 64
Q_LORA = 256
KV_LORA = 128
FNET_GROUPS = 8
FNET_CH = 64
FNET_W = FNET_GROUPS * FNET_CH
N_EXPERTS = 16
CAP_FACTOR = 2
D_EXPERT = 512
GRID_W = 64
ROPE_BASE = 10000.0
EPS = 1e-6

LANES = 128
SUBLANES = 8
HEAD_PAD = LANES
QK_W = N_HEADS * HEAD_PAD
V_W = N_HEADS * V_HEAD
TOK_TILE = 256
PRE_TILE = 1024
MIX_TILE = 512
FFN_ROWS = 1024
MODS_TILE = 1536
CTX_REQS = 4
MOE_REQS = 4
KEY_CHUNK = 512
SLOT_WIN = 64
WIN_BLOCKS = 4
OFF_STRIDE = 16
VMEM_LIMIT = 48 * 1024 * 1024

_NT = (((1,), (1,)), ((), ()))


def _cparams(sem):
    return pltpu.CompilerParams(dimension_semantics=sem, vmem_limit_bytes=VMEM_LIMIT)


def _rms(x, g):
    return x * lax.rsqrt(jnp.mean(x * x, axis=-1, keepdims=True) + EPS) * g


def _dot(a, b):
    return jnp.dot(a, b, preferred_element_type=F32)


def _mods_kernel(c_ref, w_ref, b_ref, o_ref):
    c = c_ref[...]
    s = c * jax.nn.sigmoid(c)
    o_ref[...] = _dot(s.astype(BF16), w_ref[...].astype(BF16)) + b_ref[...]


def _mods(c8, w_mod, b_mod):
    n = w_mod.shape[1]
    tn = MODS_TILE
    return pl.pallas_call(
        _mods_kernel,
        grid=(n // tn,),
        in_specs=[pl.BlockSpec((SUBLANES, D_MODEL), lambda j: (0, 0)),
                  pl.BlockSpec((D_MODEL, tn), lambda j: (0, j)),
                  pl.BlockSpec((1, tn), lambda j: (0, j))],
        out_specs=pl.BlockSpec((SUBLANES, tn), lambda j: (0, j)),
        out_shape=jax.ShapeDtypeStruct((SUBLANES, n), F32),
        compiler_params=_cparams(("arbitrary",)),
        name="mods",
    )(c8, w_mod, b_mod)


def _fold_kernel(cc_ref, sc_ref, w_ref, cw_ref, sw_ref):
    for g in range(FNET_GROUPS):
        w = w_ref[g]
        cw_ref[g] = jnp.dot(cc_ref[...], w, preferred_element_type=F32, precision=lax.Precision.HIGHEST)
        sw_ref[g] = jnp.dot(sc_ref[...], w, preferred_element_type=F32, precision=lax.Precision.HIGHEST)


def _fold(cc, sc, w_fmix):
    shp = jax.ShapeDtypeStruct((FNET_GROUPS, FNET_CH, FNET_CH), F32)
    return pl.pallas_call(_fold_kernel, out_shape=(shp, shp), name="fold")(cc, sc, w_fmix)


KPE_LO = Q_LORA + KV_LORA
PROJ_W = 1024


def _winprep_kernel(wt_ref, o_ref):
    half = QK_ROPE // 2
    kpe_hi = KPE_LO + QK_ROPE
    o_ref[0:kpe_hi, :] = wt_ref[0:kpe_hi, :].astype(BF16)
    o_ref[kpe_hi:kpe_hi + half, :] = (-wt_ref[KPE_LO + half:kpe_hi, :]).astype(BF16)
    o_ref[kpe_hi + half:kpe_hi + QK_ROPE, :] = wt_ref[KPE_LO:KPE_LO + half, :].astype(BF16)
    o_ref[kpe_hi + QK_ROPE:PROJ_W - FNET_W, :] = jnp.zeros((PROJ_W - FNET_W - kpe_hi - QK_ROPE, D_MODEL), BF16)
    o_ref[PROJ_W - FNET_W:PROJ_W, :] = wt_ref[kpe_hi:kpe_hi + FNET_W, :].astype(BF16)


def _winprep(w_in_t):
    assert w_in_t.shape == (KPE_LO + QK_ROPE + FNET_W, D_MODEL)
    return pl.pallas_call(
        _winprep_kernel,
        out_shape=jax.ShapeDtypeStruct((PROJ_W, D_MODEL), BF16),
        compiler_params=pltpu.CompilerParams(vmem_limit_bytes=VMEM_LIMIT),
        name="winprep",
    )(w_in_t)


def _premix_body(x, m_ref, g1_ref, win_ref, qg_ref, wuq_ref, kvg_ref, wk_ref, wv_ref, wcs_ref, tq_ref, tk_ref, rope):
    shift1 = m_ref[0, 0:1, :]
    scale1 = m_ref[0, 1:2, :]
    h = _rms(x, g1_ref[...] * (1.0 + scale1)) + shift1
    proj = lax.dot_general(h.astype(BF16), win_ref[...], _NT, preferred_element_type=F32)
    qn = _rms(proj[:, 0:Q_LORA], qg_ref[...]).astype(BF16)
    qq = lax.dot_general(wuq_ref[...], qn, _NT, preferred_element_type=F32)
    cosq = tq_ref[0:LANES, :]
    sinq = tq_ref[LANES:2 * LANES, :]
    q_heads = []
    for hd in range(N_HEADS):
        lo = hd * HEAD_PAD
        qh = qq[lo:lo + HEAD_PAD, :] * cosq
        if rope:
            rot = qq[QK_W + hd * QK_ROPE:QK_W + (hd + 1) * QK_ROPE, :] * sinq[QK_NOPE:QK_NOPE + QK_ROPE, :]
            qh = qh + jnp.concatenate([jnp.zeros((QK_NOPE, rot.shape[1]), F32), rot,
                                       jnp.zeros((HEAD_PAD - QK_NOPE - QK_ROPE, rot.shape[1]), F32)], axis=0)
        q_heads.append(qh.astype(BF16))
    ckv = _rms(proj[:, Q_LORA:Q_LORA + KV_LORA], kvg_ref[...])
    kpe2 = proj[:, Q_LORA + KV_LORA:Q_LORA + KV_LORA + LANES] * tk_ref[...]
    xk = jnp.concatenate([ckv, kpe2], axis=1).astype(BF16)
    k = _dot(xk, wk_ref[...]).astype(BF16)
    v_t = lax.dot_general(wv_ref[...], xk[:, 0:KV_LORA], _NT, preferred_element_type=F32).astype(BF16)
    hw = FNET_W // 2
    f_in = proj[:, 512:1024].astype(BF16)
    z = [_dot(f_in[:, a * hw:(a + 1) * hw], wcs_ref[a * hw:(a + 1) * hw, :]) for a in range(2)]
    zc = jnp.concatenate([za[:, 0:hw] for za in z], axis=1).astype(BF16)
    zs = jnp.concatenate([za[:, hw:2 * hw] for za in z], axis=1).astype(BF16)
    kpe = proj[:, Q_LORA + KV_LORA:Q_LORA + KV_LORA + LANES]
    return q_heads, k, v_t, zc, zs, ckv, kpe


def _premix_kernel(x_ref, m_ref, g1_ref, win_ref, qg_ref, wuq_ref, kvg_ref, wk_ref, wv_ref, wcs_ref,
                   tq_ref, tk_ref, q_ref, k_ref, v_ref, zc_ref, zs_ref, *, rope):
    q_heads, k, v_t, zc, zs, _, _ = _premix_body(x_ref[...], m_ref, g1_ref, win_ref, qg_ref, wuq_ref, kvg_ref,
                                                 wk_ref, wv_ref, wcs_ref, tq_ref, tk_ref, rope)
    for hd, qh in enumerate(q_heads):
        q_ref[hd * HEAD_PAD:(hd + 1) * HEAD_PAD, :] = qh
    k_ref[...] = k
    v_ref[...] = v_t
    zc_ref[...] = zc
    zs_ref[...] = zs


def _premix(x, mods6, g1, win, qg, wuq, kvg, wk, wv, wcs, tq, tk, *, mod_row, tab_row, rope):
    n = x.shape[0]
    tm = PRE_TILE
    full = lambda a: pl.BlockSpec(a.shape, lambda i: (0,) * a.ndim)
    out_shape = [jax.ShapeDtypeStruct((QK_W, n), BF16), jax.ShapeDtypeStruct((n, QK_W), BF16),
                 jax.ShapeDtypeStruct((V_W, n), BF16), jax.ShapeDtypeStruct((n, FNET_W), BF16),
                 jax.ShapeDtypeStruct((n, FNET_W), BF16)]
    out_specs = [pl.BlockSpec((QK_W, tm), lambda i: (0, i)), pl.BlockSpec((tm, QK_W), lambda i: (i, 0)),
                 pl.BlockSpec((V_W, tm), lambda i: (0, i)), pl.BlockSpec((tm, FNET_W), lambda i: (i, 0)),
                 pl.BlockSpec((tm, FNET_W), lambda i: (i, 0))]
    return pl.pallas_call(
        functools.partial(_premix_kernel, rope=rope),
        grid=(n // tm,),
        in_specs=[pl.BlockSpec((tm, D_MODEL), lambda i: (i, 0)),
                  pl.BlockSpec((1, 6, D_MODEL), lambda i: (mod_row(i), 0, 0)),
                  full(g1), full(win), full(qg), full(wuq), full(kvg), full(wk), full(wv), full(wcs),
                  pl.BlockSpec((2 * LANES, tm), lambda i: (0, tab_row(i))),
                  pl.BlockSpec((tm, LANES), lambda i: (tab_row(i), 0))],
        out_specs=out_specs,
        out_shape=out_shape,
        compiler_params=_cparams(("parallel",)),
        name="premix",
    )(x, mods6, g1, win, qg, wuq, kvg, wk, wv, wcs, tq, tk)


def _cachekv_kernel(xk_ref, wk_ref, wv_ref, k_ref, v_ref):
    xk = xk_ref[...]
    k_ref[...] = _dot(xk, wk_ref[...]).astype(BF16)
    v_ref[...] = lax.dot_general(wv_ref[...], xk[:, 0:KV_LORA], _NT, preferred_element_type=F32).astype(BF16)


def _cachekv(xk, wk, wv):
    n = xk.shape[0]
    tm = 512
    full = lambda a: pl.BlockSpec(a.shape, lambda i: (0,) * a.ndim)
    return pl.pallas_call(
        _cachekv_kernel,
        grid=(n // tm,),
        in_specs=[pl.BlockSpec((tm, 2 * LANES), lambda i: (i, 0)), full(wk), full(wv)],
        out_specs=[pl.BlockSpec((tm, QK_W), lambda i: (i, 0)), pl.BlockSpec((V_W, tm), lambda i: (0, i))],
        out_shape=[jax.ShapeDtypeStruct((n, QK_W), BF16), jax.ShapeDtypeStruct((V_W, n), BF16)],
        compiler_params=_cparams(("parallel",)),
        name="cachekv",
    )(xk, wk, wv)


def _attn_body(q_heads, k_refs, v_refs, kc):
    tq = q_heads[0].shape[1]
    zero = jnp.zeros((HEAD_PAD, tq), BF16)
    n_pairs = len(q_heads) // 2
    qbd = [jnp.concatenate([jnp.concatenate([q_heads[2 * pr], zero], axis=1),
                            jnp.concatenate([zero, q_heads[2 * pr + 1]], axis=1)], axis=0) for pr in range(n_pairs)]
    chunks = [(k_ref, v_ref, c0, min(c0 + kc, k_ref.shape[0]))
              for k_ref, v_ref in zip(k_refs, v_refs) for c0 in range(0, k_ref.shape[0], kc)]
    work = [(pr, ch) for pr in range(n_pairs) for ch in chunks]

    def score(item):
        pr, (k_ref, _, c0, c1) = item
        return _dot(k_ref[c0:c1, pr * 2 * HEAD_PAD:(pr + 1) * 2 * HEAD_PAD], qbd[pr]).astype(BF16)

    m = [None] * n_pairs
    o = [None] * n_pairs
    s_next = score(work[0])
    for wi, (pr, (k_ref, v_ref, c0, c1)) in enumerate(work):
        s = s_next
        if wi + 1 < len(work):
            s_next = score(work[wi + 1])
        cm = jnp.max(s, axis=0, keepdims=True)
        vlo = pr * 2 * V_HEAD
        va = jnp.concatenate([v_ref[vlo:vlo + 2 * V_HEAD, c0:c1], jnp.ones((16, c1 - c0), BF16)], axis=0)
        if m[pr] is None:
            m[pr] = cm
            o[pr] = _dot(va, jnp.exp2(s - cm))
        else:
            m_new = jnp.maximum(m[pr], cm)
            alpha = jnp.exp2(m[pr].astype(F32) - m_new.astype(F32))
            o[pr] = alpha * o[pr] + _dot(va, jnp.exp2(s - m_new))
            m[pr] = m_new
    outs = []
    for pr in range(n_pairs):
        on = o[pr][0:2 * V_HEAD, :] * (1.0 / o[pr][2 * V_HEAD:2 * V_HEAD + 1, :])
        ot = jnp.concatenate([on[0:V_HEAD, 0:tq], on[V_HEAD:2 * V_HEAD, tq:2 * tq]], axis=0)
        outs.append(ot.T.astype(BF16))
    return outs


def _attn_kernel(q_ref, *refs, n_kv, n_pairs, kc):
    q_heads = [q_ref[hd * HEAD_PAD:(hd + 1) * HEAD_PAD, :] for hd in range(2 * n_pairs)]
    outs = _attn_body(q_heads, refs[:n_kv], refs[n_kv:2 * n_kv], kc)
    o_ref = refs[2 * n_kv]
    for pr, o in enumerate(outs):
        o_ref[:, pr * 2 * V_HEAD:(pr + 1) * 2 * V_HEAD] = o


def _attention(q_t, ks, vs_t, *, n_req, t_q, kv_lens, tq, pairs_per_step):
    n_kv = len(ks)
    nq = t_q // tq
    pp = pairs_per_step
    in_specs = [pl.BlockSpec((pp * 2 * HEAD_PAD, tq), lambda b, p, i: (p, b * nq + i))]
    in_specs += [pl.BlockSpec((kl, pp * 2 * HEAD_PAD), lambda b, p, i: (b, p)) for kl in kv_lens]
    in_specs += [pl.BlockSpec((pp * 2 * V_HEAD, kl), lambda b, p, i: (p, b)) for kl in kv_lens]
    return pl.pallas_call(
        functools.partial(_attn_kernel, n_kv=n_kv, n_pairs=pp, kc=KEY_CHUNK),
        grid=(n_req, N_HEADS // 2 // pp, nq),
        in_specs=in_specs,
        out_specs=pl.BlockSpec((tq, pp * 2 * V_HEAD), lambda b, p, i: (b * nq + i, p)),
        out_shape=jax.ShapeDtypeStruct((n_req * t_q, V_W), BF16),
        compiler_params=_cparams(("parallel", "parallel", "parallel")),
        name="attn",
    )(q_t, *ks, *vs_t)


def _mixout_body(x, attn_pairs, fm, wo_ref, m_ref, g2_ref, wr_ref):
    y = _dot(fm, wo_ref[V_W:V_W + FNET_W, :])
    col = 0
    for a in attn_pairs:
        y = y + _dot(a, wo_ref[col:col + a.shape[1], :])
        col += a.shape[1]
    gate1 = m_ref[0, 2:3, :]
    shift2 = m_ref[0, 3:4, :]
    scale2 = m_ref[0, 4:5, :]
    x1 = x + gate1 * y
    h2 = (_rms(x1, g2_ref[...] * (1.0 + scale2)) + shift2).astype(BF16)
    lg = lax.dot_general(wr_ref[...], h2, _NT, preferred_element_type=F32)
    e = jnp.exp(lg - jnp.max(lg, axis=0, keepdims=True))
    return x1, h2, e / jnp.sum(e, axis=0, keepdims=True)


def _mixout_kernel(x_ref, a_ref, zc_ref, zs_ref, cb_ref, sb_ref, off_ref, wo_ref, m_ref, g2_ref, wr_ref,
                   x1_ref, h2_ref, aff_ref, zp_ref, zm_ref, il_ref, *, t):
    half = t // 2
    i = pl.program_id(1)

    @pl.when(i == 0)
    def _fold_halves():
        for src, col in ((zc_ref, 0), (zs_ref, FNET_W)):
            lo = src[0:half, :]
            hi = src[half:t, :]
            zp_ref[:, col:col + FNET_W] = lo + hi
            zm_ref[:, col:col + FNET_W] = lo - hi

    off = off_ref[pl.ds(i, 1), :]
    co = off[:, 0:half]
    so = off[:, half:t]
    cb = cb_ref[...]
    sb = sb_ref[...]
    ct = (cb * co - sb * so).astype(BF16)
    st = (sb * co + cb * so).astype(BF16)
    tr = cb.shape[0]
    h = tr // 2
    even = _dot(ct[0:h, :], zp_ref[:, 0:FNET_W]) - _dot(st[0:h, :], zp_ref[:, FNET_W:2 * FNET_W])
    odd = _dot(ct[h:, :], zm_ref[:, 0:FNET_W]) - _dot(st[h:, :], zm_ref[:, FNET_W:2 * FNET_W])
    for c in range(FNET_W // LANES):
        il_ref[c, pl.ds(0, h, stride=2), :] = even[:, c * LANES:(c + 1) * LANES]
        il_ref[c, pl.ds(1, h, stride=2), :] = odd[:, c * LANES:(c + 1) * LANES]
    fm = jnp.concatenate([il_ref[c] for c in range(FNET_W // LANES)], axis=1).astype(BF16)
    x1, h2, aff = _mixout_body(x_ref[...], [a_ref[...]], fm, wo_ref, m_ref, g2_ref, wr_ref)
    x1_ref[...] = x1
    h2_ref[...] = h2
    aff_ref[...] = aff


def _ctx_front_kernel(x_ref, m_ref, g1_ref, win_ref, qg_ref, wuq_ref, kvg_ref, wk_ref, wv_ref, wcs_ref,
                      tq_ref, tk_ref, ct_ref, st_ref, wo_ref, g2_ref, wr_ref,
                      x1_ref, h2_ref, aff_ref, ckv_ref, kpe_ref, *, kc, t):
    x = x_ref[...]
    q_heads, k, v_t, zc, zs, ckv, kpe = _premix_body(x, m_ref, g1_ref, win_ref, qg_ref, wuq_ref, kvg_ref,
                                                     wk_ref, wv_ref, wcs_ref, tq_ref, tk_ref, False)
    ckv_ref[...] = ckv
    for r in range(x.shape[0] // t):
        kpe_ref[r] = kpe[r * t:(r + 1) * t, :].T[0:QK_ROPE, :]
    attn, fm = [], []
    for r in range(x.shape[0] // t):
        rows = slice(r * t, (r + 1) * t)
        attn.append(jnp.concatenate(_attn_body([q[:, rows] for q in q_heads], [k[rows, :]], [v_t[:, rows]], kc),
                                    axis=1))
        fm.append((_dot(ct_ref[...], zc[rows, :]) - _dot(st_ref[...], zs[rows, :])).astype(BF16))
    x1, h2, aff = _mixout_body(x_ref[...], [jnp.concatenate(attn, axis=0)], jnp.concatenate(fm, axis=0),
                               wo_ref, m_ref, g2_ref, wr_ref)
    x1_ref[...] = x1
    h2_ref[...] = h2
    for r in range(x.shape[0] // t):
        aff_ref[r] = aff[:, r * t:(r + 1) * t]


def _ctx_front(x, mods6, g1, win, qg, wuq, kvg, wk, wv, wcs, tq, tk, ct, st, wo, g2, wr_t, *, n_req, t, mod_row):
    full = lambda a: pl.BlockSpec(a.shape, lambda b: (0,) * a.ndim)
    rps = CTX_REQS
    row = lambda w: pl.BlockSpec((rps * t, w), lambda b: (b, 0))
    return pl.pallas_call(
        functools.partial(_ctx_front_kernel, kc=KEY_CHUNK, t=t),
        grid=(n_req // rps,),
        in_specs=[row(D_MODEL), pl.BlockSpec((1, 6, D_MODEL), lambda b: (mod_row, 0, 0)),
                  full(g1), full(win), full(qg), full(wuq), full(kvg), full(wk), full(wv), full(wcs),
                  full(tq), full(tk), full(ct), full(st), full(wo), full(g2), full(wr_t)],
        out_specs=[row(D_MODEL), row(D_MODEL), pl.BlockSpec((rps, N_EXPERTS, t), lambda b: (b, 0, 0)),
                   row(KV_LORA), pl.BlockSpec((rps, QK_ROPE, t), lambda b: (b, 0, 0))],
        out_shape=[jax.ShapeDtypeStruct((n_req * t, D_MODEL), F32),
                   jax.ShapeDtypeStruct((n_req * t, D_MODEL), BF16),
                   jax.ShapeDtypeStruct((n_req, N_EXPERTS, t), F32),
                   jax.ShapeDtypeStruct((n_req * t, KV_LORA), F32),
                   jax.ShapeDtypeStruct((n_req, QK_ROPE, t), F32)],
        compiler_params=_cparams(("parallel",)),
        name="ctx_front",
    )(x, mods6, g1, win, qg, wuq, kvg, wk, wv, wcs, tq, tk, ct, st, wo, g2, wr_t)


def _mixout(x, attn, zc, zs, cb, sb, off, wo, mods6, g2, wr_t, *, n_req, t, mod_row):
    tr = MIX_TILE
    nr = t // tr
    full = lambda a: pl.BlockSpec(a.shape, lambda b, i: (0,) * a.ndim)
    return pl.pallas_call(
        functools.partial(_mixout_kernel, t=t),
        grid=(n_req, nr),
        in_specs=[pl.BlockSpec((tr, D_MODEL), lambda b, i: (b * nr + i, 0)),
                  pl.BlockSpec((tr, V_W), lambda b, i: (b * nr + i, 0)),
                  pl.BlockSpec((t, FNET_W), lambda b, i: (b, 0)),
                  pl.BlockSpec((t, FNET_W), lambda b, i: (b, 0)),
                  full(cb), full(sb), full(off),
                  full(wo),
                  pl.BlockSpec((1, 6, D_MODEL), lambda b, i: (mod_row(b), 0, 0)),
                  full(g2), full(wr_t)],
        out_specs=[pl.BlockSpec((tr, D_MODEL), lambda b, i: (b * nr + i, 0)),
                   pl.BlockSpec((tr, D_MODEL), lambda b, i: (b * nr + i, 0)),
                   pl.BlockSpec((None, N_EXPERTS, tr), lambda b, i: (b, 0, i))],
        out_shape=[jax.ShapeDtypeStruct((n_req * t, D_MODEL), F32),
                   jax.ShapeDtypeStruct((n_req * t, D_MODEL), BF16),
                   jax.ShapeDtypeStruct((n_req, N_EXPERTS, t), F32)],
        scratch_shapes=[pltpu.VMEM((t // 2, 2 * FNET_W), BF16), pltpu.VMEM((t // 2, 2 * FNET_W), BF16),
                        pltpu.VMEM((FNET_W // LANES, tr, LANES), F32)],
        compiler_params=_cparams(("parallel", "arbitrary")),
        name="mixout",
    )(x, attn, zc, zs, cb, sb, off, wo, mods6, g2, wr_t)


def _prefix_count(flags, tri):
    n = flags.shape[1]
    carry = None
    outs = []
    ends = []
    for j in range(n // TOK_TILE):
        c = _dot(flags[:, j * TOK_TILE:(j + 1) * TOK_TILE].astype(BF16), tri)
        if carry is not None:
            c = c + carry
        outs.append(c)
        carry = c[:, TOK_TILE - 1:TOK_TILE]
        ends.append(carry)
    return (outs[0] if len(outs) == 1 else jnp.concatenate(outs, axis=1)), ends


def _route_kernel(aff_ref, pos_ref, off_ref, *, cap):
    a = aff_ref[...]
    rows = a.shape[0]
    capf = jnp.float32(cap)
    thr = jnp.zeros((rows, 1), jnp.int32)
    for bit in range(30, -1, -1):
        cand = thr | jnp.int32(1 << bit)
        cand_f = lax.bitcast_convert_type(cand, F32)
        cnt = jnp.sum(jnp.where(a >= cand_f, 1.0, 0.0), axis=1, keepdims=True)
        thr = jnp.where(cnt >= capf, cand, thr)
    thr_f = lax.bitcast_convert_type(thr, F32)
    above_f = lax.bitcast_convert_type(thr + 1, F32)
    gt = jnp.where(a >= above_f, 1.0, 0.0)
    tie = jnp.where(a >= thr_f, 1.0, 0.0) - gt
    need = capf - jnp.sum(gt, axis=1, keepdims=True)
    r_i = lax.broadcasted_iota(jnp.int32, (TOK_TILE, TOK_TILE), 0)
    c_i = lax.broadcasted_iota(jnp.int32, (TOK_TILE, TOK_TILE), 1)
    tri = jnp.where(r_i <= c_i, 1.0, 0.0).astype(BF16)
    tie_before = _prefix_count(tie, tri)[0] - tie
    sel = gt + tie * jnp.where(tie_before < need, 1.0, 0.0)
    count, ends = _prefix_count(sel, tri)
    pos_ref[...] = jnp.where(sel > 0.5, count - 1.0, -1.0)
    lane = lax.broadcasted_iota(jnp.int32, (rows, LANES), 1)
    offs = jnp.zeros((rows, LANES), F32)
    for j, end in enumerate(ends):
        offs = offs + jnp.where(lane == j + 1, end, 0.0)
    off_ref[...] = offs.astype(jnp.int32)


def _route(aff_t, cap):
    return pl.pallas_call(
        functools.partial(_route_kernel, cap=cap),
        out_shape=[jax.ShapeDtypeStruct(aff_t.shape, F32),
                   jax.ShapeDtypeStruct((aff_t.shape[0], LANES), jnp.int32)],
        compiler_params=pltpu.CompilerParams(vmem_limit_bytes=VMEM_LIMIT),
        name="route",
    )(aff_t)


def _gather_kernel(pos_ref, aff_ref, h_ref, xs_ref, g_ref, *, cap):
    rps, ne, n = pos_ref.shape
    slot = lax.broadcasted_iota(jnp.int32, (ne, cap, n), 1).astype(F32)
    for r in range(rps):
        pos = pos_ref[r]
        aff = aff_ref[r]
        hit = pos[:, None, :] == slot
        onehot = jnp.where(hit, 1.0, 0.0).reshape(ne * cap, n).astype(BF16)
        xs = _dot(onehot, h_ref[r * n:(r + 1) * n, :])
        xs_ref[:, r * cap:(r + 1) * cap, :] = xs.astype(BF16).reshape(ne, cap, D_MODEL)
        g_ref[:, r * cap:(r + 1) * cap, :] = jnp.sum(jnp.where(hit, aff[:, None, :], 0.0), axis=2, keepdims=True)


def _gather(pos_t, aff_t, h2, *, n_req, n, cap, rps):
    pos3 = pos_t.reshape(n_req, N_EXPERTS, n)
    aff3 = aff_t.reshape(n_req, N_EXPERTS, n)
    return pl.pallas_call(
        functools.partial(_gather_kernel, cap=cap),
        grid=(n_req // rps,),
        in_specs=[pl.BlockSpec((rps, N_EXPERTS, n), lambda b: (b, 0, 0)),
                  pl.BlockSpec((rps, N_EXPERTS, n), lambda b: (b, 0, 0)),
                  pl.BlockSpec((rps * n, D_MODEL), lambda b: (b, 0))],
        out_specs=[pl.BlockSpec((N_EXPERTS, rps * cap, D_MODEL), lambda b: (0, b, 0)),
                   pl.BlockSpec((N_EXPERTS, rps * cap, 1), lambda b: (0, b, 0))],
        out_shape=[jax.ShapeDtypeStruct((N_EXPERTS, n_req * cap, D_MODEL), BF16),
                   jax.ShapeDtypeStruct((N_EXPERTS, n_req * cap, 1), F32)],
        compiler_params=_cparams(("parallel",)),
        name="gather",
    )(pos3, aff3, h2)


def _window_plan(off_ref, b, j, cap, w):
    w0 = []
    need = jnp.int32(0)
    for e in range(N_EXPERTS):
        idx = (b * N_EXPERTS + e) * OFF_STRIDE + j
        base = (off_ref[idx] >> 4) << 4
        w0.append(base)
        need = jnp.maximum(need, off_ref[idx + 1] - base)
    return w0, (need + (w - 1)) >> (w.bit_length() - 1)


def _window(w0_e, p, cap, w):
    low = w0_e + p * w
    start = pl.multiple_of(jnp.minimum(low, cap - w), 16)
    return low, start


def _gather_win_kernel(off_ref, pos_ref, aff_ref, h_ref, xs_ref, g_ref, *, cap, w):
    b = pl.program_id(0)
    step = pl.program_id(1)

    @pl.when(step == 0)
    def _init():
        xs_ref[...] = jnp.zeros_like(xs_ref)
        g_ref[...] = jnp.zeros_like(g_ref)

    r = lax.broadcasted_iota(jnp.int32, (w, 1), 0).astype(F32)
    n_sub = pos_ref.shape[1] // TOK_TILE
    for sub in range(n_sub):
        cols = slice(sub * TOK_TILE, (sub + 1) * TOK_TILE)
        pos = pos_ref[:, cols]
        aff = aff_ref[:, cols]
        h = h_ref[cols, :]
        w0, n_pass = _window_plan(off_ref, b, step * n_sub + sub, cap, w)

        def one_pass(p, carry, pos=pos, aff=aff, h=h, w0=w0):
            starts, hots, gates = [], [], []
            for e in range(N_EXPERTS):
                low, start = _window(w0[e], p, cap, w)
                starts.append(start)
                mine = jnp.where(r >= (low - start).astype(F32), 1.0, 0.0)
                hot = jnp.where(pos[e:e + 1, :] - start.astype(F32) == r, mine, 0.0)
                hots.append(hot)
                gates.append(jnp.sum(hot * aff[e:e + 1, :], axis=1, keepdims=True))
            rows = _dot(jnp.concatenate(hots, axis=0).astype(BF16), h).astype(BF16)
            for e in range(N_EXPERTS):
                win = pl.ds(starts[e], w)
                xs_ref[e, win, :] = xs_ref[e, win, :] + rows[e * w:(e + 1) * w, :]
                g_ref[e, win, :] = g_ref[e, win, :] + gates[e]
            return carry

        one_pass(0, 0)
        lax.fori_loop(1, n_pass, one_pass, 0)


def _gather_win(offs, pos_t, aff_t, h2, *, n_req, n, cap):
    tb = WIN_BLOCKS * TOK_TILE
    nb = n // tb
    grid_spec = pltpu.PrefetchScalarGridSpec(
        num_scalar_prefetch=1,
        grid=(n_req, nb),
        in_specs=[pl.BlockSpec((N_EXPERTS, tb), lambda b, j, off: (b, j)),
                  pl.BlockSpec((N_EXPERTS, tb), lambda b, j, off: (b, j)),
                  pl.BlockSpec((tb, D_MODEL), lambda b, j, off: (b * nb + j, 0))],
        out_specs=[pl.BlockSpec((N_EXPERTS, cap, D_MODEL), lambda b, j, off: (0, b, 0)),
                   pl.BlockSpec((N_EXPERTS, cap, 1), lambda b, j, off: (0, b, 0))])
    return pl.pallas_call(
        functools.partial(_gather_win_kernel, cap=cap, w=SLOT_WIN),
        grid_spec=grid_spec,
        out_shape=[jax.ShapeDtypeStruct((N_EXPERTS, n_req * cap, D_MODEL), BF16),
                   jax.ShapeDtypeStruct((N_EXPERTS, n_req * cap, 1), F32)],
        compiler_params=_cparams(("parallel", "arbitrary")),
        name="gather_win",
    )(offs, pos_t, aff_t, h2)


def _ffn_kernel(xc_ref, xl_ref, gc_ref, gl_ref, wg_ref, wu_ref, wd_ref, yc_ref, yl_ref, wgb, wub, wdb):
    wgb[...] = wg_ref[0].astype(BF16)
    wub[...] = wu_ref[0].astype(BF16)
    wdb[...] = wd_ref[0].astype(BF16)
    for x_ref, g_ref, y_ref in ((xc_ref, gc_ref, yc_ref), (xl_ref, gl_ref, yl_ref)):
        for j in range(x_ref.shape[1] // FFN_ROWS):
            rows = slice(j * FFN_ROWS, (j + 1) * FFN_ROWS)
            x = x_ref[0, rows, :]
            gate = _dot(x, wgb[...])
            up = _dot(x, wub[...])
            hid = (gate * jax.nn.sigmoid(gate) * up).astype(BF16)
            ys = _dot(hid, wdb[...]) * g_ref[0, rows, :]
            y_ref[0, rows, :] = ys.astype(BF16)


def _ffn(xc, xl, gc, gl, wg, wu, wd):
    m = xc.shape[1]
    xspec = pl.BlockSpec((1, m, D_MODEL), lambda e: (e, 0, 0))
    gspec = pl.BlockSpec((1, m, 1), lambda e: (e, 0, 0))
    shp = jax.ShapeDtypeStruct((N_EXPERTS, m, D_MODEL), BF16)
    return pl.pallas_call(
        _ffn_kernel,
        grid=(N_EXPERTS,),
        in_specs=[xspec, xspec, gspec, gspec,
                  pl.BlockSpec((1, D_MODEL, D_EXPERT), lambda e: (e, 0, 0)),
                  pl.BlockSpec((1, D_MODEL, D_EXPERT), lambda e: (e, 0, 0)),
                  pl.BlockSpec((1, D_EXPERT, D_MODEL), lambda e: (e, 0, 0))],
        out_specs=[xspec, xspec],
        out_shape=[shp, shp],
        scratch_shapes=[pltpu.VMEM((D_MODEL, D_EXPERT), BF16), pltpu.VMEM((D_MODEL, D_EXPERT), BF16),
                        pltpu.VMEM((D_EXPERT, D_MODEL), BF16)],
        compiler_params=_cparams(("arbitrary",)),
        name="ffn",
    )(xc, xl, gc, gl, wg, wu, wd)


def _combine_kernel(x1_ref, pos_ref, ys_ref, m_ref, fg_ref, o_ref, *, cap):
    rps, n, _ = pos_ref.shape
    w = N_EXPERTS * cap
    e_i = lax.broadcasted_iota(jnp.int32, (N_EXPERTS, w), 0)
    j_i = lax.broadcasted_iota(jnp.int32, (N_EXPERTS, w), 1)
    spread = jnp.where((j_i >> (cap.bit_length() - 1)) == e_i, 1.0, 0.0).astype(BF16)
    lane_slot = (lax.broadcasted_iota(jnp.int32, (1, w), 1) & (cap - 1)).astype(F32)
    gate2 = m_ref[0, 5:6, :]
    for r in range(rps):
        pos = pos_ref[r].astype(BF16)
        onehot = jnp.where(_dot(pos, spread) == lane_slot, 1.0, 0.0).astype(BF16)
        acc = _dot(onehot, ys_ref[:, r * cap:(r + 1) * cap, :].reshape(w, D_MODEL))
        rows = slice(r * n, (r + 1) * n)
        o_ref[rows, :] = _rms(x1_ref[rows, :] + gate2 * acc, fg_ref[...])


def _combine(x1, pos_tok, ys, mods6, fg, *, n_req, n, cap, mod_row, rps):
    return pl.pallas_call(
        functools.partial(_combine_kernel, cap=cap),
        grid=(n_req // rps,),
        in_specs=[pl.BlockSpec((rps * n, D_MODEL), lambda b: (b, 0)),
                  pl.BlockSpec((rps, n, N_EXPERTS), lambda b: (b, 0, 0)),
                  pl.BlockSpec((N_EXPERTS, rps * cap, D_MODEL), lambda b: (0, b, 0)),
                  pl.BlockSpec((1, 6, D_MODEL), lambda b: (mod_row, 0, 0)),
                  pl.BlockSpec((1, D_MODEL), lambda b: (0, 0))],
        out_specs=pl.BlockSpec((rps * n, D_MODEL), lambda b: (b, 0)),
        out_shape=jax.ShapeDtypeStruct((n_req * n, D_MODEL), F32),
        compiler_params=_cparams(("parallel",)),
        name="combine",
    )(x1, pos_tok, ys, mods6, fg)


def _combine_win_kernel(off_ref, x1_ref, pos_ref, ys_ref, m_ref, fg_ref, o_ref, acc_ref, *, cap, w):
    b = pl.program_id(0)
    step = pl.program_id(1)
    width = N_EXPERTS * w
    e_i = lax.broadcasted_iota(jnp.int32, (N_EXPERTS, width), 0)
    j_i = lax.broadcasted_iota(jnp.int32, (N_EXPERTS, width), 1)
    spread = jnp.where((j_i >> (w.bit_length() - 1)) == e_i, 1.0, 0.0).astype(BF16)
    lane_slot = (lax.broadcasted_iota(jnp.int32, (1, width), 1) & (w - 1)).astype(F32)
    lane_e = lax.broadcasted_iota(jnp.int32, (1, N_EXPERTS), 1)
    n_sub = pos_ref.shape[0] // TOK_TILE
    for sub in range(n_sub):
        rows = slice(sub * TOK_TILE, (sub + 1) * TOK_TILE)
        pos = pos_ref[rows, :]
        w0, n_pass = _window_plan(off_ref, b, step * n_sub + sub, cap, w)

        def window_sum(p, pos=pos, w0=w0):
            start_row = jnp.zeros((1, N_EXPERTS), F32)
            first_row = jnp.zeros((1, N_EXPERTS), F32)
            wins = []
            for e in range(N_EXPERTS):
                low, start = _window(w0[e], p, cap, w)
                start_row = jnp.where(lane_e == e, start.astype(F32), start_row)
                first_row = jnp.where(lane_e == e, (low - start).astype(F32), first_row)
                wins.append(ys_ref[e, pl.ds(start, w), :])
            rel = pos - start_row
            rel = jnp.where(rel >= first_row, rel, -1.0).astype(BF16)
            onehot = jnp.where(_dot(rel, spread) == lane_slot, 1.0, 0.0).astype(BF16)
            return _dot(onehot, jnp.concatenate(wins, axis=0))

        def finish(acc, rows=rows):
            o_ref[rows, :] = _rms(x1_ref[rows, :] + m_ref[0, 5:6, :] * acc, fg_ref[...])

        acc0 = window_sum(0)
        acc_ref[sub] = acc0
        finish(acc0)

        @pl.when(n_pass > 1)
        def _more_passes(sub=sub, n_pass=n_pass, window_sum=window_sum, finish=finish):
            def one_pass(p, carry):
                acc_ref[sub] += window_sum(p)
                return carry

            lax.fori_loop(1, n_pass, one_pass, 0)
            finish(acc_ref[sub])


def _combine_win(offs, x1, pos_tok, ys, mods6, fg, *, n_req, n, cap, mod_row):
    tr = WIN_BLOCKS * TOK_TILE
    nr = n // tr
    grid_spec = pltpu.PrefetchScalarGridSpec(
        num_scalar_prefetch=1,
        grid=(n_req, nr),
        in_specs=[pl.BlockSpec((tr, D_MODEL), lambda b, i, off: (b * nr + i, 0)),
                  pl.BlockSpec((None, tr, N_EXPERTS), lambda b, i, off: (b, i, 0)),
                  pl.BlockSpec((N_EXPERTS, cap, D_MODEL), lambda b, i, off: (0, b, 0)),
                  pl.BlockSpec((1, 6, D_MODEL), lambda b, i, off: (mod_row(b), 0, 0)),
                  pl.BlockSpec((1, D_MODEL), lambda b, i, off: (0, 0))],
        out_specs=pl.BlockSpec((tr, D_MODEL), lambda b, i, off: (b * nr + i, 0)),
        scratch_shapes=[pltpu.VMEM((WIN_BLOCKS, TOK_TILE, D_MODEL), F32)])
    return pl.pallas_call(
        functools.partial(_combine_win_kernel, cap=cap, w=SLOT_WIN),
        grid_spec=grid_spec,
        out_shape=jax.ShapeDtypeStruct((n_req * n, D_MODEL), F32),
        compiler_params=_cparams(("parallel", "parallel")),
        name="combine_win",
    )(offs, x1, pos_tok, ys, mods6, fg)


def _rot_half(w):
    half = QK_ROPE // 2
    return jnp.concatenate([-w[..., half:], w[..., :half]], axis=-1)


def _rope_tables(t):
    n_rows = t // GRID_W
    rows = np.repeat(np.arange(n_rows, dtype=np.float64), GRID_W)
    cols = np.tile(np.arange(GRID_W, dtype=np.float64), n_rows)
    n_freq = QK_ROPE // 4
    inv_freq = ROPE_BASE ** (-np.arange(n_freq, dtype=np.float64) / n_freq)
    ang = np.concatenate([rows[:, None] * inv_freq, cols[:, None] * inv_freq], axis=-1)
    cos = np.concatenate([np.cos(ang), np.cos(ang)], axis=-1)
    sin = np.concatenate([np.sin(ang), np.sin(ang)], axis=-1)
    return cos, sin


def _qk_tables(cos, sin):
    t = cos.shape[0]
    scale = (QK_NOPE + QK_ROPE) ** -0.5 * np.log2(np.e)
    pad = np.zeros((t, HEAD_PAD - QK_NOPE - QK_ROPE))
    cosq = np.concatenate([np.full((t, QK_NOPE), scale), cos * scale, pad], axis=1)
    sinq = np.concatenate([np.zeros((t, QK_NOPE)), sin * scale, pad], axis=1)
    tq_t = np.concatenate([cosq, sinq], axis=1).T
    tk = np.concatenate([cos, sin, np.zeros((t, LANES - 2 * QK_ROPE))], axis=1)
    return jnp.asarray(tq_t, F32), jnp.asarray(tk, F32)


def _dft_angles(rows, t):
    k = np.arange(t, dtype=np.int64)
    return ((rows[:, None] * k[None, :]) % t).astype(np.float64) * (2.0 * np.pi / t)


def _dft_tables(t):
    ang = _dft_angles(np.arange(t, dtype=np.int64), t)
    scale = (t * FNET_CH) ** -0.5
    return jnp.asarray(np.cos(ang) * scale, F32).astype(BF16), jnp.asarray(np.sin(ang) * scale, F32).astype(BF16)


def _dft_half_tables(t):
    r = np.arange(MIX_TILE, dtype=np.int64)
    ang = _dft_angles(np.concatenate([r[0::2], r[1::2]]), t)[:, :t // 2]
    scale = (t * FNET_CH) ** -0.5
    ang_off = _dft_angles(np.arange(t // MIX_TILE, dtype=np.int64) * MIX_TILE, t)[:, :t // 2]
    off = np.concatenate([np.cos(ang_off), np.sin(ang_off)], axis=1)
    return jnp.asarray(np.cos(ang) * scale, F32), jnp.asarray(np.sin(ang) * scale, F32), jnp.asarray(off, F32)


def _block_diag(w):
    g, a, b = w.shape
    eye = jnp.eye(g, dtype=w.dtype)
    return (eye[:, None, :, None] * w[:, :, None, :]).reshape(g * a, g * b)


def kernel(x_prompt, x_sample, cache_ckv, cache_kpe, c, c_ctx, w_mod, b_mod, norm1_g, w_in, q_norm_g, w_uq,
           kv_norm_g, w_ukv, w_fmix, w_out, norm2_g, w_router, w_e_gate, w_e_up, w_e_down, final_g):
    assert w_mod.shape[0] == 1, "single-layer problem"
    n_ctx, t_ctx, _ = x_prompt.shape
    n_lat, t_lat, _ = x_sample.shape
    past = cache_ckv.shape[2]
    ctx_row = n_lat

    win = _winprep(jnp.swapaxes(w_in, 1, 2)[0])
    wq3 = w_uq[0].reshape(Q_LORA, N_HEADS, QK_NOPE + QK_ROPE)
    qpad = jnp.zeros((Q_LORA, N_HEADS, HEAD_PAD - QK_NOPE - QK_ROPE), F32)
    wuq_main = jnp.concatenate([wq3, qpad], axis=2).reshape(Q_LORA, QK_W)
    wuq_rot = _rot_half(wq3[..., QK_NOPE:]).reshape(Q_LORA, N_HEADS * QK_ROPE)
    wuq_lat = jnp.concatenate([wuq_main, wuq_rot], axis=1).T.astype(BF16)
    wuq_ctx = wuq_main.T.astype(BF16)
    wkv3 = w_ukv[0].reshape(KV_LORA, N_HEADS, QK_NOPE + V_HEAD)
    wk_top = jnp.concatenate([wkv3[..., :QK_NOPE], jnp.zeros((KV_LORA, N_HEADS, HEAD_PAD - QK_NOPE), F32)],
                             axis=2).reshape(KV_LORA, QK_W)
    place = jnp.concatenate([jnp.zeros((QK_ROPE, QK_NOPE), F32), jnp.eye(QK_ROPE, dtype=F32),
                             jnp.zeros((QK_ROPE, HEAD_PAD - QK_NOPE - QK_ROPE), F32)], axis=1)
    place = jnp.tile(place, (1, N_HEADS))
    wk = jnp.concatenate([wk_top, place, place, jnp.zeros((LANES - 2 * QK_ROPE, QK_W), F32)], axis=0).astype(BF16)
    wv = wkv3[..., QK_NOPE:].reshape(KV_LORA, V_W).T.astype(BF16)
    wo = w_out[0].astype(BF16)
    wr_t = w_router[0].T.astype(BF16)

    cos, sin = _rope_tables(t_lat)
    tq_lat, tk_lat = _qk_tables(cos, sin)
    assert n_ctx % CTX_REQS == 0
    tq_ctx, tk_ctx = _qk_tables(np.ones((CTX_REQS * t_ctx, QK_ROPE)), np.zeros((CTX_REQS * t_ctx, QK_ROPE)))
    ch_ang = _dft_angles(np.arange(FNET_CH, dtype=np.int64), FNET_CH)
    dft_ctx = _dft_tables(t_ctx)
    dft_lat = _dft_half_tables(t_lat)

    assert n_lat + 1 <= SUBLANES
    c8 = jnp.concatenate([c, c_ctx[None, :], jnp.zeros((SUBLANES - n_lat - 1, D_MODEL), F32)], axis=0)
    mods6 = _mods(c8, w_mod[0], b_mod[0][None, :]).reshape(SUBLANES, 6, D_MODEL)
    cw, sw = _fold(jnp.asarray(np.cos(ch_ang), F32), jnp.asarray(np.sin(ch_ang), F32), w_fmix[0])
    gh = FNET_GROUPS // 2
    wcs = jnp.concatenate([jnp.concatenate([_block_diag(cw[a * gh:(a + 1) * gh]), _block_diag(sw[a * gh:(a + 1) * gh])],
                                           axis=1) for a in range(2)], axis=0).astype(BF16)

    g1 = norm1_g[0][None, :]
    qg = q_norm_g[0][None, :]
    kvg = kv_norm_g[0][None, :]
    g2 = norm2_g[0][None, :]
    fg = final_g[None, :]

    xp = x_prompt.reshape(n_ctx * t_ctx, D_MODEL)
    xs = x_sample.reshape(n_lat * t_lat, D_MODEL)
    tiles_lat = t_lat // PRE_TILE

    x1c, h2c, affc, ckv_c, kpe_c = _ctx_front(
        xp, mods6, g1, win, qg, wuq_ctx, kvg, wk, wv, wcs, tq_ctx, tk_ctx,
        *dft_ctx, wo, g2, wr_t, n_req=n_ctx, t=t_ctx, mod_row=ctx_row)
    ql, kl, vl, zcl, zsl = _premix(
        xs, mods6, g1, win, qg, wuq_lat, kvg, wk, wv, wcs, tq_lat, tk_lat,
        mod_row=lambda i: i // tiles_lat, tab_row=lambda i: i % tiles_lat, rope=True)
    xk_cache = jnp.concatenate([cache_ckv[:, 0], cache_kpe[:, 0],
                                jnp.zeros((n_lat, past, 2 * LANES - KV_LORA - QK_ROPE), F32)],
                               axis=-1).reshape(n_lat * past, 2 * LANES).astype(BF16)
    kpast, vpast = _cachekv(xk_cache, wk, wv)

    attn_l = _attention(ql, [kpast, kl], [vpast, vl], n_req=n_lat, t_q=t_lat, kv_lens=[past, t_lat], tq=TOK_TILE,
                        pairs_per_step=2)

    x1l, h2l, affl = _mixout(xs, attn_l, zcl, zsl, *dft_lat, wo, mods6, g2, wr_t,
                             n_req=n_lat, t=t_lat, mod_row=lambda b: b)

    cap_c = CAP_FACTOR * t_ctx // N_EXPERTS
    cap_l = CAP_FACTOR * t_lat // N_EXPERTS
    affc2 = affc.reshape(n_ctx * N_EXPERTS, t_ctx)
    affl2 = affl.reshape(n_lat * N_EXPERTS, t_lat)
    posc, _ = _route(affc2, cap_c)
    posl, offl = _route(affl2, cap_l)
    assert t_lat // TOK_TILE + 1 <= OFF_STRIDE and cap_l % SLOT_WIN == 0
    offl = offl[:, :OFF_STRIDE].reshape(-1)
    xsc, gc = _gather(posc, affc2, h2c, n_req=n_ctx, n=t_ctx, cap=cap_c, rps=MOE_REQS)
    xsl, gl = _gather_win(offl, posl, affl2, h2l, n_req=n_lat, n=t_lat, cap=cap_l)
    ysc, ysl = _ffn(xsc, xsl, gc, gl, w_e_gate[0], w_e_up[0], w_e_down[0])

    posc_tok = posc.reshape(n_ctx, N_EXPERTS, t_ctx).transpose(0, 2, 1)
    posl_tok = posl.reshape(n_lat, N_EXPERTS, t_lat).transpose(0, 2, 1)
    y_prompt = _combine(x1c, posc_tok, ysc, mods6, fg, n_req=n_ctx, n=t_ctx, cap=cap_c, mod_row=ctx_row,
                        rps=MOE_REQS)
    y_sample = _combine_win(offl, x1l, posl_tok, ysl, mods6, fg, n_req=n_lat, n=t_lat, cap=cap_l,
                            mod_row=lambda b: b)

    return (y_prompt.reshape(n_ctx, t_ctx, D_MODEL), y_sample.reshape(n_lat, t_lat, D_MODEL),
            ckv_c.reshape(n_ctx, 1, t_ctx, KV_LORA), kpe_c.transpose(0, 2, 1).reshape(n_ctx, 1, t_ctx, QK_ROPE))
```

```python
import functools

import jax
import jax.numpy as jnp
import numpy as np
from jax import lax
from jax.experimental import pallas as pl
from jax.experimental.pallas import tpu as pltpu

F32 = jnp.float32
BF16 = jnp.bfloat16

D_MODEL = 1024
N_HEADS = 8
QK_NOPE = 64
QK_ROPE = 32
V_HEAD = 64
Q_LORA = 256
KV_LORA = 128
FNET_GROUPS = 8
FNET_CH = 64
FNET_W = FNET_GROUPS * FNET_CH
N_EXPERTS = 16
CAP_FACTOR = 2
D_EXPERT = 512
GRID_W = 64
ROPE_BASE = 10000.0
EPS = 1e-6

LANES = 128
SUBLANES = 8
HEAD_PAD = LANES
QK_W = N_HEADS * HEAD_PAD
V_W = N_HEADS * V_HEAD
TOK_TILE = 256
PRE_TILE = 1024
MIX_TILE = 1024
FFN_ROWS = 1024
MODS_TILE = 1536
CTX_REQS = 4
MOE_REQS = 4
KEY_CHUNK = 512
SLOT_WIN = 64
WIN_BLOCKS = 4
OFF_STRIDE = 16
VMEM_LIMIT = 48 * 1024 * 1024
MIX_VMEM_LIMIT = 56 * 1024 * 1024

_NT = (((1,), (1,)), ((), ()))


def _cparams(sem):
    return pltpu.CompilerParams(dimension_semantics=sem, vmem_limit_bytes=VMEM_LIMIT)


def _rms(x, g):
    return x * lax.rsqrt(jnp.mean(x * x, axis=-1, keepdims=True) + EPS) * g


def _dot(a, b):
    return jnp.dot(a, b, preferred_element_type=F32)


def _mods_kernel(c_ref, w_ref, b_ref, o_ref):
    c = c_ref[...]
    s = c * jax.nn.sigmoid(c)
    o_ref[...] = _dot(s.astype(BF16), w_ref[...].astype(BF16)) + b_ref[...]


def _mods(c8, w_mod, b_mod):
    n = w_mod.shape[1]
    tn = MODS_TILE
    return pl.pallas_call(
        _mods_kernel,
        grid=(n // tn,),
        in_specs=[pl.BlockSpec((SUBLANES, D_MODEL), lambda j: (0, 0)),
                  pl.BlockSpec((D_MODEL, tn), lambda j: (0, j)),
                  pl.BlockSpec((1, tn), lambda j: (0, j))],
        out_specs=pl.BlockSpec((SUBLANES, tn), lambda j: (0, j)),
        out_shape=jax.ShapeDtypeStruct((SUBLANES, n), F32),
        compiler_params=_cparams(("arbitrary",)),
        name="mods",
    )(c8, w_mod, b_mod)


def _fold_kernel(cc_ref, sc_ref, w_ref, cw_ref, sw_ref):
    for g in range(FNET_GROUPS):
        w = w_ref[g]
        cw_ref[g] = jnp.dot(cc_ref[...], w, preferred_element_type=F32, precision=lax.Precision.HIGHEST)
        sw_ref[g] = jnp.dot(sc_ref[...], w, preferred_element_type=F32, precision=lax.Precision.HIGHEST)


def _fold(cc, sc, w_fmix):
    shp = jax.ShapeDtypeStruct((FNET_GROUPS, FNET_CH, FNET_CH), F32)
    return pl.pallas_call(_fold_kernel, out_shape=(shp, shp), name="fold")(cc, sc, w_fmix)


KPE_LO = Q_LORA + KV_LORA
PROJ_W = 1024


def _winprep_kernel(wt_ref, o_ref):
    half = QK_ROPE // 2
    kpe_hi = KPE_LO + QK_ROPE
    o_ref[0:kpe_hi, :] = wt_ref[0:kpe_hi, :].astype(BF16)
    o_ref[kpe_hi:kpe_hi + half, :] = (-wt_ref[KPE_LO + half:kpe_hi, :]).astype(BF16)
    o_ref[kpe_hi + half:kpe_hi + QK_ROPE, :] = wt_ref[KPE_LO:KPE_LO + half, :].astype(BF16)
    o_ref[kpe_hi + QK_ROPE:PROJ_W - FNET_W, :] = jnp.zeros((PROJ_W - FNET_W - kpe_hi - QK_ROPE, D_MODEL), BF16)
    o_ref[PROJ_W - FNET_W:PROJ_W, :] = wt_ref[kpe_hi:kpe_hi + FNET_W, :].astype(BF16)


def _winprep(w_in_t):
    assert w_in_t.shape == (KPE_LO + QK_ROPE + FNET_W, D_MODEL)
    return pl.pallas_call(
        _winprep_kernel,
        out_shape=jax.ShapeDtypeStruct((PROJ_W, D_MODEL), BF16),
        compiler_params=pltpu.CompilerParams(vmem_limit_bytes=VMEM_LIMIT),
        name="winprep",
    )(w_in_t)


def _premix_body(x, m_ref, g1_ref, win_ref, qg_ref, wuq_ref, kvg_ref, wk_ref, wv_ref, wcs_ref, tq_ref, tk_ref, rope):
    shift1 = m_ref[0, 0:1, :]
    scale1 = m_ref[0, 1:2, :]
    h = _rms(x, g1_ref[...] * (1.0 + scale1)) + shift1
    proj = lax.dot_general(h.astype(BF16), win_ref[...], _NT, preferred_element_type=F32)
    qn = _rms(proj[:, 0:Q_LORA], qg_ref[...]).astype(BF16)
    qq = lax.dot_general(wuq_ref[...], qn, _NT, preferred_element_type=F32)
    cosq = tq_ref[0:LANES, :]
    sinq = tq_ref[LANES:2 * LANES, :]
    q_heads = []
    for hd in range(N_HEADS):
        lo = hd * HEAD_PAD
        qh = qq[lo:lo + HEAD_PAD, :] * cosq
        if rope:
            rot = qq[QK_W + hd * QK_ROPE:QK_W + (hd + 1) * QK_ROPE, :] * sinq[QK_NOPE:QK_NOPE + QK_ROPE, :]
            qh = qh + jnp.concatenate([jnp.zeros((QK_NOPE, rot.shape[1]), F32), rot,
                                       jnp.zeros((HEAD_PAD - QK_NOPE - QK_ROPE, rot.shape[1]), F32)], axis=0)
        q_heads.append(qh.astype(BF16))
    ckv = _rms(proj[:, Q_LORA:Q_LORA + KV_LORA], kvg_ref[...])
    kpe2 = proj[:, Q_LORA + KV_LORA:Q_LORA + KV_LORA + LANES] * tk_ref[...]
    xk = jnp.concatenate([ckv, kpe2], axis=1).astype(BF16)
    k = _dot(xk, wk_ref[...]).astype(BF16)
    v_t = lax.dot_general(wv_ref[...], xk[:, 0:KV_LORA], _NT, preferred_element_type=F32).astype(BF16)
    hw = FNET_W // 2
    f_in = proj[:, 512:1024].astype(BF16)
    z = [_dot(f_in[:, a * hw:(a + 1) * hw], wcs_ref[a * hw:(a + 1) * hw, :]) for a in range(2)]
    zc = jnp.concatenate([za[:, 0:hw] for za in z], axis=1).astype(BF16)
    zs = jnp.concatenate([za[:, hw:2 * hw] for za in z], axis=1).astype(BF16)
    kpe = proj[:, Q_LORA + KV_LORA:Q_LORA + KV_LORA + LANES]
    return q_heads, k, v_t, zc, zs, ckv, kpe


def _premix_kernel(x_ref, m_ref, g1_ref, win_ref, qg_ref, wuq_ref, kvg_ref, wk_ref, wv_ref, wcs_ref,
                   tq_ref, tk_ref, q_ref, k_ref, v_ref, zc_ref, zs_ref, *, rope):
    q_heads, k, v_t, zc, zs, _, _ = _premix_body(x_ref[...], m_ref, g1_ref, win_ref, qg_ref, wuq_ref, kvg_ref,
                                                 wk_ref, wv_ref, wcs_ref, tq_ref, tk_ref, rope)
    for hd, qh in enumerate(q_heads):
        q_ref[hd * HEAD_PAD:(hd + 1) * HEAD_PAD, :] = qh
    k_ref[...] = k
    v_ref[...] = v_t
    zc_ref[...] = zc
    zs_ref[...] = zs


def _premix(x, mods6, g1, win, qg, wuq, kvg, wk, wv, wcs, tq, tk, *, mod_row, tab_row, rope):
    n = x.shape[0]
    tm = PRE_TILE
    full = lambda a: pl.BlockSpec(a.shape, lambda i: (0,) * a.ndim)
    out_shape = [jax.ShapeDtypeStruct((QK_W, n), BF16), jax.ShapeDtypeStruct((n, QK_W), BF16),
                 jax.ShapeDtypeStruct((V_W, n), BF16), jax.ShapeDtypeStruct((n, FNET_W), BF16),
                 jax.ShapeDtypeStruct((n, FNET_W), BF16)]
    out_specs = [pl.BlockSpec((QK_W, tm), lambda i: (0, i)), pl.BlockSpec((tm, QK_W), lambda i: (i, 0)),
                 pl.BlockSpec((V_W, tm), lambda i: (0, i)), pl.BlockSpec((tm, FNET_W), lambda i: (i, 0)),
                 pl.BlockSpec((tm, FNET_W), lambda i: (i, 0))]
    return pl.pallas_call(
        functools.partial(_premix_kernel, rope=rope),
        grid=(n // tm,),
        in_specs=[pl.BlockSpec((tm, D_MODEL), lambda i: (i, 0)),
                  pl.BlockSpec((1, 6, D_MODEL), lambda i: (mod_row(i), 0, 0)),
                  full(g1), full(win), full(qg), full(wuq), full(kvg), full(wk), full(wv), full(wcs),
                  pl.BlockSpec((2 * LANES, tm), lambda i: (0, tab_row(i))),
                  pl.BlockSpec((tm, LANES), lambda i: (tab_row(i), 0))],
        out_specs=out_specs,
        out_shape=out_shape,
        compiler_params=_cparams(("parallel",)),
        name="premix",
    )(x, mods6, g1, win, qg, wuq, kvg, wk, wv, wcs, tq, tk)


def _cachekv_kernel(xk_ref, wk_ref, wv_ref, k_ref, v_ref):
    xk = xk_ref[...]
    k_ref[...] = _dot(xk, wk_ref[...]).astype(BF16)
    v_ref[...] = lax.dot_general(wv_ref[...], xk[:, 0:KV_LORA], _NT, preferred_element_type=F32).astype(BF16)


def _cachekv(xk, wk, wv):
    n = xk.shape[0]
    tm = 512
    full = lambda a: pl.BlockSpec(a.shape, lambda i: (0,) * a.ndim)
    return pl.pallas_call(
        _cachekv_kernel,
        grid=(n // tm,),
        in_specs=[pl.BlockSpec((tm, 2 * LANES), lambda i: (i, 0)), full(wk), full(wv)],
        out_specs=[pl.BlockSpec((tm, QK_W), lambda i: (i, 0)), pl.BlockSpec((V_W, tm), lambda i: (0, i))],
        out_shape=[jax.ShapeDtypeStruct((n, QK_W), BF16), jax.ShapeDtypeStruct((V_W, n), BF16)],
        compiler_params=_cparams(("parallel",)),
        name="cachekv",
    )(xk, wk, wv)


def _attn_body(q_heads, k_refs, v_refs, kc):
    tq = q_heads[0].shape[1]
    zero = jnp.zeros((HEAD_PAD, tq), BF16)
    n_pairs = len(q_heads) // 2
    qbd = [jnp.concatenate([jnp.concatenate([q_heads[2 * pr], zero], axis=1),
                            jnp.concatenate([zero, q_heads[2 * pr + 1]], axis=1)], axis=0) for pr in range(n_pairs)]
    chunks = [(k_ref, v_ref, c0, min(c0 + kc, k_ref.shape[0]))
              for k_ref, v_ref in zip(k_refs, v_refs) for c0 in range(0, k_ref.shape[0], kc)]
    work = [(pr, ch) for pr in range(n_pairs) for ch in chunks]

    def score(item):
        pr, (k_ref, _, c0, c1) = item
        return _dot(k_ref[c0:c1, pr * 2 * HEAD_PAD:(pr + 1) * 2 * HEAD_PAD], qbd[pr]).astype(BF16)

    m = [None] * n_pairs
    o = [None] * n_pairs
    s_next = score(work[0])
    for wi, (pr, (k_ref, v_ref, c0, c1)) in enumerate(work):
        s = s_next
        if wi + 1 < len(work):
            s_next = score(work[wi + 1])
        cm = jnp.max(s, axis=0, keepdims=True)
        vlo = pr * 2 * V_HEAD
        va = jnp.concatenate([v_ref[vlo:vlo + 2 * V_HEAD, c0:c1], jnp.ones((16, c1 - c0), BF16)], axis=0)
        if m[pr] is None:
            m[pr] = cm
            o[pr] = _dot(va, jnp.exp2(s - cm))
        else:
            m_new = jnp.maximum(m[pr], cm)
            alpha = jnp.exp2(m[pr].astype(F32) - m_new.astype(F32))
            o[pr] = alpha * o[pr] + _dot(va, jnp.exp2(s - m_new))
            m[pr] = m_new
    outs = []
    for pr in range(n_pairs):
        on = o[pr][0:2 * V_HEAD, :] * (1.0 / o[pr][2 * V_HEAD:2 * V_HEAD + 1, :])
        ot = jnp.concatenate([on[0:V_HEAD, 0:tq], on[V_HEAD:2 * V_HEAD, tq:2 * tq]], axis=0)
        outs.append(ot.T.astype(BF16))
    return outs


def _attn_kernel(q_ref, *refs, n_kv, n_pairs, kc):
    q_heads = [q_ref[hd * HEAD_PAD:(hd + 1) * HEAD_PAD, :] for hd in range(2 * n_pairs)]
    outs = _attn_body(q_heads, refs[:n_kv], refs[n_kv:2 * n_kv], kc)
    o_ref = refs[2 * n_kv]
    for pr, o in enumerate(outs):
        o_ref[:, pr * 2 * V_HEAD:(pr + 1) * 2 * V_HEAD] = o


def _attention(q_t, ks, vs_t, *, n_req, t_q, kv_lens, tq, pairs_per_step):
    n_kv = len(ks)
    nq = t_q // tq
    pp = pairs_per_step
    in_specs = [pl.BlockSpec((pp * 2 * HEAD_PAD, tq), lambda b, p, i: (p, b * nq + i))]
    in_specs += [pl.BlockSpec((kl, pp * 2 * HEAD_PAD), lambda b, p, i: (b, p)) for kl in kv_lens]
    in_specs += [pl.BlockSpec((pp * 2 * V_HEAD, kl), lambda b, p, i: (p, b)) for kl in kv_lens]
    return pl.pallas_call(
        functools.partial(_attn_kernel, n_kv=n_kv, n_pairs=pp, kc=KEY_CHUNK),
        grid=(n_req, N_HEADS // 2 // pp, nq),
        in_specs=in_specs,
        out_specs=pl.BlockSpec((tq, pp * 2 * V_HEAD), lambda b, p, i: (b * nq + i, p)),
        out_shape=jax.ShapeDtypeStruct((n_req * t_q, V_W), BF16),
        compiler_params=_cparams(("parallel", "parallel", "parallel")),
        name="attn",
    )(q_t, *ks, *vs_t)


def _mixout_body(x, attn_pairs, fm, wo_ref, m_ref, g2_ref, wr_ref):
    y = _dot(fm, wo_ref[V_W:V_W + FNET_W, :])
    col = 0
    for a in attn_pairs:
        y = y + _dot(a, wo_ref[col:col + a.shape[1], :])
        col += a.shape[1]
    gate1 = m_ref[0, 2:3, :]
    shift2 = m_ref[0, 3:4, :]
    scale2 = m_ref[0, 4:5, :]
    x1 = x + gate1 * y
    h2 = (_rms(x1, g2_ref[...] * (1.0 + scale2)) + shift2).astype(BF16)
    lg = lax.dot_general(wr_ref[...], h2, _NT, preferred_element_type=F32)
    e = jnp.exp(lg - jnp.max(lg, axis=0, keepdims=True))
    return x1, h2, e / jnp.sum(e, axis=0, keepdims=True)


def _mixout_kernel(x_ref, a_ref, zc_ref, zs_ref, cb_ref, sb_ref, off_ref, wo_ref, m_ref, g2_ref, wr_ref,
                   x1_ref, h2_ref, aff_ref, zp_ref, zm_ref, il_ref, *, t):
    half = t // 2
    i = pl.program_id(1)

    @pl.when(i == 0)
    def _fold_halves():
        for src, col in ((zc_ref, 0), (zs_ref, FNET_W)):
            lo = src[0:half, :]
            hi = src[half:t, :]
            zp_ref[:, col:col + FNET_W] = lo + hi
            zm_ref[:, col:col + FNET_W] = lo - hi

    off = off_ref[pl.ds(i, 1), :]
    co = off[:, 0:half]
    so = off[:, half:t]
    cb = cb_ref[...]
    sb = sb_ref[...]
    ct = (cb * co - sb * so).astype(BF16)
    st = (sb * co + cb * so).astype(BF16)
    tr = cb.shape[0]
    h = tr // 2
    even = _dot(ct[0:h, :], zp_ref[:, 0:FNET_W]) - _dot(st[0:h, :], zp_ref[:, FNET_W:2 * FNET_W])
    odd = _dot(ct[h:, :], zm_ref[:, 0:FNET_W]) - _dot(st[h:, :], zm_ref[:, FNET_W:2 * FNET_W])
    for c in range(FNET_W // LANES):
        il_ref[c, pl.ds(0, h, stride=2), :] = even[:, c * LANES:(c + 1) * LANES]
        il_ref[c, pl.ds(1, h, stride=2), :] = odd[:, c * LANES:(c + 1) * LANES]
    fm = jnp.concatenate([il_ref[c] for c in range(FNET_W // LANES)], axis=1).astype(BF16)
    x1, h2, aff = _mixout_body(x_ref[...], [a_ref[...]], fm, wo_ref, m_ref, g2_ref, wr_ref)
    x1_ref[...] = x1
    h2_ref[...] = h2
    aff_ref[...] = aff


def _ctx_front_kernel(x_ref, m_ref, g1_ref, win_ref, qg_ref, wuq_ref, kvg_ref, wk_ref, wv_ref, wcs_ref,
                      tq_ref, tk_ref, ct_ref, st_ref, wo_ref, g2_ref, wr_ref,
                      x1_ref, h2_ref, aff_ref, ckv_ref, kpe_ref, *, kc, t):
    x = x_ref[...]
    q_heads, k, v_t, zc, zs, ckv, kpe = _premix_body(x, m_ref, g1_ref, win_ref, qg_ref, wuq_ref, kvg_ref,
                                                     wk_ref, wv_ref, wcs_ref, tq_ref, tk_ref, False)
    ckv_ref[...] = ckv
    for r in range(x.shape[0] // t):
        kpe_ref[r] = kpe[r * t:(r + 1) * t, :].T[0:QK_ROPE, :]
    attn, fm = [], []
    for r in range(x.shape[0] // t):
        rows = slice(r * t, (r + 1) * t)
        attn.append(jnp.concatenate(_attn_body([q[:, rows] for q in q_heads], [k[rows, :]], [v_t[:, rows]], kc),
                                    axis=1))
        fm.append((_dot(ct_ref[...], zc[rows, :]) - _dot(st_ref[...], zs[rows, :])).astype(BF16))
    x1, h2, aff = _mixout_body(x_ref[...], [jnp.concatenate(attn, axis=0)], jnp.concatenate(fm, axis=0),
                               wo_ref, m_ref, g2_ref, wr_ref)
    x1_ref[...] = x1
    h2_ref[...] = h2
    for r in range(x.shape[0] // t):
        aff_ref[r] = aff[:, r * t:(r + 1) * t]


def _ctx_front(x, mods6, g1, win, qg, wuq, kvg, wk, wv, wcs, tq, tk, ct, st, wo, g2, wr_t, *, n_req, t, mod_row):
    full = lambda a: pl.BlockSpec(a.shape, lambda b: (0,) * a.ndim)
    rps = CTX_REQS
    row = lambda w: pl.BlockSpec((rps * t, w), lambda b: (b, 0))
    return pl.pallas_call(
        functools.partial(_ctx_front_kernel, kc=KEY_CHUNK, t=t),
        grid=(n_req // rps,),
        in_specs=[row(D_MODEL), pl.BlockSpec((1, 6, D_MODEL), lambda b: (mod_row, 0, 0)),
                  full(g1), full(win), full(qg), full(wuq), full(kvg), full(wk), full(wv), full(wcs),
                  full(tq), full(tk), full(ct), full(st), full(wo), full(g2), full(wr_t)],
        out_specs=[row(D_MODEL), row(D_MODEL), pl.BlockSpec((rps, N_EXPERTS, t), lambda b: (b, 0, 0)),
                   row(KV_LORA), pl.BlockSpec((rps, QK_ROPE, t), lambda b: (b, 0, 0))],
        out_shape=[jax.ShapeDtypeStruct((n_req * t, D_MODEL), F32),
                   jax.ShapeDtypeStruct((n_req * t, D_MODEL), BF16),
                   jax.ShapeDtypeStruct((n_req, N_EXPERTS, t), F32),
                   jax.ShapeDtypeStruct((n_req * t, KV_LORA), F32),
                   jax.ShapeDtypeStruct((n_req, QK_ROPE, t), F32)],
        compiler_params=_cparams(("parallel",)),
        name="ctx_front",
    )(x, mods6, g1, win, qg, wuq, kvg, wk, wv, wcs, tq, tk, ct, st, wo, g2, wr_t)


def _mixout(x, attn, zc, zs, cb, sb, off, wo, mods6, g2, wr_t, *, n_req, t, mod_row):
    tr = MIX_TILE
    nr = t // tr
    full = lambda a: pl.BlockSpec(a.shape, lambda b, i: (0,) * a.ndim)
    return pl.pallas_call(
        functools.partial(_mixout_kernel, t=t),
        grid=(n_req, nr),
        in_specs=[pl.BlockSpec((tr, D_MODEL), lambda b, i: (b * nr + i, 0)),
                  pl.BlockSpec((tr, V_W), lambda b, i: (b * nr + i, 0)),
                  pl.BlockSpec((t, FNET_W), lambda b, i: (b, 0)),
                  pl.BlockSpec((t, FNET_W), lambda b, i: (b, 0)),
                  full(cb), full(sb), full(off),
                  full(wo),
                  pl.BlockSpec((1, 6, D_MODEL), lambda b, i: (mod_row(b), 0, 0)),
                  full(g2), full(wr_t)],
        out_specs=[pl.BlockSpec((tr, D_MODEL), lambda b, i: (b * nr + i, 0)),
                   pl.BlockSpec((tr, D_MODEL), lambda b, i: (b * nr + i, 0)),
                   pl.BlockSpec((None, N_EXPERTS, tr), lambda b, i: (b, 0, i))],
        out_shape=[jax.ShapeDtypeStruct((n_req * t, D_MODEL), F32),
                   jax.ShapeDtypeStruct((n_req * t, D_MODEL), BF16),
                   jax.ShapeDtypeStruct((n_req, N_EXPERTS, t), F32)],
        scratch_shapes=[pltpu.VMEM((t // 2, 2 * FNET_W), BF16), pltpu.VMEM((t // 2, 2 * FNET_W), BF16),
                        pltpu.VMEM((FNET_W // LANES, tr, LANES), F32)],
        compiler_params=pltpu.CompilerParams(dimension_semantics=("parallel", "arbitrary"),
                                             vmem_limit_bytes=MIX_VMEM_LIMIT),
        name="mixout",
    )(x, attn, zc, zs, cb, sb, off, wo, mods6, g2, wr_t)


def _prefix_count(flags, tri):
    n = flags.shape[1]
    carry = None
    outs = []
    ends = []
    for j in range(n // TOK_TILE):
        c = _dot(flags[:, j * TOK_TILE:(j + 1) * TOK_TILE].astype(BF16), tri)
        if carry is not None:
            c = c + carry
        outs.append(c)
        carry = c[:, TOK_TILE - 1:TOK_TILE]
        ends.append(carry)
    return (outs[0] if len(outs) == 1 else jnp.concatenate(outs, axis=1)), ends


def _route_kernel(aff_ref, pos_ref, off_ref, *, cap):
    a = aff_ref[...]
    rows = a.shape[0]
    capf = jnp.float32(cap)
    thr = jnp.zeros((rows, 1), jnp.int32)
    for bit in range(30, -1, -1):
        cand = thr | jnp.int32(1 << bit)
        cand_f = lax.bitcast_convert_type(cand, F32)
        cnt = jnp.sum(jnp.where(a >= cand_f, 1.0, 0.0), axis=1, keepdims=True)
        thr = jnp.where(cnt >= capf, cand, thr)
    thr_f = lax.bitcast_convert_type(thr, F32)
    above_f = lax.bitcast_convert_type(thr + 1, F32)
    gt = jnp.where(a >= above_f, 1.0, 0.0)
    tie = jnp.where(a >= thr_f, 1.0, 0.0) - gt
    need = capf - jnp.sum(gt, axis=1, keepdims=True)
    r_i = lax.broadcasted_iota(jnp.int32, (TOK_TILE, TOK_TILE), 0)
    c_i = lax.broadcasted_iota(jnp.int32, (TOK_TILE, TOK_TILE), 1)
    tri = jnp.where(r_i <= c_i, 1.0, 0.0).astype(BF16)
    tie_before = _prefix_count(tie, tri)[0] - tie
    sel = gt + tie * jnp.where(tie_before < need, 1.0, 0.0)
    count, ends = _prefix_count(sel, tri)
    pos_ref[...] = jnp.where(sel > 0.5, count - 1.0, -1.0)
    lane = lax.broadcasted_iota(jnp.int32, (rows, LANES), 1)
    offs = jnp.zeros((rows, LANES), F32)
    for j, end in enumerate(ends):
        offs = offs + jnp.where(lane == j + 1, end, 0.0)
    off_ref[...] = offs.astype(jnp.int32)


def _route(aff_t, cap):
    return pl.pallas_call(
        functools.partial(_route_kernel, cap=cap),
        out_shape=[jax.ShapeDtypeStruct(aff_t.shape, F32),
                   jax.ShapeDtypeStruct((aff_t.shape[0], LANES), jnp.int32)],
        compiler_params=pltpu.CompilerParams(vmem_limit_bytes=VMEM_LIMIT),
        name="route",
    )(aff_t)


def _gather_kernel(pos_ref, aff_ref, h_ref, xs_ref, g_ref, *, cap):
    rps, ne, n = pos_ref.shape
    slot = lax.broadcasted_iota(jnp.int32, (ne, cap, n), 1).astype(F32)
    for r in range(rps):
        pos = pos_ref[r]
        aff = aff_ref[r]
        hit = pos[:, None, :] == slot
        onehot = jnp.where(hit, 1.0, 0.0).reshape(ne * cap, n).astype(BF16)
        xs = _dot(onehot, h_ref[r * n:(r + 1) * n, :])
        xs_ref[:, r * cap:(r + 1) * cap, :] = xs.astype(BF16).reshape(ne, cap, D_MODEL)
        g_ref[:, r * cap:(r + 1) * cap, :] = jnp.sum(jnp.where(hit, aff[:, None, :], 0.0), axis=2, keepdims=True)


def _gather(pos_t, aff_t, h2, *, n_req, n, cap, rps):
    pos3 = pos_t.reshape(n_req, N_EXPERTS, n)
    aff3 = aff_t.reshape(n_req, N_EXPERTS, n)
    return pl.pallas_call(
        functools.partial(_gather_kernel, cap=cap),
        grid=(n_req // rps,),
        in_specs=[pl.BlockSpec((rps, N_EXPERTS, n), lambda b: (b, 0, 0)),
                  pl.BlockSpec((rps, N_EXPERTS, n), lambda b: (b, 0, 0)),
                  pl.BlockSpec((rps * n, D_MODEL), lambda b: (b, 0))],
        out_specs=[pl.BlockSpec((N_EXPERTS, rps * cap, D_MODEL), lambda b: (0, b, 0)),
                   pl.BlockSpec((N_EXPERTS, rps * cap, 1), lambda b: (0, b, 0))],
        out_shape=[jax.ShapeDtypeStruct((N_EXPERTS, n_req * cap, D_MODEL), BF16),
                   jax.ShapeDtypeStruct((N_EXPERTS, n_req * cap, 1), F32)],
        compiler_params=_cparams(("parallel",)),
        name="gather",
    )(pos3, aff3, h2)


def _window_plan(off_ref, b, j, cap, w):
    w0 = []
    need = jnp.int32(0)
    for e in range(N_EXPERTS):
        idx = (b * N_EXPERTS + e) * OFF_STRIDE + j
        base = (off_ref[idx] >> 4) << 4
        w0.append(base)
        need = jnp.maximum(need, off_ref[idx + 1] - base)
    return w0, (need + (w - 1)) >> (w.bit_length() - 1)


def _window(w0_e, p, cap, w):
    low = w0_e + p * w
    start = pl.multiple_of(jnp.minimum(low, cap - w), 16)
    return low, start


def _gather_win_kernel(off_ref, pos_ref, aff_ref, h_ref, xs_ref, g_ref, *, cap, w):
    b = pl.program_id(0)
    step = pl.program_id(1)

    @pl.when(step == 0)
    def _init():
        xs_ref[...] = jnp.zeros_like(xs_ref)
        g_ref[...] = jnp.zeros_like(g_ref)

    r = lax.broadcasted_iota(jnp.int32, (w, 1), 0).astype(F32)
    n_sub = pos_ref.shape[1] // TOK_TILE
    for sub in range(n_sub):
        cols = slice(sub * TOK_TILE, (sub + 1) * TOK_TILE)
        pos = pos_ref[:, cols]
        aff = aff_ref[:, cols]
        h = h_ref[cols, :]
        w0, n_pass = _window_plan(off_ref, b, step * n_sub + sub, cap, w)

        def one_pass(p, carry, pos=pos, aff=aff, h=h, w0=w0):
            starts, hots, gates = [], [], []
            for e in range(N_EXPERTS):
                low, start = _window(w0[e], p, cap, w)
                starts.append(start)
                mine = jnp.where(r >= (low - start).astype(F32), 1.0, 0.0)
                hot = jnp.where(pos[e:e + 1, :] - start.astype(F32) == r, mine, 0.0)
                hots.append(hot)
                gates.append(jnp.sum(hot * aff[e:e + 1, :], axis=1, keepdims=True))
            rows = _dot(jnp.concatenate(hots, axis=0).astype(BF16), h).astype(BF16)
            for e in range(N_EXPERTS):
                win = pl.ds(starts[e], w)
                xs_ref[e, win, :] = xs_ref[e, win, :] + rows[e * w:(e + 1) * w, :]
                g_ref[e, win, :] = g_ref[e, win, :] + gates[e]
            return carry

        one_pass(0, 0)
        lax.fori_loop(1, n_pass, one_pass, 0)


def _gather_win(offs, pos_t, aff_t, h2, *, n_req, n, cap):
    tb = WIN_BLOCKS * TOK_TILE
    nb = n // tb
    grid_spec = pltpu.PrefetchScalarGridSpec(
        num_scalar_prefetch=1,
        grid=(n_req, nb),
        in_specs=[pl.BlockSpec((N_EXPERTS, tb), lambda b, j, off: (b, j)),
                  pl.BlockSpec((N_EXPERTS, tb), lambda b, j, off: (b, j)),
                  pl.BlockSpec((tb, D_MODEL), lambda b, j, off: (b * nb + j, 0))],
        out_specs=[pl.BlockSpec((N_EXPERTS, cap, D_MODEL), lambda b, j, off: (0, b, 0)),
                   pl.BlockSpec((N_EXPERTS, cap, 1), lambda b, j, off: (0, b, 0))])
    return pl.pallas_call(
        functools.partial(_gather_win_kernel, cap=cap, w=SLOT_WIN),
        grid_spec=grid_spec,
        out_shape=[jax.ShapeDtypeStruct((N_EXPERTS, n_req * cap, D_MODEL), BF16),
                   jax.ShapeDtypeStruct((N_EXPERTS, n_req * cap, 1), F32)],
        compiler_params=_cparams(("parallel", "arbitrary")),
        name="gather_win",
    )(offs, pos_t, aff_t, h2)


def _ffn_kernel(xc_ref, xl_ref, gc_ref, gl_ref, wg_ref, wu_ref, wd_ref, yc_ref, yl_ref, wgb, wub, wdb):
    wgb[...] = wg_ref[0].astype(BF16)
    wub[...] = wu_ref[0].astype(BF16)
    wdb[...] = wd_ref[0].astype(BF16)
    for x_ref, g_ref, y_ref in ((xc_ref, gc_ref, yc_ref), (xl_ref, gl_ref, yl_ref)):
        for j in range(x_ref.shape[1] // FFN_ROWS):
            rows = slice(j * FFN_ROWS, (j + 1) * FFN_ROWS)
            x = x_ref[0, rows, :]
            gate = _dot(x, wgb[...])
            up = _dot(x, wub[...])
            hid = (gate * jax.nn.sigmoid(gate) * up).astype(BF16)
            ys = _dot(hid, wdb[...]) * g_ref[0, rows, :]
            y_ref[0, rows, :] = ys.astype(BF16)


def _ffn(xc, xl, gc, gl, wg, wu, wd):
    m = xc.shape[1]
    xspec = pl.BlockSpec((1, m, D_MODEL), lambda e: (e, 0, 0))
    gspec = pl.BlockSpec((1, m, 1), lambda e: (e, 0, 0))
    shp = jax.ShapeDtypeStruct((N_EXPERTS, m, D_MODEL), BF16)
    return pl.pallas_call(
        _ffn_kernel,
        grid=(N_EXPERTS,),
        in_specs=[xspec, xspec, gspec, gspec,
                  pl.BlockSpec((1, D_MODEL, D_EXPERT), lambda e: (e, 0, 0)),
                  pl.BlockSpec((1, D_MODEL, D_EXPERT), lambda e: (e, 0, 0)),
                  pl.BlockSpec((1, D_EXPERT, D_MODEL), lambda e: (e, 0, 0))],
        out_specs=[xspec, xspec],
        out_shape=[shp, shp],
        scratch_shapes=[pltpu.VMEM((D_MODEL, D_EXPERT), BF16), pltpu.VMEM((D_MODEL, D_EXPERT), BF16),
                        pltpu.VMEM((D_EXPERT, D_MODEL), BF16)],
        compiler_params=_cparams(("arbitrary",)),
        name="ffn",
    )(xc, xl, gc, gl, wg, wu, wd)


def _combine_kernel(x1_ref, pos_ref, ys_ref, m_ref, fg_ref, o_ref, *, cap):
    rps, n, _ = pos_ref.shape
    w = N_EXPERTS * cap
    e_i = lax.broadcasted_iota(jnp.int32, (N_EXPERTS, w), 0)
    j_i = lax.broadcasted_iota(jnp.int32, (N_EXPERTS, w), 1)
    spread = jnp.where((j_i >> (cap.bit_length() - 1)) == e_i, 1.0, 0.0).astype(BF16)
    lane_slot = (lax.broadcasted_iota(jnp.int32, (1, w), 1) & (cap - 1)).astype(F32)
    gate2 = m_ref[0, 5:6, :]
    for r in range(rps):
        pos = pos_ref[r].astype(BF16)
        onehot = jnp.where(_dot(pos, spread) == lane_slot, 1.0, 0.0).astype(BF16)
        acc = _dot(onehot, ys_ref[:, r * cap:(r + 1) * cap, :].reshape(w, D_MODEL))
        rows = slice(r * n, (r + 1) * n)
        o_ref[rows, :] = _rms(x1_ref[rows, :] + gate2 * acc, fg_ref[...])


def _combine(x1, pos_tok, ys, mods6, fg, *, n_req, n, cap, mod_row, rps):
    return pl.pallas_call(
        functools.partial(_combine_kernel, cap=cap),
        grid=(n_req // rps,),
        in_specs=[pl.BlockSpec((rps * n, D_MODEL), lambda b: (b, 0)),
                  pl.BlockSpec((rps, n, N_EXPERTS), lambda b: (b, 0, 0)),
                  pl.BlockSpec((N_EXPERTS, rps * cap, D_MODEL), lambda b: (0, b, 0)),
                  pl.BlockSpec((1, 6, D_MODEL), lambda b: (mod_row, 0, 0)),
                  pl.BlockSpec((1, D_MODEL), lambda b: (0, 0))],
        out_specs=pl.BlockSpec((rps * n, D_MODEL), lambda b: (b, 0)),
        out_shape=jax.ShapeDtypeStruct((n_req * n, D_MODEL), F32),
        compiler_params=_cparams(("parallel",)),
        name="combine",
    )(x1, pos_tok, ys, mods6, fg)


def _combine_win_kernel(off_ref, x1_ref, pos_ref, ys_ref, m_ref, fg_ref, o_ref, acc_ref, *, cap, w):
    b = pl.program_id(0)
    step = pl.program_id(1)
    width = N_EXPERTS * w
    e_i = lax.broadcasted_iota(jnp.int32, (N_EXPERTS, width), 0)
    j_i = lax.broadcasted_iota(jnp.int32, (N_EXPERTS, width), 1)
    spread = jnp.where((j_i >> (w.bit_length() - 1)) == e_i, 1.0, 0.0).astype(BF16)
    lane_slot = (lax.broadcasted_iota(jnp.int32, (1, width), 1) & (w - 1)).astype(F32)
    lane_e = lax.broadcasted_iota(jnp.int32, (1, N_EXPERTS), 1)
    n_sub = pos_ref.shape[0] // TOK_TILE
    for sub in range(n_sub):
        rows = slice(sub * TOK_TILE, (sub + 1) * TOK_TILE)
        pos = pos_ref[rows, :]
        w0, n_pass = _window_plan(off_ref, b, step * n_sub + sub, cap, w)

        def window_sum(p, pos=pos, w0=w0):
            start_row = jnp.zeros((1, N_EXPERTS), F32)
            first_row = jnp.zeros((1, N_EXPERTS), F32)
            wins = []
            for e in range(N_EXPERTS):
                low, start = _window(w0[e], p, cap, w)
                start_row = jnp.where(lane_e == e, start.astype(F32), start_row)
                first_row = jnp.where(lane_e == e, (low - start).astype(F32), first_row)
                wins.append(ys_ref[e, pl.ds(start, w), :])
            rel = pos - start_row
            rel = jnp.where(rel >= first_row, rel, -1.0).astype(BF16)
            onehot = jnp.where(_dot(rel, spread) == lane_slot, 1.0, 0.0).astype(BF16)
            return _dot(onehot, jnp.concatenate(wins, axis=0))

        def finish(acc, rows=rows):
            o_ref[rows, :] = _rms(x1_ref[rows, :] + m_ref[0, 5:6, :] * acc, fg_ref[...])

        acc0 = window_sum(0)
        acc_ref[sub] = acc0
        finish(acc0)

        @pl.when(n_pass > 1)
        def _more_passes(sub=sub, n_pass=n_pass, window_sum=window_sum, finish=finish):
            def one_pass(p, carry):
                acc_ref[sub] += window_sum(p)
                return carry

            lax.fori_loop(1, n_pass, one_pass, 0)
            finish(acc_ref[sub])


def _combine_win(offs, x1, pos_tok, ys, mods6, fg, *, n_req, n, cap, mod_row):
    tr = WIN_BLOCKS * TOK_TILE
    nr = n // tr
    grid_spec = pltpu.PrefetchScalarGridSpec(
        num_scalar_prefetch=1,
        grid=(n_req, nr),
        in_specs=[pl.BlockSpec((tr, D_MODEL), lambda b, i, off: (b * nr + i, 0)),
                  pl.BlockSpec((None, tr, N_EXPERTS), lambda b, i, off: (b, i, 0)),
                  pl.BlockSpec((N_EXPERTS, cap, D_MODEL), lambda b, i, off: (0, b, 0)),
                  pl.BlockSpec((1, 6, D_MODEL), lambda b, i, off: (mod_row(b), 0, 0)),
                  pl.BlockSpec((1, D_MODEL), lambda b, i, off: (0, 0))],
        out_specs=pl.BlockSpec((tr, D_MODEL), lambda b, i, off: (b * nr + i, 0)),
        scratch_shapes=[pltpu.VMEM((WIN_BLOCKS, TOK_TILE, D_MODEL), F32)])
    return pl.pallas_call(
        functools.partial(_combine_win_kernel, cap=cap, w=SLOT_WIN),
        grid_spec=grid_spec,
        out_shape=jax.ShapeDtypeStruct((n_req * n, D_MODEL), F32),
        compiler_params=_cparams(("parallel", "parallel")),
        name="combine_win",
    )(offs, x1, pos_tok, ys, mods6, fg)


def _rot_half(w):
    half = QK_ROPE // 2
    return jnp.concatenate([-w[..., half:], w[..., :half]], axis=-1)


def _rope_tables(t):
    n_rows = t // GRID_W
    rows = np.repeat(np.arange(n_rows, dtype=np.float64), GRID_W)
    cols = np.tile(np.arange(GRID_W, dtype=np.float64), n_rows)
    n_freq = QK_ROPE // 4
    inv_freq = ROPE_BASE ** (-np.arange(n_freq, dtype=np.float64) / n_freq)
    ang = np.concatenate([rows[:, None] * inv_freq, cols[:, None] * inv_freq], axis=-1)
    cos = np.concatenate([np.cos(ang), np.cos(ang)], axis=-1)
    sin = np.concatenate([np.sin(ang), np.sin(ang)], axis=-1)
    return cos, sin


def _qk_tables(cos, sin):
    t = cos.shape[0]
    scale = (QK_NOPE + QK_ROPE) ** -0.5 * np.log2(np.e)
    pad = np.zeros((t, HEAD_PAD - QK_NOPE - QK_ROPE))
    cosq = np.concatenate([np.full((t, QK_NOPE), scale), cos * scale, pad], axis=1)
    sinq = np.concatenate([np.zeros((t, QK_NOPE)), sin * scale, pad], axis=1)
    tq_t = np.concatenate([cosq, sinq], axis=1).T
    tk = np.concatenate([cos, sin, np.zeros((t, LANES - 2 * QK_ROPE))], axis=1)
    return jnp.asarray(tq_t, F32), jnp.asarray(tk, F32)


def _dft_angles(rows, t):
    k = np.arange(t, dtype=np.int64)
    return ((rows[:, None] * k[None, :]) % t).astype(np.float64) * (2.0 * np.pi / t)


def _dft_tables(t):
    ang = _dft_angles(np.arange(t, dtype=np.int64), t)
    scale = (t * FNET_CH) ** -0.5
    return jnp.asarray(np.cos(ang) * scale, F32).astype(BF16), jnp.asarray(np.sin(ang) * scale, F32).astype(BF16)


def _dft_half_tables(t):
    r = np.arange(MIX_TILE, dtype=np.int64)
    ang = _dft_angles(np.concatenate([r[0::2], r[1::2]]), t)[:, :t // 2]
    scale = (t * FNET_CH) ** -0.5
    ang_off = _dft_angles(np.arange(t // MIX_TILE, dtype=np.int64) * MIX_TILE, t)[:, :t // 2]
    off = np.concatenate([np.cos(ang_off), np.sin(ang_off)], axis=1)
    return jnp.asarray(np.cos(ang) * scale, F32), jnp.asarray(np.sin(ang) * scale, F32), jnp.asarray(off, F32)


def _block_diag(w):
    g, a, b = w.shape
    eye = jnp.eye(g, dtype=w.dtype)
    return (eye[:, None, :, None] * w[:, :, None, :]).reshape(g * a, g * b)


def kernel(x_prompt, x_sample, cache_ckv, cache_kpe, c, c_ctx, w_mod, b_mod, norm1_g, w_in, q_norm_g, w_uq,
           kv_norm_g, w_ukv, w_fmix, w_out, norm2_g, w_router, w_e_gate, w_e_up, w_e_down, final_g):
    assert w_mod.shape[0] == 1, "single-layer problem"
    n_ctx, t_ctx, _ = x_prompt.shape
    n_lat, t_lat, _ = x_sample.shape
    past = cache_ckv.shape[2]
    ctx_row = n_lat

    win = _winprep(jnp.swapaxes(w_in, 1, 2)[0])
    wq3 = w_uq[0].reshape(Q_LORA, N_HEADS, QK_NOPE + QK_ROPE)
    qpad = jnp.zeros((Q_LORA, N_HEADS, HEAD_PAD - QK_NOPE - QK_ROPE), F32)
    wuq_main = jnp.concatenate([wq3, qpad], axis=2).reshape(Q_LORA, QK_W)
    wuq_rot = _rot_half(wq3[..., QK_NOPE:]).reshape(Q_LORA, N_HEADS * QK_ROPE)
    wuq_lat = jnp.concatenate([wuq_main, wuq_rot], axis=1).T.astype(BF16)
    wuq_ctx = wuq_main.T.astype(BF16)
    wkv3 = w_ukv[0].reshape(KV_LORA, N_HEADS, QK_NOPE + V_HEAD)
    wk_top = jnp.concatenate([wkv3[..., :QK_NOPE], jnp.zeros((KV_LORA, N_HEADS, HEAD_PAD - QK_NOPE), F32)],
                             axis=2).reshape(KV_LORA, QK_W)
    place = jnp.concatenate([jnp.zeros((QK_ROPE, QK_NOPE), F32), jnp.eye(QK_ROPE, dtype=F32),
                             jnp.zeros((QK_ROPE, HEAD_PAD - QK_NOPE - QK_ROPE), F32)], axis=1)
    place = jnp.tile(place, (1, N_HEADS))
    wk = jnp.concatenate([wk_top, place, place, jnp.zeros((LANES - 2 * QK_ROPE, QK_W), F32)], axis=0).astype(BF16)
    wv = wkv3[..., QK_NOPE:].reshape(KV_LORA, V_W).T.astype(BF16)
    wo = w_out[0].astype(BF16)
    wr_t = w_router[0].T.astype(BF16)

    cos, sin = _rope_tables(t_lat)
    tq_lat, tk_lat = _qk_tables(cos, sin)
    assert n_ctx % CTX_REQS == 0
    tq_ctx, tk_ctx = _qk_tables(np.ones((CTX_REQS * t_ctx, QK_ROPE)), np.zeros((CTX_REQS * t_ctx, QK_ROPE)))
    ch_ang = _dft_angles(np.arange(FNET_CH, dtype=np.int64), FNET_CH)
    dft_ctx = _dft_tables(t_ctx)
    dft_lat = _dft_half_tables(t_lat)

    assert n_lat + 1 <= SUBLANES
    c8 = jnp.concatenate([c, c_ctx[None, :], jnp.zeros((SUBLANES - n_lat - 1, D_MODEL), F32)], axis=0)
    mods6 = _mods(c8, w_mod[0], b_mod[0][None, :]).reshape(SUBLANES, 6, D_MODEL)
    cw, sw = _fold(jnp.asarray(np.cos(ch_ang), F32), jnp.asarray(np.sin(ch_ang), F32), w_fmix[0])
    gh = FNET_GROUPS // 2
    wcs = jnp.concatenate([jnp.concatenate([_block_diag(cw[a * gh:(a + 1) * gh]), _block_diag(sw[a * gh:(a + 1) * gh])],
                                           axis=1) for a in range(2)], axis=0).astype(BF16)

    g1 = norm1_g[0][None, :]
    qg = q_norm_g[0][None, :]
    kvg = kv_norm_g[0][None, :]
    g2 = norm2_g[0][None, :]
    fg = final_g[None, :]

    xp = x_prompt.reshape(n_ctx * t_ctx, D_MODEL)
    xs = x_sample.reshape(n_lat * t_lat, D_MODEL)
    tiles_lat = t_lat // PRE_TILE

    x1c, h2c, affc, ckv_c, kpe_c = _ctx_front(
        xp, mods6, g1, win, qg, wuq_ctx, kvg, wk, wv, wcs, tq_ctx, tk_ctx,
        *dft_ctx, wo, g2, wr_t, n_req=n_ctx, t=t_ctx, mod_row=ctx_row)
    ql, kl, vl, zcl, zsl = _premix(
        xs, mods6, g1, win, qg, wuq_lat, kvg, wk, wv, wcs, tq_lat, tk_lat,
        mod_row=lambda i: i // tiles_lat, tab_row=lambda i: i % tiles_lat, rope=True)
    xk_cache = jnp.concatenate([cache_ckv[:, 0], cache_kpe[:, 0],
                                jnp.zeros((n_lat, past, 2 * LANES - KV_LORA - QK_ROPE), F32)],
                               axis=-1).reshape(n_lat * past, 2 * LANES).astype(BF16)
    kpast, vpast = _cachekv(xk_cache, wk, wv)

    attn_l = _attention(ql, [kpast, kl], [vpast, vl], n_req=n_lat, t_q=t_lat, kv_lens=[past, t_lat], tq=TOK_TILE,
                        pairs_per_step=2)

    x1l, h2l, affl = _mixout(xs, attn_l, zcl, zsl, *dft_lat, wo, mods6, g2, wr_t,
                             n_req=n_lat, t=t_lat, mod_row=lambda b: b)

    cap_c = CAP_FACTOR * t_ctx // N_EXPERTS
    cap_l = CAP_FACTOR * t_lat // N_EXPERTS
    affc2 = affc.reshape(n_ctx * N_EXPERTS, t_ctx)
    affl2 = affl.reshape(n_lat * N_EXPERTS, t_lat)
    posc, _ = _route(affc2, cap_c)
    posl, offl = _route(affl2, cap_l)
    assert t_lat // TOK_TILE + 1 <= OFF_STRIDE and cap_l % SLOT_WIN == 0
    offl = offl[:, :OFF_STRIDE].reshape(-1)
    xsc, gc = _gather(posc, affc2, h2c, n_req=n_ctx, n=t_ctx, cap=cap_c, rps=MOE_REQS)
    xsl, gl = _gather_win(offl, posl, affl2, h2l, n_req=n_lat, n=t_lat, cap=cap_l)
    ysc, ysl = _ffn(xsc, xsl, gc, gl, w_e_gate[0], w_e_up[0], w_e_down[0])

    posc_tok = posc.reshape(n_ctx, N_EXPERTS, t_ctx).transpose(0, 2, 1)
    posl_tok = posl.reshape(n_lat, N_EXPERTS, t_lat).transpose(0, 2, 1)
    y_prompt = _combine(x1c, posc_tok, ysc, mods6, fg, n_req=n_ctx, n=t_ctx, cap=cap_c, mod_row=ctx_row,
                        rps=MOE_REQS)
    y_sample = _combine_win(offl, x1l, posl_tok, ysl, mods6, fg, n_req=n_lat, n=t_lat, cap=cap_l,
                            mod_row=lambda b: b)

    return (y_prompt.reshape(n_ctx, t_ctx, D_MODEL), y_sample.reshape(n_lat, t_lat, D_MODEL),
            ckv_c.reshape(n_ctx, 1, t_ctx, KV_LORA), kpe_c.transpose(0, 2, 1).reshape(n_ctx, 1, t_ctx, QK_ROPE))
```

```python
import functools

import jax
import jax.numpy as jnp
import numpy as np
from jax import lax
from jax.experimental import pallas as pl
from jax.experimental.pallas import tpu as pltpu

F32 = jnp.float32
BF16 = jnp.bfloat16

D_MODEL = 1024
N_HEADS = 8
QK_NOPE = 64
QK_ROPE = 32
V_HEAD = 64
Q_LORA = 256
KV_LORA = 128
FNET_GROUPS = 8
FNET_CH = 64
FNET_W = FNET_GROUPS * FNET_CH
N_EXPERTS = 16
CAP_FACTOR = 2
D_EXPERT = 512
GRID_W = 64
ROPE_BASE = 10000.0
EPS = 1e-6

LANES = 128
SUBLANES = 8
HEAD_PAD = LANES
QK_W = N_HEADS * HEAD_PAD
V_W = N_HEADS * V_HEAD
TOK_TILE = 256
PRE_TILE = 1024
MIX_TILE = 512
FFN_ROWS = 1024
MODS_TILE = 1536
CTX_REQS = 4
MOE_REQS = 4
KEY_CHUNK = 512
SLOT_WIN = 64
WIN_BLOCKS = 4
OFF_STRIDE = 16
VMEM_LIMIT = 48 * 1024 * 1024

_NT = (((1,), (1,)), ((), ()))


def _cparams(sem):
    return pltpu.CompilerParams(dimension_semantics=sem, vmem_limit_bytes=VMEM_LIMIT)


def _rms(x, g):
    return x * lax.rsqrt(jnp.mean(x * x, axis=-1, keepdims=True) + EPS) * g


def _dot(a, b):
    return jnp.dot(a, b, preferred_element_type=F32)


def _mods_kernel(c_ref, w_ref, b_ref, o_ref):
    c = c_ref[...]
    s = c * jax.nn.sigmoid(c)
    o_ref[...] = _dot(s.astype(BF16), w_ref[...].astype(BF16)) + b_ref[...]


def _mods(c8, w_mod, b_mod):
    n = w_mod.shape[1]
    tn = MODS_TILE
    return pl.pallas_call(
        _mods_kernel,
        grid=(n // tn,),
        in_specs=[pl.BlockSpec((SUBLANES, D_MODEL), lambda j: (0, 0)),
                  pl.BlockSpec((D_MODEL, tn), lambda j: (0, j)),
                  pl.BlockSpec((1, tn), lambda j: (0, j))],
        out_specs=pl.BlockSpec((SUBLANES, tn), lambda j: (0, j)),
        out_shape=jax.ShapeDtypeStruct((SUBLANES, n), F32),
        compiler_params=_cparams(("arbitrary",)),
        name="mods",
    )(c8, w_mod, b_mod)


def _fold_kernel(cc_ref, sc_ref, w_ref, cw_ref, sw_ref):
    for g in range(FNET_GROUPS):
        w = w_ref[g]
        cw_ref[g] = jnp.dot(cc_ref[...], w, preferred_element_type=F32, precision=lax.Precision.HIGHEST)
        sw_ref[g] = jnp.dot(sc_ref[...], w, preferred_element_type=F32, precision=lax.Precision.HIGHEST)


def _fold(cc, sc, w_fmix):
    shp = jax.ShapeDtypeStruct((FNET_GROUPS, FNET_CH, FNET_CH), F32)
    return pl.pallas_call(_fold_kernel, out_shape=(shp, shp), name="fold")(cc, sc, w_fmix)


KPE_LO = Q_LORA + KV_LORA
PROJ_W = 1024


def _winprep_kernel(wt_ref, o_ref):
    half = QK_ROPE // 2
    kpe_hi = KPE_LO + QK_ROPE
    o_ref[0:kpe_hi, :] = wt_ref[0:kpe_hi, :].astype(BF16)
    o_ref[kpe_hi:kpe_hi + half, :] = (-wt_ref[KPE_LO + half:kpe_hi, :]).astype(BF16)
    o_ref[kpe_hi + half:kpe_hi + QK_ROPE, :] = wt_ref[KPE_LO:KPE_LO + half, :].astype(BF16)
    o_ref[kpe_hi + QK_ROPE:PROJ_W - FNET_W, :] = jnp.zeros((PROJ_W - FNET_W - kpe_hi - QK_ROPE, D_MODEL), BF16)
    o_ref[PROJ_W - FNET_W:PROJ_W, :] = wt_ref[kpe_hi:kpe_hi + FNET_W, :].astype(BF16)


def _winprep(w_in_t):
    assert w_in_t.shape == (KPE_LO + QK_ROPE + FNET_W, D_MODEL)
    return pl.pallas_call(
        _winprep_kernel,
        out_shape=jax.ShapeDtypeStruct((PROJ_W, D_MODEL), BF16),
        compiler_params=pltpu.CompilerParams(vmem_limit_bytes=VMEM_LIMIT),
        name="winprep",
    )(w_in_t)


def _premix_body(x, m_ref, g1_ref, win_ref, qg_ref, wuq_ref, kvg_ref, wk_ref, wv_ref, wcs_ref, tq_ref, tk_ref, rope):
    shift1 = m_ref[0, 0:1, :]
    scale1 = m_ref[0, 1:2, :]
    h = _rms(x, g1_ref[...] * (1.0 + scale1)) + shift1
    proj = lax.dot_general(h.astype(BF16), win_ref[...], _NT, preferred_element_type=F32)
    qn = _rms(proj[:, 0:Q_LORA], qg_ref[...]).astype(BF16)
    qq = lax.dot_general(wuq_ref[...], qn, _NT, preferred_element_type=F32)
    cosq = tq_ref[0:LANES, :]
    sinq = tq_ref[LANES:2 * LANES, :]
    q_heads = []
    for hd in range(N_HEADS):
        lo = hd * HEAD_PAD
        qh = qq[lo:lo + HEAD_PAD, :] * cosq
        if rope:
            rot = qq[QK_W + hd * QK_ROPE:QK_W + (hd + 1) * QK_ROPE, :] * sinq[QK_NOPE:QK_NOPE + QK_ROPE, :]
            qh = qh + jnp.concatenate([jnp.zeros((QK_NOPE, rot.shape[1]), F32), rot,
                                       jnp.zeros((HEAD_PAD - QK_NOPE - QK_ROPE, rot.shape[1]), F32)], axis=0)
        q_heads.append(qh.astype(BF16))
    ckv = _rms(proj[:, Q_LORA:Q_LORA + KV_LORA], kvg_ref[...])
    kpe2 = proj[:, Q_LORA + KV_LORA:Q_LORA + KV_LORA + LANES] * tk_ref[...]
    xk = jnp.concatenate([ckv, kpe2], axis=1).astype(BF16)
    k = _dot(xk, wk_ref[...]).astype(BF16)
    v_t = lax.dot_general(wv_ref[...], xk[:, 0:KV_LORA], _NT, preferred_element_type=F32).astype(BF16)
    hw = FNET_W // 2
    f_in = proj[:, 512:1024].astype(BF16)
    z = [_dot(f_in[:, a * hw:(a + 1) * hw], wcs_ref[a * hw:(a + 1) * hw, :]) for a in range(2)]
    zc = jnp.concatenate([za[:, 0:hw] for za in z], axis=1).astype(BF16)
    zs = jnp.concatenate([za[:, hw:2 * hw] for za in z], axis=1).astype(BF16)
    kpe = proj[:, Q_LORA + KV_LORA:Q_LORA + KV_LORA + LANES]
    return q_heads, k, v_t, zc, zs, ckv, kpe


def _premix_kernel(x_ref, m_ref, g1_ref, win_ref, qg_ref, wuq_ref, kvg_ref, wk_ref, wv_ref, wcs_ref,
                   tq_ref, tk_ref, q_ref, k_ref, v_ref, zc_ref, zs_ref, *, rope):
    q_heads, k, v_t, zc, zs, _, _ = _premix_body(x_ref[...], m_ref, g1_ref, win_ref, qg_ref, wuq_ref, kvg_ref,
                                                 wk_ref, wv_ref, wcs_ref, tq_ref, tk_ref, rope)
    for hd, qh in enumerate(q_heads):
        q_ref[hd * HEAD_PAD:(hd + 1) * HEAD_PAD, :] = qh
    k_ref[...] = k
    v_ref[...] = v_t
    zc_ref[...] = zc
    zs_ref[...] = zs


def _premix(x, mods6, g1, win, qg, wuq, kvg, wk, wv, wcs, tq, tk, *, mod_row, tab_row, rope):
    n = x.shape[0]
    tm = PRE_TILE
    full = lambda a: pl.BlockSpec(a.shape, lambda i: (0,) * a.ndim)
    out_shape = [jax.ShapeDtypeStruct((QK_W, n), BF16), jax.ShapeDtypeStruct((n, QK_W), BF16),
                 jax.ShapeDtypeStruct((V_W, n), BF16), jax.ShapeDtypeStruct((n, FNET_W), BF16),
                 jax.ShapeDtypeStruct((n, FNET_W), BF16)]
    out_specs = [pl.BlockSpec((QK_W, tm), lambda i: (0, i)), pl.BlockSpec((tm, QK_W), lambda i: (i, 0)),
                 pl.BlockSpec((V_W, tm), lambda i: (0, i)), pl.BlockSpec((tm, FNET_W), lambda i: (i, 0)),
                 pl.BlockSpec((tm, FNET_W), lambda i: (i, 0))]
    return pl.pallas_call(
        functools.partial(_premix_kernel, rope=rope),
        grid=(n // tm,),
        in_specs=[pl.BlockSpec((tm, D_MODEL), lambda i: (i, 0)),
                  pl.BlockSpec((1, 6, D_MODEL), lambda i: (mod_row(i), 0, 0)),
                  full(g1), full(win), full(qg), full(wuq), full(kvg), full(wk), full(wv), full(wcs),
                  pl.BlockSpec((2 * LANES, tm), lambda i: (0, tab_row(i))),
                  pl.BlockSpec((tm, LANES), lambda i: (tab_row(i), 0))],
        out_specs=out_specs,
        out_shape=out_shape,
        compiler_params=_cparams(("parallel",)),
        name="premix",
    )(x, mods6, g1, win, qg, wuq, kvg, wk, wv, wcs, tq, tk)


def _cachekv_kernel(xk_ref, wk_ref, wv_ref, k_ref, v_ref):
    xk = xk_ref[...]
    k_ref[...] = _dot(xk, wk_ref[...]).astype(BF16)
    v_ref[...] = lax.dot_general(wv_ref[...], xk[:, 0:KV_LORA], _NT, preferred_element_type=F32).astype(BF16)


def _cachekv(xk, wk, wv):
    n = xk.shape[0]
    tm = 512
    full = lambda a: pl.BlockSpec(a.shape, lambda i: (0,) * a.ndim)
    return pl.pallas_call(
        _cachekv_kernel,
        grid=(n // tm,),
        in_specs=[pl.BlockSpec((tm, 2 * LANES), lambda i: (i, 0)), full(wk), full(wv)],
        out_specs=[pl.BlockSpec((tm, QK_W), lambda i: (i, 0)), pl.BlockSpec((V_W, tm), lambda i: (0, i))],
        out_shape=[jax.ShapeDtypeStruct((n, QK_W), BF16), jax.ShapeDtypeStruct((V_W, n), BF16)],
        compiler_params=_cparams(("parallel",)),
        name="cachekv",
    )(xk, wk, wv)


def _attn_body(q_heads, k_refs, v_refs, kc):
    tq = q_heads[0].shape[1]
    zero = jnp.zeros((HEAD_PAD, tq), BF16)
    n_pairs = len(q_heads) // 2
    qbd = [jnp.concatenate([jnp.concatenate([q_heads[2 * pr], zero], axis=1),
                            jnp.concatenate([zero, q_heads[2 * pr + 1]], axis=1)], axis=0) for pr in range(n_pairs)]
    chunks = [(k_ref, v_ref, c0, min(c0 + kc, k_ref.shape[0]))
              for k_ref, v_ref in zip(k_refs, v_refs) for c0 in range(0, k_ref.shape[0], kc)]
    work = [(pr, ch) for pr in range(n_pairs) for ch in chunks]

    def score(item):
        pr, (k_ref, _, c0, c1) = item
        return _dot(k_ref[c0:c1, pr * 2 * HEAD_PAD:(pr + 1) * 2 * HEAD_PAD], qbd[pr]).astype(BF16)

    m = [None] * n_pairs
    o = [None] * n_pairs
    s_next = score(work[0])
    for wi, (pr, (k_ref, v_ref, c0, c1)) in enumerate(work):
        s = s_next
        if wi + 1 < len(work):
            s_next = score(work[wi + 1])
        cm = jnp.max(s, axis=0, keepdims=True)
        vlo = pr * 2 * V_HEAD
        va = jnp.concatenate([v_ref[vlo:vlo + 2 * V_HEAD, c0:c1], jnp.ones((16, c1 - c0), BF16)], axis=0)
        if m[pr] is None:
            m[pr] = cm
            o[pr] = _dot(va, jnp.exp2(s - cm))
        else:
            m_new = jnp.maximum(m[pr], cm)
            alpha = jnp.exp2(m[pr].astype(F32) - m_new.astype(F32))
            o[pr] = alpha * o[pr] + _dot(va, jnp.exp2(s - m_new))
            m[pr] = m_new
    outs = []
    for pr in range(n_pairs):
        on = o[pr][0:2 * V_HEAD, :] * (1.0 / o[pr][2 * V_HEAD:2 * V_HEAD + 1, :])
        ot = jnp.concatenate([on[0:V_HEAD, 0:tq], on[V_HEAD:2 * V_HEAD, tq:2 * tq]], axis=0)
        outs.append(ot.T.astype(BF16))
    return outs


def _attn_kernel(q_ref, *refs, n_kv, n_pairs, kc):
    q_heads = [q_ref[hd * HEAD_PAD:(hd + 1) * HEAD_PAD, :] for hd in range(2 * n_pairs)]
    outs = _attn_body(q_heads, refs[:n_kv], refs[n_kv:2 * n_kv], kc)
    o_ref = refs[2 * n_kv]
    for pr, o in enumerate(outs):
        o_ref[:, pr * 2 * V_HEAD:(pr + 1) * 2 * V_HEAD] = o


def _attention(q_t, ks, vs_t, *, n_req, t_q, kv_lens, tq, pairs_per_step):
    n_kv = len(ks)
    nq = t_q // tq
    pp = pairs_per_step
    in_specs = [pl.BlockSpec((pp * 2 * HEAD_PAD, tq), lambda b, p, i: (p, b * nq + i))]
    in_specs += [pl.BlockSpec((kl, pp * 2 * HEAD_PAD), lambda b, p, i: (b, p)) for kl in kv_lens]
    in_specs += [pl.BlockSpec((pp * 2 * V_HEAD, kl), lambda b, p, i: (p, b)) for kl in kv_lens]
    return pl.pallas_call(
        functools.partial(_attn_kernel, n_kv=n_kv, n_pairs=pp, kc=KEY_CHUNK),
        grid=(n_req, N_HEADS // 2 // pp, nq),
        in_specs=in_specs,
        out_specs=pl.BlockSpec((tq, pp * 2 * V_HEAD), lambda b, p, i: (b * nq + i, p)),
        out_shape=jax.ShapeDtypeStruct((n_req * t_q, V_W), BF16),
        compiler_params=_cparams(("parallel", "parallel", "parallel")),
        name="attn",
    )(q_t, *ks, *vs_t)


def _mixout_body(x, attn_pairs, fm, wo_ref, m_ref, g2_ref, wr_ref):
    y = _dot(fm, wo_ref[V_W:V_W + FNET_W, :])
    col = 0
    for a in attn_pairs:
        y = y + _dot(a, wo_ref[col:col + a.shape[1], :])
        col += a.shape[1]
    gate1 = m_ref[0, 2:3, :]
    shift2 = m_ref[0, 3:4, :]
    scale2 = m_ref[0, 4:5, :]
    x1 = x + gate1 * y
    h2 = (_rms(x1, g2_ref[...] * (1.0 + scale2)) + shift2).astype(BF16)
    lg = lax.dot_general(wr_ref[...], h2, _NT, preferred_element_type=F32)
    e = jnp.exp(lg - jnp.max(lg, axis=0, keepdims=True))
    return x1, h2, e / jnp.sum(e, axis=0, keepdims=True)


def _mixout_kernel(x_ref, a_ref, zc_ref, zs_ref, cb_ref, sb_ref, off_ref, wo_ref, m_ref, g2_ref, wr_ref,
                   x1_ref, h2_ref, aff_ref, zp_ref, zm_ref, il_ref, *, t):
    half = t // 2
    i = pl.program_id(1)

    @pl.when(i == 0)
    def _fold_halves():
        for src, col in ((zc_ref, 0), (zs_ref, FNET_W)):
            lo = src[0:half, :]
            hi = src[half:t, :]
            zp_ref[:, col:col + FNET_W] = lo + hi
            zm_ref[:, col:col + FNET_W] = lo - hi

    off = off_ref[pl.ds(i, 1), :]
    co = off[:, 0:half]
    so = off[:, half:t]
    cb = cb_ref[...]
    sb = sb_ref[...]
    ct = (cb * co - sb * so).astype(BF16)
    st = (sb * co + cb * so).astype(BF16)
    tr = cb.shape[0]
    h = tr // 2
    even = _dot(ct[0:h, :], zp_ref[:, 0:FNET_W]) - _dot(st[0:h, :], zp_ref[:, FNET_W:2 * FNET_W])
    odd = _dot(ct[h:, :], zm_ref[:, 0:FNET_W]) - _dot(st[h:, :], zm_ref[:, FNET_W:2 * FNET_W])
    for c in range(FNET_W // LANES):
        il_ref[c, pl.ds(0, h, stride=2), :] = even[:, c * LANES:(c + 1) * LANES]
        il_ref[c, pl.ds(1, h, stride=2), :] = odd[:, c * LANES:(c + 1) * LANES]
    fm = jnp.concatenate([il_ref[c] for c in range(FNET_W // LANES)], axis=1).astype(BF16)
    x1, h2, aff = _mixout_body(x_ref[...], [a_ref[...]], fm, wo_ref, m_ref, g2_ref, wr_ref)
    x1_ref[...] = x1
    h2_ref[...] = h2
    aff_ref[...] = aff


def _ctx_front_kernel(x_ref, m_ref, g1_ref, win_ref, qg_ref, wuq_ref, kvg_ref, wk_ref, wv_ref, wcs_ref,
                      tq_ref, tk_ref, ct_ref, st_ref, wo_ref, g2_ref, wr_ref,
                      x1_ref, h2_ref, aff_ref, ckv_ref, kpe_ref, *, kc, t):
    x = x_ref[...]
    q_heads, k, v_t, zc, zs, ckv, kpe = _premix_body(x, m_ref, g1_ref, win_ref, qg_ref, wuq_ref, kvg_ref,
                                                     wk_ref, wv_ref, wcs_ref, tq_ref, tk_ref, False)
    ckv_ref[...] = ckv
    for r in range(x.shape[0] // t):
        kpe_ref[r] = kpe[r * t:(r + 1) * t, :].T[0:QK_ROPE, :]
    attn, fm = [], []
    for r in range(x.shape[0] // t):
        rows = slice(r * t, (r + 1) * t)
        attn.append(jnp.concatenate(_attn_body([q[:, rows] for q in q_heads], [k[rows, :]], [v_t[:, rows]], kc),
                                    axis=1))
        fm.append((_dot(ct_ref[...], zc[rows, :]) - _dot(st_ref[...], zs[rows, :])).astype(BF16))
    x1, h2, aff = _mixout_body(x_ref[...], [jnp.concatenate(attn, axis=0)], jnp.concatenate(fm, axis=0),
                               wo_ref, m_ref, g2_ref, wr_ref)
    x1_ref[...] = x1
    h2_ref[...] = h2
    for r in range(x.shape[0] // t):
        aff_ref[r] = aff[:, r * t:(r + 1) * t]


def _ctx_front(x, mods6, g1, win, qg, wuq, kvg, wk, wv, wcs, tq, tk, ct, st, wo, g2, wr_t, *, n_req, t, mod_row):
    full = lambda a: pl.BlockSpec(a.shape, lambda b: (0,) * a.ndim)
    rps = CTX_REQS
    row = lambda w: pl.BlockSpec((rps * t, w), lambda b: (b, 0))
    return pl.pallas_call(
        functools.partial(_ctx_front_kernel, kc=KEY_CHUNK, t=t),
        grid=(n_req // rps,),
        in_specs=[row(D_MODEL), pl.BlockSpec((1, 6, D_MODEL), lambda b: (mod_row, 0, 0)),
                  full(g1), full(win), full(qg), full(wuq), full(kvg), full(wk), full(wv), full(wcs),
                  full(tq), full(tk), full(ct), full(st), full(wo), full(g2), full(wr_t)],
        out_specs=[row(D_MODEL), row(D_MODEL), pl.BlockSpec((rps, N_EXPERTS, t), lambda b: (b, 0, 0)),
                   row(KV_LORA), pl.BlockSpec((rps, QK_ROPE, t), lambda b: (b, 0, 0))],
        out_shape=[jax.ShapeDtypeStruct((n_req * t, D_MODEL), F32),
                   jax.ShapeDtypeStruct((n_req * t, D_MODEL), BF16),
                   jax.ShapeDtypeStruct((n_req, N_EXPERTS, t), F32),
                   jax.ShapeDtypeStruct((n_req * t, KV_LORA), F32),
                   jax.ShapeDtypeStruct((n_req, QK_ROPE, t), F32)],
        compiler_params=_cparams(("parallel",)),
        name="ctx_front",
    )(x, mods6, g1, win, qg, wuq, kvg, wk, wv, wcs, tq, tk, ct, st, wo, g2, wr_t)


def _mixout(x, attn, zc, zs, cb, sb, off, wo, mods6, g2, wr_t, *, n_req, t, mod_row):
    tr = MIX_TILE
    nr = t // tr
    full = lambda a: pl.BlockSpec(a.shape, lambda b, i: (0,) * a.ndim)
    return pl.pallas_call(
        functools.partial(_mixout_kernel, t=t),
        grid=(n_req, nr),
        in_specs=[pl.BlockSpec((tr, D_MODEL), lambda b, i: (b * nr + i, 0)),
                  pl.BlockSpec((tr, V_W), lambda b, i: (b * nr + i, 0)),
                  pl.BlockSpec((t, FNET_W), lambda b, i: (b, 0)),
                  pl.BlockSpec((t, FNET_W), lambda b, i: (b, 0)),
                  full(cb), full(sb), full(off),
                  full(wo),
                  pl.BlockSpec((1, 6, D_MODEL), lambda b, i: (mod_row(b), 0, 0)),
                  full(g2), full(wr_t)],
        out_specs=[pl.BlockSpec((tr, D_MODEL), lambda b, i: (b * nr + i, 0)),
                   pl.BlockSpec((tr, D_MODEL), lambda b, i: (b * nr + i, 0)),
                   pl.BlockSpec((None, N_EXPERTS, tr), lambda b, i: (b, 0, i))],
        out_shape=[jax.ShapeDtypeStruct((n_req * t, D_MODEL), F32),
                   jax.ShapeDtypeStruct((n_req * t, D_MODEL), BF16),
                   jax.ShapeDtypeStruct((n_req, N_EXPERTS, t), F32)],
        scratch_shapes=[pltpu.VMEM((t // 2, 2 * FNET_W), BF16), pltpu.VMEM((t // 2, 2 * FNET_W), BF16),
                        pltpu.VMEM((FNET_W // LANES, tr, LANES), F32)],
        compiler_params=_cparams(("parallel", "arbitrary")),
        name="mixout",
    )(x, attn, zc, zs, cb, sb, off, wo, mods6, g2, wr_t)


def _prefix_count(flags, tri):
    n = flags.shape[1]
    carry = None
    outs = []
    ends = []
    for j in range(n // TOK_TILE):
        c = _dot(flags[:, j * TOK_TILE:(j + 1) * TOK_TILE].astype(BF16), tri)
        if carry is not None:
            c = c + carry
        outs.append(c)
        carry = c[:, TOK_TILE - 1:TOK_TILE]
        ends.append(carry)
    return (outs[0] if len(outs) == 1 else jnp.concatenate(outs, axis=1)), ends


def _route_kernel(aff_ref, pos_ref, off_ref, *, cap):
    a = aff_ref[...]
    rows = a.shape[0]
    capf = jnp.float32(cap)
    thr = jnp.zeros((rows, 1), jnp.int32)
    for bit in range(30, -1, -1):
        cand = thr | jnp.int32(1 << bit)
        cand_f = lax.bitcast_convert_type(cand, F32)
        cnt = jnp.sum(jnp.where(a >= cand_f, 1.0, 0.0), axis=1, keepdims=True)
        thr = jnp.where(cnt >= capf, cand, thr)
    thr_f = lax.bitcast_convert_type(thr, F32)
    above_f = lax.bitcast_convert_type(thr + 1, F32)
    gt = jnp.where(a >= above_f, 1.0, 0.0)
    tie = jnp.where(a >= thr_f, 1.0, 0.0) - gt
    need = capf - jnp.sum(gt, axis=1, keepdims=True)
    r_i = lax.broadcasted_iota(jnp.int32, (TOK_TILE, TOK_TILE), 0)
    c_i = lax.broadcasted_iota(jnp.int32, (TOK_TILE, TOK_TILE), 1)
    tri = jnp.where(r_i <= c_i, 1.0, 0.0).astype(BF16)
    tie_before = _prefix_count(tie, tri)[0] - tie
    sel = gt + tie * jnp.where(tie_before < need, 1.0, 0.0)
    count, ends = _prefix_count(sel, tri)
    pos_ref[...] = jnp.where(sel > 0.5, count - 1.0, -1.0)
    lane = lax.broadcasted_iota(jnp.int32, (rows, LANES), 1)
    offs = jnp.zeros((rows, LANES), F32)
    for j, end in enumerate(ends):
        offs = offs + jnp.where(lane == j + 1, end, 0.0)
    off_ref[...] = offs.astype(jnp.int32)


def _route(aff_t, cap):
    return pl.pallas_call(
        functools.partial(_route_kernel, cap=cap),
        out_shape=[jax.ShapeDtypeStruct(aff_t.shape, F32),
                   jax.ShapeDtypeStruct((aff_t.shape[0], LANES), jnp.int32)],
        compiler_params=pltpu.CompilerParams(vmem_limit_bytes=VMEM_LIMIT),
        name="route",
    )(aff_t)


def _gather_kernel(pos_ref, aff_ref, h_ref, xs_ref, g_ref, *, cap):
    rps, ne, n = pos_ref.shape
    slot = lax.broadcasted_iota(jnp.int32, (ne, cap, n), 1).astype(F32)
    for r in range(rps):
        pos = pos_ref[r]
        aff = aff_ref[r]
        hit = pos[:, None, :] == slot
        onehot = jnp.where(hit, 1.0, 0.0).reshape(ne * cap, n).astype(BF16)
        xs = _dot(onehot, h_ref[r * n:(r + 1) * n, :])
        xs_ref[:, r * cap:(r + 1) * cap, :] = xs.astype(BF16).reshape(ne, cap, D_MODEL)
        g_ref[:, r * cap:(r + 1) * cap, :] = jnp.sum(jnp.where(hit, aff[:, None, :], 0.0), axis=2, keepdims=True)


def _gather(pos_t, aff_t, h2, *, n_req, n, cap, rps):
    pos3 = pos_t.reshape(n_req, N_EXPERTS, n)
    aff3 = aff_t.reshape(n_req, N_EXPERTS, n)
    return pl.pallas_call(
        functools.partial(_gather_kernel, cap=cap),
        grid=(n_req // rps,),
        in_specs=[pl.BlockSpec((rps, N_EXPERTS, n), lambda b: (b, 0, 0)),
                  pl.BlockSpec((rps, N_EXPERTS, n), lambda b: (b, 0, 0)),
                  pl.BlockSpec((rps * n, D_MODEL), lambda b: (b, 0))],
        out_specs=[pl.BlockSpec((N_EXPERTS, rps * cap, D_MODEL), lambda b: (0, b, 0)),
                   pl.BlockSpec((N_EXPERTS, rps * cap, 1), lambda b: (0, b, 0))],
        out_shape=[jax.ShapeDtypeStruct((N_EXPERTS, n_req * cap, D_MODEL), BF16),
                   jax.ShapeDtypeStruct((N_EXPERTS, n_req * cap, 1), F32)],
        compiler_params=_cparams(("parallel",)),
        name="gather",
    )(pos3, aff3, h2)


def _window_plan(off_ref, b, j, cap, w):
    w0 = []
    need = jnp.int32(0)
    for e in range(N_EXPERTS):
        idx = (b * N_EXPERTS + e) * OFF_STRIDE + j
        base = (off_ref[idx] >> 4) << 4
        w0.append(base)
        need = jnp.maximum(need, off_ref[idx + 1] - base)
    return w0, (need + (w - 1)) >> (w.bit_length() - 1)


def _window(w0_e, p, cap, w):
    low = w0_e + p * w
    start = pl.multiple_of(jnp.minimum(low, cap - w), 16)
    return low, start


def _gather_win_kernel(off_ref, pos_ref, aff_ref, h_ref, xs_ref, g_ref, *, cap, w):
    b = pl.program_id(0)
    step = pl.program_id(1)

    @pl.when(step == 0)
    def _init():
        xs_ref[...] = jnp.zeros_like(xs_ref)
        g_ref[...] = jnp.zeros_like(g_ref)

    r = lax.broadcasted_iota(jnp.int32, (w, 1), 0).astype(F32)
    n_sub = pos_ref.shape[1] // TOK_TILE
    for sub in range(n_sub):
        cols = slice(sub * TOK_TILE, (sub + 1) * TOK_TILE)
        pos = pos_ref[:, cols]
        aff = aff_ref[:, cols]
        h = h_ref[cols, :]
        w0, n_pass = _window_plan(off_ref, b, step * n_sub + sub, cap, w)

        def one_pass(p, carry, pos=pos, aff=aff, h=h, w0=w0):
            starts, hots, gates = [], [], []
            for e in range(N_EXPERTS):
                low, start = _window(w0[e], p, cap, w)
                starts.append(start)
                mine = jnp.where(r >= (low - start).astype(F32), 1.0, 0.0)
                hot = jnp.where(pos[e:e + 1, :] - start.astype(F32) == r, mine, 0.0)
                hots.append(hot)
                gates.append(jnp.sum(hot * aff[e:e + 1, :], axis=1, keepdims=True))
            rows = _dot(jnp.concatenate(hots, axis=0).astype(BF16), h).astype(BF16)
            for e in range(N_EXPERTS):
                win = pl.ds(starts[e], w)
                xs_ref[e, win, :] = xs_ref[e, win, :] + rows[e * w:(e + 1) * w, :]
                g_ref[e, win, :] = g_ref[e, win, :] + gates[e]
            return carry

        one_pass(0, 0)
        lax.fori_loop(1, n_pass, one_pass, 0)


def _gather_win(offs, pos_t, aff_t, h2, *, n_req, n, cap):
    tb = WIN_BLOCKS * TOK_TILE
    nb = n // tb
    grid_spec = pltpu.PrefetchScalarGridSpec(
        num_scalar_prefetch=1,
        grid=(n_req, nb),
        in_specs=[pl.BlockSpec((N_EXPERTS, tb), lambda b, j, off: (b, j)),
                  pl.BlockSpec((N_EXPERTS, tb), lambda b, j, off: (b, j)),
                  pl.BlockSpec((tb, D_MODEL), lambda b, j, off: (b * nb + j, 0))],
        out_specs=[pl.BlockSpec((N_EXPERTS, cap, D_MODEL), lambda b, j, off: (0, b, 0)),
                   pl.BlockSpec((N_EXPERTS, cap, 1), lambda b, j, off: (0, b, 0))])
    return pl.pallas_call(
        functools.partial(_gather_win_kernel, cap=cap, w=SLOT_WIN),
        grid_spec=grid_spec,
        out_shape=[jax.ShapeDtypeStruct((N_EXPERTS, n_req * cap, D_MODEL), BF16),
                   jax.ShapeDtypeStruct((N_EXPERTS, n_req * cap, 1), F32)],
        compiler_params=_cparams(("parallel", "arbitrary")),
        name="gather_win",
    )(offs, pos_t, aff_t, h2)


def _ffn_kernel(xc_ref, xl_ref, gc_ref, gl_ref, wg_ref, wu_ref, wd_ref, yc_ref, yl_ref, wgb, wub, wdb):
    wgb[...] = wg_ref[0].astype(BF16)
    wub[...] = wu_ref[0].astype(BF16)
    wdb[...] = wd_ref[0].astype(BF16)
    for x_ref, g_ref, y_ref in ((xc_ref, gc_ref, yc_ref), (xl_ref, gl_ref, yl_ref)):
        for j in range(x_ref.shape[1] // FFN_ROWS):
            rows = slice(j * FFN_ROWS, (j + 1) * FFN_ROWS)
            x = x_ref[0, rows, :]
            gate = _dot(x, wgb[...])
            up = _dot(x, wub[...])
            hid = (gate * jax.nn.sigmoid(gate) * up).astype(BF16)
            ys = _dot(hid, wdb[...]) * g_ref[0, rows, :]
            y_ref[0, rows, :] = ys.astype(BF16)


def _ffn(xc, xl, gc, gl, wg, wu, wd):
    m = xc.shape[1]
    xspec = pl.BlockSpec((1, m, D_MODEL), lambda e: (e, 0, 0))
    gspec = pl.BlockSpec((1, m, 1), lambda e: (e, 0, 0))
    shp = jax.ShapeDtypeStruct((N_EXPERTS, m, D_MODEL), BF16)
    return pl.pallas_call(
        _ffn_kernel,
        grid=(N_EXPERTS,),
        in_specs=[xspec, xspec, gspec, gspec,
                  pl.BlockSpec((1, D_MODEL, D_EXPERT), lambda e: (e, 0, 0)),
                  pl.BlockSpec((1, D_MODEL, D_EXPERT), lambda e: (e, 0, 0)),
                  pl.BlockSpec((1, D_EXPERT, D_MODEL), lambda e: (e, 0, 0))],
        out_specs=[xspec, xspec],
        out_shape=[shp, shp],
        scratch_shapes=[pltpu.VMEM((D_MODEL, D_EXPERT), BF16), pltpu.VMEM((D_MODEL, D_EXPERT), BF16),
                        pltpu.VMEM((D_EXPERT, D_MODEL), BF16)],
        compiler_params=_cparams(("arbitrary",)),
        name="ffn",
    )(xc, xl, gc, gl, wg, wu, wd)


def _combine_kernel(x1_ref, pos_ref, ys_ref, m_ref, fg_ref, o_ref, *, cap):
    rps, n, _ = pos_ref.shape
    w = N_EXPERTS * cap
    e_i = lax.broadcasted_iota(jnp.int32, (N_EXPERTS, w), 0)
    j_i = lax.broadcasted_iota(jnp.int32, (N_EXPERTS, w), 1)
    spread = jnp.where((j_i >> (cap.bit_length() - 1)) == e_i, 1.0, 0.0).astype(BF16)
    lane_slot = (lax.broadcasted_iota(jnp.int32, (1, w), 1) & (cap - 1)).astype(F32)
    gate2 = m_ref[0, 5:6, :]
    for r in range(rps):
        pos = pos_ref[r].astype(BF16)
        onehot = jnp.where(_dot(pos, spread) == lane_slot, 1.0, 0.0).astype(BF16)
        acc = _dot(onehot, ys_ref[:, r * cap:(r + 1) * cap, :].reshape(w, D_MODEL))
        rows = slice(r * n, (r + 1) * n)
        o_ref[rows, :] = _rms(x1_ref[rows, :] + gate2 * acc, fg_ref[...])


def _combine(x1, pos_tok, ys, mods6, fg, *, n_req, n, cap, mod_row, rps):
    return pl.pallas_call(
        functools.partial(_combine_kernel, cap=cap),
        grid=(n_req // rps,),
        in_specs=[pl.BlockSpec((rps * n, D_MODEL), lambda b: (b, 0)),
                  pl.BlockSpec((rps, n, N_EXPERTS), lambda b: (b, 0, 0)),
                  pl.BlockSpec((N_EXPERTS, rps * cap, D_MODEL), lambda b: (0, b, 0)),
                  pl.BlockSpec((1, 6, D_MODEL), lambda b: (mod_row, 0, 0)),
                  pl.BlockSpec((1, D_MODEL), lambda b: (0, 0))],
        out_specs=pl.BlockSpec((rps * n, D_MODEL), lambda b: (b, 0)),
        out_shape=jax.ShapeDtypeStruct((n_req * n, D_MODEL), F32),
        compiler_params=_cparams(("parallel",)),
        name="combine",
    )(x1, pos_tok, ys, mods6, fg)


def _combine_win_kernel(off_ref, x1_ref, pos_ref, ys_ref, m_ref, fg_ref, o_ref, acc_ref, *, cap, w):
    b = pl.program_id(0)
    step = pl.program_id(1)
    width = N_EXPERTS * w
    e_i = lax.broadcasted_iota(jnp.int32, (N_EXPERTS, width), 0)
    j_i = lax.broadcasted_iota(jnp.int32, (N_EXPERTS, width), 1)
    spread = jnp.where((j_i >> (w.bit_length() - 1)) == e_i, 1.0, 0.0).astype(BF16)
    lane_slot = (lax.broadcasted_iota(jnp.int32, (1, width), 1) & (w - 1)).astype(F32)
    lane_e = lax.broadcasted_iota(jnp.int32, (1, N_EXPERTS), 1)
    n_sub = pos_ref.shape[0] // TOK_TILE
    for sub in range(n_sub):
        rows = slice(sub * TOK_TILE, (sub + 1) * TOK_TILE)
        pos = pos_ref[rows, :]
        w0, n_pass = _window_plan(off_ref, b, step * n_sub + sub, cap, w)

        def window_sum(p, pos=pos, w0=w0):
            start_row = jnp.zeros((1, N_EXPERTS), F32)
            first_row = jnp.zeros((1, N_EXPERTS), F32)
            wins = []
            for e in range(N_EXPERTS):
                low, start = _window(w0[e], p, cap, w)
                start_row = jnp.where(lane_e == e, start.astype(F32), start_row)
                first_row = jnp.where(lane_e == e, (low - start).astype(F32), first_row)
                wins.append(ys_ref[e, pl.ds(start, w), :])
            rel = pos - start_row
            rel = jnp.where(rel >= first_row, rel, -1.0).astype(BF16)
            onehot = jnp.where(_dot(rel, spread) == lane_slot, 1.0, 0.0).astype(BF16)
            return _dot(onehot, jnp.concatenate(wins, axis=0))

        def finish(acc, rows=rows):
            o_ref[rows, :] = _rms(x1_ref[rows, :] + m_ref[0, 5:6, :] * acc, fg_ref[...])

        acc0 = window_sum(0)
        acc_ref[sub] = acc0
        finish(acc0)

        @pl.when(n_pass > 1)
        def _more_passes(sub=sub, n_pass=n_pass, window_sum=window_sum, finish=finish):
            def one_pass(p, carry):
                acc_ref[sub] += window_sum(p)
                return carry

            lax.fori_loop(1, n_pass, one_pass, 0)
            finish(acc_ref[sub])


def _combine_win(offs, x1, pos_tok, ys, mods6, fg, *, n_req, n, cap, mod_row):
    tr = WIN_BLOCKS * TOK_TILE
    nr = n // tr
    grid_spec = pltpu.PrefetchScalarGridSpec(
        num_scalar_prefetch=1,
        grid=(n_req, nr),
        in_specs=[pl.BlockSpec((tr, D_MODEL), lambda b, i, off: (b * nr + i, 0)),
                  pl.BlockSpec((None, tr, N_EXPERTS), lambda b, i, off: (b, i, 0)),
                  pl.BlockSpec((N_EXPERTS, cap, D_MODEL), lambda b, i, off: (0, b, 0)),
                  pl.BlockSpec((1, 6, D_MODEL), lambda b, i, off: (mod_row(b), 0, 0)),
                  pl.BlockSpec((1, D_MODEL), lambda b, i, off: (0, 0))],
        out_specs=pl.BlockSpec((tr, D_MODEL), lambda b, i, off: (b * nr + i, 0)),
        scratch_shapes=[pltpu.VMEM((WIN_BLOCKS, TOK_TILE, D_MODEL), F32)])
    return pl.pallas_call(
        functools.partial(_combine_win_kernel, cap=cap, w=SLOT_WIN),
        grid_spec=grid_spec,
        out_shape=jax.ShapeDtypeStruct((n_req * n, D_MODEL), F32),
        compiler_params=_cparams(("parallel", "parallel")),
        name="combine_win",
    )(offs, x1, pos_tok, ys, mods6, fg)


def _rot_half(w):
    half = QK_ROPE // 2
    return jnp.concatenate([-w[..., half:], w[..., :half]], axis=-1)


def _rope_tables(t):
    n_rows = t // GRID_W
    rows = np.repeat(np.arange(n_rows, dtype=np.float64), GRID_W)
    cols = np.tile(np.arange(GRID_W, dtype=np.float64), n_rows)
    n_freq = QK_ROPE // 4
    inv_freq = ROPE_BASE ** (-np.arange(n_freq, dtype=np.float64) / n_freq)
    ang = np.concatenate([rows[:, None] * inv_freq, cols[:, None] * inv_freq], axis=-1)
    cos = np.concatenate([np.cos(ang), np.cos(ang)], axis=-1)
    sin = np.concatenate([np.sin(ang), np.sin(ang)], axis=-1)
    return cos, sin


def _qk_tables(cos, sin):
    t = cos.shape[0]
    scale = (QK_NOPE + QK_ROPE) ** -0.5 * np.log2(np.e)
    pad = np.zeros((t, HEAD_PAD - QK_NOPE - QK_ROPE))
    cosq = np.concatenate([np.full((t, QK_NOPE), scale), cos * scale, pad], axis=1)
    sinq = np.concatenate([np.zeros((t, QK_NOPE)), sin * scale, pad], axis=1)
    tq_t = np.concatenate([cosq, sinq], axis=1).T
    tk = np.concatenate([cos, sin, np.zeros((t, LANES - 2 * QK_ROPE))], axis=1)
    return jnp.asarray(tq_t, F32), jnp.asarray(tk, F32)


def _dft_angles(rows, t):
    k = np.arange(t, dtype=np.int64)
    return ((rows[:, None] * k[None, :]) % t).astype(np.float64) * (2.0 * np.pi / t)


def _dft_tables(t):
    ang = _dft_angles(np.arange(t, dtype=np.int64), t)
    scale = (t * FNET_CH) ** -0.5
    return jnp.asarray(np.cos(ang) * scale, F32).astype(BF16), jnp.asarray(np.sin(ang) * scale, F32).astype(BF16)


def _dft_half_tables(t):
    r = np.arange(MIX_TILE, dtype=np.int64)
    ang = _dft_angles(np.concatenate([r[0::2], r[1::2]]), t)[:, :t // 2]
    scale = (t * FNET_CH) ** -0.5
    ang_off = _dft_angles(np.arange(t // MIX_TILE, dtype=np.int64) * MIX_TILE, t)[:, :t // 2]
    off = np.concatenate([np.cos(ang_off), np.sin(ang_off)], axis=1)
    return jnp.asarray(np.cos(ang) * scale, F32), jnp.asarray(np.sin(ang) * scale, F32), jnp.asarray(off, F32)


def _block_diag(w):
    g, a, b = w.shape
    eye = jnp.eye(g, dtype=w.dtype)
    return (eye[:, None, :, None] * w[:, :, None, :]).reshape(g * a, g * b)


def kernel(x_prompt, x_sample, cache_ckv, cache_kpe, c, c_ctx, w_mod, b_mod, norm1_g, w_in, q_norm_g, w_uq,
           kv_norm_g, w_ukv, w_fmix, w_out, norm2_g, w_router, w_e_gate, w_e_up, w_e_down, final_g):
    assert w_mod.shape[0] == 1, "single-layer problem"
    n_ctx, t_ctx, _ = x_prompt.shape
    n_lat, t_lat, _ = x_sample.shape
    past = cache_ckv.shape[2]
    ctx_row = n_lat

    win = _winprep(jnp.swapaxes(w_in, 1, 2)[0])
    wq3 = w_uq[0].reshape(Q_LORA, N_HEADS, QK_NOPE + QK_ROPE)
    qpad = jnp.zeros((Q_LORA, N_HEADS, HEAD_PAD - QK_NOPE - QK_ROPE), F32)
    wuq_main = jnp.concatenate([wq3, qpad], axis=2).reshape(Q_LORA, QK_W)
    wuq_rot = _rot_half(wq3[..., QK_NOPE:]).reshape(Q_LORA, N_HEADS * QK_ROPE)
    wuq_lat = jnp.concatenate([wuq_main, wuq_rot], axis=1).T.astype(BF16)
    wuq_ctx = wuq_main.T.astype(BF16)
    wkv3 = w_ukv[0].reshape(KV_LORA, N_HEADS, QK_NOPE + V_HEAD)
    wk_top = jnp.concatenate([wkv3[..., :QK_NOPE], jnp.zeros((KV_LORA, N_HEADS, HEAD_PAD - QK_NOPE), F32)],
                             axis=2).reshape(KV_LORA, QK_W)
    place = jnp.concatenate([jnp.zeros((QK_ROPE, QK_NOPE), F32), jnp.eye(QK_ROPE, dtype=F32),
                             jnp.zeros((QK_ROPE, HEAD_PAD - QK_NOPE - QK_ROPE), F32)], axis=1)
    place = jnp.tile(place, (1, N_HEADS))
    wk = jnp.concatenate([wk_top, place, place, jnp.zeros((LANES - 2 * QK_ROPE, QK_W), F32)], axis=0).astype(BF16)
    wv = wkv3[..., QK_NOPE:].reshape(KV_LORA, V_W).T.astype(BF16)
    wo = w_out[0].astype(BF16)
    wr_t = w_router[0].T.astype(BF16)

    cos, sin = _rope_tables(t_lat)
    tq_lat, tk_lat = _qk_tables(cos, sin)
    assert n_ctx % CTX_REQS == 0
    tq_ctx, tk_ctx = _qk_tables(np.ones((CTX_REQS * t_ctx, QK_ROPE)), np.zeros((CTX_REQS * t_ctx, QK_ROPE)))
    ch_ang = _dft_angles(np.arange(FNET_CH, dtype=np.int64), FNET_CH)
    dft_ctx = _dft_tables(t_ctx)
    dft_lat = _dft_half_tables(t_lat)

    assert n_lat + 1 <= SUBLANES
    c8 = jnp.concatenate([c, c_ctx[None, :], jnp.zeros((SUBLANES - n_lat - 1, D_MODEL), F32)], axis=0)
    mods6 = _mods(c8, w_mod[0], b_mod[0][None, :]).reshape(SUBLANES, 6, D_MODEL)
    cw, sw = _fold(jnp.asarray(np.cos(ch_ang), F32), jnp.asarray(np.sin(ch_ang), F32), w_fmix[0])
    gh = FNET_GROUPS // 2
    wcs = jnp.concatenate([jnp.concatenate([_block_diag(cw[a * gh:(a + 1) * gh]), _block_diag(sw[a * gh:(a + 1) * gh])],
                                           axis=1) for a in range(2)], axis=0).astype(BF16)

    g1 = norm1_g[0][None, :]
    qg = q_norm_g[0][None, :]
    kvg = kv_norm_g[0][None, :]
    g2 = norm2_g[0][None, :]
    fg = final_g[None, :]

    xp = x_prompt.reshape(n_ctx * t_ctx, D_MODEL)
    xs = x_sample.reshape(n_lat * t_lat, D_MODEL)
    tiles_lat = t_lat // PRE_TILE

    x1c, h2c, affc, ckv_c, kpe_c = _ctx_front(
        xp, mods6, g1, win, qg, wuq_ctx, kvg, wk, wv, wcs, tq_ctx, tk_ctx,
        *dft_ctx, wo, g2, wr_t, n_req=n_ctx, t=t_ctx, mod_row=ctx_row)
    ql, kl, vl, zcl, zsl = _premix(
        xs, mods6, g1, win, qg, wuq_lat, kvg, wk, wv, wcs, tq_lat, tk_lat,
        mod_row=lambda i: i // tiles_lat, tab_row=lambda i: i % tiles_lat, rope=True)
    xk_cache = jnp.concatenate([cache_ckv[:, 0], cache_kpe[:, 0],
                                jnp.zeros((n_lat, past, 2 * LANES - KV_LORA - QK_ROPE), F32)],
                               axis=-1).reshape(n_lat * past, 2 * LANES).astype(BF16)
    kpast, vpast = _cachekv(xk_cache, wk, wv)

    attn_l = _attention(ql, [kpast, kl], [vpast, vl], n_req=n_lat, t_q=t_lat, kv_lens=[past, t_lat], tq=TOK_TILE,
                        pairs_per_step=N_HEADS // 2)

    x1l, h2l, affl = _mixout(xs, attn_l, zcl, zsl, *dft_lat, wo, mods6, g2, wr_t,
                             n_req=n_lat, t=t_lat, mod_row=lambda b: b)

    cap_c = CAP_FACTOR * t_ctx // N_EXPERTS
    cap_l = CAP_FACTOR * t_lat // N_EXPERTS
    affc2 = affc.reshape(n_ctx * N_EXPERTS, t_ctx)
    affl2 = affl.reshape(n_lat * N_EXPERTS, t_lat)
    posc, _ = _route(affc2, cap_c)
    posl, offl = _route(affl2, cap_l)
    assert t_lat // TOK_TILE + 1 <= OFF_STRIDE and cap_l % SLOT_WIN == 0
    offl = offl[:, :OFF_STRIDE].reshape(-1)
    xsc, gc = _gather(posc, affc2, h2c, n_req=n_ctx, n=t_ctx, cap=cap_c, rps=MOE_REQS)
    xsl, gl = _gather_win(offl, posl, affl2, h2l, n_req=n_lat, n=t_lat, cap=cap_l)
    ysc, ysl = _ffn(xsc, xsl, gc, gl, w_e_gate[0], w_e_up[0], w_e_down[0])

    posc_tok = posc.reshape(n_ctx, N_EXPERTS, t_ctx).transpose(0, 2, 1)
    posl_tok = posl.reshape(n_lat, N_EXPERTS, t_lat).transpose(0, 2, 1)
    y_prompt = _combine(x1c, posc_tok, ysc, mods6, fg, n_req=n_ctx, n=t_ctx, cap=cap_c, mod_row=ctx_row,
                        rps=MOE_REQS)
    y_sample = _combine_win(offl, x1l, posl_tok, ysl, mods6, fg, n_req=n_lat, n=t_lat, cap=cap_l,
                            mod_row=lambda b: b)

    return (y_prompt.reshape(n_ctx, t_ctx, D_MODEL), y_sample.reshape(n_lat, t_lat, D_MODEL),
            ckv_c.reshape(n_ctx, 1, t_ctx, KV_LORA), kpe_c.transpose(0, 2, 1).reshape(n_ctx, 1, t_ctx, QK_ROPE))
```

```python
import functools

import jax
import jax.numpy as jnp
import numpy as np
from jax import lax
from jax.experimental import pallas as pl
from jax.experimental.pallas import tpu as pltpu

F32 = jnp.float32
BF16 = jnp.bfloat16

D_MODEL = 1024
N_HEADS = 8
QK_NOPE = 64
QK_ROPE = 32
V_HEAD = 64
Q_LORA = 256
KV_LORA = 128
FNET_GROUPS = 8
FNET_CH = 64
FNET_W = FNET_GROUPS * FNET_CH
N_EXPERTS = 16
CAP_FACTOR = 2
D_EXPERT = 512
GRID_W = 64
ROPE_BASE = 10000.0
EPS = 1e-6

LANES = 128
SUBLANES = 8
HEAD_PAD = LANES
QK_W = N_HEADS * HEAD_PAD
V_W = N_HEADS * V_HEAD
TOK_TILE = 256
PRE_TILE = 1024
MIX_TILE = 512
FFN_ROWS = 1024
MODS_TILE = 1536
CTX_REQS = 4
MOE_REQS = 4
KEY_CHUNK = 512
SLOT_WIN = 64
WIN_BLOCKS = 4
OFF_STRIDE = 16
VMEM_LIMIT = 48 * 1024 * 1024

_NT = (((1,), (1,)), ((), ()))


def _cparams(sem):
    return pltpu.CompilerParams(dimension_semantics=sem, vmem_limit_bytes=VMEM_LIMIT)


def _rms(x, g):
    return x * lax.rsqrt(jnp.mean(x * x, axis=-1, keepdims=True) + EPS) * g


def _dot(a, b):
    return jnp.dot(a, b, preferred_element_type=F32)


def _mods_kernel(c_ref, w_ref, b_ref, o_ref):
    c = c_ref[...]
    s = c * jax.nn.sigmoid(c)
    o_ref[...] = _dot(s.astype(BF16), w_ref[...].astype(BF16)) + b_ref[...]


def _mods(c8, w_mod, b_mod):
    n = w_mod.shape[1]
    tn = MODS_TILE
    return pl.pallas_call(
        _mods_kernel,
        grid=(n // tn,),
        in_specs=[pl.BlockSpec((SUBLANES, D_MODEL), lambda j: (0, 0)),
                  pl.BlockSpec((D_MODEL, tn), lambda j: (0, j)),
                  pl.BlockSpec((1, tn), lambda j: (0, j))],
        out_specs=pl.BlockSpec((SUBLANES, tn), lambda j: (0, j)),
        out_shape=jax.ShapeDtypeStruct((SUBLANES, n), F32),
        compiler_params=_cparams(("arbitrary",)),
        name="mods",
    )(c8, w_mod, b_mod)


def _fold_kernel(cc_ref, sc_ref, w_ref, cw_ref, sw_ref):
    for g in range(FNET_GROUPS):
        w = w_ref[g]
        cw_ref[g] = jnp.dot(cc_ref[...], w, preferred_element_type=F32, precision=lax.Precision.HIGHEST)
        sw_ref[g] = jnp.dot(sc_ref[...], w, preferred_element_type=F32, precision=lax.Precision.HIGHEST)


def _fold(cc, sc, w_fmix):
    shp = jax.ShapeDtypeStruct((FNET_GROUPS, FNET_CH, FNET_CH), F32)
    return pl.pallas_call(_fold_kernel, out_shape=(shp, shp), name="fold")(cc, sc, w_fmix)


KPE_LO = Q_LORA + KV_LORA
PROJ_W = 1024


def _winprep_kernel(wt_ref, o_ref):
    half = QK_ROPE // 2
    kpe_hi = KPE_LO + QK_ROPE
    o_ref[0:kpe_hi, :] = wt_ref[0:kpe_hi, :].astype(BF16)
    o_ref[kpe_hi:kpe_hi + half, :] = (-wt_ref[KPE_LO + half:kpe_hi, :]).astype(BF16)
    o_ref[kpe_hi + half:kpe_hi + QK_ROPE, :] = wt_ref[KPE_LO:KPE_LO + half, :].astype(BF16)
    o_ref[kpe_hi + QK_ROPE:PROJ_W - FNET_W, :] = jnp.zeros((PROJ_W - FNET_W - kpe_hi - QK_ROPE, D_MODEL), BF16)
    o_ref[PROJ_W - FNET_W:PROJ_W, :] = wt_ref[kpe_hi:kpe_hi + FNET_W, :].astype(BF16)


def _winprep(w_in_t):
    assert w_in_t.shape == (KPE_LO + QK_ROPE + FNET_W, D_MODEL)
    return pl.pallas_call(
        _winprep_kernel,
        out_shape=jax.ShapeDtypeStruct((PROJ_W, D_MODEL), BF16),
        compiler_params=pltpu.CompilerParams(vmem_limit_bytes=VMEM_LIMIT),
        name="winprep",
    )(w_in_t)


def _premix_body(x, m_ref, g1_ref, win_ref, qg_ref, wuq_ref, kvg_ref, wk_ref, wv_ref, wcs_ref, tq_ref, tk_ref, rope):
    shift1 = m_ref[0, 0:1, :]
    scale1 = m_ref[0, 1:2, :]
    h = _rms(x, g1_ref[...] * (1.0 + scale1)) + shift1
    proj = lax.dot_general(h.astype(BF16), win_ref[...], _NT, preferred_element_type=F32)
    qn = _rms(proj[:, 0:Q_LORA], qg_ref[...]).astype(BF16)
    qq = lax.dot_general(wuq_ref[...], qn, _NT, preferred_element_type=F32)
    cosq = tq_ref[0:LANES, :]
    sinq = tq_ref[LANES:2 * LANES, :]
    q_heads = []
    for hd in range(N_HEADS):
        lo = hd * HEAD_PAD
        qh = qq[lo:lo + HEAD_PAD, :] * cosq
        if rope:
            rot = qq[QK_W + hd * QK_ROPE:QK_W + (hd + 1) * QK_ROPE, :] * sinq[QK_NOPE:QK_NOPE + QK_ROPE, :]
            qh = qh + jnp.concatenate([jnp.zeros((QK_NOPE, rot.shape[1]), F32), rot,
                                       jnp.zeros((HEAD_PAD - QK_NOPE - QK_ROPE, rot.shape[1]), F32)], axis=0)
        q_heads.append(qh.astype(BF16))
    ckv = _rms(proj[:, Q_LORA:Q_LORA + KV_LORA], kvg_ref[...])
    kpe2 = proj[:, Q_LORA + KV_LORA:Q_LORA + KV_LORA + LANES] * tk_ref[...]
    xk = jnp.concatenate([ckv, kpe2], axis=1).astype(BF16)
    k = _dot(xk, wk_ref[...]).astype(BF16)
    v_t = lax.dot_general(wv_ref[...], xk[:, 0:KV_LORA], _NT, preferred_element_type=F32).astype(BF16)
    hw = FNET_W // 2
    f_in = proj[:, 512:1024].astype(BF16)
    z = [_dot(f_in[:, a * hw:(a + 1) * hw], wcs_ref[a * hw:(a + 1) * hw, :]) for a in range(2)]
    zc = jnp.concatenate([za[:, 0:hw] for za in z], axis=1).astype(BF16)
    zs = jnp.concatenate([za[:, hw:2 * hw] for za in z], axis=1).astype(BF16)
    kpe = proj[:, Q_LORA + KV_LORA:Q_LORA + KV_LORA + LANES]
    return q_heads, k, v_t, zc, zs, ckv, kpe


def _premix_kernel(x_ref, m_ref, g1_ref, win_ref, qg_ref, wuq_ref, kvg_ref, wk_ref, wv_ref, wcs_ref,
                   tq_ref, tk_ref, q_ref, k_ref, v_ref, zc_ref, zs_ref, *, rope):
    q_heads, k, v_t, zc, zs, _, _ = _premix_body(x_ref[...], m_ref, g1_ref, win_ref, qg_ref, wuq_ref, kvg_ref,
                                                 wk_ref, wv_ref, wcs_ref, tq_ref, tk_ref, rope)
    for hd, qh in enumerate(q_heads):
        q_ref[hd * HEAD_PAD:(hd + 1) * HEAD_PAD, :] = qh
    k_ref[...] = k
    v_ref[...] = v_t
    zc_ref[...] = zc
    zs_ref[...] = zs


def _premix(x, mods6, g1, win, qg, wuq, kvg, wk, wv, wcs, tq, tk, *, mod_row, tab_row, rope):
    n = x.shape[0]
    tm = PRE_TILE
    full = lambda a: pl.BlockSpec(a.shape, lambda i: (0,) * a.ndim)
    out_shape = [jax.ShapeDtypeStruct((QK_W, n), BF16), jax.ShapeDtypeStruct((n, QK_W), BF16),
                 jax.ShapeDtypeStruct((V_W, n), BF16), jax.ShapeDtypeStruct((n, FNET_W), BF16),
                 jax.ShapeDtypeStruct((n, FNET_W), BF16)]
    out_specs = [pl.BlockSpec((QK_W, tm), lambda i: (0, i)), pl.BlockSpec((tm, QK_W), lambda i: (i, 0)),
                 pl.BlockSpec((V_W, tm), lambda i: (0, i)), pl.BlockSpec((tm, FNET_W), lambda i: (i, 0)),
                 pl.BlockSpec((tm, FNET_W), lambda i: (i, 0))]
    return pl.pallas_call(
        functools.partial(_premix_kernel, rope=rope),
        grid=(n // tm,),
        in_specs=[pl.BlockSpec((tm, D_MODEL), lambda i: (i, 0)),
                  pl.BlockSpec((1, 6, D_MODEL), lambda i: (mod_row(i), 0, 0)),
                  full(g1), full(win), full(qg), full(wuq), full(kvg), full(wk), full(wv), full(wcs),
                  pl.BlockSpec((2 * LANES, tm), lambda i: (0, tab_row(i))),
                  pl.BlockSpec((tm, LANES), lambda i: (tab_row(i), 0))],
        out_specs=out_specs,
        out_shape=out_shape,
        compiler_params=_cparams(("parallel",)),
        name="premix",
    )(x, mods6, g1, win, qg, wuq, kvg, wk, wv, wcs, tq, tk)


def _cachekv_kernel(xk_ref, wk_ref, wv_ref, k_ref, v_ref):
    xk = xk_ref[...]
    k_ref[...] = _dot(xk, wk_ref[...]).astype(BF16)
    v_ref[...] = lax.dot_general(wv_ref[...], xk[:, 0:KV_LORA], _NT, preferred_element_type=F32).astype(BF16)


def _cachekv(xk, wk, wv):
    n = xk.shape[0]
    tm = 512
    full = lambda a: pl.BlockSpec(a.shape, lambda i: (0,) * a.ndim)
    return pl.pallas_call(
        _cachekv_kernel,
        grid=(n // tm,),
        in_specs=[pl.BlockSpec((tm, 2 * LANES), lambda i: (i, 0)), full(wk), full(wv)],
        out_specs=[pl.BlockSpec((tm, QK_W), lambda i: (i, 0)), pl.BlockSpec((V_W, tm), lambda i: (0, i))],
        out_shape=[jax.ShapeDtypeStruct((n, QK_W), BF16), jax.ShapeDtypeStruct((V_W, n), BF16)],
        compiler_params=_cparams(("parallel",)),
        name="cachekv",
    )(xk, wk, wv)


def _attn_body(q_heads, k_refs, v_refs, kc):
    tq = q_heads[0].shape[1]
    zero = jnp.zeros((HEAD_PAD, tq), BF16)
    n_pairs = len(q_heads) // 2
    qbd = [jnp.concatenate([jnp.concatenate([q_heads[2 * pr], zero], axis=1),
                            jnp.concatenate([zero, q_heads[2 * pr + 1]], axis=1)], axis=0) for pr in range(n_pairs)]
    chunks = [(k_ref, v_ref, c0, min(c0 + kc, k_ref.shape[0]))
              for k_ref, v_ref in zip(k_refs, v_refs) for c0 in range(0, k_ref.shape[0], kc)]
    work = [(pr, ch) for pr in range(n_pairs) for ch in chunks]

    def score(item):
        pr, (k_ref, _, c0, c1) = item
        return _dot(k_ref[c0:c1, pr * 2 * HEAD_PAD:(pr + 1) * 2 * HEAD_PAD], qbd[pr]).astype(BF16)

    m = [None] * n_pairs
    o = [None] * n_pairs
    s_next = score(work[0])
    for wi, (pr, (k_ref, v_ref, c0, c1)) in enumerate(work):
        s = s_next
        if wi + 1 < len(work):
            s_next = score(work[wi + 1])
        cm = jnp.max(s, axis=0, keepdims=True)
        vlo = pr * 2 * V_HEAD
        va = jnp.concatenate([v_ref[vlo:vlo + 2 * V_HEAD, c0:c1], jnp.ones((16, c1 - c0), BF16)], axis=0)
        if m[pr] is None:
            m[pr] = cm
            o[pr] = _dot(va, jnp.exp2(s - cm))
        else:
            m_new = jnp.maximum(m[pr], cm)
            alpha = jnp.exp2(m[pr].astype(F32) - m_new.astype(F32))
            o[pr] = alpha * o[pr] + _dot(va, jnp.exp2(s - m_new))
            m[pr] = m_new
    outs = []
    for pr in range(n_pairs):
        on = o[pr][0:2 * V_HEAD, :] * (1.0 / o[pr][2 * V_HEAD:2 * V_HEAD + 1, :])
        ot = jnp.concatenate([on[0:V_HEAD, 0:tq], on[V_HEAD:2 * V_HEAD, tq:2 * tq]], axis=0)
        outs.append(ot.T.astype(BF16))
    return outs


def _attn_kernel(q_ref, *refs, n_kv, n_pairs, kc):
    q_heads = [q_ref[hd * HEAD_PAD:(hd + 1) * HEAD_PAD, :] for hd in range(2 * n_pairs)]
    outs = _attn_body(q_heads, refs[:n_kv], refs[n_kv:2 * n_kv], kc)
    o_ref = refs[2 * n_kv]
    for pr, o in enumerate(outs):
        o_ref[:, pr * 2 * V_HEAD:(pr + 1) * 2 * V_HEAD] = o


def _attention(q_t, ks, vs_t, *, n_req, t_q, kv_lens, tq, pairs_per_step):
    n_kv = len(ks)
    nq = t_q // tq
    pp = pairs_per_step
    in_specs = [pl.BlockSpec((pp * 2 * HEAD_PAD, tq), lambda b, p, i: (p, b * nq + i))]
    in_specs += [pl.BlockSpec((kl, pp * 2 * HEAD_PAD), lambda b, p, i: (b, p)) for kl in kv_lens]
    in_specs += [pl.BlockSpec((pp * 2 * V_HEAD, kl), lambda b, p, i: (p, b)) for kl in kv_lens]
    return pl.pallas_call(
        functools.partial(_attn_kernel, n_kv=n_kv, n_pairs=pp, kc=KEY_CHUNK),
        grid=(n_req, N_HEADS // 2 // pp, nq),
        in_specs=in_specs,
        out_specs=pl.BlockSpec((tq, pp * 2 * V_HEAD), lambda b, p, i: (b * nq + i, p)),
        out_shape=jax.ShapeDtypeStruct((n_req * t_q, V_W), BF16),
        compiler_params=_cparams(("parallel", "parallel", "parallel")),
        name="attn",
    )(q_t, *ks, *vs_t)


def _mixout_body(x, attn_pairs, fm, wo_ref, m_ref, g2_ref, wr_ref):
    y = _dot(fm, wo_ref[V_W:V_W + FNET_W, :])
    col = 0
    for a in attn_pairs:
        y = y + _dot(a, wo_ref[col:col + a.shape[1], :])
        col += a.shape[1]
    gate1 = m_ref[0, 2:3, :]
    shift2 = m_ref[0, 3:4, :]
    scale2 = m_ref[0, 4:5, :]
    x1 = x + gate1 * y
    h2 = (_rms(x1, g2_ref[...] * (1.0 + scale2)) + shift2).astype(BF16)
    lg = lax.dot_general(wr_ref[...], h2, _NT, preferred_element_type=F32)
    e = jnp.exp(lg - jnp.max(lg, axis=0, keepdims=True))
    return x1, h2, e / jnp.sum(e, axis=0, keepdims=True)


def _mixout_kernel(x_ref, a_ref, zc_ref, zs_ref, cb_ref, sb_ref, off_ref, wo_ref, m_ref, g2_ref, wr_ref,
                   x1_ref, h2_ref, aff_ref, zp_ref, zm_ref, il_ref, wob_ref, *, t):
    half = t // 2
    i = pl.program_id(1)

    @pl.when((pl.program_id(0) == 0) & (i == 0))
    def _cast_weight():
        wob_ref[...] = wo_ref[...].astype(BF16)

    @pl.when(i == 0)
    def _fold_halves():
        for src, col in ((zc_ref, 0), (zs_ref, FNET_W)):
            lo = src[0:half, :]
            hi = src[half:t, :]
            zp_ref[:, col:col + FNET_W] = lo + hi
            zm_ref[:, col:col + FNET_W] = lo - hi

    off = off_ref[pl.ds(i, 1), :]
    co = off[:, 0:half]
    so = off[:, half:t]
    cb = cb_ref[...]
    sb = sb_ref[...]
    ct = (cb * co - sb * so).astype(BF16)
    st = (sb * co + cb * so).astype(BF16)
    tr = cb.shape[0]
    h = tr // 2
    even = _dot(ct[0:h, :], zp_ref[:, 0:FNET_W]) - _dot(st[0:h, :], zp_ref[:, FNET_W:2 * FNET_W])
    odd = _dot(ct[h:, :], zm_ref[:, 0:FNET_W]) - _dot(st[h:, :], zm_ref[:, FNET_W:2 * FNET_W])
    for c in range(FNET_W // LANES):
        il_ref[c, pl.ds(0, h, stride=2), :] = even[:, c * LANES:(c + 1) * LANES]
        il_ref[c, pl.ds(1, h, stride=2), :] = odd[:, c * LANES:(c + 1) * LANES]
    fm = jnp.concatenate([il_ref[c] for c in range(FNET_W // LANES)], axis=1).astype(BF16)
    x1, h2, aff = _mixout_body(x_ref[...], [a_ref[...]], fm, wob_ref, m_ref, g2_ref, wr_ref)
    x1_ref[...] = x1
    h2_ref[...] = h2
    aff_ref[...] = aff


def _ctx_front_kernel(x_ref, m_ref, g1_ref, win_ref, qg_ref, wuq_ref, kvg_ref, wk_ref, wv_ref, wcs_ref,
                      tq_ref, tk_ref, ct_ref, st_ref, wo_ref, g2_ref, wr_ref,
                      x1_ref, h2_ref, aff_ref, ckv_ref, kpe_ref, wob_ref, *, kc, t):
    @pl.when(pl.program_id(0) == 0)
    def _cast_weight():
        wob_ref[...] = wo_ref[...].astype(BF16)

    x = x_ref[...]
    q_heads, k, v_t, zc, zs, ckv, kpe = _premix_body(x, m_ref, g1_ref, win_ref, qg_ref, wuq_ref, kvg_ref,
                                                     wk_ref, wv_ref, wcs_ref, tq_ref, tk_ref, False)
    ckv_ref[...] = ckv
    for r in range(x.shape[0] // t):
        kpe_ref[r] = kpe[r * t:(r + 1) * t, :].T[0:QK_ROPE, :]
    attn, fm = [], []
    for r in range(x.shape[0] // t):
        rows = slice(r * t, (r + 1) * t)
        attn.append(jnp.concatenate(_attn_body([q[:, rows] for q in q_heads], [k[rows, :]], [v_t[:, rows]], kc),
                                    axis=1))
        fm.append((_dot(ct_ref[...], zc[rows, :]) - _dot(st_ref[...], zs[rows, :])).astype(BF16))
    x1, h2, aff = _mixout_body(x_ref[...], [jnp.concatenate(attn, axis=0)], jnp.concatenate(fm, axis=0),
                               wob_ref, m_ref, g2_ref, wr_ref)
    x1_ref[...] = x1
    h2_ref[...] = h2
    for r in range(x.shape[0] // t):
        aff_ref[r] = aff[:, r * t:(r + 1) * t]


def _ctx_front(x, mods6, g1, win, qg, wuq, kvg, wk, wv, wcs, tq, tk, ct, st, wo, g2, wr_t, *, n_req, t, mod_row):
    full = lambda a: pl.BlockSpec(a.shape, lambda b: (0,) * a.ndim)
    rps = CTX_REQS
    row = lambda w: pl.BlockSpec((rps * t, w), lambda b: (b, 0))
    return pl.pallas_call(
        functools.partial(_ctx_front_kernel, kc=KEY_CHUNK, t=t),
        grid=(n_req // rps,),
        in_specs=[row(D_MODEL), pl.BlockSpec((1, 6, D_MODEL), lambda b: (mod_row, 0, 0)),
                  full(g1), full(win), full(qg), full(wuq), full(kvg), full(wk), full(wv), full(wcs),
                  full(tq), full(tk), full(ct), full(st), full(wo), full(g2), full(wr_t)],
        out_specs=[row(D_MODEL), row(D_MODEL), pl.BlockSpec((rps, N_EXPERTS, t), lambda b: (b, 0, 0)),
                   row(KV_LORA), pl.BlockSpec((rps, QK_ROPE, t), lambda b: (b, 0, 0))],
        out_shape=[jax.ShapeDtypeStruct((n_req * t, D_MODEL), F32),
                   jax.ShapeDtypeStruct((n_req * t, D_MODEL), BF16),
                   jax.ShapeDtypeStruct((n_req, N_EXPERTS, t), F32),
                   jax.ShapeDtypeStruct((n_req * t, KV_LORA), F32),
                   jax.ShapeDtypeStruct((n_req, QK_ROPE, t), F32)],
        scratch_shapes=[pltpu.VMEM(wo.shape, BF16)],
        compiler_params=_cparams(("arbitrary",)),
        name="ctx_front",
    )(x, mods6, g1, win, qg, wuq, kvg, wk, wv, wcs, tq, tk, ct, st, wo, g2, wr_t)


def _mixout(x, attn, zc, zs, cb, sb, off, wo, mods6, g2, wr_t, *, n_req, t, mod_row):
    tr = MIX_TILE
    nr = t // tr
    full = lambda a: pl.BlockSpec(a.shape, lambda b, i: (0,) * a.ndim)
    return pl.pallas_call(
        functools.partial(_mixout_kernel, t=t),
        grid=(n_req, nr),
        in_specs=[pl.BlockSpec((tr, D_MODEL), lambda b, i: (b * nr + i, 0)),
                  pl.BlockSpec((tr, V_W), lambda b, i: (b * nr + i, 0)),
                  pl.BlockSpec((t, FNET_W), lambda b, i: (b, 0)),
                  pl.BlockSpec((t, FNET_W), lambda b, i: (b, 0)),
                  full(cb), full(sb), full(off),
                  full(wo),
                  pl.BlockSpec((1, 6, D_MODEL), lambda b, i: (mod_row(b), 0, 0)),
                  full(g2), full(wr_t)],
        out_specs=[pl.BlockSpec((tr, D_MODEL), lambda b, i: (b * nr + i, 0)),
                   pl.BlockSpec((tr, D_MODEL), lambda b, i: (b * nr + i, 0)),
                   pl.BlockSpec((None, N_EXPERTS, tr), lambda b, i: (b, 0, i))],
        out_shape=[jax.ShapeDtypeStruct((n_req * t, D_MODEL), F32),
                   jax.ShapeDtypeStruct((n_req * t, D_MODEL), BF16),
                   jax.ShapeDtypeStruct((n_req, N_EXPERTS, t), F32)],
        scratch_shapes=[pltpu.VMEM((t // 2, 2 * FNET_W), BF16), pltpu.VMEM((t // 2, 2 * FNET_W), BF16),
                        pltpu.VMEM((FNET_W // LANES, tr, LANES), F32), pltpu.VMEM(wo.shape, BF16)],
        compiler_params=_cparams(("arbitrary", "arbitrary")),
        name="mixout",
    )(x, attn, zc, zs, cb, sb, off, wo, mods6, g2, wr_t)


def _prefix_count(flags, tri):
    n = flags.shape[1]
    carry = None
    outs = []
    ends = []
    for j in range(n // TOK_TILE):
        c = _dot(flags[:, j * TOK_TILE:(j + 1) * TOK_TILE].astype(BF16), tri)
        if carry is not None:
            c = c + carry
        outs.append(c)
        carry = c[:, TOK_TILE - 1:TOK_TILE]
        ends.append(carry)
    return (outs[0] if len(outs) == 1 else jnp.concatenate(outs, axis=1)), ends


def _route_kernel(aff_ref, pos_ref, off_ref, *, cap):
    a = aff_ref[...]
    rows = a.shape[0]
    capf = jnp.float32(cap)
    thr = jnp.zeros((rows, 1), jnp.int32)
    for bit in range(30, -1, -1):
        cand = thr | jnp.int32(1 << bit)
        cand_f = lax.bitcast_convert_type(cand, F32)
        cnt = jnp.sum(jnp.where(a >= cand_f, 1.0, 0.0), axis=1, keepdims=True)
        thr = jnp.where(cnt >= capf, cand, thr)
    thr_f = lax.bitcast_convert_type(thr, F32)
    above_f = lax.bitcast_convert_type(thr + 1, F32)
    gt = jnp.where(a >= above_f, 1.0, 0.0)
    tie = jnp.where(a >= thr_f, 1.0, 0.0) - gt
    need = capf - jnp.sum(gt, axis=1, keepdims=True)
    r_i = lax.broadcasted_iota(jnp.int32, (TOK_TILE, TOK_TILE), 0)
    c_i = lax.broadcasted_iota(jnp.int32, (TOK_TILE, TOK_TILE), 1)
    tri = jnp.where(r_i <= c_i, 1.0, 0.0).astype(BF16)
    tie_before = _prefix_count(tie, tri)[0] - tie
    sel = gt + tie * jnp.where(tie_before < need, 1.0, 0.0)
    count, ends = _prefix_count(sel, tri)
    pos_ref[...] = jnp.where(sel > 0.5, count - 1.0, -1.0)
    lane = lax.broadcasted_iota(jnp.int32, (rows, LANES), 1)
    offs = jnp.zeros((rows, LANES), F32)
    for j, end in enumerate(ends):
        offs = offs + jnp.where(lane == j + 1, end, 0.0)
    off_ref[...] = offs.astype(jnp.int32)


def _route(aff_t, cap):
    return pl.pallas_call(
        functools.partial(_route_kernel, cap=cap),
        out_shape=[jax.ShapeDtypeStruct(aff_t.shape, F32),
                   jax.ShapeDtypeStruct((aff_t.shape[0], LANES), jnp.int32)],
        compiler_params=pltpu.CompilerParams(vmem_limit_bytes=VMEM_LIMIT),
        name="route",
    )(aff_t)


def _gather_kernel(pos_ref, aff_ref, h_ref, xs_ref, g_ref, *, cap):
    rps, ne, n = pos_ref.shape
    slot = lax.broadcasted_iota(jnp.int32, (ne, cap, n), 1).astype(F32)
    for r in range(rps):
        pos = pos_ref[r]
        aff = aff_ref[r]
        hit = pos[:, None, :] == slot
        onehot = jnp.where(hit, 1.0, 0.0).reshape(ne * cap, n).astype(BF16)
        xs = _dot(onehot, h_ref[r * n:(r + 1) * n, :])
        xs_ref[:, r * cap:(r + 1) * cap, :] = xs.astype(BF16).reshape(ne, cap, D_MODEL)
        g_ref[:, r * cap:(r + 1) * cap, :] = jnp.sum(jnp.where(hit, aff[:, None, :], 0.0), axis=2, keepdims=True)


def _gather(pos_t, aff_t, h2, *, n_req, n, cap, rps):
    pos3 = pos_t.reshape(n_req, N_EXPERTS, n)
    aff3 = aff_t.reshape(n_req, N_EXPERTS, n)
    return pl.pallas_call(
        functools.partial(_gather_kernel, cap=cap),
        grid=(n_req // rps,),
        in_specs=[pl.BlockSpec((rps, N_EXPERTS, n), lambda b: (b, 0, 0)),
                  pl.BlockSpec((rps, N_EXPERTS, n), lambda b: (b, 0, 0)),
                  pl.BlockSpec((rps * n, D_MODEL), lambda b: (b, 0))],
        out_specs=[pl.BlockSpec((N_EXPERTS, rps * cap, D_MODEL), lambda b: (0, b, 0)),
                   pl.BlockSpec((N_EXPERTS, rps * cap, 1), lambda b: (0, b, 0))],
        out_shape=[jax.ShapeDtypeStruct((N_EXPERTS, n_req * cap, D_MODEL), BF16),
                   jax.ShapeDtypeStruct((N_EXPERTS, n_req * cap, 1), F32)],
        compiler_params=_cparams(("parallel",)),
        name="gather",
    )(pos3, aff3, h2)


def _window_plan(off_ref, b, j, cap, w):
    w0 = []
    need = jnp.int32(0)
    for e in range(N_EXPERTS):
        idx = (b * N_EXPERTS + e) * OFF_STRIDE + j
        base = (off_ref[idx] >> 4) << 4
        w0.append(base)
        need = jnp.maximum(need, off_ref[idx + 1] - base)
    return w0, (need + (w - 1)) >> (w.bit_length() - 1)


def _window(w0_e, p, cap, w):
    low = w0_e + p * w
    start = pl.multiple_of(jnp.minimum(low, cap - w), 16)
    return low, start


def _gather_win_kernel(off_ref, pos_ref, aff_ref, h_ref, xs_ref, g_ref, *, cap, w):
    b = pl.program_id(0)
    step = pl.program_id(1)

    @pl.when(step == 0)
    def _init():
        xs_ref[...] = jnp.zeros_like(xs_ref)
        g_ref[...] = jnp.zeros_like(g_ref)

    r = lax.broadcasted_iota(jnp.int32, (w, 1), 0).astype(F32)
    n_sub = pos_ref.shape[1] // TOK_TILE
    for sub in range(n_sub):
        cols = slice(sub * TOK_TILE, (sub + 1) * TOK_TILE)
        pos = pos_ref[:, cols]
        aff = aff_ref[:, cols]
        h = h_ref[cols, :]
        w0, n_pass = _window_plan(off_ref, b, step * n_sub + sub, cap, w)

        def one_pass(p, carry, pos=pos, aff=aff, h=h, w0=w0):
            starts, hots, gates = [], [], []
            for e in range(N_EXPERTS):
                low, start = _window(w0[e], p, cap, w)
                starts.append(start)
                mine = jnp.where(r >= (low - start).astype(F32), 1.0, 0.0)
                hot = jnp.where(pos[e:e + 1, :] - start.astype(F32) == r, mine, 0.0)
                hots.append(hot)
                gates.append(jnp.sum(hot * aff[e:e + 1, :], axis=1, keepdims=True))
            rows = _dot(jnp.concatenate(hots, axis=0).astype(BF16), h).astype(BF16)
            for e in range(N_EXPERTS):
                win = pl.ds(starts[e], w)
                xs_ref[e, win, :] = xs_ref[e, win, :] + rows[e * w:(e + 1) * w, :]
                g_ref[e, win, :] = g_ref[e, win, :] + gates[e]
            return carry

        one_pass(0, 0)
        lax.fori_loop(1, n_pass, one_pass, 0)


def _gather_win(offs, pos_t, aff_t, h2, *, n_req, n, cap):
    tb = WIN_BLOCKS * TOK_TILE
    nb = n // tb
    grid_spec = pltpu.PrefetchScalarGridSpec(
        num_scalar_prefetch=1,
        grid=(n_req, nb),
        in_specs=[pl.BlockSpec((N_EXPERTS, tb), lambda b, j, off: (b, j)),
                  pl.BlockSpec((N_EXPERTS, tb), lambda b, j, off: (b, j)),
                  pl.BlockSpec((tb, D_MODEL), lambda b, j, off: (b * nb + j, 0))],
        out_specs=[pl.BlockSpec((N_EXPERTS, cap, D_MODEL), lambda b, j, off: (0, b, 0)),
                   pl.BlockSpec((N_EXPERTS, cap, 1), lambda b, j, off: (0, b, 0))])
    return pl.pallas_call(
        functools.partial(_gather_win_kernel, cap=cap, w=SLOT_WIN),
        grid_spec=grid_spec,
        out_shape=[jax.ShapeDtypeStruct((N_EXPERTS, n_req * cap, D_MODEL), BF16),
                   jax.ShapeDtypeStruct((N_EXPERTS, n_req * cap, 1), F32)],
        compiler_params=_cparams(("parallel", "arbitrary")),
        name="gather_win",
    )(offs, pos_t, aff_t, h2)


def _ffn_kernel(xc_ref, xl_ref, gc_ref, gl_ref, wg_ref, wu_ref, wd_ref, yc_ref, yl_ref, wgb, wub, wdb):
    wgb[...] = wg_ref[0].astype(BF16)
    wub[...] = wu_ref[0].astype(BF16)
    wdb[...] = wd_ref[0].astype(BF16)
    for x_ref, g_ref, y_ref in ((xc_ref, gc_ref, yc_ref), (xl_ref, gl_ref, yl_ref)):
        for j in range(x_ref.shape[1] // FFN_ROWS):
            rows = slice(j * FFN_ROWS, (j + 1) * FFN_ROWS)
            x = x_ref[0, rows, :]
            gate = _dot(x, wgb[...])
            up = _dot(x, wub[...])
            hid = (gate * jax.nn.sigmoid(gate) * up).astype(BF16)
            ys = _dot(hid, wdb[...]) * g_ref[0, rows, :]
            y_ref[0, rows, :] = ys.astype(BF16)


def _ffn(xc, xl, gc, gl, wg, wu, wd):
    m = xc.shape[1]
    xspec = pl.BlockSpec((1, m, D_MODEL), lambda e: (e, 0, 0))
    gspec = pl.BlockSpec((1, m, 1), lambda e: (e, 0, 0))
    shp = jax.ShapeDtypeStruct((N_EXPERTS, m, D_MODEL), BF16)
    return pl.pallas_call(
        _ffn_kernel,
        grid=(N_EXPERTS,),
        in_specs=[xspec, xspec, gspec, gspec,
                  pl.BlockSpec((1, D_MODEL, D_EXPERT), lambda e: (e, 0, 0)),
                  pl.BlockSpec((1, D_MODEL, D_EXPERT), lambda e: (e, 0, 0)),
                  pl.BlockSpec((1, D_EXPERT, D_MODEL), lambda e: (e, 0, 0))],
        out_specs=[xspec, xspec],
        out_shape=[shp, shp],
        scratch_shapes=[pltpu.VMEM((D_MODEL, D_EXPERT), BF16), pltpu.VMEM((D_MODEL, D_EXPERT), BF16),
                        pltpu.VMEM((D_EXPERT, D_MODEL), BF16)],
        compiler_params=_cparams(("arbitrary",)),
        name="ffn",
    )(xc, xl, gc, gl, wg, wu, wd)


def _combine_kernel(x1_ref, pos_ref, ys_ref, m_ref, fg_ref, o_ref, *, cap):
    rps, n, _ = pos_ref.shape
    w = N_EXPERTS * cap
    e_i = lax.broadcasted_iota(jnp.int32, (N_EXPERTS, w), 0)
    j_i = lax.broadcasted_iota(jnp.int32, (N_EXPERTS, w), 1)
    spread = jnp.where((j_i >> (cap.bit_length() - 1)) == e_i, 1.0, 0.0).astype(BF16)
    lane_slot = (lax.broadcasted_iota(jnp.int32, (1, w), 1) & (cap - 1)).astype(F32)
    gate2 = m_ref[0, 5:6, :]
    for r in range(rps):
        pos = pos_ref[r].astype(BF16)
        onehot = jnp.where(_dot(pos, spread) == lane_slot, 1.0, 0.0).astype(BF16)
        acc = _dot(onehot, ys_ref[:, r * cap:(r + 1) * cap, :].reshape(w, D_MODEL))
        rows = slice(r * n, (r + 1) * n)
        o_ref[rows, :] = _rms(x1_ref[rows, :] + gate2 * acc, fg_ref[...])


def _combine(x1, pos_tok, ys, mods6, fg, *, n_req, n, cap, mod_row, rps):
    return pl.pallas_call(
        functools.partial(_combine_kernel, cap=cap),
        grid=(n_req // rps,),
        in_specs=[pl.BlockSpec((rps * n, D_MODEL), lambda b: (b, 0)),
                  pl.BlockSpec((rps, n, N_EXPERTS), lambda b: (b, 0, 0)),
                  pl.BlockSpec((N_EXPERTS, rps * cap, D_MODEL), lambda b: (0, b, 0)),
                  pl.BlockSpec((1, 6, D_MODEL), lambda b: (mod_row, 0, 0)),
                  pl.BlockSpec((1, D_MODEL), lambda b: (0, 0))],
        out_specs=pl.BlockSpec((rps * n, D_MODEL), lambda b: (b, 0)),
        out_shape=jax.ShapeDtypeStruct((n_req * n, D_MODEL), F32),
        compiler_params=_cparams(("parallel",)),
        name="combine",
    )(x1, pos_tok, ys, mods6, fg)


def _combine_win_kernel(off_ref, x1_ref, pos_ref, ys_ref, m_ref, fg_ref, o_ref, acc_ref, *, cap, w):
    b = pl.program_id(0)
    step = pl.program_id(1)
    width = N_EXPERTS * w
    e_i = lax.broadcasted_iota(jnp.int32, (N_EXPERTS, width), 0)
    j_i = lax.broadcasted_iota(jnp.int32, (N_EXPERTS, width), 1)
    spread = jnp.where((j_i >> (w.bit_length() - 1)) == e_i, 1.0, 0.0).astype(BF16)
    lane_slot = (lax.broadcasted_iota(jnp.int32, (1, width), 1) & (w - 1)).astype(F32)
    lane_e = lax.broadcasted_iota(jnp.int32, (1, N_EXPERTS), 1)
    n_sub = pos_ref.shape[0] // TOK_TILE
    for sub in range(n_sub):
        rows = slice(sub * TOK_TILE, (sub + 1) * TOK_TILE)
        pos = pos_ref[rows, :]
        w0, n_pass = _window_plan(off_ref, b, step * n_sub + sub, cap, w)

        def window_sum(p, pos=pos, w0=w0):
            start_row = jnp.zeros((1, N_EXPERTS), F32)
            first_row = jnp.zeros((1, N_EXPERTS), F32)
            wins = []
            for e in range(N_EXPERTS):
                low, start = _window(w0[e], p, cap, w)
                start_row = jnp.where(lane_e == e, start.astype(F32), start_row)
                first_row = jnp.where(lane_e == e, (low - start).astype(F32), first_row)
                wins.append(ys_ref[e, pl.ds(start, w), :])
            rel = pos - start_row
            rel = jnp.where(rel >= first_row, rel, -1.0).astype(BF16)
            onehot = jnp.where(_dot(rel, spread) == lane_slot, 1.0, 0.0).astype(BF16)
            return _dot(onehot, jnp.concatenate(wins, axis=0))

        def finish(acc, rows=rows):
            o_ref[rows, :] = _rms(x1_ref[rows, :] + m_ref[0, 5:6, :] * acc, fg_ref[...])

        acc0 = window_sum(0)
        acc_ref[sub] = acc0
        finish(acc0)

        @pl.when(n_pass > 1)
        def _more_passes(sub=sub, n_pass=n_pass, window_sum=window_sum, finish=finish):
            def one_pass(p, carry):
                acc_ref[sub] += window_sum(p)
                return carry

            lax.fori_loop(1, n_pass, one_pass, 0)
            finish(acc_ref[sub])


def _combine_win(offs, x1, pos_tok, ys, mods6, fg, *, n_req, n, cap, mod_row):
    tr = WIN_BLOCKS * TOK_TILE
    nr = n // tr
    grid_spec = pltpu.PrefetchScalarGridSpec(
        num_scalar_prefetch=1,
        grid=(n_req, nr),
        in_specs=[pl.BlockSpec((tr, D_MODEL), lambda b, i, off: (b * nr + i, 0)),
                  pl.BlockSpec((None, tr, N_EXPERTS), lambda b, i, off: (b, i, 0)),
                  pl.BlockSpec((N_EXPERTS, cap, D_MODEL), lambda b, i, off: (0, b, 0)),
                  pl.BlockSpec((1, 6, D_MODEL), lambda b, i, off: (mod_row(b), 0, 0)),
                  pl.BlockSpec((1, D_MODEL), lambda b, i, off: (0, 0))],
        out_specs=pl.BlockSpec((tr, D_MODEL), lambda b, i, off: (b * nr + i, 0)),
        scratch_shapes=[pltpu.VMEM((WIN_BLOCKS, TOK_TILE, D_MODEL), F32)])
    return pl.pallas_call(
        functools.partial(_combine_win_kernel, cap=cap, w=SLOT_WIN),
        grid_spec=grid_spec,
        out_shape=jax.ShapeDtypeStruct((n_req * n, D_MODEL), F32),
        compiler_params=_cparams(("parallel", "parallel")),
        name="combine_win",
    )(offs, x1, pos_tok, ys, mods6, fg)


def _rot_half(w):
    half = QK_ROPE // 2
    return jnp.concatenate([-w[..., half:], w[..., :half]], axis=-1)


def _rope_tables(t):
    n_rows = t // GRID_W
    rows = np.repeat(np.arange(n_rows, dtype=np.float64), GRID_W)
    cols = np.tile(np.arange(GRID_W, dtype=np.float64), n_rows)
    n_freq = QK_ROPE // 4
    inv_freq = ROPE_BASE ** (-np.arange(n_freq, dtype=np.float64) / n_freq)
    ang = np.concatenate([rows[:, None] * inv_freq, cols[:, None] * inv_freq], axis=-1)
    cos = np.concatenate([np.cos(ang), np.cos(ang)], axis=-1)
    sin = np.concatenate([np.sin(ang), np.sin(ang)], axis=-1)
    return cos, sin


def _qk_tables(cos, sin):
    t = cos.shape[0]
    scale = (QK_NOPE + QK_ROPE) ** -0.5 * np.log2(np.e)
    pad = np.zeros((t, HEAD_PAD - QK_NOPE - QK_ROPE))
    cosq = np.concatenate([np.full((t, QK_NOPE), scale), cos * scale, pad], axis=1)
    sinq = np.concatenate([np.zeros((t, QK_NOPE)), sin * scale, pad], axis=1)
    tq_t = np.concatenate([cosq, sinq], axis=1).T
    tk = np.concatenate([cos, sin, np.zeros((t, LANES - 2 * QK_ROPE))], axis=1)
    return jnp.asarray(tq_t, F32), jnp.asarray(tk, F32)


def _dft_angles(rows, t):
    k = np.arange(t, dtype=np.int64)
    return ((rows[:, None] * k[None, :]) % t).astype(np.float64) * (2.0 * np.pi / t)


def _dft_tables(t):
    ang = _dft_angles(np.arange(t, dtype=np.int64), t)
    scale = (t * FNET_CH) ** -0.5
    return jnp.asarray(np.cos(ang) * scale, F32).astype(BF16), jnp.asarray(np.sin(ang) * scale, F32).astype(BF16)


def _dft_half_tables(t):
    r = np.arange(MIX_TILE, dtype=np.int64)
    ang = _dft_angles(np.concatenate([r[0::2], r[1::2]]), t)[:, :t // 2]
    scale = (t * FNET_CH) ** -0.5
    ang_off = _dft_angles(np.arange(t // MIX_TILE, dtype=np.int64) * MIX_TILE, t)[:, :t // 2]
    off = np.concatenate([np.cos(ang_off), np.sin(ang_off)], axis=1)
    return jnp.asarray(np.cos(ang) * scale, F32), jnp.asarray(np.sin(ang) * scale, F32), jnp.asarray(off, F32)


def _block_diag(w):
    g, a, b = w.shape
    eye = jnp.eye(g, dtype=w.dtype)
    return (eye[:, None, :, None] * w[:, :, None, :]).reshape(g * a, g * b)


def kernel(x_prompt, x_sample, cache_ckv, cache_kpe, c, c_ctx, w_mod, b_mod, norm1_g, w_in, q_norm_g, w_uq,
           kv_norm_g, w_ukv, w_fmix, w_out, norm2_g, w_router, w_e_gate, w_e_up, w_e_down, final_g):
    assert w_mod.shape[0] == 1, "single-layer problem"
    n_ctx, t_ctx, _ = x_prompt.shape
    n_lat, t_lat, _ = x_sample.shape
    past = cache_ckv.shape[2]
    ctx_row = n_lat

    win = _winprep(jnp.swapaxes(w_in, 1, 2)[0])
    wq3 = w_uq[0].reshape(Q_LORA, N_HEADS, QK_NOPE + QK_ROPE)
    qpad = jnp.zeros((Q_LORA, N_HEADS, HEAD_PAD - QK_NOPE - QK_ROPE), F32)
    wuq_main = jnp.concatenate([wq3, qpad], axis=2).reshape(Q_LORA, QK_W)
    wuq_rot = _rot_half(wq3[..., QK_NOPE:]).reshape(Q_LORA, N_HEADS * QK_ROPE)
    wuq_lat = jnp.concatenate([wuq_main, wuq_rot], axis=1).T.astype(BF16)
    wuq_ctx = wuq_main.T.astype(BF16)
    wkv3 = w_ukv[0].reshape(KV_LORA, N_HEADS, QK_NOPE + V_HEAD)
    wk_top = jnp.concatenate([wkv3[..., :QK_NOPE], jnp.zeros((KV_LORA, N_HEADS, HEAD_PAD - QK_NOPE), F32)],
                             axis=2).reshape(KV_LORA, QK_W)
    place = jnp.concatenate([jnp.zeros((QK_ROPE, QK_NOPE), F32), jnp.eye(QK_ROPE, dtype=F32),
                             jnp.zeros((QK_ROPE, HEAD_PAD - QK_NOPE - QK_ROPE), F32)], axis=1)
    place = jnp.tile(place, (1, N_HEADS))
    wk = jnp.concatenate([wk_top, place, place, jnp.zeros((LANES - 2 * QK_ROPE, QK_W), F32)], axis=0).astype(BF16)
    wv = wkv3[..., QK_NOPE:].reshape(KV_LORA, V_W).T.astype(BF16)
    wo = w_out[0]
    wr_t = w_router[0].T.astype(BF16)

    cos, sin = _rope_tables(t_lat)
    tq_lat, tk_lat = _qk_tables(cos, sin)
    assert n_ctx % CTX_REQS == 0
    tq_ctx, tk_ctx = _qk_tables(np.ones((CTX_REQS * t_ctx, QK_ROPE)), np.zeros((CTX_REQS * t_ctx, QK_ROPE)))
    ch_ang = _dft_angles(np.arange(FNET_CH, dtype=np.int64), FNET_CH)
    dft_ctx = _dft_tables(t_ctx)
    dft_lat = _dft_half_tables(t_lat)

    assert n_lat + 1 <= SUBLANES
    c8 = jnp.concatenate([c, c_ctx[None, :], jnp.zeros((SUBLANES - n_lat - 1, D_MODEL), F32)], axis=0)
    mods6 = _mods(c8, w_mod[0], b_mod[0][None, :]).reshape(SUBLANES, 6, D_MODEL)
    cw, sw = _fold(jnp.asarray(np.cos(ch_ang), F32), jnp.asarray(np.sin(ch_ang), F32), w_fmix[0])
    gh = FNET_GROUPS // 2
    wcs = jnp.concatenate([jnp.concatenate([_block_diag(cw[a * gh:(a + 1) * gh]), _block_diag(sw[a * gh:(a + 1) * gh])],
                                           axis=1) for a in range(2)], axis=0).astype(BF16)

    g1 = norm1_g[0][None, :]
    qg = q_norm_g[0][None, :]
    kvg = kv_norm_g[0][None, :]
    g2 = norm2_g[0][None, :]
    fg = final_g[None, :]

    xp = x_prompt.reshape(n_ctx * t_ctx, D_MODEL)
    xs = x_sample.reshape(n_lat * t_lat, D_MODEL)
    tiles_lat = t_lat // PRE_TILE

    x1c, h2c, affc, ckv_c, kpe_c = _ctx_front(
        xp, mods6, g1, win, qg, wuq_ctx, kvg, wk, wv, wcs, tq_ctx, tk_ctx,
        *dft_ctx, wo, g2, wr_t, n_req=n_ctx, t=t_ctx, mod_row=ctx_row)
    ql, kl, vl, zcl, zsl = _premix(
        xs, mods6, g1, win, qg, wuq_lat, kvg, wk, wv, wcs, tq_lat, tk_lat,
        mod_row=lambda i: i // tiles_lat, tab_row=lambda i: i % tiles_lat, rope=True)
    xk_cache = jnp.concatenate([cache_ckv[:, 0], cache_kpe[:, 0],
                                jnp.zeros((n_lat, past, 2 * LANES - KV_LORA - QK_ROPE), F32)],
                               axis=-1).reshape(n_lat * past, 2 * LANES).astype(BF16)
    kpast, vpast = _cachekv(xk_cache, wk, wv)

    attn_l = _attention(ql, [kpast, kl], [vpast, vl], n_req=n_lat, t_q=t_lat, kv_lens=[past, t_lat], tq=TOK_TILE,
                        pairs_per_step=2)

    x1l, h2l, affl = _mixout(xs, attn_l, zcl, zsl, *dft_lat, wo, mods6, g2, wr_t,
                             n_req=n_lat, t=t_lat, mod_row=lambda b: b)

    cap_c = CAP_FACTOR * t_ctx // N_EXPERTS
    cap_l = CAP_FACTOR * t_lat // N_EXPERTS
    affc2 = affc.reshape(n_ctx * N_EXPERTS, t_ctx)
    affl2 = affl.reshape(n_lat * N_EXPERTS, t_lat)
    posc, _ = _route(affc2, cap_c)
    posl, offl = _route(affl2, cap_l)
    assert t_lat // TOK_TILE + 1 <= OFF_STRIDE and cap_l % SLOT_WIN == 0
    offl = offl[:, :OFF_STRIDE].reshape(-1)
    xsc, gc = _gather(posc, affc2, h2c, n_req=n_ctx, n=t_ctx, cap=cap_c, rps=MOE_REQS)
    xsl, gl = _gather_win(offl, posl, affl2, h2l, n_req=n_lat, n=t_lat, cap=cap_l)
    ysc, ysl = _ffn(xsc, xsl, gc, gl, w_e_gate[0], w_e_up[0], w_e_down[0])

    posc_tok = posc.reshape(n_ctx, N_EXPERTS, t_ctx).transpose(0, 2, 1)
    posl_tok = posl.reshape(n_lat, N_EXPERTS, t_lat).transpose(0, 2, 1)
    y_prompt = _combine(x1c, posc_tok, ysc, mods6, fg, n_req=n_ctx, n=t_ctx, cap=cap_c, mod_row=ctx_row,
                        rps=MOE_REQS)
    y_sample = _combine_win(offl, x1l, posl_tok, ysl, mods6, fg, n_req=n_lat, n=t_lat, cap=cap_l,
                            mod_row=lambda b: b)

    return (y_prompt.reshape(n_ctx, t_ctx, D_MODEL), y_sample.reshape(n_lat, t_lat, D_MODEL),
            ckv_c.reshape(n_ctx, 1, t_ctx, KV_LORA), kpe_c.transpose(0, 2, 1).reshape(n_ctx, 1, t_ctx, QK_ROPE))
```

```python
import functools

import jax
import jax.numpy as jnp
import numpy as np
from jax import lax
from jax.experimental import pallas as pl
from jax.experimental.pallas import tpu as pltpu

F32 = jnp.float32
BF16 = jnp.bfloat16

D_MODEL = 1024
N_HEADS = 8
QK_NOPE = 64
QK_ROPE = 32
V_HEAD = 64
Q_LORA = 256
KV_LORA = 128
FNET_GROUPS = 8
FNET_CH = 64
FNET_W = FNET_GROUPS * FNET_CH
N_EXPERTS = 16
CAP_FACTOR = 2
D_EXPERT = 512
GRID_W = 64
ROPE_BASE = 10000.0
EPS = 1e-6

LANES = 128
SUBLANES = 8
HEAD_PAD = LANES
QK_W = N_HEADS * HEAD_PAD
V_W = N_HEADS * V_HEAD
TOK_TILE = 256
PRE_TILE = 1024
MIX_TILE = 512
FFN_ROWS = 1024
MODS_TILE = 1536
CTX_REQS = 4
MOE_REQS = 4
KEY_CHUNK = 512
SLOT_WIN = 64
WIN_BLOCKS = 4
OFF_STRIDE = 16
VMEM_LIMIT = 48 * 1024 * 1024

_NT = (((1,), (1,)), ((), ()))


def _cparams(sem):
    return pltpu.CompilerParams(dimension_semantics=sem, vmem_limit_bytes=VMEM_LIMIT)


def _rms(x, g):
    return x * lax.rsqrt(jnp.mean(x * x, axis=-1, keepdims=True) + EPS) * g


def _dot(a, b):
    return jnp.dot(a, b, preferred_element_type=F32)


def _mods_kernel(c_ref, w_ref, b_ref, o_ref):
    c = c_ref[...]
    s = c * jax.nn.sigmoid(c)
    o_ref[...] = _dot(s.astype(BF16), w_ref[...].astype(BF16)) + b_ref[...]


def _mods(c8, w_mod, b_mod):
    n = w_mod.shape[1]
    tn = MODS_TILE
    return pl.pallas_call(
        _mods_kernel,
        grid=(n // tn,),
        in_specs=[pl.BlockSpec((SUBLANES, D_MODEL), lambda j: (0, 0)),
                  pl.BlockSpec((D_MODEL, tn), lambda j: (0, j)),
                  pl.BlockSpec((1, tn), lambda j: (0, j))],
        out_specs=pl.BlockSpec((SUBLANES, tn), lambda j: (0, j)),
        out_shape=jax.ShapeDtypeStruct((SUBLANES, n), F32),
        compiler_params=_cparams(("arbitrary",)),
        name="mods",
    )(c8, w_mod, b_mod)


def _fold_kernel(cc_ref, sc_ref, w_ref, cw_ref, sw_ref):
    for g in range(FNET_GROUPS):
        w = w_ref[g]
        cw_ref[g] = jnp.dot(cc_ref[...], w, preferred_element_type=F32, precision=lax.Precision.HIGHEST)
        sw_ref[g] = jnp.dot(sc_ref[...], w, preferred_element_type=F32, precision=lax.Precision.HIGHEST)


def _fold(cc, sc, w_fmix):
    shp = jax.ShapeDtypeStruct((FNET_GROUPS, FNET_CH, FNET_CH), F32)
    return pl.pallas_call(_fold_kernel, out_shape=(shp, shp), name="fold")(cc, sc, w_fmix)


KPE_LO = Q_LORA + KV_LORA
PROJ_W = 1024


def _winprep_kernel(wt_ref, o_ref):
    half = QK_ROPE // 2
    kpe_hi = KPE_LO + QK_ROPE
    o_ref[0:kpe_hi, :] = wt_ref[0:kpe_hi, :].astype(BF16)
    o_ref[kpe_hi:kpe_hi + half, :] = (-wt_ref[KPE_LO + half:kpe_hi, :]).astype(BF16)
    o_ref[kpe_hi + half:kpe_hi + QK_ROPE, :] = wt_ref[KPE_LO:KPE_LO + half, :].astype(BF16)
    o_ref[kpe_hi + QK_ROPE:PROJ_W - FNET_W, :] = jnp.zeros((PROJ_W - FNET_W - kpe_hi - QK_ROPE, D_MODEL), BF16)
    o_ref[PROJ_W - FNET_W:PROJ_W, :] = wt_ref[kpe_hi:kpe_hi + FNET_W, :].astype(BF16)


def _winprep(w_in_t):
    assert w_in_t.shape == (KPE_LO + QK_ROPE + FNET_W, D_MODEL)
    return pl.pallas_call(
        _winprep_kernel,
        out_shape=jax.ShapeDtypeStruct((PROJ_W, D_MODEL), BF16),
        compiler_params=pltpu.CompilerParams(vmem_limit_bytes=VMEM_LIMIT),
        name="winprep",
    )(w_in_t)


def _premix_body(x, m_ref, g1_ref, win_ref, qg_ref, wuq_ref, kvg_ref, wk_ref, wv_ref, wcs_ref, tq_ref, tk_ref, rope):
    shift1 = m_ref[0, 0:1, :]
    scale1 = m_ref[0, 1:2, :]
    h = _rms(x, g1_ref[...] * (1.0 + scale1)) + shift1
    proj = lax.dot_general(h.astype(BF16), win_ref[...], _NT, preferred_element_type=F32)
    qn = _rms(proj[:, 0:Q_LORA], qg_ref[...]).astype(BF16)
    qq = lax.dot_general(wuq_ref[...], qn, _NT, preferred_element_type=F32)
    cosq = tq_ref[0:LANES, :]
    sinq = tq_ref[LANES:2 * LANES, :]
    q_heads = []
    for hd in range(N_HEADS):
        lo = hd * HEAD_PAD
        qh = qq[lo:lo + HEAD_PAD, :] * cosq
        if rope:
            rot = qq[QK_W + hd * QK_ROPE:QK_W + (hd + 1) * QK_ROPE, :] * sinq[QK_NOPE:QK_NOPE + QK_ROPE, :]
            qh = qh + jnp.concatenate([jnp.zeros((QK_NOPE, rot.shape[1]), F32), rot,
                                       jnp.zeros((HEAD_PAD - QK_NOPE - QK_ROPE, rot.shape[1]), F32)], axis=0)
        q_heads.append(qh.astype(BF16))
    ckv = _rms(proj[:, Q_LORA:Q_LORA + KV_LORA], kvg_ref[...])
    kpe2 = proj[:, Q_LORA + KV_LORA:Q_LORA + KV_LORA + LANES] * tk_ref[...]
    xk = jnp.concatenate([ckv, kpe2], axis=1).astype(BF16)
    k = _dot(xk, wk_ref[...]).astype(BF16)
    v_t = lax.dot_general(wv_ref[...], xk[:, 0:KV_LORA], _NT, preferred_element_type=F32).astype(BF16)
    hw = FNET_W // 2
    f_in = proj[:, 512:1024].astype(BF16)
    z = [_dot(f_in[:, a * hw:(a + 1) * hw], wcs_ref[a * hw:(a + 1) * hw, :]) for a in range(2)]
    zc = jnp.concatenate([za[:, 0:hw] for za in z], axis=1).astype(BF16)
    zs = jnp.concatenate([za[:, hw:2 * hw] for za in z], axis=1).astype(BF16)
    kpe = proj[:, Q_LORA + KV_LORA:Q_LORA + KV_LORA + LANES]
    return q_heads, k, v_t, zc, zs, ckv, kpe


def _premix_kernel(x_ref, m_ref, g1_ref, win_ref, qg_ref, wuq_ref, kvg_ref, wk_ref, wv_ref, wcs_ref,
                   tq_ref, tk_ref, q_ref, k_ref, v_ref, zc_ref, zs_ref, *, rope):
    q_heads, k, v_t, zc, zs, _, _ = _premix_body(x_ref[...], m_ref, g1_ref, win_ref, qg_ref, wuq_ref, kvg_ref,
                                                 wk_ref, wv_ref, wcs_ref, tq_ref, tk_ref, rope)
    for hd, qh in enumerate(q_heads):
        q_ref[hd * HEAD_PAD:(hd + 1) * HEAD_PAD, :] = qh
    k_ref[...] = k
    v_ref[...] = v_t
    zc_ref[...] = zc
    zs_ref[...] = zs


def _premix(x, mods6, g1, win, qg, wuq, kvg, wk, wv, wcs, tq, tk, *, mod_row, tab_row, rope):
    n = x.shape[0]
    tm = PRE_TILE
    full = lambda a: pl.BlockSpec(a.shape, lambda i: (0,) * a.ndim)
    out_shape = [jax.ShapeDtypeStruct((QK_W, n), BF16), jax.ShapeDtypeStruct((n, QK_W), BF16),
                 jax.ShapeDtypeStruct((V_W, n), BF16), jax.ShapeDtypeStruct((n, FNET_W), BF16),
                 jax.ShapeDtypeStruct((n, FNET_W), BF16)]
    out_specs = [pl.BlockSpec((QK_W, tm), lambda i: (0, i)), pl.BlockSpec((tm, QK_W), lambda i: (i, 0)),
                 pl.BlockSpec((V_W, tm), lambda i: (0, i)), pl.BlockSpec((tm, FNET_W), lambda i: (i, 0)),
                 pl.BlockSpec((tm, FNET_W), lambda i: (i, 0))]
    return pl.pallas_call(
        functools.partial(_premix_kernel, rope=rope),
        grid=(n // tm,),
        in_specs=[pl.BlockSpec((tm, D_MODEL), lambda i: (i, 0)),
                  pl.BlockSpec((1, 6, D_MODEL), lambda i: (mod_row(i), 0, 0)),
                  full(g1), full(win), full(qg), full(wuq), full(kvg), full(wk), full(wv), full(wcs),
                  pl.BlockSpec((2 * LANES, tm), lambda i: (0, tab_row(i))),
                  pl.BlockSpec((tm, LANES), lambda i: (tab_row(i), 0))],
        out_specs=out_specs,
        out_shape=out_shape,
        compiler_params=_cparams(("parallel",)),
        name="premix",
    )(x, mods6, g1, win, qg, wuq, kvg, wk, wv, wcs, tq, tk)


def _cachekv_kernel(xk_ref, wk_ref, wv_ref, k_ref, v_ref):
    xk = xk_ref[...]
    k_ref[...] = _dot(xk, wk_ref[...]).astype(BF16)
    v_ref[...] = lax.dot_general(wv_ref[...], xk[:, 0:KV_LORA], _NT, preferred_element_type=F32).astype(BF16)


def _cachekv(xk, wk, wv):
    n = xk.shape[0]
    tm = 512
    full = lambda a: pl.BlockSpec(a.shape, lambda i: (0,) * a.ndim)
    return pl.pallas_call(
        _cachekv_kernel,
        grid=(n // tm,),
        in_specs=[pl.BlockSpec((tm, 2 * LANES), lambda i: (i, 0)), full(wk), full(wv)],
        out_specs=[pl.BlockSpec((tm, QK_W), lambda i: (i, 0)), pl.BlockSpec((V_W, tm), lambda i: (0, i))],
        out_shape=[jax.ShapeDtypeStruct((n, QK_W), BF16), jax.ShapeDtypeStruct((V_W, n), BF16)],
        compiler_params=_cparams(("parallel",)),
        name="cachekv",
    )(xk, wk, wv)


def _attn_body(q_heads, k_refs, v_refs, kc):
    tq = q_heads[0].shape[1]
    zero = jnp.zeros((HEAD_PAD, tq), BF16)
    n_pairs = len(q_heads) // 2
    qbd = [jnp.concatenate([jnp.concatenate([q_heads[2 * pr], zero], axis=1),
                            jnp.concatenate([zero, q_heads[2 * pr + 1]], axis=1)], axis=0) for pr in range(n_pairs)]
    chunks = [(k_ref, v_ref, c0, min(c0 + kc, k_ref.shape[0]))
              for k_ref, v_ref in zip(k_refs, v_refs) for c0 in range(0, k_ref.shape[0], kc)]
    work = [(pr, ch) for pr in range(n_pairs) for ch in chunks]

    def score(item):
        pr, (k_ref, _, c0, c1) = item
        return _dot(k_ref[c0:c1, pr * 2 * HEAD_PAD:(pr + 1) * 2 * HEAD_PAD], qbd[pr]).astype(BF16)

    m = [None] * n_pairs
    o = [None] * n_pairs
    s_next = score(work[0])
    for wi, (pr, (k_ref, v_ref, c0, c1)) in enumerate(work):
        s = s_next
        if wi + 1 < len(work):
            s_next = score(work[wi + 1])
        cm = jnp.max(s, axis=0, keepdims=True)
        vlo = pr * 2 * V_HEAD
        va = jnp.concatenate([v_ref[vlo:vlo + 2 * V_HEAD, c0:c1], jnp.ones((16, c1 - c0), BF16)], axis=0)
        if m[pr] is None:
            m[pr] = cm
            o[pr] = _dot(va, jnp.exp2(s - cm))
        else:
            m_new = jnp.maximum(m[pr], cm)
            alpha = jnp.exp2(m[pr].astype(F32) - m_new.astype(F32))
            o[pr] = alpha * o[pr] + _dot(va, jnp.exp2(s - m_new))
            m[pr] = m_new
    outs = []
    for pr in range(n_pairs):
        on = o[pr][0:2 * V_HEAD, :] * (1.0 / o[pr][2 * V_HEAD:2 * V_HEAD + 1, :])
        ot = jnp.concatenate([on[0:V_HEAD, 0:tq], on[V_HEAD:2 * V_HEAD, tq:2 * tq]], axis=0)
        outs.append(ot.T.astype(BF16))
    return outs


def _attn_kernel(q_ref, *refs, n_kv, n_pairs, kc):
    q_heads = [q_ref[hd * HEAD_PAD:(hd + 1) * HEAD_PAD, :] for hd in range(2 * n_pairs)]
    outs = _attn_body(q_heads, refs[:n_kv], refs[n_kv:2 * n_kv], kc)
    o_ref = refs[2 * n_kv]
    for pr, o in enumerate(outs):
        o_ref[:, pr * 2 * V_HEAD:(pr + 1) * 2 * V_HEAD] = o


def _attention(q_t, ks, vs_t, *, n_req, t_q, kv_lens, tq, pairs_per_step):
    n_kv = len(ks)
    nq = t_q // tq
    pp = pairs_per_step
    in_specs = [pl.BlockSpec((pp * 2 * HEAD_PAD, tq), lambda b, p, i: (p, b * nq + i))]
    in_specs += [pl.BlockSpec((kl, pp * 2 * HEAD_PAD), lambda b, p, i: (b, p)) for kl in kv_lens]
    in_specs += [pl.BlockSpec((pp * 2 * V_HEAD, kl), lambda b, p, i: (p, b)) for kl in kv_lens]
    return pl.pallas_call(
        functools.partial(_attn_kernel, n_kv=n_kv, n_pairs=pp, kc=KEY_CHUNK),
        grid=(n_req, N_HEADS // 2 // pp, nq),
        in_specs=in_specs,
        out_specs=pl.BlockSpec((tq, pp * 2 * V_HEAD), lambda b, p, i: (b * nq + i, p)),
        out_shape=jax.ShapeDtypeStruct((n_req * t_q, V_W), BF16),
        compiler_params=_cparams(("parallel", "parallel", "parallel")),
        name="attn",
    )(q_t, *ks, *vs_t)


def _mixout_body(x, attn_pairs, fm, wo_ref, m_ref, g2_ref, wr_ref):
    y = _dot(fm, wo_ref[V_W:V_W + FNET_W, :])
    col = 0
    for a in attn_pairs:
        y = y + _dot(a, wo_ref[col:col + a.shape[1], :])
        col += a.shape[1]
    gate1 = m_ref[0, 2:3, :]
    shift2 = m_ref[0, 3:4, :]
    scale2 = m_ref[0, 4:5, :]
    x1 = x + gate1 * y
    h2 = (_rms(x1, g2_ref[...] * (1.0 + scale2)) + shift2).astype(BF16)
    lg = lax.dot_general(wr_ref[...], h2, _NT, preferred_element_type=F32)
    e = jnp.exp(lg - jnp.max(lg, axis=0, keepdims=True))
    return x1, h2, e / jnp.sum(e, axis=0, keepdims=True)


def _mixout_kernel(x_ref, a_ref, zc_ref, zs_ref, cb_ref, sb_ref, off_ref, wo_ref, m_ref, g2_ref, wr_ref,
                   x1_ref, h2_ref, aff_ref, zp_ref, zm_ref, il_ref, *, t):
    half = t // 2
    i = pl.program_id(1)

    @pl.when(i == 0)
    def _fold_halves():
        for src, col in ((zc_ref, 0), (zs_ref, FNET_W)):
            lo = src[0:half, :]
            hi = src[half:t, :]
            zp_ref[:, col:col + FNET_W] = lo + hi
            zm_ref[:, col:col + FNET_W] = lo - hi

    off = off_ref[pl.ds(i, 1), :]
    co = off[:, 0:half]
    so = off[:, half:t]
    cb = cb_ref[...]
    sb = sb_ref[...]
    ct = (cb * co - sb * so).astype(BF16)
    st = (sb * co + cb * so).astype(BF16)
    tr = cb.shape[0]
    h = tr // 2
    even = _dot(ct[0:h, :], zp_ref[:, 0:FNET_W]) - _dot(st[0:h, :], zp_ref[:, FNET_W:2 * FNET_W])
    odd = _dot(ct[h:, :], zm_ref[:, 0:FNET_W]) - _dot(st[h:, :], zm_ref[:, FNET_W:2 * FNET_W])
    for c in range(FNET_W // LANES):
        il_ref[c, pl.ds(0, h, stride=2), :] = even[:, c * LANES:(c + 1) * LANES]
        il_ref[c, pl.ds(1, h, stride=2), :] = odd[:, c * LANES:(c + 1) * LANES]
    fm = jnp.concatenate([il_ref[c] for c in range(FNET_W // LANES)], axis=1).astype(BF16)
    x1, h2, aff = _mixout_body(x_ref[...], [a_ref[...]], fm, wo_ref, m_ref, g2_ref, wr_ref)
    x1_ref[...] = x1.astype(BF16)
    h2_ref[...] = h2
    aff_ref[...] = aff


def _ctx_front_kernel(x_ref, m_ref, g1_ref, win_ref, qg_ref, wuq_ref, kvg_ref, wk_ref, wv_ref, wcs_ref,
                      tq_ref, tk_ref, ct_ref, st_ref, wo_ref, g2_ref, wr_ref,
                      x1_ref, h2_ref, aff_ref, ckv_ref, kpe_ref, *, kc, t):
    x = x_ref[...]
    q_heads, k, v_t, zc, zs, ckv, kpe = _premix_body(x, m_ref, g1_ref, win_ref, qg_ref, wuq_ref, kvg_ref,
                                                     wk_ref, wv_ref, wcs_ref, tq_ref, tk_ref, False)
    ckv_ref[...] = ckv
    for r in range(x.shape[0] // t):
        kpe_ref[r] = kpe[r * t:(r + 1) * t, :].T[0:QK_ROPE, :]
    attn, fm = [], []
    for r in range(x.shape[0] // t):
        rows = slice(r * t, (r + 1) * t)
        attn.append(jnp.concatenate(_attn_body([q[:, rows] for q in q_heads], [k[rows, :]], [v_t[:, rows]], kc),
                                    axis=1))
        fm.append((_dot(ct_ref[...], zc[rows, :]) - _dot(st_ref[...], zs[rows, :])).astype(BF16))
    x1, h2, aff = _mixout_body(x_ref[...], [jnp.concatenate(attn, axis=0)], jnp.concatenate(fm, axis=0),
                               wo_ref, m_ref, g2_ref, wr_ref)
    x1_ref[...] = x1.astype(BF16)
    h2_ref[...] = h2
    for r in range(x.shape[0] // t):
        aff_ref[r] = aff[:, r * t:(r + 1) * t]


def _ctx_front(x, mods6, g1, win, qg, wuq, kvg, wk, wv, wcs, tq, tk, ct, st, wo, g2, wr_t, *, n_req, t, mod_row):
    full = lambda a: pl.BlockSpec(a.shape, lambda b: (0,) * a.ndim)
    rps = CTX_REQS
    row = lambda w: pl.BlockSpec((rps * t, w), lambda b: (b, 0))
    return pl.pallas_call(
        functools.partial(_ctx_front_kernel, kc=KEY_CHUNK, t=t),
        grid=(n_req // rps,),
        in_specs=[row(D_MODEL), pl.BlockSpec((1, 6, D_MODEL), lambda b: (mod_row, 0, 0)),
                  full(g1), full(win), full(qg), full(wuq), full(kvg), full(wk), full(wv), full(wcs),
                  full(tq), full(tk), full(ct), full(st), full(wo), full(g2), full(wr_t)],
        out_specs=[row(D_MODEL), row(D_MODEL), pl.BlockSpec((rps, N_EXPERTS, t), lambda b: (b, 0, 0)),
                   row(KV_LORA), pl.BlockSpec((rps, QK_ROPE, t), lambda b: (b, 0, 0))],
        out_shape=[jax.ShapeDtypeStruct((n_req * t, D_MODEL), BF16),
                   jax.ShapeDtypeStruct((n_req * t, D_MODEL), BF16),
                   jax.ShapeDtypeStruct((n_req, N_EXPERTS, t), F32),
                   jax.ShapeDtypeStruct((n_req * t, KV_LORA), F32),
                   jax.ShapeDtypeStruct((n_req, QK_ROPE, t), F32)],
        compiler_params=_cparams(("parallel",)),
        name="ctx_front",
    )(x, mods6, g1, win, qg, wuq, kvg, wk, wv, wcs, tq, tk, ct, st, wo, g2, wr_t)


def _mixout(x, attn, zc, zs, cb, sb, off, wo, mods6, g2, wr_t, *, n_req, t, mod_row):
    tr = MIX_TILE
    nr = t // tr
    full = lambda a: pl.BlockSpec(a.shape, lambda b, i: (0,) * a.ndim)
    return pl.pallas_call(
        functools.partial(_mixout_kernel, t=t),
        grid=(n_req, nr),
        in_specs=[pl.BlockSpec((tr, D_MODEL), lambda b, i: (b * nr + i, 0)),
                  pl.BlockSpec((tr, V_W), lambda b, i: (b * nr + i, 0)),
                  pl.BlockSpec((t, FNET_W), lambda b, i: (b, 0)),
                  pl.BlockSpec((t, FNET_W), lambda b, i: (b, 0)),
                  full(cb), full(sb), full(off),
                  full(wo),
                  pl.BlockSpec((1, 6, D_MODEL), lambda b, i: (mod_row(b), 0, 0)),
                  full(g2), full(wr_t)],
        out_specs=[pl.BlockSpec((tr, D_MODEL), lambda b, i: (b * nr + i, 0)),
                   pl.BlockSpec((tr, D_MODEL), lambda b, i: (b * nr + i, 0)),
                   pl.BlockSpec((None, N_EXPERTS, tr), lambda b, i: (b, 0, i))],
        out_shape=[jax.ShapeDtypeStruct((n_req * t, D_MODEL), BF16),
                   jax.ShapeDtypeStruct((n_req * t, D_MODEL), BF16),
                   jax.ShapeDtypeStruct((n_req, N_EXPERTS, t), F32)],
        scratch_shapes=[pltpu.VMEM((t // 2, 2 * FNET_W), BF16), pltpu.VMEM((t // 2, 2 * FNET_W), BF16),
                        pltpu.VMEM((FNET_W // LANES, tr, LANES), F32)],
        compiler_params=_cparams(("parallel", "arbitrary")),
        name="mixout",
    )(x, attn, zc, zs, cb, sb, off, wo, mods6, g2, wr_t)


def _prefix_count(flags, tri):
    n = flags.shape[1]
    carry = None
    outs = []
    ends = []
    for j in range(n // TOK_TILE):
        c = _dot(flags[:, j * TOK_TILE:(j + 1) * TOK_TILE].astype(BF16), tri)
        if carry is not None:
            c = c + carry
        outs.append(c)
        carry = c[:, TOK_TILE - 1:TOK_TILE]
        ends.append(carry)
    return (outs[0] if len(outs) == 1 else jnp.concatenate(outs, axis=1)), ends


def _route_kernel(aff_ref, pos_ref, off_ref, *, cap):
    a = aff_ref[...]
    rows = a.shape[0]
    capf = jnp.float32(cap)
    thr = jnp.zeros((rows, 1), jnp.int32)
    for bit in range(30, -1, -1):
        cand = thr | jnp.int32(1 << bit)
        cand_f = lax.bitcast_convert_type(cand, F32)
        cnt = jnp.sum(jnp.where(a >= cand_f, 1.0, 0.0), axis=1, keepdims=True)
        thr = jnp.where(cnt >= capf, cand, thr)
    thr_f = lax.bitcast_convert_type(thr, F32)
    above_f = lax.bitcast_convert_type(thr + 1, F32)
    gt = jnp.where(a >= above_f, 1.0, 0.0)
    tie = jnp.where(a >= thr_f, 1.0, 0.0) - gt
    need = capf - jnp.sum(gt, axis=1, keepdims=True)
    r_i = lax.broadcasted_iota(jnp.int32, (TOK_TILE, TOK_TILE), 0)
    c_i = lax.broadcasted_iota(jnp.int32, (TOK_TILE, TOK_TILE), 1)
    tri = jnp.where(r_i <= c_i, 1.0, 0.0).astype(BF16)
    tie_before = _prefix_count(tie, tri)[0] - tie
    sel = gt + tie * jnp.where(tie_before < need, 1.0, 0.0)
    count, ends = _prefix_count(sel, tri)
    pos_ref[...] = jnp.where(sel > 0.5, count - 1.0, -1.0)
    lane = lax.broadcasted_iota(jnp.int32, (rows, LANES), 1)
    offs = jnp.zeros((rows, LANES), F32)
    for j, end in enumerate(ends):
        offs = offs + jnp.where(lane == j + 1, end, 0.0)
    off_ref[...] = offs.astype(jnp.int32)


def _route(aff_t, cap):
    return pl.pallas_call(
        functools.partial(_route_kernel, cap=cap),
        out_shape=[jax.ShapeDtypeStruct(aff_t.shape, F32),
                   jax.ShapeDtypeStruct((aff_t.shape[0], LANES), jnp.int32)],
        compiler_params=pltpu.CompilerParams(vmem_limit_bytes=VMEM_LIMIT),
        name="route",
    )(aff_t)


def _gather_kernel(pos_ref, aff_ref, h_ref, xs_ref, g_ref, *, cap):
    rps, ne, n = pos_ref.shape
    slot = lax.broadcasted_iota(jnp.int32, (ne, cap, n), 1).astype(F32)
    for r in range(rps):
        pos = pos_ref[r]
        aff = aff_ref[r]
        hit = pos[:, None, :] == slot
        onehot = jnp.where(hit, 1.0, 0.0).reshape(ne * cap, n).astype(BF16)
        xs = _dot(onehot, h_ref[r * n:(r + 1) * n, :])
        xs_ref[:, r * cap:(r + 1) * cap, :] = xs.astype(BF16).reshape(ne, cap, D_MODEL)
        g_ref[:, r * cap:(r + 1) * cap, :] = jnp.sum(jnp.where(hit, aff[:, None, :], 0.0), axis=2, keepdims=True)


def _gather(pos_t, aff_t, h2, *, n_req, n, cap, rps):
    pos3 = pos_t.reshape(n_req, N_EXPERTS, n)
    aff3 = aff_t.reshape(n_req, N_EXPERTS, n)
    return pl.pallas_call(
        functools.partial(_gather_kernel, cap=cap),
        grid=(n_req // rps,),
        in_specs=[pl.BlockSpec((rps, N_EXPERTS, n), lambda b: (b, 0, 0)),
                  pl.BlockSpec((rps, N_EXPERTS, n), lambda b: (b, 0, 0)),
                  pl.BlockSpec((rps * n, D_MODEL), lambda b: (b, 0))],
        out_specs=[pl.BlockSpec((N_EXPERTS, rps * cap, D_MODEL), lambda b: (0, b, 0)),
                   pl.BlockSpec((N_EXPERTS, rps * cap, 1), lambda b: (0, b, 0))],
        out_shape=[jax.ShapeDtypeStruct((N_EXPERTS, n_req * cap, D_MODEL), BF16),
                   jax.ShapeDtypeStruct((N_EXPERTS, n_req * cap, 1), F32)],
        compiler_params=_cparams(("parallel",)),
        name="gather",
    )(pos3, aff3, h2)


def _window_plan(off_ref, b, j, cap, w):
    w0 = []
    need = jnp.int32(0)
    for e in range(N_EXPERTS):
        idx = (b * N_EXPERTS + e) * OFF_STRIDE + j
        base = (off_ref[idx] >> 4) << 4
        w0.append(base)
        need = jnp.maximum(need, off_ref[idx + 1] - base)
    return w0, (need + (w - 1)) >> (w.bit_length() - 1)


def _window(w0_e, p, cap, w):
    low = w0_e + p * w
    start = pl.multiple_of(jnp.minimum(low, cap - w), 16)
    return low, start


def _gather_win_kernel(off_ref, pos_ref, aff_ref, h_ref, xs_ref, g_ref, *, cap, w):
    b = pl.program_id(0)
    step = pl.program_id(1)

    @pl.when(step == 0)
    def _init():
        xs_ref[...] = jnp.zeros_like(xs_ref)
        g_ref[...] = jnp.zeros_like(g_ref)

    r = lax.broadcasted_iota(jnp.int32, (w, 1), 0).astype(F32)
    n_sub = pos_ref.shape[1] // TOK_TILE
    for sub in range(n_sub):
        cols = slice(sub * TOK_TILE, (sub + 1) * TOK_TILE)
        pos = pos_ref[:, cols]
        aff = aff_ref[:, cols]
        h = h_ref[cols, :]
        w0, n_pass = _window_plan(off_ref, b, step * n_sub + sub, cap, w)

        def one_pass(p, carry, pos=pos, aff=aff, h=h, w0=w0):
            starts, hots, gates = [], [], []
            for e in range(N_EXPERTS):
                low, start = _window(w0[e], p, cap, w)
                starts.append(start)
                mine = jnp.where(r >= (low - start).astype(F32), 1.0, 0.0)
                hot = jnp.where(pos[e:e + 1, :] - start.astype(F32) == r, mine, 0.0)
                hots.append(hot)
                gates.append(jnp.sum(hot * aff[e:e + 1, :], axis=1, keepdims=True))
            rows = _dot(jnp.concatenate(hots, axis=0).astype(BF16), h).astype(BF16)
            for e in range(N_EXPERTS):
                win = pl.ds(starts[e], w)
                xs_ref[e, win, :] = xs_ref[e, win, :] + rows[e * w:(e + 1) * w, :]
                g_ref[e, win, :] = g_ref[e, win, :] + gates[e]
            return carry

        one_pass(0, 0)
        lax.fori_loop(1, n_pass, one_pass, 0)


def _gather_win(offs, pos_t, aff_t, h2, *, n_req, n, cap):
    tb = WIN_BLOCKS * TOK_TILE
    nb = n // tb
    grid_spec = pltpu.PrefetchScalarGridSpec(
        num_scalar_prefetch=1,
        grid=(n_req, nb),
        in_specs=[pl.BlockSpec((N_EXPERTS, tb), lambda b, j, off: (b, j)),
                  pl.BlockSpec((N_EXPERTS, tb), lambda b, j, off: (b, j)),
                  pl.BlockSpec((tb, D_MODEL), lambda b, j, off: (b * nb + j, 0))],
        out_specs=[pl.BlockSpec((N_EXPERTS, cap, D_MODEL), lambda b, j, off: (0, b, 0)),
                   pl.BlockSpec((N_EXPERTS, cap, 1), lambda b, j, off: (0, b, 0))])
    return pl.pallas_call(
        functools.partial(_gather_win_kernel, cap=cap, w=SLOT_WIN),
        grid_spec=grid_spec,
        out_shape=[jax.ShapeDtypeStruct((N_EXPERTS, n_req * cap, D_MODEL), BF16),
                   jax.ShapeDtypeStruct((N_EXPERTS, n_req * cap, 1), F32)],
        compiler_params=_cparams(("parallel", "arbitrary")),
        name="gather_win",
    )(offs, pos_t, aff_t, h2)


def _ffn_kernel(xc_ref, xl_ref, gc_ref, gl_ref, wg_ref, wu_ref, wd_ref, yc_ref, yl_ref, wgb, wub, wdb):
    wgb[...] = wg_ref[0].astype(BF16)
    wub[...] = wu_ref[0].astype(BF16)
    wdb[...] = wd_ref[0].astype(BF16)
    for x_ref, g_ref, y_ref in ((xc_ref, gc_ref, yc_ref), (xl_ref, gl_ref, yl_ref)):
        for j in range(x_ref.shape[1] // FFN_ROWS):
            rows = slice(j * FFN_ROWS, (j + 1) * FFN_ROWS)
            x = x_ref[0, rows, :]
            gate = _dot(x, wgb[...])
            up = _dot(x, wub[...])
            hid = (gate * jax.nn.sigmoid(gate) * up).astype(BF16)
            ys = _dot(hid, wdb[...]) * g_ref[0, rows, :]
            y_ref[0, rows, :] = ys.astype(BF16)


def _ffn(xc, xl, gc, gl, wg, wu, wd):
    m = xc.shape[1]
    xspec = pl.BlockSpec((1, m, D_MODEL), lambda e: (e, 0, 0))
    gspec = pl.BlockSpec((1, m, 1), lambda e: (e, 0, 0))
    shp = jax.ShapeDtypeStruct((N_EXPERTS, m, D_MODEL), BF16)
    return pl.pallas_call(
        _ffn_kernel,
        grid=(N_EXPERTS,),
        in_specs=[xspec, xspec, gspec, gspec,
                  pl.BlockSpec((1, D_MODEL, D_EXPERT), lambda e: (e, 0, 0)),
                  pl.BlockSpec((1, D_MODEL, D_EXPERT), lambda e: (e, 0, 0)),
                  pl.BlockSpec((1, D_EXPERT, D_MODEL), lambda e: (e, 0, 0))],
        out_specs=[xspec, xspec],
        out_shape=[shp, shp],
        scratch_shapes=[pltpu.VMEM((D_MODEL, D_EXPERT), BF16), pltpu.VMEM((D_MODEL, D_EXPERT), BF16),
                        pltpu.VMEM((D_EXPERT, D_MODEL), BF16)],
        compiler_params=_cparams(("arbitrary",)),
        name="ffn",
    )(xc, xl, gc, gl, wg, wu, wd)


def _combine_kernel(x1_ref, pos_ref, ys_ref, m_ref, fg_ref, o_ref, *, cap):
    rps, n, _ = pos_ref.shape
    w = N_EXPERTS * cap
    e_i = lax.broadcasted_iota(jnp.int32, (N_EXPERTS, w), 0)
    j_i = lax.broadcasted_iota(jnp.int32, (N_EXPERTS, w), 1)
    spread = jnp.where((j_i >> (cap.bit_length() - 1)) == e_i, 1.0, 0.0).astype(BF16)
    lane_slot = (lax.broadcasted_iota(jnp.int32, (1, w), 1) & (cap - 1)).astype(F32)
    gate2 = m_ref[0, 5:6, :]
    for r in range(rps):
        pos = pos_ref[r].astype(BF16)
        onehot = jnp.where(_dot(pos, spread) == lane_slot, 1.0, 0.0).astype(BF16)
        acc = _dot(onehot, ys_ref[:, r * cap:(r + 1) * cap, :].reshape(w, D_MODEL))
        rows = slice(r * n, (r + 1) * n)
        o_ref[rows, :] = _rms(x1_ref[rows, :].astype(F32) + gate2 * acc, fg_ref[...])


def _combine(x1, pos_tok, ys, mods6, fg, *, n_req, n, cap, mod_row, rps):
    return pl.pallas_call(
        functools.partial(_combine_kernel, cap=cap),
        grid=(n_req // rps,),
        in_specs=[pl.BlockSpec((rps * n, D_MODEL), lambda b: (b, 0)),
                  pl.BlockSpec((rps, n, N_EXPERTS), lambda b: (b, 0, 0)),
                  pl.BlockSpec((N_EXPERTS, rps * cap, D_MODEL), lambda b: (0, b, 0)),
                  pl.BlockSpec((1, 6, D_MODEL), lambda b: (mod_row, 0, 0)),
                  pl.BlockSpec((1, D_MODEL), lambda b: (0, 0))],
        out_specs=pl.BlockSpec((rps * n, D_MODEL), lambda b: (b, 0)),
        out_shape=jax.ShapeDtypeStruct((n_req * n, D_MODEL), F32),
        compiler_params=_cparams(("parallel",)),
        name="combine",
    )(x1, pos_tok, ys, mods6, fg)


def _combine_win_kernel(off_ref, x1_ref, pos_ref, ys_ref, m_ref, fg_ref, o_ref, acc_ref, *, cap, w):
    b = pl.program_id(0)
    step = pl.program_id(1)
    width = N_EXPERTS * w
    e_i = lax.broadcasted_iota(jnp.int32, (N_EXPERTS, width), 0)
    j_i = lax.broadcasted_iota(jnp.int32, (N_EXPERTS, width), 1)
    spread = jnp.where((j_i >> (w.bit_length() - 1)) == e_i, 1.0, 0.0).astype(BF16)
    lane_slot = (lax.broadcasted_iota(jnp.int32, (1, width), 1) & (w - 1)).astype(F32)
    lane_e = lax.broadcasted_iota(jnp.int32, (1, N_EXPERTS), 1)
    n_sub = pos_ref.shape[0] // TOK_TILE
    for sub in range(n_sub):
        rows = slice(sub * TOK_TILE, (sub + 1) * TOK_TILE)
        pos = pos_ref[rows, :]
        w0, n_pass = _window_plan(off_ref, b, step * n_sub + sub, cap, w)

        def window_sum(p, pos=pos, w0=w0):
            start_row = jnp.zeros((1, N_EXPERTS), F32)
            first_row = jnp.zeros((1, N_EXPERTS), F32)
            wins = []
            for e in range(N_EXPERTS):
                low, start = _window(w0[e], p, cap, w)
                start_row = jnp.where(lane_e == e, start.astype(F32), start_row)
                first_row = jnp.where(lane_e == e, (low - start).astype(F32), first_row)
                wins.append(ys_ref[e, pl.ds(start, w), :])
            rel = pos - start_row
            rel = jnp.where(rel >= first_row, rel, -1.0).astype(BF16)
            onehot = jnp.where(_dot(rel, spread) == lane_slot, 1.0, 0.0).astype(BF16)
            return _dot(onehot, jnp.concatenate(wins, axis=0))

        def finish(acc, rows=rows):
            o_ref[rows, :] = _rms(x1_ref[rows, :].astype(F32) + m_ref[0, 5:6, :] * acc, fg_ref[...])

        acc0 = window_sum(0)
        acc_ref[sub] = acc0
        finish(acc0)

        @pl.when(n_pass > 1)
        def _more_passes(sub=sub, n_pass=n_pass, window_sum=window_sum, finish=finish):
            def one_pass(p, carry):
                acc_ref[sub] += window_sum(p)
                return carry

            lax.fori_loop(1, n_pass, one_pass, 0)
            finish(acc_ref[sub])


def _combine_win(offs, x1, pos_tok, ys, mods6, fg, *, n_req, n, cap, mod_row):
    tr = WIN_BLOCKS * TOK_TILE
    nr = n // tr
    grid_spec = pltpu.PrefetchScalarGridSpec(
        num_scalar_prefetch=1,
        grid=(n_req, nr),
        in_specs=[pl.BlockSpec((tr, D_MODEL), lambda b, i, off: (b * nr + i, 0)),
                  pl.BlockSpec((None, tr, N_EXPERTS), lambda b, i, off: (b, i, 0)),
                  pl.BlockSpec((N_EXPERTS, cap, D_MODEL), lambda b, i, off: (0, b, 0)),
                  pl.BlockSpec((1, 6, D_MODEL), lambda b, i, off: (mod_row(b), 0, 0)),
                  pl.BlockSpec((1, D_MODEL), lambda b, i, off: (0, 0))],
        out_specs=pl.BlockSpec((tr, D_MODEL), lambda b, i, off: (b * nr + i, 0)),
        scratch_shapes=[pltpu.VMEM((WIN_BLOCKS, TOK_TILE, D_MODEL), F32)])
    return pl.pallas_call(
        functools.partial(_combine_win_kernel, cap=cap, w=SLOT_WIN),
        grid_spec=grid_spec,
        out_shape=jax.ShapeDtypeStruct((n_req * n, D_MODEL), F32),
        compiler_params=_cparams(("parallel", "parallel")),
        name="combine_win",
    )(offs, x1, pos_tok, ys, mods6, fg)


def _rot_half(w):
    half = QK_ROPE // 2
    return jnp.concatenate([-w[..., half:], w[..., :half]], axis=-1)


def _rope_tables(t):
    n_rows = t // GRID_W
    rows = np.repeat(np.arange(n_rows, dtype=np.float64), GRID_W)
    cols = np.tile(np.arange(GRID_W, dtype=np.float64), n_rows)
    n_freq = QK_ROPE // 4
    inv_freq = ROPE_BASE ** (-np.arange(n_freq, dtype=np.float64) / n_freq)
    ang = np.concatenate([rows[:, None] * inv_freq, cols[:, None] * inv_freq], axis=-1)
    cos = np.concatenate([np.cos(ang), np.cos(ang)], axis=-1)
    sin = np.concatenate([np.sin(ang), np.sin(ang)], axis=-1)
    return cos, sin


def _qk_tables(cos, sin):
    t = cos.shape[0]
    scale = (QK_NOPE + QK_ROPE) ** -0.5 * np.log2(np.e)
    pad = np.zeros((t, HEAD_PAD - QK_NOPE - QK_ROPE))
    cosq = np.concatenate([np.full((t, QK_NOPE), scale), cos * scale, pad], axis=1)
    sinq = np.concatenate([np.zeros((t, QK_NOPE)), sin * scale, pad], axis=1)
    tq_t = np.concatenate([cosq, sinq], axis=1).T
    tk = np.concatenate([cos, sin, np.zeros((t, LANES - 2 * QK_ROPE))], axis=1)
    return jnp.asarray(tq_t, F32), jnp.asarray(tk, F32)


def _dft_angles(rows, t):
    k = np.arange(t, dtype=np.int64)
    return ((rows[:, None] * k[None, :]) % t).astype(np.float64) * (2.0 * np.pi / t)


def _dft_tables(t):
    ang = _dft_angles(np.arange(t, dtype=np.int64), t)
    scale = (t * FNET_CH) ** -0.5
    return jnp.asarray(np.cos(ang) * scale, F32).astype(BF16), jnp.asarray(np.sin(ang) * scale, F32).astype(BF16)


def _dft_half_tables(t):
    r = np.arange(MIX_TILE, dtype=np.int64)
    ang = _dft_angles(np.concatenate([r[0::2], r[1::2]]), t)[:, :t // 2]
    scale = (t * FNET_CH) ** -0.5
    ang_off = _dft_angles(np.arange(t // MIX_TILE, dtype=np.int64) * MIX_TILE, t)[:, :t // 2]
    off = np.concatenate([np.cos(ang_off), np.sin(ang_off)], axis=1)
    return jnp.asarray(np.cos(ang) * scale, F32), jnp.asarray(np.sin(ang) * scale, F32), jnp.asarray(off, F32)


def _block_diag(w):
    g, a, b = w.shape
    eye = jnp.eye(g, dtype=w.dtype)
    return (eye[:, None, :, None] * w[:, :, None, :]).reshape(g * a, g * b)


def kernel(x_prompt, x_sample, cache_ckv, cache_kpe, c, c_ctx, w_mod, b_mod, norm1_g, w_in, q_norm_g, w_uq,
           kv_norm_g, w_ukv, w_fmix, w_out, norm2_g, w_router, w_e_gate, w_e_up, w_e_down, final_g):
    assert w_mod.shape[0] == 1, "single-layer problem"
    n_ctx, t_ctx, _ = x_prompt.shape
    n_lat, t_lat, _ = x_sample.shape
    past = cache_ckv.shape[2]
    ctx_row = n_lat

    win = _winprep(jnp.swapaxes(w_in, 1, 2)[0])
    wq3 = w_uq[0].reshape(Q_LORA, N_HEADS, QK_NOPE + QK_ROPE)
    qpad = jnp.zeros((Q_LORA, N_HEADS, HEAD_PAD - QK_NOPE - QK_ROPE), F32)
    wuq_main = jnp.concatenate([wq3, qpad], axis=2).reshape(Q_LORA, QK_W)
    wuq_rot = _rot_half(wq3[..., QK_NOPE:]).reshape(Q_LORA, N_HEADS * QK_ROPE)
    wuq_lat = jnp.concatenate([wuq_main, wuq_rot], axis=1).T.astype(BF16)
    wuq_ctx = wuq_main.T.astype(BF16)
    wkv3 = w_ukv[0].reshape(KV_LORA, N_HEADS, QK_NOPE + V_HEAD)
    wk_top = jnp.concatenate([wkv3[..., :QK_NOPE], jnp.zeros((KV_LORA, N_HEADS, HEAD_PAD - QK_NOPE), F32)],
                             axis=2).reshape(KV_LORA, QK_W)
    place = jnp.concatenate([jnp.zeros((QK_ROPE, QK_NOPE), F32), jnp.eye(QK_ROPE, dtype=F32),
                             jnp.zeros((QK_ROPE, HEAD_PAD - QK_NOPE - QK_ROPE), F32)], axis=1)
    place = jnp.tile(place, (1, N_HEADS))
    wk = jnp.concatenate([wk_top, place, place, jnp.zeros((LANES - 2 * QK_ROPE, QK_W), F32)], axis=0).astype(BF16)
    wv = wkv3[..., QK_NOPE:].reshape(KV_LORA, V_W).T.astype(BF16)
    wo = w_out[0].astype(BF16)
    wr_t = w_router[0].T.astype(BF16)

    cos, sin = _rope_tables(t_lat)
    tq_lat, tk_lat = _qk_tables(cos, sin)
    assert n_ctx % CTX_REQS == 0
    tq_ctx, tk_ctx = _qk_tables(np.ones((CTX_REQS * t_ctx, QK_ROPE)), np.zeros((CTX_REQS * t_ctx, QK_ROPE)))
    ch_ang = _dft_angles(np.arange(FNET_CH, dtype=np.int64), FNET_CH)
    dft_ctx = _dft_tables(t_ctx)
    dft_lat = _dft_half_tables(t_lat)

    assert n_lat + 1 <= SUBLANES
    c8 = jnp.concatenate([c, c_ctx[None, :], jnp.zeros((SUBLANES - n_lat - 1, D_MODEL), F32)], axis=0)
    mods6 = _mods(c8, w_mod[0], b_mod[0][None, :]).reshape(SUBLANES, 6, D_MODEL)
    cw, sw = _fold(jnp.asarray(np.cos(ch_ang), F32), jnp.asarray(np.sin(ch_ang), F32), w_fmix[0])
    gh = FNET_GROUPS // 2
    wcs = jnp.concatenate([jnp.concatenate([_block_diag(cw[a * gh:(a + 1) * gh]), _block_diag(sw[a * gh:(a + 1) * gh])],
                                           axis=1) for a in range(2)], axis=0).astype(BF16)

    g1 = norm1_g[0][None, :]
    qg = q_norm_g[0][None, :]
    kvg = kv_norm_g[0][None, :]
    g2 = norm2_g[0][None, :]
    fg = final_g[None, :]

    xp = x_prompt.reshape(n_ctx * t_ctx, D_MODEL)
    xs = x_sample.reshape(n_lat * t_lat, D_MODEL)
    tiles_lat = t_lat // PRE_TILE

    x1c, h2c, affc, ckv_c, kpe_c = _ctx_front(
        xp, mods6, g1, win, qg, wuq_ctx, kvg, wk, wv, wcs, tq_ctx, tk_ctx,
        *dft_ctx, wo, g2, wr_t, n_req=n_ctx, t=t_ctx, mod_row=ctx_row)
    ql, kl, vl, zcl, zsl = _premix(
        xs, mods6, g1, win, qg, wuq_lat, kvg, wk, wv, wcs, tq_lat, tk_lat,
        mod_row=lambda i: i // tiles_lat, tab_row=lambda i: i % tiles_lat, rope=True)
    xk_cache = jnp.concatenate([cache_ckv[:, 0], cache_kpe[:, 0],
                                jnp.zeros((n_lat, past, 2 * LANES - KV_LORA - QK_ROPE), F32)],
                               axis=-1).reshape(n_lat * past, 2 * LANES).astype(BF16)
    kpast, vpast = _cachekv(xk_cache, wk, wv)

    attn_l = _attention(ql, [kpast, kl], [vpast, vl], n_req=n_lat, t_q=t_lat, kv_lens=[past, t_lat], tq=TOK_TILE,
                        pairs_per_step=2)

    x1l, h2l, affl = _mixout(xs, attn_l, zcl, zsl, *dft_lat, wo, mods6, g2, wr_t,
                             n_req=n_lat, t=t_lat, mod_row=lambda b: b)

    cap_c = CAP_FACTOR * t_ctx // N_EXPERTS
    cap_l = CAP_FACTOR * t_lat // N_EXPERTS
    affc2 = affc.reshape(n_ctx * N_EXPERTS, t_ctx)
    affl2 = affl.reshape(n_lat * N_EXPERTS, t_lat)
    posc, _ = _route(affc2, cap_c)
    posl, offl = _route(affl2, cap_l)
    assert t_lat // TOK_TILE + 1 <= OFF_STRIDE and cap_l % SLOT_WIN == 0
    offl = offl[:, :OFF_STRIDE].reshape(-1)
    xsc, gc = _gather(posc, affc2, h2c, n_req=n_ctx, n=t_ctx, cap=cap_c, rps=MOE_REQS)
    xsl, gl = _gather_win(offl, posl, affl2, h2l, n_req=n_lat, n=t_lat, cap=cap_l)
    ysc, ysl = _ffn(xsc, xsl, gc, gl, w_e_gate[0], w_e_up[0], w_e_down[0])

    posc_tok = posc.reshape(n_ctx, N_EXPERTS, t_ctx).transpose(0, 2, 1)
    posl_tok = posl.reshape(n_lat, N_EXPERTS, t_lat).transpose(0, 2, 1)
    y_prompt = _combine(x1c, posc_tok, ysc, mods6, fg, n_req=n_ctx, n=t_ctx, cap=cap_c, mod_row=ctx_row,
                        rps=MOE_REQS)
    y_sample = _combine_win(offl, x1l, posl_tok, ysl, mods6, fg, n_req=n_lat, n=t_lat, cap=cap_l,
                            mod_row=lambda b: b)

    return (y_prompt.reshape(n_ctx, t_ctx, D_MODEL), y_sample.reshape(n_lat, t_lat, D_MODEL),
            ckv_c.reshape(n_ctx, 1, t_ctx, KV_LORA), kpe_c.transpose(0, 2, 1).reshape(n_ctx, 1, t_ctx, QK_ROPE))
```

```python
import functools

import jax
import jax.numpy as jnp
import numpy as np
from jax import lax
from jax.experimental import pallas as pl
from jax.experimental.pallas import tpu as pltpu

F32 = jnp.float32
BF16 = jnp.bfloat16

D_MODEL = 1024
N_HEADS = 8
QK_NOPE = 64
QK_ROPE = 32
V_HEAD = 64
Q_LORA = 256
KV_LORA = 128
FNET_GROUPS = 8
FNET_CH = 64
FNET_W = FNET_GROUPS * FNET_CH
N_EXPERTS = 16
CAP_FACTOR = 2
D_EXPERT = 512
GRID_W = 64
ROPE_BASE = 10000.0
EPS = 1e-6

LANES = 128
SUBLANES = 8
HEAD_PAD = LANES
QK_W = N_HEADS * HEAD_PAD
V_W = N_HEADS * V_HEAD
TOK_TILE = 256
PRE_TILE = 1024
MIX_TILE = 512
FFN_ROWS = 1024
MODS_TILE = 1536
CTX_REQS = 4
MOE_REQS = 4
KEY_CHUNK = 512
SLOT_WIN = 64
WIN_BLOCKS = 4
OFF_STRIDE = 16
VMEM_LIMIT = 48 * 1024 * 1024

_NT = (((1,), (1,)), ((), ()))


def _cparams(sem):
    return pltpu.CompilerParams(dimension_semantics=sem, vmem_limit_bytes=VMEM_LIMIT)


def _rms(x, g):
    return x * lax.rsqrt(jnp.mean(x * x, axis=-1, keepdims=True) + EPS) * g


def _dot(a, b):
    return jnp.dot(a, b, preferred_element_type=F32)


def _mods_kernel(c_ref, w_ref, b_ref, o_ref):
    c = c_ref[...]
    s = c * jax.nn.sigmoid(c)
    o_ref[...] = _dot(s.astype(BF16), w_ref[...].astype(BF16)) + b_ref[...]


def _mods(c8, w_mod, b_mod):
    n = w_mod.shape[1]
    tn = MODS_TILE
    return pl.pallas_call(
        _mods_kernel,
        grid=(n // tn,),
        in_specs=[pl.BlockSpec((SUBLANES, D_MODEL), lambda j: (0, 0)),
                  pl.BlockSpec((D_MODEL, tn), lambda j: (0, j)),
                  pl.BlockSpec((1, tn), lambda j: (0, j))],
        out_specs=pl.BlockSpec((SUBLANES, tn), lambda j: (0, j)),
        out_shape=jax.ShapeDtypeStruct((SUBLANES, n), F32),
        compiler_params=_cparams(("arbitrary",)),
        name="mods",
    )(c8, w_mod, b_mod)


def _fold_kernel(cc_ref, sc_ref, w_ref, place_ref, o_ref):
    hi = lax.Precision.HIGHEST
    gh = FNET_GROUPS // 2
    for g in range(FNET_GROUPS):
        w = w_ref[g]
        place = place_ref[g % gh]
        cw = jnp.dot(jnp.dot(cc_ref[...], w, preferred_element_type=F32, precision=hi), place,
                     preferred_element_type=F32, precision=hi)
        sw = jnp.dot(jnp.dot(sc_ref[...], w, preferred_element_type=F32, precision=hi), place,
                     preferred_element_type=F32, precision=hi)
        o_ref[g * FNET_CH:(g + 1) * FNET_CH, :] = jnp.concatenate([cw, sw], axis=1).astype(BF16)


def _fold(cc, sc, w_fmix):
    gh = FNET_GROUPS // 2
    place = np.zeros((gh, FNET_CH, gh * FNET_CH), np.float32)
    for g in range(gh):
        place[g, np.arange(FNET_CH), g * FNET_CH + np.arange(FNET_CH)] = 1.0
    return pl.pallas_call(_fold_kernel, out_shape=jax.ShapeDtypeStruct((FNET_W, 2 * gh * FNET_CH), BF16),
                          name="fold")(cc, sc, w_fmix, jnp.asarray(place))


KPE_LO = Q_LORA + KV_LORA
PROJ_W = 1024


def _winprep_kernel(wt_ref, o_ref):
    half = QK_ROPE // 2
    kpe_hi = KPE_LO + QK_ROPE
    o_ref[0:kpe_hi, :] = wt_ref[0:kpe_hi, :].astype(BF16)
    o_ref[kpe_hi:kpe_hi + half, :] = (-wt_ref[KPE_LO + half:kpe_hi, :]).astype(BF16)
    o_ref[kpe_hi + half:kpe_hi + QK_ROPE, :] = wt_ref[KPE_LO:KPE_LO + half, :].astype(BF16)
    o_ref[kpe_hi + QK_ROPE:PROJ_W - FNET_W, :] = jnp.zeros((PROJ_W - FNET_W - kpe_hi - QK_ROPE, D_MODEL), BF16)
    o_ref[PROJ_W - FNET_W:PROJ_W, :] = wt_ref[kpe_hi:kpe_hi + FNET_W, :].astype(BF16)


def _winprep(w_in_t):
    assert w_in_t.shape == (KPE_LO + QK_ROPE + FNET_W, D_MODEL)
    return pl.pallas_call(
        _winprep_kernel,
        out_shape=jax.ShapeDtypeStruct((PROJ_W, D_MODEL), BF16),
        compiler_params=pltpu.CompilerParams(vmem_limit_bytes=VMEM_LIMIT),
        name="winprep",
    )(w_in_t)


def _premix_body(x, m_ref, g1_ref, win_ref, qg_ref, wuq_ref, kvg_ref, wk_ref, wv_ref, wcs_ref, tq_ref, tk_ref, rope):
    shift1 = m_ref[0, 0:1, :]
    scale1 = m_ref[0, 1:2, :]
    h = _rms(x, g1_ref[...] * (1.0 + scale1)) + shift1
    proj = lax.dot_general(h.astype(BF16), win_ref[...], _NT, preferred_element_type=F32)
    qn = _rms(proj[:, 0:Q_LORA], qg_ref[...]).astype(BF16)
    qq = lax.dot_general(wuq_ref[...], qn, _NT, preferred_element_type=F32)
    cosq = tq_ref[0:LANES, :]
    sinq = tq_ref[LANES:2 * LANES, :]
    q_heads = []
    for hd in range(N_HEADS):
        lo = hd * HEAD_PAD
        qh = qq[lo:lo + HEAD_PAD, :] * cosq
        if rope:
            rot = qq[QK_W + hd * QK_ROPE:QK_W + (hd + 1) * QK_ROPE, :] * sinq[QK_NOPE:QK_NOPE + QK_ROPE, :]
            qh = qh + jnp.concatenate([jnp.zeros((QK_NOPE, rot.shape[1]), F32), rot,
                                       jnp.zeros((HEAD_PAD - QK_NOPE - QK_ROPE, rot.shape[1]), F32)], axis=0)
        q_heads.append(qh.astype(BF16))
    ckv = _rms(proj[:, Q_LORA:Q_LORA + KV_LORA], kvg_ref[...])
    kpe2 = proj[:, Q_LORA + KV_LORA:Q_LORA + KV_LORA + LANES] * tk_ref[...]
    xk = jnp.concatenate([ckv, kpe2], axis=1).astype(BF16)
    k = _dot(xk, wk_ref[...]).astype(BF16)
    v_t = lax.dot_general(wv_ref[...], xk[:, 0:KV_LORA], _NT, preferred_element_type=F32).astype(BF16)
    hw = FNET_W // 2
    f_in = proj[:, 512:1024].astype(BF16)
    z = [_dot(f_in[:, a * hw:(a + 1) * hw], wcs_ref[a * hw:(a + 1) * hw, :]) for a in range(2)]
    zc = jnp.concatenate([za[:, 0:hw] for za in z], axis=1).astype(BF16)
    zs = jnp.concatenate([za[:, hw:2 * hw] for za in z], axis=1).astype(BF16)
    kpe = proj[:, Q_LORA + KV_LORA:Q_LORA + KV_LORA + LANES]
    return q_heads, k, v_t, zc, zs, ckv, kpe


def _premix_kernel(x_ref, m_ref, g1_ref, win_ref, qg_ref, wuq_ref, kvg_ref, wk_ref, wv_ref, wcs_ref,
                   tq_ref, tk_ref, q_ref, k_ref, v_ref, zc_ref, zs_ref, *, rope):
    q_heads, k, v_t, zc, zs, _, _ = _premix_body(x_ref[...], m_ref, g1_ref, win_ref, qg_ref, wuq_ref, kvg_ref,
                                                 wk_ref, wv_ref, wcs_ref, tq_ref, tk_ref, rope)
    for hd, qh in enumerate(q_heads):
        q_ref[hd * HEAD_PAD:(hd + 1) * HEAD_PAD, :] = qh
    k_ref[...] = k
    v_ref[...] = v_t
    zc_ref[...] = zc
    zs_ref[...] = zs


def _premix(x, mods6, g1, win, qg, wuq, kvg, wk, wv, wcs, tq, tk, *, mod_row, tab_row, rope):
    n = x.shape[0]
    tm = PRE_TILE
    full = lambda a: pl.BlockSpec(a.shape, lambda i: (0,) * a.ndim)
    out_shape = [jax.ShapeDtypeStruct((QK_W, n), BF16), jax.ShapeDtypeStruct((n, QK_W), BF16),
                 jax.ShapeDtypeStruct((V_W, n), BF16), jax.ShapeDtypeStruct((n, FNET_W), BF16),
                 jax.ShapeDtypeStruct((n, FNET_W), BF16)]
    out_specs = [pl.BlockSpec((QK_W, tm), lambda i: (0, i)), pl.BlockSpec((tm, QK_W), lambda i: (i, 0)),
                 pl.BlockSpec((V_W, tm), lambda i: (0, i)), pl.BlockSpec((tm, FNET_W), lambda i: (i, 0)),
                 pl.BlockSpec((tm, FNET_W), lambda i: (i, 0))]
    return pl.pallas_call(
        functools.partial(_premix_kernel, rope=rope),
        grid=(n // tm,),
        in_specs=[pl.BlockSpec((tm, D_MODEL), lambda i: (i, 0)),
                  pl.BlockSpec((1, 6, D_MODEL), lambda i: (mod_row(i), 0, 0)),
                  full(g1), full(win), full(qg), full(wuq), full(kvg), full(wk), full(wv), full(wcs),
                  pl.BlockSpec((2 * LANES, tm), lambda i: (0, tab_row(i))),
                  pl.BlockSpec((tm, LANES), lambda i: (tab_row(i), 0))],
        out_specs=out_specs,
        out_shape=out_shape,
        compiler_params=_cparams(("parallel",)),
        name="premix",
    )(x, mods6, g1, win, qg, wuq, kvg, wk, wv, wcs, tq, tk)


def _cachekv_kernel(xk_ref, wk_ref, wv_ref, k_ref, v_ref):
    xk = xk_ref[...]
    k_ref[...] = _dot(xk, wk_ref[...]).astype(BF16)
    v_ref[...] = lax.dot_general(wv_ref[...], xk[:, 0:KV_LORA], _NT, preferred_element_type=F32).astype(BF16)


def _cachekv(xk, wk, wv):
    n = xk.shape[0]
    tm = 512
    full = lambda a: pl.BlockSpec(a.shape, lambda i: (0,) * a.ndim)
    return pl.pallas_call(
        _cachekv_kernel,
        grid=(n // tm,),
        in_specs=[pl.BlockSpec((tm, 2 * LANES), lambda i: (i, 0)), full(wk), full(wv)],
        out_specs=[pl.BlockSpec((tm, QK_W), lambda i: (i, 0)), pl.BlockSpec((V_W, tm), lambda i: (0, i))],
        out_shape=[jax.ShapeDtypeStruct((n, QK_W), BF16), jax.ShapeDtypeStruct((V_W, n), BF16)],
        compiler_params=_cparams(("parallel",)),
        name="cachekv",
    )(xk, wk, wv)


def _attn_body(q_heads, k_refs, v_refs, kc):
    tq = q_heads[0].shape[1]
    zero = jnp.zeros((HEAD_PAD, tq), BF16)
    n_pairs = len(q_heads) // 2
    qbd = [jnp.concatenate([jnp.concatenate([q_heads[2 * pr], zero], axis=1),
                            jnp.concatenate([zero, q_heads[2 * pr + 1]], axis=1)], axis=0) for pr in range(n_pairs)]
    chunks = [(k_ref, v_ref, c0, min(c0 + kc, k_ref.shape[0]))
              for k_ref, v_ref in zip(k_refs, v_refs) for c0 in range(0, k_ref.shape[0], kc)]
    work = [(pr, ch) for pr in range(n_pairs) for ch in chunks]

    def score(item):
        pr, (k_ref, _, c0, c1) = item
        return _dot(k_ref[c0:c1, pr * 2 * HEAD_PAD:(pr + 1) * 2 * HEAD_PAD], qbd[pr]).astype(BF16)

    m = [None] * n_pairs
    o = [None] * n_pairs
    s_next = score(work[0])
    for wi, (pr, (k_ref, v_ref, c0, c1)) in enumerate(work):
        s = s_next
        if wi + 1 < len(work):
            s_next = score(work[wi + 1])
        cm = jnp.max(s, axis=0, keepdims=True)
        vlo = pr * 2 * V_HEAD
        va = jnp.concatenate([v_ref[vlo:vlo + 2 * V_HEAD, c0:c1], jnp.ones((16, c1 - c0), BF16)], axis=0)
        if m[pr] is None:
            m[pr] = cm
            o[pr] = _dot(va, jnp.exp2(s - cm))
        else:
            m_new = jnp.maximum(m[pr], cm)
            alpha = jnp.exp2(m[pr].astype(F32) - m_new.astype(F32))
            o[pr] = alpha * o[pr] + _dot(va, jnp.exp2(s - m_new))
            m[pr] = m_new
    outs = []
    for pr in range(n_pairs):
        on = o[pr][0:2 * V_HEAD, :] * (1.0 / o[pr][2 * V_HEAD:2 * V_HEAD + 1, :])
        ot = jnp.concatenate([on[0:V_HEAD, 0:tq], on[V_HEAD:2 * V_HEAD, tq:2 * tq]], axis=0)
        outs.append(ot.T.astype(BF16))
    return outs


def _attn_kernel(q_ref, *refs, n_kv, n_pairs, kc):
    q_heads = [q_ref[hd * HEAD_PAD:(hd + 1) * HEAD_PAD, :] for hd in range(2 * n_pairs)]
    outs = _attn_body(q_heads, refs[:n_kv], refs[n_kv:2 * n_kv], kc)
    o_ref = refs[2 * n_kv]
    for pr, o in enumerate(outs):
        o_ref[:, pr * 2 * V_HEAD:(pr + 1) * 2 * V_HEAD] = o


def _attention(q_t, ks, vs_t, *, n_req, t_q, kv_lens, tq, pairs_per_step):
    n_kv = len(ks)
    nq = t_q // tq
    pp = pairs_per_step
    in_specs = [pl.BlockSpec((pp * 2 * HEAD_PAD, tq), lambda b, p, i: (p, b * nq + i))]
    in_specs += [pl.BlockSpec((kl, pp * 2 * HEAD_PAD), lambda b, p, i: (b, p)) for kl in kv_lens]
    in_specs += [pl.BlockSpec((pp * 2 * V_HEAD, kl), lambda b, p, i: (p, b)) for kl in kv_lens]
    return pl.pallas_call(
        functools.partial(_attn_kernel, n_kv=n_kv, n_pairs=pp, kc=KEY_CHUNK),
        grid=(n_req, N_HEADS // 2 // pp, nq),
        in_specs=in_specs,
        out_specs=pl.BlockSpec((tq, pp * 2 * V_HEAD), lambda b, p, i: (b * nq + i, p)),
        out_shape=jax.ShapeDtypeStruct((n_req * t_q, V_W), BF16),
        compiler_params=_cparams(("parallel", "parallel", "parallel")),
        name="attn",
    )(q_t, *ks, *vs_t)


def _mixout_body(x, attn_pairs, fm, wo_ref, m_ref, g2_ref, wr_ref):
    y = _dot(fm, wo_ref[V_W:V_W + FNET_W, :])
    col = 0
    for a in attn_pairs:
        y = y + _dot(a, wo_ref[col:col + a.shape[1], :])
        col += a.shape[1]
    gate1 = m_ref[0, 2:3, :]
    shift2 = m_ref[0, 3:4, :]
    scale2 = m_ref[0, 4:5, :]
    x1 = x + gate1 * y
    h2 = (_rms(x1, g2_ref[...] * (1.0 + scale2)) + shift2).astype(BF16)
    lg = lax.dot_general(wr_ref[...], h2, _NT, preferred_element_type=F32)
    e = jnp.exp(lg - jnp.max(lg, axis=0, keepdims=True))
    return x1, h2, e / jnp.sum(e, axis=0, keepdims=True)


def _mixout_kernel(x_ref, a_ref, zc_ref, zs_ref, cb_ref, sb_ref, off_ref, wo_ref, m_ref, g2_ref, wr_ref,
                   x1_ref, h2_ref, aff_ref, zp_ref, zm_ref, il_ref, *, t):
    half = t // 2
    i = pl.program_id(1)

    @pl.when(i == 0)
    def _fold_halves():
        for src, col in ((zc_ref, 0), (zs_ref, FNET_W)):
            lo = src[0:half, :]
            hi = src[half:t, :]
            zp_ref[:, col:col + FNET_W] = lo + hi
            zm_ref[:, col:col + FNET_W] = lo - hi

    off = off_ref[pl.ds(i, 1), :]
    co = off[:, 0:half]
    so = off[:, half:t]
    cb = cb_ref[...]
    sb = sb_ref[...]
    ct = (cb * co - sb * so).astype(BF16)
    st = (sb * co + cb * so).astype(BF16)
    tr = cb.shape[0]
    h = tr // 2
    even = _dot(ct[0:h, :], zp_ref[:, 0:FNET_W]) - _dot(st[0:h, :], zp_ref[:, FNET_W:2 * FNET_W])
    odd = _dot(ct[h:, :], zm_ref[:, 0:FNET_W]) - _dot(st[h:, :], zm_ref[:, FNET_W:2 * FNET_W])
    for c in range(FNET_W // LANES):
        il_ref[c, pl.ds(0, h, stride=2), :] = even[:, c * LANES:(c + 1) * LANES]
        il_ref[c, pl.ds(1, h, stride=2), :] = odd[:, c * LANES:(c + 1) * LANES]
    fm = jnp.concatenate([il_ref[c] for c in range(FNET_W // LANES)], axis=1).astype(BF16)
    x1, h2, aff = _mixout_body(x_ref[...], [a_ref[...]], fm, wo_ref, m_ref, g2_ref, wr_ref)
    x1_ref[...] = x1.astype(BF16)
    h2_ref[...] = h2
    aff_ref[...] = aff


def _ctx_front_kernel(x_ref, m_ref, g1_ref, win_ref, qg_ref, wuq_ref, kvg_ref, wk_ref, wv_ref, wcs_ref,
                      tq_ref, tk_ref, ct_ref, st_ref, wo_ref, g2_ref, wr_ref,
                      x1_ref, h2_ref, aff_ref, ckv_ref, kpe_ref, *, kc, t):
    x = x_ref[...]
    q_heads, k, v_t, zc, zs, ckv, kpe = _premix_body(x, m_ref, g1_ref, win_ref, qg_ref, wuq_ref, kvg_ref,
                                                     wk_ref, wv_ref, wcs_ref, tq_ref, tk_ref, False)
    ckv_ref[...] = ckv
    for r in range(x.shape[0] // t):
        kpe_ref[r] = kpe[r * t:(r + 1) * t, :].T[0:QK_ROPE, :]
    attn, fm = [], []
    for r in range(x.shape[0] // t):
        rows = slice(r * t, (r + 1) * t)
        attn.append(jnp.concatenate(_attn_body([q[:, rows] for q in q_heads], [k[rows, :]], [v_t[:, rows]], kc),
                                    axis=1))
        fm.append((_dot(ct_ref[...], zc[rows, :]) - _dot(st_ref[...], zs[rows, :])).astype(BF16))
    x1, h2, aff = _mixout_body(x_ref[...], [jnp.concatenate(attn, axis=0)], jnp.concatenate(fm, axis=0),
                               wo_ref, m_ref, g2_ref, wr_ref)
    x1_ref[...] = x1.astype(BF16)
    h2_ref[...] = h2
    for r in range(x.shape[0] // t):
        aff_ref[r] = aff[:, r * t:(r + 1) * t]


def _ctx_front(x, mods6, g1, win, qg, wuq, kvg, wk, wv, wcs, tq, tk, ct, st, wo, g2, wr_t, *, n_req, t, mod_row):
    full = lambda a: pl.BlockSpec(a.shape, lambda b: (0,) * a.ndim)
    rps = CTX_REQS
    row = lambda w: pl.BlockSpec((rps * t, w), lambda b: (b, 0))
    return pl.pallas_call(
        functools.partial(_ctx_front_kernel, kc=KEY_CHUNK, t=t),
        grid=(n_req // rps,),
        in_specs=[row(D_MODEL), pl.BlockSpec((1, 6, D_MODEL), lambda b: (mod_row, 0, 0)),
                  full(g1), full(win), full(qg), full(wuq), full(kvg), full(wk), full(wv), full(wcs),
                  full(tq), full(tk), full(ct), full(st), full(wo), full(g2), full(wr_t)],
        out_specs=[row(D_MODEL), row(D_MODEL), pl.BlockSpec((rps, N_EXPERTS, t), lambda b: (b, 0, 0)),
                   row(KV_LORA), pl.BlockSpec((rps, QK_ROPE, t), lambda b: (b, 0, 0))],
        out_shape=[jax.ShapeDtypeStruct((n_req * t, D_MODEL), BF16),
                   jax.ShapeDtypeStruct((n_req * t, D_MODEL), BF16),
                   jax.ShapeDtypeStruct((n_req, N_EXPERTS, t), F32),
                   jax.ShapeDtypeStruct((n_req * t, KV_LORA), F32),
                   jax.ShapeDtypeStruct((n_req, QK_ROPE, t), F32)],
        compiler_params=_cparams(("parallel",)),
        name="ctx_front",
    )(x, mods6, g1, win, qg, wuq, kvg, wk, wv, wcs, tq, tk, ct, st, wo, g2, wr_t)


def _mixout(x, attn, zc, zs, cb, sb, off, wo, mods6, g2, wr_t, *, n_req, t, mod_row):
    tr = MIX_TILE
    nr = t // tr
    full = lambda a: pl.BlockSpec(a.shape, lambda b, i: (0,) * a.ndim)
    return pl.pallas_call(
        functools.partial(_mixout_kernel, t=t),
        grid=(n_req, nr),
        in_specs=[pl.BlockSpec((tr, D_MODEL), lambda b, i: (b * nr + i, 0)),
                  pl.BlockSpec((tr, V_W), lambda b, i: (b * nr + i, 0)),
                  pl.BlockSpec((t, FNET_W), lambda b, i: (b, 0)),
                  pl.BlockSpec((t, FNET_W), lambda b, i: (b, 0)),
                  full(cb), full(sb), full(off),
                  full(wo),
                  pl.BlockSpec((1, 6, D_MODEL), lambda b, i: (mod_row(b), 0, 0)),
                  full(g2), full(wr_t)],
        out_specs=[pl.BlockSpec((tr, D_MODEL), lambda b, i: (b * nr + i, 0)),
                   pl.BlockSpec((tr, D_MODEL), lambda b, i: (b * nr + i, 0)),
                   pl.BlockSpec((None, N_EXPERTS, tr), lambda b, i: (b, 0, i))],
        out_shape=[jax.ShapeDtypeStruct((n_req * t, D_MODEL), BF16),
                   jax.ShapeDtypeStruct((n_req * t, D_MODEL), BF16),
                   jax.ShapeDtypeStruct((n_req, N_EXPERTS, t), F32)],
        scratch_shapes=[pltpu.VMEM((t // 2, 2 * FNET_W), BF16), pltpu.VMEM((t // 2, 2 * FNET_W), BF16),
                        pltpu.VMEM((FNET_W // LANES, tr, LANES), F32)],
        compiler_params=_cparams(("parallel", "arbitrary")),
        name="mixout",
    )(x, attn, zc, zs, cb, sb, off, wo, mods6, g2, wr_t)


def _prefix_count(flags, tri):
    n = flags.shape[1]
    carry = None
    outs = []
    ends = []
    for j in range(n // TOK_TILE):
        c = _dot(flags[:, j * TOK_TILE:(j + 1) * TOK_TILE].astype(BF16), tri)
        if carry is not None:
            c = c + carry
        outs.append(c)
        carry = c[:, TOK_TILE - 1:TOK_TILE]
        ends.append(carry)
    return (outs[0] if len(outs) == 1 else jnp.concatenate(outs, axis=1)), ends


def _route_kernel(aff_ref, pos_ref, off_ref, *, cap):
    a = aff_ref[...]
    rows = a.shape[0]
    capf = jnp.float32(cap)
    thr = jnp.zeros((rows, 1), jnp.int32)
    for bit in range(30, -1, -1):
        cand = thr | jnp.int32(1 << bit)
        cand_f = lax.bitcast_convert_type(cand, F32)
        cnt = jnp.sum(jnp.where(a >= cand_f, 1.0, 0.0), axis=1, keepdims=True)
        thr = jnp.where(cnt >= capf, cand, thr)
    thr_f = lax.bitcast_convert_type(thr, F32)
    above_f = lax.bitcast_convert_type(thr + 1, F32)
    gt = jnp.where(a >= above_f, 1.0, 0.0)
    tie = jnp.where(a >= thr_f, 1.0, 0.0) - gt
    need = capf - jnp.sum(gt, axis=1, keepdims=True)
    r_i = lax.broadcasted_iota(jnp.int32, (TOK_TILE, TOK_TILE), 0)
    c_i = lax.broadcasted_iota(jnp.int32, (TOK_TILE, TOK_TILE), 1)
    tri = jnp.where(r_i <= c_i, 1.0, 0.0).astype(BF16)
    tie_before = _prefix_count(tie, tri)[0] - tie
    sel = gt + tie * jnp.where(tie_before < need, 1.0, 0.0)
    count, ends = _prefix_count(sel, tri)
    pos_ref[...] = jnp.where(sel > 0.5, count - 1.0, -1.0)
    lane = lax.broadcasted_iota(jnp.int32, (rows, LANES), 1)
    offs = jnp.zeros((rows, LANES), F32)
    for j, end in enumerate(ends):
        offs = offs + jnp.where(lane == j + 1, end, 0.0)
    off_ref[...] = offs.astype(jnp.int32)


def _route(aff_t, cap):
    return pl.pallas_call(
        functools.partial(_route_kernel, cap=cap),
        out_shape=[jax.ShapeDtypeStruct(aff_t.shape, F32),
                   jax.ShapeDtypeStruct((aff_t.shape[0], LANES), jnp.int32)],
        compiler_params=pltpu.CompilerParams(vmem_limit_bytes=VMEM_LIMIT),
        name="route",
    )(aff_t)


def _gather_kernel(pos_ref, aff_ref, h_ref, xs_ref, g_ref, *, cap):
    rps, ne, n = pos_ref.shape
    slot = lax.broadcasted_iota(jnp.int32, (ne, cap, n), 1).astype(F32)
    for r in range(rps):
        pos = pos_ref[r]
        aff = aff_ref[r]
        hit = pos[:, None, :] == slot
        onehot = jnp.where(hit, 1.0, 0.0).reshape(ne * cap, n).astype(BF16)
        xs = _dot(onehot, h_ref[r * n:(r + 1) * n, :])
        xs_ref[:, r * cap:(r + 1) * cap, :] = xs.astype(BF16).reshape(ne, cap, D_MODEL)
        g_ref[:, r * cap:(r + 1) * cap, :] = jnp.sum(jnp.where(hit, aff[:, None, :], 0.0), axis=2, keepdims=True)


def _gather(pos_t, aff_t, h2, *, n_req, n, cap, rps):
    pos3 = pos_t.reshape(n_req, N_EXPERTS, n)
    aff3 = aff_t.reshape(n_req, N_EXPERTS, n)
    return pl.pallas_call(
        functools.partial(_gather_kernel, cap=cap),
        grid=(n_req // rps,),
        in_specs=[pl.BlockSpec((rps, N_EXPERTS, n), lambda b: (b, 0, 0)),
                  pl.BlockSpec((rps, N_EXPERTS, n), lambda b: (b, 0, 0)),
                  pl.BlockSpec((rps * n, D_MODEL), lambda b: (b, 0))],
        out_specs=[pl.BlockSpec((N_EXPERTS, rps * cap, D_MODEL), lambda b: (0, b, 0)),
                   pl.BlockSpec((N_EXPERTS, rps * cap, 1), lambda b: (0, b, 0))],
        out_shape=[jax.ShapeDtypeStruct((N_EXPERTS, n_req * cap, D_MODEL), BF16),
                   jax.ShapeDtypeStruct((N_EXPERTS, n_req * cap, 1), F32)],
        compiler_params=_cparams(("parallel",)),
        name="gather",
    )(pos3, aff3, h2)


def _window_plan(off_ref, b, j, cap, w):
    w0 = []
    need = jnp.int32(0)
    for e in range(N_EXPERTS):
        idx = (b * N_EXPERTS + e) * OFF_STRIDE + j
        base = (off_ref[idx] >> 4) << 4
        w0.append(base)
        need = jnp.maximum(need, off_ref[idx + 1] - base)
    return w0, (need + (w - 1)) >> (w.bit_length() - 1)


def _window(w0_e, p, cap, w):
    low = w0_e + p * w
    start = pl.multiple_of(jnp.minimum(low, cap - w), 16)
    return low, start


def _gather_win_kernel(off_ref, pos_ref, aff_ref, h_ref, xs_ref, g_ref, *, cap, w):
    b = pl.program_id(0)
    step = pl.program_id(1)

    @pl.when(step == 0)
    def _init():
        xs_ref[...] = jnp.zeros_like(xs_ref)
        g_ref[...] = jnp.zeros_like(g_ref)

    r = lax.broadcasted_iota(jnp.int32, (w, 1), 0).astype(F32)
    n_sub = pos_ref.shape[1] // TOK_TILE
    for sub in range(n_sub):
        cols = slice(sub * TOK_TILE, (sub + 1) * TOK_TILE)
        pos = pos_ref[:, cols]
        aff = aff_ref[:, cols]
        h = h_ref[cols, :]
        w0, n_pass = _window_plan(off_ref, b, step * n_sub + sub, cap, w)

        def one_pass(p, carry, pos=pos, aff=aff, h=h, w0=w0):
            starts, hots, gates = [], [], []
            for e in range(N_EXPERTS):
                low, start = _window(w0[e], p, cap, w)
                starts.append(start)
                mine = jnp.where(r >= (low - start).astype(F32), 1.0, 0.0)
                hot = jnp.where(pos[e:e + 1, :] - start.astype(F32) == r, mine, 0.0)
                hots.append(hot)
                gates.append(jnp.sum(hot * aff[e:e + 1, :], axis=1, keepdims=True))
            rows = _dot(jnp.concatenate(hots, axis=0).astype(BF16), h).astype(BF16)
            for e in range(N_EXPERTS):
                win = pl.ds(starts[e], w)
                xs_ref[e, win, :] = xs_ref[e, win, :] + rows[e * w:(e + 1) * w, :]
                g_ref[e, win, :] = g_ref[e, win, :] + gates[e]
            return carry

        one_pass(0, 0)
        lax.fori_loop(1, n_pass, one_pass, 0)


def _gather_win(offs, pos_t, aff_t, h2, *, n_req, n, cap):
    tb = WIN_BLOCKS * TOK_TILE
    nb = n // tb
    grid_spec = pltpu.PrefetchScalarGridSpec(
        num_scalar_prefetch=1,
        grid=(n_req, nb),
        in_specs=[pl.BlockSpec((N_EXPERTS, tb), lambda b, j, off: (b, j)),
                  pl.BlockSpec((N_EXPERTS, tb), lambda b, j, off: (b, j)),
                  pl.BlockSpec((tb, D_MODEL), lambda b, j, off: (b * nb + j, 0))],
        out_specs=[pl.BlockSpec((N_EXPERTS, cap, D_MODEL), lambda b, j, off: (0, b, 0)),
                   pl.BlockSpec((N_EXPERTS, cap, 1), lambda b, j, off: (0, b, 0))])
    return pl.pallas_call(
        functools.partial(_gather_win_kernel, cap=cap, w=SLOT_WIN),
        grid_spec=grid_spec,
        out_shape=[jax.ShapeDtypeStruct((N_EXPERTS, n_req * cap, D_MODEL), BF16),
                   jax.ShapeDtypeStruct((N_EXPERTS, n_req * cap, 1), F32)],
        compiler_params=_cparams(("parallel", "arbitrary")),
        name="gather_win",
    )(offs, pos_t, aff_t, h2)


def _ffn_kernel(xc_ref, xl_ref, gc_ref, gl_ref, wg_ref, wu_ref, wd_ref, yc_ref, yl_ref, wgb, wub, wdb):
    wgb[...] = wg_ref[0].astype(BF16)
    wub[...] = wu_ref[0].astype(BF16)
    wdb[...] = wd_ref[0].astype(BF16)
    for x_ref, g_ref, y_ref in ((xc_ref, gc_ref, yc_ref), (xl_ref, gl_ref, yl_ref)):
        for j in range(x_ref.shape[1] // FFN_ROWS):
            rows = slice(j * FFN_ROWS, (j + 1) * FFN_ROWS)
            x = x_ref[0, rows, :]
            gate = _dot(x, wgb[...])
            up = _dot(x, wub[...])
            hid = (gate * jax.nn.sigmoid(gate) * up).astype(BF16)
            ys = _dot(hid, wdb[...]) * g_ref[0, rows, :]
            y_ref[0, rows, :] = ys.astype(BF16)


def _ffn(xc, xl, gc, gl, wg, wu, wd):
    m = xc.shape[1]
    xspec = pl.BlockSpec((1, m, D_MODEL), lambda e: (e, 0, 0))
    gspec = pl.BlockSpec((1, m, 1), lambda e: (e, 0, 0))
    shp = jax.ShapeDtypeStruct((N_EXPERTS, m, D_MODEL), BF16)
    return pl.pallas_call(
        _ffn_kernel,
        grid=(N_EXPERTS,),
        in_specs=[xspec, xspec, gspec, gspec,
                  pl.BlockSpec((1, D_MODEL, D_EXPERT), lambda e: (e, 0, 0)),
                  pl.BlockSpec((1, D_MODEL, D_EXPERT), lambda e: (e, 0, 0)),
                  pl.BlockSpec((1, D_EXPERT, D_MODEL), lambda e: (e, 0, 0))],
        out_specs=[xspec, xspec],
        out_shape=[shp, shp],
        scratch_shapes=[pltpu.VMEM((D_MODEL, D_EXPERT), BF16), pltpu.VMEM((D_MODEL, D_EXPERT), BF16),
                        pltpu.VMEM((D_EXPERT, D_MODEL), BF16)],
        compiler_params=_cparams(("arbitrary",)),
        name="ffn",
    )(xc, xl, gc, gl, wg, wu, wd)


def _combine_kernel(x1_ref, pos_ref, ys_ref, m_ref, fg_ref, o_ref, *, cap):
    rps, n, _ = pos_ref.shape
    w = N_EXPERTS * cap
    e_i = lax.broadcasted_iota(jnp.int32, (N_EXPERTS, w), 0)
    j_i = lax.broadcasted_iota(jnp.int32, (N_EXPERTS, w), 1)
    spread = jnp.where((j_i >> (cap.bit_length() - 1)) == e_i, 1.0, 0.0).astype(BF16)
    lane_slot = (lax.broadcasted_iota(jnp.int32, (1, w), 1) & (cap - 1)).astype(F32)
    gate2 = m_ref[0, 5:6, :]
    for r in range(rps):
        pos = pos_ref[r].astype(BF16)
        onehot = jnp.where(_dot(pos, spread) == lane_slot, 1.0, 0.0).astype(BF16)
        acc = _dot(onehot, ys_ref[:, r * cap:(r + 1) * cap, :].reshape(w, D_MODEL))
        rows = slice(r * n, (r + 1) * n)
        o_ref[rows, :] = _rms(x1_ref[rows, :].astype(F32) + gate2 * acc, fg_ref[...])


def _combine(x1, pos_tok, ys, mods6, fg, *, n_req, n, cap, mod_row, rps):
    return pl.pallas_call(
        functools.partial(_combine_kernel, cap=cap),
        grid=(n_req // rps,),
        in_specs=[pl.BlockSpec((rps * n, D_MODEL), lambda b: (b, 0)),
                  pl.BlockSpec((rps, n, N_EXPERTS), lambda b: (b, 0, 0)),
                  pl.BlockSpec((N_EXPERTS, rps * cap, D_MODEL), lambda b: (0, b, 0)),
                  pl.BlockSpec((1, 6, D_MODEL), lambda b: (mod_row, 0, 0)),
                  pl.BlockSpec((1, D_MODEL), lambda b: (0, 0))],
        out_specs=pl.BlockSpec((rps * n, D_MODEL), lambda b: (b, 0)),
        out_shape=jax.ShapeDtypeStruct((n_req * n, D_MODEL), F32),
        compiler_params=_cparams(("parallel",)),
        name="combine",
    )(x1, pos_tok, ys, mods6, fg)


def _combine_win_kernel(off_ref, x1_ref, pos_ref, ys_ref, m_ref, fg_ref, o_ref, acc_ref, *, cap, w):
    b = pl.program_id(0)
    step = pl.program_id(1)
    width = N_EXPERTS * w
    e_i = lax.broadcasted_iota(jnp.int32, (N_EXPERTS, width), 0)
    j_i = lax.broadcasted_iota(jnp.int32, (N_EXPERTS, width), 1)
    spread = jnp.where((j_i >> (w.bit_length() - 1)) == e_i, 1.0, 0.0).astype(BF16)
    lane_slot = (lax.broadcasted_iota(jnp.int32, (1, width), 1) & (w - 1)).astype(F32)
    lane_e = lax.broadcasted_iota(jnp.int32, (1, N_EXPERTS), 1)
    n_sub = pos_ref.shape[0] // TOK_TILE
    for sub in range(n_sub):
        rows = slice(sub * TOK_TILE, (sub + 1) * TOK_TILE)
        pos = pos_ref[rows, :]
        w0, n_pass = _window_plan(off_ref, b, step * n_sub + sub, cap, w)

        def window_sum(p, pos=pos, w0=w0):
            start_row = jnp.zeros((1, N_EXPERTS), F32)
            first_row = jnp.zeros((1, N_EXPERTS), F32)
            wins = []
            for e in range(N_EXPERTS):
                low, start = _window(w0[e], p, cap, w)
                start_row = jnp.where(lane_e == e, start.astype(F32), start_row)
                first_row = jnp.where(lane_e == e, (low - start).astype(F32), first_row)
                wins.append(ys_ref[e, pl.ds(start, w), :])
            rel = pos - start_row
            rel = jnp.where(rel >= first_row, rel, -1.0).astype(BF16)
            onehot = jnp.where(_dot(rel, spread) == lane_slot, 1.0, 0.0).astype(BF16)
            return _dot(onehot, jnp.concatenate(wins, axis=0))

        def finish(acc, rows=rows):
            o_ref[rows, :] = _rms(x1_ref[rows, :].astype(F32) + m_ref[0, 5:6, :] * acc, fg_ref[...])

        acc0 = window_sum(0)
        acc_ref[sub] = acc0
        finish(acc0)

        @pl.when(n_pass > 1)
        def _more_passes(sub=sub, n_pass=n_pass, window_sum=window_sum, finish=finish):
            def one_pass(p, carry):
                acc_ref[sub] += window_sum(p)
                return carry

            lax.fori_loop(1, n_pass, one_pass, 0)
            finish(acc_ref[sub])


def _combine_win(offs, x1, pos_tok, ys, mods6, fg, *, n_req, n, cap, mod_row):
    tr = WIN_BLOCKS * TOK_TILE
    nr = n // tr
    grid_spec = pltpu.PrefetchScalarGridSpec(
        num_scalar_prefetch=1,
        grid=(n_req, nr),
        in_specs=[pl.BlockSpec((tr, D_MODEL), lambda b, i, off: (b * nr + i, 0)),
                  pl.BlockSpec((None, tr, N_EXPERTS), lambda b, i, off: (b, i, 0)),
                  pl.BlockSpec((N_EXPERTS, cap, D_MODEL), lambda b, i, off: (0, b, 0)),
                  pl.BlockSpec((1, 6, D_MODEL), lambda b, i, off: (mod_row(b), 0, 0)),
                  pl.BlockSpec((1, D_MODEL), lambda b, i, off: (0, 0))],
        out_specs=pl.BlockSpec((tr, D_MODEL), lambda b, i, off: (b * nr + i, 0)),
        scratch_shapes=[pltpu.VMEM((WIN_BLOCKS, TOK_TILE, D_MODEL), F32)])
    return pl.pallas_call(
        functools.partial(_combine_win_kernel, cap=cap, w=SLOT_WIN),
        grid_spec=grid_spec,
        out_shape=jax.ShapeDtypeStruct((n_req * n, D_MODEL), F32),
        compiler_params=_cparams(("parallel", "parallel")),
        name="combine_win",
    )(offs, x1, pos_tok, ys, mods6, fg)


def _rot_half(w):
    half = QK_ROPE // 2
    return jnp.concatenate([-w[..., half:], w[..., :half]], axis=-1)


def _rope_tables(t):
    n_rows = t // GRID_W
    rows = np.repeat(np.arange(n_rows, dtype=np.float64), GRID_W)
    cols = np.tile(np.arange(GRID_W, dtype=np.float64), n_rows)
    n_freq = QK_ROPE // 4
    inv_freq = ROPE_BASE ** (-np.arange(n_freq, dtype=np.float64) / n_freq)
    ang = np.concatenate([rows[:, None] * inv_freq, cols[:, None] * inv_freq], axis=-1)
    cos = np.concatenate([np.cos(ang), np.cos(ang)], axis=-1)
    sin = np.concatenate([np.sin(ang), np.sin(ang)], axis=-1)
    return cos, sin


def _qk_tables(cos, sin):
    t = cos.shape[0]
    scale = (QK_NOPE + QK_ROPE) ** -0.5 * np.log2(np.e)
    pad = np.zeros((t, HEAD_PAD - QK_NOPE - QK_ROPE))
    cosq = np.concatenate([np.full((t, QK_NOPE), scale), cos * scale, pad], axis=1)
    sinq = np.concatenate([np.zeros((t, QK_NOPE)), sin * scale, pad], axis=1)
    tq_t = np.concatenate([cosq, sinq], axis=1).T
    tk = np.concatenate([cos, sin, np.zeros((t, LANES - 2 * QK_ROPE))], axis=1)
    return jnp.asarray(tq_t, F32), jnp.asarray(tk, F32)


def _dft_angles(rows, t):
    k = np.arange(t, dtype=np.int64)
    return ((rows[:, None] * k[None, :]) % t).astype(np.float64) * (2.0 * np.pi / t)


def _dft_tables(t):
    ang = _dft_angles(np.arange(t, dtype=np.int64), t)
    scale = (t * FNET_CH) ** -0.5
    return jnp.asarray(np.cos(ang) * scale, F32).astype(BF16), jnp.asarray(np.sin(ang) * scale, F32).astype(BF16)


def _dft_half_tables(t):
    r = np.arange(MIX_TILE, dtype=np.int64)
    ang = _dft_angles(np.concatenate([r[0::2], r[1::2]]), t)[:, :t // 2]
    scale = (t * FNET_CH) ** -0.5
    ang_off = _dft_angles(np.arange(t // MIX_TILE, dtype=np.int64) * MIX_TILE, t)[:, :t // 2]
    off = np.concatenate([np.cos(ang_off), np.sin(ang_off)], axis=1)
    return jnp.asarray(np.cos(ang) * scale, F32), jnp.asarray(np.sin(ang) * scale, F32), jnp.asarray(off, F32)


def kernel(x_prompt, x_sample, cache_ckv, cache_kpe, c, c_ctx, w_mod, b_mod, norm1_g, w_in, q_norm_g, w_uq,
           kv_norm_g, w_ukv, w_fmix, w_out, norm2_g, w_router, w_e_gate, w_e_up, w_e_down, final_g):
    assert w_mod.shape[0] == 1, "single-layer problem"
    n_ctx, t_ctx, _ = x_prompt.shape
    n_lat, t_lat, _ = x_sample.shape
    past = cache_ckv.shape[2]
    ctx_row = n_lat

    win = _winprep(jnp.swapaxes(w_in, 1, 2)[0])
    wq3 = w_uq[0].reshape(Q_LORA, N_HEADS, QK_NOPE + QK_ROPE)
    qpad = jnp.zeros((Q_LORA, N_HEADS, HEAD_PAD - QK_NOPE - QK_ROPE), F32)
    wuq_main = jnp.concatenate([wq3, qpad], axis=2).reshape(Q_LORA, QK_W)
    wuq_rot = _rot_half(wq3[..., QK_NOPE:]).reshape(Q_LORA, N_HEADS * QK_ROPE)
    wuq_lat = jnp.concatenate([wuq_main, wuq_rot], axis=1).T.astype(BF16)
    wuq_ctx = wuq_main.T.astype(BF16)
    wkv3 = w_ukv[0].reshape(KV_LORA, N_HEADS, QK_NOPE + V_HEAD)
    wk_top = jnp.concatenate([wkv3[..., :QK_NOPE], jnp.zeros((KV_LORA, N_HEADS, HEAD_PAD - QK_NOPE), F32)],
                             axis=2).reshape(KV_LORA, QK_W)
    place = jnp.concatenate([jnp.zeros((QK_ROPE, QK_NOPE), F32), jnp.eye(QK_ROPE, dtype=F32),
                             jnp.zeros((QK_ROPE, HEAD_PAD - QK_NOPE - QK_ROPE), F32)], axis=1)
    place = jnp.tile(place, (1, N_HEADS))
    wk = jnp.concatenate([wk_top, place, place, jnp.zeros((LANES - 2 * QK_ROPE, QK_W), F32)], axis=0).astype(BF16)
    wv = wkv3[..., QK_NOPE:].reshape(KV_LORA, V_W).T.astype(BF16)
    wo = w_out[0].astype(BF16)
    wr_t = w_router[0].T.astype(BF16)

    cos, sin = _rope_tables(t_lat)
    tq_lat, tk_lat = _qk_tables(cos, sin)
    assert n_ctx % CTX_REQS == 0
    tq_ctx, tk_ctx = _qk_tables(np.ones((CTX_REQS * t_ctx, QK_ROPE)), np.zeros((CTX_REQS * t_ctx, QK_ROPE)))
    ch_ang = _dft_angles(np.arange(FNET_CH, dtype=np.int64), FNET_CH)
    dft_ctx = _dft_tables(t_ctx)
    dft_lat = _dft_half_tables(t_lat)

    assert n_lat + 1 <= SUBLANES
    c8 = jnp.concatenate([c, c_ctx[None, :], jnp.zeros((SUBLANES - n_lat - 1, D_MODEL), F32)], axis=0)
    mods6 = _mods(c8, w_mod[0], b_mod[0][None, :]).reshape(SUBLANES, 6, D_MODEL)
    wcs = _fold(jnp.asarray(np.cos(ch_ang), F32), jnp.asarray(np.sin(ch_ang), F32), w_fmix[0])

    g1 = norm1_g[0][None, :]
    qg = q_norm_g[0][None, :]
    kvg = kv_norm_g[0][None, :]
    g2 = norm2_g[0][None, :]
    fg = final_g[None, :]

    xp = x_prompt.reshape(n_ctx * t_ctx, D_MODEL)
    xs = x_sample.reshape(n_lat * t_lat, D_MODEL)
    tiles_lat = t_lat // PRE_TILE

    x1c, h2c, affc, ckv_c, kpe_c = _ctx_front(
        xp, mods6, g1, win, qg, wuq_ctx, kvg, wk, wv, wcs, tq_ctx, tk_ctx,
        *dft_ctx, wo, g2, wr_t, n_req=n_ctx, t=t_ctx, mod_row=ctx_row)
    ql, kl, vl, zcl, zsl = _premix(
        xs, mods6, g1, win, qg, wuq_lat, kvg, wk, wv, wcs, tq_lat, tk_lat,
        mod_row=lambda i: i // tiles_lat, tab_row=lambda i: i % tiles_lat, rope=True)
    xk_cache = jnp.concatenate([cache_ckv[:, 0], cache_kpe[:, 0],
                                jnp.zeros((n_lat, past, 2 * LANES - KV_LORA - QK_ROPE), F32)],
                               axis=-1).reshape(n_lat * past, 2 * LANES).astype(BF16)
    kpast, vpast = _cachekv(xk_cache, wk, wv)

    attn_l = _attention(ql, [kpast, kl], [vpast, vl], n_req=n_lat, t_q=t_lat, kv_lens=[past, t_lat], tq=TOK_TILE,
                        pairs_per_step=2)

    x1l, h2l, affl = _mixout(xs, attn_l, zcl, zsl, *dft_lat, wo, mods6, g2, wr_t,
                             n_req=n_lat, t=t_lat, mod_row=lambda b: b)

    cap_c = CAP_FACTOR * t_ctx // N_EXPERTS
    cap_l = CAP_FACTOR * t_lat // N_EXPERTS
    affc2 = affc.reshape(n_ctx * N_EXPERTS, t_ctx)
    affl2 = affl.reshape(n_lat * N_EXPERTS, t_lat)
    posc, _ = _route(affc2, cap_c)
    posl, offl = _route(affl2, cap_l)
    assert t_lat // TOK_TILE + 1 <= OFF_STRIDE and cap_l % SLOT_WIN == 0
    offl = offl[:, :OFF_STRIDE].reshape(-1)
    xsc, gc = _gather(posc, affc2, h2c, n_req=n_ctx, n=t_ctx, cap=cap_c, rps=MOE_REQS)
    xsl, gl = _gather_win(offl, posl, affl2, h2l, n_req=n_lat, n=t_lat, cap=cap_l)
    ysc, ysl = _ffn(xsc, xsl, gc, gl, w_e_gate[0], w_e_up[0], w_e_down[0])

    posc_tok = posc.reshape(n_ctx, N_EXPERTS, t_ctx).transpose(0, 2, 1)
    posl_tok = posl.reshape(n_lat, N_EXPERTS, t_lat).transpose(0, 2, 1)
    y_prompt = _combine(x1c, posc_tok, ysc, mods6, fg, n_req=n_ctx, n=t_ctx, cap=cap_c, mod_row=ctx_row,
                        rps=MOE_REQS)
    y_sample = _combine_win(offl, x1l, posl_tok, ysl, mods6, fg, n_req=n_lat, n=t_lat, cap=cap_l,
                            mod_row=lambda b: b)

    return (y_prompt.reshape(n_ctx, t_ctx, D_MODEL), y_sample.reshape(n_lat, t_lat, D_MODEL),
            ckv_c.reshape(n_ctx, 1, t_ctx, KV_LORA), kpe_c.transpose(0, 2, 1).reshape(n_ctx, 1, t_ctx, QK_ROPE))
```
